```python
import jax, jax.numpy as jnp
from jax import lax
import numpy as np

D_MODEL = 1024
BATCH = 16
SEQ = 2048
DEPTH = 2

N_MIXERS = 2
N_CONV_LAYERS = (DEPTH + 1) // 2
N_MLA_LAYERS = DEPTH // 2
MIX_WIDTH = 2 * D_MODEL
MEM_LEN = 256
MEM_HEADS = 4
MEM_HEAD_DIM = 128
MEM_WIDTH = MEM_HEADS * MEM_HEAD_DIM
MAIN_WIDTH = MIX_WIDTH - MEM_WIDTH
CONV_WIDTH = MAIN_WIDTH
CONV_KERNEL = 31
MLA_HEADS = 12
MLA_NOPE = 128
MLA_ROPE = 64
MLA_V = 128
MLA_QK = MLA_NOPE + MLA_ROPE
Q_RANK = 512
KV_RANK = 256
ROPE_THETA = 10000.0
Q_BLOCK = 128
RMS_EPS = 1e-6
LN_EPS = 1e-5
CONV_IN_COLS = 2 * CONV_WIDTH + MEM_WIDTH + MIX_WIDTH
MLA_IN_COLS = Q_RANK + KV_RANK + MLA_ROPE + MEM_WIDTH + MIX_WIDTH

kernel_name = "hybrid_conformer_conv_mla_memxattn"


def rmsnorm(x, g):
    xf = x.astype(jnp.float32)
    y = xf * lax.rsqrt(jnp.mean(xf * xf, axis=-1, keepdims=True) + RMS_EPS)
    return (y * g.astype(jnp.float32)).astype(x.dtype)


def layernorm(x, g, b):
    xf = x.astype(jnp.float32)
    mu = jnp.mean(xf, axis=-1, keepdims=True)
    var = jnp.mean(jnp.square(xf - mu), axis=-1, keepdims=True)
    y = (xf - mu) * lax.rsqrt(var + LN_EPS)
    return (y * g.astype(jnp.float32) + b.astype(jnp.float32)).astype(x.dtype)


def rope_tables(positions):
    inv_freq = 1.0 / (ROPE_THETA ** (jnp.arange(0, MLA_ROPE, 2, dtype=jnp.float32) / MLA_ROPE))
    ang = positions.astype(jnp.float32)[..., None] * inv_freq
    return jnp.cos(ang), jnp.sin(ang)


def apply_rope(x, cos, sin):
    xf = x.astype(jnp.float32)
    x1, x2 = jnp.split(xf, 2, axis=-1)
    out = jnp.concatenate([x1 * cos - x2 * sin, x2 * cos + x1 * sin], axis=-1)
    return out.astype(x.dtype)


def mem_cross_attention(q, mem, mem_g, w_mem_kv):
    b, s = q.shape[0], q.shape[1]
    kv = rmsnorm(mem, mem_g) @ w_mem_kv
    k, v = jnp.split(kv, 2, axis=-1)
    k = k.reshape(b, -1, MEM_HEADS, MEM_HEAD_DIM)
    v = v.reshape(b, -1, MEM_HEADS, MEM_HEAD_DIM)
    qh = q.reshape(b, s, MEM_HEADS, MEM_HEAD_DIM)
    sc = jnp.einsum('bshd,bmhd->bhsm', qh, k).astype(jnp.float32) * (MEM_HEAD_DIM ** -0.5)
    p = jax.nn.softmax(sc, axis=-1).astype(v.dtype)
    o = jnp.einsum('bhsm,bmhd->bshd', p, v)
    return o.reshape(b, s, MEM_WIDTH)


def conv_branch(u, dw, dw_b, ln_g, ln_b):
    a, g = jnp.split(u, 2, axis=-1)
    h = a * jax.nn.sigmoid(g)
    h = lax.conv_general_dilated(
        h, dw[:, None, :], window_strides=(1,), padding=[(CONV_KERNEL - 1, 0)],
        dimension_numbers=('NWC', 'WIO', 'NWC'), feature_group_count=CONV_WIDTH) + dw_b
    h = layernorm(h, ln_g, ln_b)
    return jax.nn.silu(h)


def mla_branch(cq, ckv, kr, cos, sin, q_g, w_uq, kv_g, w_ukv):
    b, s = cq.shape[0], cq.shape[1]
    q = (rmsnorm(cq, q_g) @ w_uq).reshape(b, s, MLA_HEADS, MLA_QK)
    q_nope = q[..., :MLA_NOPE]
    q_rope = apply_rope(q[..., MLA_NOPE:], cos[:, :, None, :], sin[:, :, None, :])
    kv = (rmsnorm(ckv, kv_g) @ w_ukv).reshape(b, s, MLA_HEADS, MLA_NOPE + MLA_V)
    k_nope, v = kv[..., :MLA_NOPE], kv[..., MLA_NOPE:]
    k_rope = apply_rope(kr, cos, sin)
    nb = s // Q_BLOCK
    qn_b = q_nope.reshape(b, nb, Q_BLOCK, MLA_HEADS, MLA_NOPE).transpose(1, 0, 2, 3, 4)
    qr_b = q_rope.reshape(b, nb, Q_BLOCK, MLA_HEADS, MLA_ROPE).transpose(1, 0, 2, 3, 4)
    key_idx = jnp.arange(s)
    scale = MLA_QK ** -0.5

    def block(args):
        qn, qr, i = args
        sc = (jnp.einsum('bqhd,bkhd->bhqk', qn, k_nope)
              + jnp.einsum('bqhd,bkd->bhqk', qr, k_rope)).astype(jnp.float32) * scale
        q_idx = i * Q_BLOCK + jnp.arange(Q_BLOCK)
        mask = key_idx[None, :] <= q_idx[:, None]
        sc = jnp.where(mask, sc, -jnp.inf)
        p = jax.nn.softmax(sc, axis=-1).astype(v.dtype)
        return jnp.einsum('bhqk,bkhd->bqhd', p, v)

    o = lax.map(block, (qn_b, qr_b, jnp.arange(nb)))
    return o.transpose(1, 0, 2, 3, 4).reshape(b, s, MLA_HEADS * MLA_V)


def _fwd_setup_inputs(seed: int = 0) -> dict:
    key = jax.random.key(seed)
    ks = jax.random.split(key, 24)
    nrm = jax.random.normal
    f32 = jnp.float32
    x = nrm(ks[0], (BATCH, SEQ, D_MODEL), f32)
    mem = nrm(ks[1], (BATCH, MEM_LEN, D_MODEL), f32)
    offs = jax.random.randint(ks[2], (BATCH, 1), 0, 1024, dtype=jnp.int32)
    positions = (offs + jnp.arange(SEQ, dtype=jnp.int32)[None, :]).astype(jnp.int32)
    gain = lambda k, shape: 1.0 + 0.02 * nrm(k, shape, f32)
    return {
        "x": x,
        "mem": mem,
        "positions": positions,
        "norm_g": gain(ks[3], (DEPTH, D_MODEL)),
        "mem_norm_g": gain(ks[4], (DEPTH, D_MODEL)),
        "w_mem_kv": nrm(ks[5], (DEPTH, D_MODEL, 2 * MEM_WIDTH), f32) * D_MODEL ** -0.5,
        "w_out": nrm(ks[6], (DEPTH, MIX_WIDTH, D_MODEL), f32) * MIX_WIDTH ** -0.5,
        "conv_w_in": nrm(ks[7], (N_CONV_LAYERS, D_MODEL, CONV_IN_COLS), f32) * D_MODEL ** -0.5,
        "conv_dw": nrm(ks[8], (N_CONV_LAYERS, CONV_KERNEL, CONV_WIDTH), f32) * CONV_KERNEL ** -0.5,
        "conv_dw_b": 0.02 * nrm(ks[9], (N_CONV_LAYERS, CONV_WIDTH), f32),
        "conv_ln_g": gain(ks[10], (N_CONV_LAYERS, CONV_WIDTH)),
        "conv_ln_b": 0.02 * nrm(ks[11], (N_CONV_LAYERS, CONV_WIDTH), f32),
        "mla_w_in": nrm(ks[12], (N_MLA_LAYERS, D_MODEL, MLA_IN_COLS), f32) * D_MODEL ** -0.5,
        "mla_q_norm_g": gain(ks[13], (N_MLA_LAYERS, Q_RANK)),
        "mla_w_uq": nrm(ks[14], (N_MLA_LAYERS, Q_RANK, MLA_HEADS * MLA_QK), f32) * Q_RANK ** -0.5,
        "mla_kv_norm_g": gain(ks[15], (N_MLA_LAYERS, KV_RANK)),
        "mla_w_ukv": nrm(ks[16], (N_MLA_LAYERS, KV_RANK, MLA_HEADS * (MLA_NOPE + MLA_V)), f32) * KV_RANK ** -0.5,
        "final_norm_g": gain(ks[17], (D_MODEL,)),
    }


def _fwd_reference(x, mem, positions, norm_g, mem_norm_g, w_mem_kv, w_out, conv_w_in, conv_dw,
              conv_dw_b, conv_ln_g, conv_ln_b, mla_w_in, mla_q_norm_g, mla_w_uq,
              mla_kv_norm_g, mla_w_ukv, final_norm_g):
    cos, sin = rope_tables(positions)
    h = x
    for i in range(DEPTH):
        j = i // N_MIXERS
        u = rmsnorm(h, norm_g[i])
        if i % N_MIXERS == 0:
            proj = u @ conv_w_in[j]
            u_conv, q_mem, z = jnp.split(proj, [2 * CONV_WIDTH, 2 * CONV_WIDTH + MEM_WIDTH], axis=-1)
            y_main = conv_branch(u_conv, conv_dw[j], conv_dw_b[j], conv_ln_g[j], conv_ln_b[j])
        else:
            proj = u @ mla_w_in[j]
            c1 = Q_RANK
            c2 = c1 + KV_RANK
            c3 = c2 + MLA_ROPE
            c4 = c3 + MEM_WIDTH
            cq, ckv, kr, q_mem, z = jnp.split(proj, [c1, c2, c3, c4], axis=-1)
            y_main = mla_branch(cq, ckv, kr, cos, sin, mla_q_norm_g[j], mla_w_uq[j],
                                mla_kv_norm_g[j], mla_w_ukv[j])
        y_mem = mem_cross_attention(q_mem, mem, mem_norm_g[i], w_mem_kv[i])
        y = jnp.concatenate([y_main, y_mem], axis=-1) * jax.nn.silu(z)
        h = h + y @ w_out[i]
    return rmsnorm(h, final_norm_g)


import jax as _jax
import jax.numpy as _jnp

TWIN_FORMAT = 'train_step'
FWD_PARAMS = ['x', 'mem', 'positions', 'norm_g', 'mem_norm_g', 'w_mem_kv', 'w_out', 'conv_w_in', 'conv_dw', 'conv_dw_b', 'conv_ln_g', 'conv_ln_b', 'mla_w_in', 'mla_q_norm_g', 'mla_w_uq', 'mla_kv_norm_g', 'mla_w_ukv', 'final_norm_g']
TWIN_WEIGHTS = ['norm_g', 'mem_norm_g', 'w_mem_kv', 'w_out', 'conv_w_in', 'conv_dw', 'conv_dw_b', 'conv_ln_g', 'conv_ln_b', 'mla_w_in', 'mla_q_norm_g', 'mla_w_uq', 'mla_kv_norm_g', 'mla_w_ukv', 'final_norm_g']
TWIN_DIFF_INPUT = 'x'
TWIN_INPUTS = ['x', 'mem', 'positions', 'norm_g', 'mem_norm_g', 'w_mem_kv', 'w_out', 'conv_w_in', 'conv_dw', 'conv_dw_b', 'conv_ln_g', 'conv_ln_b', 'mla_w_in', 'mla_q_norm_g', 'mla_w_uq', 'mla_kv_norm_g', 'mla_w_ukv', 'final_norm_g', 'loss_target', 'm_norm_g', 'm_mem_norm_g', 'm_w_mem_kv', 'm_w_out', 'm_conv_w_in', 'm_conv_dw', 'm_conv_dw_b', 'm_conv_ln_g', 'm_conv_ln_b', 'm_mla_w_in', 'm_mla_q_norm_g', 'm_mla_w_uq', 'm_mla_kv_norm_g', 'm_mla_w_ukv', 'm_final_norm_g', 'v_norm_g', 'v_mem_norm_g', 'v_w_mem_kv', 'v_w_out', 'v_conv_w_in', 'v_conv_dw', 'v_conv_dw_b', 'v_conv_ln_g', 'v_conv_ln_b', 'v_mla_w_in', 'v_mla_q_norm_g', 'v_mla_w_uq', 'v_mla_kv_norm_g', 'v_mla_w_ukv', 'v_final_norm_g']
TWIN_OUTPUTS = ['loss', 'grad_x', 'grad_norm_g', 'grad_mem_norm_g', 'grad_w_mem_kv', 'grad_w_out', 'grad_conv_w_in', 'grad_conv_dw', 'grad_conv_dw_b', 'grad_conv_ln_g', 'grad_conv_ln_b', 'grad_mla_w_in', 'grad_mla_q_norm_g', 'grad_mla_w_uq', 'grad_mla_kv_norm_g', 'grad_mla_w_ukv', 'grad_final_norm_g', 'delta_norm_g', 'delta_mem_norm_g', 'delta_w_mem_kv', 'delta_w_out', 'delta_conv_w_in', 'delta_conv_dw', 'delta_conv_dw_b', 'delta_conv_ln_g', 'delta_conv_ln_b', 'delta_mla_w_in', 'delta_mla_q_norm_g', 'delta_mla_w_uq', 'delta_mla_kv_norm_g', 'delta_mla_w_ukv', 'delta_final_norm_g', 'new_m_norm_g', 'new_m_mem_norm_g', 'new_m_w_mem_kv', 'new_m_w_out', 'new_m_conv_w_in', 'new_m_conv_dw', 'new_m_conv_dw_b', 'new_m_conv_ln_g', 'new_m_conv_ln_b', 'new_m_mla_w_in', 'new_m_mla_q_norm_g', 'new_m_mla_w_uq', 'new_m_mla_kv_norm_g', 'new_m_mla_w_ukv', 'new_m_final_norm_g', 'new_v_norm_g', 'new_v_mem_norm_g', 'new_v_w_mem_kv', 'new_v_w_out', 'new_v_conv_w_in', 'new_v_conv_dw', 'new_v_conv_dw_b', 'new_v_conv_ln_g', 'new_v_conv_ln_b', 'new_v_mla_w_in', 'new_v_mla_q_norm_g', 'new_v_mla_w_uq', 'new_v_mla_kv_norm_g', 'new_v_mla_w_ukv', 'new_v_final_norm_g']
TWIN_LEAF_KINDS = {'loss': 'loss', 'grad_x': 'grad_x', 'grad_norm_g': 'grad_w', 'grad_mem_norm_g': 'grad_w', 'grad_w_mem_kv': 'grad_w', 'grad_w_out': 'grad_w', 'grad_conv_w_in': 'grad_w', 'grad_conv_dw': 'grad_w', 'grad_conv_dw_b': 'grad_w', 'grad_conv_ln_g': 'grad_w', 'grad_conv_ln_b': 'grad_w', 'grad_mla_w_in': 'grad_w', 'grad_mla_q_norm_g': 'grad_w', 'grad_mla_w_uq': 'grad_w', 'grad_mla_kv_norm_g': 'grad_w', 'grad_mla_w_ukv': 'grad_w', 'grad_final_norm_g': 'grad_w', 'delta_norm_g': 'delta_w', 'delta_mem_norm_g': 'delta_w', 'delta_w_mem_kv': 'delta_w', 'delta_w_out': 'delta_w', 'delta_conv_w_in': 'delta_w', 'delta_conv_dw': 'delta_w', 'delta_conv_dw_b': 'delta_w', 'delta_conv_ln_g': 'delta_w', 'delta_conv_ln_b': 'delta_w', 'delta_mla_w_in': 'delta_w', 'delta_mla_q_norm_g': 'delta_w', 'delta_mla_w_uq': 'delta_w', 'delta_mla_kv_norm_g': 'delta_w', 'delta_mla_w_ukv': 'delta_w', 'delta_final_norm_g': 'delta_w', 'new_m_norm_g': 'new_m', 'new_m_mem_norm_g': 'new_m', 'new_m_w_mem_kv': 'new_m', 'new_m_w_out': 'new_m', 'new_m_conv_w_in': 'new_m', 'new_m_conv_dw': 'new_m', 'new_m_conv_dw_b': 'new_m', 'new_m_conv_ln_g': 'new_m', 'new_m_conv_ln_b': 'new_m', 'new_m_mla_w_in': 'new_m', 'new_m_mla_q_norm_g': 'new_m', 'new_m_mla_w_uq': 'new_m', 'new_m_mla_kv_norm_g': 'new_m', 'new_m_mla_w_ukv': 'new_m', 'new_m_final_norm_g': 'new_m', 'new_v_norm_g': 'new_v', 'new_v_mem_norm_g': 'new_v', 'new_v_w_mem_kv': 'new_v', 'new_v_w_out': 'new_v', 'new_v_conv_w_in': 'new_v', 'new_v_conv_dw': 'new_v', 'new_v_conv_dw_b': 'new_v', 'new_v_conv_ln_g': 'new_v', 'new_v_conv_ln_b': 'new_v', 'new_v_mla_w_in': 'new_v', 'new_v_mla_q_norm_g': 'new_v', 'new_v_mla_w_uq': 'new_v', 'new_v_mla_kv_norm_g': 'new_v', 'new_v_mla_w_ukv': 'new_v', 'new_v_final_norm_g': 'new_v'}


def _forward(args):
    return _fwd_reference(*[args[k] for k in FWD_PARAMS])


def _output_shape():
    out = _jax.eval_shape(lambda: _forward(_fwd_setup_inputs(0)))
    return out.shape, out.dtype

N_MICROBATCH = 1
ADAM_LR = 0.001
ADAM_B1 = 0.9
ADAM_B2 = 0.999
ADAM_EPS = 1e-08
ADAM_WD = 0.01
ADAM_STEP = 10
PER_EXAMPLE_BATCH_AXIS = {'x': 0, 'mem': 0, 'positions': 0, 'loss_target': 0}
SHARED_INPUTS = []
_WEIGHT_DTYPES = {'norm_g': _jnp.float32, 'mem_norm_g': _jnp.float32, 'w_mem_kv': _jnp.float32, 'w_out': _jnp.float32, 'conv_w_in': _jnp.float32, 'conv_dw': _jnp.float32, 'conv_dw_b': _jnp.float32, 'conv_ln_g': _jnp.float32, 'conv_ln_b': _jnp.float32, 'mla_w_in': _jnp.float32, 'mla_q_norm_g': _jnp.float32, 'mla_w_uq': _jnp.float32, 'mla_kv_norm_g': _jnp.float32, 'mla_w_ukv': _jnp.float32, 'final_norm_g': _jnp.float32}
MOMENT_SCALE = {'norm_g': 6.326992e-02, 'mem_norm_g': 7.949480e-03, 'w_mem_kv': 7.523840e-03, 'w_out': 4.023575e-02, 'conv_w_in': 3.449491e-02, 'conv_dw': 4.514653e-02, 'conv_dw_b': 9.951650e-02, 'conv_ln_g': 5.233356e-02, 'conv_ln_b': 4.694806e-02, 'mla_w_in': 1.960319e-02, 'mla_q_norm_g': 2.238059e-02, 'mla_w_uq': 1.047119e-02, 'mla_kv_norm_g': 4.953361e-02, 'mla_w_ukv': 1.347376e-02, 'final_norm_g': 3.197281e+01}


def _to_microbatches(a, axis):
    t = _jnp.moveaxis(a, axis, 0)
    t = t.reshape((N_MICROBATCH, t.shape[0] // N_MICROBATCH) + t.shape[1:])
    return _jnp.moveaxis(t, 1, axis + 1)


def setup_inputs(seed: int = 0) -> dict:
    inp = _fwd_setup_inputs(seed)
    key = _jax.random.fold_in(_jax.random.key(seed), 7919)
    shape, _ = _output_shape()
    out = dict(inp)
    out["loss_target"] = _jax.random.normal(_jax.random.fold_in(key, 0), shape, _jnp.float32)
    for i, name in enumerate(TWIN_WEIGHTS):
        w = inp[name].astype(_jnp.float32)
        if MOMENT_SCALE is None:
            s = _jnp.sqrt(_jnp.mean(_jnp.square(w)) + 1e-30)
        else:
            s = MOMENT_SCALE[name]
        km, kv = _jax.random.split(_jax.random.fold_in(key, i + 1))
        out[name] = w
        out["m_" + name] = s * _jax.random.normal(km, w.shape, _jnp.float32)
        out["v_" + name] = (s * s) * _jax.random.uniform(kv, w.shape, _jnp.float32, 0.5, 1.5)
    if N_MICROBATCH > 1:
        for name, axis in PER_EXAMPLE_BATCH_AXIS.items():
            out[name] = _to_microbatches(out[name], axis)
    return {'x': out['x'], 'mem': out['mem'], 'positions': out['positions'], 'norm_g': out['norm_g'], 'mem_norm_g': out['mem_norm_g'], 'w_mem_kv': out['w_mem_kv'], 'w_out': out['w_out'], 'conv_w_in': out['conv_w_in'], 'conv_dw': out['conv_dw'], 'conv_dw_b': out['conv_dw_b'], 'conv_ln_g': out['conv_ln_g'], 'conv_ln_b': out['conv_ln_b'], 'mla_w_in': out['mla_w_in'], 'mla_q_norm_g': out['mla_q_norm_g'], 'mla_w_uq': out['mla_w_uq'], 'mla_kv_norm_g': out['mla_kv_norm_g'], 'mla_w_ukv': out['mla_w_ukv'], 'final_norm_g': out['final_norm_g'], 'loss_target': out['loss_target'], 'm_norm_g': out['m_norm_g'], 'm_mem_norm_g': out['m_mem_norm_g'], 'm_w_mem_kv': out['m_w_mem_kv'], 'm_w_out': out['m_w_out'], 'm_conv_w_in': out['m_conv_w_in'], 'm_conv_dw': out['m_conv_dw'], 'm_conv_dw_b': out['m_conv_dw_b'], 'm_conv_ln_g': out['m_conv_ln_g'], 'm_conv_ln_b': out['m_conv_ln_b'], 'm_mla_w_in': out['m_mla_w_in'], 'm_mla_q_norm_g': out['m_mla_q_norm_g'], 'm_mla_w_uq': out['m_mla_w_uq'], 'm_mla_kv_norm_g': out['m_mla_kv_norm_g'], 'm_mla_w_ukv': out['m_mla_w_ukv'], 'm_final_norm_g': out['m_final_norm_g'], 'v_norm_g': out['v_norm_g'], 'v_mem_norm_g': out['v_mem_norm_g'], 'v_w_mem_kv': out['v_w_mem_kv'], 'v_w_out': out['v_w_out'], 'v_conv_w_in': out['v_conv_w_in'], 'v_conv_dw': out['v_conv_dw'], 'v_conv_dw_b': out['v_conv_dw_b'], 'v_conv_ln_g': out['v_conv_ln_g'], 'v_conv_ln_b': out['v_conv_ln_b'], 'v_mla_w_in': out['v_mla_w_in'], 'v_mla_q_norm_g': out['v_mla_q_norm_g'], 'v_mla_w_uq': out['v_mla_w_uq'], 'v_mla_kv_norm_g': out['v_mla_kv_norm_g'], 'v_mla_w_ukv': out['v_mla_w_ukv'], 'v_final_norm_g': out['v_final_norm_g']}


def _loss(weights, diff, rest, loss_target):
    with _jax.named_scope("forward"):
        args = {**rest, TWIN_DIFF_INPUT: diff, **{k: w.astype(_WEIGHT_DTYPES[k]) for k, w in weights.items()}}
        y = _forward(args)
    with _jax.named_scope("loss_head"):
        err = _jnp.square(y.astype(_jnp.float32) - loss_target)
        return 0.5 * _jnp.sum(_jnp.mean(err, axis=-1)) if err.ndim else 0.5 * err


def _adamw(w, g, m, v):
    m = ADAM_B1 * m + (1.0 - ADAM_B1) * g
    v = ADAM_B2 * v + (1.0 - ADAM_B2) * _jnp.square(g)
    m_hat = m / (1.0 - ADAM_B1 ** ADAM_STEP)
    v_hat = v / (1.0 - ADAM_B2 ** ADAM_STEP)
    delta = -ADAM_LR * (m_hat / (_jnp.sqrt(v_hat) + ADAM_EPS) + ADAM_WD * w)
    return delta, m, v


def reference(x, mem, positions, norm_g, mem_norm_g, w_mem_kv, w_out, conv_w_in, conv_dw, conv_dw_b, conv_ln_g, conv_ln_b, mla_w_in, mla_q_norm_g, mla_w_uq, mla_kv_norm_g, mla_w_ukv, final_norm_g, loss_target, m_norm_g, m_mem_norm_g, m_w_mem_kv, m_w_out, m_conv_w_in, m_conv_dw, m_conv_dw_b, m_conv_ln_g, m_conv_ln_b, m_mla_w_in, m_mla_q_norm_g, m_mla_w_uq, m_mla_kv_norm_g, m_mla_w_ukv, m_final_norm_g, v_norm_g, v_mem_norm_g, v_w_mem_kv, v_w_out, v_conv_w_in, v_conv_dw, v_conv_dw_b, v_conv_ln_g, v_conv_ln_b, v_mla_w_in, v_mla_q_norm_g, v_mla_w_uq, v_mla_kv_norm_g, v_mla_w_ukv, v_final_norm_g):
    given = dict(x=x, mem=mem, positions=positions, norm_g=norm_g, mem_norm_g=mem_norm_g, w_mem_kv=w_mem_kv, w_out=w_out, conv_w_in=conv_w_in, conv_dw=conv_dw, conv_dw_b=conv_dw_b, conv_ln_g=conv_ln_g, conv_ln_b=conv_ln_b, mla_w_in=mla_w_in, mla_q_norm_g=mla_q_norm_g, mla_w_uq=mla_w_uq, mla_kv_norm_g=mla_kv_norm_g, mla_w_ukv=mla_w_ukv, final_norm_g=final_norm_g, loss_target=loss_target, m_norm_g=m_norm_g, m_mem_norm_g=m_mem_norm_g, m_w_mem_kv=m_w_mem_kv, m_w_out=m_w_out, m_conv_w_in=m_conv_w_in, m_conv_dw=m_conv_dw, m_conv_dw_b=m_conv_dw_b, m_conv_ln_g=m_conv_ln_g, m_conv_ln_b=m_conv_ln_b, m_mla_w_in=m_mla_w_in, m_mla_q_norm_g=m_mla_q_norm_g, m_mla_w_uq=m_mla_w_uq, m_mla_kv_norm_g=m_mla_kv_norm_g, m_mla_w_ukv=m_mla_w_ukv, m_final_norm_g=m_final_norm_g, v_norm_g=v_norm_g, v_mem_norm_g=v_mem_norm_g, v_w_mem_kv=v_w_mem_kv, v_w_out=v_w_out, v_conv_w_in=v_conv_w_in, v_conv_dw=v_conv_dw, v_conv_dw_b=v_conv_dw_b, v_conv_ln_g=v_conv_ln_g, v_conv_ln_b=v_conv_ln_b, v_mla_w_in=v_mla_w_in, v_mla_q_norm_g=v_mla_q_norm_g, v_mla_w_uq=v_mla_w_uq, v_mla_kv_norm_g=v_mla_kv_norm_g, v_mla_w_ukv=v_mla_w_ukv, v_final_norm_g=v_final_norm_g)
    weights = {n: given[n] for n in TWIN_WEIGHTS}
    shared = {n: given[n] for n in SHARED_INPUTS}
    per_example = {n: given[n] for n in ['x', 'mem', 'positions']}
    grad_fn = _jax.value_and_grad(_loss, argnums=(0, 1))

    def one_microbatch(ex, loss_target):
        ex = dict(ex)
        diff = ex.pop(TWIN_DIFF_INPUT)
        return grad_fn(weights, diff, {**shared, **ex}, loss_target)

    if N_MICROBATCH == 1:
        loss, (grad_w, grad_x) = one_microbatch(per_example, given["loss_target"])
    else:
        def body(carry, xs):
            loss_sum, grad_sum = carry
            l_k, (gw_k, gx_k) = one_microbatch(xs[0], xs[1])
            with _jax.named_scope("update"):
                return (loss_sum + l_k, _jax.tree.map(_jnp.add, grad_sum, gw_k)), gx_k

        init = (_jnp.zeros((), _jnp.float32), _jax.tree.map(_jnp.zeros_like, weights))
        (loss, grad_w), grad_x = _jax.lax.scan(body, init, (per_example, given["loss_target"]))
    with _jax.named_scope("update"):
        delta_w, new_m, new_v = {}, {}, {}
        for n in TWIN_WEIGHTS:
            delta_w[n], new_m[n], new_v[n] = _adamw(weights[n], grad_w[n], given["m_" + n], given["v_" + n])
    return (loss, grad_x, *[grad_w[n] for n in TWIN_WEIGHTS], *[delta_w[n] for n in TWIN_WEIGHTS],
            *[new_m[n] for n in TWIN_WEIGHTS], *[new_v[n] for n in TWIN_WEIGHTS])
```

```python
import jax
import jax.numpy as jnp
from jax import lax
from jax.experimental import pallas as pl
from jax.experimental.pallas import tpu as pltpu

F32 = jnp.float32
BF16 = jnp.bfloat16
MESH = pl.DeviceIdType.MESH
N_DEV = 8
VMEM_LIMIT_BYTES = 48 * 1024 * 1024

MEM_HEADS, MEM_HEAD_DIM = 4, 128
MEM_WIDTH = MEM_HEADS * MEM_HEAD_DIM
CONV_KERNEL = 31
CONV_PAD = 32
MLA_HEADS, MLA_NOPE, MLA_ROPE, MLA_V = 12, 128, 64, 128
MLA_QK = MLA_NOPE + MLA_ROPE
HALF_ROPE = MLA_ROPE // 2
Q_RANK, KV_RANK = 512, 256
ROPE_THETA = 10000.0
ROPE_LANES = 512
RMS_EPS = 1e-6
LN_EPS = 1e-5
ADAM_LR, ADAM_B1, ADAM_B2, ADAM_EPS, ADAM_WD, ADAM_STEP = 0.001, 0.9, 0.999, 1e-08, 0.01, 10
NEG = -1e30
PACK_COLS = 1024
PACK_ROW_TILE = 128


def _call(body, name, out_shape, grid=None, in_specs=None, out_specs=None, scratch=(), dims=None, grid_spec=None):
    params = dict(vmem_limit_bytes=VMEM_LIMIT_BYTES)
    if dims is not None:
        params["dimension_semantics"] = dims
    kw = {}
    if grid_spec is not None:
        kw["grid_spec"] = grid_spec
    else:
        if grid is not None:
            kw["grid"] = grid
        kw["in_specs"] = in_specs
        kw["out_specs"] = out_specs
        kw["scratch_shapes"] = list(scratch)
    return pl.pallas_call(body, name=name, out_shape=out_shape, compiler_params=pltpu.CompilerParams(**params), **kw)


def _pick(n, target, mult):
    best = None
    for d in range(mult, min(n, target) + 1, mult):
        if n % d == 0:
            best = d
    return n if best is None else best


_DOT_DIMS = {"nn": (((1,), (0,)), ((), ())), "nt": (((1,), (1,)), ((), ())), "tn": (((0,), (0,)), ((), ()))}


def _mm(a, b, mode, out_dtype, name, res=None):
    if mode == "nn":
        (M, K), N = a.shape, b.shape[1]
    elif mode == "nt":
        (M, K), N = a.shape, b.shape[0]
    else:
        (K, M), N = a.shape, b.shape[1]
    tm = _pick(M, 1024 if mode != "tn" else 512, 8)
    tn = _pick(N, 512, 128)
    tk = _pick(K, 1024, 128)
    nk = K // tk
    has_res = res is not None

    def body(*refs):
        if has_res:
            a_ref, b_ref, r_ref, o_ref, acc = refs
        else:
            a_ref, b_ref, o_ref, acc = refs
        k = pl.program_id(2)

        @pl.when(k == 0)
        def _():
            acc[...] = jnp.zeros_like(acc)

        acc[...] += lax.dot_general(a_ref[...].astype(BF16), b_ref[...].astype(BF16), _DOT_DIMS[mode],
                                    preferred_element_type=F32)

        @pl.when(k == nk - 1)
        def _():
            r = acc[...]
            if has_res:
                r = r + r_ref[...]
            o_ref[...] = r.astype(o_ref.dtype)

    a_spec = {"nn": pl.BlockSpec((tm, tk), lambda i, j, k: (i, k)),
              "nt": pl.BlockSpec((tm, tk), lambda i, j, k: (i, k)),
              "tn": pl.BlockSpec((tk, tm), lambda i, j, k: (k, i))}[mode]
    b_spec = {"nn": pl.BlockSpec((tk, tn), lambda i, j, k: (k, j)),
              "nt": pl.BlockSpec((tn, tk), lambda i, j, k: (j, k)),
              "tn": pl.BlockSpec((tk, tn), lambda i, j, k: (k, j))}[mode]
    o_spec = pl.BlockSpec((tm, tn), lambda i, j, k: (i, j))
    in_specs = [a_spec, b_spec] + ([o_spec] if has_res else [])
    args = (a, b) + ((res,) if has_res else ())
    return _call(body, name, jax.ShapeDtypeStruct((M, N), out_dtype), grid=(M // tm, N // tn, nk),
                 in_specs=in_specs, out_specs=o_spec, scratch=[pltpu.VMEM((tm, tn), F32)],
                 dims=("parallel", "parallel", "arbitrary"))(*args)


def _rowwise(f, rows, params, outs, name, tb=256):
    T = rows[0].shape[0]
    tb = min(tb, T)
    nr, npar = len(rows), len(params)

    def body(*refs):
        vals = f(*[r[...].astype(F32) for r in refs[:nr]], *[p[...] for p in refs[nr:nr + npar]])
        for o_ref, v in zip(refs[nr + npar:], vals):
            o_ref[...] = v.astype(o_ref.dtype)

    row_spec = lambda w: pl.BlockSpec((tb, w), lambda i: (i, 0))
    par_spec = lambda w: pl.BlockSpec((1, w), lambda i: (0, 0))
    res = _call(body, name, [jax.ShapeDtypeStruct((T, w), dt) for w, dt in outs], grid=(T // tb,),
                in_specs=[row_spec(r.shape[1]) for r in rows] + [par_spec(p.shape[1]) for p in params],
                out_specs=[row_spec(w) for w, _ in outs], dims=("parallel",))(*rows, *params)
    return res


def _rowwise_bwd(f, rows, params, douts, n_diff, name, tb=256):
    T = rows[0].shape[0]
    tb = min(tb, T)
    nr, npar, nd = len(rows), len(params), len(douts)

    def body(*refs):
        rv = [r[...].astype(F32) for r in refs[:nr]]
        pv = [p[...] for p in refs[nr:nr + npar]]
        dv = [d[...].astype(F32) for d in refs[nr + npar:nr + npar + nd]]
        o_refs = refs[nr + npar + nd:]
        fixed = rv[n_diff:]

        def g(*xs):
            return tuple(f(*xs[:n_diff], *fixed, *xs[n_diff:]))

        _, vjp = jax.vjp(g, *rv[:n_diff], *pv)
        grads = vjp(tuple(dv))
        for o_ref, gr in zip(o_refs[:n_diff], grads[:n_diff]):
            o_ref[...] = gr.astype(o_ref.dtype)
        first = pl.program_id(0) == 0
        for o_ref, gr in zip(o_refs[n_diff:], grads[n_diff:]):
            @pl.when(first)
            def _(o_ref=o_ref):
                o_ref[...] = jnp.zeros_like(o_ref)

            o_ref[...] += gr

    row_spec = lambda w: pl.BlockSpec((tb, w), lambda i: (i, 0))
    par_spec = lambda w: pl.BlockSpec((1, w), lambda i: (0, 0))
    out_shape = ([jax.ShapeDtypeStruct((T, r.shape[1]), F32) for r in rows[:n_diff]]
                 + [jax.ShapeDtypeStruct((1, p.shape[1]), F32) for p in params])
    return _call(body, name, out_shape, grid=(T // tb,),
                 in_specs=([row_spec(r.shape[1]) for r in rows] + [par_spec(p.shape[1]) for p in params]
                           + [row_spec(d.shape[1]) for d in douts]),
                 out_specs=([row_spec(r.shape[1]) for r in rows[:n_diff]] + [par_spec(p.shape[1]) for p in params]),
                 dims=("arbitrary",))(*rows, *params, *douts)


def _sig(x):
    return 1.0 / (1.0 + jnp.exp(-x))


def _rms(x, g):
    return x * lax.rsqrt(jnp.mean(x * x, axis=-1, keepdims=True) + RMS_EPS) * g


def _f_rms(x, g):
    return (_rms(x, g),)


def _f_glu(a, gate):
    return (a * _sig(gate),)


def _f_ln_silu(x, g, b):
    mu = jnp.mean(x, axis=-1, keepdims=True)
    xc = x - mu
    var = jnp.mean(xc * xc, axis=-1, keepdims=True)
    y = xc * lax.rsqrt(var + LN_EPS) * g + b
    return (y * _sig(y),)


def _f_gate(y_main, y_mem, z):
    return (jnp.concatenate([y_main, y_mem], axis=-1) * (z * _sig(z)),)


def _f_rope(x1, x2, cos, sin):
    return (x1 * cos - x2 * sin, x2 * cos + x1 * sin)


def _f_rope_t(d1, d2, cos, sin):
    return (d1 * cos + d2 * sin, d2 * cos - d1 * sin)


def _f_rowdot(a, b):
    return (jnp.sum(a * b, axis=-1, keepdims=True),)


def _final_loss(h, tgt, g, name, tb=256):
    T, D = h.shape

    def body(h_ref, t_ref, g_ref, dh_ref, dg_ref, loss_ref):
        tv = t_ref[...]

        def rowloss(hh, gg):
            e = _rms(hh, gg) - tv
            return 0.5 * jnp.mean(e * e, axis=-1, keepdims=True)

        lr, vjp = jax.vjp(rowloss, h_ref[...], g_ref[...])
        dh, dg = vjp(jnp.ones_like(lr))
        dh_ref[...] = dh

        @pl.when(pl.program_id(0) == 0)
        def _():
            dg_ref[...] = jnp.zeros_like(dg_ref)
            loss_ref[...] = jnp.zeros_like(loss_ref)

        dg_ref[...] += dg
        loss_ref[...] += jnp.broadcast_to(jnp.sum(lr, axis=0, keepdims=True), loss_ref.shape)

    row = pl.BlockSpec((tb, D), lambda i: (i, 0))
    par = pl.BlockSpec((1, D), lambda i: (0, 0))
    return _call(body, name,
                 [jax.ShapeDtypeStruct((T, D), F32), jax.ShapeDtypeStruct((1, D), F32), jax.ShapeDtypeStruct((1, 128), F32)],
                 grid=(T // tb,), in_specs=[row, row, par],
                 out_specs=[row, par, pl.BlockSpec((1, 128), lambda i: (0, 0))], dims=("arbitrary",))(h, tgt, g)


CONV_ROWS = 128
CONV_LANES = 256


def _dwconv_fwd(x, w, b, name):
    B, S, C = x.shape
    cb = CONV_LANES
    off = CONV_PAD - (CONV_KERNEL - 1)

    def body(x_ref, w_ref, b_ref, o_ref, pad):
        pad[0:CONV_PAD, :] = jnp.zeros((CONV_PAD, cb), F32)
        pad[CONV_PAD:, :] = x_ref[...]
        for t0 in range(0, S, CONV_ROWS):
            acc = jnp.broadcast_to(b_ref[...], (CONV_ROWS, cb))
            for k in range(CONV_KERNEL):
                acc = acc + w_ref[k:k + 1, :] * pad[t0 + off + k:t0 + off + k + CONV_ROWS, :]
            o_ref[t0:t0 + CONV_ROWS, :] = acc

    return _call(body, name, jax.ShapeDtypeStruct((B, S, C), F32), grid=(B, C // cb),
                 in_specs=[pl.BlockSpec((None, S, cb), lambda i, j: (i, 0, j)),
                           pl.BlockSpec((CONV_KERNEL, cb), lambda i, j: (0, j)),
                           pl.BlockSpec((1, cb), lambda i, j: (0, j))],
                 out_specs=pl.BlockSpec((None, S, cb), lambda i, j: (i, 0, j)),
                 scratch=[pltpu.VMEM((S + CONV_PAD, cb), F32)], dims=("parallel", "parallel"))(x, w, b)


def _dwconv_bwd(x, w, dy, name):
    B, S, C = x.shape
    cb = CONV_LANES
    off = CONV_PAD - (CONV_KERNEL - 1)
    groups = CONV_ROWS // 8

    def body(x_ref, w_ref, dy_ref, dx_ref, dw_ref, db_ref, xpad, dypad, wacc):
        xpad[0:CONV_PAD, :] = jnp.zeros((CONV_PAD, cb), F32)
        xpad[CONV_PAD:, :] = x_ref[...]
        dypad[0:S, :] = dy_ref[...]
        dypad[S:, :] = jnp.zeros((CONV_PAD, cb), F32)
        wacc[...] = jnp.zeros_like(wacc)
        for t0 in range(0, S, CONV_ROWS):
            dyc = dy_ref[t0:t0 + CONV_ROWS, :]
            acc = jnp.zeros((CONV_ROWS, cb), F32)
            for k in range(CONV_KERNEL):
                acc = acc + w_ref[k:k + 1, :] * dypad[t0 + (CONV_KERNEL - 1) - k:t0 + (CONV_KERNEL - 1) - k + CONV_ROWS, :]
                prod = dyc * xpad[t0 + off + k:t0 + off + k + CONV_ROWS, :]
                wacc[k] += jnp.sum(prod.reshape(groups, 8, cb), axis=0)
            wacc[CONV_KERNEL] += jnp.sum(dyc.reshape(groups, 8, cb), axis=0)
            dx_ref[t0:t0 + CONV_ROWS, :] = acc

        @pl.when(pl.program_id(1) == 0)
        def _():
            dw_ref[...] = jnp.zeros_like(dw_ref)
            db_ref[...] = jnp.zeros_like(db_ref)

        for k in range(CONV_KERNEL):
            dw_ref[k:k + 1, :] += jnp.sum(wacc[k], axis=0, keepdims=True)
        db_ref[...] += jnp.sum(wacc[CONV_KERNEL], axis=0, keepdims=True)

    blk = pl.BlockSpec((None, S, cb), lambda j, i: (i, 0, j))
    return _call(body, name,
                 [jax.ShapeDtypeStruct((B, S, C), F32), jax.ShapeDtypeStruct((CONV_KERNEL, C), F32),
                  jax.ShapeDtypeStruct((1, C), F32)],
                 grid=(C // cb, B),
                 in_specs=[blk, pl.BlockSpec((CONV_KERNEL, cb), lambda j, i: (0, j)), blk],
                 out_specs=[blk, pl.BlockSpec((CONV_KERNEL, cb), lambda j, i: (0, j)),
                            pl.BlockSpec((1, cb), lambda j, i: (0, j))],
                 scratch=[pltpu.VMEM((S + CONV_PAD, cb), F32), pltpu.VMEM((S + CONV_PAD, cb), F32),
                          pltpu.VMEM((CONV_KERNEL + 1, 8, cb), F32)],
                 dims=("parallel", "arbitrary"))(x, w, dy)


ATTN_TILE = 512


def _attn_tiles(Sq, Sk, causal):
    tq = min(Sq, ATTN_TILE)
    tk = tq if causal else min(Sk, ATTN_TILE)
    return tq, tk


def _causal_mask(qi, kj, tq, tk):
    rows = qi * tq + lax.broadcasted_iota(jnp.int32, (tq, tk), 0)
    cols = kj * tk + lax.broadcasted_iota(jnp.int32, (tq, tk), 1)
    return cols <= rows


def _flash_fwd(q, k, v, causal, scale, name):
    G, Sq, dk = q.shape
    Sk, dv = v.shape[1], v.shape[2]
    tq, tk = _attn_tiles(Sq, Sk, causal)
    nq, nk = Sq // tq, Sk // tk

    def body(q_ref, k_ref, v_ref, o_ref, lse_ref, m_s, l_s, acc):
        qi, kj = pl.program_id(1), pl.program_id(2)

        @pl.when(kj == 0)
        def _():
            m_s[...] = jnp.full_like(m_s, NEG)
            l_s[...] = jnp.zeros_like(l_s)
            acc[...] = jnp.zeros_like(acc)

        def step(masked):
            s = lax.dot_general(q_ref[...], k_ref[...], _DOT_DIMS["nt"], preferred_element_type=F32) * scale
            if masked:
                s = jnp.where(_causal_mask(qi, kj, tq, tk), s, NEG)
            m_old = m_s[...]
            m_new = jnp.maximum(m_old, jnp.max(s, axis=-1, keepdims=True))
            p = jnp.exp(s - m_new)
            alpha = jnp.exp(m_old - m_new)
            l_s[...] = alpha * l_s[...] + jnp.sum(p, axis=-1, keepdims=True)
            acc[...] = alpha * acc[...] + jnp.dot(p.astype(BF16), v_ref[...], preferred_element_type=F32)
            m_s[...] = m_new

        if causal:
            pl.when(kj < qi)(lambda: step(False))
            pl.when(kj == qi)(lambda: step(True))
            last = kj == qi
        else:
            step(False)
            last = kj == nk - 1

        @pl.when(last)
        def _():
            o_ref[...] = (acc[...] / l_s[...]).astype(o_ref.dtype)
            lse_ref[...] = m_s[...] + jnp.log(l_s[...])

    kv_idx = (lambda g, i, j: (g, jnp.minimum(i, j), 0)) if causal else (lambda g, i, j: (g, j, 0))
    return _call(body, name,
                 [jax.ShapeDtypeStruct((G, Sq, dv), F32), jax.ShapeDtypeStruct((G, Sq, 1), F32)],
                 grid=(G, nq, nk),
                 in_specs=[pl.BlockSpec((None, tq, dk), lambda g, i, j: (g, i, 0)),
                           pl.BlockSpec((None, tk, dk), kv_idx), pl.BlockSpec((None, tk, dv), kv_idx)],
                 out_specs=[pl.BlockSpec((None, tq, dv), lambda g, i, j: (g, i, 0)),
                            pl.BlockSpec((None, tq, 1), lambda g, i, j: (g, i, 0))],
                 scratch=[pltpu.VMEM((tq, 1), F32), pltpu.VMEM((tq, 1), F32), pltpu.VMEM((tq, dv), F32)],
                 dims=("parallel", "parallel", "arbitrary"))(q, k, v)


def _flash_bwd_dq(q, k, v, do, lse, delta, causal, scale, name):
    G, Sq, dk = q.shape
    Sk, dv = v.shape[1], v.shape[2]
    tq, tk = _attn_tiles(Sq, Sk, causal)
    nq, nk = Sq // tq, Sk // tk

    def body(q_ref, k_ref, v_ref, do_ref, lse_ref, dl_ref, dq_ref, acc):
        qi, kj = pl.program_id(1), pl.program_id(2)

        @pl.when(kj == 0)
        def _():
            acc[...] = jnp.zeros_like(acc)

        def step(masked):
            s = lax.dot_general(q_ref[...], k_ref[...], _DOT_DIMS["nt"], preferred_element_type=F32) * scale
            if masked:
                s = jnp.where(_causal_mask(qi, kj, tq, tk), s, NEG)
            p = jnp.exp(s - lse_ref[...])
            dp = lax.dot_general(do_ref[...].astype(BF16), v_ref[...], _DOT_DIMS["nt"], preferred_element_type=F32)
            ds = p * (dp - dl_ref[...]) * scale
            acc[...] += jnp.dot(ds.astype(BF16), k_ref[...], preferred_element_type=F32)

        if causal:
            pl.when(kj < qi)(lambda: step(False))
            pl.when(kj == qi)(lambda: step(True))
            last = kj == qi
        else:
            step(False)
            last = kj == nk - 1

        @pl.when(last)
        def _():
            dq_ref[...] = acc[...]

    kv_idx = (lambda g, i, j: (g, jnp.minimum(i, j), 0)) if causal else (lambda g, i, j: (g, j, 0))
    q_idx = lambda g, i, j: (g, i, 0)
    return _call(body, name, jax.ShapeDtypeStruct((G, Sq, dk), F32), grid=(G, nq, nk),
                 in_specs=[pl.BlockSpec((None, tq, dk), q_idx), pl.BlockSpec((None, tk, dk), kv_idx),
                           pl.BlockSpec((None, tk, dv), kv_idx), pl.BlockSpec((None, tq, dv), q_idx),
                           pl.BlockSpec((None, tq, 1), q_idx), pl.BlockSpec((None, tq, 1), q_idx)],
                 out_specs=pl.BlockSpec((None, tq, dk), q_idx),
                 scratch=[pltpu.VMEM((tq, dk), F32)],
                 dims=("parallel", "parallel", "arbitrary"))(q, k, v, do, lse, delta)


def _flash_bwd_dkv(q, k, v, do, lse_row, delta_row, causal, scale, name):
    G, Sq, dk = q.shape
    Sk, dv = v.shape[1], v.shape[2]
    tq, tk = _attn_tiles(Sq, Sk, causal)
    nq, nk = Sq // tq, Sk // tk

    def body(q_ref, k_ref, v_ref, do_ref, lse_ref, dl_ref, dk_ref, dv_ref, dk_acc, dv_acc):
        kj, qi = pl.program_id(1), pl.program_id(2)
        first = (qi == kj) if causal else (qi == 0)

        @pl.when(first)
        def _():
            dk_acc[...] = jnp.zeros_like(dk_acc)
            dv_acc[...] = jnp.zeros_like(dv_acc)

        def step(masked):
            st = lax.dot_general(k_ref[...], q_ref[...], _DOT_DIMS["nt"], preferred_element_type=F32) * scale
            if masked:
                rows = kj * tk + lax.broadcasted_iota(jnp.int32, (tk, tq), 0)
                cols = qi * tq + lax.broadcasted_iota(jnp.int32, (tk, tq), 1)
                st = jnp.where(rows <= cols, st, NEG)
            pt = jnp.exp(st - lse_ref[...])
            dob = do_ref[...].astype(BF16)
            dv_acc[...] += jnp.dot(pt.astype(BF16), dob, preferred_element_type=F32)
            dpt = lax.dot_general(v_ref[...], dob, _DOT_DIMS["nt"], preferred_element_type=F32)
            dst = pt * (dpt - dl_ref[...]) * scale
            dk_acc[...] += jnp.dot(dst.astype(BF16), q_ref[...], preferred_element_type=F32)

        if causal:
            pl.when(qi > kj)(lambda: step(False))
            pl.when(qi == kj)(lambda: step(True))
        else:
            step(False)

        @pl.when(qi == nq - 1)
        def _():
            dk_ref[...] = dk_acc[...]
            dv_ref[...] = dv_acc[...]

    q_idx = (lambda g, j, i: (g, jnp.maximum(i, j), 0)) if causal else (lambda g, j, i: (g, i, 0))
    qrow_idx = (lambda g, j, i: (g, 0, jnp.maximum(i, j))) if causal else (lambda g, j, i: (g, 0, i))
    kv_idx = lambda g, j, i: (g, j, 0)
    return _call(body, name,
                 [jax.ShapeDtypeStruct((G, Sk, dk), F32), jax.ShapeDtypeStruct((G, Sk, dv), F32)],
                 grid=(G, nk, nq),
                 in_specs=[pl.BlockSpec((None, tq, dk), q_idx), pl.BlockSpec((None, tk, dk), kv_idx),
                           pl.BlockSpec((None, tk, dv), kv_idx), pl.BlockSpec((None, tq, dv), q_idx),
                           pl.BlockSpec((None, 1, tq), qrow_idx), pl.BlockSpec((None, 1, tq), qrow_idx)],
                 out_specs=[pl.BlockSpec((None, tk, dk), kv_idx), pl.BlockSpec((None, tk, dv), kv_idx)],
                 scratch=[pltpu.VMEM((tk, dk), F32), pltpu.VMEM((tk, dv), F32)],
                 dims=("parallel", "parallel", "arbitrary"))(q, k, v, do, lse_row, delta_row)


def _attention_bwd(q, k, v, o, lse, do, causal, scale, name):
    G, Sq, dv = o.shape
    (delta,) = _rowwise(_f_rowdot, [do.reshape(G * Sq, dv), o.reshape(G * Sq, dv)], [], [(1, F32)], name + "_delta")
    delta = delta.reshape(G, Sq, 1)
    dq = _flash_bwd_dq(q, k, v, do, lse, delta, causal, scale, name + "_dq")
    dk_, dv_ = _flash_bwd_dkv(q, k, v, do, lse.reshape(G, 1, Sq), delta.reshape(G, 1, Sq), causal, scale, name + "_dkv")
    return dq, dk_, dv_


def _to_heads(a, B, S, H):
    return a.reshape(B, S, H, -1).transpose(0, 2, 1, 3).reshape(B * H, S, -1)


def _from_heads(a, B, S, H):
    return a.reshape(B, H, S, -1).transpose(0, 2, 1, 3).reshape(B * S, -1)


def _mem_attention_fwd(q_mem, mem2, mem_g, w_mem, B, S, tag):
    M = mem2.shape[0] // B
    (memn,) = _rowwise(_f_rms, [mem2], [mem_g], [(mem2.shape[1], BF16)], tag + "_memnorm")
    kvm = _mm(memn, w_mem, "nn", BF16, tag + "_memkv")
    qh = _to_heads(q_mem.astype(BF16), B, S, MEM_HEADS)
    kh = _to_heads(kvm[:, :MEM_WIDTH], B, M, MEM_HEADS)
    vh = _to_heads(kvm[:, MEM_WIDTH:], B, M, MEM_HEADS)
    o, lse = _flash_fwd(qh, kh, vh, False, MEM_HEAD_DIM ** -0.5, tag + "_memattn")
    return _from_heads(o, B, S, MEM_HEADS), (memn, qh, kh, vh, o, lse)


def _mem_attention_bwd(d_y_mem, saved, mem2, mem_g, w_mem, B, S, tag):
    memn, qh, kh, vh, o, lse = saved
    M = mem2.shape[0] // B
    do = _to_heads(d_y_mem, B, S, MEM_HEADS)
    dq, dk_, dv_ = _attention_bwd(qh, kh, vh, o, lse, do, False, MEM_HEAD_DIM ** -0.5, tag + "_memattn_bwd")
    d_q_mem = _from_heads(dq, B, S, MEM_HEADS)
    d_kvm = jnp.concatenate([_from_heads(dk_, B, M, MEM_HEADS), _from_heads(dv_, B, M, MEM_HEADS)], axis=1)
    d_w_mem = _mm(memn, d_kvm, "tn", F32, tag + "_memkv_dw")
    d_memn = _mm(d_kvm, w_mem, "nt", F32, tag + "_memkv_dx")
    _, d_mem_g = _rowwise_bwd(_f_rms, [mem2], [mem_g], [d_memn], 1, tag + "_memnorm_bwd")
    return d_q_mem, d_w_mem, d_mem_g


def _rope_tables(positions):
    inv_freq = 1.0 / (ROPE_THETA ** (jnp.arange(0, MLA_ROPE, 2, dtype=F32) / MLA_ROPE))
    ang = positions.astype(F32).reshape(-1, 1) * inv_freq
    reps = ROPE_LANES // HALF_ROPE
    return jnp.tile(jnp.cos(ang), (1, reps)), jnp.tile(jnp.sin(ang), (1, reps))


def _forward_backward(x, mem, positions, target, W):
    B, S, D = x.shape
    T = B * S
    conv_w = W["conv_dw"].shape[1]
    h0 = x.reshape(T, D)
    mem2 = mem.reshape(-1, D)
    tgt = target.reshape(T, D)
    row = lambda v: v.reshape(1, -1)

    g0 = row(W["norm_g"][0])
    (u0,) = _rowwise(_f_rms, [h0], [g0], [(D, BF16)], "l0_norm")
    proj0 = _mm(u0, W["conv_w_in"], "nn", F32, "l0_in")
    a0, gate0 = proj0[:, :conv_w], proj0[:, conv_w:2 * conv_w]
    qm0, z0 = proj0[:, 2 * conv_w:2 * conv_w + MEM_WIDTH], proj0[:, 2 * conv_w + MEM_WIDTH:]
    (glu,) = _rowwise(_f_glu, [a0, gate0], [], [(conv_w, F32)], "l0_glu")
    dw, dwb = W["conv_dw"], row(W["conv_dw_b"][0])
    cv = _dwconv_fwd(glu.reshape(B, S, conv_w), dw, dwb, "l0_dwconv").reshape(T, conv_w)
    ln_g, ln_b = row(W["conv_ln_g"][0]), row(W["conv_ln_b"][0])
    (ymain0,) = _rowwise(_f_ln_silu, [cv], [ln_g, ln_b], [(conv_w, F32)], "l0_ln")
    mg0 = row(W["mem_norm_g"][0])
    ymem0, mem_saved0 = _mem_attention_fwd(qm0, mem2, mg0, W["w_mem_kv"][0], B, S, "l0")
    (y0,) = _rowwise(_f_gate, [ymain0, ymem0, z0], [], [(z0.shape[1], BF16)], "l0_gate", tb=128)
    h1 = _mm(y0, W["w_out"][0], "nn", F32, "l0_out", res=h0)

    g1 = row(W["norm_g"][1])
    (u1,) = _rowwise(_f_rms, [h1], [g1], [(D, BF16)], "l1_norm")
    proj1 = _mm(u1, W["mla_w_in"], "nn", F32, "l1_in")
    c1, c2, c3, c4 = Q_RANK, Q_RANK + KV_RANK, Q_RANK + KV_RANK + MEM_WIDTH, Q_RANK + KV_RANK + MEM_WIDTH + 2 * D
    cq, ckv, qm1, z1, kr = proj1[:, :c1], proj1[:, c1:c2], proj1[:, c2:c3], proj1[:, c3:c4], proj1[:, c4:c4 + MLA_ROPE]
    qg, kvg = row(W["mla_q_norm_g"]), row(W["mla_kv_norm_g"])
    (cqn,) = _rowwise(_f_rms, [cq], [qg], [(Q_RANK, BF16)], "l1_qnorm")
    (ckvn,) = _rowwise(_f_rms, [ckv], [kvg], [(KV_RANK, BF16)], "l1_kvnorm")
    qf = _mm(cqn, W["mla_w_uq"], "nn", F32, "l1_uq")
    kvf = _mm(ckvn, W["mla_w_ukv"], "nn", BF16, "l1_ukv")
    n_nope, n_half = MLA_HEADS * MLA_NOPE, MLA_HEADS * HALF_ROPE
    lane_pad = jnp.zeros((T, ROPE_LANES - n_half - HALF_ROPE), F32)
    x1 = jnp.concatenate([qf[:, n_nope:n_nope + n_half], kr[:, :HALF_ROPE], lane_pad], axis=1)
    x2 = jnp.concatenate([qf[:, n_nope + n_half:], kr[:, HALF_ROPE:], lane_pad], axis=1)
    cos, sin = _rope_tables(positions)
    o1, o2 = _rowwise(_f_rope, [x1, x2, cos, sin], [], [(ROPE_LANES, BF16), (ROPE_LANES, BF16)], "l1_rope")
    qn4 = qf[:, :n_nope].astype(BF16).reshape(B, S, MLA_HEADS, MLA_NOPE)
    q4 = jnp.concatenate([qn4, o1[:, :n_half].reshape(B, S, MLA_HEADS, HALF_ROPE),
                          o2[:, :n_half].reshape(B, S, MLA_HEADS, HALF_ROPE)], axis=-1)
    kv4 = kvf.reshape(B, S, MLA_HEADS, MLA_NOPE + MLA_V)
    krot = jnp.concatenate([o1[:, n_half:n_half + HALF_ROPE], o2[:, n_half:n_half + HALF_ROPE]], axis=-1)
    k4 = jnp.concatenate([kv4[..., :MLA_NOPE],
                          jnp.broadcast_to(krot.reshape(B, S, 1, MLA_ROPE), (B, S, MLA_HEADS, MLA_ROPE))], axis=-1)
    qh = q4.transpose(0, 2, 1, 3).reshape(B * MLA_HEADS, S, MLA_QK)
    kh = k4.transpose(0, 2, 1, 3).reshape(B * MLA_HEADS, S, MLA_QK)
    vh = kv4[..., MLA_NOPE:].transpose(0, 2, 1, 3).reshape(B * MLA_HEADS, S, MLA_V)
    oh, lse1 = _flash_fwd(qh, kh, vh, True, MLA_QK ** -0.5, "l1_attn")
    ymain1 = _from_heads(oh, B, S, MLA_HEADS)
    mg1 = row(W["mem_norm_g"][1])
    ymem1, mem_saved1 = _mem_attention_fwd(qm1, mem2, mg1, W["w_mem_kv"][1], B, S, "l1")
    (y1,) = _rowwise(_f_gate, [ymain1, ymem1, z1], [], [(z1.shape[1], BF16)], "l1_gate", tb=128)
    h2 = _mm(y1, W["w_out"][1], "nn", F32, "l1_out", res=h1)

    gf = row(W["final_norm_g"])
    dh2, d_gf, loss128 = _final_loss(h2, tgt, gf, "final_loss")
    G = {"final_norm_g": d_gf.reshape(-1)}

    dy1 = _mm(dh2, W["w_out"][1], "nt", F32, "l1_out_dx")
    d_wout1 = _mm(y1, dh2, "tn", F32, "l1_out_dw")
    d_ymain1, d_ymem1, d_z1 = _rowwise_bwd(_f_gate, [ymain1, ymem1, z1], [], [dy1], 3, "l1_gate_bwd", tb=128)
    d_qm1, d_wmem1, d_mg1 = _mem_attention_bwd(d_ymem1, mem_saved1, mem2, mg1, W["w_mem_kv"][1], B, S, "l1")
    doh = _to_heads(d_ymain1, B, S, MLA_HEADS)
    dqh, dkh, dvh = _attention_bwd(qh, kh, vh, oh, lse1, doh, True, MLA_QK ** -0.5, "l1_attn_bwd")
    dq4 = dqh.reshape(B, MLA_HEADS, S, MLA_QK).transpose(0, 2, 1, 3)
    dk4 = dkh.reshape(B, MLA_HEADS, S, MLA_QK).transpose(0, 2, 1, 3)
    dv4 = dvh.reshape(B, MLA_HEADS, S, MLA_V).transpose(0, 2, 1, 3)
    d_krot = jnp.sum(dk4[..., MLA_NOPE:], axis=2).reshape(T, MLA_ROPE)
    d_o1 = jnp.concatenate([dq4[..., MLA_NOPE:MLA_NOPE + HALF_ROPE].reshape(T, n_half), d_krot[:, :HALF_ROPE], lane_pad], axis=1)
    d_o2 = jnp.concatenate([dq4[..., MLA_NOPE + HALF_ROPE:].reshape(T, n_half), d_krot[:, HALF_ROPE:], lane_pad], axis=1)
    d_x1, d_x2 = _rowwise(_f_rope_t, [d_o1, d_o2, cos, sin], [], [(ROPE_LANES, F32), (ROPE_LANES, F32)], "l1_rope_bwd")
    d_qf = jnp.concatenate([dq4[..., :MLA_NOPE].reshape(T, n_nope), d_x1[:, :n_half], d_x2[:, :n_half]], axis=1)
    d_kr = jnp.concatenate([d_x1[:, n_half:n_half + HALF_ROPE], d_x2[:, n_half:n_half + HALF_ROPE]], axis=1)
    d_kvf = jnp.concatenate([dk4[..., :MLA_NOPE], dv4], axis=-1).reshape(T, MLA_HEADS * (MLA_NOPE + MLA_V))
    d_cqn = _mm(d_qf, W["mla_w_uq"], "nt", F32, "l1_uq_dx")
    G["mla_w_uq"] = _mm(cqn, d_qf, "tn", F32, "l1_uq_dw")
    d_ckvn = _mm(d_kvf, W["mla_w_ukv"], "nt", F32, "l1_ukv_dx")
    G["mla_w_ukv"] = _mm(ckvn, d_kvf, "tn", F32, "l1_ukv_dw")
    d_cq, d_qg = _rowwise_bwd(_f_rms, [cq], [qg], [d_cqn], 1, "l1_qnorm_bwd")
    d_ckv, d_kvg = _rowwise_bwd(_f_rms, [ckv], [kvg], [d_ckvn], 1, "l1_kvnorm_bwd")
    G["mla_q_norm_g"], G["mla_kv_norm_g"] = d_qg.reshape(-1), d_kvg.reshape(-1)
    d_proj1 = jnp.concatenate([d_cq, d_ckv, d_qm1, d_z1, d_kr, jnp.zeros((T, proj1.shape[1] - c4 - MLA_ROPE), F32)], axis=1)
    d_u1 = _mm(d_proj1, W["mla_w_in"], "nt", F32, "l1_in_dx")
    G["mla_w_in"] = _mm(u1, d_proj1, "tn", F32, "l1_in_dw")
    d_h1n, d_g1 = _rowwise_bwd(_f_rms, [h1], [g1], [d_u1], 1, "l1_norm_bwd")
    dh1 = dh2 + d_h1n

    dy0 = _mm(dh1, W["w_out"][0], "nt", F32, "l0_out_dx")
    d_wout0 = _mm(y0, dh1, "tn", F32, "l0_out_dw")
    d_ymain0, d_ymem0, d_z0 = _rowwise_bwd(_f_gate, [ymain0, ymem0, z0], [], [dy0], 3, "l0_gate_bwd", tb=128)
    d_qm0, d_wmem0, d_mg0 = _mem_attention_bwd(d_ymem0, mem_saved0, mem2, mg0, W["w_mem_kv"][0], B, S, "l0")
    d_cv, d_ln_g, d_ln_b = _rowwise_bwd(_f_ln_silu, [cv], [ln_g, ln_b], [d_ymain0], 1, "l0_ln_bwd")
    d_glu, d_dw, d_dwb = _dwconv_bwd(glu.reshape(B, S, conv_w), dw, d_cv.reshape(B, S, conv_w), "l0_dwconv_bwd")
    d_a0, d_gate0 = _rowwise_bwd(_f_glu, [a0, gate0], [], [d_glu.reshape(T, conv_w)], 2, "l0_glu_bwd")
    d_proj0 = jnp.concatenate([d_a0, d_gate0, d_qm0, d_z0], axis=1)
    d_u0 = _mm(d_proj0, W["conv_w_in"], "nt", F32, "l0_in_dx")
    G["conv_w_in"] = _mm(u0, d_proj0, "tn", F32, "l0_in_dw")
    d_h0n, d_g0 = _rowwise_bwd(_f_rms, [h0], [g0], [d_u0], 1, "l0_norm_bwd")
    dx = (dh1 + d_h0n).reshape(B, S, D)

    G["norm_g"] = jnp.concatenate([d_g0, d_g1], axis=0)
    G["mem_norm_g"] = jnp.concatenate([d_mg0, d_mg1], axis=0)
    G["w_mem_kv"] = jnp.stack([d_wmem0, d_wmem1])
    G["w_out"] = jnp.stack([d_wout0, d_wout1])
    G["conv_dw"], G["conv_dw_b"] = d_dw, d_dwb
    G["conv_ln_g"], G["conv_ln_b"] = d_ln_g, d_ln_b
    return loss128[0, 0], dx, G


def _mla_in_perm(w):
    c2 = Q_RANK + KV_RANK
    pad = (-w.shape[1]) % 128
    return jnp.concatenate([w[:, :c2], w[:, c2 + MLA_ROPE:], w[:, c2:c2 + MLA_ROPE],
                            jnp.zeros((w.shape[0], pad), w.dtype)], axis=1)


def _mla_in_unperm(g, cols):
    c2 = Q_RANK + KV_RANK
    return jnp.concatenate([g[:, :c2], g[:, cols - MLA_ROPE:cols], g[:, c2:cols - MLA_ROPE]], axis=1)


def _uq_perm(w):
    w3 = w.reshape(w.shape[0], MLA_HEADS, MLA_QK)
    return jnp.concatenate([w3[:, :, :MLA_NOPE].reshape(w.shape[0], -1),
                            w3[:, :, MLA_NOPE:MLA_NOPE + HALF_ROPE].reshape(w.shape[0], -1),
                            w3[:, :, MLA_NOPE + HALF_ROPE:].reshape(w.shape[0], -1)], axis=1)


def _uq_unperm(g):
    r = g.shape[0]
    n_nope, n_half = MLA_HEADS * MLA_NOPE, MLA_HEADS * HALF_ROPE
    return jnp.concatenate([g[:, :n_nope].reshape(r, MLA_HEADS, MLA_NOPE),
                            g[:, n_nope:n_nope + n_half].reshape(r, MLA_HEADS, HALF_ROPE),
                            g[:, n_nope + n_half:].reshape(r, MLA_HEADS, HALF_ROPE)], axis=2).reshape(r, -1)


_BIG = (("w_mem_kv", "rows"), ("w_out", "rows"), ("conv_w_in", "cols"), ("mla_w_in", "cols"),
        ("mla_w_uq", "cols"), ("mla_w_ukv", "cols"))
_SMALL_SHARDED = ("conv_dw", "mla_q_norm_g", "mla_kv_norm_g")
_REPLICATED = ("norm_g", "mem_norm_g", "conv_dw_b", "conv_ln_g", "conv_ln_b", "final_norm_g")


def _pad_rows(a, rows):
    return jnp.concatenate([a, jnp.zeros((rows - a.shape[0],) + a.shape[1:], a.dtype)], axis=0)


def _flat_pad(parts, size):
    flat = jnp.concatenate([p.reshape(-1) for p in parts])
    return jnp.concatenate([flat, jnp.zeros((size - flat.shape[0],), flat.dtype)])


def _pack_layout(shards):
    big_rows = [shards[n].size // PACK_COLS for n, _ in _BIG]
    small = sum(shards[n].size for n in _SMALL_SHARDED)
    small_rows = -(-small // PACK_COLS)
    total = sum(big_rows) + small_rows
    return big_rows, small_rows, -(-total // PACK_ROW_TILE) * PACK_ROW_TILE


def _pack_shards(shards):
    _, small_rows, rows = _pack_layout(shards)
    parts = [shards[n].reshape(-1, PACK_COLS) for n, _ in _BIG]
    parts.append(_flat_pad([shards[n] for n in _SMALL_SHARDED], small_rows * PACK_COLS).reshape(small_rows, PACK_COLS))
    return _pad_rows(jnp.concatenate(parts, axis=0), rows)


def _unpack_shards(slab, like):
    big_rows, small_rows, _ = _pack_layout(like)
    out, r = {}, 0
    for (n, _), nr in zip(_BIG, big_rows):
        out[n] = slab[r:r + nr].reshape(like[n].shape)
        r += nr
    flat, o = slab[r:r + small_rows].reshape(-1), 0
    for n in _SMALL_SHARDED:
        out[n] = flat[o:o + like[n].size].reshape(like[n].shape)
        o += like[n].size
    return out


def _unpack_weights(gathered, small_gathered, shards):
    big_rows, _, _ = _pack_layout(shards)
    W, r = {}, 0
    for (n, cut), nr in zip(_BIG, big_rows):
        blk = gathered[:, r:r + nr]
        r += nr
        s = shards[n].shape
        if cut == "rows":
            W[n] = blk.reshape(N_DEV, s[0], s[1], s[2]).transpose(1, 0, 2, 3).reshape(s[0], N_DEV * s[1], s[2])
        else:
            W[n] = blk.reshape(N_DEV, s[1], s[2]).transpose(1, 0, 2).reshape(s[1], N_DEV * s[2])
    o = 0
    for n in _SMALL_SHARDED:
        s = shards[n].shape
        piece = small_gathered[:, o:o + shards[n].size]
        o += shards[n].size
        if n == "conv_dw":
            W[n] = piece.reshape(N_DEV, s[1], s[2]).transpose(1, 0, 2).reshape(s[1], N_DEV * s[2])
        else:
            W[n] = piece.reshape(-1)
    W["mla_w_in"] = _mla_in_perm(W["mla_w_in"])
    W["mla_w_uq"] = _uq_perm(W["mla_w_uq"])
    return W


def _pack_grads(G, shards):
    _, small_rows, rows = _pack_layout(shards)
    G = dict(G)
    G["mla_w_in"] = _mla_in_unperm(G["mla_w_in"], shards["mla_w_in"].shape[2] * N_DEV)
    G["mla_w_uq"] = _uq_unperm(G["mla_w_uq"])
    parts = []
    for n, cut in _BIG:
        s = shards[n].shape
        if cut == "rows":
            blk = G[n].reshape(s[0], N_DEV, s[1], s[2]).transpose(1, 0, 2, 3)
        else:
            blk = G[n].reshape(s[1], N_DEV, s[2]).transpose(1, 0, 2)
        parts.append(blk.reshape(N_DEV, -1, PACK_COLS))
    s = shards["conv_dw"].shape
    small = [G["conv_dw"].reshape(s[1], N_DEV, s[2]).transpose(1, 0, 2).reshape(N_DEV, -1),
             G["mla_q_norm_g"].reshape(N_DEV, -1), G["mla_kv_norm_g"].reshape(N_DEV, -1)]
    small = jnp.concatenate(small, axis=1)
    small = jnp.concatenate([small, jnp.zeros((N_DEV, small_rows * PACK_COLS - small.shape[1]), F32)], axis=1)
    parts.append(small.reshape(N_DEV, small_rows, PACK_COLS))
    packed = jnp.concatenate(parts, axis=1)
    return jnp.concatenate([packed, jnp.zeros((N_DEV, rows - packed.shape[1], PACK_COLS), F32)], axis=1)


_HBM = pl.BlockSpec(memory_space=pltpu.HBM)
_VMEM = pl.BlockSpec(memory_space=pltpu.VMEM)


def _position():
    return lax.axis_index("x"), lax.axis_index("y"), lax.axis_index("c")


def _all_gather_slabs(slab, name):
    rows, cols = slab.shape

    def body(x_ref, out_ref, send_sems, recv_sems, local_sem):
        x, y, c = _position()
        me, sibling = (x, y, c), (x, y, 1 - c)
        chips = [(1 - x, y), (x, 1 - y), (1 - x, 1 - y)]

        def slot(px, py, pc):
            return out_ref.at[4 * px + 2 * py + pc]

        def copy(k, block, to, src=None):
            return pltpu.make_async_remote_copy(src_ref=slot(*block) if src is None else src, dst_ref=slot(*block),
                                                send_sem=send_sems.at[k], recv_sem=recv_sems.at[k],
                                                device_id=to, device_id_type=MESH)

        mine = pltpu.make_async_copy(x_ref, slot(*me), local_sem)
        mine.start()
        first = [copy(0, me, sibling, src=x_ref)]
        first += [copy(1 + j, me, (*chip, c), src=x_ref) for j, chip in enumerate(chips)]
        for cp in first:
            cp.start()
        passed = [copy(4 + j, (*chip, c), sibling) for j, chip in enumerate(chips)]
        for j, chip in enumerate(chips):
            copy(1 + j, (*chip, c), me).wait_recv()
            passed[j].start()
        copy(0, sibling, me).wait_recv()
        for j, chip in enumerate(chips):
            copy(4 + j, (*chip, 1 - c), me).wait_recv()
        for cp in first + passed:
            cp.wait_send()
        mine.wait()

    return _call(body, name, jax.ShapeDtypeStruct((N_DEV, rows, cols), slab.dtype), in_specs=[_HBM], out_specs=_HBM,
                 scratch=[pltpu.SemaphoreType.DMA((7,)), pltpu.SemaphoreType.DMA((7,)), pltpu.SemaphoreType.DMA(())])(slab)


def _all_gather_small(v, name):
    r, n = v.shape

    def body(x_ref, out_ref, send_sems, recv_sems, local_sem):
        x, y, c = _position()
        me = 4 * x + 2 * y + c
        mine = pltpu.make_async_copy(x_ref, out_ref.at[me], local_sem)
        mine.start()
        flips = [(fx, fy, fc) for fx in (0, 1) for fy in (0, 1) for fc in (0, 1)][1:]
        copies = []
        for k, (fx, fy, fc) in enumerate(flips):
            peer = (x ^ fx, y ^ fy, c ^ fc)
            cp = pltpu.make_async_remote_copy(src_ref=x_ref, dst_ref=out_ref.at[me], send_sem=send_sems.at[k],
                                              recv_sem=recv_sems.at[k], device_id=peer, device_id_type=MESH)
            cp.start()
            copies.append(cp)
        for k, (fx, fy, fc) in enumerate(flips):
            px, py, pc = x ^ fx, y ^ fy, c ^ fc
            src = out_ref.at[4 * px + 2 * py + pc]
            pltpu.make_async_remote_copy(src_ref=x_ref, dst_ref=src, send_sem=send_sems.at[k], recv_sem=recv_sems.at[k],
                                         device_id=(px, py, pc), device_id_type=MESH).wait_recv()
        for cp in copies:
            cp.wait_send()
        mine.wait()

    return _call(body, name, jax.ShapeDtypeStruct((N_DEV, r, n), v.dtype), in_specs=[_VMEM], out_specs=_VMEM,
                 scratch=[pltpu.SemaphoreType.DMA((7,)), pltpu.SemaphoreType.DMA((7,)), pltpu.SemaphoreType.DMA(())])(v)


def _sibling_exchange(g, name):
    _, rows, cols = g.shape

    def body(g_ref, out_ref, send_sems, recv_sems):
        x, y, c = _position()
        copies = []
        for k in range(4):
            cp = pltpu.make_async_remote_copy(src_ref=g_ref.at[2 * k + (1 - c)], dst_ref=out_ref.at[k],
                                              send_sem=send_sems.at[k], recv_sem=recv_sems.at[k],
                                              device_id=(x, y, 1 - c), device_id_type=MESH)
            cp.start()
            copies.append(cp)
        for cp in copies:
            cp.wait()

    return _call(body, name, jax.ShapeDtypeStruct((4, rows, cols), g.dtype), in_specs=[_HBM], out_specs=_HBM,
                 scratch=[pltpu.SemaphoreType.DMA((4,)), pltpu.SemaphoreType.DMA((4,))])(g)


def _add_own(g, recv, name):
    _, rows, cols = g.shape
    tr = PACK_ROW_TILE
    c = lax.axis_index("c").astype(jnp.int32).reshape(1)

    def body(c_ref, g_ref, r_ref, o_ref):
        o_ref[...] = g_ref[...] + r_ref[...]

    grid_spec = pltpu.PrefetchScalarGridSpec(
        num_scalar_prefetch=1, grid=(4, rows // tr),
        in_specs=[pl.BlockSpec((None, None, tr, cols), lambda k, i, c_ref: (k, c_ref[0], i, 0)),
                  pl.BlockSpec((None, tr, cols), lambda k, i, c_ref: (k, i, 0))],
        out_specs=pl.BlockSpec((None, tr, cols), lambda k, i, c_ref: (k, i, 0)))
    return _call(body, name, jax.ShapeDtypeStruct((4, rows, cols), F32), grid_spec=grid_spec,
                 dims=("parallel", "parallel"))(c, g.reshape(4, 2, rows, cols), recv)


def _chip_exchange(pa, name):
    _, rows, cols = pa.shape

    def body(pa_ref, out_ref, send_sems, recv_sems, local_sem):
        x, y, c = _position()
        my_chip = 2 * x + y
        chips = [(1 - x, y), (x, 1 - y), (1 - x, 1 - y)]
        mine = pltpu.make_async_copy(pa_ref.at[my_chip], out_ref.at[my_chip], local_sem)
        mine.start()
        copies = []
        for j, (px, py) in enumerate(chips):
            cp = pltpu.make_async_remote_copy(src_ref=pa_ref.at[2 * px + py], dst_ref=out_ref.at[my_chip],
                                              send_sem=send_sems.at[j], recv_sem=recv_sems.at[j],
                                              device_id=(px, py, c), device_id_type=MESH)
            cp.start()
            copies.append(cp)
        for j, (px, py) in enumerate(chips):
            pltpu.make_async_remote_copy(src_ref=pa_ref.at[2 * px + py], dst_ref=out_ref.at[2 * px + py],
                                         send_sem=send_sems.at[j], recv_sem=recv_sems.at[j],
                                         device_id=(px, py, c), device_id_type=MESH).wait_recv()
        for cp in copies:
            cp.wait_send()
        mine.wait()

    return _call(body, name, jax.ShapeDtypeStruct((4, rows, cols), pa.dtype), in_specs=[_HBM], out_specs=_HBM,
                 scratch=[pltpu.SemaphoreType.DMA((3,)), pltpu.SemaphoreType.DMA((3,)), pltpu.SemaphoreType.DMA(())])(pa)


def _adamw_math(w, g, m, v):
    m = ADAM_B1 * m + (1.0 - ADAM_B1) * g
    v = ADAM_B2 * v + (1.0 - ADAM_B2) * (g * g)
    m_hat = m / (1.0 - ADAM_B1 ** ADAM_STEP)
    v_hat = v / (1.0 - ADAM_B2 ** ADAM_STEP)
    delta = -ADAM_LR * (m_hat / (jnp.sqrt(v_hat) + ADAM_EPS) + ADAM_WD * w)
    return delta, m, v


def _sum_adamw(parts, w, m, v, name):
    n, rows, cols = parts.shape
    tr = _pick(rows, PACK_ROW_TILE, 8)

    def body(p_ref, w_ref, m_ref, v_ref, g_ref, d_ref, nm_ref, nv_ref):
        g = p_ref[0]
        for k in range(1, n):
            g = g + p_ref[k]
        d, nm, nv = _adamw_math(w_ref[...], g, m_ref[...], v_ref[...])
        g_ref[...], d_ref[...], nm_ref[...], nv_ref[...] = g, d, nm, nv

    blk = pl.BlockSpec((tr, cols), lambda i: (i, 0))
    return _call(body, name, [jax.ShapeDtypeStruct((rows, cols), F32)] * 4, grid=(rows // tr,),
                 in_specs=[pl.BlockSpec((n, tr, cols), lambda i: (0, i, 0)), blk, blk, blk],
                 out_specs=[blk] * 4, dims=("parallel",))(parts, w, m, v)


_WEIGHTS = ("norm_g", "mem_norm_g", "w_mem_kv", "w_out", "conv_w_in", "conv_dw", "conv_dw_b", "conv_ln_g", "conv_ln_b",
            "mla_w_in", "mla_q_norm_g", "mla_w_uq", "mla_kv_norm_g", "mla_w_ukv", "final_norm_g")
SMALL_LANES = 128 * 8


def _as_tiles(flat_parts):
    total = sum(p.size for p in flat_parts)
    size = -(-total // SMALL_LANES) * SMALL_LANES
    return _flat_pad(flat_parts, size).reshape(8, size // 8)


def _split_flat(flat, like):
    out, o = [], 0
    for a in like:
        out.append(flat[o:o + a.size].reshape(a.shape))
        o += a.size
    return out


def kernel(x, mem, positions, norm_g, mem_norm_g, w_mem_kv, w_out, conv_w_in, conv_dw, conv_dw_b, conv_ln_g, conv_ln_b, mla_w_in, mla_q_norm_g, mla_w_uq, mla_kv_norm_g, mla_w_ukv, final_norm_g, loss_target, m_norm_g, m_mem_norm_g, m_w_mem_kv, m_w_out, m_conv_w_in, m_conv_dw, m_conv_dw_b, m_conv_ln_g, m_conv_ln_b, m_mla_w_in, m_mla_q_norm_g, m_mla_w_uq, m_mla_kv_norm_g, m_mla_w_ukv, m_final_norm_g, v_norm_g, v_mem_norm_g, v_w_mem_kv, v_w_out, v_conv_w_in, v_conv_dw, v_conv_dw_b, v_conv_ln_g, v_conv_ln_b, v_mla_w_in, v_mla_q_norm_g, v_mla_w_uq, v_mla_kv_norm_g, v_mla_w_ukv, v_final_norm_g):
    w = dict(zip(_WEIGHTS, (norm_g, mem_norm_g, w_mem_kv, w_out, conv_w_in, conv_dw, conv_dw_b, conv_ln_g, conv_ln_b,
                            mla_w_in, mla_q_norm_g, mla_w_uq, mla_kv_norm_g, mla_w_ukv, final_norm_g)))
    m = dict(zip(_WEIGHTS, (m_norm_g, m_mem_norm_g, m_w_mem_kv, m_w_out, m_conv_w_in, m_conv_dw, m_conv_dw_b, m_conv_ln_g,
                            m_conv_ln_b, m_mla_w_in, m_mla_q_norm_g, m_mla_w_uq, m_mla_kv_norm_g, m_mla_w_ukv, m_final_norm_g)))
    v = dict(zip(_WEIGHTS, (v_norm_g, v_mem_norm_g, v_w_mem_kv, v_w_out, v_conv_w_in, v_conv_dw, v_conv_dw_b, v_conv_ln_g,
                            v_conv_ln_b, v_mla_w_in, v_mla_q_norm_g, v_mla_w_uq, v_mla_kv_norm_g, v_mla_w_ukv, v_final_norm_g)))
    sharded = [n for n, _ in _BIG] + list(_SMALL_SHARDED)
    shards = {n: w[n] for n in sharded}

    w_slab = _pack_shards(shards)
    gathered = _all_gather_slabs(w_slab.astype(BF16), "gather_weights")
    small_mine = _as_tiles([w[n] for n in _SMALL_SHARDED])
    small_all = _all_gather_small(small_mine, "gather_small_weights").reshape(N_DEV, -1)
    W = _unpack_weights(gathered, small_all, shards)
    for n in _REPLICATED:
        W[n] = w[n]

    loss_local, dx, G = _forward_backward(x, mem, positions, loss_target, W)
    loss = lax.psum(loss_local, ("x", "y", "c"))

    g_all = _pack_grads(G, shards)
    from_sibling = _sibling_exchange(g_all, "reduce_sibling")
    chip_partial = _add_own(g_all, from_sibling, "reduce_sibling_add")
    from_chips = _chip_exchange(chip_partial, "reduce_chips")
    g_slab, d_slab, nm_slab, nv_slab = _sum_adamw(
        from_chips, w_slab, _pack_shards({n: m[n] for n in sharded}), _pack_shards({n: v[n] for n in sharded}), "adamw_sharded")
    out_g, out_d = _unpack_shards(g_slab, shards), _unpack_shards(d_slab, shards)
    out_m, out_v = _unpack_shards(nm_slab, shards), _unpack_shards(nv_slab, shards)

    rep_like = [w[n] for n in _REPLICATED]
    rep_parts = _all_gather_small(_as_tiles([G[n] for n in _REPLICATED]), "gather_replicated_grads")
    rep = _sum_adamw(rep_parts, _as_tiles(rep_like), _as_tiles([m[n] for n in _REPLICATED]),
                     _as_tiles([v[n] for n in _REPLICATED]), "adamw_replicated")
    for res, out in zip(rep, (out_g, out_d, out_m, out_v)):
        for n, a in zip(_REPLICATED, _split_flat(res.reshape(-1), rep_like)):
            out[n] = a

    return (loss, dx, *[out_g[n] for n in _WEIGHTS], *[out_d[n] for n in _WEIGHTS],
            *[out_m[n] for n in _WEIGHTS], *[out_v[n] for n in _WEIGHTS])
```

```python
import jax
import jax.numpy as jnp
from jax import lax
from jax.experimental import pallas as pl
from jax.experimental.pallas import tpu as pltpu

F32 = jnp.float32
BF16 = jnp.bfloat16
MESH = pl.DeviceIdType.MESH
N_DEV = 8
VMEM_LIMIT_BYTES = 48 * 1024 * 1024

MEM_HEADS, MEM_HEAD_DIM = 4, 128
MEM_WIDTH = MEM_HEADS * MEM_HEAD_DIM
CONV_KERNEL = 31
CONV_PAD = 32
MLA_HEADS, MLA_NOPE, MLA_ROPE, MLA_V = 12, 128, 64, 128
MLA_QK = MLA_NOPE + MLA_ROPE
HALF_ROPE = MLA_ROPE // 2
Q_RANK, KV_RANK = 512, 256
ROPE_THETA = 10000.0
RMS_EPS = 1e-6
LN_EPS = 1e-5
ADAM_LR, ADAM_B1, ADAM_B2, ADAM_EPS, ADAM_WD, ADAM_STEP = 0.001, 0.9, 0.999, 1e-08, 0.01, 10
NEG = -1e30


def _call(body, name, out_shape, grid=None, in_specs=None, out_specs=None, scratch=(), dims=None, grid_spec=None, aliases=None):
    params = dict(vmem_limit_bytes=VMEM_LIMIT_BYTES)
    if dims is not None:
        params["dimension_semantics"] = dims
    kw = {}
    if aliases:
        kw["input_output_aliases"] = aliases
    if grid_spec is not None:
        kw["grid_spec"] = grid_spec
    else:
        if grid is not None:
            kw["grid"] = grid
        kw["in_specs"] = in_specs
        kw["out_specs"] = out_specs
        kw["scratch_shapes"] = list(scratch)
    return pl.pallas_call(body, name=name, out_shape=out_shape, compiler_params=pltpu.CompilerParams(**params), **kw)


def _pick(n, target, mult):
    best = None
    for d in range(mult, min(n, target) + 1, mult):
        if n % d == 0:
            best = d
    return n if best is None else best


_DOT_DIMS = {"nn": (((1,), (0,)), ((), ())), "nt": (((1,), (1,)), ((), ())), "tn": (((0,), (0,)), ((), ()))}


def _mm(a, b, mode, out_dtype, name, res=None):
    if mode == "nn":
        (M, K), N = a.shape, b.shape[1]
    elif mode == "nt":
        (M, K), N = a.shape, b.shape[0]
    else:
        (K, M), N = a.shape, b.shape[1]
    tm = _pick(M, 1024 if mode != "tn" else 512, 8)
    tn = _pick(N, 512, 128)
    tk = _pick(K, 1024, 128)
    nk = K // tk
    has_res = res is not None

    def body(*refs):
        if has_res:
            a_ref, b_ref, r_ref, o_ref, acc = refs
        else:
            a_ref, b_ref, o_ref, acc = refs
        k = pl.program_id(2)

        @pl.when(k == 0)
        def _():
            acc[...] = jnp.zeros_like(acc)

        acc[...] += lax.dot_general(a_ref[...].astype(BF16), b_ref[...].astype(BF16), _DOT_DIMS[mode],
                                    preferred_element_type=F32)

        @pl.when(k == nk - 1)
        def _():
            r = acc[...]
            if has_res:
                r = r + r_ref[...]
            o_ref[...] = r.astype(o_ref.dtype)

    a_spec = {"nn": pl.BlockSpec((tm, tk), lambda i, j, k: (i, k)),
              "nt": pl.BlockSpec((tm, tk), lambda i, j, k: (i, k)),
              "tn": pl.BlockSpec((tk, tm), lambda i, j, k: (k, i))}[mode]
    b_spec = {"nn": pl.BlockSpec((tk, tn), lambda i, j, k: (k, j)),
              "nt": pl.BlockSpec((tn, tk), lambda i, j, k: (j, k)),
              "tn": pl.BlockSpec((tk, tn), lambda i, j, k: (k, j))}[mode]
    o_spec = pl.BlockSpec((tm, tn), lambda i, j, k: (i, j))
    in_specs = [a_spec, b_spec] + ([o_spec] if has_res else [])
    args = (a, b) + ((res,) if has_res else ())
    return _call(body, name, jax.ShapeDtypeStruct((M, N), out_dtype), grid=(M // tm, N // tn, nk),
                 in_specs=in_specs, out_specs=o_spec, scratch=[pltpu.VMEM((tm, tn), F32)],
                 dims=("parallel", "parallel", "arbitrary"))(*args)


def _views(rows):
    return [r if isinstance(r, tuple) else (r, r.shape[1], 0) for r in rows]


def _rowwise(f, rows, params, outs, name, tb=256):
    rows = _views(rows)
    T = rows[0][0].shape[0]
    tb = min(tb, T)
    nr, npar = len(rows), len(params)
    outs = [o if len(o) == 3 else (o[0], o[1], o[0]) for o in outs]

    def body(*refs):
        vals = f(*[r[...].astype(F32) for r in refs[:nr]], *[p[...] for p in refs[nr:nr + npar]])
        for o_ref, v in zip(refs[nr + npar:], vals):
            o_ref[...] = v.astype(o_ref.dtype)

    row_spec = lambda w, cb=0: pl.BlockSpec((tb, w), lambda i: (i, cb))
    par_spec = lambda w: pl.BlockSpec((1, w), lambda i: (0, 0))
    res = _call(body, name, [jax.ShapeDtypeStruct((T, tw), dt) for _, dt, tw in outs], grid=(T // tb,),
                in_specs=[row_spec(w, cb) for _, w, cb in rows] + [par_spec(p.shape[1]) for p in params],
                out_specs=[row_spec(w) for w, _, _ in outs], dims=("parallel",))(*[r[0] for r in rows], *params)
    return res


def _rowwise_bwd(f, rows, params, douts, n_diff, name, tb=256):
    rows, douts = _views(rows), _views(douts)
    T = rows[0][0].shape[0]
    tb = min(tb, T)
    nr, npar, nd = len(rows), len(params), len(douts)

    def body(*refs):
        rv = [r[...].astype(F32) for r in refs[:nr]]
        pv = [p[...] for p in refs[nr:nr + npar]]
        dv = [d[...].astype(F32) for d in refs[nr + npar:nr + npar + nd]]
        o_refs = refs[nr + npar + nd:]
        fixed = rv[n_diff:]

        def g(*xs):
            return tuple(f(*xs[:n_diff], *fixed, *xs[n_diff:]))

        _, vjp = jax.vjp(g, *rv[:n_diff], *pv)
        grads = vjp(tuple(dv))
        for o_ref, gr in zip(o_refs[:n_diff], grads[:n_diff]):
            o_ref[...] = gr.astype(o_ref.dtype)
        first = pl.program_id(0) == 0
        for o_ref, gr in zip(o_refs[n_diff:], grads[n_diff:]):
            @pl.when(first)
            def _(o_ref=o_ref):
                o_ref[...] = jnp.zeros_like(o_ref)

            o_ref[...] += gr

    row_spec = lambda w, cb=0: pl.BlockSpec((tb, w), lambda i: (i, cb))
    par_spec = lambda w: pl.BlockSpec((1, w), lambda i: (0, 0))
    out_shape = ([jax.ShapeDtypeStruct((T, w), F32) for _, w, _ in rows[:n_diff]]
                 + [jax.ShapeDtypeStruct((1, p.shape[1]), F32) for p in params])
    return _call(body, name, out_shape, grid=(T // tb,),
                 in_specs=([row_spec(w, cb) for _, w, cb in rows] + [par_spec(p.shape[1]) for p in params]
                           + [row_spec(w, cb) for _, w, cb in douts]),
                 out_specs=([row_spec(w) for _, w, _ in rows[:n_diff]] + [par_spec(p.shape[1]) for p in params]),
                 dims=("arbitrary",))(*[r[0] for r in rows], *params, *[d[0] for d in douts])


def _sig(x):
    return 1.0 / (1.0 + jnp.exp(-x))


def _rms(x, g):
    return x * lax.rsqrt(jnp.mean(x * x, axis=-1, keepdims=True) + RMS_EPS) * g


def _f_rms(x, g):
    return (_rms(x, g),)


def _f_glu(a, gate):
    return (a * _sig(gate),)


def _f_ln_silu(x, g, b):
    mu = jnp.mean(x, axis=-1, keepdims=True)
    xc = x - mu
    var = jnp.mean(xc * xc, axis=-1, keepdims=True)
    y = xc * lax.rsqrt(var + LN_EPS) * g + b
    return (y * _sig(y),)


def _rope128(x, cos_p, sin_p):
    return x * cos_p + pltpu.roll(x, 64, 1) * sin_p


def _rope128_t(d, cos_p, sin_p):
    return d * cos_p + pltpu.roll(d * sin_p, 64, 1)


def _f_rope(xq, xk, cos_p, sin_p):
    heads = [_rope128(xq[:, h * 128:(h + 1) * 128], cos_p, sin_p) for h in range(MLA_HEADS)]
    return (jnp.concatenate(heads, axis=1), _rope128(xk, cos_p, sin_p))


def _f_rope_t(dq, dk_heads, cos_p, sin_p):
    heads = [_rope128_t(dq[:, h * 128:(h + 1) * 128], cos_p, sin_p) for h in range(MLA_HEADS)]
    dk = dk_heads[:, 0:128]
    for h in range(1, MLA_HEADS):
        dk = dk + dk_heads[:, h * 128:(h + 1) * 128]
    return (jnp.concatenate(heads, axis=1), _rope128_t(dk, cos_p, sin_p))


GATE_LANES = 256


def _gate_fwd(ycat, proj, z_col, name, tb=512):
    T, width = ycat.shape
    zb = z_col // GATE_LANES

    def body(y_ref, z_ref, o_ref):
        z = z_ref[...]
        o_ref[...] = (y_ref[...] * (z * _sig(z))).astype(o_ref.dtype)

    blk = pl.BlockSpec((tb, GATE_LANES), lambda i, c: (i, c))
    return _call(body, name, jax.ShapeDtypeStruct((T, width), BF16), grid=(T // tb, width // GATE_LANES),
                 in_specs=[blk, pl.BlockSpec((tb, GATE_LANES), lambda i, c: (i, zb + c))], out_specs=blk,
                 dims=("parallel", "parallel"))(ycat, proj)


def _gate_bwd(ycat, proj, z_col, dy, name, tb=512):
    T, width = ycat.shape
    zb = z_col // GATE_LANES

    def body(y_ref, z_ref, dy_ref, dycat_ref, dz_ref):
        z, d = z_ref[...], dy_ref[...]
        s = _sig(z)
        dycat_ref[...] = d * (z * s)
        dz_ref[...] = d * y_ref[...] * (s * (1.0 + z * (1.0 - s)))

    blk = pl.BlockSpec((tb, GATE_LANES), lambda i, c: (i, c))
    return _call(body, name, [jax.ShapeDtypeStruct((T, width), F32)] * 2, grid=(T // tb, width // GATE_LANES),
                 in_specs=[blk, pl.BlockSpec((tb, GATE_LANES), lambda i, c: (i, zb + c)), blk], out_specs=[blk, blk],
                 dims=("parallel", "parallel"))(ycat, proj, dy)


def _final_loss(h, tgt, g, name, tb=256):
    T, D = h.shape

    def body(h_ref, t_ref, g_ref, dh_ref, dg_ref, loss_ref):
        tv = t_ref[...]

        def rowloss(hh, gg):
            e = _rms(hh, gg) - tv
            return 0.5 * jnp.mean(e * e, axis=-1, keepdims=True)

        lr, vjp = jax.vjp(rowloss, h_ref[...], g_ref[...])
        dh, dg = vjp(jnp.ones_like(lr))
        dh_ref[...] = dh

        @pl.when(pl.program_id(0) == 0)
        def _():
            dg_ref[...] = jnp.zeros_like(dg_ref)
            loss_ref[...] = jnp.zeros_like(loss_ref)

        dg_ref[...] += dg
        loss_ref[...] += jnp.broadcast_to(jnp.sum(lr, axis=0, keepdims=True), loss_ref.shape)

    row = pl.BlockSpec((tb, D), lambda i: (i, 0))
    par = pl.BlockSpec((1, D), lambda i: (0, 0))
    return _call(body, name,
                 [jax.ShapeDtypeStruct((T, D), F32), jax.ShapeDtypeStruct((1, D), F32), jax.ShapeDtypeStruct((1, 128), F32)],
                 grid=(T // tb,), in_specs=[row, row, par],
                 out_specs=[row, par, pl.BlockSpec((1, 128), lambda i: (0, 0))], dims=("arbitrary",))(h, tgt, g)


CONV_ROWS = 128
CONV_LANES = 256


def _dwconv_fwd(x, w, b, name):
    B, S, C = x.shape
    cb = CONV_LANES
    off = CONV_PAD - (CONV_KERNEL - 1)

    def body(x_ref, w_ref, b_ref, o_ref, pad):
        pad[0:CONV_PAD, :] = jnp.zeros((CONV_PAD, cb), F32)
        pad[CONV_PAD:, :] = x_ref[...]
        for t0 in range(0, S, CONV_ROWS):
            acc = jnp.broadcast_to(b_ref[...], (CONV_ROWS, cb))
            for k in range(CONV_KERNEL):
                acc = acc + w_ref[k:k + 1, :] * pad[t0 + off + k:t0 + off + k + CONV_ROWS, :]
            o_ref[t0:t0 + CONV_ROWS, :] = acc

    return _call(body, name, jax.ShapeDtypeStruct((B, S, C), F32), grid=(B, C // cb),
                 in_specs=[pl.BlockSpec((None, S, cb), lambda i, j: (i, 0, j)),
                           pl.BlockSpec((CONV_KERNEL, cb), lambda i, j: (0, j)),
                           pl.BlockSpec((1, cb), lambda i, j: (0, j))],
                 out_specs=pl.BlockSpec((None, S, cb), lambda i, j: (i, 0, j)),
                 scratch=[pltpu.VMEM((S + CONV_PAD, cb), F32)], dims=("parallel", "parallel"))(x, w, b)


def _dwconv_bwd(x, w, dy, name):
    B, S, C = x.shape
    cb = CONV_LANES
    off = CONV_PAD - (CONV_KERNEL - 1)
    groups = CONV_ROWS // 8

    def body(x_ref, w_ref, dy_ref, dx_ref, dw_ref, db_ref, xpad, dypad, wacc):
        xpad[0:CONV_PAD, :] = jnp.zeros((CONV_PAD, cb), F32)
        xpad[CONV_PAD:, :] = x_ref[...]
        dypad[0:S, :] = dy_ref[...]
        dypad[S:, :] = jnp.zeros((CONV_PAD, cb), F32)
        wacc[...] = jnp.zeros_like(wacc)
        for t0 in range(0, S, CONV_ROWS):
            dyc = dy_ref[t0:t0 + CONV_ROWS, :]
            acc = jnp.zeros((CONV_ROWS, cb), F32)
            for k in range(CONV_KERNEL):
                acc = acc + w_ref[k:k + 1, :] * dypad[t0 + (CONV_KERNEL - 1) - k:t0 + (CONV_KERNEL - 1) - k + CONV_ROWS, :]
                prod = dyc * xpad[t0 + off + k:t0 + off + k + CONV_ROWS, :]
                wacc[k] += jnp.sum(prod.reshape(groups, 8, cb), axis=0)
            wacc[CONV_KERNEL] += jnp.sum(dyc.reshape(groups, 8, cb), axis=0)
            dx_ref[t0:t0 + CONV_ROWS, :] = acc

        @pl.when(pl.program_id(1) == 0)
        def _():
            dw_ref[...] = jnp.zeros_like(dw_ref)
            db_ref[...] = jnp.zeros_like(db_ref)

        for k in range(CONV_KERNEL):
            dw_ref[k:k + 1, :] += jnp.sum(wacc[k], axis=0, keepdims=True)
        db_ref[...] += jnp.sum(wacc[CONV_KERNEL], axis=0, keepdims=True)

    blk = pl.BlockSpec((None, S, cb), lambda j, i: (i, 0, j))
    return _call(body, name,
                 [jax.ShapeDtypeStruct((B, S, C), F32), jax.ShapeDtypeStruct((CONV_KERNEL, C), F32),
                  jax.ShapeDtypeStruct((1, C), F32)],
                 grid=(C // cb, B),
                 in_specs=[blk, pl.BlockSpec((CONV_KERNEL, cb), lambda j, i: (0, j)), blk],
                 out_specs=[blk, pl.BlockSpec((CONV_KERNEL, cb), lambda j, i: (0, j)),
                            pl.BlockSpec((1, cb), lambda j, i: (0, j))],
                 scratch=[pltpu.VMEM((S + CONV_PAD, cb), F32), pltpu.VMEM((S + CONV_PAD, cb), F32),
                          pltpu.VMEM((CONV_KERNEL + 1, 8, cb), F32)],
                 dims=("parallel", "arbitrary"))(x, w, dy)


ATTN_TILE = 512
ATTN_SUB = 256


def _attn_shapes(Sq, Sk, causal):
    tq = min(Sq, ATTN_TILE)
    tk = tq if causal else min(Sk, ATTN_TILE)
    return tq, tk, min(ATTN_SUB, tq)


def _mask(row0, col0, rows, cols):
    r = row0 + lax.broadcasted_iota(jnp.int32, (rows, cols), 0)
    c = col0 + lax.broadcasted_iota(jnp.int32, (rows, cols), 1)
    return c <= r


def _attn_fwd(q, q_c0, qr, k, k_c0, kr, v, v_c0, B, Sq, Sk, H, causal, scale, name, into=None, o_c0=0, o_width=None):
    tq, tk, sub = _attn_shapes(Sq, Sk, causal)
    nq, nk, nsub = Sq // tq, Sk // tk, tq // sub
    rope = qr is not None

    def body(*refs):
        refs = list(refs)
        qn_ref = refs.pop(0)
        qr_ref = refs.pop(0) if rope else None
        kn_ref = refs.pop(0)
        kr_ref = refs.pop(0) if rope else None
        v_ref = refs.pop(0)
        if into is not None:
            refs.pop(0)
        o_ref, lse_ref, m_s, l_s, acc = refs
        qi = pl.program_id(2)
        m_s[...] = jnp.full_like(m_s, NEG)
        l_s[...] = jnp.zeros_like(l_s)
        acc[...] = jnp.zeros_like(acc)
        qs = []
        for r in range(nsub):
            qn = qn_ref[r * sub:(r + 1) * sub, :].astype(BF16)
            qs.append(jnp.concatenate([qn, qr_ref[r * sub:(r + 1) * sub, :]], axis=1) if rope else qn)

        def step(j, masked):
            ks = pl.ds(pl.multiple_of(j * tk, tk), tk)
            kk = jnp.concatenate([kn_ref[ks, :], kr_ref[ks, :]], axis=1) if rope else kn_ref[ks, :]
            vv = v_ref[ks, :]
            for r in range(nsub):
                rows = slice(r * sub, (r + 1) * sub)
                s = lax.dot_general(qs[r], kk, _DOT_DIMS["nt"], preferred_element_type=F32) * scale
                if masked:
                    s = jnp.where(_mask(qi * tq + r * sub, j * tk, sub, tk), s, NEG)
                m_old = m_s[rows, :]
                m_new = jnp.maximum(m_old, jnp.max(s, axis=-1, keepdims=True))
                p = jnp.exp(s - m_new)
                alpha = jnp.exp(m_old - m_new)
                l_s[rows, :] = alpha * l_s[rows, :] + jnp.sum(p, axis=-1, keepdims=True)
                acc[rows, :] = alpha * acc[rows, :] + jnp.dot(p.astype(BF16), vv, preferred_element_type=F32)
                m_s[rows, :] = m_new

        def unmasked(j, carry):
            step(j, False)
            return carry

        if causal:
            lax.fori_loop(0, qi, unmasked, 0)
            step(qi, True)
        else:
            lax.fori_loop(0, nk, unmasked, 0)
        o_ref[...] = (acc[...] / l_s[...]).astype(o_ref.dtype)
        lse_ref[...] = m_s[...] + jnp.log(l_s[...])

    qspec = lambda c0: pl.BlockSpec((tq, 128), lambda b, h, i: (b * nq + i, c0 + h))
    kspec = lambda c0: pl.BlockSpec((Sk, 128), lambda b, h, i: (b, c0 + h))
    in_specs, args = [qspec(q_c0)], [q]
    if rope:
        in_specs.append(qspec(0)); args.append(qr)
    in_specs.append(kspec(k_c0)); args.append(k)
    if rope:
        in_specs.append(pl.BlockSpec((Sk, 128), lambda b, h, i: (b, 0))); args.append(kr)
    in_specs.append(kspec(v_c0)); args.append(v)
    aliases = {}
    if into is not None:
        aliases = {len(args): 0}
        in_specs.append(pl.BlockSpec(memory_space=pl.ANY)); args.append(into)
        o_shape = jax.ShapeDtypeStruct(into.shape, into.dtype)
    else:
        o_shape = jax.ShapeDtypeStruct((B * Sq, o_width), F32)
    return _call(body, name, [o_shape, jax.ShapeDtypeStruct((B * H, Sq, 1), F32)], grid=(B, H, nq), in_specs=in_specs,
                 out_specs=[qspec(o_c0), pl.BlockSpec((None, tq, 1), lambda b, h, i: (b * H + h, i, 0))],
                 scratch=[pltpu.VMEM((tq, 1), F32), pltpu.VMEM((tq, 1), F32), pltpu.VMEM((tq, 128), F32)],
                 dims=("parallel", "parallel", "arbitrary"), aliases=aliases)(*args)


def _attn_bwd(q, q_c0, qr, k, k_c0, kr, v, v_c0, o, do, o_c0, lse, B, Sq, Sk, H, causal, scale, name, dq_width=None):
    tq, tk, sub = _attn_shapes(Sq, Sk, causal)
    nq, nk, nsub = Sq // tq, Sk // tk, tq // sub
    rope = qr is not None
    dk_w = 256 if rope else 128

    def body(*refs):
        refs = list(refs)
        qn_ref = refs.pop(0)
        qr_ref = refs.pop(0) if rope else None
        kn_ref = refs.pop(0)
        kr_ref = refs.pop(0) if rope else None
        v_ref, o_ref, do_ref, lse_ref = refs[:4]
        refs = refs[4:]
        dqn_ref = refs.pop(0)
        dqr_ref = refs.pop(0) if rope else None
        dkn_ref = refs.pop(0)
        dkr_ref = refs.pop(0) if rope else None
        dv_ref, q_s, do_s, dl_s, dq_acc, dk_acc, dv_acc = refs
        kj = pl.program_id(2)

        @pl.when(kj == 0)
        def _():
            qn = qn_ref[...].astype(BF16)
            q_s[...] = jnp.concatenate([qn, qr_ref[...]], axis=1) if rope else qn
            dof = do_ref[...]
            do_s[...] = dof.astype(BF16)
            dl_s[...] = jnp.sum(dof * o_ref[...], axis=-1, keepdims=True)
            dq_acc[...] = jnp.zeros_like(dq_acc)

        kk = jnp.concatenate([kn_ref[...], kr_ref[...]], axis=1) if rope else kn_ref[...]
        vv = v_ref[...]
        dk_acc[...] = jnp.zeros_like(dk_acc)
        dv_acc[...] = jnp.zeros_like(dv_acc)

        def step(i, masked):
            for r in range(nsub):
                rows = pl.ds(pl.multiple_of(i * tq + r * sub, sub), sub)
                qq, dob = q_s[rows, :], do_s[rows, :]
                s = lax.dot_general(qq, kk, _DOT_DIMS["nt"], preferred_element_type=F32) * scale
                if masked:
                    s = jnp.where(_mask(i * tq + r * sub, kj * tk, sub, tk), s, NEG)
                p = jnp.exp(s - lse_ref[rows, :])
                dp = lax.dot_general(dob, vv, _DOT_DIMS["nt"], preferred_element_type=F32)
                ds = (p * (dp - dl_s[rows, :]) * scale).astype(BF16)
                dv_acc[...] += lax.dot_general(p.astype(BF16), dob, _DOT_DIMS["tn"], preferred_element_type=F32)
                dk_acc[...] += lax.dot_general(ds, qq, _DOT_DIMS["tn"], preferred_element_type=F32)
                dq_acc[rows, :] += jnp.dot(ds, kk, preferred_element_type=F32)

        def unmasked(i, carry):
            step(i, False)
            return carry

        if causal:
            step(kj, True)
            lax.fori_loop(kj + 1, nq, unmasked, 0)
        else:
            lax.fori_loop(0, nq, unmasked, 0)
        dkn_ref[...] = dk_acc[:, 0:128]
        if rope:
            dkr_ref[...] = dk_acc[:, 128:256]
        dv_ref[...] = dv_acc[...]

        @pl.when(kj == nk - 1)
        def _():
            dqn_ref[...] = dq_acc[:, 0:128]
            if rope:
                dqr_ref[...] = dq_acc[:, 128:256]

    qspec = lambda c0: pl.BlockSpec((Sq, 128), lambda b, h, j: (b, c0 + h))
    kspec = lambda c0: pl.BlockSpec((tk, 128), lambda b, h, j: (b * nk + j, c0 + h))
    in_specs, args = [qspec(q_c0)], [q]
    if rope:
        in_specs.append(qspec(0)); args.append(qr)
    in_specs.append(kspec(k_c0)); args.append(k)
    if rope:
        in_specs.append(pl.BlockSpec((tk, 128), lambda b, h, j: (b * nk + j, 0))); args.append(kr)
    in_specs += [kspec(v_c0), qspec(o_c0), qspec(o_c0), pl.BlockSpec((None, Sq, 1), lambda b, h, j: (b * H + h, 0, 0))]
    args += [v, o, do, lse]
    q_rows = jax.ShapeDtypeStruct((B * Sq, dq_width or H * 128), F32)
    h_rows_q = jax.ShapeDtypeStruct((B * Sq, H * 128), F32)
    h_rows_k = jax.ShapeDtypeStruct((B * Sk, H * 128), F32)
    out_shape, out_specs = [q_rows], [qspec(0)]
    if rope:
        out_shape.append(h_rows_q); out_specs.append(qspec(0))
    out_shape.append(h_rows_k); out_specs.append(kspec(0))
    if rope:
        out_shape.append(h_rows_k); out_specs.append(kspec(0))
    out_shape.append(h_rows_k); out_specs.append(kspec(0))
    return _call(body, name, out_shape, grid=(B, H, nk), in_specs=in_specs, out_specs=out_specs,
                 scratch=[pltpu.VMEM((Sq, dk_w), BF16), pltpu.VMEM((Sq, 128), BF16), pltpu.VMEM((Sq, 1), F32),
                          pltpu.VMEM((Sq, dk_w), F32), pltpu.VMEM((tk, dk_w), F32), pltpu.VMEM((tk, 128), F32)],
                 dims=("parallel", "parallel", "arbitrary"))(*args)


def _mem_attention_fwd(proj, q_col, ycat, mem2, mem_g, w_mem, B, S, tag):
    M = mem2.shape[0] // B
    (memn,) = _rowwise(_f_rms, [mem2], [mem_g], [(mem2.shape[1], BF16)], tag + "_memnorm")
    kvm = _mm(memn, w_mem, "nn", BF16, tag + "_memkv")
    o_c0 = ycat.shape[1] // 128 - MEM_HEADS
    ycat, lse = _attn_fwd(proj, q_col // 128, None, kvm, 0, None, kvm, MEM_HEADS, B, S, M, MEM_HEADS, False,
                          MEM_HEAD_DIM ** -0.5, tag + "_memattn", into=ycat, o_c0=o_c0)
    return ycat, (memn, kvm, lse)


def _mem_attention_bwd(proj, q_col, ycat, d_ycat, saved, mem2, mem_g, w_mem, B, S, tag):
    memn, kvm, lse = saved
    M = mem2.shape[0] // B
    o_c0 = ycat.shape[1] // 128 - MEM_HEADS
    d_q, d_k, d_v = _attn_bwd(proj, q_col // 128, None, kvm, 0, None, kvm, MEM_HEADS, ycat, d_ycat, o_c0, lse, B, S, M,
                              MEM_HEADS, False, MEM_HEAD_DIM ** -0.5, tag + "_memattn_bwd")
    d_kvm = jnp.concatenate([d_k, d_v], axis=1)
    d_w_mem = _mm(memn, d_kvm, "tn", F32, tag + "_memkv_dw")
    d_memn = _mm(d_kvm, w_mem, "nt", F32, tag + "_memkv_dx")
    _, d_mem_g = _rowwise_bwd(_f_rms, [mem2], [mem_g], [d_memn], 1, tag + "_memnorm_bwd")
    return d_q, d_w_mem, d_mem_g


def _rope_tables(positions):
    inv_freq = 1.0 / (ROPE_THETA ** (jnp.arange(0, MLA_ROPE, 2, dtype=F32) / MLA_ROPE))
    ang = positions.astype(F32).reshape(-1, 1) * inv_freq
    cos, sin, zero = jnp.cos(ang), jnp.sin(ang), jnp.zeros_like(ang)
    return jnp.concatenate([cos, zero, cos, zero], axis=1), jnp.concatenate([-sin, zero, sin, zero], axis=1)


def _forward_backward(x, mem, positions, target, W):
    B, S, D = x.shape
    T = B * S
    conv_w = W["conv_dw"].shape[1]
    mix_w = 2 * D
    h0 = x.reshape(T, D)
    mem2 = mem.reshape(-1, D)
    tgt = target.reshape(T, D)
    row = lambda v: v.reshape(1, -1)
    n_nope = MLA_HEADS * MLA_NOPE

    g0 = row(W["norm_g"][0])
    (u0,) = _rowwise(_f_rms, [h0], [g0], [(D, BF16)], "l0_norm")
    proj0 = _mm(u0, W["conv_w_in"], "nn", F32, "l0_in")
    a0, gate0 = (proj0, conv_w, 0), (proj0, conv_w, 1)
    qm0_col, z0_col = 2 * conv_w, 2 * conv_w + MEM_WIDTH
    (glu,) = _rowwise(_f_glu, [a0, gate0], [], [(conv_w, F32)], "l0_glu")
    dw, dwb = W["conv_dw"], row(W["conv_dw_b"][0])
    cv = _dwconv_fwd(glu.reshape(B, S, conv_w), dw, dwb, "l0_dwconv").reshape(T, conv_w)
    ln_g, ln_b = row(W["conv_ln_g"][0]), row(W["conv_ln_b"][0])
    (ycat0,) = _rowwise(_f_ln_silu, [cv], [ln_g, ln_b], [(conv_w, F32, mix_w)], "l0_ln")
    mg0 = row(W["mem_norm_g"][0])
    ycat0, mem_saved0 = _mem_attention_fwd(proj0, qm0_col, ycat0, mem2, mg0, W["w_mem_kv"][0], B, S, "l0")
    y0 = _gate_fwd(ycat0, proj0, z0_col, "l0_gate")
    h1 = _mm(y0, W["w_out"][0], "nn", F32, "l0_out", res=h0)

    g1 = row(W["norm_g"][1])
    (u1,) = _rowwise(_f_rms, [h1], [g1], [(D, BF16)], "l1_norm")
    proj1 = _mm(u1, W["mla_w_in"], "nn", F32, "l1_in")
    cq, ckv = (proj1, Q_RANK, 0), (proj1, KV_RANK, Q_RANK // KV_RANK)
    qm1_col = Q_RANK + KV_RANK
    z1_col = qm1_col + MEM_WIDTH
    kr_col = z1_col + mix_w
    qg, kvg = row(W["mla_q_norm_g"]), row(W["mla_kv_norm_g"])
    (cqn,) = _rowwise(_f_rms, [cq], [qg], [(Q_RANK, BF16)], "l1_qnorm")
    (ckvn,) = _rowwise(_f_rms, [ckv], [kvg], [(KV_RANK, BF16)], "l1_kvnorm")
    qf = _mm(cqn, W["mla_w_uq"], "nn", F32, "l1_uq")
    kvf = _mm(ckvn, W["mla_w_ukv"], "nn", BF16, "l1_ukv")
    cos_p, sin_p = _rope_tables(positions)
    qr, kr = _rowwise(_f_rope, [(qf, n_nope, 1), (proj1, 128, kr_col // 128), cos_p, sin_p], [],
                      [(n_nope, BF16), (128, BF16)], "l1_rope")
    scale1 = MLA_QK ** -0.5
    ycat1, lse1 = _attn_fwd(qf, 0, qr, kvf, 0, kr, kvf, MLA_HEADS, B, S, S, MLA_HEADS, True, scale1, "l1_attn",
                            o_width=mix_w)
    mg1 = row(W["mem_norm_g"][1])
    ycat1, mem_saved1 = _mem_attention_fwd(proj1, qm1_col, ycat1, mem2, mg1, W["w_mem_kv"][1], B, S, "l1")
    y1 = _gate_fwd(ycat1, proj1, z1_col, "l1_gate")
    h2 = _mm(y1, W["w_out"][1], "nn", F32, "l1_out", res=h1)

    gf = row(W["final_norm_g"])
    dh2, d_gf, loss128 = _final_loss(h2, tgt, gf, "final_loss")
    G = {"final_norm_g": d_gf.reshape(-1)}

    dy1 = _mm(dh2, W["w_out"][1], "nt", F32, "l1_out_dx")
    d_wout1 = _mm(y1, dh2, "tn", F32, "l1_out_dw")
    d_ycat1, d_z1 = _gate_bwd(ycat1, proj1, z1_col, dy1, "l1_gate_bwd")
    d_qm1, d_wmem1, d_mg1 = _mem_attention_bwd(proj1, qm1_col, ycat1, d_ycat1, mem_saved1, mem2, mg1, W["w_mem_kv"][1],
                                               B, S, "l1")
    d_qn, d_qr, d_kn, d_kr_heads, d_v = _attn_bwd(qf, 0, qr, kvf, 0, kr, kvf, MLA_HEADS, ycat1, d_ycat1, 0, lse1, B, S, S,
                                                  MLA_HEADS, True, scale1, "l1_attn_bwd")
    d_xq, d_kr = _rowwise(_f_rope_t, [d_qr, d_kr_heads, cos_p, sin_p], [], [(n_nope, F32), (128, F32)], "l1_rope_bwd")
    d_qf = jnp.concatenate([d_qn, d_xq], axis=1)
    d_kvf = jnp.concatenate([d_kn, d_v], axis=1)
    d_cqn = _mm(d_qf, W["mla_w_uq"], "nt", F32, "l1_uq_dx")
    G["mla_w_uq"] = _mm(cqn, d_qf, "tn", F32, "l1_uq_dw")
    d_ckvn = _mm(d_kvf, W["mla_w_ukv"], "nt", F32, "l1_ukv_dx")
    G["mla_w_ukv"] = _mm(ckvn, d_kvf, "tn", F32, "l1_ukv_dw")
    d_cq, d_qg = _rowwise_bwd(_f_rms, [cq], [qg], [d_cqn], 1, "l1_qnorm_bwd")
    d_ckv, d_kvg = _rowwise_bwd(_f_rms, [ckv], [kvg], [d_ckvn], 1, "l1_kvnorm_bwd")
    G["mla_q_norm_g"], G["mla_kv_norm_g"] = d_qg.reshape(-1), d_kvg.reshape(-1)
    d_proj1 = jnp.concatenate([d_cq, d_ckv, d_qm1, d_z1, d_kr], axis=1)
    d_u1 = _mm(d_proj1, W["mla_w_in"], "nt", F32, "l1_in_dx")
    G["mla_w_in"] = _mm(u1, d_proj1, "tn", F32, "l1_in_dw")
    d_h1n, d_g1 = _rowwise_bwd(_f_rms, [h1], [g1], [d_u1], 1, "l1_norm_bwd")
    dh1 = dh2 + d_h1n

    dy0 = _mm(dh1, W["w_out"][0], "nt", F32, "l0_out_dx")
    d_wout0 = _mm(y0, dh1, "tn", F32, "l0_out_dw")
    d_ycat0, d_z0 = _gate_bwd(ycat0, proj0, z0_col, dy0, "l0_gate_bwd")
    d_qm0, d_wmem0, d_mg0 = _mem_attention_bwd(proj0, qm0_col, ycat0, d_ycat0, mem_saved0, mem2, mg0, W["w_mem_kv"][0],
                                               B, S, "l0")
    d_cv, d_ln_g, d_ln_b = _rowwise_bwd(_f_ln_silu, [cv], [ln_g, ln_b], [(d_ycat0, conv_w, 0)], 1, "l0_ln_bwd")
    d_glu, d_dw, d_dwb = _dwconv_bwd(glu.reshape(B, S, conv_w), dw, d_cv.reshape(B, S, conv_w), "l0_dwconv_bwd")
    d_a0, d_gate0 = _rowwise_bwd(_f_glu, [a0, gate0], [], [d_glu.reshape(T, conv_w)], 2, "l0_glu_bwd")
    d_proj0 = jnp.concatenate([d_a0, d_gate0, d_qm0, d_z0], axis=1)
    d_u0 = _mm(d_proj0, W["conv_w_in"], "nt", F32, "l0_in_dx")
    G["conv_w_in"] = _mm(u0, d_proj0, "tn", F32, "l0_in_dw")
    d_h0n, d_g0 = _rowwise_bwd(_f_rms, [h0], [g0], [d_u0], 1, "l0_norm_bwd")
    dx = (dh1 + d_h0n).reshape(B, S, D)

    G["norm_g"] = jnp.concatenate([d_g0, d_g1], axis=0)
    G["mem_norm_g"] = jnp.concatenate([d_mg0, d_mg1], axis=0)
    G["w_mem_kv"] = jnp.stack([d_wmem0, d_wmem1])
    G["w_out"] = jnp.stack([d_wout0, d_wout1])
    G["conv_dw"], G["conv_dw_b"] = d_dw, d_dwb
    G["conv_ln_g"], G["conv_ln_b"] = d_ln_g, d_ln_b
    return loss128[0, 0], dx, G


def _mla_in_perm(w):
    c2 = Q_RANK + KV_RANK
    zero = jnp.zeros((w.shape[0], HALF_ROPE), w.dtype)
    return jnp.concatenate([w[:, :c2], w[:, c2 + MLA_ROPE:], w[:, c2:c2 + HALF_ROPE], zero,
                            w[:, c2 + HALF_ROPE:c2 + MLA_ROPE], zero], axis=1)


def _mla_in_unperm(g):
    c2 = Q_RANK + KV_RANK
    r = g.shape[1] - 128
    return jnp.concatenate([g[:, :c2], g[:, r:r + HALF_ROPE], g[:, r + 64:r + 64 + HALF_ROPE], g[:, c2:r]], axis=1)


def _uq_perm(w):
    n = w.shape[0]
    w3 = w.reshape(n, MLA_HEADS, MLA_QK)
    zero = jnp.zeros((n, MLA_HEADS, HALF_ROPE), w.dtype)
    rope = jnp.concatenate([w3[:, :, MLA_NOPE:MLA_NOPE + HALF_ROPE], zero, w3[:, :, MLA_NOPE + HALF_ROPE:], zero], axis=2)
    return jnp.concatenate([w3[:, :, :MLA_NOPE].reshape(n, -1), rope.reshape(n, -1)], axis=1)


def _uq_unperm(g):
    n = g.shape[0]
    n_nope = MLA_HEADS * MLA_NOPE
    rope = g[:, n_nope:].reshape(n, MLA_HEADS, 128)
    return jnp.concatenate([g[:, :n_nope].reshape(n, MLA_HEADS, MLA_NOPE), rope[:, :, :HALF_ROPE],
                            rope[:, :, 64:64 + HALF_ROPE]], axis=2).reshape(n, -1)


def _ukv_perm(w):
    w3 = w.reshape(w.shape[0], MLA_HEADS, MLA_NOPE + MLA_V)
    return jnp.concatenate([w3[:, :, :MLA_NOPE].reshape(w.shape[0], -1), w3[:, :, MLA_NOPE:].reshape(w.shape[0], -1)], axis=1)


def _ukv_unperm(g):
    n = g.shape[0]
    half = MLA_HEADS * MLA_NOPE
    return jnp.concatenate([g[:, :half].reshape(n, MLA_HEADS, MLA_NOPE), g[:, half:].reshape(n, MLA_HEADS, MLA_V)],
                           axis=2).reshape(n, -1)


_ROW_CUT = ("w_mem_kv", "w_out")
_COL_CUT = ("conv_w_in", "mla_w_in", "mla_w_uq", "mla_w_ukv", "conv_dw")
_BIG = ("w_mem_kv", "w_out", "conv_w_in", "mla_w_in", "mla_w_uq", "mla_w_ukv")
_SMALL_SHARDED = ("conv_dw", "mla_q_norm_g", "mla_kv_norm_g")
_REPLICATED = ("norm_g", "mem_norm_g", "conv_dw_b", "conv_ln_g", "conv_ln_b", "final_norm_g")
_PERM = {"mla_w_in": (_mla_in_perm, _mla_in_unperm), "mla_w_uq": (_uq_perm, _uq_unperm), "mla_w_ukv": (_ukv_perm, _ukv_unperm)}


def _join(n, blocks):
    if n in _ROW_CUT:
        _, L, r, c = blocks.shape
        return blocks.transpose(1, 0, 2, 3).reshape(L, N_DEV * r, c)
    if n in _COL_CUT:
        _, _, r, c = blocks.shape
        return blocks.reshape(N_DEV, r, c).transpose(1, 0, 2).reshape(r, N_DEV * c)
    return blocks.reshape(-1)


def _cut(n, full, shard_shape):
    if n in _ROW_CUT:
        L, r, c = shard_shape
        return full.reshape(L, N_DEV, r, c).transpose(1, 0, 2, 3)
    if n in _COL_CUT:
        _, r, c = shard_shape
        return full.reshape(r, N_DEV, c).transpose(1, 0, 2).reshape(N_DEV, 1, r, c)
    return full.reshape(N_DEV, 1, -1)


def _flat_pad(parts, size):
    flat = jnp.concatenate([p.reshape(-1) for p in parts])
    return jnp.concatenate([flat, jnp.zeros((size - flat.shape[0],), flat.dtype)])


SMALL_LANES = 128 * 8


def _as_tiles(flat_parts):
    total = sum(p.size for p in flat_parts)
    size = -(-total // SMALL_LANES) * SMALL_LANES
    return _flat_pad(flat_parts, size).reshape(8, size // 8)


def _split_flat(flat, like):
    out, o = [], 0
    for a in like:
        out.append(flat[o:o + a.size].reshape(a.shape))
        o += a.size
    return out


_HBM = pl.BlockSpec(memory_space=pltpu.HBM)
_VMEM = pl.BlockSpec(memory_space=pltpu.VMEM)


def _position():
    return lax.axis_index("x"), lax.axis_index("y"), lax.axis_index("c")


def _dma_sems(n):
    return [pltpu.SemaphoreType.DMA((n,)), pltpu.SemaphoreType.DMA((n,))]


def _all_gather(shards, name):
    n = len(shards)

    def body(*refs):
        x_refs, out_refs = refs[:n], refs[n:2 * n]
        send_sems, recv_sems, local_sems = refs[2 * n:]
        x, y, c = _position()
        me, sibling = (x, y, c), (x, y, 1 - c)
        chips = [(1 - x, y), (x, 1 - y), (1 - x, 1 - y)]

        def copy(a, k, block, to, src=None):
            slot = out_refs[a].at[4 * block[0] + 2 * block[1] + block[2]]
            return pltpu.make_async_remote_copy(src_ref=slot if src is None else src, dst_ref=slot,
                                                send_sem=send_sems.at[7 * a + k], recv_sem=recv_sems.at[7 * a + k],
                                                device_id=to, device_id_type=MESH)

        mine = [pltpu.make_async_copy(x_refs[a], out_refs[a].at[4 * x + 2 * y + c], local_sems.at[a]) for a in range(n)]
        for cp in mine:
            cp.start()
        first = []
        for a in range(n):
            first.append(copy(a, 0, me, sibling, src=x_refs[a]))
            first += [copy(a, 1 + j, me, (*chip, c), src=x_refs[a]) for j, chip in enumerate(chips)]
        for cp in first:
            cp.start()
        passed = []
        for j, chip in enumerate(chips):
            for a in range(n):
                copy(a, 1 + j, (*chip, c), me).wait_recv()
                cp = copy(a, 4 + j, (*chip, c), sibling)
                cp.start()
                passed.append(cp)
        for a in range(n):
            copy(a, 0, sibling, me).wait_recv()
            for j, chip in enumerate(chips):
                copy(a, 4 + j, (*chip, 1 - c), me).wait_recv()
        for cp in first + passed:
            cp.wait_send()
        for cp in mine:
            cp.wait()

    return _call(body, name, [jax.ShapeDtypeStruct((N_DEV,) + a.shape, a.dtype) for a in shards],
                 in_specs=[_HBM] * n, out_specs=[_HBM] * n,
                 scratch=_dma_sems(7 * n) + [pltpu.SemaphoreType.DMA((n,))])(*shards)


def _all_gather_small(v, name):
    r, n = v.shape

    def body(x_ref, out_ref, send_sems, recv_sems, local_sem):
        x, y, c = _position()
        me = 4 * x + 2 * y + c
        mine = pltpu.make_async_copy(x_ref, out_ref.at[me], local_sem)
        mine.start()
        flips = [(fx, fy, fc) for fx in (0, 1) for fy in (0, 1) for fc in (0, 1)][1:]
        copies = []
        for k, (fx, fy, fc) in enumerate(flips):
            peer = (x ^ fx, y ^ fy, c ^ fc)
            cp = pltpu.make_async_remote_copy(src_ref=x_ref, dst_ref=out_ref.at[me], send_sem=send_sems.at[k],
                                              recv_sem=recv_sems.at[k], device_id=peer, device_id_type=MESH)
            cp.start()
            copies.append(cp)
        for k, (fx, fy, fc) in enumerate(flips):
            px, py, pc = x ^ fx, y ^ fy, c ^ fc
            src = out_ref.at[4 * px + 2 * py + pc]
            pltpu.make_async_remote_copy(src_ref=x_ref, dst_ref=src, send_sem=send_sems.at[k], recv_sem=recv_sems.at[k],
                                         device_id=(px, py, pc), device_id_type=MESH).wait_recv()
        for cp in copies:
            cp.wait_send()
        mine.wait()

    return _call(body, name, jax.ShapeDtypeStruct((N_DEV, r, n), v.dtype), in_specs=[_VMEM], out_specs=_VMEM,
                 scratch=_dma_sems(7) + [pltpu.SemaphoreType.DMA(())])(v)


def _sibling_exchange(gs, name):
    n = len(gs)

    def body(*refs):
        g_refs, out_refs = refs[:n], refs[n:2 * n]
        send_sems, recv_sems = refs[2 * n:]
        x, y, c = _position()
        copies = []
        for a in range(n):
            for k in range(4):
                cp = pltpu.make_async_remote_copy(src_ref=g_refs[a].at[2 * k + (1 - c)], dst_ref=out_refs[a].at[k],
                                                  send_sem=send_sems.at[4 * a + k], recv_sem=recv_sems.at[4 * a + k],
                                                  device_id=(x, y, 1 - c), device_id_type=MESH)
                cp.start()
                copies.append(cp)
        for cp in copies:
            cp.wait()

    return _call(body, name, [jax.ShapeDtypeStruct((4,) + g.shape[1:], g.dtype) for g in gs],
                 in_specs=[_HBM] * n, out_specs=[_HBM] * n, scratch=_dma_sems(4 * n))(*gs)


def _rows2d(shape):
    cols = shape[-1]
    rows = 1
    for s in shape[:-1]:
        rows *= s
    return rows, cols


def _add_own(g, recv, name):
    rows, cols = _rows2d(g.shape[1:])
    tr = _pick(rows, 256, 8)
    c = lax.axis_index("c").astype(jnp.int32).reshape(1)

    def body(c_ref, g_ref, r_ref, o_ref):
        o_ref[...] = g_ref[...] + r_ref[...]

    grid_spec = pltpu.PrefetchScalarGridSpec(
        num_scalar_prefetch=1, grid=(4, rows // tr),
        in_specs=[pl.BlockSpec((None, None, tr, cols), lambda k, i, c_ref: (k, c_ref[0], i, 0)),
                  pl.BlockSpec((None, tr, cols), lambda k, i, c_ref: (k, i, 0))],
        out_specs=pl.BlockSpec((None, tr, cols), lambda k, i, c_ref: (k, i, 0)))
    return _call(body, name, jax.ShapeDtypeStruct((4, rows, cols), F32), grid_spec=grid_spec,
                 dims=("parallel", "parallel"))(c, g.reshape(4, 2, rows, cols), recv.reshape(4, rows, cols))


def _chip_exchange(pas, name):
    n = len(pas)

    def body(*refs):
        pa_refs, out_refs = refs[:n], refs[n:2 * n]
        send_sems, recv_sems, local_sems = refs[2 * n:]
        x, y, c = _position()
        my_chip = 2 * x + y
        chips = [(1 - x, y), (x, 1 - y), (1 - x, 1 - y)]
        mine = [pltpu.make_async_copy(pa_refs[a].at[my_chip], out_refs[a].at[my_chip], local_sems.at[a]) for a in range(n)]
        for cp in mine:
            cp.start()
        copies = []
        for a in range(n):
            for j, (px, py) in enumerate(chips):
                cp = pltpu.make_async_remote_copy(src_ref=pa_refs[a].at[2 * px + py], dst_ref=out_refs[a].at[my_chip],
                                                  send_sem=send_sems.at[3 * a + j], recv_sem=recv_sems.at[3 * a + j],
                                                  device_id=(px, py, c), device_id_type=MESH)
                cp.start()
                copies.append(cp)
        for a in range(n):
            for j, (px, py) in enumerate(chips):
                pltpu.make_async_remote_copy(src_ref=pa_refs[a].at[2 * px + py], dst_ref=out_refs[a].at[2 * px + py],
                                             send_sem=send_sems.at[3 * a + j], recv_sem=recv_sems.at[3 * a + j],
                                             device_id=(px, py, c), device_id_type=MESH).wait_recv()
        for cp in copies:
            cp.wait_send()
        for cp in mine:
            cp.wait()

    return _call(body, name, [jax.ShapeDtypeStruct(pa.shape, pa.dtype) for pa in pas],
                 in_specs=[_HBM] * n, out_specs=[_HBM] * n,
                 scratch=_dma_sems(3 * n) + [pltpu.SemaphoreType.DMA((n,))])(*pas)


def _adamw_math(w, g, m, v):
    m = ADAM_B1 * m + (1.0 - ADAM_B1) * g
    v = ADAM_B2 * v + (1.0 - ADAM_B2) * (g * g)
    m_hat = m / (1.0 - ADAM_B1 ** ADAM_STEP)
    v_hat = v / (1.0 - ADAM_B2 ** ADAM_STEP)
    delta = -ADAM_LR * (m_hat / (jnp.sqrt(v_hat) + ADAM_EPS) + ADAM_WD * w)
    return delta, m, v


def _sum_adamw(parts, w, m, v, name):
    n, rows, cols = parts.shape
    tr = _pick(rows, 128, 8)

    def body(p_ref, w_ref, m_ref, v_ref, g_ref, d_ref, nm_ref, nv_ref):
        g = p_ref[0]
        for k in range(1, n):
            g = g + p_ref[k]
        d, nm, nv = _adamw_math(w_ref[...], g, m_ref[...], v_ref[...])
        g_ref[...], d_ref[...], nm_ref[...], nv_ref[...] = g, d, nm, nv

    blk = pl.BlockSpec((tr, cols), lambda i: (i, 0))
    return _call(body, name, [jax.ShapeDtypeStruct((rows, cols), F32)] * 4, grid=(rows // tr,),
                 in_specs=[pl.BlockSpec((n, tr, cols), lambda i: (0, i, 0)), blk, blk, blk],
                 out_specs=[blk] * 4, dims=("parallel",))(parts, w, m, v)


_WEIGHTS = ("norm_g", "mem_norm_g", "w_mem_kv", "w_out", "conv_w_in", "conv_dw", "conv_dw_b", "conv_ln_g", "conv_ln_b",
            "mla_w_in", "mla_q_norm_g", "mla_w_uq", "mla_kv_norm_g", "mla_w_ukv", "final_norm_g")


def _full_weights(gathered, small_all, w):
    W = {}
    for n in _BIG:
        W[n] = _join(n, gathered[n])
        if n in _PERM:
            W[n] = _PERM[n][0](W[n])
    o = 0
    for n in _SMALL_SHARDED:
        W[n] = _join(n, small_all[:, o:o + w[n].size].reshape((N_DEV,) + w[n].shape))
        o += w[n].size
    for n in _REPLICATED:
        W[n] = w[n]
    return W


def kernel(x, mem, positions, norm_g, mem_norm_g, w_mem_kv, w_out, conv_w_in, conv_dw, conv_dw_b, conv_ln_g, conv_ln_b, mla_w_in, mla_q_norm_g, mla_w_uq, mla_kv_norm_g, mla_w_ukv, final_norm_g, loss_target, m_norm_g, m_mem_norm_g, m_w_mem_kv, m_w_out, m_conv_w_in, m_conv_dw, m_conv_dw_b, m_conv_ln_g, m_conv_ln_b, m_mla_w_in, m_mla_q_norm_g, m_mla_w_uq, m_mla_kv_norm_g, m_mla_w_ukv, m_final_norm_g, v_norm_g, v_mem_norm_g, v_w_mem_kv, v_w_out, v_conv_w_in, v_conv_dw, v_conv_dw_b, v_conv_ln_g, v_conv_ln_b, v_mla_w_in, v_mla_q_norm_g, v_mla_w_uq, v_mla_kv_norm_g, v_mla_w_ukv, v_final_norm_g):
    w = dict(zip(_WEIGHTS, (norm_g, mem_norm_g, w_mem_kv, w_out, conv_w_in, conv_dw, conv_dw_b, conv_ln_g, conv_ln_b,
                            mla_w_in, mla_q_norm_g, mla_w_uq, mla_kv_norm_g, mla_w_ukv, final_norm_g)))
    m = dict(zip(_WEIGHTS, (m_norm_g, m_mem_norm_g, m_w_mem_kv, m_w_out, m_conv_w_in, m_conv_dw, m_conv_dw_b, m_conv_ln_g,
                            m_conv_ln_b, m_mla_w_in, m_mla_q_norm_g, m_mla_w_uq, m_mla_kv_norm_g, m_mla_w_ukv, m_final_norm_g)))
    v = dict(zip(_WEIGHTS, (v_norm_g, v_mem_norm_g, v_w_mem_kv, v_w_out, v_conv_w_in, v_conv_dw, v_conv_dw_b, v_conv_ln_g,
                            v_conv_ln_b, v_mla_w_in, v_mla_q_norm_g, v_mla_w_uq, v_mla_kv_norm_g, v_mla_w_ukv, v_final_norm_g)))

    gathered = dict(zip(_BIG, _all_gather([w[n].astype(BF16) for n in _BIG], "gather_weights")))
    small_all = _all_gather_small(_as_tiles([w[n] for n in _SMALL_SHARDED]), "gather_small_weights").reshape(N_DEV, -1)
    W = _full_weights(gathered, small_all, w)

    loss_local, dx, G = _forward_backward(x, mem, positions, loss_target, W)
    loss = lax.psum(loss_local, ("x", "y", "c"))

    for n, (_, unperm) in _PERM.items():
        G[n] = unperm(G[n])
    small_like = [w[n] for n in _SMALL_SHARDED]
    small_cut = jnp.concatenate([_cut(n, G[n], w[n].shape).reshape(N_DEV, -1) for n in _SMALL_SHARDED], axis=1)
    small_tiles = jax.vmap(lambda r: _as_tiles([r]))(small_cut)
    g_cut = [_cut(n, G[n], w[n].shape) for n in _BIG] + [small_tiles]
    from_sibling = _sibling_exchange(g_cut, "reduce_sibling")
    chip_partial = [_add_own(g, r, "reduce_sibling_add_%d" % i) for i, (g, r) in enumerate(zip(g_cut, from_sibling))]
    from_chips = _chip_exchange(chip_partial, "reduce_chips")
    out = [{}, {}, {}, {}]
    for i, n in enumerate(_BIG):
        rows, cols = _rows2d(w[n].shape)
        res = _sum_adamw(from_chips[i], w[n].reshape(rows, cols), m[n].reshape(rows, cols), v[n].reshape(rows, cols),
                         "adamw_" + n)
        for o, r in zip(out, res):
            o[n] = r.reshape(w[n].shape)
    res = _sum_adamw(from_chips[-1], _as_tiles(small_like), _as_tiles([m[n] for n in _SMALL_SHARDED]),
                     _as_tiles([v[n] for n in _SMALL_SHARDED]), "adamw_small")
    for o, r in zip(out, res):
        for n, a in zip(_SMALL_SHARDED, _split_flat(r.reshape(-1), small_like)):
            o[n] = a

    rep_like = [w[n] for n in _REPLICATED]
    rep_parts = _all_gather_small(_as_tiles([G[n] for n in _REPLICATED]), "gather_replicated_grads")
    res = _sum_adamw(rep_parts, _as_tiles(rep_like), _as_tiles([m[n] for n in _REPLICATED]),
                     _as_tiles([v[n] for n in _REPLICATED]), "adamw_replicated")
    for o, r in zip(out, res):
        for n, a in zip(_REPLICATED, _split_flat(r.reshape(-1), rep_like)):
            o[n] = a

    return (loss, dx, *[out[0][n] for n in _WEIGHTS], *[out[1][n] for n in _WEIGHTS],
            *[out[2][n] for n in _WEIGHTS], *[out[3][n] for n in _WEIGHTS])
```

```python
import jax
import jax.numpy as jnp
from jax import lax
from jax.experimental import pallas as pl
from jax.experimental.pallas import tpu as pltpu

F32 = jnp.float32
BF16 = jnp.bfloat16
MESH = pl.DeviceIdType.MESH
N_DEV = 8
VMEM_LIMIT_BYTES = 48 * 1024 * 1024

MEM_HEADS, MEM_HEAD_DIM = 4, 128
MEM_WIDTH = MEM_HEADS * MEM_HEAD_DIM
CONV_KERNEL = 31
CONV_PAD = 32
MLA_HEADS, MLA_NOPE, MLA_ROPE, MLA_V = 12, 128, 64, 128
MLA_QK = MLA_NOPE + MLA_ROPE
HALF_ROPE = MLA_ROPE // 2
Q_RANK, KV_RANK = 512, 256
ROPE_THETA = 10000.0
RMS_EPS = 1e-6
LN_EPS = 1e-5
ADAM_LR, ADAM_B1, ADAM_B2, ADAM_EPS, ADAM_WD, ADAM_STEP = 0.001, 0.9, 0.999, 1e-08, 0.01, 10
NEG = -1e30


class _Stage:
    def __init__(self, ins, out_shapes, sems, start, wait, aliases=None):
        self.ins, self.out_shapes, self.sems = list(ins), list(out_shapes), list(sems)
        self.start, self.wait, self.aliases, self.outs = start, wait, dict(aliases or {}), None


def _merge_stages(a, b):
    na_i, na_o, na_s = len(a.ins), len(a.out_shapes), len(a.sems)

    def start(i, o, s):
        a.start(i[:na_i], o[:na_o], s[:na_s])
        b.start(i[na_i:], o[na_o:], s[na_s:])

    def wait(i, o, s):
        a.wait(i[:na_i], o[:na_o], s[:na_s])
        b.wait(i[na_i:], o[na_o:], s[na_s:])

    aliases = dict(a.aliases)
    aliases.update({na_i + k: na_o + v for k, v in b.aliases.items()})
    merged = _Stage(a.ins + b.ins, a.out_shapes + b.out_shapes, a.sems + b.sems, start, wait, aliases)
    merged.parts = (a, b)
    return merged


def _deliver(stage, outs):
    stage.outs = list(outs)
    if hasattr(stage, "parts"):
        a, b = stage.parts
        _deliver(a, outs[:len(a.out_shapes)])
        _deliver(b, outs[len(a.out_shapes):])


def _call(body, name, out_shape, grid=None, in_specs=None, out_specs=None, scratch=(), dims=None, grid_spec=None, aliases=None,
          carry=None):
    params = dict(vmem_limit_bytes=VMEM_LIMIT_BYTES)
    if dims is not None:
        params["dimension_semantics"] = dims
    kw = {}
    if carry is not None:
        single = not isinstance(out_shape, (list, tuple))
        main_out = [out_shape] if single else list(out_shape)
        main_specs = [out_specs] if single else list(out_specs)
        n_in, n_out, n_scr = len(in_specs), len(main_out), len(scratch)
        x_in, x_out = len(carry.ins), len(carry.out_shapes)
        inner, steps = body, tuple(grid)

        def body(*refs):
            ins, xin = refs[:n_in], refs[n_in:n_in + x_in]
            outs = refs[n_in + x_in:n_in + x_in + n_out]
            xout = refs[n_in + x_in + n_out:n_in + x_in + n_out + x_out]
            scr = refs[n_in + x_in + n_out + x_out:n_in + x_in + n_out + x_out + n_scr]
            xsem = refs[n_in + x_in + n_out + x_out + n_scr:]
            ids = [pl.program_id(a) for a in range(len(steps))]
            first, last = ids[0] == 0, ids[0] == steps[0] - 1
            for a in range(1, len(steps)):
                first = jnp.logical_and(first, ids[a] == 0)
                last = jnp.logical_and(last, ids[a] == steps[a] - 1)
            pl.when(first)(lambda: carry.start(xin, xout, xsem))
            inner(*ins, *outs, *scr)
            pl.when(last)(lambda: carry.wait(xin, xout, xsem))

        hbm = pl.BlockSpec(memory_space=pltpu.HBM)
        aliases = dict(aliases or {})
        aliases.update({n_in + k: n_out + v for k, v in carry.aliases.items()})
        res = _call(body, name, main_out + carry.out_shapes, grid=grid, in_specs=list(in_specs) + [hbm] * x_in,
                    out_specs=main_specs + [hbm] * x_out, scratch=list(scratch) + carry.sems, dims=dims, aliases=aliases)

        def run(*args):
            outs = res(*args, *carry.ins)
            _deliver(carry, outs[n_out:])
            return outs[0] if single else outs[:n_out]

        return run
    if aliases:
        kw["input_output_aliases"] = aliases
    if grid_spec is not None:
        kw["grid_spec"] = grid_spec
    else:
        if grid is not None:
            kw["grid"] = grid
        kw["in_specs"] = in_specs
        kw["out_specs"] = out_specs
        kw["scratch_shapes"] = list(scratch)
    return pl.pallas_call(body, name=name, out_shape=out_shape, compiler_params=pltpu.CompilerParams(**params), **kw)


def _pick(n, target, mult):
    best = None
    for d in range(mult, min(n, target) + 1, mult):
        if n % d == 0:
            best = d
    return n if best is None else best


_DOT_DIMS = {"nn": (((1,), (0,)), ((), ())), "nt": (((1,), (1,)), ((), ())), "tn": (((0,), (0,)), ((), ()))}


def _mm(a, b, mode, out_dtype, name, res=None, carry=None):
    if mode == "nn":
        (M, K), N = a.shape, b.shape[1]
    elif mode == "nt":
        (M, K), N = a.shape, b.shape[0]
    else:
        (K, M), N = a.shape, b.shape[1]
    tm = _pick(M, 1024 if mode != "tn" else 512, 8)
    tn = _pick(N, 512, 128)
    tk = _pick(K, 1024, 128)
    nk = K // tk
    has_res = res is not None

    def body(*refs):
        if has_res:
            a_ref, b_ref, r_ref, o_ref, acc = refs
        else:
            a_ref, b_ref, o_ref, acc = refs
        k = pl.program_id(2)

        @pl.when(k == 0)
        def _():
            acc[...] = jnp.zeros_like(acc)

        acc[...] += lax.dot_general(a_ref[...].astype(BF16), b_ref[...].astype(BF16), _DOT_DIMS[mode],
                                    preferred_element_type=F32)

        @pl.when(k == nk - 1)
        def _():
            r = acc[...]
            if has_res:
                r = r + r_ref[...]
            o_ref[...] = r.astype(o_ref.dtype)

    a_spec = {"nn": pl.BlockSpec((tm, tk), lambda i, j, k: (i, k)),
              "nt": pl.BlockSpec((tm, tk), lambda i, j, k: (i, k)),
              "tn": pl.BlockSpec((tk, tm), lambda i, j, k: (k, i))}[mode]
    b_spec = {"nn": pl.BlockSpec((tk, tn), lambda i, j, k: (k, j)),
              "nt": pl.BlockSpec((tn, tk), lambda i, j, k: (j, k)),
              "tn": pl.BlockSpec((tk, tn), lambda i, j, k: (k, j))}[mode]
    o_spec = pl.BlockSpec((tm, tn), lambda i, j, k: (i, j))
    in_specs = [a_spec, b_spec] + ([o_spec] if has_res else [])
    args = (a, b) + ((res,) if has_res else ())
    return _call(body, name, jax.ShapeDtypeStruct((M, N), out_dtype), grid=(M // tm, N // tn, nk),
                 in_specs=in_specs, out_specs=o_spec, scratch=[pltpu.VMEM((tm, tn), F32)],
                 dims=("parallel", "parallel", "arbitrary"), carry=carry)(*args)


def _views(rows):
    return [r if isinstance(r, tuple) else (r, r.shape[1], 0) for r in rows]


def _rowwise(f, rows, params, outs, name, tb=256, carry=None):
    rows = _views(rows)
    T = rows[0][0].shape[0]
    tb = min(tb, T)
    nr, npar = len(rows), len(params)
    outs = [o if len(o) == 3 else (o[0], o[1], o[0]) for o in outs]

    def body(*refs):
        vals = f(*[r[...].astype(F32) for r in refs[:nr]], *[p[...] for p in refs[nr:nr + npar]])
        for o_ref, v in zip(refs[nr + npar:], vals):
            o_ref[...] = v.astype(o_ref.dtype)

    row_spec = lambda w, cb=0: pl.BlockSpec((tb, w), lambda i: (i, cb))
    par_spec = lambda w: pl.BlockSpec((1, w), lambda i: (0, 0))
    res = _call(body, name, [jax.ShapeDtypeStruct((T, tw), dt) for _, dt, tw in outs], grid=(T // tb,),
                in_specs=[row_spec(w, cb) for _, w, cb in rows] + [par_spec(p.shape[1]) for p in params],
                out_specs=[row_spec(w) for w, _, _ in outs], dims=("parallel",), carry=carry)(*[r[0] for r in rows], *params)
    return res


def _rowwise_bwd(f, rows, params, douts, n_diff, name, tb=256, carry=None):
    rows, douts = _views(rows), _views(douts)
    T = rows[0][0].shape[0]
    tb = min(tb, T)
    nr, npar, nd = len(rows), len(params), len(douts)

    def body(*refs):
        rv = [r[...].astype(F32) for r in refs[:nr]]
        pv = [p[...] for p in refs[nr:nr + npar]]
        dv = [d[...].astype(F32) for d in refs[nr + npar:nr + npar + nd]]
        o_refs = refs[nr + npar + nd:]
        fixed = rv[n_diff:]

        def g(*xs):
            return tuple(f(*xs[:n_diff], *fixed, *xs[n_diff:]))

        _, vjp = jax.vjp(g, *rv[:n_diff], *pv)
        grads = vjp(tuple(dv))
        for o_ref, gr in zip(o_refs[:n_diff], grads[:n_diff]):
            o_ref[...] = gr.astype(o_ref.dtype)
        first = pl.program_id(0) == 0
        for o_ref, gr in zip(o_refs[n_diff:], grads[n_diff:]):
            @pl.when(first)
            def _(o_ref=o_ref):
                o_ref[...] = jnp.zeros_like(o_ref)

            o_ref[...] += gr

    row_spec = lambda w, cb=0: pl.BlockSpec((tb, w), lambda i: (i, cb))
    par_spec = lambda w: pl.BlockSpec((1, w), lambda i: (0, 0))
    out_shape = ([jax.ShapeDtypeStruct((T, w), F32) for _, w, _ in rows[:n_diff]]
                 + [jax.ShapeDtypeStruct((1, p.shape[1]), F32) for p in params])
    return _call(body, name, out_shape, grid=(T // tb,),
                 in_specs=([row_spec(w, cb) for _, w, cb in rows] + [par_spec(p.shape[1]) for p in params]
                           + [row_spec(w, cb) for _, w, cb in douts]),
                 out_specs=([row_spec(w) for _, w, _ in rows[:n_diff]] + [par_spec(p.shape[1]) for p in params]),
                 dims=("arbitrary",), carry=carry)(*[r[0] for r in rows], *params, *[d[0] for d in douts])


def _sig(x):
    return 1.0 / (1.0 + jnp.exp(-x))


def _rms(x, g):
    return x * lax.rsqrt(jnp.mean(x * x, axis=-1, keepdims=True) + RMS_EPS) * g


def _f_rms(x, g):
    return (_rms(x, g),)


def _f_glu(a, gate):
    return (a * _sig(gate),)


def _f_ln_silu(x, g, b):
    mu = jnp.mean(x, axis=-1, keepdims=True)
    xc = x - mu
    var = jnp.mean(xc * xc, axis=-1, keepdims=True)
    y = xc * lax.rsqrt(var + LN_EPS) * g + b
    return (y * _sig(y),)


def _rope128(x, cos_p, sin_p):
    return x * cos_p + pltpu.roll(x, 64, 1) * sin_p


def _rope128_t(d, cos_p, sin_p):
    return d * cos_p + pltpu.roll(d * sin_p, 64, 1)


def _f_rope(xq, xk, cos_p, sin_p):
    heads = [_rope128(xq[:, h * 128:(h + 1) * 128], cos_p, sin_p) for h in range(MLA_HEADS)]
    return (jnp.concatenate(heads, axis=1), _rope128(xk, cos_p, sin_p))


def _f_rope_t(dq, dk_heads, cos_p, sin_p):
    heads = [_rope128_t(dq[:, h * 128:(h + 1) * 128], cos_p, sin_p) for h in range(MLA_HEADS)]
    dk = dk_heads[:, 0:128]
    for h in range(1, MLA_HEADS):
        dk = dk + dk_heads[:, h * 128:(h + 1) * 128]
    return (jnp.concatenate(heads, axis=1), _rope128_t(dk, cos_p, sin_p))


GATE_LANES = 256


def _gate_fwd(ycat, proj, z_col, name, tb=1024):
    T, width = ycat.shape
    zb = z_col // GATE_LANES

    def body(y_ref, z_ref, o_ref):
        z = z_ref[...]
        o_ref[...] = (y_ref[...] * (z * _sig(z))).astype(o_ref.dtype)

    blk = pl.BlockSpec((tb, GATE_LANES), lambda i, c: (i, c))
    return _call(body, name, jax.ShapeDtypeStruct((T, width), BF16), grid=(T // tb, width // GATE_LANES),
                 in_specs=[blk, pl.BlockSpec((tb, GATE_LANES), lambda i, c: (i, zb + c))], out_specs=blk,
                 dims=("parallel", "parallel"))(ycat, proj)


def _gate_bwd(ycat, proj, z_col, dy, name, tb=1024):
    T, width = ycat.shape
    zb = z_col // GATE_LANES

    def body(y_ref, z_ref, dy_ref, dycat_ref, dz_ref):
        z, d = z_ref[...], dy_ref[...]
        s = _sig(z)
        dycat_ref[...] = d * (z * s)
        dz_ref[...] = d * y_ref[...] * (s * (1.0 + z * (1.0 - s)))

    blk = pl.BlockSpec((tb, GATE_LANES), lambda i, c: (i, c))
    return _call(body, name, [jax.ShapeDtypeStruct((T, width), F32)] * 2, grid=(T // tb, width // GATE_LANES),
                 in_specs=[blk, pl.BlockSpec((tb, GATE_LANES), lambda i, c: (i, zb + c)), blk], out_specs=[blk, blk],
                 dims=("parallel", "parallel"))(ycat, proj, dy)


def _final_loss(h, tgt, g, name, tb=256):
    T, D = h.shape

    def body(h_ref, t_ref, g_ref, dh_ref, dg_ref, loss_ref):
        tv = t_ref[...]

        def rowloss(hh, gg):
            e = _rms(hh, gg) - tv
            return 0.5 * jnp.mean(e * e, axis=-1, keepdims=True)

        lr, vjp = jax.vjp(rowloss, h_ref[...], g_ref[...])
        dh, dg = vjp(jnp.ones_like(lr))
        dh_ref[...] = dh

        @pl.when(pl.program_id(0) == 0)
        def _():
            dg_ref[...] = jnp.zeros_like(dg_ref)
            loss_ref[...] = jnp.zeros_like(loss_ref)

        dg_ref[...] += dg
        loss_ref[...] += jnp.broadcast_to(jnp.sum(lr, axis=0, keepdims=True), loss_ref.shape)

    row = pl.BlockSpec((tb, D), lambda i: (i, 0))
    par = pl.BlockSpec((1, D), lambda i: (0, 0))
    return _call(body, name,
                 [jax.ShapeDtypeStruct((T, D), F32), jax.ShapeDtypeStruct((1, D), F32), jax.ShapeDtypeStruct((1, 128), F32)],
                 grid=(T // tb,), in_specs=[row, row, par],
                 out_specs=[row, par, pl.BlockSpec((1, 128), lambda i: (0, 0))], dims=("arbitrary",))(h, tgt, g)


CONV_ROWS = 128
CONV_LANES = 256


def _dwconv_fwd(x, w, b, name, carry=None):
    B, S, C = x.shape
    cb = CONV_LANES
    off = CONV_PAD - (CONV_KERNEL - 1)

    def body(x_ref, w_ref, b_ref, o_ref, pad):
        pad[0:CONV_PAD, :] = jnp.zeros((CONV_PAD, cb), F32)
        pad[CONV_PAD:, :] = x_ref[...]
        for t0 in range(0, S, CONV_ROWS):
            acc = jnp.broadcast_to(b_ref[...], (CONV_ROWS, cb))
            for k in range(CONV_KERNEL):
                acc = acc + w_ref[k:k + 1, :] * pad[t0 + off + k:t0 + off + k + CONV_ROWS, :]
            o_ref[t0:t0 + CONV_ROWS, :] = acc

    return _call(body, name, jax.ShapeDtypeStruct((B, S, C), F32), grid=(B, C // cb),
                 in_specs=[pl.BlockSpec((None, S, cb), lambda i, j: (i, 0, j)),
                           pl.BlockSpec((CONV_KERNEL, cb), lambda i, j: (0, j)),
                           pl.BlockSpec((1, cb), lambda i, j: (0, j))],
                 out_specs=pl.BlockSpec((None, S, cb), lambda i, j: (i, 0, j)),
                 scratch=[pltpu.VMEM((S + CONV_PAD, cb), F32)], dims=("parallel", "parallel"), carry=carry)(x, w, b)


def _dwconv_bwd(x, w, dy, name, carry=None):
    B, S, C = x.shape
    cb = CONV_LANES
    off = CONV_PAD - (CONV_KERNEL - 1)
    groups = CONV_ROWS // 8

    def body(x_ref, w_ref, dy_ref, dx_ref, dw_ref, db_ref, xpad, dypad, wacc):
        xpad[0:CONV_PAD, :] = jnp.zeros((CONV_PAD, cb), F32)
        xpad[CONV_PAD:, :] = x_ref[...]
        dypad[0:S, :] = dy_ref[...]
        dypad[S:, :] = jnp.zeros((CONV_PAD, cb), F32)
        wacc[...] = jnp.zeros_like(wacc)
        for t0 in range(0, S, CONV_ROWS):
            dyc = dy_ref[t0:t0 + CONV_ROWS, :]
            acc = jnp.zeros((CONV_ROWS, cb), F32)
            for k in range(CONV_KERNEL):
                acc = acc + w_ref[k:k + 1, :] * dypad[t0 + (CONV_KERNEL - 1) - k:t0 + (CONV_KERNEL - 1) - k + CONV_ROWS, :]
                prod = dyc * xpad[t0 + off + k:t0 + off + k + CONV_ROWS, :]
                wacc[k] += jnp.sum(prod.reshape(groups, 8, cb), axis=0)
            wacc[CONV_KERNEL] += jnp.sum(dyc.reshape(groups, 8, cb), axis=0)
            dx_ref[t0:t0 + CONV_ROWS, :] = acc

        @pl.when(pl.program_id(1) == 0)
        def _():
            dw_ref[...] = jnp.zeros_like(dw_ref)
            db_ref[...] = jnp.zeros_like(db_ref)

        for k in range(CONV_KERNEL):
            dw_ref[k:k + 1, :] += jnp.sum(wacc[k], axis=0, keepdims=True)
        db_ref[...] += jnp.sum(wacc[CONV_KERNEL], axis=0, keepdims=True)

    blk = pl.BlockSpec((None, S, cb), lambda j, i: (i, 0, j))
    return _call(body, name,
                 [jax.ShapeDtypeStruct((B, S, C), F32), jax.ShapeDtypeStruct((CONV_KERNEL, C), F32),
                  jax.ShapeDtypeStruct((1, C), F32)],
                 grid=(C // cb, B),
                 in_specs=[blk, pl.BlockSpec((CONV_KERNEL, cb), lambda j, i: (0, j)), blk],
                 out_specs=[blk, pl.BlockSpec((CONV_KERNEL, cb), lambda j, i: (0, j)),
                            pl.BlockSpec((1, cb), lambda j, i: (0, j))],
                 scratch=[pltpu.VMEM((S + CONV_PAD, cb), F32), pltpu.VMEM((S + CONV_PAD, cb), F32),
                          pltpu.VMEM((CONV_KERNEL + 1, 8, cb), F32)],
                 dims=("parallel", "arbitrary"), carry=carry)(x, w, dy)


ATTN_TILE = 512
ATTN_SUB = 256


def _attn_shapes(Sq, Sk, causal):
    tq = min(Sq, ATTN_TILE)
    tk = tq if causal else min(Sk, ATTN_TILE)
    return tq, tk, min(ATTN_SUB, tq)


def _mask(row0, col0, rows, cols):
    r = row0 + lax.broadcasted_iota(jnp.int32, (rows, cols), 0)
    c = col0 + lax.broadcasted_iota(jnp.int32, (rows, cols), 1)
    return c <= r


def _attn_fwd(q, q_c0, qr, k, k_c0, kr, v, v_c0, B, Sq, Sk, H, causal, scale, name, into=None, o_c0=0, o_width=None):
    tq, tk, sub = _attn_shapes(Sq, Sk, causal)
    nq, nk, nsub = Sq // tq, Sk // tk, tq // sub
    rope = qr is not None

    def body(*refs):
        refs = list(refs)
        qn_ref = refs.pop(0)
        qr_ref = refs.pop(0) if rope else None
        kn_ref = refs.pop(0)
        kr_ref = refs.pop(0) if rope else None
        v_ref = refs.pop(0)
        if into is not None:
            refs.pop(0)
        o_ref, lse_ref, m_s, l_s, acc = refs
        qi = pl.program_id(2)
        m_s[...] = jnp.full_like(m_s, NEG)
        l_s[...] = jnp.zeros_like(l_s)
        acc[...] = jnp.zeros_like(acc)
        qs = []
        for r in range(nsub):
            qn = qn_ref[r * sub:(r + 1) * sub, :].astype(BF16)
            qs.append(jnp.concatenate([qn, qr_ref[r * sub:(r + 1) * sub, :]], axis=1) if rope else qn)

        def step(j, masked):
            ks = pl.ds(pl.multiple_of(j * tk, tk), tk)
            kk = jnp.concatenate([kn_ref[ks, :], kr_ref[ks, :]], axis=1) if rope else kn_ref[ks, :]
            vv = v_ref[ks, :]
            for r in range(nsub):
                rows = slice(r * sub, (r + 1) * sub)
                s = lax.dot_general(qs[r], kk, _DOT_DIMS["nt"], preferred_element_type=F32) * scale
                if masked:
                    s = jnp.where(_mask(qi * tq + r * sub, j * tk, sub, tk), s, NEG)
                m_old = m_s[rows, :]
                m_new = jnp.maximum(m_old, jnp.max(s, axis=-1, keepdims=True))
                p = jnp.exp(s - m_new)
                alpha = jnp.exp(m_old - m_new)
                l_s[rows, :] = alpha * l_s[rows, :] + jnp.sum(p, axis=-1, keepdims=True)
                acc[rows, :] = alpha * acc[rows, :] + jnp.dot(p.astype(BF16), vv, preferred_element_type=F32)
                m_s[rows, :] = m_new

        def unmasked(j, carry):
            step(j, False)
            return carry

        if causal:
            lax.fori_loop(0, qi, unmasked, 0)
            step(qi, True)
        else:
            lax.fori_loop(0, nk, unmasked, 0)
        o_ref[...] = (acc[...] / l_s[...]).astype(o_ref.dtype)
        lse_ref[...] = m_s[...] + jnp.log(l_s[...])

    qspec = lambda c0: pl.BlockSpec((tq, 128), lambda b, h, i: (b * nq + i, c0 + h))
    kspec = lambda c0: pl.BlockSpec((Sk, 128), lambda b, h, i: (b, c0 + h))
    in_specs, args = [qspec(q_c0)], [q]
    if rope:
        in_specs.append(qspec(0)); args.append(qr)
    in_specs.append(kspec(k_c0)); args.append(k)
    if rope:
        in_specs.append(pl.BlockSpec((Sk, 128), lambda b, h, i: (b, 0))); args.append(kr)
    in_specs.append(kspec(v_c0)); args.append(v)
    aliases = {}
    if into is not None:
        aliases = {len(args): 0}
        in_specs.append(pl.BlockSpec(memory_space=pl.ANY)); args.append(into)
        o_shape = jax.ShapeDtypeStruct(into.shape, into.dtype)
    else:
        o_shape = jax.ShapeDtypeStruct((B * Sq, o_width), F32)
    return _call(body, name, [o_shape, jax.ShapeDtypeStruct((B * H, Sq, 1), F32)], grid=(B, H, nq), in_specs=in_specs,
                 out_specs=[qspec(o_c0), pl.BlockSpec((None, tq, 1), lambda b, h, i: (b * H + h, i, 0))],
                 scratch=[pltpu.VMEM((tq, 1), F32), pltpu.VMEM((tq, 1), F32), pltpu.VMEM((tq, 128), F32)],
                 dims=("parallel", "parallel", "arbitrary"), aliases=aliases)(*args)


def _attn_bwd(q, q_c0, qr, k, k_c0, kr, v, v_c0, o, do, o_c0, lse, B, Sq, Sk, H, causal, scale, name, dq_width=None):
    tq, tk, sub = _attn_shapes(Sq, Sk, causal)
    nq, nk, nsub = Sq // tq, Sk // tk, tq // sub
    rope = qr is not None
    dk_w = 256 if rope else 128

    def body(*refs):
        refs = list(refs)
        qn_ref = refs.pop(0)
        qr_ref = refs.pop(0) if rope else None
        kn_ref = refs.pop(0)
        kr_ref = refs.pop(0) if rope else None
        v_ref, o_ref, do_ref, lse_ref = refs[:4]
        refs = refs[4:]
        dqn_ref = refs.pop(0)
        dqr_ref = refs.pop(0) if rope else None
        dkn_ref = refs.pop(0)
        dkr_ref = refs.pop(0) if rope else None
        dv_ref, q_s, do_s, dl_s, dq_acc, dk_acc, dv_acc = refs
        kj = pl.program_id(2)

        @pl.when(kj == 0)
        def _():
            qn = qn_ref[...].astype(BF16)
            q_s[...] = jnp.concatenate([qn, qr_ref[...]], axis=1) if rope else qn
            dof = do_ref[...]
            do_s[...] = dof.astype(BF16)
            dl_s[...] = jnp.sum(dof * o_ref[...], axis=-1, keepdims=True)
            dq_acc[...] = jnp.zeros_like(dq_acc)

        kk = jnp.concatenate([kn_ref[...], kr_ref[...]], axis=1) if rope else kn_ref[...]
        vv = v_ref[...]
        dk_acc[...] = jnp.zeros_like(dk_acc)
        dv_acc[...] = jnp.zeros_like(dv_acc)

        def step(i, masked):
            for r in range(nsub):
                rows = pl.ds(pl.multiple_of(i * tq + r * sub, sub), sub)
                qq, dob = q_s[rows, :], do_s[rows, :]
                s = lax.dot_general(qq, kk, _DOT_DIMS["nt"], preferred_element_type=F32) * scale
                if masked:
                    s = jnp.where(_mask(i * tq + r * sub, kj * tk, sub, tk), s, NEG)
                p = jnp.exp(s - lse_ref[rows, :])
                dp = lax.dot_general(dob, vv, _DOT_DIMS["nt"], preferred_element_type=F32)
                ds = (p * (dp - dl_s[rows, :]) * scale).astype(BF16)
                dv_acc[...] += lax.dot_general(p.astype(BF16), dob, _DOT_DIMS["tn"], preferred_element_type=F32)
                dk_acc[...] += lax.dot_general(ds, qq, _DOT_DIMS["tn"], preferred_element_type=F32)
                dq_acc[rows, :] += jnp.dot(ds, kk, preferred_element_type=F32)

        def unmasked(i, carry):
            step(i, False)
            return carry

        if causal:
            step(kj, True)
            lax.fori_loop(kj + 1, nq, unmasked, 0)
        else:
            lax.fori_loop(0, nq, unmasked, 0)
        dkn_ref[...] = dk_acc[:, 0:128]
        if rope:
            dkr_ref[...] = dk_acc[:, 128:256]
        dv_ref[...] = dv_acc[...]

        @pl.when(kj == nk - 1)
        def _():
            dqn_ref[...] = dq_acc[:, 0:128]
            if rope:
                dqr_ref[...] = dq_acc[:, 128:256]

    qspec = lambda c0: pl.BlockSpec((Sq, 128), lambda b, h, j: (b, c0 + h))
    kspec = lambda c0: pl.BlockSpec((tk, 128), lambda b, h, j: (b * nk + j, c0 + h))
    in_specs, args = [qspec(q_c0)], [q]
    if rope:
        in_specs.append(qspec(0)); args.append(qr)
    in_specs.append(kspec(k_c0)); args.append(k)
    if rope:
        in_specs.append(pl.BlockSpec((tk, 128), lambda b, h, j: (b * nk + j, 0))); args.append(kr)
    in_specs += [kspec(v_c0), qspec(o_c0), qspec(o_c0), pl.BlockSpec((None, Sq, 1), lambda b, h, j: (b * H + h, 0, 0))]
    args += [v, o, do, lse]
    q_rows = jax.ShapeDtypeStruct((B * Sq, dq_width or H * 128), F32)
    h_rows_q = jax.ShapeDtypeStruct((B * Sq, H * 128), F32)
    h_rows_k = jax.ShapeDtypeStruct((B * Sk, H * 128), F32)
    out_shape, out_specs = [q_rows], [qspec(0)]
    if rope:
        out_shape.append(h_rows_q); out_specs.append(qspec(0))
    out_shape.append(h_rows_k); out_specs.append(kspec(0))
    if rope:
        out_shape.append(h_rows_k); out_specs.append(kspec(0))
    out_shape.append(h_rows_k); out_specs.append(kspec(0))
    return _call(body, name, out_shape, grid=(B, H, nk), in_specs=in_specs, out_specs=out_specs,
                 scratch=[pltpu.VMEM((Sq, dk_w), BF16), pltpu.VMEM((Sq, 128), BF16), pltpu.VMEM((Sq, 1), F32),
                          pltpu.VMEM((Sq, dk_w), F32), pltpu.VMEM((tk, dk_w), F32), pltpu.VMEM((tk, 128), F32)],
                 dims=("parallel", "parallel", "arbitrary"))(*args)


def _mem_attention_fwd(proj, q_col, ycat, mem2, mem_g, w_mem, B, S, tag):
    M = mem2.shape[0] // B
    (memn,) = _rowwise(_f_rms, [mem2], [mem_g], [(mem2.shape[1], BF16)], tag + "_memnorm")
    kvm = _mm(memn, w_mem, "nn", BF16, tag + "_memkv")
    o_c0 = ycat.shape[1] // 128 - MEM_HEADS
    ycat, lse = _attn_fwd(proj, q_col // 128, None, kvm, 0, None, kvm, MEM_HEADS, B, S, M, MEM_HEADS, False,
                          MEM_HEAD_DIM ** -0.5, tag + "_memattn", into=ycat, o_c0=o_c0)
    return ycat, (memn, kvm, lse)


def _mem_attention_bwd(proj, q_col, ycat, d_ycat, saved, mem2, mem_g, w_mem, B, S, tag):
    memn, kvm, lse = saved
    M = mem2.shape[0] // B
    o_c0 = ycat.shape[1] // 128 - MEM_HEADS
    d_q, d_k, d_v = _attn_bwd(proj, q_col // 128, None, kvm, 0, None, kvm, MEM_HEADS, ycat, d_ycat, o_c0, lse, B, S, M,
                              MEM_HEADS, False, MEM_HEAD_DIM ** -0.5, tag + "_memattn_bwd")
    d_kvm = jnp.concatenate([d_k, d_v], axis=1)
    d_w_mem = _mm(memn, d_kvm, "tn", F32, tag + "_memkv_dw")
    d_memn = _mm(d_kvm, w_mem, "nt", F32, tag + "_memkv_dx")
    _, d_mem_g = _rowwise_bwd(_f_rms, [mem2], [mem_g], [d_memn], 1, tag + "_memnorm_bwd")
    return d_q, d_w_mem, d_mem_g


def _rope_tables(positions):
    inv_freq = 1.0 / (ROPE_THETA ** (jnp.arange(0, MLA_ROPE, 2, dtype=F32) / MLA_ROPE))
    ang = positions.astype(F32).reshape(-1, 1) * inv_freq
    cos, sin, zero = jnp.cos(ang), jnp.sin(ang), jnp.zeros_like(ang)
    return jnp.concatenate([cos, zero, cos, zero], axis=1), jnp.concatenate([-sin, zero, sin, zero], axis=1)


def _forward_backward(x, mem, positions, target, W):
    B, S, D = x.shape
    T = B * S
    conv_w = W["conv_dw"].shape[1]
    mix_w = 2 * D
    h0 = x.reshape(T, D)
    mem2 = mem.reshape(-1, D)
    tgt = target.reshape(T, D)
    row = lambda v: v.reshape(1, -1)
    n_nope = MLA_HEADS * MLA_NOPE

    g0 = row(W["norm_g"][0])
    (u0,) = _rowwise(_f_rms, [h0], [g0], [(D, BF16)], "l0_norm", carry=W.carry("l0_norm"))
    proj0 = _mm(u0, W["conv_w_in"], "nn", F32, "l0_in", carry=W.carry("l0_in"))
    a0, gate0 = (proj0, conv_w, 0), (proj0, conv_w, 1)
    qm0_col, z0_col = 2 * conv_w, 2 * conv_w + MEM_WIDTH
    (glu,) = _rowwise(_f_glu, [a0, gate0], [], [(conv_w, F32)], "l0_glu", carry=W.carry("l0_glu"))
    dw, dwb = W["conv_dw"], row(W["conv_dw_b"][0])
    cv = _dwconv_fwd(glu.reshape(B, S, conv_w), dw, dwb, "l0_dwconv", carry=W.carry("l0_dwconv")).reshape(T, conv_w)
    ln_g, ln_b = row(W["conv_ln_g"][0]), row(W["conv_ln_b"][0])
    (ycat0,) = _rowwise(_f_ln_silu, [cv], [ln_g, ln_b], [(conv_w, F32, mix_w)], "l0_ln", carry=W.carry("l0_ln"))
    mg0 = row(W["mem_norm_g"][0])
    ycat0, mem_saved0 = _mem_attention_fwd(proj0, qm0_col, ycat0, mem2, mg0, W["w_mem_kv"][0], B, S, "l0")
    y0 = _gate_fwd(ycat0, proj0, z0_col, "l0_gate")
    h1 = _mm(y0, W["w_out"][0], "nn", F32, "l0_out", res=h0)

    g1 = row(W["norm_g"][1])
    (u1,) = _rowwise(_f_rms, [h1], [g1], [(D, BF16)], "l1_norm")
    proj1 = _mm(u1, W["mla_w_in"], "nn", F32, "l1_in")
    cq, ckv = (proj1, Q_RANK, 0), (proj1, KV_RANK, Q_RANK // KV_RANK)
    qm1_col = Q_RANK + KV_RANK
    z1_col = qm1_col + MEM_WIDTH
    kr_col = z1_col + mix_w
    qg, kvg = row(W["mla_q_norm_g"]), row(W["mla_kv_norm_g"])
    (cqn,) = _rowwise(_f_rms, [cq], [qg], [(Q_RANK, BF16)], "l1_qnorm")
    (ckvn,) = _rowwise(_f_rms, [ckv], [kvg], [(KV_RANK, BF16)], "l1_kvnorm")
    qf = _mm(cqn, W["mla_w_uq"], "nn", F32, "l1_uq")
    kvf = _mm(ckvn, W["mla_w_ukv"], "nn", BF16, "l1_ukv")
    cos_p, sin_p = _rope_tables(positions)
    qr, kr = _rowwise(_f_rope, [(qf, n_nope, 1), (proj1, 128, kr_col // 128), cos_p, sin_p], [],
                      [(n_nope, BF16), (128, BF16)], "l1_rope")
    scale1 = MLA_QK ** -0.5
    ycat1, lse1 = _attn_fwd(qf, 0, qr, kvf, 0, kr, kvf, MLA_HEADS, B, S, S, MLA_HEADS, True, scale1, "l1_attn",
                            o_width=mix_w)
    mg1 = row(W["mem_norm_g"][1])
    ycat1, mem_saved1 = _mem_attention_fwd(proj1, qm1_col, ycat1, mem2, mg1, W["w_mem_kv"][1], B, S, "l1")
    y1 = _gate_fwd(ycat1, proj1, z1_col, "l1_gate")
    h2 = _mm(y1, W["w_out"][1], "nn", F32, "l1_out", res=h1)

    gf = row(W["final_norm_g"])
    dh2, d_gf, loss128 = _final_loss(h2, tgt, gf, "final_loss")
    G = {"final_norm_g": d_gf.reshape(-1)}
    L1 = {}

    dy1 = _mm(dh2, W["w_out"][1], "nt", F32, "l1_out_dx")
    d_wout1 = _mm(y1, dh2, "tn", F32, "l1_out_dw")
    d_ycat1, d_z1 = _gate_bwd(ycat1, proj1, z1_col, dy1, "l1_gate_bwd")
    d_qm1, d_wmem1, d_mg1 = _mem_attention_bwd(proj1, qm1_col, ycat1, d_ycat1, mem_saved1, mem2, mg1, W["w_mem_kv"][1],
                                               B, S, "l1")
    d_qn, d_qr, d_kn, d_kr_heads, d_v = _attn_bwd(qf, 0, qr, kvf, 0, kr, kvf, MLA_HEADS, ycat1, d_ycat1, 0, lse1, B, S, S,
                                                  MLA_HEADS, True, scale1, "l1_attn_bwd")
    d_xq, d_kr = _rowwise(_f_rope_t, [d_qr, d_kr_heads, cos_p, sin_p], [], [(n_nope, F32), (128, F32)], "l1_rope_bwd")
    d_qf = jnp.concatenate([d_qn, d_xq], axis=1)
    d_kvf = jnp.concatenate([d_kn, d_v], axis=1)
    d_cqn = _mm(d_qf, W["mla_w_uq"], "nt", F32, "l1_uq_dx")
    L1[("mla_w_uq", None)] = _mm(cqn, d_qf, "tn", F32, "l1_uq_dw")
    d_ckvn = _mm(d_kvf, W["mla_w_ukv"], "nt", F32, "l1_ukv_dx")
    L1[("mla_w_ukv", None)] = _mm(ckvn, d_kvf, "tn", F32, "l1_ukv_dw")
    d_cq, d_qg = _rowwise_bwd(_f_rms, [cq], [qg], [d_cqn], 1, "l1_qnorm_bwd")
    d_ckv, d_kvg = _rowwise_bwd(_f_rms, [ckv], [kvg], [d_ckvn], 1, "l1_kvnorm_bwd")
    d_proj1 = jnp.concatenate([d_cq, d_ckv, d_qm1, d_z1, d_kr], axis=1)
    L1[("mla_w_in", None)] = _mm(u1, d_proj1, "tn", F32, "l1_in_dw")
    L1[("w_mem_kv", 1)], L1[("w_out", 1)] = d_wmem1, d_wout1
    W.ready("l1", L1)
    d_u1 = _mm(d_proj1, W["mla_w_in"], "nt", F32, "l1_in_dx", carry=W.carry("l1_in_dx"))
    d_h1n, d_g1 = _rowwise_bwd(_f_rms, [h1], [g1], [d_u1], 1, "l1_norm_bwd")
    dh1 = dh2 + d_h1n

    dy0 = _mm(dh1, W["w_out"][0], "nt", F32, "l0_out_dx")
    d_wout0 = _mm(y0, dh1, "tn", F32, "l0_out_dw")
    d_ycat0, d_z0 = _gate_bwd(ycat0, proj0, z0_col, dy0, "l0_gate_bwd")
    d_qm0, d_wmem0, d_mg0 = _mem_attention_bwd(proj0, qm0_col, ycat0, d_ycat0, mem_saved0, mem2, mg0, W["w_mem_kv"][0],
                                               B, S, "l0")
    W.ready("l0a", {("w_mem_kv", 0): d_wmem0, ("w_out", 0): d_wout0})
    d_cv, d_ln_g, d_ln_b = _rowwise_bwd(_f_ln_silu, [cv], [ln_g, ln_b], [(d_ycat0, conv_w, 0)], 1, "l0_ln_bwd",
                                        carry=W.carry("l0_ln_bwd"))
    d_glu, d_dw, d_dwb = _dwconv_bwd(glu.reshape(B, S, conv_w), dw, d_cv.reshape(B, S, conv_w), "l0_dwconv_bwd",
                                     carry=W.carry("l0_dwconv_bwd"))
    d_a0, d_gate0 = _rowwise_bwd(_f_glu, [a0, gate0], [], [d_glu.reshape(T, conv_w)], 2, "l0_glu_bwd")
    d_proj0 = jnp.concatenate([d_a0, d_gate0, d_qm0, d_z0], axis=1)
    W.ready("l0b", {("conv_w_in", None): _mm(u0, d_proj0, "tn", F32, "l0_in_dw"), ("conv_dw", None): d_dw,
                    ("mla_q_norm_g", None): d_qg.reshape(-1), ("mla_kv_norm_g", None): d_kvg.reshape(-1)})
    d_u0 = _mm(d_proj0, W["conv_w_in"], "nt", F32, "l0_in_dx", carry=W.carry("l0_in_dx"))
    d_h0n, d_g0 = _rowwise_bwd(_f_rms, [h0], [g0], [d_u0], 1, "l0_norm_bwd", carry=W.carry("l0_norm_bwd"))
    dx = (dh1 + d_h0n).reshape(B, S, D)

    G["norm_g"] = jnp.concatenate([d_g0, d_g1], axis=0)
    G["mem_norm_g"] = jnp.concatenate([d_mg0, d_mg1], axis=0)
    G["conv_dw_b"] = d_dwb
    G["conv_ln_g"], G["conv_ln_b"] = d_ln_g, d_ln_b
    return loss128[0, 0], dx, G


def _mla_in_perm(w):
    c2 = Q_RANK + KV_RANK
    zero = jnp.zeros((w.shape[0], HALF_ROPE), w.dtype)
    return jnp.concatenate([w[:, :c2], w[:, c2 + MLA_ROPE:], w[:, c2:c2 + HALF_ROPE], zero,
                            w[:, c2 + HALF_ROPE:c2 + MLA_ROPE], zero], axis=1)


def _mla_in_unperm(g):
    c2 = Q_RANK + KV_RANK
    r = g.shape[1] - 128
    return jnp.concatenate([g[:, :c2], g[:, r:r + HALF_ROPE], g[:, r + 64:r + 64 + HALF_ROPE], g[:, c2:r]], axis=1)


def _uq_perm(w):
    n = w.shape[0]
    w3 = w.reshape(n, MLA_HEADS, MLA_QK)
    zero = jnp.zeros((n, MLA_HEADS, HALF_ROPE), w.dtype)
    rope = jnp.concatenate([w3[:, :, MLA_NOPE:MLA_NOPE + HALF_ROPE], zero, w3[:, :, MLA_NOPE + HALF_ROPE:], zero], axis=2)
    return jnp.concatenate([w3[:, :, :MLA_NOPE].reshape(n, -1), rope.reshape(n, -1)], axis=1)


def _uq_unperm(g):
    n = g.shape[0]
    n_nope = MLA_HEADS * MLA_NOPE
    rope = g[:, n_nope:].reshape(n, MLA_HEADS, 128)
    return jnp.concatenate([g[:, :n_nope].reshape(n, MLA_HEADS, MLA_NOPE), rope[:, :, :HALF_ROPE],
                            rope[:, :, 64:64 + HALF_ROPE]], axis=2).reshape(n, -1)


def _ukv_perm(w):
    w3 = w.reshape(w.shape[0], MLA_HEADS, MLA_NOPE + MLA_V)
    return jnp.concatenate([w3[:, :, :MLA_NOPE].reshape(w.shape[0], -1), w3[:, :, MLA_NOPE:].reshape(w.shape[0], -1)], axis=1)


def _ukv_unperm(g):
    n = g.shape[0]
    half = MLA_HEADS * MLA_NOPE
    return jnp.concatenate([g[:, :half].reshape(n, MLA_HEADS, MLA_NOPE), g[:, half:].reshape(n, MLA_HEADS, MLA_V)],
                           axis=2).reshape(n, -1)


_ROW_CUT = ("w_mem_kv", "w_out")
_COL_CUT = ("conv_w_in", "mla_w_in", "mla_w_uq", "mla_w_ukv", "conv_dw")
_BIG = ("w_mem_kv", "w_out", "conv_w_in", "mla_w_in", "mla_w_uq", "mla_w_ukv")
_SMALL_SHARDED = ("conv_dw", "mla_q_norm_g", "mla_kv_norm_g")
_REPLICATED = ("norm_g", "mem_norm_g", "conv_dw_b", "conv_ln_g", "conv_ln_b", "final_norm_g")
_PERM = {"mla_w_in": (_mla_in_perm, _mla_in_unperm), "mla_w_uq": (_uq_perm, _uq_unperm), "mla_w_ukv": (_ukv_perm, _ukv_unperm)}


def _join(n, blocks):
    if n in _ROW_CUT:
        _, L, r, c = blocks.shape
        return blocks.transpose(1, 0, 2, 3).reshape(L, N_DEV * r, c)
    if n in _COL_CUT:
        _, _, r, c = blocks.shape
        return blocks.reshape(N_DEV, r, c).transpose(1, 0, 2).reshape(r, N_DEV * c)
    return blocks.reshape(-1)


def _cut(n, full, shard_shape):
    if n in _ROW_CUT:
        L, r, c = shard_shape
        return full.reshape(L, N_DEV, r, c).transpose(1, 0, 2, 3)
    if n in _COL_CUT:
        _, r, c = shard_shape
        return full.reshape(r, N_DEV, c).transpose(1, 0, 2).reshape(N_DEV, 1, r, c)
    return full.reshape(N_DEV, 1, -1)


def _flat_pad(parts, size):
    flat = jnp.concatenate([p.reshape(-1) for p in parts])
    return jnp.concatenate([flat, jnp.zeros((size - flat.shape[0],), flat.dtype)])


SMALL_LANES = 128 * 8


def _as_tiles(flat_parts):
    total = sum(p.size for p in flat_parts)
    size = -(-total // SMALL_LANES) * SMALL_LANES
    return _flat_pad(flat_parts, size).reshape(8, size // 8)


def _split_flat(flat, like):
    out, o = [], 0
    for a in like:
        out.append(flat[o:o + a.size].reshape(a.shape))
        o += a.size
    return out


_HBM = pl.BlockSpec(memory_space=pltpu.HBM)
_VMEM = pl.BlockSpec(memory_space=pltpu.VMEM)


def _position():
    return lax.axis_index("x"), lax.axis_index("y"), lax.axis_index("c")


def _dma_sems(n):
    return [pltpu.SemaphoreType.DMA((n,)), pltpu.SemaphoreType.DMA((n,))]


def _run_stage(stage, name):
    n_in, n_out = len(stage.ins), len(stage.out_shapes)

    def body(*refs):
        ins, outs, sems = refs[:n_in], refs[n_in:n_in + n_out], refs[n_in + n_out:]
        stage.start(ins, outs, sems)
        stage.wait(ins, outs, sems)

    outs = _call(body, name, stage.out_shapes, in_specs=[_HBM] * n_in, out_specs=[_HBM] * n_out, scratch=stage.sems,
                 aliases=stage.aliases)(*stage.ins)
    _deliver(stage, outs)
    return stage.outs


def _gather_chips_stage(shards):
    n = len(shards)

    def copies(x_refs, out_refs, sems):
        send_sems, recv_sems, _ = sems
        x, y, c = _position()
        peers = [(x, y, 1 - c), (1 - x, y, c), (x, 1 - y, c), (1 - x, 1 - y, c)]
        out = []
        for a in range(n):
            for k, (px, py, pc) in enumerate(peers):
                send = pltpu.make_async_remote_copy(src_ref=x_refs[a], dst_ref=out_refs[a].at[4 * x + 2 * y + c],
                                                    send_sem=send_sems.at[4 * a + k], recv_sem=recv_sems.at[4 * a + k],
                                                    device_id=(px, py, pc), device_id_type=MESH)
                recv = pltpu.make_async_remote_copy(src_ref=x_refs[a], dst_ref=out_refs[a].at[4 * px + 2 * py + pc],
                                                    send_sem=send_sems.at[4 * a + k], recv_sem=recv_sems.at[4 * a + k],
                                                    device_id=(px, py, pc), device_id_type=MESH)
                out.append((send, recv))
        return out

    def local(x_refs, out_refs, sems):
        x, y, c = _position()
        return [pltpu.make_async_copy(x_refs[a], out_refs[a].at[4 * x + 2 * y + c], sems[2].at[a]) for a in range(n)]

    def start(x_refs, out_refs, sems):
        for cp in local(x_refs, out_refs, sems):
            cp.start()
        for send, _ in copies(x_refs, out_refs, sems):
            send.start()

    def wait(x_refs, out_refs, sems):
        for send, recv in copies(x_refs, out_refs, sems):
            recv.wait_recv()
            send.wait_send()
        for cp in local(x_refs, out_refs, sems):
            cp.wait()

    return _Stage(shards, [jax.ShapeDtypeStruct((N_DEV,) + a.shape, a.dtype) for a in shards],
                  _dma_sems(4 * n) + [pltpu.SemaphoreType.DMA((n,))], start, wait)


def _gather_sibling_stage(bufs):
    n = len(bufs)

    def copies(out_refs, sems):
        send_sems, recv_sems = sems
        x, y, c = _position()
        out = []
        for a in range(n):
            for j, (px, py) in enumerate([(1 - x, y), (x, 1 - y), (1 - x, 1 - y)]):
                mine, theirs = out_refs[a].at[4 * px + 2 * py + c], out_refs[a].at[4 * px + 2 * py + (1 - c)]
                send = pltpu.make_async_remote_copy(src_ref=mine, dst_ref=mine, send_sem=send_sems.at[3 * a + j],
                                                    recv_sem=recv_sems.at[3 * a + j], device_id=(x, y, 1 - c),
                                                    device_id_type=MESH)
                recv = pltpu.make_async_remote_copy(src_ref=mine, dst_ref=theirs, send_sem=send_sems.at[3 * a + j],
                                                    recv_sem=recv_sems.at[3 * a + j], device_id=(x, y, 1 - c),
                                                    device_id_type=MESH)
                out.append((send, recv))
        return out

    def start(_, out_refs, sems):
        for send, _r in copies(out_refs, sems):
            send.start()

    def wait(_, out_refs, sems):
        for send, recv in copies(out_refs, sems):
            recv.wait_recv()
            send.wait_send()

    return _Stage(bufs, [jax.ShapeDtypeStruct(b.shape, b.dtype) for b in bufs], _dma_sems(3 * n), start, wait,
                  aliases={a: a for a in range(n)})


def _all_gather_small(v, name):
    r, n = v.shape

    def body(x_ref, out_ref, send_sems, recv_sems, local_sem):
        x, y, c = _position()
        me = 4 * x + 2 * y + c
        mine = pltpu.make_async_copy(x_ref, out_ref.at[me], local_sem)
        mine.start()
        flips = [(fx, fy, fc) for fx in (0, 1) for fy in (0, 1) for fc in (0, 1)][1:]
        copies = []
        for k, (fx, fy, fc) in enumerate(flips):
            peer = (x ^ fx, y ^ fy, c ^ fc)
            cp = pltpu.make_async_remote_copy(src_ref=x_ref, dst_ref=out_ref.at[me], send_sem=send_sems.at[k],
                                              recv_sem=recv_sems.at[k], device_id=peer, device_id_type=MESH)
            cp.start()
            copies.append(cp)
        for k, (fx, fy, fc) in enumerate(flips):
            px, py, pc = x ^ fx, y ^ fy, c ^ fc
            src = out_ref.at[4 * px + 2 * py + pc]
            pltpu.make_async_remote_copy(src_ref=x_ref, dst_ref=src, send_sem=send_sems.at[k], recv_sem=recv_sems.at[k],
                                         device_id=(px, py, pc), device_id_type=MESH).wait_recv()
        for cp in copies:
            cp.wait_send()
        mine.wait()

    return _call(body, name, jax.ShapeDtypeStruct((N_DEV, r, n), v.dtype), in_specs=[_VMEM], out_specs=_VMEM,
                 scratch=_dma_sems(7) + [pltpu.SemaphoreType.DMA(())])(v)


def _reduce_sibling_stage(gs):
    n = len(gs)

    def copies(g_refs, out_refs, sems):
        send_sems, recv_sems = sems
        x, y, c = _position()
        return [pltpu.make_async_remote_copy(src_ref=g_refs[a].at[2 * k + (1 - c)], dst_ref=out_refs[a].at[k],
                                             send_sem=send_sems.at[4 * a + k], recv_sem=recv_sems.at[4 * a + k],
                                             device_id=(x, y, 1 - c), device_id_type=MESH)
                for a in range(n) for k in range(4)]

    def start(g_refs, out_refs, sems):
        for cp in copies(g_refs, out_refs, sems):
            cp.start()

    def wait(g_refs, out_refs, sems):
        for cp in copies(g_refs, out_refs, sems):
            cp.wait()

    return _Stage(gs, [jax.ShapeDtypeStruct((4,) + g.shape[1:], g.dtype) for g in gs], _dma_sems(4 * n), start, wait)


def _rows2d(shape):
    cols = shape[-1]
    rows = 1
    for s in shape[:-1]:
        rows *= s
    return rows, cols


def _add_own(g, recv, name):
    rows, cols = _rows2d(g.shape[1:])
    tr = _pick(rows, 256, 8)
    c = lax.axis_index("c").astype(jnp.int32).reshape(1)

    def body(c_ref, g_ref, r_ref, o_ref):
        o_ref[...] = g_ref[...] + r_ref[...]

    grid_spec = pltpu.PrefetchScalarGridSpec(
        num_scalar_prefetch=1, grid=(4, rows // tr),
        in_specs=[pl.BlockSpec((None, None, tr, cols), lambda k, i, c_ref: (k, c_ref[0], i, 0)),
                  pl.BlockSpec((None, tr, cols), lambda k, i, c_ref: (k, i, 0))],
        out_specs=pl.BlockSpec((None, tr, cols), lambda k, i, c_ref: (k, i, 0)))
    return _call(body, name, jax.ShapeDtypeStruct((4, rows, cols), F32), grid_spec=grid_spec,
                 dims=("parallel", "parallel"))(c, g.reshape(4, 2, rows, cols), recv.reshape(4, rows, cols))


def _reduce_chips_stage(pas):
    n = len(pas)

    def copies(pa_refs, out_refs, sems):
        send_sems, recv_sems, _ = sems
        x, y, c = _position()
        my_chip = 2 * x + y
        out = []
        for a in range(n):
            for j, (px, py) in enumerate([(1 - x, y), (x, 1 - y), (1 - x, 1 - y)]):
                send = pltpu.make_async_remote_copy(src_ref=pa_refs[a].at[2 * px + py], dst_ref=out_refs[a].at[my_chip],
                                                    send_sem=send_sems.at[3 * a + j], recv_sem=recv_sems.at[3 * a + j],
                                                    device_id=(px, py, c), device_id_type=MESH)
                recv = pltpu.make_async_remote_copy(src_ref=pa_refs[a].at[2 * px + py], dst_ref=out_refs[a].at[2 * px + py],
                                                    send_sem=send_sems.at[3 * a + j], recv_sem=recv_sems.at[3 * a + j],
                                                    device_id=(px, py, c), device_id_type=MESH)
                out.append((send, recv))
        return out

    def local(pa_refs, out_refs, sems):
        x, y, _ = _position()
        return [pltpu.make_async_copy(pa_refs[a].at[2 * x + y], out_refs[a].at[2 * x + y], sems[2].at[a]) for a in range(n)]

    def start(pa_refs, out_refs, sems):
        for cp in local(pa_refs, out_refs, sems):
            cp.start()
        for send, _r in copies(pa_refs, out_refs, sems):
            send.start()

    def wait(pa_refs, out_refs, sems):
        for send, recv in copies(pa_refs, out_refs, sems):
            recv.wait_recv()
            send.wait_send()
        for cp in local(pa_refs, out_refs, sems):
            cp.wait()

    return _Stage(pas, [jax.ShapeDtypeStruct(pa.shape, pa.dtype) for pa in pas],
                  _dma_sems(3 * n) + [pltpu.SemaphoreType.DMA((n,))], start, wait)


def _adamw_math(w, g, m, v):
    m = ADAM_B1 * m + (1.0 - ADAM_B1) * g
    v = ADAM_B2 * v + (1.0 - ADAM_B2) * (g * g)
    m_hat = m / (1.0 - ADAM_B1 ** ADAM_STEP)
    v_hat = v / (1.0 - ADAM_B2 ** ADAM_STEP)
    delta = -ADAM_LR * (m_hat / (jnp.sqrt(v_hat) + ADAM_EPS) + ADAM_WD * w)
    return delta, m, v


def _sum_adamw(parts, w, m, v, name):
    n, rows, cols = parts.shape
    tr = _pick(rows, 128, 8)

    def body(p_ref, w_ref, m_ref, v_ref, g_ref, d_ref, nm_ref, nv_ref):
        g = p_ref[0]
        for k in range(1, n):
            g = g + p_ref[k]
        d, nm, nv = _adamw_math(w_ref[...], g, m_ref[...], v_ref[...])
        g_ref[...], d_ref[...], nm_ref[...], nv_ref[...] = g, d, nm, nv

    blk = pl.BlockSpec((tr, cols), lambda i: (i, 0))
    return _call(body, name, [jax.ShapeDtypeStruct((rows, cols), F32)] * 4, grid=(rows // tr,),
                 in_specs=[pl.BlockSpec((n, tr, cols), lambda i: (0, i, 0)), blk, blk, blk],
                 out_specs=[blk] * 4, dims=("parallel",))(parts, w, m, v)


_WEIGHTS = ("norm_g", "mem_norm_g", "w_mem_kv", "w_out", "conv_w_in", "conv_dw", "conv_dw_b", "conv_ln_g", "conv_ln_b",
            "mla_w_in", "mla_q_norm_g", "mla_w_uq", "mla_kv_norm_g", "mla_w_ukv", "final_norm_g")


_GATHER_GROUPS = {"a": ("conv_w_in",), "b": ("w_mem_kv", "w_out"), "c": ("mla_w_in", "mla_w_uq", "mla_w_ukv")}
_CARRIERS = {"l0_norm": ("gather chips", ("a",)), "l0_in": ("gather chips", ("b",)), "l0_glu": ("gather sibling", ("b",)),
             "l0_dwconv": ("gather chips", ("c",)), "l0_ln": ("gather sibling", ("c",)),
             "l1_in_dx": ("reduce sibling", ("l1",)), "l0_ln_bwd": ("reduce sibling", ("l0a",)),
             "l0_dwconv_bwd": ("reduce chips", ("l1", "l0a")), "l0_in_dx": ("reduce sibling", ("l0b",)),
             "l0_norm_bwd": ("reduce chips", ("l0b",))}


class _Schedule:
    def __init__(self, w):
        self.w, self.full, self.gather, self.reduce, self.reduced = w, {}, {}, {}, {}
        small = _all_gather_small(_as_tiles([w[n] for n in _SMALL_SHARDED]), "gather_small_weights").reshape(N_DEV, -1)
        o = 0
        for n in _SMALL_SHARDED:
            self.full[n] = _join(n, small[:, o:o + w[n].size].reshape((N_DEV,) + w[n].shape))
            o += w[n].size
        for n in _REPLICATED:
            self.full[n] = w[n]

    def carry(self, call):
        kind, groups = _CARRIERS[call]
        stages = []
        for g in groups:
            if kind == "gather chips":
                self.gather[g] = [_gather_chips_stage([self.w[n].astype(BF16) for n in _GATHER_GROUPS[g]])]
                stages.append(self.gather[g][0])
            elif kind == "gather sibling":
                self.gather[g].append(_gather_sibling_stage(self.gather[g][0].outs))
                stages.append(self.gather[g][1])
            elif kind == "reduce sibling":
                self.reduce[g]["sibling"] = _reduce_sibling_stage(self.reduce[g]["cut"])
                stages.append(self.reduce[g]["sibling"])
            else:
                r = self.reduce[g]
                partial = [_add_own(c, s, "reduce_add_%s_%d" % (g, i)) for i, (c, s) in enumerate(zip(r["cut"], r["sibling"].outs))]
                r["chips"] = _reduce_chips_stage(partial)
                stages.append(r["chips"])
        return stages[0] if len(stages) == 1 else _merge_stages(stages[0], stages[1])

    def __getitem__(self, name):
        if name not in self.full:
            g = [k for k, names in _GATHER_GROUPS.items() if name in names][0]
            if len(self.gather[g]) == 1:
                self.gather[g].append(_gather_sibling_stage(self.gather[g][0].outs))
                _run_stage(self.gather[g][1], "gather_sibling_" + g)
            for n, buf in zip(_GATHER_GROUPS[g], self.gather[g][1].outs):
                self.full[n] = _PERM[n][0](_join(n, buf)) if n in _PERM else _join(n, buf)
        return self.full[name]

    def ready(self, group, grads):
        keys, cut, small = [], [], []
        for (n, layer), g in grads.items():
            if n in _SMALL_SHARDED:
                small.append(_cut(n, g, self.w[n].shape).reshape(N_DEV, -1))
                continue
            keys.append((n, layer))
            if layer is not None:
                cut.append(g.reshape((N_DEV,) + self.w[n].shape[1:]))
            else:
                cut.append(_cut(n, _PERM[n][1](g) if n in _PERM else g, self.w[n].shape))
        if small:
            keys.append(("small", None))
            cut.append(jax.vmap(lambda r: _as_tiles([r]))(jnp.concatenate(small, axis=1)))
        self.reduce[group] = {"keys": keys, "cut": cut}

    def finish(self):
        out = {}
        for r in self.reduce.values():
            out.update(dict(zip(r["keys"], r["chips"].outs)))
        return out


def kernel(x, mem, positions, norm_g, mem_norm_g, w_mem_kv, w_out, conv_w_in, conv_dw, conv_dw_b, conv_ln_g, conv_ln_b, mla_w_in, mla_q_norm_g, mla_w_uq, mla_kv_norm_g, mla_w_ukv, final_norm_g, loss_target, m_norm_g, m_mem_norm_g, m_w_mem_kv, m_w_out, m_conv_w_in, m_conv_dw, m_conv_dw_b, m_conv_ln_g, m_conv_ln_b, m_mla_w_in, m_mla_q_norm_g, m_mla_w_uq, m_mla_kv_norm_g, m_mla_w_ukv, m_final_norm_g, v_norm_g, v_mem_norm_g, v_w_mem_kv, v_w_out, v_conv_w_in, v_conv_dw, v_conv_dw_b, v_conv_ln_g, v_conv_ln_b, v_mla_w_in, v_mla_q_norm_g, v_mla_w_uq, v_mla_kv_norm_g, v_mla_w_ukv, v_final_norm_g):
    w = dict(zip(_WEIGHTS, (norm_g, mem_norm_g, w_mem_kv, w_out, conv_w_in, conv_dw, conv_dw_b, conv_ln_g, conv_ln_b,
                            mla_w_in, mla_q_norm_g, mla_w_uq, mla_kv_norm_g, mla_w_ukv, final_norm_g)))
    m = dict(zip(_WEIGHTS, (m_norm_g, m_mem_norm_g, m_w_mem_kv, m_w_out, m_conv_w_in, m_conv_dw, m_conv_dw_b, m_conv_ln_g,
                            m_conv_ln_b, m_mla_w_in, m_mla_q_norm_g, m_mla_w_uq, m_mla_kv_norm_g, m_mla_w_ukv, m_final_norm_g)))
    v = dict(zip(_WEIGHTS, (v_norm_g, v_mem_norm_g, v_w_mem_kv, v_w_out, v_conv_w_in, v_conv_dw, v_conv_dw_b, v_conv_ln_g,
                            v_conv_ln_b, v_mla_w_in, v_mla_q_norm_g, v_mla_w_uq, v_mla_kv_norm_g, v_mla_w_ukv, v_final_norm_g)))

    sched = _Schedule(w)
    loss_local, dx, G = _forward_backward(x, mem, positions, loss_target, sched)
    loss = lax.psum(loss_local, ("x", "y", "c"))

    from_chips = sched.finish()
    out = [{}, {}, {}, {}]
    for n in _BIG:
        if n in _ROW_CUT:
            res = [_sum_adamw(from_chips[(n, l)], w[n][l], m[n][l], v[n][l], "adamw_%s_%d" % (n, l)) for l in range(w[n].shape[0])]
            res = [jnp.stack(r) for r in zip(*res)]
        else:
            rows, cols = _rows2d(w[n].shape)
            res = _sum_adamw(from_chips[(n, None)], w[n].reshape(rows, cols), m[n].reshape(rows, cols),
                             v[n].reshape(rows, cols), "adamw_" + n)
        for o, r in zip(out, res):
            o[n] = r.reshape(w[n].shape)
    small_like = [w[n] for n in _SMALL_SHARDED]
    res = _sum_adamw(from_chips[("small", None)], _as_tiles(small_like), _as_tiles([m[n] for n in _SMALL_SHARDED]),
                     _as_tiles([v[n] for n in _SMALL_SHARDED]), "adamw_small")
    for o, r in zip(out, res):
        for n, a in zip(_SMALL_SHARDED, _split_flat(r.reshape(-1), small_like)):
            o[n] = a

    rep_like = [w[n] for n in _REPLICATED]
    rep_parts = _all_gather_small(_as_tiles([G[n] for n in _REPLICATED]), "gather_replicated_grads")
    res = _sum_adamw(rep_parts, _as_tiles(rep_like), _as_tiles([m[n] for n in _REPLICATED]),
                     _as_tiles([v[n] for n in _REPLICATED]), "adamw_replicated")
    for o, r in zip(out, res):
        for n, a in zip(_REPLICATED, _split_flat(r.reshape(-1), rep_like)):
            o[n] = a

    return (loss, dx, *[out[0][n] for n in _WEIGHTS], *[out[1][n] for n in _WEIGHTS],
            *[out[2][n] for n in _WEIGHTS], *[out[3][n] for n in _WEIGHTS])
```

```python
import jax
import jax.numpy as jnp
from jax import lax
from jax.experimental import pallas as pl
from jax.experimental.pallas import tpu as pltpu

F32 = jnp.float32
BF16 = jnp.bfloat16
MESH = pl.DeviceIdType.MESH
N_DEV = 8
VMEM_LIMIT_BYTES = 48 * 1024 * 1024

MEM_HEADS, MEM_HEAD_DIM = 4, 128
MEM_WIDTH = MEM_HEADS * MEM_HEAD_DIM
CONV_KERNEL = 31
CONV_PAD = 32
MLA_HEADS, MLA_NOPE, MLA_ROPE, MLA_V = 12, 128, 64, 128
MLA_QK = MLA_NOPE + MLA_ROPE
HALF_ROPE = MLA_ROPE // 2
Q_RANK, KV_RANK = 512, 256
ROPE_THETA = 10000.0
RMS_EPS = 1e-6
LN_EPS = 1e-5
ADAM_LR, ADAM_B1, ADAM_B2, ADAM_EPS, ADAM_WD, ADAM_STEP = 0.001, 0.9, 0.999, 1e-08, 0.01, 10
NEG = -1e30


class _Stage:
    def __init__(self, ins, out_shapes, sems, start, wait, aliases=None):
        self.ins, self.out_shapes, self.sems = list(ins), list(out_shapes), list(sems)
        self.start, self.wait, self.aliases, self.outs = start, wait, dict(aliases or {}), None


def _merge_stages(a, b):
    na_i, na_o, na_s = len(a.ins), len(a.out_shapes), len(a.sems)

    def start(i, o, s):
        a.start(i[:na_i], o[:na_o], s[:na_s])
        b.start(i[na_i:], o[na_o:], s[na_s:])

    def wait(i, o, s):
        a.wait(i[:na_i], o[:na_o], s[:na_s])
        b.wait(i[na_i:], o[na_o:], s[na_s:])

    aliases = dict(a.aliases)
    aliases.update({na_i + k: na_o + v for k, v in b.aliases.items()})
    merged = _Stage(a.ins + b.ins, a.out_shapes + b.out_shapes, a.sems + b.sems, start, wait, aliases)
    merged.parts = (a, b)
    return merged


def _deliver(stage, outs):
    stage.outs = list(outs)
    if hasattr(stage, "parts"):
        a, b = stage.parts
        _deliver(a, outs[:len(a.out_shapes)])
        _deliver(b, outs[len(a.out_shapes):])


def _call(body, name, out_shape, grid=None, in_specs=None, out_specs=None, scratch=(), dims=None, grid_spec=None, aliases=None,
          carry=None):
    params = dict(vmem_limit_bytes=VMEM_LIMIT_BYTES)
    if dims is not None:
        params["dimension_semantics"] = dims
    kw = {}
    if carry is not None:
        single = not isinstance(out_shape, (list, tuple))
        main_out = [out_shape] if single else list(out_shape)
        main_specs = [out_specs] if single else list(out_specs)
        n_in, n_out, n_scr = len(in_specs), len(main_out), len(scratch)
        x_in, x_out = len(carry.ins), len(carry.out_shapes)
        inner, steps = body, tuple(grid)

        def body(*refs):
            ins, xin = refs[:n_in], refs[n_in:n_in + x_in]
            outs = refs[n_in + x_in:n_in + x_in + n_out]
            xout = refs[n_in + x_in + n_out:n_in + x_in + n_out + x_out]
            scr = refs[n_in + x_in + n_out + x_out:n_in + x_in + n_out + x_out + n_scr]
            xsem = refs[n_in + x_in + n_out + x_out + n_scr:]
            ids = [pl.program_id(a) for a in range(len(steps))]
            first, last = ids[0] == 0, ids[0] == steps[0] - 1
            for a in range(1, len(steps)):
                first = jnp.logical_and(first, ids[a] == 0)
                last = jnp.logical_and(last, ids[a] == steps[a] - 1)
            pl.when(first)(lambda: carry.start(xin, xout, xsem))
            inner(*ins, *outs, *scr)
            pl.when(last)(lambda: carry.wait(xin, xout, xsem))

        hbm = pl.BlockSpec(memory_space=pltpu.HBM)
        aliases = dict(aliases or {})
        aliases.update({n_in + k: n_out + v for k, v in carry.aliases.items()})
        res = _call(body, name, main_out + carry.out_shapes, grid=grid, in_specs=list(in_specs) + [hbm] * x_in,
                    out_specs=main_specs + [hbm] * x_out, scratch=list(scratch) + carry.sems, dims=dims, aliases=aliases)

        def run(*args):
            outs = res(*args, *carry.ins)
            _deliver(carry, outs[n_out:])
            return outs[0] if single else outs[:n_out]

        return run
    if aliases:
        kw["input_output_aliases"] = aliases
    if grid_spec is not None:
        kw["grid_spec"] = grid_spec
    else:
        if grid is not None:
            kw["grid"] = grid
        kw["in_specs"] = in_specs
        kw["out_specs"] = out_specs
        kw["scratch_shapes"] = list(scratch)
    return pl.pallas_call(body, name=name, out_shape=out_shape, compiler_params=pltpu.CompilerParams(**params), **kw)


def _pick(n, target, mult):
    best = None
    for d in range(mult, min(n, target) + 1, mult):
        if n % d == 0:
            best = d
    return n if best is None else best


_DOT_DIMS = {"nn": (((1,), (0,)), ((), ())), "nt": (((1,), (1,)), ((), ())), "tn": (((0,), (0,)), ((), ()))}


def _mm(a, b, mode, out_dtype, name, res=None, carry=None):
    if mode == "tn":
        a, mode = a.T, "nn"
    if mode == "nn":
        (M, K), N = a.shape, b.shape[1]
    else:
        (M, K), N = a.shape, b.shape[0]
    tm = _pick(M, 1024, 8)
    tn = _pick(N, 512, 128)
    tk = _pick(K, 1024, 128)
    nk = K // tk
    has_res = res is not None

    def body(*refs):
        if has_res:
            a_ref, b_ref, r_ref, o_ref, acc = refs
        else:
            a_ref, b_ref, o_ref, acc = refs
        k = pl.program_id(2)

        @pl.when(k == 0)
        def _():
            acc[...] = jnp.zeros_like(acc)

        acc[...] += lax.dot_general(a_ref[...].astype(BF16), b_ref[...].astype(BF16), _DOT_DIMS[mode],
                                    preferred_element_type=F32)

        @pl.when(k == nk - 1)
        def _():
            r = acc[...]
            if has_res:
                r = r + r_ref[...]
            o_ref[...] = r.astype(o_ref.dtype)

    a_spec = pl.BlockSpec((tm, tk), lambda i, j, k: (i, k))
    b_spec = {"nn": pl.BlockSpec((tk, tn), lambda i, j, k: (k, j)),
              "nt": pl.BlockSpec((tn, tk), lambda i, j, k: (j, k))}[mode]
    o_spec = pl.BlockSpec((tm, tn), lambda i, j, k: (i, j))
    in_specs = [a_spec, b_spec] + ([o_spec] if has_res else [])
    args = (a, b) + ((res,) if has_res else ())
    return _call(body, name, jax.ShapeDtypeStruct((M, N), out_dtype), grid=(M // tm, N // tn, nk),
                 in_specs=in_specs, out_specs=o_spec, scratch=[pltpu.VMEM((tm, tn), F32)],
                 dims=("parallel", "parallel", "arbitrary"), carry=carry)(*args)


def _views(rows):
    return [r if isinstance(r, tuple) else (r, r.shape[1], 0) for r in rows]


def _rowwise(f, rows, params, outs, name, tb=256, carry=None):
    rows = _views(rows)
    T = rows[0][0].shape[0]
    tb = min(tb, T)
    nr, npar = len(rows), len(params)
    outs = [o if len(o) == 3 else (o[0], o[1], o[0]) for o in outs]

    def body(*refs):
        vals = f(*[r[...].astype(F32) for r in refs[:nr]], *[p[...] for p in refs[nr:nr + npar]])
        for o_ref, v in zip(refs[nr + npar:], vals):
            o_ref[...] = v.astype(o_ref.dtype)

    row_spec = lambda w, cb=0: pl.BlockSpec((tb, w), lambda i: (i, cb))
    par_spec = lambda w: pl.BlockSpec((1, w), lambda i: (0, 0))
    res = _call(body, name, [jax.ShapeDtypeStruct((T, tw), dt) for _, dt, tw in outs], grid=(T // tb,),
                in_specs=[row_spec(w, cb) for _, w, cb in rows] + [par_spec(p.shape[1]) for p in params],
                out_specs=[row_spec(w) for w, _, _ in outs], dims=("parallel",), carry=carry)(*[r[0] for r in rows], *params)
    return res


def _rowwise_bwd(f, rows, params, douts, n_diff, name, tb=256, carry=None):
    rows, douts = _views(rows), _views(douts)
    T = rows[0][0].shape[0]
    tb = min(tb, T)
    nr, npar, nd = len(rows), len(params), len(douts)

    def body(*refs):
        rv = [r[...].astype(F32) for r in refs[:nr]]
        pv = [p[...] for p in refs[nr:nr + npar]]
        dv = [d[...].astype(F32) for d in refs[nr + npar:nr + npar + nd]]
        o_refs = refs[nr + npar + nd:]
        fixed = rv[n_diff:]

        def g(*xs):
            return tuple(f(*xs[:n_diff], *fixed, *xs[n_diff:]))

        _, vjp = jax.vjp(g, *rv[:n_diff], *pv)
        grads = vjp(tuple(dv))
        for o_ref, gr in zip(o_refs[:n_diff], grads[:n_diff]):
            o_ref[...] = gr.astype(o_ref.dtype)
        first = pl.program_id(0) == 0
        for o_ref, gr in zip(o_refs[n_diff:], grads[n_diff:]):
            @pl.when(first)
            def _(o_ref=o_ref):
                o_ref[...] = jnp.zeros_like(o_ref)

            o_ref[...] += gr

    row_spec = lambda w, cb=0: pl.BlockSpec((tb, w), lambda i: (i, cb))
    par_spec = lambda w: pl.BlockSpec((1, w), lambda i: (0, 0))
    out_shape = ([jax.ShapeDtypeStruct((T, w), F32) for _, w, _ in rows[:n_diff]]
                 + [jax.ShapeDtypeStruct((1, p.shape[1]), F32) for p in params])
    return _call(body, name, out_shape, grid=(T // tb,),
                 in_specs=([row_spec(w, cb) for _, w, cb in rows] + [par_spec(p.shape[1]) for p in params]
                           + [row_spec(w, cb) for _, w, cb in douts]),
                 out_specs=([row_spec(w) for _, w, _ in rows[:n_diff]] + [par_spec(p.shape[1]) for p in params]),
                 dims=("arbitrary",), carry=carry)(*[r[0] for r in rows], *params, *[d[0] for d in douts])


def _sig(x):
    return 1.0 / (1.0 + jnp.exp(-x))


def _rms(x, g):
    return x * lax.rsqrt(jnp.mean(x * x, axis=-1, keepdims=True) + RMS_EPS) * g


def _f_rms(x, g):
    return (_rms(x, g),)


def _f_glu(a, gate):
    return (a * _sig(gate),)


def _f_ln_silu(x, g, b):
    mu = jnp.mean(x, axis=-1, keepdims=True)
    xc = x - mu
    var = jnp.mean(xc * xc, axis=-1, keepdims=True)
    y = xc * lax.rsqrt(var + LN_EPS) * g + b
    return (y * _sig(y),)


def _rope128(x, cos_p, sin_p):
    return x * cos_p + pltpu.roll(x, 64, 1) * sin_p


def _rope128_t(d, cos_p, sin_p):
    return d * cos_p + pltpu.roll(d * sin_p, 64, 1)


def _f_rope(xq, xk, cos_p, sin_p):
    heads = [_rope128(xq[:, h * 128:(h + 1) * 128], cos_p, sin_p) for h in range(MLA_HEADS)]
    return (jnp.concatenate(heads, axis=1), _rope128(xk, cos_p, sin_p))


def _f_rope_t(dq, dk_heads, cos_p, sin_p):
    heads = [_rope128_t(dq[:, h * 128:(h + 1) * 128], cos_p, sin_p) for h in range(MLA_HEADS)]
    dk = dk_heads[:, 0:128]
    for h in range(1, MLA_HEADS):
        dk = dk + dk_heads[:, h * 128:(h + 1) * 128]
    return (jnp.concatenate(heads, axis=1), _rope128_t(dk, cos_p, sin_p))


GATE_LANES = 256


def _gate_fwd(ycat, proj, z_col, name, tb=1024):
    T, width = ycat.shape
    zb = z_col // GATE_LANES

    def body(y_ref, z_ref, o_ref):
        z = z_ref[...]
        o_ref[...] = (y_ref[...] * (z * _sig(z))).astype(o_ref.dtype)

    blk = pl.BlockSpec((tb, GATE_LANES), lambda i, c: (i, c))
    return _call(body, name, jax.ShapeDtypeStruct((T, width), BF16), grid=(T // tb, width // GATE_LANES),
                 in_specs=[blk, pl.BlockSpec((tb, GATE_LANES), lambda i, c: (i, zb + c))], out_specs=blk,
                 dims=("parallel", "parallel"))(ycat, proj)


def _gate_bwd(ycat, proj, z_col, dy, name, tb=1024):
    T, width = ycat.shape
    zb = z_col // GATE_LANES

    def body(y_ref, z_ref, dy_ref, dycat_ref, dz_ref):
        z, d = z_ref[...], dy_ref[...]
        s = _sig(z)
        dycat_ref[...] = d * (z * s)
        dz_ref[...] = d * y_ref[...] * (s * (1.0 + z * (1.0 - s)))

    blk = pl.BlockSpec((tb, GATE_LANES), lambda i, c: (i, c))
    return _call(body, name, [jax.ShapeDtypeStruct((T, width), F32)] * 2, grid=(T // tb, width // GATE_LANES),
                 in_specs=[blk, pl.BlockSpec((tb, GATE_LANES), lambda i, c: (i, zb + c)), blk], out_specs=[blk, blk],
                 dims=("parallel", "parallel"))(ycat, proj, dy)


def _final_loss(h, tgt, g, name, tb=256):
    T, D = h.shape

    def body(h_ref, t_ref, g_ref, dh_ref, dg_ref, loss_ref):
        tv = t_ref[...]

        def rowloss(hh, gg):
            e = _rms(hh, gg) - tv
            return 0.5 * jnp.mean(e * e, axis=-1, keepdims=True)

        lr, vjp = jax.vjp(rowloss, h_ref[...], g_ref[...])
        dh, dg = vjp(jnp.ones_like(lr))
        dh_ref[...] = dh

        @pl.when(pl.program_id(0) == 0)
        def _():
            dg_ref[...] = jnp.zeros_like(dg_ref)
            loss_ref[...] = jnp.zeros_like(loss_ref)

        dg_ref[...] += dg
        loss_ref[...] += jnp.broadcast_to(jnp.sum(lr, axis=0, keepdims=True), loss_ref.shape)

    row = pl.BlockSpec((tb, D), lambda i: (i, 0))
    par = pl.BlockSpec((1, D), lambda i: (0, 0))
    return _call(body, name,
                 [jax.ShapeDtypeStruct((T, D), F32), jax.ShapeDtypeStruct((1, D), F32), jax.ShapeDtypeStruct((1, 128), F32)],
                 grid=(T // tb,), in_specs=[row, row, par],
                 out_specs=[row, par, pl.BlockSpec((1, 128), lambda i: (0, 0))], dims=("arbitrary",))(h, tgt, g)


CONV_ROWS = 128
CONV_LANES = 256


def _dwconv_fwd(x, w, b, name, carry=None):
    B, S, C = x.shape
    cb = CONV_LANES
    off = CONV_PAD - (CONV_KERNEL - 1)

    def body(x_ref, w_ref, b_ref, o_ref, pad):
        pad[0:CONV_PAD, :] = jnp.zeros((CONV_PAD, cb), F32)
        pad[CONV_PAD:, :] = x_ref[...]
        for t0 in range(0, S, CONV_ROWS):
            acc = jnp.broadcast_to(b_ref[...], (CONV_ROWS, cb))
            for k in range(CONV_KERNEL):
                acc = acc + w_ref[k:k + 1, :] * pad[t0 + off + k:t0 + off + k + CONV_ROWS, :]
            o_ref[t0:t0 + CONV_ROWS, :] = acc

    return _call(body, name, jax.ShapeDtypeStruct((B, S, C), F32), grid=(B, C // cb),
                 in_specs=[pl.BlockSpec((None, S, cb), lambda i, j: (i, 0, j)),
                           pl.BlockSpec((CONV_KERNEL, cb), lambda i, j: (0, j)),
                           pl.BlockSpec((1, cb), lambda i, j: (0, j))],
                 out_specs=pl.BlockSpec((None, S, cb), lambda i, j: (i, 0, j)),
                 scratch=[pltpu.VMEM((S + CONV_PAD, cb), F32)], dims=("parallel", "parallel"), carry=carry)(x, w, b)


def _dwconv_bwd(x, w, dy, name, carry=None):
    B, S, C = x.shape
    cb = CONV_LANES
    off = CONV_PAD - (CONV_KERNEL - 1)
    groups = CONV_ROWS // 8

    def body(x_ref, w_ref, dy_ref, dx_ref, dw_ref, db_ref, xpad, dypad, wacc):
        xpad[0:CONV_PAD, :] = jnp.zeros((CONV_PAD, cb), F32)
        xpad[CONV_PAD:, :] = x_ref[...]
        dypad[0:S, :] = dy_ref[...]
        dypad[S:, :] = jnp.zeros((CONV_PAD, cb), F32)
        wacc[...] = jnp.zeros_like(wacc)
        for t0 in range(0, S, CONV_ROWS):
            dyc = dy_ref[t0:t0 + CONV_ROWS, :]
            acc = jnp.zeros((CONV_ROWS, cb), F32)
            for k in range(CONV_KERNEL):
                acc = acc + w_ref[k:k + 1, :] * dypad[t0 + (CONV_KERNEL - 1) - k:t0 + (CONV_KERNEL - 1) - k + CONV_ROWS, :]
                prod = dyc * xpad[t0 + off + k:t0 + off + k + CONV_ROWS, :]
                wacc[k] += jnp.sum(prod.reshape(groups, 8, cb), axis=0)
            wacc[CONV_KERNEL] += jnp.sum(dyc.reshape(groups, 8, cb), axis=0)
            dx_ref[t0:t0 + CONV_ROWS, :] = acc

        @pl.when(pl.program_id(1) == 0)
        def _():
            dw_ref[...] = jnp.zeros_like(dw_ref)
            db_ref[...] = jnp.zeros_like(db_ref)

        for k in range(CONV_KERNEL):
            dw_ref[k:k + 1, :] += jnp.sum(wacc[k], axis=0, keepdims=True)
        db_ref[...] += jnp.sum(wacc[CONV_KERNEL], axis=0, keepdims=True)

    blk = pl.BlockSpec((None, S, cb), lambda j, i: (i, 0, j))
    return _call(body, name,
                 [jax.ShapeDtypeStruct((B, S, C), F32), jax.ShapeDtypeStruct((CONV_KERNEL, C), F32),
                  jax.ShapeDtypeStruct((1, C), F32)],
                 grid=(C // cb, B),
                 in_specs=[blk, pl.BlockSpec((CONV_KERNEL, cb), lambda j, i: (0, j)), blk],
                 out_specs=[blk, pl.BlockSpec((CONV_KERNEL, cb), lambda j, i: (0, j)),
                            pl.BlockSpec((1, cb), lambda j, i: (0, j))],
                 scratch=[pltpu.VMEM((S + CONV_PAD, cb), F32), pltpu.VMEM((S + CONV_PAD, cb), F32),
                          pltpu.VMEM((CONV_KERNEL + 1, 8, cb), F32)],
                 dims=("parallel", "arbitrary"), carry=carry)(x, w, dy)


ATTN_TILE = 512
ATTN_SUB = 256


def _attn_shapes(Sq, Sk, causal):
    tq = min(Sq, ATTN_TILE)
    tk = tq if causal else min(Sk, ATTN_TILE)
    return tq, tk, min(ATTN_SUB, tq)


def _mask(row0, col0, rows, cols):
    r = row0 + lax.broadcasted_iota(jnp.int32, (rows, cols), 0)
    c = col0 + lax.broadcasted_iota(jnp.int32, (rows, cols), 1)
    return c <= r


def _attn_fwd(q, q_c0, qr, k, k_c0, kr, v, v_c0, B, Sq, Sk, H, causal, scale, name, into=None, o_c0=0, o_width=None):
    tq, tk, sub = _attn_shapes(Sq, Sk, causal)
    nq, nk, nsub = Sq // tq, Sk // tk, tq // sub
    rope = qr is not None

    def body(*refs):
        refs = list(refs)
        qn_ref = refs.pop(0)
        qr_ref = refs.pop(0) if rope else None
        kn_ref = refs.pop(0)
        kr_ref = refs.pop(0) if rope else None
        v_ref = refs.pop(0)
        if into is not None:
            refs.pop(0)
        o_ref, lse_ref, m_s, l_s, acc = refs
        qi = pl.program_id(2)
        m_s[...] = jnp.full_like(m_s, NEG)
        l_s[...] = jnp.zeros_like(l_s)
        acc[...] = jnp.zeros_like(acc)
        qs = []
        for r in range(nsub):
            qn = qn_ref[r * sub:(r + 1) * sub, :].astype(BF16)
            qs.append(jnp.concatenate([qn, qr_ref[r * sub:(r + 1) * sub, :]], axis=1) if rope else qn)

        def step(j, masked):
            ks = pl.ds(pl.multiple_of(j * tk, tk), tk)
            kk = jnp.concatenate([kn_ref[ks, :], kr_ref[ks, :]], axis=1) if rope else kn_ref[ks, :]
            vv = v_ref[ks, :]
            for r in range(nsub):
                rows = slice(r * sub, (r + 1) * sub)
                s = lax.dot_general(qs[r], kk, _DOT_DIMS["nt"], preferred_element_type=F32) * scale
                if masked:
                    s = jnp.where(_mask(qi * tq + r * sub, j * tk, sub, tk), s, NEG)
                m_old = m_s[rows, :]
                m_new = jnp.maximum(m_old, jnp.max(s, axis=-1, keepdims=True))
                p = jnp.exp(s - m_new)
                alpha = jnp.exp(m_old - m_new)
                l_s[rows, :] = alpha * l_s[rows, :] + jnp.sum(p, axis=-1, keepdims=True)
                acc[rows, :] = alpha * acc[rows, :] + jnp.dot(p.astype(BF16), vv, preferred_element_type=F32)
                m_s[rows, :] = m_new

        def unmasked(j, carry):
            step(j, False)
            return carry

        if causal:
            lax.fori_loop(0, qi, unmasked, 0)
            step(qi, True)
        else:
            lax.fori_loop(0, nk, unmasked, 0)
        o_ref[...] = (acc[...] / l_s[...]).astype(o_ref.dtype)
        lse_ref[...] = m_s[...] + jnp.log(l_s[...])

    qspec = lambda c0: pl.BlockSpec((tq, 128), lambda b, h, i: (b * nq + i, c0 + h))
    kspec = lambda c0: pl.BlockSpec((Sk, 128), lambda b, h, i: (b, c0 + h))
    in_specs, args = [qspec(q_c0)], [q]
    if rope:
        in_specs.append(qspec(0)); args.append(qr)
    in_specs.append(kspec(k_c0)); args.append(k)
    if rope:
        in_specs.append(pl.BlockSpec((Sk, 128), lambda b, h, i: (b, 0))); args.append(kr)
    in_specs.append(kspec(v_c0)); args.append(v)
    aliases = {}
    if into is not None:
        aliases = {len(args): 0}
        in_specs.append(pl.BlockSpec(memory_space=pl.ANY)); args.append(into)
        o_shape = jax.ShapeDtypeStruct(into.shape, into.dtype)
    else:
        o_shape = jax.ShapeDtypeStruct((B * Sq, o_width), F32)
    return _call(body, name, [o_shape, jax.ShapeDtypeStruct((B * H, Sq, 1), F32)], grid=(B, H, nq), in_specs=in_specs,
                 out_specs=[qspec(o_c0), pl.BlockSpec((None, tq, 1), lambda b, h, i: (b * H + h, i, 0))],
                 scratch=[pltpu.VMEM((tq, 1), F32), pltpu.VMEM((tq, 1), F32), pltpu.VMEM((tq, 128), F32)],
                 dims=("parallel", "parallel", "arbitrary"), aliases=aliases)(*args)


def _attn_bwd(q, q_c0, qr, k, k_c0, kr, v, v_c0, o, do, o_c0, lse, B, Sq, Sk, H, causal, scale, name, dq_width=None):
    tq, tk, sub = _attn_shapes(Sq, Sk, causal)
    nq, nk, nsub = Sq // tq, Sk // tk, tq // sub
    rope = qr is not None
    dk_w = 256 if rope else 128

    def body(*refs):
        refs = list(refs)
        qn_ref = refs.pop(0)
        qr_ref = refs.pop(0) if rope else None
        kn_ref = refs.pop(0)
        kr_ref = refs.pop(0) if rope else None
        v_ref, o_ref, do_ref, lse_ref = refs[:4]
        refs = refs[4:]
        dqn_ref = refs.pop(0)
        dqr_ref = refs.pop(0) if rope else None
        dkn_ref = refs.pop(0)
        dkr_ref = refs.pop(0) if rope else None
        dv_ref, q_s, do_s, dl_s, dq_acc, dk_acc, dv_acc = refs
        kj = pl.program_id(2)

        @pl.when(kj == 0)
        def _():
            qn = qn_ref[...].astype(BF16)
            q_s[...] = jnp.concatenate([qn, qr_ref[...]], axis=1) if rope else qn
            dof = do_ref[...]
            do_s[...] = dof.astype(BF16)
            dl_s[...] = jnp.sum(dof * o_ref[...], axis=-1, keepdims=True)
            dq_acc[...] = jnp.zeros_like(dq_acc)

        kk = jnp.concatenate([kn_ref[...], kr_ref[...]], axis=1) if rope else kn_ref[...]
        vv = v_ref[...]
        dk_acc[...] = jnp.zeros_like(dk_acc)
        dv_acc[...] = jnp.zeros_like(dv_acc)

        def step(i, masked):
            for r in range(nsub):
                rows = pl.ds(pl.multiple_of(i * tq + r * sub, sub), sub)
                qq, dob = q_s[rows, :], do_s[rows, :]
                s = lax.dot_general(qq, kk, _DOT_DIMS["nt"], preferred_element_type=F32) * scale
                if masked:
                    s = jnp.where(_mask(i * tq + r * sub, kj * tk, sub, tk), s, NEG)
                p = jnp.exp(s - lse_ref[rows, :])
                dp = lax.dot_general(dob, vv, _DOT_DIMS["nt"], preferred_element_type=F32)
                ds = (p * (dp - dl_s[rows, :]) * scale).astype(BF16)
                dv_acc[...] += lax.dot_general(p.astype(BF16), dob, _DOT_DIMS["tn"], preferred_element_type=F32)
                dk_acc[...] += lax.dot_general(ds, qq, _DOT_DIMS["tn"], preferred_element_type=F32)
                dq_acc[rows, :] += jnp.dot(ds, kk, preferred_element_type=F32)

        def unmasked(i, carry):
            step(i, False)
            return carry

        if causal:
            step(kj, True)
            lax.fori_loop(kj + 1, nq, unmasked, 0)
        else:
            lax.fori_loop(0, nq, unmasked, 0)
        dkn_ref[...] = dk_acc[:, 0:128]
        if rope:
            dkr_ref[...] = dk_acc[:, 128:256]
        dv_ref[...] = dv_acc[...]

        @pl.when(kj == nk - 1)
        def _():
            dqn_ref[...] = dq_acc[:, 0:128]
            if rope:
                dqr_ref[...] = dq_acc[:, 128:256]

    qspec = lambda c0: pl.BlockSpec((Sq, 128), lambda b, h, j: (b, c0 + h))
    kspec = lambda c0: pl.BlockSpec((tk, 128), lambda b, h, j: (b * nk + j, c0 + h))
    in_specs, args = [qspec(q_c0)], [q]
    if rope:
        in_specs.append(qspec(0)); args.append(qr)
    in_specs.append(kspec(k_c0)); args.append(k)
    if rope:
        in_specs.append(pl.BlockSpec((tk, 128), lambda b, h, j: (b * nk + j, 0))); args.append(kr)
    in_specs += [kspec(v_c0), qspec(o_c0), qspec(o_c0), pl.BlockSpec((None, Sq, 1), lambda b, h, j: (b * H + h, 0, 0))]
    args += [v, o, do, lse]
    q_rows = jax.ShapeDtypeStruct((B * Sq, dq_width or H * 128), F32)
    h_rows_q = jax.ShapeDtypeStruct((B * Sq, H * 128), F32)
    h_rows_k = jax.ShapeDtypeStruct((B * Sk, H * 128), F32)
    out_shape, out_specs = [q_rows], [qspec(0)]
    if rope:
        out_shape.append(h_rows_q); out_specs.append(qspec(0))
    out_shape.append(h_rows_k); out_specs.append(kspec(0))
    if rope:
        out_shape.append(h_rows_k); out_specs.append(kspec(0))
    out_shape.append(h_rows_k); out_specs.append(kspec(0))
    return _call(body, name, out_shape, grid=(B, H, nk), in_specs=in_specs, out_specs=out_specs,
                 scratch=[pltpu.VMEM((Sq, dk_w), BF16), pltpu.VMEM((Sq, 128), BF16), pltpu.VMEM((Sq, 1), F32),
                          pltpu.VMEM((Sq, dk_w), F32), pltpu.VMEM((tk, dk_w), F32), pltpu.VMEM((tk, 128), F32)],
                 dims=("parallel", "parallel", "arbitrary"))(*args)


def _mem_attention_fwd(proj, q_col, ycat, mem2, mem_g, w_mem, B, S, tag):
    M = mem2.shape[0] // B
    (memn,) = _rowwise(_f_rms, [mem2], [mem_g], [(mem2.shape[1], BF16)], tag + "_memnorm")
    kvm = _mm(memn, w_mem, "nn", BF16, tag + "_memkv")
    o_c0 = ycat.shape[1] // 128 - MEM_HEADS
    ycat, lse = _attn_fwd(proj, q_col // 128, None, kvm, 0, None, kvm, MEM_HEADS, B, S, M, MEM_HEADS, False,
                          MEM_HEAD_DIM ** -0.5, tag + "_memattn", into=ycat, o_c0=o_c0)
    return ycat, (memn, kvm, lse)


def _mem_attention_bwd(proj, q_col, ycat, d_ycat, saved, mem2, mem_g, w_mem, B, S, tag):
    memn, kvm, lse = saved
    M = mem2.shape[0] // B
    o_c0 = ycat.shape[1] // 128 - MEM_HEADS
    d_q, d_k, d_v = _attn_bwd(proj, q_col // 128, None, kvm, 0, None, kvm, MEM_HEADS, ycat, d_ycat, o_c0, lse, B, S, M,
                              MEM_HEADS, False, MEM_HEAD_DIM ** -0.5, tag + "_memattn_bwd")
    d_kvm = jnp.concatenate([d_k, d_v], axis=1).astype(BF16)
    d_w_mem = _mm(memn, d_kvm, "tn", F32, tag + "_memkv_dw")
    d_memn = _mm(d_kvm, w_mem, "nt", F32, tag + "_memkv_dx")
    _, d_mem_g = _rowwise_bwd(_f_rms, [mem2], [mem_g], [d_memn], 1, tag + "_memnorm_bwd")
    return d_q, d_w_mem, d_mem_g


def _rope_tables(positions):
    inv_freq = 1.0 / (ROPE_THETA ** (jnp.arange(0, MLA_ROPE, 2, dtype=F32) / MLA_ROPE))
    ang = positions.astype(F32).reshape(-1, 1) * inv_freq
    cos, sin, zero = jnp.cos(ang), jnp.sin(ang), jnp.zeros_like(ang)
    return jnp.concatenate([cos, zero, cos, zero], axis=1), jnp.concatenate([-sin, zero, sin, zero], axis=1)


def _forward_backward(x, mem, positions, target, W):
    B, S, D = x.shape
    T = B * S
    conv_w = W["conv_dw"].shape[1]
    mix_w = 2 * D
    h0 = x.reshape(T, D)
    mem2 = mem.reshape(-1, D)
    tgt = target.reshape(T, D)
    row = lambda v: v.reshape(1, -1)
    n_nope = MLA_HEADS * MLA_NOPE

    g0 = row(W["norm_g"][0])
    (u0,) = _rowwise(_f_rms, [h0], [g0], [(D, BF16)], "l0_norm", carry=W.carry("l0_norm"))
    proj0 = _mm(u0, W["conv_w_in"], "nn", F32, "l0_in", carry=W.carry("l0_in"))
    a0, gate0 = (proj0, conv_w, 0), (proj0, conv_w, 1)
    qm0_col, z0_col = 2 * conv_w, 2 * conv_w + MEM_WIDTH
    (glu,) = _rowwise(_f_glu, [a0, gate0], [], [(conv_w, F32)], "l0_glu", carry=W.carry("l0_glu"))
    dw, dwb = W["conv_dw"], row(W["conv_dw_b"][0])
    cv = _dwconv_fwd(glu.reshape(B, S, conv_w), dw, dwb, "l0_dwconv", carry=W.carry("l0_dwconv")).reshape(T, conv_w)
    ln_g, ln_b = row(W["conv_ln_g"][0]), row(W["conv_ln_b"][0])
    (ycat0,) = _rowwise(_f_ln_silu, [cv], [ln_g, ln_b], [(conv_w, F32, mix_w)], "l0_ln", carry=W.carry("l0_ln"))
    mg0 = row(W["mem_norm_g"][0])
    ycat0, mem_saved0 = _mem_attention_fwd(proj0, qm0_col, ycat0, mem2, mg0, W["w_mem_kv"][0], B, S, "l0")
    y0 = _gate_fwd(ycat0, proj0, z0_col, "l0_gate")
    h1 = _mm(y0, W["w_out"][0], "nn", F32, "l0_out", res=h0)

    g1 = row(W["norm_g"][1])
    (u1,) = _rowwise(_f_rms, [h1], [g1], [(D, BF16)], "l1_norm")
    proj1 = _mm(u1, W["mla_w_in"], "nn", F32, "l1_in")
    cq, ckv = (proj1, Q_RANK, 0), (proj1, KV_RANK, Q_RANK // KV_RANK)
    qm1_col = Q_RANK + KV_RANK
    z1_col = qm1_col + MEM_WIDTH
    kr_col = z1_col + mix_w
    qg, kvg = row(W["mla_q_norm_g"]), row(W["mla_kv_norm_g"])
    (cqn,) = _rowwise(_f_rms, [cq], [qg], [(Q_RANK, BF16)], "l1_qnorm")
    (ckvn,) = _rowwise(_f_rms, [ckv], [kvg], [(KV_RANK, BF16)], "l1_kvnorm")
    qf = _mm(cqn, W["mla_w_uq"], "nn", F32, "l1_uq")
    kvf = _mm(ckvn, W["mla_w_ukv"], "nn", BF16, "l1_ukv")
    cos_p, sin_p = _rope_tables(positions)
    qr, kr = _rowwise(_f_rope, [(qf, n_nope, 1), (proj1, 128, kr_col // 128), cos_p, sin_p], [],
                      [(n_nope, BF16), (128, BF16)], "l1_rope")
    scale1 = MLA_QK ** -0.5
    ycat1, lse1 = _attn_fwd(qf, 0, qr, kvf, 0, kr, kvf, MLA_HEADS, B, S, S, MLA_HEADS, True, scale1, "l1_attn",
                            o_width=mix_w)
    mg1 = row(W["mem_norm_g"][1])
    ycat1, mem_saved1 = _mem_attention_fwd(proj1, qm1_col, ycat1, mem2, mg1, W["w_mem_kv"][1], B, S, "l1")
    y1 = _gate_fwd(ycat1, proj1, z1_col, "l1_gate")
    h2 = _mm(y1, W["w_out"][1], "nn", F32, "l1_out", res=h1)

    gf = row(W["final_norm_g"])
    dh2, d_gf, loss128 = _final_loss(h2, tgt, gf, "final_loss")
    G = {"final_norm_g": d_gf.reshape(-1)}
    L1 = {}

    dy1 = _mm(dh2, W["w_out"][1], "nt", F32, "l1_out_dx")
    d_wout1 = _mm(y1, dh2, "tn", F32, "l1_out_dw")
    d_ycat1, d_z1 = _gate_bwd(ycat1, proj1, z1_col, dy1, "l1_gate_bwd")
    d_qm1, d_wmem1, d_mg1 = _mem_attention_bwd(proj1, qm1_col, ycat1, d_ycat1, mem_saved1, mem2, mg1, W["w_mem_kv"][1],
                                               B, S, "l1")
    d_qn, d_qr, d_kn, d_kr_heads, d_v = _attn_bwd(qf, 0, qr, kvf, 0, kr, kvf, MLA_HEADS, ycat1, d_ycat1, 0, lse1, B, S, S,
                                                  MLA_HEADS, True, scale1, "l1_attn_bwd")
    d_xq, d_kr = _rowwise(_f_rope_t, [d_qr, d_kr_heads, cos_p, sin_p], [], [(n_nope, F32), (128, F32)], "l1_rope_bwd")
    d_qf = jnp.concatenate([d_qn, d_xq], axis=1).astype(BF16)
    d_kvf = jnp.concatenate([d_kn, d_v], axis=1).astype(BF16)
    d_cqn = _mm(d_qf, W["mla_w_uq"], "nt", F32, "l1_uq_dx")
    L1[("mla_w_uq", None)] = _mm(cqn, d_qf, "tn", F32, "l1_uq_dw")
    d_ckvn = _mm(d_kvf, W["mla_w_ukv"], "nt", F32, "l1_ukv_dx")
    L1[("mla_w_ukv", None)] = _mm(ckvn, d_kvf, "tn", F32, "l1_ukv_dw")
    d_cq, d_qg = _rowwise_bwd(_f_rms, [cq], [qg], [d_cqn], 1, "l1_qnorm_bwd")
    d_ckv, d_kvg = _rowwise_bwd(_f_rms, [ckv], [kvg], [d_ckvn], 1, "l1_kvnorm_bwd")
    d_proj1 = jnp.concatenate([d_cq, d_ckv, d_qm1, d_z1, d_kr], axis=1).astype(BF16)
    L1[("mla_w_in", None)] = _mm(u1, d_proj1, "tn", F32, "l1_in_dw")
    L1[("w_mem_kv", 1)], L1[("w_out", 1)] = d_wmem1, d_wout1
    W.ready("l1", L1)
    d_u1 = _mm(d_proj1, W["mla_w_in"], "nt", F32, "l1_in_dx", carry=W.carry("l1_in_dx"))
    d_h1n, d_g1 = _rowwise_bwd(_f_rms, [h1], [g1], [d_u1], 1, "l1_norm_bwd")
    dh1 = dh2 + d_h1n

    dy0 = _mm(dh1, W["w_out"][0], "nt", F32, "l0_out_dx")
    d_wout0 = _mm(y0, dh1, "tn", F32, "l0_out_dw")
    d_ycat0, d_z0 = _gate_bwd(ycat0, proj0, z0_col, dy0, "l0_gate_bwd")
    d_qm0, d_wmem0, d_mg0 = _mem_attention_bwd(proj0, qm0_col, ycat0, d_ycat0, mem_saved0, mem2, mg0, W["w_mem_kv"][0],
                                               B, S, "l0")
    W.ready("l0a", {("w_mem_kv", 0): d_wmem0, ("w_out", 0): d_wout0})
    d_cv, d_ln_g, d_ln_b = _rowwise_bwd(_f_ln_silu, [cv], [ln_g, ln_b], [(d_ycat0, conv_w, 0)], 1, "l0_ln_bwd",
                                        carry=W.carry("l0_ln_bwd"))
    d_glu, d_dw, d_dwb = _dwconv_bwd(glu.reshape(B, S, conv_w), dw, d_cv.reshape(B, S, conv_w), "l0_dwconv_bwd",
                                     carry=W.carry("l0_dwconv_bwd"))
    d_a0, d_gate0 = _rowwise_bwd(_f_glu, [a0, gate0], [], [d_glu.reshape(T, conv_w)], 2, "l0_glu_bwd")
    d_proj0 = jnp.concatenate([d_a0, d_gate0, d_qm0, d_z0], axis=1).astype(BF16)
    d_conv_w_in = _mm(u0, d_proj0, "tn", F32, "l0_in_dw", carry=W.carry("l0_in_dw"))
    W.ready("l0b", {("conv_w_in", None): d_conv_w_in, ("conv_dw", None): d_dw,
                    ("mla_q_norm_g", None): d_qg.reshape(-1), ("mla_kv_norm_g", None): d_kvg.reshape(-1)})
    d_u0 = _mm(d_proj0, W["conv_w_in"], "nt", F32, "l0_in_dx", carry=W.carry("l0_in_dx"))
    d_h0n, d_g0 = _rowwise_bwd(_f_rms, [h0], [g0], [d_u0], 1, "l0_norm_bwd")
    dx = (dh1 + d_h0n).reshape(B, S, D)

    G["norm_g"] = jnp.concatenate([d_g0, d_g1], axis=0)
    G["mem_norm_g"] = jnp.concatenate([d_mg0, d_mg1], axis=0)
    G["conv_dw_b"] = d_dwb
    G["conv_ln_g"], G["conv_ln_b"] = d_ln_g, d_ln_b
    return loss128[0, 0], dx, G


def _mla_in_perm(w):
    c2 = Q_RANK + KV_RANK
    zero = jnp.zeros((w.shape[0], HALF_ROPE), w.dtype)
    return jnp.concatenate([w[:, :c2], w[:, c2 + MLA_ROPE:], w[:, c2:c2 + HALF_ROPE], zero,
                            w[:, c2 + HALF_ROPE:c2 + MLA_ROPE], zero], axis=1)


def _mla_in_unperm(g):
    c2 = Q_RANK + KV_RANK
    r = g.shape[1] - 128
    return jnp.concatenate([g[:, :c2], g[:, r:r + HALF_ROPE], g[:, r + 64:r + 64 + HALF_ROPE], g[:, c2:r]], axis=1)


def _uq_perm(w):
    n = w.shape[0]
    w3 = w.reshape(n, MLA_HEADS, MLA_QK)
    zero = jnp.zeros((n, MLA_HEADS, HALF_ROPE), w.dtype)
    rope = jnp.concatenate([w3[:, :, MLA_NOPE:MLA_NOPE + HALF_ROPE], zero, w3[:, :, MLA_NOPE + HALF_ROPE:], zero], axis=2)
    return jnp.concatenate([w3[:, :, :MLA_NOPE].reshape(n, -1), rope.reshape(n, -1)], axis=1)


def _uq_unperm(g):
    n = g.shape[0]
    n_nope = MLA_HEADS * MLA_NOPE
    rope = g[:, n_nope:].reshape(n, MLA_HEADS, 128)
    return jnp.concatenate([g[:, :n_nope].reshape(n, MLA_HEADS, MLA_NOPE), rope[:, :, :HALF_ROPE],
                            rope[:, :, 64:64 + HALF_ROPE]], axis=2).reshape(n, -1)


def _ukv_perm(w):
    w3 = w.reshape(w.shape[0], MLA_HEADS, MLA_NOPE + MLA_V)
    return jnp.concatenate([w3[:, :, :MLA_NOPE].reshape(w.shape[0], -1), w3[:, :, MLA_NOPE:].reshape(w.shape[0], -1)], axis=1)


def _ukv_unperm(g):
    n = g.shape[0]
    half = MLA_HEADS * MLA_NOPE
    return jnp.concatenate([g[:, :half].reshape(n, MLA_HEADS, MLA_NOPE), g[:, half:].reshape(n, MLA_HEADS, MLA_V)],
                           axis=2).reshape(n, -1)


_ROW_CUT = ("w_mem_kv", "w_out")
_COL_CUT = ("conv_w_in", "mla_w_in", "mla_w_uq", "mla_w_ukv", "conv_dw")
_BIG = ("w_mem_kv", "w_out", "conv_w_in", "mla_w_in", "mla_w_uq", "mla_w_ukv")
_SMALL_SHARDED = ("conv_dw", "mla_q_norm_g", "mla_kv_norm_g")
_REPLICATED = ("norm_g", "mem_norm_g", "conv_dw_b", "conv_ln_g", "conv_ln_b", "final_norm_g")
_PERM = {"mla_w_in": (_mla_in_perm, _mla_in_unperm), "mla_w_uq": (_uq_perm, _uq_unperm), "mla_w_ukv": (_ukv_perm, _ukv_unperm)}


def _join(n, blocks):
    if n in _ROW_CUT:
        _, L, r, c = blocks.shape
        return blocks.transpose(1, 0, 2, 3).reshape(L, N_DEV * r, c)
    if n in _COL_CUT:
        _, _, r, c = blocks.shape
        return blocks.reshape(N_DEV, r, c).transpose(1, 0, 2).reshape(r, N_DEV * c)
    return blocks.reshape(-1)


def _cut(n, full, shard_shape):
    if n in _ROW_CUT:
        L, r, c = shard_shape
        return full.reshape(L, N_DEV, r, c).transpose(1, 0, 2, 3)
    if n in _COL_CUT:
        _, r, c = shard_shape
        return full.reshape(r, N_DEV, c).transpose(1, 0, 2).reshape(N_DEV, 1, r, c)
    return full.reshape(N_DEV, 1, -1)


def _flat_pad(parts, size):
    flat = jnp.concatenate([p.reshape(-1) for p in parts])
    return jnp.concatenate([flat, jnp.zeros((size - flat.shape[0],), flat.dtype)])


SMALL_LANES = 128 * 8


def _as_tiles(flat_parts):
    total = sum(p.size for p in flat_parts)
    size = -(-total // SMALL_LANES) * SMALL_LANES
    return _flat_pad(flat_parts, size).reshape(8, size // 8)


def _split_flat(flat, like):
    out, o = [], 0
    for a in like:
        out.append(flat[o:o + a.size].reshape(a.shape))
        o += a.size
    return out


_HBM = pl.BlockSpec(memory_space=pltpu.HBM)
_VMEM = pl.BlockSpec(memory_space=pltpu.VMEM)


def _position():
    return lax.axis_index("x"), lax.axis_index("y"), lax.axis_index("c")


def _dma_sems(n):
    return [pltpu.SemaphoreType.DMA((n,)), pltpu.SemaphoreType.DMA((n,))]


def _run_stage(stage, name):
    n_in, n_out = len(stage.ins), len(stage.out_shapes)

    def body(*refs):
        ins, outs, sems = refs[:n_in], refs[n_in:n_in + n_out], refs[n_in + n_out:]
        stage.start(ins, outs, sems)
        stage.wait(ins, outs, sems)

    outs = _call(body, name, stage.out_shapes, in_specs=[_HBM] * n_in, out_specs=[_HBM] * n_out, scratch=stage.sems,
                 aliases=stage.aliases)(*stage.ins)
    _deliver(stage, outs)
    return stage.outs


def _gather_chips_stage(shards):
    n = len(shards)

    def copies(x_refs, out_refs, sems):
        send_sems, recv_sems, _ = sems
        x, y, c = _position()
        peers = [(x, y, 1 - c), (1 - x, y, c), (x, 1 - y, c), (1 - x, 1 - y, c)]
        out = []
        for a in range(n):
            for k, (px, py, pc) in enumerate(peers):
                send = pltpu.make_async_remote_copy(src_ref=x_refs[a], dst_ref=out_refs[a].at[4 * x + 2 * y + c],
                                                    send_sem=send_sems.at[4 * a + k], recv_sem=recv_sems.at[4 * a + k],
                                                    device_id=(px, py, pc), device_id_type=MESH)
                recv = pltpu.make_async_remote_copy(src_ref=x_refs[a], dst_ref=out_refs[a].at[4 * px + 2 * py + pc],
                                                    send_sem=send_sems.at[4 * a + k], recv_sem=recv_sems.at[4 * a + k],
                                                    device_id=(px, py, pc), device_id_type=MESH)
                out.append((send, recv))
        return out

    def local(x_refs, out_refs, sems):
        x, y, c = _position()
        return [pltpu.make_async_copy(x_refs[a], out_refs[a].at[4 * x + 2 * y + c], sems[2].at[a]) for a in range(n)]

    def start(x_refs, out_refs, sems):
        for cp in local(x_refs, out_refs, sems):
            cp.start()
        for send, _ in copies(x_refs, out_refs, sems):
            send.start()

    def wait(x_refs, out_refs, sems):
        for send, recv in copies(x_refs, out_refs, sems):
            recv.wait_recv()
            send.wait_send()
        for cp in local(x_refs, out_refs, sems):
            cp.wait()

    return _Stage(shards, [jax.ShapeDtypeStruct((N_DEV,) + a.shape, a.dtype) for a in shards],
                  _dma_sems(4 * n) + [pltpu.SemaphoreType.DMA((n,))], start, wait)


def _gather_sibling_stage(bufs):
    n = len(bufs)

    def copies(out_refs, sems):
        send_sems, recv_sems = sems
        x, y, c = _position()
        out = []
        for a in range(n):
            for j, (px, py) in enumerate([(1 - x, y), (x, 1 - y), (1 - x, 1 - y)]):
                mine, theirs = out_refs[a].at[4 * px + 2 * py + c], out_refs[a].at[4 * px + 2 * py + (1 - c)]
                send = pltpu.make_async_remote_copy(src_ref=mine, dst_ref=mine, send_sem=send_sems.at[3 * a + j],
                                                    recv_sem=recv_sems.at[3 * a + j], device_id=(x, y, 1 - c),
                                                    device_id_type=MESH)
                recv = pltpu.make_async_remote_copy(src_ref=mine, dst_ref=theirs, send_sem=send_sems.at[3 * a + j],
                                                    recv_sem=recv_sems.at[3 * a + j], device_id=(x, y, 1 - c),
                                                    device_id_type=MESH)
                out.append((send, recv))
        return out

    def start(_, out_refs, sems):
        for send, _r in copies(out_refs, sems):
            send.start()

    def wait(_, out_refs, sems):
        for send, recv in copies(out_refs, sems):
            recv.wait_recv()
            send.wait_send()

    return _Stage(bufs, [jax.ShapeDtypeStruct(b.shape, b.dtype) for b in bufs], _dma_sems(3 * n), start, wait,
                  aliases={a: a for a in range(n)})


def _all_gather_small(v, name):
    r, n = v.shape

    def body(x_ref, out_ref, send_sems, recv_sems, local_sem):
        x, y, c = _position()
        me = 4 * x + 2 * y + c
        mine = pltpu.make_async_copy(x_ref, out_ref.at[me], local_sem)
        mine.start()
        flips = [(fx, fy, fc) for fx in (0, 1) for fy in (0, 1) for fc in (0, 1)][1:]
        copies = []
        for k, (fx, fy, fc) in enumerate(flips):
            peer = (x ^ fx, y ^ fy, c ^ fc)
            cp = pltpu.make_async_remote_copy(src_ref=x_ref, dst_ref=out_ref.at[me], send_sem=send_sems.at[k],
                                              recv_sem=recv_sems.at[k], device_id=peer, device_id_type=MESH)
            cp.start()
            copies.append(cp)
        for k, (fx, fy, fc) in enumerate(flips):
            px, py, pc = x ^ fx, y ^ fy, c ^ fc
            src = out_ref.at[4 * px + 2 * py + pc]
            pltpu.make_async_remote_copy(src_ref=x_ref, dst_ref=src, send_sem=send_sems.at[k], recv_sem=recv_sems.at[k],
                                         device_id=(px, py, pc), device_id_type=MESH).wait_recv()
        for cp in copies:
            cp.wait_send()
        mine.wait()

    return _call(body, name, jax.ShapeDtypeStruct((N_DEV, r, n), v.dtype), in_specs=[_VMEM], out_specs=_VMEM,
                 scratch=_dma_sems(7) + [pltpu.SemaphoreType.DMA(())])(v)


def _reduce_sibling_stage(gs):
    n = len(gs)

    def copies(g_refs, out_refs, sems):
        send_sems, recv_sems = sems
        x, y, c = _position()
        return [pltpu.make_async_remote_copy(src_ref=g_refs[a].at[2 * k + (1 - c)], dst_ref=out_refs[a].at[k],
                                             send_sem=send_sems.at[4 * a + k], recv_sem=recv_sems.at[4 * a + k],
                                             device_id=(x, y, 1 - c), device_id_type=MESH)
                for a in range(n) for k in range(4)]

    def start(g_refs, out_refs, sems):
        for cp in copies(g_refs, out_refs, sems):
            cp.start()

    def wait(g_refs, out_refs, sems):
        for cp in copies(g_refs, out_refs, sems):
            cp.wait()

    return _Stage(gs, [jax.ShapeDtypeStruct((4,) + g.shape[1:], g.dtype) for g in gs], _dma_sems(4 * n), start, wait)


def _rows2d(shape):
    cols = shape[-1]
    rows = 1
    for s in shape[:-1]:
        rows *= s
    return rows, cols


def _add_own(g, recv, name):
    rows, cols = _rows2d(g.shape[1:])
    tr = _pick(rows, 256, 8)
    c = lax.axis_index("c").astype(jnp.int32).reshape(1)

    def body(c_ref, g_ref, r_ref, o_ref):
        o_ref[...] = g_ref[...] + r_ref[...]

    grid_spec = pltpu.PrefetchScalarGridSpec(
        num_scalar_prefetch=1, grid=(4, rows // tr),
        in_specs=[pl.BlockSpec((None, None, tr, cols), lambda k, i, c_ref: (k, c_ref[0], i, 0)),
                  pl.BlockSpec((None, tr, cols), lambda k, i, c_ref: (k, i, 0))],
        out_specs=pl.BlockSpec((None, tr, cols), lambda k, i, c_ref: (k, i, 0)))
    return _call(body, name, jax.ShapeDtypeStruct((4, rows, cols), F32), grid_spec=grid_spec,
                 dims=("parallel", "parallel"))(c, g.reshape(4, 2, rows, cols), recv.reshape(4, rows, cols))


def _reduce_chips_stage(pas):
    n = len(pas)

    def copies(pa_refs, out_refs, sems):
        send_sems, recv_sems, _ = sems
        x, y, c = _position()
        my_chip = 2 * x + y
        out = []
        for a in range(n):
            for j, (px, py) in enumerate([(1 - x, y), (x, 1 - y), (1 - x, 1 - y)]):
                send = pltpu.make_async_remote_copy(src_ref=pa_refs[a].at[2 * px + py], dst_ref=out_refs[a].at[my_chip],
                                                    send_sem=send_sems.at[3 * a + j], recv_sem=recv_sems.at[3 * a + j],
                                                    device_id=(px, py, c), device_id_type=MESH)
                recv = pltpu.make_async_remote_copy(src_ref=pa_refs[a].at[2 * px + py], dst_ref=out_refs[a].at[2 * px + py],
                                                    send_sem=send_sems.at[3 * a + j], recv_sem=recv_sems.at[3 * a + j],
                                                    device_id=(px, py, c), device_id_type=MESH)
                out.append((send, recv))
        return out

    def local(pa_refs, out_refs, sems):
        x, y, _ = _position()
        return [pltpu.make_async_copy(pa_refs[a].at[2 * x + y], out_refs[a].at[2 * x + y], sems[2].at[a]) for a in range(n)]

    def start(pa_refs, out_refs, sems):
        for cp in local(pa_refs, out_refs, sems):
            cp.start()
        for send, _r in copies(pa_refs, out_refs, sems):
            send.start()

    def wait(pa_refs, out_refs, sems):
        for send, recv in copies(pa_refs, out_refs, sems):
            recv.wait_recv()
            send.wait_send()
        for cp in local(pa_refs, out_refs, sems):
            cp.wait()

    return _Stage(pas, [jax.ShapeDtypeStruct(pa.shape, pa.dtype) for pa in pas],
                  _dma_sems(3 * n) + [pltpu.SemaphoreType.DMA((n,))], start, wait)


def _adamw_math(w, g, m, v):
    m = ADAM_B1 * m + (1.0 - ADAM_B1) * g
    v = ADAM_B2 * v + (1.0 - ADAM_B2) * (g * g)
    m_hat = m / (1.0 - ADAM_B1 ** ADAM_STEP)
    v_hat = v / (1.0 - ADAM_B2 ** ADAM_STEP)
    delta = -ADAM_LR * (m_hat / (jnp.sqrt(v_hat) + ADAM_EPS) + ADAM_WD * w)
    return delta, m, v


def _sum_adamw(parts, w, m, v, name):
    n, rows, cols = parts.shape
    tr = _pick(rows, 128, 8)

    def body(p_ref, w_ref, m_ref, v_ref, g_ref, d_ref, nm_ref, nv_ref):
        g = p_ref[0]
        for k in range(1, n):
            g = g + p_ref[k]
        d, nm, nv = _adamw_math(w_ref[...], g, m_ref[...], v_ref[...])
        g_ref[...], d_ref[...], nm_ref[...], nv_ref[...] = g, d, nm, nv

    blk = pl.BlockSpec((tr, cols), lambda i: (i, 0))
    return _call(body, name, [jax.ShapeDtypeStruct((rows, cols), F32)] * 4, grid=(rows // tr,),
                 in_specs=[pl.BlockSpec((n, tr, cols), lambda i: (0, i, 0)), blk, blk, blk],
                 out_specs=[blk] * 4, dims=("parallel",))(parts, w, m, v)


_WEIGHTS = ("norm_g", "mem_norm_g", "w_mem_kv", "w_out", "conv_w_in", "conv_dw", "conv_dw_b", "conv_ln_g", "conv_ln_b",
            "mla_w_in", "mla_q_norm_g", "mla_w_uq", "mla_kv_norm_g", "mla_w_ukv", "final_norm_g")


_GATHER_GROUPS = {"a": ("conv_w_in",), "b": ("w_mem_kv", "w_out"), "c": ("mla_w_in", "mla_w_uq", "mla_w_ukv")}
_CARRIERS = {"l0_norm": ("gather chips", ("a",)), "l0_in": ("gather chips", ("b",)), "l0_glu": ("gather sibling", ("b",)),
             "l0_dwconv": ("gather chips", ("c",)), "l0_ln": ("gather sibling", ("c",)),
             "l1_in_dx": ("reduce sibling", ("l1",)), "l0_ln_bwd": ("reduce sibling", ("l0a",)),
             "l0_dwconv_bwd": ("reduce chips", ("l1",)), "l0_in_dw": ("reduce chips", ("l0a",)),
             "l0_in_dx": ("reduce sibling alone, then chips", ("l0b",))}


class _Schedule:
    def __init__(self, w):
        self.w, self.full, self.gather, self.reduce, self.reduced = w, {}, {}, {}, {}
        small = _all_gather_small(_as_tiles([w[n] for n in _SMALL_SHARDED]), "gather_small_weights").reshape(N_DEV, -1)
        o = 0
        for n in _SMALL_SHARDED:
            self.full[n] = _join(n, small[:, o:o + w[n].size].reshape((N_DEV,) + w[n].shape))
            o += w[n].size
        for n in _REPLICATED:
            self.full[n] = w[n]

    def carry(self, call):
        kind, groups = _CARRIERS[call]
        stages = []
        for g in groups:
            if kind == "gather chips":
                self.gather[g] = [_gather_chips_stage([self.w[n].astype(BF16) for n in _GATHER_GROUPS[g]])]
                stages.append(self.gather[g][0])
            elif kind == "gather sibling":
                self.gather[g].append(_gather_sibling_stage(self.gather[g][0].outs))
                stages.append(self.gather[g][1])
            elif kind == "reduce sibling":
                self.reduce[g]["sibling"] = _reduce_sibling_stage(self.reduce[g]["cut"])
                stages.append(self.reduce[g]["sibling"])
            else:
                r = self.reduce[g]
                if kind != "reduce chips":
                    r["sibling"] = _reduce_sibling_stage(r["cut"])
                    _run_stage(r["sibling"], "reduce_sibling_" + g)
                partial = [_add_own(c, s, "reduce_add_%s_%d" % (g, i)) for i, (c, s) in enumerate(zip(r["cut"], r["sibling"].outs))]
                r["chips"] = _reduce_chips_stage(partial)
                stages.append(r["chips"])
        return stages[0] if len(stages) == 1 else _merge_stages(stages[0], stages[1])

    def __getitem__(self, name):
        if name not in self.full:
            g = [k for k, names in _GATHER_GROUPS.items() if name in names][0]
            if len(self.gather[g]) == 1:
                self.gather[g].append(_gather_sibling_stage(self.gather[g][0].outs))
                _run_stage(self.gather[g][1], "gather_sibling_" + g)
            for n, buf in zip(_GATHER_GROUPS[g], self.gather[g][1].outs):
                self.full[n] = _PERM[n][0](_join(n, buf)) if n in _PERM else _join(n, buf)
        return self.full[name]

    def ready(self, group, grads):
        keys, cut, small = [], [], []
        for (n, layer), g in grads.items():
            if n in _SMALL_SHARDED:
                small.append(_cut(n, g, self.w[n].shape).reshape(N_DEV, -1))
                continue
            keys.append((n, layer))
            if layer is not None:
                cut.append(g.reshape((N_DEV,) + self.w[n].shape[1:]))
            else:
                cut.append(_cut(n, _PERM[n][1](g) if n in _PERM else g, self.w[n].shape))
        if small:
            keys.append(("small", None))
            cut.append(jax.vmap(lambda r: _as_tiles([r]))(jnp.concatenate(small, axis=1)))
        self.reduce[group] = {"keys": keys, "cut": cut}

    def finish(self):
        out = {}
        for r in self.reduce.values():
            out.update(dict(zip(r["keys"], r["chips"].outs)))
        return out


def kernel(x, mem, positions, norm_g, mem_norm_g, w_mem_kv, w_out, conv_w_in, conv_dw, conv_dw_b, conv_ln_g, conv_ln_b, mla_w_in, mla_q_norm_g, mla_w_uq, mla_kv_norm_g, mla_w_ukv, final_norm_g, loss_target, m_norm_g, m_mem_norm_g, m_w_mem_kv, m_w_out, m_conv_w_in, m_conv_dw, m_conv_dw_b, m_conv_ln_g, m_conv_ln_b, m_mla_w_in, m_mla_q_norm_g, m_mla_w_uq, m_mla_kv_norm_g, m_mla_w_ukv, m_final_norm_g, v_norm_g, v_mem_norm_g, v_w_mem_kv, v_w_out, v_conv_w_in, v_conv_dw, v_conv_dw_b, v_conv_ln_g, v_conv_ln_b, v_mla_w_in, v_mla_q_norm_g, v_mla_w_uq, v_mla_kv_norm_g, v_mla_w_ukv, v_final_norm_g):
    w = dict(zip(_WEIGHTS, (norm_g, mem_norm_g, w_mem_kv, w_out, conv_w_in, conv_dw, conv_dw_b, conv_ln_g, conv_ln_b,
                            mla_w_in, mla_q_norm_g, mla_w_uq, mla_kv_norm_g, mla_w_ukv, final_norm_g)))
    m = dict(zip(_WEIGHTS, (m_norm_g, m_mem_norm_g, m_w_mem_kv, m_w_out, m_conv_w_in, m_conv_dw, m_conv_dw_b, m_conv_ln_g,
                            m_conv_ln_b, m_mla_w_in, m_mla_q_norm_g, m_mla_w_uq, m_mla_kv_norm_g, m_mla_w_ukv, m_final_norm_g)))
    v = dict(zip(_WEIGHTS, (v_norm_g, v_mem_norm_g, v_w_mem_kv, v_w_out, v_conv_w_in, v_conv_dw, v_conv_dw_b, v_conv_ln_g,
                            v_conv_ln_b, v_mla_w_in, v_mla_q_norm_g, v_mla_w_uq, v_mla_kv_norm_g, v_mla_w_ukv, v_final_norm_g)))

    sched = _Schedule(w)
    loss_local, dx, G = _forward_backward(x, mem, positions, loss_target, sched)
    loss = lax.psum(loss_local, ("x", "y", "c"))

    from_chips = sched.finish()
    out = [{}, {}, {}, {}]
    for n in _BIG:
        if n in _ROW_CUT:
            res = [_sum_adamw(from_chips[(n, l)], w[n][l], m[n][l], v[n][l], "adamw_%s_%d" % (n, l)) for l in range(w[n].shape[0])]
            res = [jnp.stack(r) for r in zip(*res)]
        else:
            rows, cols = _rows2d(w[n].shape)
            res = _sum_adamw(from_chips[(n, None)], w[n].reshape(rows, cols), m[n].reshape(rows, cols),
                             v[n].reshape(rows, cols), "adamw_" + n)
        for o, r in zip(out, res):
            o[n] = r.reshape(w[n].shape)
    small_like = [w[n] for n in _SMALL_SHARDED]
    res = _sum_adamw(from_chips[("small", None)], _as_tiles(small_like), _as_tiles([m[n] for n in _SMALL_SHARDED]),
                     _as_tiles([v[n] for n in _SMALL_SHARDED]), "adamw_small")
    for o, r in zip(out, res):
        for n, a in zip(_SMALL_SHARDED, _split_flat(r.reshape(-1), small_like)):
            o[n] = a

    rep_like = [w[n] for n in _REPLICATED]
    rep_parts = _all_gather_small(_as_tiles([G[n] for n in _REPLICATED]), "gather_replicated_grads")
    res = _sum_adamw(rep_parts, _as_tiles(rep_like), _as_tiles([m[n] for n in _REPLICATED]),
                     _as_tiles([v[n] for n in _REPLICATED]), "adamw_replicated")
    for o, r in zip(out, res):
        for n, a in zip(_REPLICATED, _split_flat(r.reshape(-1), rep_like)):
            o[n] = a

    return (loss, dx, *[out[0][n] for n in _WEIGHTS], *[out[1][n] for n in _WEIGHTS],
            *[out[2][n] for n in _WEIGHTS], *[out[3][n] for n in _WEIGHTS])
```

```python
import jax
import jax.numpy as jnp
from jax import lax
from jax.experimental import pallas as pl
from jax.experimental.pallas import tpu as pltpu

F32 = jnp.float32
BF16 = jnp.bfloat16
MESH = pl.DeviceIdType.MESH
N_DEV = 8
VMEM_LIMIT_BYTES = 48 * 1024 * 1024

MEM_HEADS, MEM_HEAD_DIM = 4, 128
MEM_WIDTH = MEM_HEADS * MEM_HEAD_DIM
CONV_KERNEL = 31
CONV_PAD = 32
MLA_HEADS, MLA_NOPE, MLA_ROPE, MLA_V = 12, 128, 64, 128
MLA_QK = MLA_NOPE + MLA_ROPE
HALF_ROPE = MLA_ROPE // 2
Q_RANK, KV_RANK = 512, 256
ROPE_THETA = 10000.0
RMS_EPS = 1e-6
LN_EPS = 1e-5
ADAM_LR, ADAM_B1, ADAM_B2, ADAM_EPS, ADAM_WD, ADAM_STEP = 0.001, 0.9, 0.999, 1e-08, 0.01, 10
NEG = -1e30


class _Stage:
    def __init__(self, ins, out_shapes, sems, start, wait, aliases=None):
        self.ins, self.out_shapes, self.sems = list(ins), list(out_shapes), list(sems)
        self.start, self.wait, self.aliases, self.outs = start, wait, dict(aliases or {}), None


def _merge_stages(a, b):
    na_i, na_o, na_s = len(a.ins), len(a.out_shapes), len(a.sems)

    def start(i, o, s):
        a.start(i[:na_i], o[:na_o], s[:na_s])
        b.start(i[na_i:], o[na_o:], s[na_s:])

    def wait(i, o, s):
        a.wait(i[:na_i], o[:na_o], s[:na_s])
        b.wait(i[na_i:], o[na_o:], s[na_s:])

    aliases = dict(a.aliases)
    aliases.update({na_i + k: na_o + v for k, v in b.aliases.items()})
    merged = _Stage(a.ins + b.ins, a.out_shapes + b.out_shapes, a.sems + b.sems, start, wait, aliases)
    merged.parts = (a, b)
    return merged


def _deliver(stage, outs):
    stage.outs = list(outs)
    if hasattr(stage, "parts"):
        a, b = stage.parts
        _deliver(a, outs[:len(a.out_shapes)])
        _deliver(b, outs[len(a.out_shapes):])


def _call(body, name, out_shape, grid=None, in_specs=None, out_specs=None, scratch=(), dims=None, grid_spec=None, aliases=None,
          carry=None):
    params = dict(vmem_limit_bytes=VMEM_LIMIT_BYTES)
    if dims is not None:
        params["dimension_semantics"] = dims
    kw = {}
    if carry is not None:
        single = not isinstance(out_shape, (list, tuple))
        main_out = [out_shape] if single else list(out_shape)
        main_specs = [out_specs] if single else list(out_specs)
        n_in, n_out, n_scr = len(in_specs), len(main_out), len(scratch)
        x_in, x_out = len(carry.ins), len(carry.out_shapes)
        inner, steps = body, tuple(grid)

        def body(*refs):
            ins, xin = refs[:n_in], refs[n_in:n_in + x_in]
            outs = refs[n_in + x_in:n_in + x_in + n_out]
            xout = refs[n_in + x_in + n_out:n_in + x_in + n_out + x_out]
            scr = refs[n_in + x_in + n_out + x_out:n_in + x_in + n_out + x_out + n_scr]
            xsem = refs[n_in + x_in + n_out + x_out + n_scr:]
            ids = [pl.program_id(a) for a in range(len(steps))]
            first, last = ids[0] == 0, ids[0] == steps[0] - 1
            for a in range(1, len(steps)):
                first = jnp.logical_and(first, ids[a] == 0)
                last = jnp.logical_and(last, ids[a] == steps[a] - 1)
            pl.when(first)(lambda: carry.start(xin, xout, xsem))
            inner(*ins, *outs, *scr)
            pl.when(last)(lambda: carry.wait(xin, xout, xsem))

        hbm = pl.BlockSpec(memory_space=pltpu.HBM)
        aliases = dict(aliases or {})
        aliases.update({n_in + k: n_out + v for k, v in carry.aliases.items()})
        res = _call(body, name, main_out + carry.out_shapes, grid=grid, in_specs=list(in_specs) + [hbm] * x_in,
                    out_specs=main_specs + [hbm] * x_out, scratch=list(scratch) + carry.sems, dims=dims, aliases=aliases)

        def run(*args):
            outs = res(*args, *carry.ins)
            _deliver(carry, outs[n_out:])
            return outs[0] if single else outs[:n_out]

        return run
    if aliases:
        kw["input_output_aliases"] = aliases
    if grid_spec is not None:
        kw["grid_spec"] = grid_spec
    else:
        if grid is not None:
            kw["grid"] = grid
        kw["in_specs"] = in_specs
        kw["out_specs"] = out_specs
        kw["scratch_shapes"] = list(scratch)
    return pl.pallas_call(body, name=name, out_shape=out_shape, compiler_params=pltpu.CompilerParams(**params), **kw)


def _pick(n, target, mult):
    best = None
    for d in range(mult, min(n, target) + 1, mult):
        if n % d == 0:
            best = d
    return n if best is None else best


_DOT_DIMS = {"nn": (((1,), (0,)), ((), ())), "nt": (((1,), (1,)), ((), ())), "tn": (((0,), (0,)), ((), ()))}


def _mm(a, b, mode, out_dtype, name, res=None, carry=None):
    if mode == "tn":
        a, mode = a.T, "nn"
    if mode == "nn":
        (M, K), N = a.shape, b.shape[1]
    else:
        (M, K), N = a.shape, b.shape[0]
    tm = _pick(M, 1024, 8)
    tn = _pick(N, 512, 128)
    tk = _pick(K, 1024, 128)
    nk = K // tk
    has_res = res is not None

    def body(*refs):
        if has_res:
            a_ref, b_ref, r_ref, o_ref, acc = refs
        else:
            a_ref, b_ref, o_ref, acc = refs
        k = pl.program_id(2)

        @pl.when(k == 0)
        def _():
            acc[...] = jnp.zeros_like(acc)

        acc[...] += lax.dot_general(a_ref[...].astype(BF16), b_ref[...].astype(BF16), _DOT_DIMS[mode],
                                    preferred_element_type=F32)

        @pl.when(k == nk - 1)
        def _():
            r = acc[...]
            if has_res:
                r = r + r_ref[...]
            o_ref[...] = r.astype(o_ref.dtype)

    a_spec = pl.BlockSpec((tm, tk), lambda i, j, k: (i, k))
    b_spec = {"nn": pl.BlockSpec((tk, tn), lambda i, j, k: (k, j)),
              "nt": pl.BlockSpec((tn, tk), lambda i, j, k: (j, k))}[mode]
    o_spec = pl.BlockSpec((tm, tn), lambda i, j, k: (i, j))
    in_specs = [a_spec, b_spec] + ([o_spec] if has_res else [])
    args = (a, b) + ((res,) if has_res else ())
    return _call(body, name, jax.ShapeDtypeStruct((M, N), out_dtype), grid=(M // tm, N // tn, nk),
                 in_specs=in_specs, out_specs=o_spec, scratch=[pltpu.VMEM((tm, tn), F32)],
                 dims=("parallel", "parallel", "arbitrary"), carry=carry)(*args)


def _views(rows):
    return [r if isinstance(r, tuple) else (r, r.shape[1], 0) for r in rows]


def _rowwise(f, rows, params, outs, name, tb=256, carry=None, into=None):
    rows = _views(rows)
    T = rows[0][0].shape[0]
    tb = min(tb, T)
    nr, npar = len(rows), len(params)
    outs = [o if len(o) == 3 else (o[0], o[1], o[0]) for o in outs]

    def body(*refs):
        vals = f(*[r[...].astype(F32) for r in refs[:nr]], *[p[...] for p in refs[nr:nr + npar]])
        for o_ref, v in zip(refs[nr + npar + (0 if into is None else 1):], vals):
            o_ref[...] = v.astype(o_ref.dtype)

    row_spec = lambda w, cb=0: pl.BlockSpec((tb, w), lambda i: (i, cb))
    par_spec = lambda w: pl.BlockSpec((1, w), lambda i: (0, 0))
    out_shape = [jax.ShapeDtypeStruct((T, tw), dt) for _, dt, tw in outs]
    out_specs = [row_spec(w) for w, _, _ in outs]
    in_specs = [row_spec(w, cb) for _, w, cb in rows] + [par_spec(p.shape[1]) for p in params]
    args = [r[0] for r in rows] + list(params)
    aliases = None
    if into is not None:
        k, arr, cb = into
        aliases = {len(args): k}
        in_specs.append(pl.BlockSpec(memory_space=pl.ANY))
        args.append(arr)
        out_shape[k] = jax.ShapeDtypeStruct(arr.shape, arr.dtype)
        out_specs[k] = row_spec(outs[k][0], cb)
    return _call(body, name, out_shape, grid=(T // tb,), in_specs=in_specs, out_specs=out_specs, dims=("parallel",),
                 carry=carry, aliases=aliases)(*args)


def _rowwise_bwd(f, rows, params, douts, n_diff, name, tb=256, carry=None, add=None, into=None):
    rows, douts = _views(rows), _views(douts)
    T = rows[0][0].shape[0]
    tb = min(tb, T)
    nr, npar, nd = len(rows), len(params), len(douts)
    n_add = 0 if add is None else 1

    def body(*refs):
        rv = [r[...].astype(F32) for r in refs[:nr]]
        pv = [p[...] for p in refs[nr:nr + npar]]
        dv = [d[...].astype(F32) for d in refs[nr + npar:nr + npar + nd]]
        o_refs = refs[nr + npar + nd + n_add + (0 if into is None else 1):]
        fixed = rv[n_diff:]

        def g(*xs):
            return tuple(f(*xs[:n_diff], *fixed, *xs[n_diff:]))

        _, vjp = jax.vjp(g, *rv[:n_diff], *pv)
        grads = list(vjp(tuple(dv)))
        if add is not None:
            grads[0] = grads[0] + refs[nr + npar + nd][...]
        for o_ref, gr in zip(o_refs[:n_diff], grads[:n_diff]):
            o_ref[...] = gr.astype(o_ref.dtype)
        first = pl.program_id(0) == 0
        for o_ref, gr in zip(o_refs[n_diff:], grads[n_diff:]):
            @pl.when(first)
            def _(o_ref=o_ref):
                o_ref[...] = jnp.zeros_like(o_ref)

            o_ref[...] += gr

    row_spec = lambda w, cb=0: pl.BlockSpec((tb, w), lambda i: (i, cb))
    par_spec = lambda w: pl.BlockSpec((1, w), lambda i: (0, 0))
    out_shape = ([jax.ShapeDtypeStruct((T, w), F32) for _, w, _ in rows[:n_diff]]
                 + [jax.ShapeDtypeStruct((1, p.shape[1]), F32) for p in params])
    out_specs = [row_spec(w) for _, w, _ in rows[:n_diff]] + [par_spec(p.shape[1]) for p in params]
    in_specs = ([row_spec(w, cb) for _, w, cb in rows] + [par_spec(p.shape[1]) for p in params]
                + [row_spec(w, cb) for _, w, cb in douts])
    args = [r[0] for r in rows] + list(params) + [d[0] for d in douts]
    aliases = None
    if add is not None:
        in_specs.append(row_spec(add.shape[1]))
        args.append(add)
    if into is not None:
        aliases = {len(args): 0}
        in_specs.append(pl.BlockSpec(memory_space=pl.ANY))
        args.append(into[0])
        out_shape[0] = jax.ShapeDtypeStruct(into[0].shape, into[0].dtype)
        out_specs[0] = row_spec(rows[0][1], into[1])
    return _call(body, name, out_shape, grid=(T // tb,), in_specs=in_specs, out_specs=out_specs,
                 dims=("arbitrary",), carry=carry, aliases=aliases)(*args)


def _sig(x):
    return 1.0 / (1.0 + jnp.exp(-x))


def _rms(x, g):
    return x * lax.rsqrt(jnp.mean(x * x, axis=-1, keepdims=True) + RMS_EPS) * g


def _f_rms(x, g):
    return (_rms(x, g),)


def _f_glu(a, gate):
    return (a * _sig(gate),)


def _f_ln_silu(x, g, b):
    mu = jnp.mean(x, axis=-1, keepdims=True)
    xc = x - mu
    var = jnp.mean(xc * xc, axis=-1, keepdims=True)
    y = xc * lax.rsqrt(var + LN_EPS) * g + b
    return (y * _sig(y),)


def _rope128(x, cos_p, sin_p):
    return x * cos_p + pltpu.roll(x, 64, 1) * sin_p


def _rope128_t(d, cos_p, sin_p):
    return d * cos_p + pltpu.roll(d * sin_p, 64, 1)


def _f_rope(xq, xk, cos_p, sin_p):
    heads = [_rope128(xq[:, h * 128:(h + 1) * 128], cos_p, sin_p) for h in range(MLA_HEADS)]
    return (jnp.concatenate(heads, axis=1), _rope128(xk, cos_p, sin_p))


def _f_rope_t(dq, dk_heads, cos_p, sin_p):
    heads = [_rope128_t(dq[:, h * 128:(h + 1) * 128], cos_p, sin_p) for h in range(MLA_HEADS)]
    dk = dk_heads[:, 0:128]
    for h in range(1, MLA_HEADS):
        dk = dk + dk_heads[:, h * 128:(h + 1) * 128]
    return (jnp.concatenate(heads, axis=1), _rope128_t(dk, cos_p, sin_p))


GATE_LANES = 256


def _gate_fwd(ycat, proj, z_col, name, tb=1024):
    T, width = ycat.shape
    zb = z_col // GATE_LANES

    def body(y_ref, z_ref, o_ref):
        z = z_ref[...]
        o_ref[...] = (y_ref[...] * (z * _sig(z))).astype(o_ref.dtype)

    blk = pl.BlockSpec((tb, GATE_LANES), lambda i, c: (i, c))
    return _call(body, name, jax.ShapeDtypeStruct((T, width), BF16), grid=(T // tb, width // GATE_LANES),
                 in_specs=[blk, pl.BlockSpec((tb, GATE_LANES), lambda i, c: (i, zb + c))], out_specs=blk,
                 dims=("parallel", "parallel"))(ycat, proj)


def _gate_bwd(ycat, proj, z_col, dy, name, tb=1024):
    T, width = ycat.shape
    zb = z_col // GATE_LANES

    def body(y_ref, z_ref, dy_ref, dycat_ref, dz_ref):
        z, d = z_ref[...], dy_ref[...]
        s = _sig(z)
        dycat_ref[...] = d * (z * s)
        dz_ref[...] = (d * y_ref[...] * (s * (1.0 + z * (1.0 - s)))).astype(dz_ref.dtype)

    blk = pl.BlockSpec((tb, GATE_LANES), lambda i, c: (i, c))
    zblk = pl.BlockSpec((tb, GATE_LANES), lambda i, c: (i, zb + c))
    return _call(body, name, [jax.ShapeDtypeStruct((T, width), F32), jax.ShapeDtypeStruct(proj.shape, BF16)],
                 grid=(T // tb, width // GATE_LANES), in_specs=[blk, zblk, blk], out_specs=[blk, zblk],
                 dims=("parallel", "parallel"))(ycat, proj, dy)


def _glu_bwd(proj, d_glu, d_proj, name, tb=256):
    T, w = d_glu.shape

    def body(a_ref, g_ref, d_ref, _, o_ref):
        s = _sig(g_ref[...])

        @pl.when(pl.program_id(1) == 0)
        def _():
            o_ref[...] = (d_ref[...] * s).astype(o_ref.dtype)

        @pl.when(pl.program_id(1) == 1)
        def _():
            o_ref[...] = (d_ref[...] * a_ref[...] * (s * (1.0 - s))).astype(o_ref.dtype)

    return _call(body, name, jax.ShapeDtypeStruct(d_proj.shape, d_proj.dtype), grid=(T // tb, 2),
                 in_specs=[pl.BlockSpec((tb, w), lambda i, c: (i, 0)), pl.BlockSpec((tb, w), lambda i, c: (i, 1)),
                           pl.BlockSpec((tb, w), lambda i, c: (i, 0)), pl.BlockSpec(memory_space=pl.ANY)],
                 out_specs=pl.BlockSpec((tb, w), lambda i, c: (i, c)), dims=("parallel", "arbitrary"),
                 aliases={3: 0})(proj, proj, d_glu, d_proj)


def _final_loss(h, tgt, g, name, tb=256):
    T, D = h.shape

    def body(h_ref, t_ref, g_ref, dh_ref, dg_ref, loss_ref):
        tv = t_ref[...]

        def rowloss(hh, gg):
            e = _rms(hh, gg) - tv
            return 0.5 * jnp.mean(e * e, axis=-1, keepdims=True)

        lr, vjp = jax.vjp(rowloss, h_ref[...], g_ref[...])
        dh, dg = vjp(jnp.ones_like(lr))
        dh_ref[...] = dh

        @pl.when(pl.program_id(0) == 0)
        def _():
            dg_ref[...] = jnp.zeros_like(dg_ref)
            loss_ref[...] = jnp.zeros_like(loss_ref)

        dg_ref[...] += dg
        loss_ref[...] += jnp.broadcast_to(jnp.sum(lr, axis=0, keepdims=True), loss_ref.shape)

    row = pl.BlockSpec((tb, D), lambda i: (i, 0))
    par = pl.BlockSpec((1, D), lambda i: (0, 0))
    return _call(body, name,
                 [jax.ShapeDtypeStruct((T, D), F32), jax.ShapeDtypeStruct((1, D), F32), jax.ShapeDtypeStruct((1, 128), F32)],
                 grid=(T // tb,), in_specs=[row, row, par],
                 out_specs=[row, par, pl.BlockSpec((1, 128), lambda i: (0, 0))], dims=("arbitrary",))(h, tgt, g)


CONV_ROWS = 128
CONV_LANES = 256


def _dwconv_fwd(x, w, b, name, carry=None):
    B, S, C = x.shape
    cb = CONV_LANES
    off = CONV_PAD - (CONV_KERNEL - 1)

    def body(x_ref, w_ref, b_ref, o_ref, pad):
        pad[0:CONV_PAD, :] = jnp.zeros((CONV_PAD, cb), F32)
        pad[CONV_PAD:, :] = x_ref[...]
        for t0 in range(0, S, CONV_ROWS):
            acc = jnp.broadcast_to(b_ref[...], (CONV_ROWS, cb))
            for k in range(CONV_KERNEL):
                acc = acc + w_ref[k:k + 1, :] * pad[t0 + off + k:t0 + off + k + CONV_ROWS, :]
            o_ref[t0:t0 + CONV_ROWS, :] = acc

    return _call(body, name, jax.ShapeDtypeStruct((B, S, C), F32), grid=(B, C // cb),
                 in_specs=[pl.BlockSpec((None, S, cb), lambda i, j: (i, 0, j)),
                           pl.BlockSpec((CONV_KERNEL, cb), lambda i, j: (0, j)),
                           pl.BlockSpec((1, cb), lambda i, j: (0, j))],
                 out_specs=pl.BlockSpec((None, S, cb), lambda i, j: (i, 0, j)),
                 scratch=[pltpu.VMEM((S + CONV_PAD, cb), F32)], dims=("parallel", "parallel"), carry=carry)(x, w, b)


def _dwconv_bwd(x, w, dy, name, carry=None):
    B, S, C = x.shape
    cb = CONV_LANES
    off = CONV_PAD - (CONV_KERNEL - 1)
    groups = CONV_ROWS // 8

    def body(x_ref, w_ref, dy_ref, dx_ref, dw_ref, db_ref, xpad, dypad, wacc):
        xpad[0:CONV_PAD, :] = jnp.zeros((CONV_PAD, cb), F32)
        xpad[CONV_PAD:, :] = x_ref[...]
        dypad[0:S, :] = dy_ref[...]
        dypad[S:, :] = jnp.zeros((CONV_PAD, cb), F32)
        wacc[...] = jnp.zeros_like(wacc)
        for t0 in range(0, S, CONV_ROWS):
            dyc = dy_ref[t0:t0 + CONV_ROWS, :]
            acc = jnp.zeros((CONV_ROWS, cb), F32)
            for k in range(CONV_KERNEL):
                acc = acc + w_ref[k:k + 1, :] * dypad[t0 + (CONV_KERNEL - 1) - k:t0 + (CONV_KERNEL - 1) - k + CONV_ROWS, :]
                prod = dyc * xpad[t0 + off + k:t0 + off + k + CONV_ROWS, :]
                wacc[k] += jnp.sum(prod.reshape(groups, 8, cb), axis=0)
            wacc[CONV_KERNEL] += jnp.sum(dyc.reshape(groups, 8, cb), axis=0)
            dx_ref[t0:t0 + CONV_ROWS, :] = acc

        @pl.when(pl.program_id(1) == 0)
        def _():
            dw_ref[...] = jnp.zeros_like(dw_ref)
            db_ref[...] = jnp.zeros_like(db_ref)

        for k in range(CONV_KERNEL):
            dw_ref[k:k + 1, :] += jnp.sum(wacc[k], axis=0, keepdims=True)
        db_ref[...] += jnp.sum(wacc[CONV_KERNEL], axis=0, keepdims=True)

    blk = pl.BlockSpec((None, S, cb), lambda j, i: (i, 0, j))
    return _call(body, name,
                 [jax.ShapeDtypeStruct((B, S, C), F32), jax.ShapeDtypeStruct((CONV_KERNEL, C), F32),
                  jax.ShapeDtypeStruct((1, C), F32)],
                 grid=(C // cb, B),
                 in_specs=[blk, pl.BlockSpec((CONV_KERNEL, cb), lambda j, i: (0, j)), blk],
                 out_specs=[blk, pl.BlockSpec((CONV_KERNEL, cb), lambda j, i: (0, j)),
                            pl.BlockSpec((1, cb), lambda j, i: (0, j))],
                 scratch=[pltpu.VMEM((S + CONV_PAD, cb), F32), pltpu.VMEM((S + CONV_PAD, cb), F32),
                          pltpu.VMEM((CONV_KERNEL + 1, 8, cb), F32)],
                 dims=("parallel", "arbitrary"), carry=carry)(x, w, dy)


ATTN_TILE = 512
ATTN_SUB = {"fwd": 256, "bwd": 512}


def _attn_shapes(Sq, Sk, causal, pass_):
    tq = min(Sq, ATTN_TILE)
    tk = tq if causal else min(Sk, ATTN_TILE)
    return tq, tk, min(ATTN_SUB[pass_], tq)


def _mask(row0, col0, rows, cols):
    r = row0 + lax.broadcasted_iota(jnp.int32, (rows, cols), 0)
    c = col0 + lax.broadcasted_iota(jnp.int32, (rows, cols), 1)
    return c <= r


def _attn_fwd(q, q_c0, qr, k, k_c0, kr, v, v_c0, B, Sq, Sk, H, causal, scale, name, into=None, o_c0=0, o_width=None):
    tq, tk, sub = _attn_shapes(Sq, Sk, causal, "fwd")
    nq, nk, nsub = Sq // tq, Sk // tk, tq // sub
    rope = qr is not None

    def body(*refs):
        refs = list(refs)
        qn_ref = refs.pop(0)
        qr_ref = refs.pop(0) if rope else None
        kn_ref = refs.pop(0)
        kr_ref = refs.pop(0) if rope else None
        v_ref = refs.pop(0)
        if into is not None:
            refs.pop(0)
        o_ref, lse_ref, m_s, l_s, acc = refs
        qi = pl.program_id(2)
        m_s[...] = jnp.full_like(m_s, NEG)
        l_s[...] = jnp.zeros_like(l_s)
        acc[...] = jnp.zeros_like(acc)
        qs = []
        for r in range(nsub):
            qn = qn_ref[r * sub:(r + 1) * sub, :].astype(BF16)
            qs.append(jnp.concatenate([qn, qr_ref[r * sub:(r + 1) * sub, :]], axis=1) if rope else qn)

        def step(j, masked):
            ks = pl.ds(pl.multiple_of(j * tk, tk), tk)
            kk = jnp.concatenate([kn_ref[ks, :], kr_ref[ks, :]], axis=1) if rope else kn_ref[ks, :]
            vv = v_ref[ks, :]
            for r in range(nsub):
                rows = slice(r * sub, (r + 1) * sub)
                s = lax.dot_general(qs[r], kk, _DOT_DIMS["nt"], preferred_element_type=F32) * scale
                if masked:
                    s = jnp.where(_mask(qi * tq + r * sub, j * tk, sub, tk), s, NEG)
                m_old = m_s[rows, :]
                m_new = jnp.maximum(m_old, jnp.max(s, axis=-1, keepdims=True))
                p = jnp.exp(s - m_new)
                alpha = jnp.exp(m_old - m_new)
                l_s[rows, :] = alpha * l_s[rows, :] + jnp.sum(p, axis=-1, keepdims=True)
                acc[rows, :] = alpha * acc[rows, :] + jnp.dot(p.astype(BF16), vv, preferred_element_type=F32)
                m_s[rows, :] = m_new

        def unmasked(j, carry):
            step(j, False)
            return carry

        if causal:
            lax.fori_loop(0, qi, unmasked, 0)
            step(qi, True)
        else:
            lax.fori_loop(0, nk, unmasked, 0)
        o_ref[...] = (acc[...] / l_s[...]).astype(o_ref.dtype)
        lse_ref[...] = m_s[...] + jnp.log(l_s[...])

    qspec = lambda c0: pl.BlockSpec((tq, 128), lambda b, h, i: (b * nq + i, c0 + h))
    kspec = lambda c0: pl.BlockSpec((Sk, 128), lambda b, h, i: (b, c0 + h))
    in_specs, args = [qspec(q_c0)], [q]
    if rope:
        in_specs.append(qspec(0)); args.append(qr)
    in_specs.append(kspec(k_c0)); args.append(k)
    if rope:
        in_specs.append(pl.BlockSpec((Sk, 128), lambda b, h, i: (b, 0))); args.append(kr)
    in_specs.append(kspec(v_c0)); args.append(v)
    aliases = {}
    if into is not None:
        aliases = {len(args): 0}
        in_specs.append(pl.BlockSpec(memory_space=pl.ANY)); args.append(into)
        o_shape = jax.ShapeDtypeStruct(into.shape, into.dtype)
    else:
        o_shape = jax.ShapeDtypeStruct((B * Sq, o_width), F32)
    return _call(body, name, [o_shape, jax.ShapeDtypeStruct((B * H, Sq, 1), F32)], grid=(B, H, nq), in_specs=in_specs,
                 out_specs=[qspec(o_c0), pl.BlockSpec((None, tq, 1), lambda b, h, i: (b * H + h, i, 0))],
                 scratch=[pltpu.VMEM((tq, 1), F32), pltpu.VMEM((tq, 1), F32), pltpu.VMEM((tq, 128), F32)],
                 dims=("parallel", "parallel", "arbitrary"), aliases=aliases)(*args)


def _attn_bwd(q, q_c0, qr, k, k_c0, kr, v, v_c0, o, do, o_c0, lse, B, Sq, Sk, H, causal, scale, name, dq_into=None):
    tq, tk, sub = _attn_shapes(Sq, Sk, causal, "bwd")
    nq, nk, nsub = Sq // tq, Sk // tk, tq // sub
    rope = qr is not None
    dk_w = 256 if rope else 128

    def body(*refs):
        refs = list(refs)
        qn_ref = refs.pop(0)
        qr_ref = refs.pop(0) if rope else None
        kn_ref = refs.pop(0)
        kr_ref = refs.pop(0) if rope else None
        v_ref, o_ref, do_ref, lse_ref = refs[:4]
        refs = refs[4 + (0 if dq_into is None else 1):]
        dqn_ref = refs.pop(0)
        dqr_ref = refs.pop(0) if rope else None
        dkn_ref = refs.pop(0)
        dkr_ref = refs.pop(0) if rope else None
        dv_ref, q_s, do_s, dl_s, dq_acc, dk_acc, dv_acc = refs
        kj = pl.program_id(2)

        @pl.when(kj == 0)
        def _():
            qn = qn_ref[...].astype(BF16)
            q_s[...] = jnp.concatenate([qn, qr_ref[...]], axis=1) if rope else qn
            dof = do_ref[...]
            do_s[...] = dof.astype(BF16)
            dl_s[...] = jnp.sum(dof * o_ref[...], axis=-1, keepdims=True)
            dq_acc[...] = jnp.zeros_like(dq_acc)

        kk = jnp.concatenate([kn_ref[...], kr_ref[...]], axis=1) if rope else kn_ref[...]
        vv = v_ref[...]
        dk_acc[...] = jnp.zeros_like(dk_acc)
        dv_acc[...] = jnp.zeros_like(dv_acc)

        def step(i, masked):
            for r in range(nsub):
                rows = pl.ds(pl.multiple_of(i * tq + r * sub, sub), sub)
                qq, dob = q_s[rows, :], do_s[rows, :]
                s = lax.dot_general(qq, kk, _DOT_DIMS["nt"], preferred_element_type=F32) * scale
                if masked:
                    s = jnp.where(_mask(i * tq + r * sub, kj * tk, sub, tk), s, NEG)
                p = jnp.exp(s - lse_ref[rows, :])
                dp = lax.dot_general(dob, vv, _DOT_DIMS["nt"], preferred_element_type=F32)
                ds = (p * (dp - dl_s[rows, :]) * scale).astype(BF16)
                dv_acc[...] += lax.dot_general(p.astype(BF16), dob, _DOT_DIMS["tn"], preferred_element_type=F32)
                dk_acc[...] += lax.dot_general(ds, qq, _DOT_DIMS["tn"], preferred_element_type=F32)
                dq_acc[rows, :] += jnp.dot(ds, kk, preferred_element_type=F32)

        def unmasked(i, carry):
            step(i, False)
            return carry

        if causal:
            step(kj, True)
            lax.fori_loop(kj + 1, nq, unmasked, 0)
        else:
            lax.fori_loop(0, nq, unmasked, 0)
        dkn_ref[...] = dk_acc[:, 0:128]
        if rope:
            dkr_ref[...] = dk_acc[:, 128:256]
        dv_ref[...] = dv_acc[...]

        @pl.when(kj == nk - 1)
        def _():
            dqn_ref[...] = dq_acc[:, 0:128].astype(dqn_ref.dtype)
            if rope:
                dqr_ref[...] = dq_acc[:, 128:256]

    qspec = lambda c0: pl.BlockSpec((Sq, 128), lambda b, h, j: (b, c0 + h))
    kspec = lambda c0: pl.BlockSpec((tk, 128), lambda b, h, j: (b * nk + j, c0 + h))
    in_specs, args = [qspec(q_c0)], [q]
    if rope:
        in_specs.append(qspec(0)); args.append(qr)
    in_specs.append(kspec(k_c0)); args.append(k)
    if rope:
        in_specs.append(pl.BlockSpec((tk, 128), lambda b, h, j: (b * nk + j, 0))); args.append(kr)
    in_specs += [kspec(v_c0), qspec(o_c0), qspec(o_c0), pl.BlockSpec((None, Sq, 1), lambda b, h, j: (b * H + h, 0, 0))]
    args += [v, o, do, lse]
    h_rows_q = jax.ShapeDtypeStruct((B * Sq, H * 128), F32)
    h_rows_k = jax.ShapeDtypeStruct((B * Sk, H * 128), F32)
    out_shape, out_specs, aliases = [h_rows_q], [qspec(0)], None
    if dq_into is not None:
        aliases = {len(args): 0}
        in_specs.append(pl.BlockSpec(memory_space=pl.ANY)); args.append(dq_into[0])
        out_shape, out_specs = [jax.ShapeDtypeStruct(dq_into[0].shape, dq_into[0].dtype)], [qspec(dq_into[1])]
    if rope:
        out_shape.append(h_rows_q); out_specs.append(qspec(0))
    out_shape.append(h_rows_k); out_specs.append(kspec(0))
    if rope:
        out_shape.append(h_rows_k); out_specs.append(kspec(0))
    out_shape.append(h_rows_k); out_specs.append(kspec(0))
    return _call(body, name, out_shape, grid=(B, H, nk), in_specs=in_specs, out_specs=out_specs,
                 scratch=[pltpu.VMEM((Sq, dk_w), BF16), pltpu.VMEM((Sq, 128), BF16), pltpu.VMEM((Sq, 1), F32),
                          pltpu.VMEM((Sq, dk_w), F32), pltpu.VMEM((tk, dk_w), F32), pltpu.VMEM((tk, 128), F32)],
                 dims=("parallel", "parallel", "arbitrary"), aliases=aliases)(*args)


def _mem_attention_fwd(proj, q_col, ycat, mem2, mem_g, w_mem, B, S, tag):
    M = mem2.shape[0] // B
    (memn,) = _rowwise(_f_rms, [mem2], [mem_g], [(mem2.shape[1], BF16)], tag + "_memnorm")
    kvm = _mm(memn, w_mem, "nn", BF16, tag + "_memkv")
    o_c0 = ycat.shape[1] // 128 - MEM_HEADS
    ycat, lse = _attn_fwd(proj, q_col // 128, None, kvm, 0, None, kvm, MEM_HEADS, B, S, M, MEM_HEADS, False,
                          MEM_HEAD_DIM ** -0.5, tag + "_memattn", into=ycat, o_c0=o_c0)
    return ycat, (memn, kvm, lse)


def _mem_attention_bwd(proj, q_col, ycat, d_ycat, d_proj, saved, mem2, mem_g, w_mem, B, S, tag):
    memn, kvm, lse = saved
    M = mem2.shape[0] // B
    o_c0 = ycat.shape[1] // 128 - MEM_HEADS
    d_q, d_k, d_v = _attn_bwd(proj, q_col // 128, None, kvm, 0, None, kvm, MEM_HEADS, ycat, d_ycat, o_c0, lse, B, S, M,
                              MEM_HEADS, False, MEM_HEAD_DIM ** -0.5, tag + "_memattn_bwd", dq_into=(d_proj, q_col // 128))
    d_kvm = jnp.concatenate([d_k, d_v], axis=1).astype(BF16)
    d_w_mem = _mm(memn, d_kvm, "tn", F32, tag + "_memkv_dw")
    d_memn = _mm(d_kvm, w_mem, "nt", F32, tag + "_memkv_dx")
    _, d_mem_g = _rowwise_bwd(_f_rms, [mem2], [mem_g], [d_memn], 1, tag + "_memnorm_bwd")
    return d_q, d_w_mem, d_mem_g


def _rope_tables(positions):
    inv_freq = 1.0 / (ROPE_THETA ** (jnp.arange(0, MLA_ROPE, 2, dtype=F32) / MLA_ROPE))
    ang = positions.astype(F32).reshape(-1, 1) * inv_freq
    cos, sin, zero = jnp.cos(ang), jnp.sin(ang), jnp.zeros_like(ang)
    return jnp.concatenate([cos, zero, cos, zero], axis=1), jnp.concatenate([-sin, zero, sin, zero], axis=1)


def _forward_backward(x, mem, positions, target, W):
    B, S, D = x.shape
    T = B * S
    conv_w = W["conv_dw"].shape[1]
    mix_w = 2 * D
    h0 = x.reshape(T, D)
    mem2 = mem.reshape(-1, D)
    tgt = target.reshape(T, D)
    row = lambda v: v.reshape(1, -1)
    n_nope = MLA_HEADS * MLA_NOPE

    g0 = row(W["norm_g"][0])
    (u0,) = _rowwise(_f_rms, [h0], [g0], [(D, BF16)], "l0_norm", carry=W.carry("l0_norm"))
    proj0 = _mm(u0, W["conv_w_in"], "nn", F32, "l0_in", carry=W.carry("l0_in"))
    a0, gate0 = (proj0, conv_w, 0), (proj0, conv_w, 1)
    qm0_col, z0_col = 2 * conv_w, 2 * conv_w + MEM_WIDTH
    (glu,) = _rowwise(_f_glu, [a0, gate0], [], [(conv_w, F32)], "l0_glu", carry=W.carry("l0_glu"))
    dw, dwb = W["conv_dw"], row(W["conv_dw_b"][0])
    cv = _dwconv_fwd(glu.reshape(B, S, conv_w), dw, dwb, "l0_dwconv", carry=W.carry("l0_dwconv")).reshape(T, conv_w)
    ln_g, ln_b = row(W["conv_ln_g"][0]), row(W["conv_ln_b"][0])
    (ycat0,) = _rowwise(_f_ln_silu, [cv], [ln_g, ln_b], [(conv_w, F32, mix_w)], "l0_ln", carry=W.carry("l0_ln"))
    mg0 = row(W["mem_norm_g"][0])
    ycat0, mem_saved0 = _mem_attention_fwd(proj0, qm0_col, ycat0, mem2, mg0, W["w_mem_kv"][0], B, S, "l0")
    y0 = _gate_fwd(ycat0, proj0, z0_col, "l0_gate")
    h1 = _mm(y0, W["w_out"][0], "nn", F32, "l0_out", res=h0)

    g1 = row(W["norm_g"][1])
    (u1,) = _rowwise(_f_rms, [h1], [g1], [(D, BF16)], "l1_norm")
    proj1 = _mm(u1, W["mla_w_in"], "nn", F32, "l1_in")
    cq, ckv = (proj1, Q_RANK, 0), (proj1, KV_RANK, Q_RANK // KV_RANK)
    qm1_col = Q_RANK + KV_RANK
    z1_col = qm1_col + MEM_WIDTH
    kr_col = z1_col + mix_w
    qg, kvg = row(W["mla_q_norm_g"]), row(W["mla_kv_norm_g"])
    (cqn,) = _rowwise(_f_rms, [cq], [qg], [(Q_RANK, BF16)], "l1_qnorm")
    (ckvn,) = _rowwise(_f_rms, [ckv], [kvg], [(KV_RANK, BF16)], "l1_kvnorm")
    qf = _mm(cqn, W["mla_w_uq"], "nn", F32, "l1_uq")
    kvf = _mm(ckvn, W["mla_w_ukv"], "nn", BF16, "l1_ukv")
    cos_p, sin_p = _rope_tables(positions)
    qr, kr = _rowwise(_f_rope, [(qf, n_nope, 1), (proj1, 128, kr_col // 128), cos_p, sin_p], [],
                      [(n_nope, BF16), (128, BF16)], "l1_rope")
    scale1 = MLA_QK ** -0.5
    ycat1, lse1 = _attn_fwd(qf, 0, qr, kvf, 0, kr, kvf, MLA_HEADS, B, S, S, MLA_HEADS, True, scale1, "l1_attn",
                            o_width=mix_w)
    mg1 = row(W["mem_norm_g"][1])
    ycat1, mem_saved1 = _mem_attention_fwd(proj1, qm1_col, ycat1, mem2, mg1, W["w_mem_kv"][1], B, S, "l1")
    y1 = _gate_fwd(ycat1, proj1, z1_col, "l1_gate")
    h2 = _mm(y1, W["w_out"][1], "nn", F32, "l1_out", res=h1)

    gf = row(W["final_norm_g"])
    dh2, d_gf, loss128 = _final_loss(h2, tgt, gf, "final_loss")
    G = {"final_norm_g": d_gf.reshape(-1)}
    L1 = {}

    dy1 = _mm(dh2, W["w_out"][1], "nt", F32, "l1_out_dx")
    d_wout1 = _mm(y1, dh2, "tn", F32, "l1_out_dw")
    d_ycat1, d_proj1 = _gate_bwd(ycat1, proj1, z1_col, dy1, "l1_gate_bwd")
    d_proj1, d_wmem1, d_mg1 = _mem_attention_bwd(proj1, qm1_col, ycat1, d_ycat1, d_proj1, mem_saved1, mem2, mg1,
                                                 W["w_mem_kv"][1], B, S, "l1")
    d_qn, d_qr, d_kn, d_kr_heads, d_v = _attn_bwd(qf, 0, qr, kvf, 0, kr, kvf, MLA_HEADS, ycat1, d_ycat1, 0, lse1, B, S, S,
                                                  MLA_HEADS, True, scale1, "l1_attn_bwd")
    d_xq, d_proj1 = _rowwise(_f_rope_t, [d_qr, d_kr_heads, cos_p, sin_p], [], [(n_nope, F32), (128, F32)], "l1_rope_bwd",
                             into=(1, d_proj1, kr_col // 128))
    d_qf = jnp.concatenate([d_qn, d_xq], axis=1).astype(BF16)
    d_kvf = jnp.concatenate([d_kn, d_v], axis=1).astype(BF16)
    d_cqn = _mm(d_qf, W["mla_w_uq"], "nt", F32, "l1_uq_dx")
    L1[("mla_w_uq", None)] = _mm(cqn, d_qf, "tn", F32, "l1_uq_dw")
    d_ckvn = _mm(d_kvf, W["mla_w_ukv"], "nt", F32, "l1_ukv_dx")
    L1[("mla_w_ukv", None)] = _mm(ckvn, d_kvf, "tn", F32, "l1_ukv_dw")
    d_proj1, d_qg = _rowwise_bwd(_f_rms, [cq], [qg], [d_cqn], 1, "l1_qnorm_bwd", into=(d_proj1, cq[2]))
    d_proj1, d_kvg = _rowwise_bwd(_f_rms, [ckv], [kvg], [d_ckvn], 1, "l1_kvnorm_bwd", into=(d_proj1, ckv[2]))
    L1[("mla_w_in", None)] = _mm(u1, d_proj1, "tn", F32, "l1_in_dw")
    L1[("w_mem_kv", 1)], L1[("w_out", 1)] = d_wmem1, d_wout1
    W.ready("l1", L1)
    d_u1 = _mm(d_proj1, W["mla_w_in"], "nt", F32, "l1_in_dx", carry=W.carry("l1_in_dx"))
    dh1, d_g1 = _rowwise_bwd(_f_rms, [h1], [g1], [d_u1], 1, "l1_norm_bwd", add=dh2)

    dy0 = _mm(dh1, W["w_out"][0], "nt", F32, "l0_out_dx")
    d_wout0 = _mm(y0, dh1, "tn", F32, "l0_out_dw")
    d_ycat0, d_proj0 = _gate_bwd(ycat0, proj0, z0_col, dy0, "l0_gate_bwd")
    d_proj0, d_wmem0, d_mg0 = _mem_attention_bwd(proj0, qm0_col, ycat0, d_ycat0, d_proj0, mem_saved0, mem2, mg0,
                                                 W["w_mem_kv"][0], B, S, "l0")
    W.ready("l0a", {("w_mem_kv", 0): d_wmem0, ("w_out", 0): d_wout0})
    d_cv, d_ln_g, d_ln_b = _rowwise_bwd(_f_ln_silu, [cv], [ln_g, ln_b], [(d_ycat0, conv_w, 0)], 1, "l0_ln_bwd",
                                        carry=W.carry("l0_ln_bwd"))
    d_glu, d_dw, d_dwb = _dwconv_bwd(glu.reshape(B, S, conv_w), dw, d_cv.reshape(B, S, conv_w), "l0_dwconv_bwd",
                                     carry=W.carry("l0_dwconv_bwd"))
    d_proj0 = _glu_bwd(proj0, d_glu.reshape(T, conv_w), d_proj0, "l0_glu_bwd")
    d_conv_w_in = _mm(u0, d_proj0, "tn", F32, "l0_in_dw", carry=W.carry("l0_in_dw"))
    W.ready("l0b", {("conv_w_in", None): d_conv_w_in, ("conv_dw", None): d_dw,
                    ("mla_q_norm_g", None): d_qg.reshape(-1), ("mla_kv_norm_g", None): d_kvg.reshape(-1)})
    d_u0 = _mm(d_proj0, W["conv_w_in"], "nt", F32, "l0_in_dx", carry=W.carry("l0_in_dx"))
    dx, d_g0 = _rowwise_bwd(_f_rms, [h0], [g0], [d_u0], 1, "l0_norm_bwd", add=dh1)
    dx = dx.reshape(B, S, D)

    G["norm_g"] = jnp.concatenate([d_g0, d_g1], axis=0)
    G["mem_norm_g"] = jnp.concatenate([d_mg0, d_mg1], axis=0)
    G["conv_dw_b"] = d_dwb
    G["conv_ln_g"], G["conv_ln_b"] = d_ln_g, d_ln_b
    return loss128[0, 0], dx, G


def _mla_in_perm(w):
    c2 = Q_RANK + KV_RANK
    zero = jnp.zeros((w.shape[0], HALF_ROPE), w.dtype)
    return jnp.concatenate([w[:, :c2], w[:, c2 + MLA_ROPE:], w[:, c2:c2 + HALF_ROPE], zero,
                            w[:, c2 + HALF_ROPE:c2 + MLA_ROPE], zero], axis=1)


def _mla_in_unperm(g):
    c2 = Q_RANK + KV_RANK
    r = g.shape[1] - 128
    return jnp.concatenate([g[:, :c2], g[:, r:r + HALF_ROPE], g[:, r + 64:r + 64 + HALF_ROPE], g[:, c2:r]], axis=1)


def _uq_perm(w):
    n = w.shape[0]
    w3 = w.reshape(n, MLA_HEADS, MLA_QK)
    zero = jnp.zeros((n, MLA_HEADS, HALF_ROPE), w.dtype)
    rope = jnp.concatenate([w3[:, :, MLA_NOPE:MLA_NOPE + HALF_ROPE], zero, w3[:, :, MLA_NOPE + HALF_ROPE:], zero], axis=2)
    return jnp.concatenate([w3[:, :, :MLA_NOPE].reshape(n, -1), rope.reshape(n, -1)], axis=1)


def _uq_unperm(g):
    n = g.shape[0]
    n_nope = MLA_HEADS * MLA_NOPE
    rope = g[:, n_nope:].reshape(n, MLA_HEADS, 128)
    return jnp.concatenate([g[:, :n_nope].reshape(n, MLA_HEADS, MLA_NOPE), rope[:, :, :HALF_ROPE],
                            rope[:, :, 64:64 + HALF_ROPE]], axis=2).reshape(n, -1)


def _ukv_perm(w):
    w3 = w.reshape(w.shape[0], MLA_HEADS, MLA_NOPE + MLA_V)
    return jnp.concatenate([w3[:, :, :MLA_NOPE].reshape(w.shape[0], -1), w3[:, :, MLA_NOPE:].reshape(w.shape[0], -1)], axis=1)


def _ukv_unperm(g):
    n = g.shape[0]
    half = MLA_HEADS * MLA_NOPE
    return jnp.concatenate([g[:, :half].reshape(n, MLA_HEADS, MLA_NOPE), g[:, half:].reshape(n, MLA_HEADS, MLA_V)],
                           axis=2).reshape(n, -1)


_ROW_CUT = ("w_mem_kv", "w_out")
_COL_CUT = ("conv_w_in", "mla_w_in", "mla_w_uq", "mla_w_ukv", "conv_dw")
_BIG = ("w_mem_kv", "w_out", "conv_w_in", "mla_w_in", "mla_w_uq", "mla_w_ukv")
_SMALL_SHARDED = ("conv_dw", "mla_q_norm_g", "mla_kv_norm_g")
_REPLICATED = ("norm_g", "mem_norm_g", "conv_dw_b", "conv_ln_g", "conv_ln_b", "final_norm_g")
_PERM = {"mla_w_in": (_mla_in_perm, _mla_in_unperm), "mla_w_uq": (_uq_perm, _uq_unperm), "mla_w_ukv": (_ukv_perm, _ukv_unperm)}


def _join(n, blocks):
    if n in _ROW_CUT:
        _, L, r, c = blocks.shape
        return blocks.transpose(1, 0, 2, 3).reshape(L, N_DEV * r, c)
    if n in _COL_CUT:
        _, _, r, c = blocks.shape
        return blocks.reshape(N_DEV, r, c).transpose(1, 0, 2).reshape(r, N_DEV * c)
    return blocks.reshape(-1)


def _cut(n, full, shard_shape):
    if n in _ROW_CUT:
        L, r, c = shard_shape
        return full.reshape(L, N_DEV, r, c).transpose(1, 0, 2, 3)
    if n in _COL_CUT:
        _, r, c = shard_shape
        return full.reshape(r, N_DEV, c).transpose(1, 0, 2).reshape(N_DEV, 1, r, c)
    return full.reshape(N_DEV, 1, -1)


def _flat_pad(parts, size):
    flat = jnp.concatenate([p.reshape(-1) for p in parts])
    return jnp.concatenate([flat, jnp.zeros((size - flat.shape[0],), flat.dtype)])


SMALL_LANES = 128 * 8


def _as_tiles(flat_parts):
    total = sum(p.size for p in flat_parts)
    size = -(-total // SMALL_LANES) * SMALL_LANES
    return _flat_pad(flat_parts, size).reshape(8, size // 8)


def _split_flat(flat, like):
    out, o = [], 0
    for a in like:
        out.append(flat[o:o + a.size].reshape(a.shape))
        o += a.size
    return out


_HBM = pl.BlockSpec(memory_space=pltpu.HBM)
_VMEM = pl.BlockSpec(memory_space=pltpu.VMEM)


def _position():
    return lax.axis_index("x"), lax.axis_index("y"), lax.axis_index("c")


def _dma_sems(n):
    return [pltpu.SemaphoreType.DMA((n,)), pltpu.SemaphoreType.DMA((n,))]


def _run_stage(stage, name):
    n_in, n_out = len(stage.ins), len(stage.out_shapes)

    def body(*refs):
        ins, outs, sems = refs[:n_in], refs[n_in:n_in + n_out], refs[n_in + n_out:]
        stage.start(ins, outs, sems)
        stage.wait(ins, outs, sems)

    outs = _call(body, name, stage.out_shapes, in_specs=[_HBM] * n_in, out_specs=[_HBM] * n_out, scratch=stage.sems,
                 aliases=stage.aliases)(*stage.ins)
    _deliver(stage, outs)
    return stage.outs


def _gather_chips_stage(shards):
    n = len(shards)

    def copies(x_refs, out_refs, sems):
        send_sems, recv_sems, _ = sems
        x, y, c = _position()
        peers = [(x, y, 1 - c), (1 - x, y, c), (x, 1 - y, c), (1 - x, 1 - y, c)]
        out = []
        for a in range(n):
            for k, (px, py, pc) in enumerate(peers):
                send = pltpu.make_async_remote_copy(src_ref=x_refs[a], dst_ref=out_refs[a].at[4 * x + 2 * y + c],
                                                    send_sem=send_sems.at[4 * a + k], recv_sem=recv_sems.at[4 * a + k],
                                                    device_id=(px, py, pc), device_id_type=MESH)
                recv = pltpu.make_async_remote_copy(src_ref=x_refs[a], dst_ref=out_refs[a].at[4 * px + 2 * py + pc],
                                                    send_sem=send_sems.at[4 * a + k], recv_sem=recv_sems.at[4 * a + k],
                                                    device_id=(px, py, pc), device_id_type=MESH)
                out.append((send, recv))
        return out

    def local(x_refs, out_refs, sems):
        x, y, c = _position()
        return [pltpu.make_async_copy(x_refs[a], out_refs[a].at[4 * x + 2 * y + c], sems[2].at[a]) for a in range(n)]

    def start(x_refs, out_refs, sems):
        for cp in local(x_refs, out_refs, sems):
            cp.start()
        for send, _ in copies(x_refs, out_refs, sems):
            send.start()

    def wait(x_refs, out_refs, sems):
        for send, recv in copies(x_refs, out_refs, sems):
            recv.wait_recv()
            send.wait_send()
        for cp in local(x_refs, out_refs, sems):
            cp.wait()

    return _Stage(shards, [jax.ShapeDtypeStruct((N_DEV,) + a.shape, a.dtype) for a in shards],
                  _dma_sems(4 * n) + [pltpu.SemaphoreType.DMA((n,))], start, wait)


def _gather_sibling_stage(bufs):
    n = len(bufs)

    def copies(out_refs, sems):
        send_sems, recv_sems = sems
        x, y, c = _position()
        out = []
        for a in range(n):
            for j, (px, py) in enumerate([(1 - x, y), (x, 1 - y), (1 - x, 1 - y)]):
                mine, theirs = out_refs[a].at[4 * px + 2 * py + c], out_refs[a].at[4 * px + 2 * py + (1 - c)]
                send = pltpu.make_async_remote_copy(src_ref=mine, dst_ref=mine, send_sem=send_sems.at[3 * a + j],
                                                    recv_sem=recv_sems.at[3 * a + j], device_id=(x, y, 1 - c),
                                                    device_id_type=MESH)
                recv = pltpu.make_async_remote_copy(src_ref=mine, dst_ref=theirs, send_sem=send_sems.at[3 * a + j],
                                                    recv_sem=recv_sems.at[3 * a + j], device_id=(x, y, 1 - c),
                                                    device_id_type=MESH)
                out.append((send, recv))
        return out

    def start(_, out_refs, sems):
        for send, _r in copies(out_refs, sems):
            send.start()

    def wait(_, out_refs, sems):
        for send, recv in copies(out_refs, sems):
            recv.wait_recv()
            send.wait_send()

    return _Stage(bufs, [jax.ShapeDtypeStruct(b.shape, b.dtype) for b in bufs], _dma_sems(3 * n), start, wait,
                  aliases={a: a for a in range(n)})


def _all_gather_small(v, name):
    r, n = v.shape

    def body(x_ref, out_ref, send_sems, recv_sems, local_sem):
        x, y, c = _position()
        me = 4 * x + 2 * y + c
        mine = pltpu.make_async_copy(x_ref, out_ref.at[me], local_sem)
        mine.start()
        flips = [(fx, fy, fc) for fx in (0, 1) for fy in (0, 1) for fc in (0, 1)][1:]
        copies = []
        for k, (fx, fy, fc) in enumerate(flips):
            peer = (x ^ fx, y ^ fy, c ^ fc)
            cp = pltpu.make_async_remote_copy(src_ref=x_ref, dst_ref=out_ref.at[me], send_sem=send_sems.at[k],
                                              recv_sem=recv_sems.at[k], device_id=peer, device_id_type=MESH)
            cp.start()
            copies.append(cp)
        for k, (fx, fy, fc) in enumerate(flips):
            px, py, pc = x ^ fx, y ^ fy, c ^ fc
            src = out_ref.at[4 * px + 2 * py + pc]
            pltpu.make_async_remote_copy(src_ref=x_ref, dst_ref=src, send_sem=send_sems.at[k], recv_sem=recv_sems.at[k],
                                         device_id=(px, py, pc), device_id_type=MESH).wait_recv()
        for cp in copies:
            cp.wait_send()
        mine.wait()

    return _call(body, name, jax.ShapeDtypeStruct((N_DEV, r, n), v.dtype), in_specs=[_VMEM], out_specs=_VMEM,
                 scratch=_dma_sems(7) + [pltpu.SemaphoreType.DMA(())])(v)


def _reduce_sibling_stage(gs):
    n = len(gs)

    def copies(g_refs, out_refs, sems):
        send_sems, recv_sems = sems
        x, y, c = _position()
        return [pltpu.make_async_remote_copy(src_ref=g_refs[a].at[2 * k + (1 - c)], dst_ref=out_refs[a].at[k],
                                             send_sem=send_sems.at[4 * a + k], recv_sem=recv_sems.at[4 * a + k],
                                             device_id=(x, y, 1 - c), device_id_type=MESH)
                for a in range(n) for k in range(4)]

    def start(g_refs, out_refs, sems):
        for cp in copies(g_refs, out_refs, sems):
            cp.start()

    def wait(g_refs, out_refs, sems):
        for cp in copies(g_refs, out_refs, sems):
            cp.wait()

    return _Stage(gs, [jax.ShapeDtypeStruct((4,) + g.shape[1:], g.dtype) for g in gs], _dma_sems(4 * n), start, wait)


def _rows2d(shape):
    cols = shape[-1]
    rows = 1
    for s in shape[:-1]:
        rows *= s
    return rows, cols


def _add_own(g, recv, name):
    rows, cols = _rows2d(g.shape[1:])
    tr = _pick(rows, 256, 8)
    c = lax.axis_index("c").astype(jnp.int32).reshape(1)

    def body(c_ref, g_ref, r_ref, o_ref):
        o_ref[...] = g_ref[...] + r_ref[...]

    grid_spec = pltpu.PrefetchScalarGridSpec(
        num_scalar_prefetch=1, grid=(4, rows // tr),
        in_specs=[pl.BlockSpec((None, None, tr, cols), lambda k, i, c_ref: (k, c_ref[0], i, 0)),
                  pl.BlockSpec((None, tr, cols), lambda k, i, c_ref: (k, i, 0))],
        out_specs=pl.BlockSpec((None, tr, cols), lambda k, i, c_ref: (k, i, 0)))
    return _call(body, name, jax.ShapeDtypeStruct((4, rows, cols), F32), grid_spec=grid_spec,
                 dims=("parallel", "parallel"))(c, g.reshape(4, 2, rows, cols), recv.reshape(4, rows, cols))


def _reduce_chips_stage(pas):
    n = len(pas)

    def copies(pa_refs, out_refs, sems):
        send_sems, recv_sems, _ = sems
        x, y, c = _position()
        my_chip = 2 * x + y
        out = []
        for a in range(n):
            for j, (px, py) in enumerate([(1 - x, y), (x, 1 - y), (1 - x, 1 - y)]):
                send = pltpu.make_async_remote_copy(src_ref=pa_refs[a].at[2 * px + py], dst_ref=out_refs[a].at[my_chip],
                                                    send_sem=send_sems.at[3 * a + j], recv_sem=recv_sems.at[3 * a + j],
                                                    device_id=(px, py, c), device_id_type=MESH)
                recv = pltpu.make_async_remote_copy(src_ref=pa_refs[a].at[2 * px + py], dst_ref=out_refs[a].at[2 * px + py],
                                                    send_sem=send_sems.at[3 * a + j], recv_sem=recv_sems.at[3 * a + j],
                                                    device_id=(px, py, c), device_id_type=MESH)
                out.append((send, recv))
        return out

    def local(pa_refs, out_refs, sems):
        x, y, _ = _position()
        return [pltpu.make_async_copy(pa_refs[a].at[2 * x + y], out_refs[a].at[2 * x + y], sems[2].at[a]) for a in range(n)]

    def start(pa_refs, out_refs, sems):
        for cp in local(pa_refs, out_refs, sems):
            cp.start()
        for send, _r in copies(pa_refs, out_refs, sems):
            send.start()

    def wait(pa_refs, out_refs, sems):
        for send, recv in copies(pa_refs, out_refs, sems):
            recv.wait_recv()
            send.wait_send()
        for cp in local(pa_refs, out_refs, sems):
            cp.wait()

    return _Stage(pas, [jax.ShapeDtypeStruct(pa.shape, pa.dtype) for pa in pas],
                  _dma_sems(3 * n) + [pltpu.SemaphoreType.DMA((n,))], start, wait)


def _adamw_math(w, g, m, v):
    m = ADAM_B1 * m + (1.0 - ADAM_B1) * g
    v = ADAM_B2 * v + (1.0 - ADAM_B2) * (g * g)
    m_hat = m / (1.0 - ADAM_B1 ** ADAM_STEP)
    v_hat = v / (1.0 - ADAM_B2 ** ADAM_STEP)
    delta = -ADAM_LR * (m_hat / (jnp.sqrt(v_hat) + ADAM_EPS) + ADAM_WD * w)
    return delta, m, v


def _sum_adamw(parts, w, m, v, name):
    n, rows, cols = parts.shape
    tr = _pick(rows, 128, 8)

    def body(p_ref, w_ref, m_ref, v_ref, g_ref, d_ref, nm_ref, nv_ref):
        g = p_ref[0]
        for k in range(1, n):
            g = g + p_ref[k]
        d, nm, nv = _adamw_math(w_ref[...], g, m_ref[...], v_ref[...])
        g_ref[...], d_ref[...], nm_ref[...], nv_ref[...] = g, d, nm, nv

    blk = pl.BlockSpec((tr, cols), lambda i: (i, 0))
    return _call(body, name, [jax.ShapeDtypeStruct((rows, cols), F32)] * 4, grid=(rows // tr,),
                 in_specs=[pl.BlockSpec((n, tr, cols), lambda i: (0, i, 0)), blk, blk, blk],
                 out_specs=[blk] * 4, dims=("parallel",))(parts, w, m, v)


_WEIGHTS = ("norm_g", "mem_norm_g", "w_mem_kv", "w_out", "conv_w_in", "conv_dw", "conv_dw_b", "conv_ln_g", "conv_ln_b",
            "mla_w_in", "mla_q_norm_g", "mla_w_uq", "mla_kv_norm_g", "mla_w_ukv", "final_norm_g")


_GATHER_GROUPS = {"a": ("conv_w_in",), "b": ("w_mem_kv", "w_out"), "c": ("mla_w_in", "mla_w_uq", "mla_w_ukv")}
_CARRIERS = {"l0_norm": ("gather chips", ("a",)), "l0_in": ("gather chips", ("b",)), "l0_glu": ("gather sibling", ("b",)),
             "l0_dwconv": ("gather chips", ("c",)), "l0_ln": ("gather sibling", ("c",)),
             "l1_in_dx": ("reduce sibling", ("l1",)), "l0_ln_bwd": ("reduce sibling", ("l0a",)),
             "l0_dwconv_bwd": ("reduce chips", ("l1",)), "l0_in_dw": ("reduce chips", ("l0a",)),
             "l0_in_dx": ("reduce sibling alone, then chips", ("l0b",))}


class _Schedule:
    def __init__(self, w):
        self.w, self.full, self.gather, self.reduce, self.reduced = w, {}, {}, {}, {}
        small = _all_gather_small(_as_tiles([w[n] for n in _SMALL_SHARDED]), "gather_small_weights").reshape(N_DEV, -1)
        o = 0
        for n in _SMALL_SHARDED:
            self.full[n] = _join(n, small[:, o:o + w[n].size].reshape((N_DEV,) + w[n].shape))
            o += w[n].size
        for n in _REPLICATED:
            self.full[n] = w[n]

    def carry(self, call):
        kind, groups = _CARRIERS[call]
        stages = []
        for g in groups:
            if kind == "gather chips":
                self.gather[g] = [_gather_chips_stage([self.w[n].astype(BF16) for n in _GATHER_GROUPS[g]])]
                stages.append(self.gather[g][0])
            elif kind == "gather sibling":
                self.gather[g].append(_gather_sibling_stage(self.gather[g][0].outs))
                stages.append(self.gather[g][1])
            elif kind == "reduce sibling":
                self.reduce[g]["sibling"] = _reduce_sibling_stage(self.reduce[g]["cut"])
                stages.append(self.reduce[g]["sibling"])
            else:
                r = self.reduce[g]
                if kind != "reduce chips":
                    r["sibling"] = _reduce_sibling_stage(r["cut"])
                    _run_stage(r["sibling"], "reduce_sibling_" + g)
                partial = [_add_own(c, s, "reduce_add_%s_%d" % (g, i)) for i, (c, s) in enumerate(zip(r["cut"], r["sibling"].outs))]
                r["chips"] = _reduce_chips_stage(partial)
                stages.append(r["chips"])
        return stages[0] if len(stages) == 1 else _merge_stages(stages[0], stages[1])

    def __getitem__(self, name):
        if name not in self.full:
            g = [k for k, names in _GATHER_GROUPS.items() if name in names][0]
            if len(self.gather[g]) == 1:
                self.gather[g].append(_gather_sibling_stage(self.gather[g][0].outs))
                _run_stage(self.gather[g][1], "gather_sibling_" + g)
            for n, buf in zip(_GATHER_GROUPS[g], self.gather[g][1].outs):
                self.full[n] = _PERM[n][0](_join(n, buf)) if n in _PERM else _join(n, buf)
        return self.full[name]

    def ready(self, group, grads):
        keys, cut, small = [], [], []
        for (n, layer), g in grads.items():
            if n in _SMALL_SHARDED:
                small.append(_cut(n, g, self.w[n].shape).reshape(N_DEV, -1))
                continue
            keys.append((n, layer))
            if layer is not None:
                cut.append(g.reshape((N_DEV,) + self.w[n].shape[1:]))
            else:
                cut.append(_cut(n, _PERM[n][1](g) if n in _PERM else g, self.w[n].shape))
        if small:
            keys.append(("small", None))
            cut.append(jax.vmap(lambda r: _as_tiles([r]))(jnp.concatenate(small, axis=1)))
        self.reduce[group] = {"keys": keys, "cut": cut}

    def finish(self):
        out = {}
        for r in self.reduce.values():
            out.update(dict(zip(r["keys"], r["chips"].outs)))
        return out


def kernel(x, mem, positions, norm_g, mem_norm_g, w_mem_kv, w_out, conv_w_in, conv_dw, conv_dw_b, conv_ln_g, conv_ln_b, mla_w_in, mla_q_norm_g, mla_w_uq, mla_kv_norm_g, mla_w_ukv, final_norm_g, loss_target, m_norm_g, m_mem_norm_g, m_w_mem_kv, m_w_out, m_conv_w_in, m_conv_dw, m_conv_dw_b, m_conv_ln_g, m_conv_ln_b, m_mla_w_in, m_mla_q_norm_g, m_mla_w_uq, m_mla_kv_norm_g, m_mla_w_ukv, m_final_norm_g, v_norm_g, v_mem_norm_g, v_w_mem_kv, v_w_out, v_conv_w_in, v_conv_dw, v_conv_dw_b, v_conv_ln_g, v_conv_ln_b, v_mla_w_in, v_mla_q_norm_g, v_mla_w_uq, v_mla_kv_norm_g, v_mla_w_ukv, v_final_norm_g):
    w = dict(zip(_WEIGHTS, (norm_g, mem_norm_g, w_mem_kv, w_out, conv_w_in, conv_dw, conv_dw_b, conv_ln_g, conv_ln_b,
                            mla_w_in, mla_q_norm_g, mla_w_uq, mla_kv_norm_g, mla_w_ukv, final_norm_g)))
    m = dict(zip(_WEIGHTS, (m_norm_g, m_mem_norm_g, m_w_mem_kv, m_w_out, m_conv_w_in, m_conv_dw, m_conv_dw_b, m_conv_ln_g,
                            m_conv_ln_b, m_mla_w_in, m_mla_q_norm_g, m_mla_w_uq, m_mla_kv_norm_g, m_mla_w_ukv, m_final_norm_g)))
    v = dict(zip(_WEIGHTS, (v_norm_g, v_mem_norm_g, v_w_mem_kv, v_w_out, v_conv_w_in, v_conv_dw, v_conv_dw_b, v_conv_ln_g,
                            v_conv_ln_b, v_mla_w_in, v_mla_q_norm_g, v_mla_w_uq, v_mla_kv_norm_g, v_mla_w_ukv, v_final_norm_g)))

    sched = _Schedule(w)
    loss_local, dx, G = _forward_backward(x, mem, positions, loss_target, sched)
    loss = lax.psum(loss_local, ("x", "y", "c"))

    from_chips = sched.finish()
    out = [{}, {}, {}, {}]
    for n in _BIG:
        if n in _ROW_CUT:
            res = [_sum_adamw(from_chips[(n, l)], w[n][l], m[n][l], v[n][l], "adamw_%s_%d" % (n, l)) for l in range(w[n].shape[0])]
            res = [jnp.stack(r) for r in zip(*res)]
        else:
            rows, cols = _rows2d(w[n].shape)
            res = _sum_adamw(from_chips[(n, None)], w[n].reshape(rows, cols), m[n].reshape(rows, cols),
                             v[n].reshape(rows, cols), "adamw_" + n)
        for o, r in zip(out, res):
            o[n] = r.reshape(w[n].shape)
    small_like = [w[n] for n in _SMALL_SHARDED]
    res = _sum_adamw(from_chips[("small", None)], _as_tiles(small_like), _as_tiles([m[n] for n in _SMALL_SHARDED]),
                     _as_tiles([v[n] for n in _SMALL_SHARDED]), "adamw_small")
    for o, r in zip(out, res):
        for n, a in zip(_SMALL_SHARDED, _split_flat(r.reshape(-1), small_like)):
            o[n] = a

    rep_like = [w[n] for n in _REPLICATED]
    rep_parts = _all_gather_small(_as_tiles([G[n] for n in _REPLICATED]), "gather_replicated_grads")
    res = _sum_adamw(rep_parts, _as_tiles(rep_like), _as_tiles([m[n] for n in _REPLICATED]),
                     _as_tiles([v[n] for n in _REPLICATED]), "adamw_replicated")
    for o, r in zip(out, res):
        for n, a in zip(_REPLICATED, _split_flat(r.reshape(-1), rep_like)):
            o[n] = a

    return (loss, dx, *[out[0][n] for n in _WEIGHTS], *[out[1][n] for n in _WEIGHTS],
            *[out[2][n] for n in _WEIGHTS], *[out[3][n] for n in _WEIGHTS])
```

```python
import jax
import jax.numpy as jnp
from jax import lax
from jax.experimental import pallas as pl
from jax.experimental.pallas import tpu as pltpu

F32 = jnp.float32
BF16 = jnp.bfloat16
MESH = pl.DeviceIdType.MESH
N_DEV = 8
VMEM_LIMIT_BYTES = 48 * 1024 * 1024

MEM_HEADS, MEM_HEAD_DIM = 4, 128
MEM_WIDTH = MEM_HEADS * MEM_HEAD_DIM
CONV_KERNEL = 31
CONV_PAD = 32
MLA_HEADS, MLA_NOPE, MLA_ROPE, MLA_V = 12, 128, 64, 128
MLA_QK = MLA_NOPE + MLA_ROPE
HALF_ROPE = MLA_ROPE // 2
Q_RANK, KV_RANK = 512, 256
ROPE_THETA = 10000.0
RMS_EPS = 1e-6
LN_EPS = 1e-5
ADAM_LR, ADAM_B1, ADAM_B2, ADAM_EPS, ADAM_WD, ADAM_STEP = 0.001, 0.9, 0.999, 1e-08, 0.01, 10
NEG = -1e30


class _Stage:
    def __init__(self, ins, out_shapes, sems, start, wait, aliases=None):
        self.ins, self.out_shapes, self.sems = list(ins), list(out_shapes), list(sems)
        self.start, self.wait, self.aliases, self.outs = start, wait, dict(aliases or {}), None


def _merge_stages(a, b):
    na_i, na_o, na_s = len(a.ins), len(a.out_shapes), len(a.sems)

    def start(i, o, s):
        a.start(i[:na_i], o[:na_o], s[:na_s])
        b.start(i[na_i:], o[na_o:], s[na_s:])

    def wait(i, o, s):
        a.wait(i[:na_i], o[:na_o], s[:na_s])
        b.wait(i[na_i:], o[na_o:], s[na_s:])

    aliases = dict(a.aliases)
    aliases.update({na_i + k: na_o + v for k, v in b.aliases.items()})
    merged = _Stage(a.ins + b.ins, a.out_shapes + b.out_shapes, a.sems + b.sems, start, wait, aliases)
    merged.parts = (a, b)
    return merged


def _deliver(stage, outs):
    stage.outs = list(outs)
    if hasattr(stage, "parts"):
        a, b = stage.parts
        _deliver(a, outs[:len(a.out_shapes)])
        _deliver(b, outs[len(a.out_shapes):])


def _call(body, name, out_shape, grid=None, in_specs=None, out_specs=None, scratch=(), dims=None, grid_spec=None, aliases=None,
          carry=None):
    params = dict(vmem_limit_bytes=VMEM_LIMIT_BYTES)
    if dims is not None:
        params["dimension_semantics"] = dims
    kw = {}
    if carry is not None:
        single = not isinstance(out_shape, (list, tuple))
        main_out = [out_shape] if single else list(out_shape)
        main_specs = [out_specs] if single else list(out_specs)
        n_in, n_out, n_scr = len(in_specs), len(main_out), len(scratch)
        x_in, x_out = len(carry.ins), len(carry.out_shapes)
        inner, steps = body, tuple(grid)

        def body(*refs):
            ins, xin = refs[:n_in], refs[n_in:n_in + x_in]
            outs = refs[n_in + x_in:n_in + x_in + n_out]
            xout = refs[n_in + x_in + n_out:n_in + x_in + n_out + x_out]
            scr = refs[n_in + x_in + n_out + x_out:n_in + x_in + n_out + x_out + n_scr]
            xsem = refs[n_in + x_in + n_out + x_out + n_scr:]
            ids = [pl.program_id(a) for a in range(len(steps))]
            first, last = ids[0] == 0, ids[0] == steps[0] - 1
            for a in range(1, len(steps)):
                first = jnp.logical_and(first, ids[a] == 0)
                last = jnp.logical_and(last, ids[a] == steps[a] - 1)
            pl.when(first)(lambda: carry.start(xin, xout, xsem))
            inner(*ins, *outs, *scr)
            pl.when(last)(lambda: carry.wait(xin, xout, xsem))

        hbm = pl.BlockSpec(memory_space=pltpu.HBM)
        aliases = dict(aliases or {})
        aliases.update({n_in + k: n_out + v for k, v in carry.aliases.items()})
        res = _call(body, name, main_out + carry.out_shapes, grid=grid, in_specs=list(in_specs) + [hbm] * x_in,
                    out_specs=main_specs + [hbm] * x_out, scratch=list(scratch) + carry.sems, dims=dims, aliases=aliases)

        def run(*args):
            outs = res(*args, *carry.ins)
            _deliver(carry, outs[n_out:])
            return outs[0] if single else outs[:n_out]

        return run
    if aliases:
        kw["input_output_aliases"] = aliases
    if grid_spec is not None:
        kw["grid_spec"] = grid_spec
    else:
        if grid is not None:
            kw["grid"] = grid
        kw["in_specs"] = in_specs
        kw["out_specs"] = out_specs
        kw["scratch_shapes"] = list(scratch)
    return pl.pallas_call(body, name=name, out_shape=out_shape, compiler_params=pltpu.CompilerParams(**params), **kw)


def _pick(n, target, mult):
    best = None
    for d in range(mult, min(n, target) + 1, mult):
        if n % d == 0:
            best = d
    return n if best is None else best


_DOT_DIMS = {"nn": (((1,), (0,)), ((), ())), "nt": (((1,), (1,)), ((), ())), "tn": (((0,), (0,)), ((), ()))}


def _mm(a, b, mode, out_dtype, name, res=None, carry=None):
    if mode == "tn":
        a, mode = a.T, "nn"
    if mode == "nn":
        (M, K), N = a.shape, b.shape[1]
    else:
        (M, K), N = a.shape, b.shape[0]
    tm = _pick(M, 1024, 8)
    tn = _pick(N, 512, 128)
    tk = _pick(K, 1024, 128)
    nk = K // tk
    has_res = res is not None

    def body(*refs):
        if has_res:
            a_ref, b_ref, r_ref, o_ref, acc = refs
        else:
            a_ref, b_ref, o_ref, acc = refs
        k = pl.program_id(2)

        @pl.when(k == 0)
        def _():
            acc[...] = jnp.zeros_like(acc)

        acc[...] += lax.dot_general(a_ref[...].astype(BF16), b_ref[...].astype(BF16), _DOT_DIMS[mode],
                                    preferred_element_type=F32)

        @pl.when(k == nk - 1)
        def _():
            r = acc[...]
            if has_res:
                r = r + r_ref[...]
            o_ref[...] = r.astype(o_ref.dtype)

    a_spec = pl.BlockSpec((tm, tk), lambda i, j, k: (i, k))
    b_spec = {"nn": pl.BlockSpec((tk, tn), lambda i, j, k: (k, j)),
              "nt": pl.BlockSpec((tn, tk), lambda i, j, k: (j, k))}[mode]
    o_spec = pl.BlockSpec((tm, tn), lambda i, j, k: (i, j))
    in_specs = [a_spec, b_spec] + ([o_spec] if has_res else [])
    args = (a, b) + ((res,) if has_res else ())
    return _call(body, name, jax.ShapeDtypeStruct((M, N), out_dtype), grid=(M // tm, N // tn, nk),
                 in_specs=in_specs, out_specs=o_spec, scratch=[pltpu.VMEM((tm, tn), F32)],
                 dims=("parallel", "parallel", "arbitrary"), carry=carry)(*args)


def _views(rows):
    return [r if isinstance(r, tuple) else (r, r.shape[1], 0) for r in rows]


def _rowwise(f, rows, params, outs, name, tb=256, carry=None, into=None):
    rows = _views(rows)
    T = rows[0][0].shape[0]
    tb = min(tb, T)
    nr, npar = len(rows), len(params)
    outs = [o if len(o) == 3 else (o[0], o[1], o[0]) for o in outs]

    def body(*refs):
        vals = f(*[r[...].astype(F32) for r in refs[:nr]], *[p[...] for p in refs[nr:nr + npar]])
        for o_ref, v in zip(refs[nr + npar + (0 if into is None else 1):], vals):
            o_ref[...] = v.astype(o_ref.dtype)

    row_spec = lambda w, cb=0: pl.BlockSpec((tb, w), lambda i: (i, cb))
    par_spec = lambda w: pl.BlockSpec((1, w), lambda i: (0, 0))
    out_shape = [jax.ShapeDtypeStruct((T, tw), dt) for _, dt, tw in outs]
    out_specs = [row_spec(w) for w, _, _ in outs]
    in_specs = [row_spec(w, cb) for _, w, cb in rows] + [par_spec(p.shape[1]) for p in params]
    args = [r[0] for r in rows] + list(params)
    aliases = None
    if into is not None:
        k, arr, cb = into
        aliases = {len(args): k}
        in_specs.append(pl.BlockSpec(memory_space=pl.ANY))
        args.append(arr)
        out_shape[k] = jax.ShapeDtypeStruct(arr.shape, arr.dtype)
        out_specs[k] = row_spec(outs[k][0], cb)
    return _call(body, name, out_shape, grid=(T // tb,), in_specs=in_specs, out_specs=out_specs, dims=("parallel",),
                 carry=carry, aliases=aliases)(*args)


def _rowwise_bwd(f, rows, params, douts, n_diff, name, tb=256, carry=None, add=None, into=None):
    rows, douts = _views(rows), _views(douts)
    T = rows[0][0].shape[0]
    tb = min(tb, T)
    nr, npar, nd = len(rows), len(params), len(douts)
    n_add = 0 if add is None else 1

    def body(*refs):
        rv = [r[...].astype(F32) for r in refs[:nr]]
        pv = [p[...] for p in refs[nr:nr + npar]]
        dv = [d[...].astype(F32) for d in refs[nr + npar:nr + npar + nd]]
        o_refs = refs[nr + npar + nd + n_add + (0 if into is None else 1):]
        fixed = rv[n_diff:]

        def g(*xs):
            return tuple(f(*xs[:n_diff], *fixed, *xs[n_diff:]))

        _, vjp = jax.vjp(g, *rv[:n_diff], *pv)
        grads = list(vjp(tuple(dv)))
        if add is not None:
            grads[0] = grads[0] + refs[nr + npar + nd][...]
        for o_ref, gr in zip(o_refs[:n_diff], grads[:n_diff]):
            o_ref[...] = gr.astype(o_ref.dtype)
        first = pl.program_id(0) == 0
        for o_ref, gr in zip(o_refs[n_diff:], grads[n_diff:]):
            @pl.when(first)
            def _(o_ref=o_ref):
                o_ref[...] = jnp.zeros_like(o_ref)

            o_ref[...] += gr

    row_spec = lambda w, cb=0: pl.BlockSpec((tb, w), lambda i: (i, cb))
    par_spec = lambda w: pl.BlockSpec((1, w), lambda i: (0, 0))
    out_shape = ([jax.ShapeDtypeStruct((T, w), F32) for _, w, _ in rows[:n_diff]]
                 + [jax.ShapeDtypeStruct((1, p.shape[1]), F32) for p in params])
    out_specs = [row_spec(w) for _, w, _ in rows[:n_diff]] + [par_spec(p.shape[1]) for p in params]
    in_specs = ([row_spec(w, cb) for _, w, cb in rows] + [par_spec(p.shape[1]) for p in params]
                + [row_spec(w, cb) for _, w, cb in douts])
    args = [r[0] for r in rows] + list(params) + [d[0] for d in douts]
    aliases = None
    if add is not None:
        in_specs.append(row_spec(add.shape[1]))
        args.append(add)
    if into is not None:
        aliases = {len(args): 0}
        in_specs.append(pl.BlockSpec(memory_space=pl.ANY))
        args.append(into[0])
        out_shape[0] = jax.ShapeDtypeStruct(into[0].shape, into[0].dtype)
        out_specs[0] = row_spec(rows[0][1], into[1])
    return _call(body, name, out_shape, grid=(T // tb,), in_specs=in_specs, out_specs=out_specs,
                 dims=("arbitrary",), carry=carry, aliases=aliases)(*args)


def _sig(x):
    return 1.0 / (1.0 + jnp.exp(-x))


def _rms(x, g):
    return x * lax.rsqrt(jnp.mean(x * x, axis=-1, keepdims=True) + RMS_EPS) * g


def _f_rms(x, g):
    return (_rms(x, g),)


def _f_glu(a, gate):
    return (a * _sig(gate),)


def _f_ln_silu(x, g, b):
    mu = jnp.mean(x, axis=-1, keepdims=True)
    xc = x - mu
    var = jnp.mean(xc * xc, axis=-1, keepdims=True)
    y = xc * lax.rsqrt(var + LN_EPS) * g + b
    return (y * _sig(y),)


def _rope128(x, cos_p, sin_p):
    return x * cos_p + pltpu.roll(x, 64, 1) * sin_p


def _rope128_t(d, cos_p, sin_p):
    return d * cos_p + pltpu.roll(d * sin_p, 64, 1)


def _f_rope(xq, xk, cos_p, sin_p):
    heads = [_rope128(xq[:, h * 128:(h + 1) * 128], cos_p, sin_p) for h in range(MLA_HEADS)]
    return (jnp.concatenate(heads, axis=1), _rope128(xk, cos_p, sin_p))


def _f_rope_t(dq, dk_heads, cos_p, sin_p):
    heads = [_rope128_t(dq[:, h * 128:(h + 1) * 128], cos_p, sin_p) for h in range(MLA_HEADS)]
    dk = dk_heads[:, 0:128]
    for h in range(1, MLA_HEADS):
        dk = dk + dk_heads[:, h * 128:(h + 1) * 128]
    return (jnp.concatenate(heads, axis=1), _rope128_t(dk, cos_p, sin_p))


GATE_LANES = 256


def _gate_fwd(ycat, proj, z_col, name, tb=1024):
    T, width = ycat.shape
    zb = z_col // GATE_LANES

    def body(y_ref, z_ref, o_ref):
        z = z_ref[...]
        o_ref[...] = (y_ref[...] * (z * _sig(z))).astype(o_ref.dtype)

    blk = pl.BlockSpec((tb, GATE_LANES), lambda i, c: (i, c))
    return _call(body, name, jax.ShapeDtypeStruct((T, width), BF16), grid=(T // tb, width // GATE_LANES),
                 in_specs=[blk, pl.BlockSpec((tb, GATE_LANES), lambda i, c: (i, zb + c))], out_specs=blk,
                 dims=("parallel", "parallel"))(ycat, proj)


def _gate_bwd(ycat, proj, z_col, dy, name, tb=1024):
    T, width = ycat.shape
    zb = z_col // GATE_LANES

    def body(y_ref, z_ref, dy_ref, dycat_ref, dz_ref):
        z, d = z_ref[...], dy_ref[...]
        s = _sig(z)
        dycat_ref[...] = d * (z * s)
        dz_ref[...] = (d * y_ref[...] * (s * (1.0 + z * (1.0 - s)))).astype(dz_ref.dtype)

    blk = pl.BlockSpec((tb, GATE_LANES), lambda i, c: (i, c))
    zblk = pl.BlockSpec((tb, GATE_LANES), lambda i, c: (i, zb + c))
    return _call(body, name, [jax.ShapeDtypeStruct((T, width), F32), jax.ShapeDtypeStruct(proj.shape, BF16)],
                 grid=(T // tb, width // GATE_LANES), in_specs=[blk, zblk, blk], out_specs=[blk, zblk],
                 dims=("parallel", "parallel"))(ycat, proj, dy)


def _glu_bwd(proj, d_glu, d_proj, name, tb=256):
    T, w = d_glu.shape

    def body(a_ref, g_ref, d_ref, _, o_ref):
        s = _sig(g_ref[...])

        @pl.when(pl.program_id(1) == 0)
        def _():
            o_ref[...] = (d_ref[...] * s).astype(o_ref.dtype)

        @pl.when(pl.program_id(1) == 1)
        def _():
            o_ref[...] = (d_ref[...] * a_ref[...] * (s * (1.0 - s))).astype(o_ref.dtype)

    return _call(body, name, jax.ShapeDtypeStruct(d_proj.shape, d_proj.dtype), grid=(T // tb, 2),
                 in_specs=[pl.BlockSpec((tb, w), lambda i, c: (i, 0)), pl.BlockSpec((tb, w), lambda i, c: (i, 1)),
                           pl.BlockSpec((tb, w), lambda i, c: (i, 0)), pl.BlockSpec(memory_space=pl.ANY)],
                 out_specs=pl.BlockSpec((tb, w), lambda i, c: (i, c)), dims=("parallel", "arbitrary"),
                 aliases={3: 0})(proj, proj, d_glu, d_proj)


def _final_loss(h, tgt, g, name, tb=256):
    T, D = h.shape

    def body(h_ref, t_ref, g_ref, dh_ref, dg_ref, loss_ref):
        tv = t_ref[...]

        def rowloss(hh, gg):
            e = _rms(hh, gg) - tv
            return 0.5 * jnp.mean(e * e, axis=-1, keepdims=True)

        lr, vjp = jax.vjp(rowloss, h_ref[...], g_ref[...])
        dh, dg = vjp(jnp.ones_like(lr))
        dh_ref[...] = dh

        @pl.when(pl.program_id(0) == 0)
        def _():
            dg_ref[...] = jnp.zeros_like(dg_ref)
            loss_ref[...] = jnp.zeros_like(loss_ref)

        dg_ref[...] += dg
        loss_ref[...] += jnp.broadcast_to(jnp.sum(lr, axis=0, keepdims=True), loss_ref.shape)

    row = pl.BlockSpec((tb, D), lambda i: (i, 0))
    par = pl.BlockSpec((1, D), lambda i: (0, 0))
    return _call(body, name,
                 [jax.ShapeDtypeStruct((T, D), F32), jax.ShapeDtypeStruct((1, D), F32), jax.ShapeDtypeStruct((1, 128), F32)],
                 grid=(T // tb,), in_specs=[row, row, par],
                 out_specs=[row, par, pl.BlockSpec((1, 128), lambda i: (0, 0))], dims=("arbitrary",))(h, tgt, g)


CONV_ROWS = 128
CONV_LANES = 256


def _sublane_phases(pad, n):
    for r in range(1, 8):
        for c0 in range(0, n - 8, 256):
            rows = min(256, n - 8 - c0)
            pad[r, c0:c0 + rows, :] = pad[0, c0 + r:c0 + r + rows, :]


def _dwconv_fwd(x, w, b, name, carry=None):
    B, S, C = x.shape
    cb = CONV_LANES
    off = CONV_PAD - (CONV_KERNEL - 1)

    def body(x_ref, w_ref, b_ref, o_ref, pad):
        pad[0, 0:CONV_PAD, :] = jnp.zeros((CONV_PAD, cb), F32)
        pad[0, CONV_PAD:, :] = x_ref[...]
        _sublane_phases(pad, S + CONV_PAD)
        for t0 in range(0, S, CONV_ROWS):
            acc = jnp.broadcast_to(b_ref[...], (CONV_ROWS, cb))
            for k in range(CONV_KERNEL):
                r, base = (off + k) % 8, t0 + (off + k) // 8 * 8
                acc = acc + w_ref[k:k + 1, :] * pad[r, base:base + CONV_ROWS, :]
            o_ref[t0:t0 + CONV_ROWS, :] = acc

    return _call(body, name, jax.ShapeDtypeStruct((B, S, C), F32), grid=(B, C // cb),
                 in_specs=[pl.BlockSpec((None, S, cb), lambda i, j: (i, 0, j)),
                           pl.BlockSpec((CONV_KERNEL, cb), lambda i, j: (0, j)),
                           pl.BlockSpec((1, cb), lambda i, j: (0, j))],
                 out_specs=pl.BlockSpec((None, S, cb), lambda i, j: (i, 0, j)),
                 scratch=[pltpu.VMEM((8, S + CONV_PAD, cb), F32)], dims=("parallel", "parallel"), carry=carry)(x, w, b)


def _dwconv_bwd(x, w, dy, name, carry=None):
    B, S, C = x.shape
    cb = CONV_LANES
    off = CONV_PAD - (CONV_KERNEL - 1)
    groups = CONV_ROWS // 8

    def body(x_ref, w_ref, dy_ref, dx_ref, dw_ref, db_ref, dypad, wacc):
        dypad[0, 0:S, :] = dy_ref[...]
        dypad[0, S:, :] = jnp.zeros((CONV_PAD, cb), F32)
        _sublane_phases(dypad, S + CONV_PAD)
        wacc[...] = jnp.zeros_like(wacc)
        for t0 in range(0, S, CONV_ROWS):
            xc = x_ref[t0:t0 + CONV_ROWS, :]
            acc = jnp.zeros((CONV_ROWS, cb), F32)
            for k in range(CONV_KERNEL):
                o = (CONV_KERNEL - 1) - k
                dys = dypad[o % 8, t0 + o // 8 * 8:t0 + o // 8 * 8 + CONV_ROWS, :]
                acc = acc + w_ref[k:k + 1, :] * dys
                wacc[k] += jnp.sum((dys * xc).reshape(groups, 8, cb), axis=0)
            wacc[CONV_KERNEL] += jnp.sum(dy_ref[t0:t0 + CONV_ROWS, :].reshape(groups, 8, cb), axis=0)
            dx_ref[t0:t0 + CONV_ROWS, :] = acc

        @pl.when(pl.program_id(1) == 0)
        def _():
            dw_ref[...] = jnp.zeros_like(dw_ref)
            db_ref[...] = jnp.zeros_like(db_ref)

        for k in range(CONV_KERNEL):
            dw_ref[k:k + 1, :] += jnp.sum(wacc[k], axis=0, keepdims=True)
        db_ref[...] += jnp.sum(wacc[CONV_KERNEL], axis=0, keepdims=True)

    blk = pl.BlockSpec((None, S, cb), lambda j, i: (i, 0, j))
    return _call(body, name,
                 [jax.ShapeDtypeStruct((B, S, C), F32), jax.ShapeDtypeStruct((CONV_KERNEL, C), F32),
                  jax.ShapeDtypeStruct((1, C), F32)],
                 grid=(C // cb, B),
                 in_specs=[blk, pl.BlockSpec((CONV_KERNEL, cb), lambda j, i: (0, j)), blk],
                 out_specs=[blk, pl.BlockSpec((CONV_KERNEL, cb), lambda j, i: (0, j)),
                            pl.BlockSpec((1, cb), lambda j, i: (0, j))],
                 scratch=[pltpu.VMEM((8, S + CONV_PAD, cb), F32), pltpu.VMEM((CONV_KERNEL + 1, 8, cb), F32)],
                 dims=("parallel", "arbitrary"), carry=carry)(x, w, dy)


ATTN_TILE = 512
ATTN_SUB = {"fwd": 256, "bwd": 512}


def _attn_shapes(Sq, Sk, causal, pass_):
    tq = min(Sq, ATTN_TILE)
    tk = tq if causal else min(Sk, ATTN_TILE)
    return tq, tk, min(ATTN_SUB[pass_], tq)


def _mask(row0, col0, rows, cols):
    r = row0 + lax.broadcasted_iota(jnp.int32, (rows, cols), 0)
    c = col0 + lax.broadcasted_iota(jnp.int32, (rows, cols), 1)
    return c <= r


def _attn_fwd(q, q_c0, qr, k, k_c0, kr, v, v_c0, B, Sq, Sk, H, causal, scale, name, into=None, o_c0=0, o_width=None):
    tq, tk, sub = _attn_shapes(Sq, Sk, causal, "fwd")
    nq, nk, nsub = Sq // tq, Sk // tk, tq // sub
    rope = qr is not None

    def body(*refs):
        refs = list(refs)
        qn_ref = refs.pop(0)
        qr_ref = refs.pop(0) if rope else None
        kn_ref = refs.pop(0)
        kr_ref = refs.pop(0) if rope else None
        v_ref = refs.pop(0)
        if into is not None:
            refs.pop(0)
        o_ref, lse_ref, m_s, l_s, acc = refs
        qi = pl.program_id(2)
        m_s[...] = jnp.full_like(m_s, NEG)
        l_s[...] = jnp.zeros_like(l_s)
        acc[...] = jnp.zeros_like(acc)
        qs = []
        for r in range(nsub):
            qn = qn_ref[r * sub:(r + 1) * sub, :].astype(BF16)
            qs.append(jnp.concatenate([qn, qr_ref[r * sub:(r + 1) * sub, :]], axis=1) if rope else qn)

        def step(j, masked):
            ks = pl.ds(pl.multiple_of(j * tk, tk), tk)
            kk = jnp.concatenate([kn_ref[ks, :], kr_ref[ks, :]], axis=1) if rope else kn_ref[ks, :]
            vv = v_ref[ks, :]
            for r in range(nsub):
                rows = slice(r * sub, (r + 1) * sub)
                s = lax.dot_general(qs[r], kk, _DOT_DIMS["nt"], preferred_element_type=F32) * scale
                if masked:
                    s = jnp.where(_mask(qi * tq + r * sub, j * tk, sub, tk), s, NEG)
                m_old = m_s[rows, :]
                m_new = jnp.maximum(m_old, jnp.max(s, axis=-1, keepdims=True))
                p = jnp.exp(s - m_new)
                alpha = jnp.exp(m_old - m_new)
                l_s[rows, :] = alpha * l_s[rows, :] + jnp.sum(p, axis=-1, keepdims=True)
                acc[rows, :] = alpha * acc[rows, :] + jnp.dot(p.astype(BF16), vv, preferred_element_type=F32)
                m_s[rows, :] = m_new

        def unmasked(j, carry):
            step(j, False)
            return carry

        if causal:
            lax.fori_loop(0, qi, unmasked, 0)
            step(qi, True)
        else:
            lax.fori_loop(0, nk, unmasked, 0)
        o_ref[...] = (acc[...] / l_s[...]).astype(o_ref.dtype)
        lse_ref[...] = m_s[...] + jnp.log(l_s[...])

    qspec = lambda c0: pl.BlockSpec((tq, 128), lambda b, h, i: (b * nq + i, c0 + h))
    kspec = lambda c0: pl.BlockSpec((Sk, 128), lambda b, h, i: (b, c0 + h))
    in_specs, args = [qspec(q_c0)], [q]
    if rope:
        in_specs.append(qspec(0)); args.append(qr)
    in_specs.append(kspec(k_c0)); args.append(k)
    if rope:
        in_specs.append(pl.BlockSpec((Sk, 128), lambda b, h, i: (b, 0))); args.append(kr)
    in_specs.append(kspec(v_c0)); args.append(v)
    aliases = {}
    if into is not None:
        aliases = {len(args): 0}
        in_specs.append(pl.BlockSpec(memory_space=pl.ANY)); args.append(into)
        o_shape = jax.ShapeDtypeStruct(into.shape, into.dtype)
    else:
        o_shape = jax.ShapeDtypeStruct((B * Sq, o_width), F32)
    return _call(body, name, [o_shape, jax.ShapeDtypeStruct((B * H, Sq, 1), F32)], grid=(B, H, nq), in_specs=in_specs,
                 out_specs=[qspec(o_c0), pl.BlockSpec((None, tq, 1), lambda b, h, i: (b * H + h, i, 0))],
                 scratch=[pltpu.VMEM((tq, 1), F32), pltpu.VMEM((tq, 1), F32), pltpu.VMEM((tq, 128), F32)],
                 dims=("parallel", "parallel", "arbitrary"), aliases=aliases)(*args)


def _attn_bwd(q, q_c0, qr, k, k_c0, kr, v, v_c0, o, do, o_c0, lse, B, Sq, Sk, H, causal, scale, name, dq_into=None):
    tq, tk, sub = _attn_shapes(Sq, Sk, causal, "bwd")
    nq, nk, nsub = Sq // tq, Sk // tk, tq // sub
    rope = qr is not None
    dk_w = 256 if rope else 128

    def body(*refs):
        refs = list(refs)
        qn_ref = refs.pop(0)
        qr_ref = refs.pop(0) if rope else None
        kn_ref = refs.pop(0)
        kr_ref = refs.pop(0) if rope else None
        v_ref, o_ref, do_ref, lse_ref = refs[:4]
        refs = refs[4 + (0 if dq_into is None else 1):]
        dqn_ref = refs.pop(0)
        dqr_ref = refs.pop(0) if rope else None
        dkn_ref = refs.pop(0)
        dkr_ref = refs.pop(0) if rope else None
        dv_ref, q_s, do_s, dl_s, dq_acc, dk_acc, dv_acc = refs
        kj = pl.program_id(2)

        @pl.when(kj == 0)
        def _():
            qn = qn_ref[...].astype(BF16)
            q_s[...] = jnp.concatenate([qn, qr_ref[...]], axis=1) if rope else qn
            dof = do_ref[...]
            do_s[...] = dof.astype(BF16)
            dl_s[...] = jnp.sum(dof * o_ref[...], axis=-1, keepdims=True)
            dq_acc[...] = jnp.zeros_like(dq_acc)

        kk = jnp.concatenate([kn_ref[...], kr_ref[...]], axis=1) if rope else kn_ref[...]
        vv = v_ref[...]
        dk_acc[...] = jnp.zeros_like(dk_acc)
        dv_acc[...] = jnp.zeros_like(dv_acc)

        def step(i, masked):
            for r in range(nsub):
                rows = pl.ds(pl.multiple_of(i * tq + r * sub, sub), sub)
                qq, dob = q_s[rows, :], do_s[rows, :]
                s = lax.dot_general(qq, kk, _DOT_DIMS["nt"], preferred_element_type=F32) * scale
                if masked:
                    s = jnp.where(_mask(i * tq + r * sub, kj * tk, sub, tk), s, NEG)
                p = jnp.exp(s - lse_ref[rows, :])
                dp = lax.dot_general(dob, vv, _DOT_DIMS["nt"], preferred_element_type=F32)
                ds = (p * (dp - dl_s[rows, :]) * scale).astype(BF16)
                dv_acc[...] += lax.dot_general(p.astype(BF16), dob, _DOT_DIMS["tn"], preferred_element_type=F32)
                dk_acc[...] += lax.dot_general(ds, qq, _DOT_DIMS["tn"], preferred_element_type=F32)
                dq_acc[rows, :] += jnp.dot(ds, kk, preferred_element_type=F32)

        def unmasked(i, carry):
            step(i, False)
            return carry

        if causal:
            step(kj, True)
            lax.fori_loop(kj + 1, nq, unmasked, 0)
        else:
            lax.fori_loop(0, nq, unmasked, 0)
        dkn_ref[...] = dk_acc[:, 0:128]
        if rope:
            dkr_ref[...] = dk_acc[:, 128:256]
        dv_ref[...] = dv_acc[...]

        @pl.when(kj == nk - 1)
        def _():
            dqn_ref[...] = dq_acc[:, 0:128].astype(dqn_ref.dtype)
            if rope:
                dqr_ref[...] = dq_acc[:, 128:256]

    qspec = lambda c0: pl.BlockSpec((Sq, 128), lambda b, h, j: (b, c0 + h))
    kspec = lambda c0: pl.BlockSpec((tk, 128), lambda b, h, j: (b * nk + j, c0 + h))
    in_specs, args = [qspec(q_c0)], [q]
    if rope:
        in_specs.append(qspec(0)); args.append(qr)
    in_specs.append(kspec(k_c0)); args.append(k)
    if rope:
        in_specs.append(pl.BlockSpec((tk, 128), lambda b, h, j: (b * nk + j, 0))); args.append(kr)
    in_specs += [kspec(v_c0), qspec(o_c0), qspec(o_c0), pl.BlockSpec((None, Sq, 1), lambda b, h, j: (b * H + h, 0, 0))]
    args += [v, o, do, lse]
    h_rows_q = jax.ShapeDtypeStruct((B * Sq, H * 128), F32)
    h_rows_k = jax.ShapeDtypeStruct((B * Sk, H * 128), F32)
    out_shape, out_specs, aliases = [h_rows_q], [qspec(0)], None
    if dq_into is not None:
        aliases = {len(args): 0}
        in_specs.append(pl.BlockSpec(memory_space=pl.ANY)); args.append(dq_into[0])
        out_shape, out_specs = [jax.ShapeDtypeStruct(dq_into[0].shape, dq_into[0].dtype)], [qspec(dq_into[1])]
    if rope:
        out_shape.append(h_rows_q); out_specs.append(qspec(0))
    out_shape.append(h_rows_k); out_specs.append(kspec(0))
    if rope:
        out_shape.append(h_rows_k); out_specs.append(kspec(0))
    out_shape.append(h_rows_k); out_specs.append(kspec(0))
    return _call(body, name, out_shape, grid=(B, H, nk), in_specs=in_specs, out_specs=out_specs,
                 scratch=[pltpu.VMEM((Sq, dk_w), BF16), pltpu.VMEM((Sq, 128), BF16), pltpu.VMEM((Sq, 1), F32),
                          pltpu.VMEM((Sq, dk_w), F32), pltpu.VMEM((tk, dk_w), F32), pltpu.VMEM((tk, 128), F32)],
                 dims=("parallel", "parallel", "arbitrary"), aliases=aliases)(*args)


def _mem_attention_fwd(proj, q_col, ycat, mem2, mem_g, w_mem, B, S, tag):
    M = mem2.shape[0] // B
    (memn,) = _rowwise(_f_rms, [mem2], [mem_g], [(mem2.shape[1], BF16)], tag + "_memnorm")
    kvm = _mm(memn, w_mem, "nn", BF16, tag + "_memkv")
    o_c0 = ycat.shape[1] // 128 - MEM_HEADS
    ycat, lse = _attn_fwd(proj, q_col // 128, None, kvm, 0, None, kvm, MEM_HEADS, B, S, M, MEM_HEADS, False,
                          MEM_HEAD_DIM ** -0.5, tag + "_memattn", into=ycat, o_c0=o_c0)
    return ycat, (memn, kvm, lse)


def _mem_attention_bwd(proj, q_col, ycat, d_ycat, d_proj, saved, mem2, mem_g, w_mem, B, S, tag):
    memn, kvm, lse = saved
    M = mem2.shape[0] // B
    o_c0 = ycat.shape[1] // 128 - MEM_HEADS
    d_q, d_k, d_v = _attn_bwd(proj, q_col // 128, None, kvm, 0, None, kvm, MEM_HEADS, ycat, d_ycat, o_c0, lse, B, S, M,
                              MEM_HEADS, False, MEM_HEAD_DIM ** -0.5, tag + "_memattn_bwd", dq_into=(d_proj, q_col // 128))
    d_kvm = jnp.concatenate([d_k, d_v], axis=1).astype(BF16)
    d_w_mem = _mm(memn, d_kvm, "tn", F32, tag + "_memkv_dw")
    d_memn = _mm(d_kvm, w_mem, "nt", F32, tag + "_memkv_dx")
    _, d_mem_g = _rowwise_bwd(_f_rms, [mem2], [mem_g], [d_memn], 1, tag + "_memnorm_bwd")
    return d_q, d_w_mem, d_mem_g


def _rope_tables(positions):
    inv_freq = 1.0 / (ROPE_THETA ** (jnp.arange(0, MLA_ROPE, 2, dtype=F32) / MLA_ROPE))
    ang = positions.astype(F32).reshape(-1, 1) * inv_freq
    cos, sin, zero = jnp.cos(ang), jnp.sin(ang), jnp.zeros_like(ang)
    return jnp.concatenate([cos, zero, cos, zero], axis=1), jnp.concatenate([-sin, zero, sin, zero], axis=1)


def _forward_backward(x, mem, positions, target, W):
    B, S, D = x.shape
    T = B * S
    conv_w = W["conv_dw"].shape[1]
    mix_w = 2 * D
    h0 = x.reshape(T, D)
    mem2 = mem.reshape(-1, D)
    tgt = target.reshape(T, D)
    row = lambda v: v.reshape(1, -1)
    n_nope = MLA_HEADS * MLA_NOPE

    g0 = row(W["norm_g"][0])
    (u0,) = _rowwise(_f_rms, [h0], [g0], [(D, BF16)], "l0_norm", carry=W.carry("l0_norm"))
    proj0 = _mm(u0, W["conv_w_in"], "nn", F32, "l0_in", carry=W.carry("l0_in"))
    a0, gate0 = (proj0, conv_w, 0), (proj0, conv_w, 1)
    qm0_col, z0_col = 2 * conv_w, 2 * conv_w + MEM_WIDTH
    (glu,) = _rowwise(_f_glu, [a0, gate0], [], [(conv_w, F32)], "l0_glu", carry=W.carry("l0_glu"))
    dw, dwb = W["conv_dw"], row(W["conv_dw_b"][0])
    cv = _dwconv_fwd(glu.reshape(B, S, conv_w), dw, dwb, "l0_dwconv", carry=W.carry("l0_dwconv")).reshape(T, conv_w)
    ln_g, ln_b = row(W["conv_ln_g"][0]), row(W["conv_ln_b"][0])
    (ycat0,) = _rowwise(_f_ln_silu, [cv], [ln_g, ln_b], [(conv_w, F32, mix_w)], "l0_ln", carry=W.carry("l0_ln"))
    mg0 = row(W["mem_norm_g"][0])
    ycat0, mem_saved0 = _mem_attention_fwd(proj0, qm0_col, ycat0, mem2, mg0, W["w_mem_kv"][0], B, S, "l0")
    y0 = _gate_fwd(ycat0, proj0, z0_col, "l0_gate")
    h1 = _mm(y0, W["w_out"][0], "nn", F32, "l0_out", res=h0)

    g1 = row(W["norm_g"][1])
    (u1,) = _rowwise(_f_rms, [h1], [g1], [(D, BF16)], "l1_norm")
    proj1 = _mm(u1, W["mla_w_in"], "nn", F32, "l1_in")
    cq, ckv = (proj1, Q_RANK, 0), (proj1, KV_RANK, Q_RANK // KV_RANK)
    qm1_col = Q_RANK + KV_RANK
    z1_col = qm1_col + MEM_WIDTH
    kr_col = z1_col + mix_w
    qg, kvg = row(W["mla_q_norm_g"]), row(W["mla_kv_norm_g"])
    (cqn,) = _rowwise(_f_rms, [cq], [qg], [(Q_RANK, BF16)], "l1_qnorm")
    (ckvn,) = _rowwise(_f_rms, [ckv], [kvg], [(KV_RANK, BF16)], "l1_kvnorm")
    qf = _mm(cqn, W["mla_w_uq"], "nn", F32, "l1_uq")
    kvf = _mm(ckvn, W["mla_w_ukv"], "nn", BF16, "l1_ukv")
    cos_p, sin_p = _rope_tables(positions)
    qr, kr = _rowwise(_f_rope, [(qf, n_nope, 1), (proj1, 128, kr_col // 128), cos_p, sin_p], [],
                      [(n_nope, BF16), (128, BF16)], "l1_rope")
    scale1 = MLA_QK ** -0.5
    ycat1, lse1 = _attn_fwd(qf, 0, qr, kvf, 0, kr, kvf, MLA_HEADS, B, S, S, MLA_HEADS, True, scale1, "l1_attn",
                            o_width=mix_w)
    mg1 = row(W["mem_norm_g"][1])
    ycat1, mem_saved1 = _mem_attention_fwd(proj1, qm1_col, ycat1, mem2, mg1, W["w_mem_kv"][1], B, S, "l1")
    y1 = _gate_fwd(ycat1, proj1, z1_col, "l1_gate")
    h2 = _mm(y1, W["w_out"][1], "nn", F32, "l1_out", res=h1)

    gf = row(W["final_norm_g"])
    dh2, d_gf, loss128 = _final_loss(h2, tgt, gf, "final_loss")
    G = {"final_norm_g": d_gf.reshape(-1)}
    L1 = {}

    dy1 = _mm(dh2, W["w_out"][1], "nt", F32, "l1_out_dx")
    d_wout1 = _mm(y1, dh2, "tn", F32, "l1_out_dw")
    d_ycat1, d_proj1 = _gate_bwd(ycat1, proj1, z1_col, dy1, "l1_gate_bwd")
    d_proj1, d_wmem1, d_mg1 = _mem_attention_bwd(proj1, qm1_col, ycat1, d_ycat1, d_proj1, mem_saved1, mem2, mg1,
                                                 W["w_mem_kv"][1], B, S, "l1")
    d_qn, d_qr, d_kn, d_kr_heads, d_v = _attn_bwd(qf, 0, qr, kvf, 0, kr, kvf, MLA_HEADS, ycat1, d_ycat1, 0, lse1, B, S, S,
                                                  MLA_HEADS, True, scale1, "l1_attn_bwd")
    d_xq, d_proj1 = _rowwise(_f_rope_t, [d_qr, d_kr_heads, cos_p, sin_p], [], [(n_nope, F32), (128, F32)], "l1_rope_bwd",
                             into=(1, d_proj1, kr_col // 128))
    d_qf = jnp.concatenate([d_qn, d_xq], axis=1).astype(BF16)
    d_kvf = jnp.concatenate([d_kn, d_v], axis=1).astype(BF16)
    d_cqn = _mm(d_qf, W["mla_w_uq"], "nt", F32, "l1_uq_dx")
    L1[("mla_w_uq", None)] = _mm(cqn, d_qf, "tn", F32, "l1_uq_dw")
    d_ckvn = _mm(d_kvf, W["mla_w_ukv"], "nt", F32, "l1_ukv_dx")
    L1[("mla_w_ukv", None)] = _mm(ckvn, d_kvf, "tn", F32, "l1_ukv_dw")
    d_proj1, d_qg = _rowwise_bwd(_f_rms, [cq], [qg], [d_cqn], 1, "l1_qnorm_bwd", into=(d_proj1, cq[2]))
    d_proj1, d_kvg = _rowwise_bwd(_f_rms, [ckv], [kvg], [d_ckvn], 1, "l1_kvnorm_bwd", into=(d_proj1, ckv[2]))
    L1[("mla_w_in", None)] = _mm(u1, d_proj1, "tn", F32, "l1_in_dw")
    L1[("w_mem_kv", 1)], L1[("w_out", 1)] = d_wmem1, d_wout1
    W.ready("l1", L1)
    d_u1 = _mm(d_proj1, W["mla_w_in"], "nt", F32, "l1_in_dx", carry=W.carry("l1_in_dx"))
    dh1, d_g1 = _rowwise_bwd(_f_rms, [h1], [g1], [d_u1], 1, "l1_norm_bwd", add=dh2)

    dy0 = _mm(dh1, W["w_out"][0], "nt", F32, "l0_out_dx")
    d_wout0 = _mm(y0, dh1, "tn", F32, "l0_out_dw")
    d_ycat0, d_proj0 = _gate_bwd(ycat0, proj0, z0_col, dy0, "l0_gate_bwd")
    d_proj0, d_wmem0, d_mg0 = _mem_attention_bwd(proj0, qm0_col, ycat0, d_ycat0, d_proj0, mem_saved0, mem2, mg0,
                                                 W["w_mem_kv"][0], B, S, "l0")
    W.ready("l0a", {("w_mem_kv", 0): d_wmem0, ("w_out", 0): d_wout0})
    d_cv, d_ln_g, d_ln_b = _rowwise_bwd(_f_ln_silu, [cv], [ln_g, ln_b], [(d_ycat0, conv_w, 0)], 1, "l0_ln_bwd",
                                        carry=W.carry("l0_ln_bwd"))
    d_glu, d_dw, d_dwb = _dwconv_bwd(glu.reshape(B, S, conv_w), dw, d_cv.reshape(B, S, conv_w), "l0_dwconv_bwd",
                                     carry=W.carry("l0_dwconv_bwd"))
    d_proj0 = _glu_bwd(proj0, d_glu.reshape(T, conv_w), d_proj0, "l0_glu_bwd")
    d_conv_w_in = _mm(u0, d_proj0, "tn", F32, "l0_in_dw", carry=W.carry("l0_in_dw"))
    W.ready("l0b", {("conv_w_in", None): d_conv_w_in, ("conv_dw", None): d_dw,
                    ("mla_q_norm_g", None): d_qg.reshape(-1), ("mla_kv_norm_g", None): d_kvg.reshape(-1)})
    d_u0 = _mm(d_proj0, W["conv_w_in"], "nt", F32, "l0_in_dx", carry=W.carry("l0_in_dx"))
    dx, d_g0 = _rowwise_bwd(_f_rms, [h0], [g0], [d_u0], 1, "l0_norm_bwd", add=dh1)
    dx = dx.reshape(B, S, D)

    G["norm_g"] = jnp.concatenate([d_g0, d_g1], axis=0)
    G["mem_norm_g"] = jnp.concatenate([d_mg0, d_mg1], axis=0)
    G["conv_dw_b"] = d_dwb
    G["conv_ln_g"], G["conv_ln_b"] = d_ln_g, d_ln_b
    return loss128[0, 0], dx, G


def _mla_in_perm(w):
    c2 = Q_RANK + KV_RANK
    zero = jnp.zeros((w.shape[0], HALF_ROPE), w.dtype)
    return jnp.concatenate([w[:, :c2], w[:, c2 + MLA_ROPE:], w[:, c2:c2 + HALF_ROPE], zero,
                            w[:, c2 + HALF_ROPE:c2 + MLA_ROPE], zero], axis=1)


def _mla_in_unperm(g):
    c2 = Q_RANK + KV_RANK
    r = g.shape[1] - 128
    return jnp.concatenate([g[:, :c2], g[:, r:r + HALF_ROPE], g[:, r + 64:r + 64 + HALF_ROPE], g[:, c2:r]], axis=1)


def _uq_perm(w):
    n = w.shape[0]
    w3 = w.reshape(n, MLA_HEADS, MLA_QK)
    zero = jnp.zeros((n, MLA_HEADS, HALF_ROPE), w.dtype)
    rope = jnp.concatenate([w3[:, :, MLA_NOPE:MLA_NOPE + HALF_ROPE], zero, w3[:, :, MLA_NOPE + HALF_ROPE:], zero], axis=2)
    return jnp.concatenate([w3[:, :, :MLA_NOPE].reshape(n, -1), rope.reshape(n, -1)], axis=1)


def _uq_unperm(g):
    n = g.shape[0]
    n_nope = MLA_HEADS * MLA_NOPE
    rope = g[:, n_nope:].reshape(n, MLA_HEADS, 128)
    return jnp.concatenate([g[:, :n_nope].reshape(n, MLA_HEADS, MLA_NOPE), rope[:, :, :HALF_ROPE],
                            rope[:, :, 64:64 + HALF_ROPE]], axis=2).reshape(n, -1)


def _ukv_perm(w):
    w3 = w.reshape(w.shape[0], MLA_HEADS, MLA_NOPE + MLA_V)
    return jnp.concatenate([w3[:, :, :MLA_NOPE].reshape(w.shape[0], -1), w3[:, :, MLA_NOPE:].reshape(w.shape[0], -1)], axis=1)


def _ukv_unperm(g):
    n = g.shape[0]
    half = MLA_HEADS * MLA_NOPE
    return jnp.concatenate([g[:, :half].reshape(n, MLA_HEADS, MLA_NOPE), g[:, half:].reshape(n, MLA_HEADS, MLA_V)],
                           axis=2).reshape(n, -1)


_ROW_CUT = ("w_mem_kv", "w_out")
_COL_CUT = ("conv_w_in", "mla_w_in", "mla_w_uq", "mla_w_ukv", "conv_dw")
_BIG = ("w_mem_kv", "w_out", "conv_w_in", "mla_w_in", "mla_w_uq", "mla_w_ukv")
_SMALL_SHARDED = ("conv_dw", "mla_q_norm_g", "mla_kv_norm_g")
_REPLICATED = ("norm_g", "mem_norm_g", "conv_dw_b", "conv_ln_g", "conv_ln_b", "final_norm_g")
_PERM = {"mla_w_in": (_mla_in_perm, _mla_in_unperm), "mla_w_uq": (_uq_perm, _uq_unperm), "mla_w_ukv": (_ukv_perm, _ukv_unperm)}


def _join(n, blocks):
    if n in _ROW_CUT:
        _, L, r, c = blocks.shape
        return blocks.transpose(1, 0, 2, 3).reshape(L, N_DEV * r, c)
    if n in _COL_CUT:
        _, _, r, c = blocks.shape
        return blocks.reshape(N_DEV, r, c).transpose(1, 0, 2).reshape(r, N_DEV * c)
    return blocks.reshape(-1)


def _cut(n, full, shard_shape):
    if n in _ROW_CUT:
        L, r, c = shard_shape
        return full.reshape(L, N_DEV, r, c).transpose(1, 0, 2, 3)
    if n in _COL_CUT:
        _, r, c = shard_shape
        return full.reshape(r, N_DEV, c).transpose(1, 0, 2).reshape(N_DEV, 1, r, c)
    return full.reshape(N_DEV, 1, -1)


def _flat_pad(parts, size):
    flat = jnp.concatenate([p.reshape(-1) for p in parts])
    return jnp.concatenate([flat, jnp.zeros((size - flat.shape[0],), flat.dtype)])


SMALL_LANES = 128 * 8


def _as_tiles(flat_parts):
    total = sum(p.size for p in flat_parts)
    size = -(-total // SMALL_LANES) * SMALL_LANES
    return _flat_pad(flat_parts, size).reshape(8, size // 8)


def _split_flat(flat, like):
    out, o = [], 0
    for a in like:
        out.append(flat[o:o + a.size].reshape(a.shape))
        o += a.size
    return out


_HBM = pl.BlockSpec(memory_space=pltpu.HBM)
_VMEM = pl.BlockSpec(memory_space=pltpu.VMEM)


def _position():
    return lax.axis_index("x"), lax.axis_index("y"), lax.axis_index("c")


def _dma_sems(n):
    return [pltpu.SemaphoreType.DMA((n,)), pltpu.SemaphoreType.DMA((n,))]


def _run_stage(stage, name):
    n_in, n_out = len(stage.ins), len(stage.out_shapes)

    def body(*refs):
        ins, outs, sems = refs[:n_in], refs[n_in:n_in + n_out], refs[n_in + n_out:]
        stage.start(ins, outs, sems)
        stage.wait(ins, outs, sems)

    outs = _call(body, name, stage.out_shapes, in_specs=[_HBM] * n_in, out_specs=[_HBM] * n_out, scratch=stage.sems,
                 aliases=stage.aliases)(*stage.ins)
    _deliver(stage, outs)
    return stage.outs


def _gather_chips_stage(shards):
    n = len(shards)

    def copies(x_refs, out_refs, sems):
        send_sems, recv_sems, _ = sems
        x, y, c = _position()
        peers = [(x, y, 1 - c), (1 - x, y, c), (x, 1 - y, c), (1 - x, 1 - y, c)]
        out = []
        for a in range(n):
            for k, (px, py, pc) in enumerate(peers):
                send = pltpu.make_async_remote_copy(src_ref=x_refs[a], dst_ref=out_refs[a].at[4 * x + 2 * y + c],
                                                    send_sem=send_sems.at[4 * a + k], recv_sem=recv_sems.at[4 * a + k],
                                                    device_id=(px, py, pc), device_id_type=MESH)
                recv = pltpu.make_async_remote_copy(src_ref=x_refs[a], dst_ref=out_refs[a].at[4 * px + 2 * py + pc],
                                                    send_sem=send_sems.at[4 * a + k], recv_sem=recv_sems.at[4 * a + k],
                                                    device_id=(px, py, pc), device_id_type=MESH)
                out.append((send, recv))
        return out

    def local(x_refs, out_refs, sems):
        x, y, c = _position()
        return [pltpu.make_async_copy(x_refs[a], out_refs[a].at[4 * x + 2 * y + c], sems[2].at[a]) for a in range(n)]

    def start(x_refs, out_refs, sems):
        for cp in local(x_refs, out_refs, sems):
            cp.start()
        for send, _ in copies(x_refs, out_refs, sems):
            send.start()

    def wait(x_refs, out_refs, sems):
        for send, recv in copies(x_refs, out_refs, sems):
            recv.wait_recv()
            send.wait_send()
        for cp in local(x_refs, out_refs, sems):
            cp.wait()

    return _Stage(shards, [jax.ShapeDtypeStruct((N_DEV,) + a.shape, a.dtype) for a in shards],
                  _dma_sems(4 * n) + [pltpu.SemaphoreType.DMA((n,))], start, wait)


def _gather_sibling_stage(bufs):
    n = len(bufs)

    def copies(out_refs, sems):
        send_sems, recv_sems = sems
        x, y, c = _position()
        out = []
        for a in range(n):
            for j, (px, py) in enumerate([(1 - x, y), (x, 1 - y), (1 - x, 1 - y)]):
                mine, theirs = out_refs[a].at[4 * px + 2 * py + c], out_refs[a].at[4 * px + 2 * py + (1 - c)]
                send = pltpu.make_async_remote_copy(src_ref=mine, dst_ref=mine, send_sem=send_sems.at[3 * a + j],
                                                    recv_sem=recv_sems.at[3 * a + j], device_id=(x, y, 1 - c),
                                                    device_id_type=MESH)
                recv = pltpu.make_async_remote_copy(src_ref=mine, dst_ref=theirs, send_sem=send_sems.at[3 * a + j],
                                                    recv_sem=recv_sems.at[3 * a + j], device_id=(x, y, 1 - c),
                                                    device_id_type=MESH)
                out.append((send, recv))
        return out

    def start(_, out_refs, sems):
        for send, _r in copies(out_refs, sems):
            send.start()

    def wait(_, out_refs, sems):
        for send, recv in copies(out_refs, sems):
            recv.wait_recv()
            send.wait_send()

    return _Stage(bufs, [jax.ShapeDtypeStruct(b.shape, b.dtype) for b in bufs], _dma_sems(3 * n), start, wait,
                  aliases={a: a for a in range(n)})


def _all_gather_small(v, name):
    r, n = v.shape

    def body(x_ref, out_ref, send_sems, recv_sems, local_sem):
        x, y, c = _position()
        me = 4 * x + 2 * y + c
        mine = pltpu.make_async_copy(x_ref, out_ref.at[me], local_sem)
        mine.start()
        flips = [(fx, fy, fc) for fx in (0, 1) for fy in (0, 1) for fc in (0, 1)][1:]
        copies = []
        for k, (fx, fy, fc) in enumerate(flips):
            peer = (x ^ fx, y ^ fy, c ^ fc)
            cp = pltpu.make_async_remote_copy(src_ref=x_ref, dst_ref=out_ref.at[me], send_sem=send_sems.at[k],
                                              recv_sem=recv_sems.at[k], device_id=peer, device_id_type=MESH)
            cp.start()
            copies.append(cp)
        for k, (fx, fy, fc) in enumerate(flips):
            px, py, pc = x ^ fx, y ^ fy, c ^ fc
            src = out_ref.at[4 * px + 2 * py + pc]
            pltpu.make_async_remote_copy(src_ref=x_ref, dst_ref=src, send_sem=send_sems.at[k], recv_sem=recv_sems.at[k],
                                         device_id=(px, py, pc), device_id_type=MESH).wait_recv()
        for cp in copies:
            cp.wait_send()
        mine.wait()

    return _call(body, name, jax.ShapeDtypeStruct((N_DEV, r, n), v.dtype), in_specs=[_VMEM], out_specs=_VMEM,
                 scratch=_dma_sems(7) + [pltpu.SemaphoreType.DMA(())])(v)


def _reduce_sibling_stage(gs):
    n = len(gs)

    def copies(g_refs, out_refs, sems):
        send_sems, recv_sems = sems
        x, y, c = _position()
        return [pltpu.make_async_remote_copy(src_ref=g_refs[a].at[2 * k + (1 - c)], dst_ref=out_refs[a].at[k],
                                             send_sem=send_sems.at[4 * a + k], recv_sem=recv_sems.at[4 * a + k],
                                             device_id=(x, y, 1 - c), device_id_type=MESH)
                for a in range(n) for k in range(4)]

    def start(g_refs, out_refs, sems):
        for cp in copies(g_refs, out_refs, sems):
            cp.start()

    def wait(g_refs, out_refs, sems):
        for cp in copies(g_refs, out_refs, sems):
            cp.wait()

    return _Stage(gs, [jax.ShapeDtypeStruct((4,) + g.shape[1:], g.dtype) for g in gs], _dma_sems(4 * n), start, wait)


def _rows2d(shape):
    cols = shape[-1]
    rows = 1
    for s in shape[:-1]:
        rows *= s
    return rows, cols


def _add_own(g, recv, name):
    rows, cols = _rows2d(g.shape[1:])
    tr = _pick(rows, 256, 8)
    c = lax.axis_index("c").astype(jnp.int32).reshape(1)

    def body(c_ref, g_ref, r_ref, o_ref):
        o_ref[...] = g_ref[...] + r_ref[...]

    grid_spec = pltpu.PrefetchScalarGridSpec(
        num_scalar_prefetch=1, grid=(4, rows // tr),
        in_specs=[pl.BlockSpec((None, None, tr, cols), lambda k, i, c_ref: (k, c_ref[0], i, 0)),
                  pl.BlockSpec((None, tr, cols), lambda k, i, c_ref: (k, i, 0))],
        out_specs=pl.BlockSpec((None, tr, cols), lambda k, i, c_ref: (k, i, 0)))
    return _call(body, name, jax.ShapeDtypeStruct((4, rows, cols), F32), grid_spec=grid_spec,
                 dims=("parallel", "parallel"))(c, g.reshape(4, 2, rows, cols), recv.reshape(4, rows, cols))


def _reduce_chips_stage(pas):
    n = len(pas)

    def copies(pa_refs, out_refs, sems):
        send_sems, recv_sems, _ = sems
        x, y, c = _position()
        my_chip = 2 * x + y
        out = []
        for a in range(n):
            for j, (px, py) in enumerate([(1 - x, y), (x, 1 - y), (1 - x, 1 - y)]):
                send = pltpu.make_async_remote_copy(src_ref=pa_refs[a].at[2 * px + py], dst_ref=out_refs[a].at[my_chip],
                                                    send_sem=send_sems.at[3 * a + j], recv_sem=recv_sems.at[3 * a + j],
                                                    device_id=(px, py, c), device_id_type=MESH)
                recv = pltpu.make_async_remote_copy(src_ref=pa_refs[a].at[2 * px + py], dst_ref=out_refs[a].at[2 * px + py],
                                                    send_sem=send_sems.at[3 * a + j], recv_sem=recv_sems.at[3 * a + j],
                                                    device_id=(px, py, c), device_id_type=MESH)
                out.append((send, recv))
        return out

    def local(pa_refs, out_refs, sems):
        x, y, _ = _position()
        return [pltpu.make_async_copy(pa_refs[a].at[2 * x + y], out_refs[a].at[2 * x + y], sems[2].at[a]) for a in range(n)]

    def start(pa_refs, out_refs, sems):
        for cp in local(pa_refs, out_refs, sems):
            cp.start()
        for send, _r in copies(pa_refs, out_refs, sems):
            send.start()

    def wait(pa_refs, out_refs, sems):
        for send, recv in copies(pa_refs, out_refs, sems):
            recv.wait_recv()
            send.wait_send()
        for cp in local(pa_refs, out_refs, sems):
            cp.wait()

    return _Stage(pas, [jax.ShapeDtypeStruct(pa.shape, pa.dtype) for pa in pas],
                  _dma_sems(3 * n) + [pltpu.SemaphoreType.DMA((n,))], start, wait)


def _adamw_math(w, g, m, v):
    m = ADAM_B1 * m + (1.0 - ADAM_B1) * g
    v = ADAM_B2 * v + (1.0 - ADAM_B2) * (g * g)
    m_hat = m / (1.0 - ADAM_B1 ** ADAM_STEP)
    v_hat = v / (1.0 - ADAM_B2 ** ADAM_STEP)
    delta = -ADAM_LR * (m_hat / (jnp.sqrt(v_hat) + ADAM_EPS) + ADAM_WD * w)
    return delta, m, v


def _sum_adamw(parts, w, m, v, name):
    n, rows, cols = parts.shape
    tr = _pick(rows, 128, 8)

    def body(p_ref, w_ref, m_ref, v_ref, g_ref, d_ref, nm_ref, nv_ref):
        g = p_ref[0]
        for k in range(1, n):
            g = g + p_ref[k]
        d, nm, nv = _adamw_math(w_ref[...], g, m_ref[...], v_ref[...])
        g_ref[...], d_ref[...], nm_ref[...], nv_ref[...] = g, d, nm, nv

    blk = pl.BlockSpec((tr, cols), lambda i: (i, 0))
    return _call(body, name, [jax.ShapeDtypeStruct((rows, cols), F32)] * 4, grid=(rows // tr,),
                 in_specs=[pl.BlockSpec((n, tr, cols), lambda i: (0, i, 0)), blk, blk, blk],
                 out_specs=[blk] * 4, dims=("parallel",))(parts, w, m, v)


_WEIGHTS = ("norm_g", "mem_norm_g", "w_mem_kv", "w_out", "conv_w_in", "conv_dw", "conv_dw_b", "conv_ln_g", "conv_ln_b",
            "mla_w_in", "mla_q_norm_g", "mla_w_uq", "mla_kv_norm_g", "mla_w_ukv", "final_norm_g")


_GATHER_GROUPS = {"a": ("conv_w_in",), "b": ("w_mem_kv", "w_out"), "c": ("mla_w_in", "mla_w_uq", "mla_w_ukv")}
_CARRIERS = {"l0_norm": ("gather chips", ("a",)), "l0_in": ("gather chips", ("b",)), "l0_glu": ("gather sibling", ("b",)),
             "l0_dwconv": ("gather chips", ("c",)), "l0_ln": ("gather sibling", ("c",)),
             "l1_in_dx": ("reduce sibling", ("l1",)), "l0_ln_bwd": ("reduce sibling", ("l0a",)),
             "l0_dwconv_bwd": ("reduce chips", ("l1",)), "l0_in_dw": ("reduce chips", ("l0a",)),
             "l0_in_dx": ("reduce sibling alone, then chips", ("l0b",))}


class _Schedule:
    def __init__(self, w):
        self.w, self.full, self.gather, self.reduce, self.reduced = w, {}, {}, {}, {}
        small = _all_gather_small(_as_tiles([w[n] for n in _SMALL_SHARDED]), "gather_small_weights").reshape(N_DEV, -1)
        o = 0
        for n in _SMALL_SHARDED:
            self.full[n] = _join(n, small[:, o:o + w[n].size].reshape((N_DEV,) + w[n].shape))
            o += w[n].size
        for n in _REPLICATED:
            self.full[n] = w[n]

    def carry(self, call):
        kind, groups = _CARRIERS[call]
        stages = []
        for g in groups:
            if kind == "gather chips":
                self.gather[g] = [_gather_chips_stage([self.w[n].astype(BF16) for n in _GATHER_GROUPS[g]])]
                stages.append(self.gather[g][0])
            elif kind == "gather sibling":
                self.gather[g].append(_gather_sibling_stage(self.gather[g][0].outs))
                stages.append(self.gather[g][1])
            elif kind == "reduce sibling":
                self.reduce[g]["sibling"] = _reduce_sibling_stage(self.reduce[g]["cut"])
                stages.append(self.reduce[g]["sibling"])
            else:
                r = self.reduce[g]
                if kind != "reduce chips":
                    r["sibling"] = _reduce_sibling_stage(r["cut"])
                    _run_stage(r["sibling"], "reduce_sibling_" + g)
                partial = [_add_own(c, s, "reduce_add_%s_%d" % (g, i)) for i, (c, s) in enumerate(zip(r["cut"], r["sibling"].outs))]
                r["chips"] = _reduce_chips_stage(partial)
                stages.append(r["chips"])
        return stages[0] if len(stages) == 1 else _merge_stages(stages[0], stages[1])

    def __getitem__(self, name):
        if name not in self.full:
            g = [k for k, names in _GATHER_GROUPS.items() if name in names][0]
            if len(self.gather[g]) == 1:
                self.gather[g].append(_gather_sibling_stage(self.gather[g][0].outs))
                _run_stage(self.gather[g][1], "gather_sibling_" + g)
            for n, buf in zip(_GATHER_GROUPS[g], self.gather[g][1].outs):
                self.full[n] = _PERM[n][0](_join(n, buf)) if n in _PERM else _join(n, buf)
        return self.full[name]

    def ready(self, group, grads):
        keys, cut, small = [], [], []
        for (n, layer), g in grads.items():
            if n in _SMALL_SHARDED:
                small.append(_cut(n, g, self.w[n].shape).reshape(N_DEV, -1))
                continue
            keys.append((n, layer))
            if layer is not None:
                cut.append(g.reshape((N_DEV,) + self.w[n].shape[1:]))
            else:
                cut.append(_cut(n, _PERM[n][1](g) if n in _PERM else g, self.w[n].shape))
        if small:
            keys.append(("small", None))
            cut.append(jax.vmap(lambda r: _as_tiles([r]))(jnp.concatenate(small, axis=1)))
        self.reduce[group] = {"keys": keys, "cut": cut}

    def finish(self):
        out = {}
        for r in self.reduce.values():
            out.update(dict(zip(r["keys"], r["chips"].outs)))
        return out


def kernel(x, mem, positions, norm_g, mem_norm_g, w_mem_kv, w_out, conv_w_in, conv_dw, conv_dw_b, conv_ln_g, conv_ln_b, mla_w_in, mla_q_norm_g, mla_w_uq, mla_kv_norm_g, mla_w_ukv, final_norm_g, loss_target, m_norm_g, m_mem_norm_g, m_w_mem_kv, m_w_out, m_conv_w_in, m_conv_dw, m_conv_dw_b, m_conv_ln_g, m_conv_ln_b, m_mla_w_in, m_mla_q_norm_g, m_mla_w_uq, m_mla_kv_norm_g, m_mla_w_ukv, m_final_norm_g, v_norm_g, v_mem_norm_g, v_w_mem_kv, v_w_out, v_conv_w_in, v_conv_dw, v_conv_dw_b, v_conv_ln_g, v_conv_ln_b, v_mla_w_in, v_mla_q_norm_g, v_mla_w_uq, v_mla_kv_norm_g, v_mla_w_ukv, v_final_norm_g):
    w = dict(zip(_WEIGHTS, (norm_g, mem_norm_g, w_mem_kv, w_out, conv_w_in, conv_dw, conv_dw_b, conv_ln_g, conv_ln_b,
                            mla_w_in, mla_q_norm_g, mla_w_uq, mla_kv_norm_g, mla_w_ukv, final_norm_g)))
    m = dict(zip(_WEIGHTS, (m_norm_g, m_mem_norm_g, m_w_mem_kv, m_w_out, m_conv_w_in, m_conv_dw, m_conv_dw_b, m_conv_ln_g,
                            m_conv_ln_b, m_mla_w_in, m_mla_q_norm_g, m_mla_w_uq, m_mla_kv_norm_g, m_mla_w_ukv, m_final_norm_g)))
    v = dict(zip(_WEIGHTS, (v_norm_g, v_mem_norm_g, v_w_mem_kv, v_w_out, v_conv_w_in, v_conv_dw, v_conv_dw_b, v_conv_ln_g,
                            v_conv_ln_b, v_mla_w_in, v_mla_q_norm_g, v_mla_w_uq, v_mla_kv_norm_g, v_mla_w_ukv, v_final_norm_g)))

    sched = _Schedule(w)
    loss_local, dx, G = _forward_backward(x, mem, positions, loss_target, sched)
    loss = lax.psum(loss_local, ("x", "y", "c"))

    from_chips = sched.finish()
    out = [{}, {}, {}, {}]
    for n in _BIG:
        if n in _ROW_CUT:
            res = [_sum_adamw(from_chips[(n, l)], w[n][l], m[n][l], v[n][l], "adamw_%s_%d" % (n, l)) for l in range(w[n].shape[0])]
            res = [jnp.stack(r) for r in zip(*res)]
        else:
            rows, cols = _rows2d(w[n].shape)
            res = _sum_adamw(from_chips[(n, None)], w[n].reshape(rows, cols), m[n].reshape(rows, cols),
                             v[n].reshape(rows, cols), "adamw_" + n)
        for o, r in zip(out, res):
            o[n] = r.reshape(w[n].shape)
    small_like = [w[n] for n in _SMALL_SHARDED]
    res = _sum_adamw(from_chips[("small", None)], _as_tiles(small_like), _as_tiles([m[n] for n in _SMALL_SHARDED]),
                     _as_tiles([v[n] for n in _SMALL_SHARDED]), "adamw_small")
    for o, r in zip(out, res):
        for n, a in zip(_SMALL_SHARDED, _split_flat(r.reshape(-1), small_like)):
            o[n] = a

    rep_like = [w[n] for n in _REPLICATED]
    rep_parts = _all_gather_small(_as_tiles([G[n] for n in _REPLICATED]), "gather_replicated_grads")
    res = _sum_adamw(rep_parts, _as_tiles(rep_like), _as_tiles([m[n] for n in _REPLICATED]),
                     _as_tiles([v[n] for n in _REPLICATED]), "adamw_replicated")
    for o, r in zip(out, res):
        for n, a in zip(_REPLICATED, _split_flat(r.reshape(-1), rep_like)):
            o[n] = a

    return (loss, dx, *[out[0][n] for n in _WEIGHTS], *[out[1][n] for n in _WEIGHTS],
            *[out[2][n] for n in _WEIGHTS], *[out[3][n] for n in _WEIGHTS])
```

```python
import jax
import jax.numpy as jnp
from jax import lax
from jax.experimental import pallas as pl
from jax.experimental.pallas import tpu as pltpu

F32 = jnp.float32
BF16 = jnp.bfloat16
MESH = pl.DeviceIdType.MESH
N_DEV = 8
VMEM_LIMIT_BYTES = 48 * 1024 * 1024

MEM_HEADS, MEM_HEAD_DIM = 4, 128
MEM_WIDTH = MEM_HEADS * MEM_HEAD_DIM
CONV_KERNEL = 31
CONV_PAD = 32
MLA_HEADS, MLA_NOPE, MLA_ROPE, MLA_V = 12, 128, 64, 128
MLA_QK = MLA_NOPE + MLA_ROPE
HALF_ROPE = MLA_ROPE // 2
Q_RANK, KV_RANK = 512, 256
ROPE_THETA = 10000.0
RMS_EPS = 1e-6
LN_EPS = 1e-5
ADAM_LR, ADAM_B1, ADAM_B2, ADAM_EPS, ADAM_WD, ADAM_STEP = 0.001, 0.9, 0.999, 1e-08, 0.01, 10
NEG = -1e30


class _Stage:
    def __init__(self, ins, out_shapes, sems, start, wait, aliases=None):
        self.ins, self.out_shapes, self.sems = list(ins), list(out_shapes), list(sems)
        self.start, self.wait, self.aliases, self.outs = start, wait, dict(aliases or {}), None


def _call(body, name, out_shape, grid=None, in_specs=None, out_specs=None, scratch=(), dims=None, grid_spec=None, aliases=None,
          carry=None):
    params = dict(vmem_limit_bytes=VMEM_LIMIT_BYTES)
    if dims is not None:
        params["dimension_semantics"] = dims
    kw = {}
    if carry is not None:
        single = not isinstance(out_shape, (list, tuple))
        main_out = [out_shape] if single else list(out_shape)
        main_specs = [out_specs] if single else list(out_specs)
        n_in, n_out, n_scr = len(in_specs), len(main_out), len(scratch)
        x_in, x_out = len(carry.ins), len(carry.out_shapes)
        inner, steps = body, tuple(grid)

        def body(*refs):
            ins, xin = refs[:n_in], refs[n_in:n_in + x_in]
            outs = refs[n_in + x_in:n_in + x_in + n_out]
            xout = refs[n_in + x_in + n_out:n_in + x_in + n_out + x_out]
            scr = refs[n_in + x_in + n_out + x_out:n_in + x_in + n_out + x_out + n_scr]
            xsem = refs[n_in + x_in + n_out + x_out + n_scr:]
            ids = [pl.program_id(a) for a in range(len(steps))]
            first, last = ids[0] == 0, ids[0] == steps[0] - 1
            for a in range(1, len(steps)):
                first = jnp.logical_and(first, ids[a] == 0)
                last = jnp.logical_and(last, ids[a] == steps[a] - 1)
            pl.when(first)(lambda: carry.start(xin, xout, xsem))
            inner(*ins, *outs, *scr)
            pl.when(last)(lambda: carry.wait(xin, xout, xsem))

        hbm = pl.BlockSpec(memory_space=pltpu.HBM)
        aliases = dict(aliases or {})
        aliases.update({n_in + k: n_out + v for k, v in carry.aliases.items()})
        res = _call(body, name, main_out + carry.out_shapes, grid=grid, in_specs=list(in_specs) + [hbm] * x_in,
                    out_specs=main_specs + [hbm] * x_out, scratch=list(scratch) + carry.sems, dims=dims, aliases=aliases)

        def run(*args):
            outs = res(*args, *carry.ins)
            carry.outs = list(outs[n_out:])
            return outs[0] if single else outs[:n_out]

        return run
    if aliases:
        kw["input_output_aliases"] = aliases
    if grid_spec is not None:
        kw["grid_spec"] = grid_spec
    else:
        if grid is not None:
            kw["grid"] = grid
        kw["in_specs"] = in_specs
        kw["out_specs"] = out_specs
        kw["scratch_shapes"] = list(scratch)
    return pl.pallas_call(body, name=name, out_shape=out_shape, compiler_params=pltpu.CompilerParams(**params), **kw)


def _pick(n, target, mult):
    best = None
    for d in range(mult, min(n, target) + 1, mult):
        if n % d == 0:
            best = d
    return n if best is None else best


_DOT_DIMS = {"nn": (((1,), (0,)), ((), ())), "nt": (((1,), (1,)), ((), ())), "tn": (((0,), (0,)), ((), ()))}


def _mm(a, b, mode, out_dtype, name, res=None, carry=None):
    if mode == "tn":
        a, mode = a.T, "nn"
    if mode == "nn":
        (M, K), N = a.shape, b.shape[1]
    else:
        (M, K), N = a.shape, b.shape[0]
    tm = _pick(M, 1024, 8)
    tn = _pick(N, 512, 128)
    tk = _pick(K, 1024, 128)
    nk = K // tk
    has_res = res is not None

    def body(*refs):
        if has_res:
            a_ref, b_ref, r_ref, o_ref, acc = refs
        else:
            a_ref, b_ref, o_ref, acc = refs
        k = pl.program_id(2)

        @pl.when(k == 0)
        def _():
            acc[...] = jnp.zeros_like(acc)

        acc[...] += lax.dot_general(a_ref[...].astype(BF16), b_ref[...].astype(BF16), _DOT_DIMS[mode],
                                    preferred_element_type=F32)

        @pl.when(k == nk - 1)
        def _():
            r = acc[...]
            if has_res:
                r = r + r_ref[...]
            o_ref[...] = r.astype(o_ref.dtype)

    a_spec = pl.BlockSpec((tm, tk), lambda i, j, k: (i, k))
    b_spec = {"nn": pl.BlockSpec((tk, tn), lambda i, j, k: (k, j)),
              "nt": pl.BlockSpec((tn, tk), lambda i, j, k: (j, k))}[mode]
    o_spec = pl.BlockSpec((tm, tn), lambda i, j, k: (i, j))
    in_specs = [a_spec, b_spec] + ([o_spec] if has_res else [])
    args = (a, b) + ((res,) if has_res else ())
    return _call(body, name, jax.ShapeDtypeStruct((M, N), out_dtype), grid=(M // tm, N // tn, nk),
                 in_specs=in_specs, out_specs=o_spec, scratch=[pltpu.VMEM((tm, tn), F32)],
                 dims=("parallel", "parallel", "arbitrary"), carry=carry)(*args)


def _views(rows):
    return [r if isinstance(r, tuple) else (r, r.shape[1], 0) for r in rows]


def _rowwise(f, rows, params, outs, name, tb=256, carry=None, into=None):
    rows = _views(rows)
    T = rows[0][0].shape[0]
    tb = min(tb, T)
    nr, npar = len(rows), len(params)
    outs = [o if len(o) == 3 else (o[0], o[1], o[0]) for o in outs]
    into = into or []

    def body(*refs):
        vals = f(*[r[...].astype(F32) for r in refs[:nr]], *[p[...] for p in refs[nr:nr + npar]])
        for o_ref, v in zip(refs[nr + npar + len(into):], vals):
            o_ref[...] = v.astype(o_ref.dtype)

    row_spec = lambda w, cb=0: pl.BlockSpec((tb, w), lambda i: (i, cb))
    par_spec = lambda w: pl.BlockSpec((1, w), lambda i: (0, 0))
    out_shape = [jax.ShapeDtypeStruct((T, tw), dt) for _, dt, tw in outs]
    out_specs = [row_spec(w) for w, _, _ in outs]
    in_specs = [row_spec(w, cb) for _, w, cb in rows] + [par_spec(p.shape[1]) for p in params]
    args = [r[0] for r in rows] + list(params)
    aliases = {}
    for k, arr, cb in into:
        aliases[len(args)] = k
        in_specs.append(pl.BlockSpec(memory_space=pl.ANY))
        args.append(arr)
        out_shape[k] = jax.ShapeDtypeStruct(arr.shape, arr.dtype)
        out_specs[k] = row_spec(outs[k][0], cb)
    return _call(body, name, out_shape, grid=(T // tb,), in_specs=in_specs, out_specs=out_specs, dims=("parallel",),
                 carry=carry, aliases=aliases)(*args)


def _rowwise_bwd(f, rows, params, douts, n_diff, name, tb=256, carry=None, add=None, into=None):
    rows, douts = _views(rows), _views(douts)
    T = rows[0][0].shape[0]
    tb = min(tb, T)
    nr, npar, nd = len(rows), len(params), len(douts)
    n_add = 0 if add is None else 1

    def body(*refs):
        rv = [r[...].astype(F32) for r in refs[:nr]]
        pv = [p[...] for p in refs[nr:nr + npar]]
        dv = [d[...].astype(F32) for d in refs[nr + npar:nr + npar + nd]]
        o_refs = refs[nr + npar + nd + n_add + (0 if into is None else 1):]
        fixed = rv[n_diff:]

        def g(*xs):
            return tuple(f(*xs[:n_diff], *fixed, *xs[n_diff:]))

        _, vjp = jax.vjp(g, *rv[:n_diff], *pv)
        grads = list(vjp(tuple(dv)))
        if add is not None:
            grads[0] = grads[0] + refs[nr + npar + nd][...]
        for o_ref, gr in zip(o_refs[:n_diff], grads[:n_diff]):
            o_ref[...] = gr.astype(o_ref.dtype)
        first = pl.program_id(0) == 0
        for o_ref, gr in zip(o_refs[n_diff:], grads[n_diff:]):
            @pl.when(first)
            def _(o_ref=o_ref):
                o_ref[...] = jnp.zeros_like(o_ref)

            o_ref[...] += gr

    row_spec = lambda w, cb=0: pl.BlockSpec((tb, w), lambda i: (i, cb))
    par_spec = lambda w: pl.BlockSpec((1, w), lambda i: (0, 0))
    out_shape = ([jax.ShapeDtypeStruct((T, w), F32) for _, w, _ in rows[:n_diff]]
                 + [jax.ShapeDtypeStruct((1, p.shape[1]), F32) for p in params])
    out_specs = [row_spec(w) for _, w, _ in rows[:n_diff]] + [par_spec(p.shape[1]) for p in params]
    in_specs = ([row_spec(w, cb) for _, w, cb in rows] + [par_spec(p.shape[1]) for p in params]
                + [row_spec(w, cb) for _, w, cb in douts])
    args = [r[0] for r in rows] + list(params) + [d[0] for d in douts]
    aliases = None
    if add is not None:
        in_specs.append(row_spec(add.shape[1]))
        args.append(add)
    if into is not None:
        aliases = {len(args): 0}
        in_specs.append(pl.BlockSpec(memory_space=pl.ANY))
        args.append(into[0])
        out_shape[0] = jax.ShapeDtypeStruct(into[0].shape, into[0].dtype)
        out_specs[0] = row_spec(rows[0][1], into[1])
    return _call(body, name, out_shape, grid=(T // tb,), in_specs=in_specs, out_specs=out_specs,
                 dims=("arbitrary",), carry=carry, aliases=aliases)(*args)


def _sig(x):
    return 1.0 / (1.0 + jnp.exp(-x))


def _rms(x, g):
    return x * lax.rsqrt(jnp.mean(x * x, axis=-1, keepdims=True) + RMS_EPS) * g


def _f_rms(x, g):
    return (_rms(x, g),)


def _f_glu(a, gate):
    return (a * _sig(gate),)


def _f_ln_silu(x, g, b):
    mu = jnp.mean(x, axis=-1, keepdims=True)
    xc = x - mu
    var = jnp.mean(xc * xc, axis=-1, keepdims=True)
    y = xc * lax.rsqrt(var + LN_EPS) * g + b
    return (y * _sig(y),)


def _rope128(x, cos_p, sin_p):
    return x * cos_p + pltpu.roll(x, 64, 1) * sin_p


def _rope128_t(d, cos_p, sin_p):
    return d * cos_p + pltpu.roll(d * sin_p, 64, 1)


def _f_rope(xq, xk, cos_p, sin_p):
    heads = [_rope128(xq[:, h * 128:(h + 1) * 128], cos_p, sin_p) for h in range(MLA_HEADS)]
    return (jnp.concatenate(heads, axis=1), _rope128(xk, cos_p, sin_p))


def _f_rope_t(dq, dk_heads, cos_p, sin_p):
    heads = [_rope128_t(dq[:, h * 128:(h + 1) * 128], cos_p, sin_p) for h in range(MLA_HEADS)]
    dk = dk_heads[:, 0:128]
    for h in range(1, MLA_HEADS):
        dk = dk + dk_heads[:, h * 128:(h + 1) * 128]
    return (jnp.concatenate(heads, axis=1), _rope128_t(dk, cos_p, sin_p))


GATE_LANES = 256


def _gate_fwd(ycat, proj, z_col, name, tb=1024):
    T, width = ycat.shape
    zb = z_col // GATE_LANES

    def body(y_ref, z_ref, o_ref):
        z = z_ref[...]
        o_ref[...] = (y_ref[...] * (z * _sig(z))).astype(o_ref.dtype)

    blk = pl.BlockSpec((tb, GATE_LANES), lambda i, c: (i, c))
    return _call(body, name, jax.ShapeDtypeStruct((T, width), BF16), grid=(T // tb, width // GATE_LANES),
                 in_specs=[blk, pl.BlockSpec((tb, GATE_LANES), lambda i, c: (i, zb + c))], out_specs=blk,
                 dims=("parallel", "parallel"))(ycat, proj)


def _gate_bwd(ycat, proj, z_col, dy, name, tb=1024, carry=None):
    T, width = ycat.shape
    zb = z_col // GATE_LANES

    def body(y_ref, z_ref, dy_ref, dycat_ref, dz_ref):
        z, d = z_ref[...], dy_ref[...]
        s = _sig(z)
        dycat_ref[...] = d * (z * s)
        dz_ref[...] = (d * y_ref[...] * (s * (1.0 + z * (1.0 - s)))).astype(dz_ref.dtype)

    blk = pl.BlockSpec((tb, GATE_LANES), lambda i, c: (i, c))
    zblk = pl.BlockSpec((tb, GATE_LANES), lambda i, c: (i, zb + c))
    return _call(body, name, [jax.ShapeDtypeStruct((T, width), F32), jax.ShapeDtypeStruct(proj.shape, BF16)],
                 grid=(T // tb, width // GATE_LANES), in_specs=[blk, zblk, blk], out_specs=[blk, zblk],
                 dims=("parallel", "parallel"), carry=carry)(ycat, proj, dy)


def _glu_bwd(proj, d_glu, d_proj, name, tb=256):
    T, w = d_glu.shape

    def body(a_ref, g_ref, d_ref, _, o_ref):
        s = _sig(g_ref[...])

        @pl.when(pl.program_id(1) == 0)
        def _():
            o_ref[...] = (d_ref[...] * s).astype(o_ref.dtype)

        @pl.when(pl.program_id(1) == 1)
        def _():
            o_ref[...] = (d_ref[...] * a_ref[...] * (s * (1.0 - s))).astype(o_ref.dtype)

    return _call(body, name, jax.ShapeDtypeStruct(d_proj.shape, d_proj.dtype), grid=(T // tb, 2),
                 in_specs=[pl.BlockSpec((tb, w), lambda i, c: (i, 0)), pl.BlockSpec((tb, w), lambda i, c: (i, 1)),
                           pl.BlockSpec((tb, w), lambda i, c: (i, 0)), pl.BlockSpec(memory_space=pl.ANY)],
                 out_specs=pl.BlockSpec((tb, w), lambda i, c: (i, c)), dims=("parallel", "arbitrary"),
                 aliases={3: 0})(proj, proj, d_glu, d_proj)


def _final_loss(h, tgt, g, name, tb=256):
    T, D = h.shape

    def body(h_ref, t_ref, g_ref, dh_ref, dg_ref, loss_ref):
        tv = t_ref[...]

        def rowloss(hh, gg):
            e = _rms(hh, gg) - tv
            return 0.5 * jnp.mean(e * e, axis=-1, keepdims=True)

        lr, vjp = jax.vjp(rowloss, h_ref[...], g_ref[...])
        dh, dg = vjp(jnp.ones_like(lr))
        dh_ref[...] = dh

        @pl.when(pl.program_id(0) == 0)
        def _():
            dg_ref[...] = jnp.zeros_like(dg_ref)
            loss_ref[...] = jnp.zeros_like(loss_ref)

        dg_ref[...] += dg
        loss_ref[...] += jnp.broadcast_to(jnp.sum(lr, axis=0, keepdims=True), loss_ref.shape)

    row = pl.BlockSpec((tb, D), lambda i: (i, 0))
    par = pl.BlockSpec((1, D), lambda i: (0, 0))
    return _call(body, name,
                 [jax.ShapeDtypeStruct((T, D), F32), jax.ShapeDtypeStruct((1, D), F32), jax.ShapeDtypeStruct((1, 128), F32)],
                 grid=(T // tb,), in_specs=[row, row, par],
                 out_specs=[row, par, pl.BlockSpec((1, 128), lambda i: (0, 0))], dims=("arbitrary",))(h, tgt, g)


CONV_ROWS = 128
CONV_LANES = 256


def _sublane_phases(pad, n):
    for r in range(1, 8):
        for c0 in range(0, n - 8, 256):
            rows = min(256, n - 8 - c0)
            pad[r, c0:c0 + rows, :] = pad[0, c0 + r:c0 + r + rows, :]


def _dwconv_fwd(x, w, b, name, carry=None):
    B, S, C = x.shape
    cb = CONV_LANES
    off = CONV_PAD - (CONV_KERNEL - 1)

    def body(x_ref, w_ref, b_ref, o_ref, pad):
        pad[0, 0:CONV_PAD, :] = jnp.zeros((CONV_PAD, cb), F32)
        pad[0, CONV_PAD:, :] = x_ref[...]
        _sublane_phases(pad, S + CONV_PAD)
        for t0 in range(0, S, CONV_ROWS):
            acc = jnp.broadcast_to(b_ref[...], (CONV_ROWS, cb))
            for k in range(CONV_KERNEL):
                r, base = (off + k) % 8, t0 + (off + k) // 8 * 8
                acc = acc + w_ref[k:k + 1, :] * pad[r, base:base + CONV_ROWS, :]
            o_ref[t0:t0 + CONV_ROWS, :] = acc

    return _call(body, name, jax.ShapeDtypeStruct((B, S, C), F32), grid=(B, C // cb),
                 in_specs=[pl.BlockSpec((None, S, cb), lambda i, j: (i, 0, j)),
                           pl.BlockSpec((CONV_KERNEL, cb), lambda i, j: (0, j)),
                           pl.BlockSpec((1, cb), lambda i, j: (0, j))],
                 out_specs=pl.BlockSpec((None, S, cb), lambda i, j: (i, 0, j)),
                 scratch=[pltpu.VMEM((8, S + CONV_PAD, cb), F32)], dims=("parallel", "parallel"), carry=carry)(x, w, b)


def _dwconv_bwd(x, w, dy, name, carry=None):
    B, S, C = x.shape
    cb = CONV_LANES
    off = CONV_PAD - (CONV_KERNEL - 1)
    groups = CONV_ROWS // 8

    def body(x_ref, w_ref, dy_ref, dx_ref, dw_ref, db_ref, dypad, wacc):
        dypad[0, 0:S, :] = dy_ref[...]
        dypad[0, S:, :] = jnp.zeros((CONV_PAD, cb), F32)
        _sublane_phases(dypad, S + CONV_PAD)
        wacc[...] = jnp.zeros_like(wacc)
        for t0 in range(0, S, CONV_ROWS):
            xc = x_ref[t0:t0 + CONV_ROWS, :]
            acc = jnp.zeros((CONV_ROWS, cb), F32)
            for k in range(CONV_KERNEL):
                o = (CONV_KERNEL - 1) - k
                dys = dypad[o % 8, t0 + o // 8 * 8:t0 + o // 8 * 8 + CONV_ROWS, :]
                acc = acc + w_ref[k:k + 1, :] * dys
                wacc[k] += jnp.sum((dys * xc).reshape(groups, 8, cb), axis=0)
            wacc[CONV_KERNEL] += jnp.sum(dy_ref[t0:t0 + CONV_ROWS, :].reshape(groups, 8, cb), axis=0)
            dx_ref[t0:t0 + CONV_ROWS, :] = acc

        @pl.when(pl.program_id(1) == 0)
        def _():
            dw_ref[...] = jnp.zeros_like(dw_ref)
            db_ref[...] = jnp.zeros_like(db_ref)

        for k in range(CONV_KERNEL):
            dw_ref[k:k + 1, :] += jnp.sum(wacc[k], axis=0, keepdims=True)
        db_ref[...] += jnp.sum(wacc[CONV_KERNEL], axis=0, keepdims=True)

    blk = pl.BlockSpec((None, S, cb), lambda j, i: (i, 0, j))
    return _call(body, name,
                 [jax.ShapeDtypeStruct((B, S, C), F32), jax.ShapeDtypeStruct((CONV_KERNEL, C), F32),
                  jax.ShapeDtypeStruct((1, C), F32)],
                 grid=(C // cb, B),
                 in_specs=[blk, pl.BlockSpec((CONV_KERNEL, cb), lambda j, i: (0, j)), blk],
                 out_specs=[blk, pl.BlockSpec((CONV_KERNEL, cb), lambda j, i: (0, j)),
                            pl.BlockSpec((1, cb), lambda j, i: (0, j))],
                 scratch=[pltpu.VMEM((8, S + CONV_PAD, cb), F32), pltpu.VMEM((CONV_KERNEL + 1, 8, cb), F32)],
                 dims=("parallel", "arbitrary"), carry=carry)(x, w, dy)


ATTN_TILE = 512
ATTN_SUB = {"fwd": 256, "bwd": 512}


def _attn_shapes(Sq, Sk, causal, pass_):
    tq = min(Sq, ATTN_TILE)
    tk = tq if causal else min(Sk, ATTN_TILE)
    return tq, tk, min(ATTN_SUB[pass_], tq)


def _mask(row0, col0, rows, cols):
    r = row0 + lax.broadcasted_iota(jnp.int32, (rows, cols), 0)
    c = col0 + lax.broadcasted_iota(jnp.int32, (rows, cols), 1)
    return c <= r


def _attn_fwd(q, q_c0, qr, k, k_c0, kr, v, v_c0, B, Sq, Sk, H, causal, scale, name, into=None, o_c0=0, o_width=None,
              kv_stride=1):
    tq, tk, sub = _attn_shapes(Sq, Sk, causal, "fwd")
    nq, nk, nsub = Sq // tq, Sk // tk, tq // sub
    rope = qr is not None

    def body(*refs):
        refs = list(refs)
        qn_ref = refs.pop(0)
        qr_ref = refs.pop(0) if rope else None
        kn_ref = refs.pop(0)
        kr_ref = refs.pop(0) if rope else None
        v_ref = refs.pop(0)
        if into is not None:
            refs.pop(0)
        o_ref, lse_ref, m_s, l_s, acc = refs
        qi = pl.program_id(2)
        m_s[...] = jnp.full_like(m_s, NEG)
        l_s[...] = jnp.zeros_like(l_s)
        acc[...] = jnp.zeros_like(acc)
        qs = []
        for r in range(nsub):
            qn = qn_ref[r * sub:(r + 1) * sub, :].astype(BF16)
            qs.append(jnp.concatenate([qn, qr_ref[r * sub:(r + 1) * sub, :]], axis=1) if rope else qn)

        def step(j, masked):
            ks = pl.ds(pl.multiple_of(j * tk, tk), tk)
            kk = jnp.concatenate([kn_ref[ks, :], kr_ref[ks, :]], axis=1) if rope else kn_ref[ks, :]
            vv = v_ref[ks, :]
            for r in range(nsub):
                rows = slice(r * sub, (r + 1) * sub)
                s = lax.dot_general(qs[r], kk, _DOT_DIMS["nt"], preferred_element_type=F32) * scale
                if masked:
                    s = jnp.where(_mask(qi * tq + r * sub, j * tk, sub, tk), s, NEG)
                m_old = m_s[rows, :]
                m_new = jnp.maximum(m_old, jnp.max(s, axis=-1, keepdims=True))
                p = jnp.exp(s - m_new)
                alpha = jnp.exp(m_old - m_new)
                l_s[rows, :] = alpha * l_s[rows, :] + jnp.sum(p, axis=-1, keepdims=True)
                acc[rows, :] = alpha * acc[rows, :] + jnp.dot(p.astype(BF16), vv, preferred_element_type=F32)
                m_s[rows, :] = m_new

        def unmasked(j, carry):
            step(j, False)
            return carry

        if causal:
            lax.fori_loop(0, qi, unmasked, 0)
            step(qi, True)
        else:
            lax.fori_loop(0, nk, unmasked, 0)
        o_ref[...] = (acc[...] / l_s[...]).astype(o_ref.dtype)
        lse_ref[...] = m_s[...] + jnp.log(l_s[...])

    qspec = lambda c0: pl.BlockSpec((tq, 128), lambda b, h, i: (b * nq + i, c0 + h))
    kspec = lambda c0: pl.BlockSpec((Sk, 128), lambda b, h, i: (b, c0 + kv_stride * h))
    in_specs, args = [qspec(q_c0)], [q]
    if rope:
        in_specs.append(qspec(0)); args.append(qr)
    in_specs.append(kspec(k_c0)); args.append(k)
    if rope:
        in_specs.append(pl.BlockSpec((Sk, 128), lambda b, h, i: (b, 0))); args.append(kr)
    in_specs.append(kspec(v_c0)); args.append(v)
    aliases = {}
    if into is not None:
        aliases = {len(args): 0}
        in_specs.append(pl.BlockSpec(memory_space=pl.ANY)); args.append(into)
        o_shape = jax.ShapeDtypeStruct(into.shape, into.dtype)
    else:
        o_shape = jax.ShapeDtypeStruct((B * Sq, o_width), F32)
    return _call(body, name, [o_shape, jax.ShapeDtypeStruct((B * H, Sq, 1), F32)], grid=(B, H, nq), in_specs=in_specs,
                 out_specs=[qspec(o_c0), pl.BlockSpec((None, tq, 1), lambda b, h, i: (b * H + h, i, 0))],
                 scratch=[pltpu.VMEM((tq, 1), F32), pltpu.VMEM((tq, 1), F32), pltpu.VMEM((tq, 128), F32)],
                 dims=("parallel", "parallel", "arbitrary"), aliases=aliases)(*args)


def _attn_bwd(q, q_c0, qr, k, k_c0, kr, v, v_c0, o, do, o_c0, lse, B, Sq, Sk, H, causal, scale, name, dq_into=None,
              kv_stride=1):
    tq, tk, sub = _attn_shapes(Sq, Sk, causal, "bwd")
    nq, nk, nsub = Sq // tq, Sk // tk, tq // sub
    rope = qr is not None
    dk_w = 256 if rope else 128

    def body(*refs):
        refs = list(refs)
        qn_ref = refs.pop(0)
        qr_ref = refs.pop(0) if rope else None
        kn_ref = refs.pop(0)
        kr_ref = refs.pop(0) if rope else None
        v_ref, o_ref, do_ref, lse_ref = refs[:4]
        refs = refs[4 + (0 if dq_into is None else 1):]
        dqn_ref = refs.pop(0)
        dqr_ref = refs.pop(0) if rope else None
        dkn_ref = refs.pop(0)
        dkr_ref = refs.pop(0) if rope else None
        dv_ref = None if rope else refs.pop(0)
        q_s, do_s, dl_s, dq_acc, dk_acc, dv_acc = refs
        kj = pl.program_id(2)

        @pl.when(kj == 0)
        def _():
            qn = qn_ref[...].astype(BF16)
            q_s[...] = jnp.concatenate([qn, qr_ref[...]], axis=1) if rope else qn
            dof = do_ref[...]
            do_s[...] = dof.astype(BF16)
            dl_s[...] = jnp.sum(dof * o_ref[...], axis=-1, keepdims=True)
            dq_acc[...] = jnp.zeros_like(dq_acc)

        kk = jnp.concatenate([kn_ref[...], kr_ref[...]], axis=1) if rope else kn_ref[...]
        vv = v_ref[...]
        dk_acc[...] = jnp.zeros_like(dk_acc)
        dv_acc[...] = jnp.zeros_like(dv_acc)

        def step(i, masked):
            for r in range(nsub):
                rows = pl.ds(pl.multiple_of(i * tq + r * sub, sub), sub)
                qq, dob = q_s[rows, :], do_s[rows, :]
                s = lax.dot_general(qq, kk, _DOT_DIMS["nt"], preferred_element_type=F32) * scale
                if masked:
                    s = jnp.where(_mask(i * tq + r * sub, kj * tk, sub, tk), s, NEG)
                p = jnp.exp(s - lse_ref[rows, :])
                dp = lax.dot_general(dob, vv, _DOT_DIMS["nt"], preferred_element_type=F32)
                ds = (p * (dp - dl_s[rows, :]) * scale).astype(BF16)
                dv_acc[...] += lax.dot_general(p.astype(BF16), dob, _DOT_DIMS["tn"], preferred_element_type=F32)
                dk_acc[...] += lax.dot_general(ds, qq, _DOT_DIMS["tn"], preferred_element_type=F32)
                dq_acc[rows, :] += jnp.dot(ds, kk, preferred_element_type=F32)

        def unmasked(i, carry):
            step(i, False)
            return carry

        if causal:
            step(kj, True)
            lax.fori_loop(kj + 1, nq, unmasked, 0)
        else:
            lax.fori_loop(0, nq, unmasked, 0)
        if rope:
            dkn_ref[...] = jnp.concatenate([dk_acc[:, 0:128], dv_acc[...]], axis=1).astype(dkn_ref.dtype)
            dkr_ref[...] = dk_acc[:, 128:256]
        else:
            dkn_ref[...] = dk_acc[...]
            dv_ref[...] = dv_acc[...]

        @pl.when(kj == nk - 1)
        def _():
            dqn_ref[...] = dq_acc[:, 0:128].astype(dqn_ref.dtype)
            if rope:
                dqr_ref[...] = dq_acc[:, 128:256]

    qspec = lambda c0: pl.BlockSpec((Sq, 128), lambda b, h, j: (b, c0 + h))
    kspec = lambda c0: pl.BlockSpec((tk, 128), lambda b, h, j: (b * nk + j, c0 + kv_stride * h))
    in_specs, args = [qspec(q_c0)], [q]
    if rope:
        in_specs.append(qspec(0)); args.append(qr)
    in_specs.append(kspec(k_c0)); args.append(k)
    if rope:
        in_specs.append(pl.BlockSpec((tk, 128), lambda b, h, j: (b * nk + j, 0))); args.append(kr)
    in_specs += [kspec(v_c0), qspec(o_c0), qspec(o_c0), pl.BlockSpec((None, Sq, 1), lambda b, h, j: (b * H + h, 0, 0))]
    args += [v, o, do, lse]
    h_rows_q = jax.ShapeDtypeStruct((B * Sq, H * 128), F32)
    h_rows_k = jax.ShapeDtypeStruct((B * Sk, H * 128), F32)
    out_shape, out_specs, aliases = [h_rows_q], [qspec(0)], None
    if rope:
        out_shape = [jax.ShapeDtypeStruct((B * Sq, 2 * H * 128), BF16)]
    if dq_into is not None:
        aliases = {len(args): 0}
        in_specs.append(pl.BlockSpec(memory_space=pl.ANY)); args.append(dq_into[0])
        out_shape, out_specs = [jax.ShapeDtypeStruct(dq_into[0].shape, dq_into[0].dtype)], [qspec(dq_into[1])]
    if rope:
        out_shape.append(h_rows_q); out_specs.append(qspec(0))
    hspec = lambda w: pl.BlockSpec((tk, w), lambda b, h, j: (b * nk + j, h))
    if rope:
        out_shape += [jax.ShapeDtypeStruct((B * Sk, H * 256), BF16), h_rows_k]
        out_specs += [hspec(256), hspec(128)]
    else:
        out_shape += [h_rows_k, h_rows_k]
        out_specs += [hspec(128), hspec(128)]
    return _call(body, name, out_shape, grid=(B, H, nk), in_specs=in_specs, out_specs=out_specs,
                 scratch=[pltpu.VMEM((Sq, dk_w), BF16), pltpu.VMEM((Sq, 128), BF16), pltpu.VMEM((Sq, 1), F32),
                          pltpu.VMEM((Sq, dk_w), F32), pltpu.VMEM((tk, dk_w), F32), pltpu.VMEM((tk, 128), F32)],
                 dims=("parallel", "parallel", "arbitrary"), aliases=aliases)(*args)


def _mem_attention_fwd(proj, q_col, ycat, mem2, mem_g, w_mem, B, S, tag):
    M = mem2.shape[0] // B
    (memn,) = _rowwise(_f_rms, [mem2], [mem_g], [(mem2.shape[1], BF16)], tag + "_memnorm")
    kvm = _mm(memn, w_mem, "nn", BF16, tag + "_memkv")
    o_c0 = ycat.shape[1] // 128 - MEM_HEADS
    ycat, lse = _attn_fwd(proj, q_col // 128, None, kvm, 0, None, kvm, MEM_HEADS, B, S, M, MEM_HEADS, False,
                          MEM_HEAD_DIM ** -0.5, tag + "_memattn", into=ycat, o_c0=o_c0)
    return ycat, (memn, kvm, lse)


def _mem_attention_bwd(proj, q_col, ycat, d_ycat, d_proj, saved, mem2, mem_g, w_mem, B, S, tag):
    memn, kvm, lse = saved
    M = mem2.shape[0] // B
    o_c0 = ycat.shape[1] // 128 - MEM_HEADS
    d_q, d_k, d_v = _attn_bwd(proj, q_col // 128, None, kvm, 0, None, kvm, MEM_HEADS, ycat, d_ycat, o_c0, lse, B, S, M,
                              MEM_HEADS, False, MEM_HEAD_DIM ** -0.5, tag + "_memattn_bwd", dq_into=(d_proj, q_col // 128))
    d_kvm = jnp.concatenate([d_k, d_v], axis=1).astype(BF16)
    d_w_mem = _mm(memn, d_kvm, "tn", F32, tag + "_memkv_dw")
    d_memn = _mm(d_kvm, w_mem, "nt", F32, tag + "_memkv_dx")
    _, d_mem_g = _rowwise_bwd(_f_rms, [mem2], [mem_g], [d_memn], 1, tag + "_memnorm_bwd")
    return d_q, d_w_mem, d_mem_g


def _rope_tables(positions):
    inv_freq = 1.0 / (ROPE_THETA ** (jnp.arange(0, MLA_ROPE, 2, dtype=F32) / MLA_ROPE))
    ang = positions.astype(F32).reshape(-1, 1) * inv_freq
    cos, sin, zero = jnp.cos(ang), jnp.sin(ang), jnp.zeros_like(ang)
    return jnp.concatenate([cos, zero, cos, zero], axis=1), jnp.concatenate([-sin, zero, sin, zero], axis=1)


def _forward_backward(x, mem, positions, target, W):
    B, S, D = x.shape
    T = B * S
    conv_w = W["conv_dw"].shape[1]
    mix_w = 2 * D
    h0 = x.reshape(T, D)
    mem2 = mem.reshape(-1, D)
    tgt = target.reshape(T, D)
    row = lambda v: v.reshape(1, -1)
    n_nope = MLA_HEADS * MLA_NOPE

    g0 = row(W["norm_g"][0])
    (u0,) = _rowwise(_f_rms, [h0], [g0], [(D, BF16)], "l0_norm", carry=W.carry("l0_norm"))
    proj0 = _mm(u0, W["conv_w_in"], "nn", F32, "l0_in", carry=W.carry("l0_in"))
    a0, gate0 = (proj0, conv_w, 0), (proj0, conv_w, 1)
    qm0_col, z0_col = 2 * conv_w, 2 * conv_w + MEM_WIDTH
    (glu,) = _rowwise(_f_glu, [a0, gate0], [], [(conv_w, F32)], "l0_glu", carry=W.carry("l0_glu"))
    dw, dwb = W["conv_dw"], row(W["conv_dw_b"][0])
    cv = _dwconv_fwd(glu.reshape(B, S, conv_w), dw, dwb, "l0_dwconv", carry=W.carry("l0_dwconv")).reshape(T, conv_w)
    ln_g, ln_b = row(W["conv_ln_g"][0]), row(W["conv_ln_b"][0])
    (ycat0,) = _rowwise(_f_ln_silu, [cv], [ln_g, ln_b], [(conv_w, F32, mix_w)], "l0_ln", carry=W.carry("l0_ln"))
    mg0 = row(W["mem_norm_g"][0])
    ycat0, mem_saved0 = _mem_attention_fwd(proj0, qm0_col, ycat0, mem2, mg0, W["w_mem_kv"][0], B, S, "l0")
    y0 = _gate_fwd(ycat0, proj0, z0_col, "l0_gate")
    h1 = _mm(y0, W["w_out"][0], "nn", F32, "l0_out", res=h0)

    g1 = row(W["norm_g"][1])
    (u1,) = _rowwise(_f_rms, [h1], [g1], [(D, BF16)], "l1_norm")
    proj1 = _mm(u1, W["mla_w_in"], "nn", F32, "l1_in")
    cq, ckv = (proj1, Q_RANK, 0), (proj1, KV_RANK, Q_RANK // KV_RANK)
    qm1_col = Q_RANK + KV_RANK
    z1_col = qm1_col + MEM_WIDTH
    kr_col = z1_col + mix_w
    qg, kvg = row(W["mla_q_norm_g"]), row(W["mla_kv_norm_g"])
    (cqn,) = _rowwise(_f_rms, [cq], [qg], [(Q_RANK, BF16)], "l1_qnorm")
    (ckvn,) = _rowwise(_f_rms, [ckv], [kvg], [(KV_RANK, BF16)], "l1_kvnorm")
    qf = _mm(cqn, W["mla_w_uq"], "nn", F32, "l1_uq")
    kvf = _mm(ckvn, W["mla_w_ukv"], "nn", BF16, "l1_ukv")
    cos_p, sin_p = _rope_tables(positions)
    qr, kr = _rowwise(_f_rope, [(qf, n_nope, 1), (proj1, 128, kr_col // 128), cos_p, sin_p], [],
                      [(n_nope, BF16), (128, BF16)], "l1_rope")
    scale1 = MLA_QK ** -0.5
    ycat1, lse1 = _attn_fwd(qf, 0, qr, kvf, 0, kr, kvf, 1, B, S, S, MLA_HEADS, True, scale1, "l1_attn",
                            o_width=mix_w, kv_stride=2)
    mg1 = row(W["mem_norm_g"][1])
    ycat1, mem_saved1 = _mem_attention_fwd(proj1, qm1_col, ycat1, mem2, mg1, W["w_mem_kv"][1], B, S, "l1")
    y1 = _gate_fwd(ycat1, proj1, z1_col, "l1_gate")
    h2 = _mm(y1, W["w_out"][1], "nn", F32, "l1_out", res=h1)

    gf = row(W["final_norm_g"])
    dh2, d_gf, loss128 = _final_loss(h2, tgt, gf, "final_loss")
    G = {"final_norm_g": d_gf.reshape(-1)}
    L1 = {}

    dy1 = _mm(dh2, W["w_out"][1], "nt", F32, "l1_out_dx")
    d_wout1 = _mm(y1, dh2, "tn", F32, "l1_out_dw")
    d_ycat1, d_proj1 = _gate_bwd(ycat1, proj1, z1_col, dy1, "l1_gate_bwd")
    d_proj1, d_wmem1, d_mg1 = _mem_attention_bwd(proj1, qm1_col, ycat1, d_ycat1, d_proj1, mem_saved1, mem2, mg1,
                                                 W["w_mem_kv"][1], B, S, "l1")
    d_qf, d_qr, d_kvf, d_kr_heads = _attn_bwd(qf, 0, qr, kvf, 0, kr, kvf, 1, ycat1, d_ycat1, 0, lse1, B, S, S,
                                              MLA_HEADS, True, scale1, "l1_attn_bwd", kv_stride=2)
    d_qf, d_proj1 = _rowwise(_f_rope_t, [d_qr, d_kr_heads, cos_p, sin_p], [], [(n_nope, F32), (128, F32)], "l1_rope_bwd",
                             into=[(0, d_qf, 1), (1, d_proj1, kr_col // 128)])
    d_cqn = _mm(d_qf, W["mla_w_uq"], "nt", F32, "l1_uq_dx")
    L1[("mla_w_uq", None)] = _mm(cqn, d_qf, "tn", F32, "l1_uq_dw")
    d_ckvn = _mm(d_kvf, W["mla_w_ukv"], "nt", F32, "l1_ukv_dx")
    L1[("mla_w_ukv", None)] = _mm(ckvn, d_kvf, "tn", F32, "l1_ukv_dw")
    d_proj1, d_qg = _rowwise_bwd(_f_rms, [cq], [qg], [d_cqn], 1, "l1_qnorm_bwd", into=(d_proj1, cq[2]))
    d_proj1, d_kvg = _rowwise_bwd(_f_rms, [ckv], [kvg], [d_ckvn], 1, "l1_kvnorm_bwd", into=(d_proj1, ckv[2]))
    L1[("w_mem_kv", 1)] = d_wmem1
    L1[("mla_w_in", None)] = _mm(u1, d_proj1, "tn", F32, "l1_in_dw")
    L1[("w_out", 1)] = d_wout1
    W.ready("l1", L1)
    d_u1 = _mm(d_proj1, W["mla_w_in"], "nt", F32, "l1_in_dx", carry=W.carry("l1_in_dx"))
    dh1, d_g1 = _rowwise_bwd(_f_rms, [h1], [g1], [d_u1], 1, "l1_norm_bwd", add=dh2)

    dy0 = _mm(dh1, W["w_out"][0], "nt", F32, "l0_out_dx")
    d_wout0 = _mm(y0, dh1, "tn", F32, "l0_out_dw")
    d_ycat0, d_proj0 = _gate_bwd(ycat0, proj0, z0_col, dy0, "l0_gate_bwd", carry=W.carry("l0_gate_bwd"))
    d_proj0, d_wmem0, d_mg0 = _mem_attention_bwd(proj0, qm0_col, ycat0, d_ycat0, d_proj0, mem_saved0, mem2, mg0,
                                                 W["w_mem_kv"][0], B, S, "l0")
    W.ready("l0a", {("w_mem_kv", 0): d_wmem0, ("w_out", 0): d_wout0})
    d_cv, d_ln_g, d_ln_b = _rowwise_bwd(_f_ln_silu, [cv], [ln_g, ln_b], [(d_ycat0, conv_w, 0)], 1, "l0_ln_bwd",
                                        carry=W.carry("l0_ln_bwd"))
    d_glu, d_dw, d_dwb = _dwconv_bwd(glu.reshape(B, S, conv_w), dw, d_cv.reshape(B, S, conv_w), "l0_dwconv_bwd",
                                     carry=W.carry("l0_dwconv_bwd"))
    d_proj0 = _glu_bwd(proj0, d_glu.reshape(T, conv_w), d_proj0, "l0_glu_bwd")
    d_conv_w_in = _mm(u0, d_proj0, "tn", F32, "l0_in_dw", carry=W.carry("l0_in_dw"))
    W.ready("l0b", {("conv_w_in", None): d_conv_w_in, ("conv_dw", None): d_dw,
                    ("mla_q_norm_g", None): d_qg.reshape(-1), ("mla_kv_norm_g", None): d_kvg.reshape(-1)})
    d_u0 = _mm(d_proj0, W["conv_w_in"], "nt", F32, "l0_in_dx", carry=W.carry("l0_in_dx"))
    dx, d_g0 = _rowwise_bwd(_f_rms, [h0], [g0], [d_u0], 1, "l0_norm_bwd", add=dh1)
    dx = dx.reshape(B, S, D)

    G["norm_g"] = jnp.concatenate([d_g0, d_g1], axis=0)
    G["mem_norm_g"] = jnp.concatenate([d_mg0, d_mg1], axis=0)
    G["conv_dw_b"] = d_dwb
    G["conv_ln_g"], G["conv_ln_b"] = d_ln_g, d_ln_b
    return loss128[0, 0], dx, G


def _mla_in_perm(w):
    c2 = Q_RANK + KV_RANK
    zero = jnp.zeros((w.shape[0], HALF_ROPE), w.dtype)
    return jnp.concatenate([w[:, :c2], w[:, c2 + MLA_ROPE:], w[:, c2:c2 + HALF_ROPE], zero,
                            w[:, c2 + HALF_ROPE:c2 + MLA_ROPE], zero], axis=1)


def _mla_in_unperm(g):
    c2 = Q_RANK + KV_RANK
    r = g.shape[1] - 128
    return jnp.concatenate([g[:, :c2], g[:, r:r + HALF_ROPE], g[:, r + 64:r + 64 + HALF_ROPE], g[:, c2:r]], axis=1)


def _uq_perm(w):
    n = w.shape[0]
    w3 = w.reshape(n, MLA_HEADS, MLA_QK)
    zero = jnp.zeros((n, MLA_HEADS, HALF_ROPE), w.dtype)
    rope = jnp.concatenate([w3[:, :, MLA_NOPE:MLA_NOPE + HALF_ROPE], zero, w3[:, :, MLA_NOPE + HALF_ROPE:], zero], axis=2)
    return jnp.concatenate([w3[:, :, :MLA_NOPE].reshape(n, -1), rope.reshape(n, -1)], axis=1)


def _uq_unperm(g):
    n = g.shape[0]
    n_nope = MLA_HEADS * MLA_NOPE
    rope = g[:, n_nope:].reshape(n, MLA_HEADS, 128)
    return jnp.concatenate([g[:, :n_nope].reshape(n, MLA_HEADS, MLA_NOPE), rope[:, :, :HALF_ROPE],
                            rope[:, :, 64:64 + HALF_ROPE]], axis=2).reshape(n, -1)


_ROW_CUT = ("w_mem_kv", "w_out")
_COL_CUT = ("conv_w_in", "mla_w_in", "mla_w_uq", "mla_w_ukv", "conv_dw")
_BIG = ("w_mem_kv", "w_out", "conv_w_in", "mla_w_in", "mla_w_uq", "mla_w_ukv")
_SMALL_SHARDED = ("conv_dw", "mla_q_norm_g", "mla_kv_norm_g")
_REPLICATED = ("norm_g", "mem_norm_g", "conv_dw_b", "conv_ln_g", "conv_ln_b", "final_norm_g")
_PERM = {"mla_w_in": (_mla_in_perm, _mla_in_unperm), "mla_w_uq": (_uq_perm, _uq_unperm)}


def _join(n, blocks):
    if n in _ROW_CUT:
        _, L, r, c = blocks.shape
        return blocks.transpose(1, 0, 2, 3).reshape(L, N_DEV * r, c)
    if n in _COL_CUT:
        _, _, r, c = blocks.shape
        return blocks.reshape(N_DEV, r, c).transpose(1, 0, 2).reshape(r, N_DEV * c)
    return blocks.reshape(-1)


def _cut(n, full, shard_shape):
    if n in _ROW_CUT:
        L, r, c = shard_shape
        return full.reshape(L, N_DEV, r, c).transpose(1, 0, 2, 3)
    if n in _COL_CUT:
        _, r, c = shard_shape
        return full.reshape(r, N_DEV, c).transpose(1, 0, 2).reshape(N_DEV, 1, r, c)
    return full.reshape(N_DEV, 1, -1)


def _flat_pad(parts, size):
    flat = jnp.concatenate([p.reshape(-1) for p in parts])
    return jnp.concatenate([flat, jnp.zeros((size - flat.shape[0],), flat.dtype)])


SMALL_LANES = 128 * 8


def _as_tiles(flat_parts):
    total = sum(p.size for p in flat_parts)
    size = -(-total // SMALL_LANES) * SMALL_LANES
    return _flat_pad(flat_parts, size).reshape(8, size // 8)


def _split_flat(flat, like):
    out, o = [], 0
    for a in like:
        out.append(flat[o:o + a.size].reshape(a.shape))
        o += a.size
    return out


_HBM = pl.BlockSpec(memory_space=pltpu.HBM)
_VMEM = pl.BlockSpec(memory_space=pltpu.VMEM)


def _position():
    return lax.axis_index("x"), lax.axis_index("y"), lax.axis_index("c")


def _dma_sems(n):
    return [pltpu.SemaphoreType.DMA((n,)), pltpu.SemaphoreType.DMA((n,))]


def _run_stage(stage, name):
    n_in, n_out = len(stage.ins), len(stage.out_shapes)

    def body(*refs):
        ins, outs, sems = refs[:n_in], refs[n_in:n_in + n_out], refs[n_in + n_out:]
        stage.start(ins, outs, sems)
        stage.wait(ins, outs, sems)

    outs = _call(body, name, stage.out_shapes, in_specs=[_HBM] * n_in, out_specs=[_HBM] * n_out, scratch=stage.sems,
                 aliases=stage.aliases)(*stage.ins)
    stage.outs = list(outs)
    return stage.outs


def _gather_chips_stage(shards):
    n = len(shards)

    def copies(x_refs, out_refs, sems):
        send_sems, recv_sems, _ = sems
        x, y, c = _position()
        peers = [(x, y, 1 - c), (1 - x, y, c), (x, 1 - y, c), (1 - x, 1 - y, c)]
        out = []
        for a in range(n):
            for k, (px, py, pc) in enumerate(peers):
                send = pltpu.make_async_remote_copy(src_ref=x_refs[a], dst_ref=out_refs[a].at[4 * x + 2 * y + c],
                                                    send_sem=send_sems.at[4 * a + k], recv_sem=recv_sems.at[4 * a + k],
                                                    device_id=(px, py, pc), device_id_type=MESH)
                recv = pltpu.make_async_remote_copy(src_ref=x_refs[a], dst_ref=out_refs[a].at[4 * px + 2 * py + pc],
                                                    send_sem=send_sems.at[4 * a + k], recv_sem=recv_sems.at[4 * a + k],
                                                    device_id=(px, py, pc), device_id_type=MESH)
                out.append((send, recv))
        return out

    def local(x_refs, out_refs, sems):
        x, y, c = _position()
        return [pltpu.make_async_copy(x_refs[a], out_refs[a].at[4 * x + 2 * y + c], sems[2].at[a]) for a in range(n)]

    def start(x_refs, out_refs, sems):
        for cp in local(x_refs, out_refs, sems):
            cp.start()
        for send, _ in copies(x_refs, out_refs, sems):
            send.start()

    def wait(x_refs, out_refs, sems):
        for send, recv in copies(x_refs, out_refs, sems):
            recv.wait_recv()
            send.wait_send()
        for cp in local(x_refs, out_refs, sems):
            cp.wait()

    return _Stage(shards, [jax.ShapeDtypeStruct((N_DEV,) + a.shape, a.dtype) for a in shards],
                  _dma_sems(4 * n) + [pltpu.SemaphoreType.DMA((n,))], start, wait)


def _gather_sibling_stage(bufs):
    n = len(bufs)

    def copies(out_refs, sems):
        send_sems, recv_sems = sems
        x, y, c = _position()
        out = []
        for a in range(n):
            for j, (px, py) in enumerate([(1 - x, y), (x, 1 - y), (1 - x, 1 - y)]):
                mine, theirs = out_refs[a].at[4 * px + 2 * py + c], out_refs[a].at[4 * px + 2 * py + (1 - c)]
                send = pltpu.make_async_remote_copy(src_ref=mine, dst_ref=mine, send_sem=send_sems.at[3 * a + j],
                                                    recv_sem=recv_sems.at[3 * a + j], device_id=(x, y, 1 - c),
                                                    device_id_type=MESH)
                recv = pltpu.make_async_remote_copy(src_ref=mine, dst_ref=theirs, send_sem=send_sems.at[3 * a + j],
                                                    recv_sem=recv_sems.at[3 * a + j], device_id=(x, y, 1 - c),
                                                    device_id_type=MESH)
                out.append((send, recv))
        return out

    def start(_, out_refs, sems):
        for send, _r in copies(out_refs, sems):
            send.start()

    def wait(_, out_refs, sems):
        for send, recv in copies(out_refs, sems):
            recv.wait_recv()
            send.wait_send()

    return _Stage(bufs, [jax.ShapeDtypeStruct(b.shape, b.dtype) for b in bufs], _dma_sems(3 * n), start, wait,
                  aliases={a: a for a in range(n)})


def _all_gather_small(v, name):
    r, n = v.shape

    def body(x_ref, out_ref, send_sems, recv_sems, local_sem):
        x, y, c = _position()
        me = 4 * x + 2 * y + c
        mine = pltpu.make_async_copy(x_ref, out_ref.at[me], local_sem)
        mine.start()
        flips = [(fx, fy, fc) for fx in (0, 1) for fy in (0, 1) for fc in (0, 1)][1:]
        copies = []
        for k, (fx, fy, fc) in enumerate(flips):
            peer = (x ^ fx, y ^ fy, c ^ fc)
            cp = pltpu.make_async_remote_copy(src_ref=x_ref, dst_ref=out_ref.at[me], send_sem=send_sems.at[k],
                                              recv_sem=recv_sems.at[k], device_id=peer, device_id_type=MESH)
            cp.start()
            copies.append(cp)
        for k, (fx, fy, fc) in enumerate(flips):
            px, py, pc = x ^ fx, y ^ fy, c ^ fc
            src = out_ref.at[4 * px + 2 * py + pc]
            pltpu.make_async_remote_copy(src_ref=x_ref, dst_ref=src, send_sem=send_sems.at[k], recv_sem=recv_sems.at[k],
                                         device_id=(px, py, pc), device_id_type=MESH).wait_recv()
        for cp in copies:
            cp.wait_send()
        mine.wait()

    return _call(body, name, jax.ShapeDtypeStruct((N_DEV, r, n), v.dtype), in_specs=[_VMEM], out_specs=_VMEM,
                 scratch=_dma_sems(7) + [pltpu.SemaphoreType.DMA(())])(v)


def _reduce_sibling_stage(gs):
    n = len(gs)

    def copies(g_refs, out_refs, sems):
        send_sems, recv_sems = sems
        x, y, c = _position()
        return [pltpu.make_async_remote_copy(src_ref=g_refs[a].at[2 * k + (1 - c)], dst_ref=out_refs[a].at[k],
                                             send_sem=send_sems.at[4 * a + k], recv_sem=recv_sems.at[4 * a + k],
                                             device_id=(x, y, 1 - c), device_id_type=MESH)
                for a in range(n) for k in range(4)]

    def start(g_refs, out_refs, sems):
        for cp in copies(g_refs, out_refs, sems):
            cp.start()

    def wait(g_refs, out_refs, sems):
        for cp in copies(g_refs, out_refs, sems):
            cp.wait()

    return _Stage(gs, [jax.ShapeDtypeStruct((4,) + g.shape[1:], g.dtype) for g in gs], _dma_sems(4 * n), start, wait)


def _rows2d(shape):
    cols = shape[-1]
    rows = 1
    for s in shape[:-1]:
        rows *= s
    return rows, cols


def _add_own(g, recv, name):
    rows, cols = _rows2d(g.shape[1:])
    tr = _pick(rows, 256, 8)
    c = lax.axis_index("c").astype(jnp.int32).reshape(1)

    def body(c_ref, g_ref, r_ref, o_ref):
        o_ref[...] = g_ref[...] + r_ref[...]

    grid_spec = pltpu.PrefetchScalarGridSpec(
        num_scalar_prefetch=1, grid=(4, rows // tr),
        in_specs=[pl.BlockSpec((None, None, tr, cols), lambda k, i, c_ref: (k, c_ref[0], i, 0)),
                  pl.BlockSpec((None, tr, cols), lambda k, i, c_ref: (k, i, 0))],
        out_specs=pl.BlockSpec((None, tr, cols), lambda k, i, c_ref: (k, i, 0)))
    return _call(body, name, jax.ShapeDtypeStruct((4, rows, cols), F32), grid_spec=grid_spec,
                 dims=("parallel", "parallel"))(c, g.reshape(4, 2, rows, cols), recv.reshape(4, rows, cols))


def _reduce_chips_stage(pas):
    n = len(pas)

    def copies(pa_refs, out_refs, sems):
        send_sems, recv_sems, _ = sems
        x, y, c = _position()
        my_chip = 2 * x + y
        out = []
        for a in range(n):
            for j, (px, py) in enumerate([(1 - x, y), (x, 1 - y), (1 - x, 1 - y)]):
                send = pltpu.make_async_remote_copy(src_ref=pa_refs[a].at[2 * px + py], dst_ref=out_refs[a].at[my_chip],
                                                    send_sem=send_sems.at[3 * a + j], recv_sem=recv_sems.at[3 * a + j],
                                                    device_id=(px, py, c), device_id_type=MESH)
                recv = pltpu.make_async_remote_copy(src_ref=pa_refs[a].at[2 * px + py], dst_ref=out_refs[a].at[2 * px + py],
                                                    send_sem=send_sems.at[3 * a + j], recv_sem=recv_sems.at[3 * a + j],
                                                    device_id=(px, py, c), device_id_type=MESH)
                out.append((send, recv))
        return out

    def local(pa_refs, out_refs, sems):
        x, y, _ = _position()
        return [pltpu.make_async_copy(pa_refs[a].at[2 * x + y], out_refs[a].at[2 * x + y], sems[2].at[a]) for a in range(n)]

    def start(pa_refs, out_refs, sems):
        for cp in local(pa_refs, out_refs, sems):
            cp.start()
        for send, _r in copies(pa_refs, out_refs, sems):
            send.start()

    def wait(pa_refs, out_refs, sems):
        for send, recv in copies(pa_refs, out_refs, sems):
            recv.wait_recv()
            send.wait_send()
        for cp in local(pa_refs, out_refs, sems):
            cp.wait()

    return _Stage(pas, [jax.ShapeDtypeStruct(pa.shape, pa.dtype) for pa in pas],
                  _dma_sems(3 * n) + [pltpu.SemaphoreType.DMA((n,))], start, wait)


def _adamw_math(w, g, m, v):
    m = ADAM_B1 * m + (1.0 - ADAM_B1) * g
    v = ADAM_B2 * v + (1.0 - ADAM_B2) * (g * g)
    m_hat = m / (1.0 - ADAM_B1 ** ADAM_STEP)
    v_hat = v / (1.0 - ADAM_B2 ** ADAM_STEP)
    delta = -ADAM_LR * (m_hat / (jnp.sqrt(v_hat) + ADAM_EPS) + ADAM_WD * w)
    return delta, m, v


def _sum_adamw(parts, w, m, v, name):
    n, rows, cols = parts.shape
    tr = _pick(rows, 128, 8)

    def body(p_ref, w_ref, m_ref, v_ref, g_ref, d_ref, nm_ref, nv_ref):
        g = p_ref[0]
        for k in range(1, n):
            g = g + p_ref[k]
        d, nm, nv = _adamw_math(w_ref[...], g, m_ref[...], v_ref[...])
        g_ref[...], d_ref[...], nm_ref[...], nv_ref[...] = g, d, nm, nv

    blk = pl.BlockSpec((tr, cols), lambda i: (i, 0))
    return _call(body, name, [jax.ShapeDtypeStruct((rows, cols), F32)] * 4, grid=(rows // tr,),
                 in_specs=[pl.BlockSpec((n, tr, cols), lambda i: (0, i, 0)), blk, blk, blk],
                 out_specs=[blk] * 4, dims=("parallel",))(parts, w, m, v)


_WEIGHTS = ("norm_g", "mem_norm_g", "w_mem_kv", "w_out", "conv_w_in", "conv_dw", "conv_dw_b", "conv_ln_g", "conv_ln_b",
            "mla_w_in", "mla_q_norm_g", "mla_w_uq", "mla_kv_norm_g", "mla_w_ukv", "final_norm_g")


_GATHER_GROUPS = {"a": ("conv_w_in",), "b": ("w_mem_kv", "w_out"), "c": ("mla_w_in", "mla_w_uq", "mla_w_ukv")}
_CARRIERS = {"l0_norm": ("gather chips", ("a",)), "l0_in": ("gather chips", ("b",)), "l0_glu": ("gather sibling", ("b",)),
             "l0_dwconv": ("gather chips", ("c",)), "l0_ln": ("gather sibling", ("c",)),
             "l1_in_dx": ("reduce sibling", ("l1",)), "l0_ln_bwd": ("reduce sibling", ("l0a",)),
             "l0_gate_bwd": ("reduce chips", ("l1", 0, 3)), "l0_dwconv_bwd": ("reduce chips", ("l1", 3, 5)),
             "l0_in_dw": ("reduce chips", ("l0a",)), "l0_in_dx": ("reduce sibling alone, then chips", ("l0b",))}


class _Schedule:
    def __init__(self, w):
        self.w, self.full, self.gather, self.reduce, self.reduced = w, {}, {}, {}, {}
        small = _all_gather_small(_as_tiles([w[n] for n in _SMALL_SHARDED]), "gather_small_weights").reshape(N_DEV, -1)
        o = 0
        for n in _SMALL_SHARDED:
            self.full[n] = _join(n, small[:, o:o + w[n].size].reshape((N_DEV,) + w[n].shape))
            o += w[n].size
        for n in _REPLICATED:
            self.full[n] = w[n]

    def carry(self, call):
        kind, (g, *part) = _CARRIERS[call]
        if kind == "gather chips":
            self.gather[g] = [_gather_chips_stage([self.w[n].astype(BF16) for n in _GATHER_GROUPS[g]])]
            return self.gather[g][0]
        if kind == "gather sibling":
            self.gather[g].append(_gather_sibling_stage(self.gather[g][0].outs))
            return self.gather[g][1]
        r = self.reduce[g]
        if kind == "reduce sibling":
            r["sibling"] = _reduce_sibling_stage(r["cut"])
            return r["sibling"]
        if kind != "reduce chips":
            r["sibling"] = _reduce_sibling_stage(r["cut"])
            _run_stage(r["sibling"], "reduce_sibling_" + g)
        if "partial" not in r:
            r["partial"] = [_add_own(c, s, "reduce_add_%s_%d" % (g, i))
                            for i, (c, s) in enumerate(zip(r["cut"], r["sibling"].outs))]
        lo, hi = part if part else (0, len(r["keys"]))
        stage = _reduce_chips_stage(r["partial"][lo:hi])
        r.setdefault("chips", []).append((r["keys"][lo:hi], stage))
        return stage

    def __getitem__(self, name):
        if name not in self.full:
            g = [k for k, names in _GATHER_GROUPS.items() if name in names][0]
            if len(self.gather[g]) == 1:
                self.gather[g].append(_gather_sibling_stage(self.gather[g][0].outs))
                _run_stage(self.gather[g][1], "gather_sibling_" + g)
            for n, buf in zip(_GATHER_GROUPS[g], self.gather[g][1].outs):
                self.full[n] = _PERM[n][0](_join(n, buf)) if n in _PERM else _join(n, buf)
        return self.full[name]

    def ready(self, group, grads):
        keys, cut, small = [], [], []
        for (n, layer), g in grads.items():
            if n in _SMALL_SHARDED:
                small.append(_cut(n, g, self.w[n].shape).reshape(N_DEV, -1))
                continue
            keys.append((n, layer))
            if layer is not None:
                cut.append(g.reshape((N_DEV,) + self.w[n].shape[1:]))
            else:
                cut.append(_cut(n, _PERM[n][1](g) if n in _PERM else g, self.w[n].shape))
        if small:
            keys.append(("small", None))
            cut.append(jax.vmap(lambda r: _as_tiles([r]))(jnp.concatenate(small, axis=1)))
        self.reduce[group] = {"keys": keys, "cut": cut}

    def finish(self):
        out = {}
        for r in self.reduce.values():
            for keys, stage in r["chips"]:
                out.update(dict(zip(keys, stage.outs)))
        return out


def kernel(x, mem, positions, norm_g, mem_norm_g, w_mem_kv, w_out, conv_w_in, conv_dw, conv_dw_b, conv_ln_g, conv_ln_b, mla_w_in, mla_q_norm_g, mla_w_uq, mla_kv_norm_g, mla_w_ukv, final_norm_g, loss_target, m_norm_g, m_mem_norm_g, m_w_mem_kv, m_w_out, m_conv_w_in, m_conv_dw, m_conv_dw_b, m_conv_ln_g, m_conv_ln_b, m_mla_w_in, m_mla_q_norm_g, m_mla_w_uq, m_mla_kv_norm_g, m_mla_w_ukv, m_final_norm_g, v_norm_g, v_mem_norm_g, v_w_mem_kv, v_w_out, v_conv_w_in, v_conv_dw, v_conv_dw_b, v_conv_ln_g, v_conv_ln_b, v_mla_w_in, v_mla_q_norm_g, v_mla_w_uq, v_mla_kv_norm_g, v_mla_w_ukv, v_final_norm_g):
    w = dict(zip(_WEIGHTS, (norm_g, mem_norm_g, w_mem_kv, w_out, conv_w_in, conv_dw, conv_dw_b, conv_ln_g, conv_ln_b,
                            mla_w_in, mla_q_norm_g, mla_w_uq, mla_kv_norm_g, mla_w_ukv, final_norm_g)))
    m = dict(zip(_WEIGHTS, (m_norm_g, m_mem_norm_g, m_w_mem_kv, m_w_out, m_conv_w_in, m_conv_dw, m_conv_dw_b, m_conv_ln_g,
                            m_conv_ln_b, m_mla_w_in, m_mla_q_norm_g, m_mla_w_uq, m_mla_kv_norm_g, m_mla_w_ukv, m_final_norm_g)))
    v = dict(zip(_WEIGHTS, (v_norm_g, v_mem_norm_g, v_w_mem_kv, v_w_out, v_conv_w_in, v_conv_dw, v_conv_dw_b, v_conv_ln_g,
                            v_conv_ln_b, v_mla_w_in, v_mla_q_norm_g, v_mla_w_uq, v_mla_kv_norm_g, v_mla_w_ukv, v_final_norm_g)))

    sched = _Schedule(w)
    loss_local, dx, G = _forward_backward(x, mem, positions, loss_target, sched)
    loss = lax.psum(loss_local, ("x", "y", "c"))

    from_chips = sched.finish()
    out = [{}, {}, {}, {}]
    for n in _BIG:
        if n in _ROW_CUT:
            res = [_sum_adamw(from_chips[(n, l)], w[n][l], m[n][l], v[n][l], "adamw_%s_%d" % (n, l)) for l in range(w[n].shape[0])]
            res = [jnp.stack(r) for r in zip(*res)]
        else:
            rows, cols = _rows2d(w[n].shape)
            res = _sum_adamw(from_chips[(n, None)], w[n].reshape(rows, cols), m[n].reshape(rows, cols),
                             v[n].reshape(rows, cols), "adamw_" + n)
        for o, r in zip(out, res):
            o[n] = r.reshape(w[n].shape)
    small_like = [w[n] for n in _SMALL_SHARDED]
    res = _sum_adamw(from_chips[("small", None)], _as_tiles(small_like), _as_tiles([m[n] for n in _SMALL_SHARDED]),
                     _as_tiles([v[n] for n in _SMALL_SHARDED]), "adamw_small")
    for o, r in zip(out, res):
        for n, a in zip(_SMALL_SHARDED, _split_flat(r.reshape(-1), small_like)):
            o[n] = a

    rep_like = [w[n] for n in _REPLICATED]
    rep_parts = _all_gather_small(_as_tiles([G[n] for n in _REPLICATED]), "gather_replicated_grads")
    res = _sum_adamw(rep_parts, _as_tiles(rep_like), _as_tiles([m[n] for n in _REPLICATED]),
                     _as_tiles([v[n] for n in _REPLICATED]), "adamw_replicated")
    for o, r in zip(out, res):
        for n, a in zip(_REPLICATED, _split_flat(r.reshape(-1), rep_like)):
            o[n] = a

    return (loss, dx, *[out[0][n] for n in _WEIGHTS], *[out[1][n] for n in _WEIGHTS],
            *[out[2][n] for n in _WEIGHTS], *[out[3][n] for n in _WEIGHTS])
```

```python
import jax
import jax.numpy as jnp
from jax import lax
from jax.experimental import pallas as pl
from jax.experimental.pallas import tpu as pltpu

F32 = jnp.float32
BF16 = jnp.bfloat16
MESH = pl.DeviceIdType.MESH
N_DEV = 8
VMEM_LIMIT_BYTES = 48 * 1024 * 1024

MEM_HEADS, MEM_HEAD_DIM = 4, 128
MEM_WIDTH = MEM_HEADS * MEM_HEAD_DIM
CONV_KERNEL = 31
CONV_PAD = 32
MLA_HEADS, MLA_NOPE, MLA_ROPE, MLA_V = 12, 128, 64, 128
MLA_QK = MLA_NOPE + MLA_ROPE
HALF_ROPE = MLA_ROPE // 2
Q_RANK, KV_RANK = 512, 256
ROPE_THETA = 10000.0
RMS_EPS = 1e-6
LN_EPS = 1e-5
ADAM_LR, ADAM_B1, ADAM_B2, ADAM_EPS, ADAM_WD, ADAM_STEP = 0.001, 0.9, 0.999, 1e-08, 0.01, 10
NEG = -1e30


class _Stage:
    def __init__(self, ins, out_shapes, sems, start, wait, aliases=None):
        self.ins, self.out_shapes, self.sems = list(ins), list(out_shapes), list(sems)
        self.start, self.wait, self.aliases, self.outs = start, wait, dict(aliases or {}), None


def _call(body, name, out_shape, grid=None, in_specs=None, out_specs=None, scratch=(), dims=None, grid_spec=None, aliases=None,
          carry=None):
    params = dict(vmem_limit_bytes=VMEM_LIMIT_BYTES)
    if dims is not None:
        params["dimension_semantics"] = dims
    kw = {}
    if carry is not None:
        single = not isinstance(out_shape, (list, tuple))
        main_out = [out_shape] if single else list(out_shape)
        main_specs = [out_specs] if single else list(out_specs)
        n_in, n_out, n_scr = len(in_specs), len(main_out), len(scratch)
        x_in, x_out = len(carry.ins), len(carry.out_shapes)
        inner, steps = body, tuple(grid)

        def body(*refs):
            ins, xin = refs[:n_in], refs[n_in:n_in + x_in]
            outs = refs[n_in + x_in:n_in + x_in + n_out]
            xout = refs[n_in + x_in + n_out:n_in + x_in + n_out + x_out]
            scr = refs[n_in + x_in + n_out + x_out:n_in + x_in + n_out + x_out + n_scr]
            xsem = refs[n_in + x_in + n_out + x_out + n_scr:]
            ids = [pl.program_id(a) for a in range(len(steps))]
            first, last = ids[0] == 0, ids[0] == steps[0] - 1
            for a in range(1, len(steps)):
                first = jnp.logical_and(first, ids[a] == 0)
                last = jnp.logical_and(last, ids[a] == steps[a] - 1)
            pl.when(first)(lambda: carry.start(xin, xout, xsem))
            inner(*ins, *outs, *scr)
            pl.when(last)(lambda: carry.wait(xin, xout, xsem))

        hbm = pl.BlockSpec(memory_space=pltpu.HBM)
        aliases = dict(aliases or {})
        aliases.update({n_in + k: n_out + v for k, v in carry.aliases.items()})
        res = _call(body, name, main_out + carry.out_shapes, grid=grid, in_specs=list(in_specs) + [hbm] * x_in,
                    out_specs=main_specs + [hbm] * x_out, scratch=list(scratch) + carry.sems, dims=dims, aliases=aliases)

        def run(*args):
            outs = res(*args, *carry.ins)
            carry.outs = list(outs[n_out:])
            return outs[0] if single else outs[:n_out]

        return run
    if aliases:
        kw["input_output_aliases"] = aliases
    if grid_spec is not None:
        kw["grid_spec"] = grid_spec
    else:
        if grid is not None:
            kw["grid"] = grid
        kw["in_specs"] = in_specs
        kw["out_specs"] = out_specs
        kw["scratch_shapes"] = list(scratch)
    return pl.pallas_call(body, name=name, out_shape=out_shape, compiler_params=pltpu.CompilerParams(**params), **kw)


def _pick(n, target, mult):
    best = None
    for d in range(mult, min(n, target) + 1, mult):
        if n % d == 0:
            best = d
    return n if best is None else best


_DOT_DIMS = {"nn": (((1,), (0,)), ((), ())), "nt": (((1,), (1,)), ((), ())), "tn": (((0,), (0,)), ((), ()))}


def _mm(a, b, mode, out_dtype, name, res=None, carry=None):
    if mode == "tn":
        a, mode = a.T, "nn"
    if mode == "nn":
        (M, K), N = a.shape, b.shape[1]
    else:
        (M, K), N = a.shape, b.shape[0]
    tm = _pick(M, 1024, 8)
    tn = _pick(N, 512, 128)
    tk = _pick(K, 1024, 128)
    nk = K // tk
    has_res = res is not None

    def body(*refs):
        if has_res:
            a_ref, b_ref, r_ref, o_ref, acc = refs
        else:
            a_ref, b_ref, o_ref, acc = refs
        k = pl.program_id(2)

        @pl.when(k == 0)
        def _():
            acc[...] = jnp.zeros_like(acc)

        acc[...] += lax.dot_general(a_ref[...].astype(BF16), b_ref[...].astype(BF16), _DOT_DIMS[mode],
                                    preferred_element_type=F32)

        @pl.when(k == nk - 1)
        def _():
            r = acc[...]
            if has_res:
                r = r + r_ref[...]
            o_ref[...] = r.astype(o_ref.dtype)

    a_spec = pl.BlockSpec((tm, tk), lambda i, j, k: (i, k))
    b_spec = {"nn": pl.BlockSpec((tk, tn), lambda i, j, k: (k, j)),
              "nt": pl.BlockSpec((tn, tk), lambda i, j, k: (j, k))}[mode]
    o_spec = pl.BlockSpec((tm, tn), lambda i, j, k: (i, j))
    in_specs = [a_spec, b_spec] + ([o_spec] if has_res else [])
    args = (a, b) + ((res,) if has_res else ())
    return _call(body, name, jax.ShapeDtypeStruct((M, N), out_dtype), grid=(M // tm, N // tn, nk),
                 in_specs=in_specs, out_specs=o_spec, scratch=[pltpu.VMEM((tm, tn), F32)],
                 dims=("parallel", "parallel", "arbitrary"), carry=carry)(*args)


def _views(rows):
    return [r if isinstance(r, tuple) else (r, r.shape[1], 0) for r in rows]


def _rowwise(f, rows, params, outs, name, tb=256, carry=None, into=None):
    rows = _views(rows)
    T = rows[0][0].shape[0]
    tb = min(tb, T)
    nr, npar = len(rows), len(params)
    outs = [o if len(o) == 3 else (o[0], o[1], o[0]) for o in outs]
    into = into or []

    def body(*refs):
        vals = f(*[r[...].astype(F32) for r in refs[:nr]], *[p[...] for p in refs[nr:nr + npar]])
        for o_ref, v in zip(refs[nr + npar + len(into):], vals):
            o_ref[...] = v.astype(o_ref.dtype)

    row_spec = lambda w, cb=0: pl.BlockSpec((tb, w), lambda i: (i, cb))
    par_spec = lambda w: pl.BlockSpec((1, w), lambda i: (0, 0))
    out_shape = [jax.ShapeDtypeStruct((T, tw), dt) for _, dt, tw in outs]
    out_specs = [row_spec(w) for w, _, _ in outs]
    in_specs = [row_spec(w, cb) for _, w, cb in rows] + [par_spec(p.shape[1]) for p in params]
    args = [r[0] for r in rows] + list(params)
    aliases = {}
    for k, arr, cb in into:
        aliases[len(args)] = k
        in_specs.append(pl.BlockSpec(memory_space=pl.ANY))
        args.append(arr)
        out_shape[k] = jax.ShapeDtypeStruct(arr.shape, arr.dtype)
        out_specs[k] = row_spec(outs[k][0], cb)
    return _call(body, name, out_shape, grid=(T // tb,), in_specs=in_specs, out_specs=out_specs, dims=("parallel",),
                 carry=carry, aliases=aliases)(*args)


def _rowwise_bwd(f, rows, params, douts, n_diff, name, tb=256, carry=None, add=None, into=None):
    rows, douts = _views(rows), _views(douts)
    T = rows[0][0].shape[0]
    tb = min(tb, T)
    nr, npar, nd = len(rows), len(params), len(douts)
    n_add = 0 if add is None else 1

    def body(*refs):
        rv = [r[...].astype(F32) for r in refs[:nr]]
        pv = [p[...] for p in refs[nr:nr + npar]]
        dv = [d[...].astype(F32) for d in refs[nr + npar:nr + npar + nd]]
        o_refs = refs[nr + npar + nd + n_add + (0 if into is None else 1):]
        fixed = rv[n_diff:]

        def g(*xs):
            return tuple(f(*xs[:n_diff], *fixed, *xs[n_diff:]))

        _, vjp = jax.vjp(g, *rv[:n_diff], *pv)
        grads = list(vjp(tuple(dv)))
        if add is not None:
            grads[0] = grads[0] + refs[nr + npar + nd][...]
        for o_ref, gr in zip(o_refs[:n_diff], grads[:n_diff]):
            o_ref[...] = gr.astype(o_ref.dtype)
        first = pl.program_id(0) == 0
        for o_ref, gr in zip(o_refs[n_diff:], grads[n_diff:]):
            @pl.when(first)
            def _(o_ref=o_ref):
                o_ref[...] = jnp.zeros_like(o_ref)

            o_ref[...] += gr

    row_spec = lambda w, cb=0: pl.BlockSpec((tb, w), lambda i: (i, cb))
    par_spec = lambda w: pl.BlockSpec((1, w), lambda i: (0, 0))
    out_shape = ([jax.ShapeDtypeStruct((T, w), F32) for _, w, _ in rows[:n_diff]]
                 + [jax.ShapeDtypeStruct((1, p.shape[1]), F32) for p in params])
    out_specs = [row_spec(w) for _, w, _ in rows[:n_diff]] + [par_spec(p.shape[1]) for p in params]
    in_specs = ([row_spec(w, cb) for _, w, cb in rows] + [par_spec(p.shape[1]) for p in params]
                + [row_spec(w, cb) for _, w, cb in douts])
    args = [r[0] for r in rows] + list(params) + [d[0] for d in douts]
    aliases = None
    if add is not None:
        in_specs.append(row_spec(add.shape[1]))
        args.append(add)
    if into is not None:
        aliases = {len(args): 0}
        in_specs.append(pl.BlockSpec(memory_space=pl.ANY))
        args.append(into[0])
        out_shape[0] = jax.ShapeDtypeStruct(into[0].shape, into[0].dtype)
        out_specs[0] = row_spec(rows[0][1], into[1])
    return _call(body, name, out_shape, grid=(T // tb,), in_specs=in_specs, out_specs=out_specs,
                 dims=("arbitrary",), carry=carry, aliases=aliases)(*args)


def _sig(x):
    return 1.0 / (1.0 + jnp.exp(-x))


def _rms(x, g):
    return x * lax.rsqrt(jnp.mean(x * x, axis=-1, keepdims=True) + RMS_EPS) * g


def _f_rms(x, g):
    return (_rms(x, g),)


def _f_glu(a, gate):
    return (a * _sig(gate),)


def _f_ln_silu(x, g, b):
    mu = jnp.mean(x, axis=-1, keepdims=True)
    xc = x - mu
    var = jnp.mean(xc * xc, axis=-1, keepdims=True)
    y = xc * lax.rsqrt(var + LN_EPS) * g + b
    return (y * _sig(y),)


def _rope128(x, cos_p, sin_p):
    return x * cos_p + pltpu.roll(x, 64, 1) * sin_p


def _rope128_t(d, cos_p, sin_p):
    return d * cos_p + pltpu.roll(d * sin_p, 64, 1)


def _f_rope(xq, xk, cos_p, sin_p):
    heads = [_rope128(xq[:, h * 128:(h + 1) * 128], cos_p, sin_p) for h in range(MLA_HEADS)]
    return (jnp.concatenate(heads, axis=1), _rope128(xk, cos_p, sin_p))


def _f_rope_t(dq, dk_heads, cos_p, sin_p):
    heads = [_rope128_t(dq[:, h * 128:(h + 1) * 128], cos_p, sin_p) for h in range(MLA_HEADS)]
    dk = dk_heads[:, 0:128]
    for h in range(1, MLA_HEADS):
        dk = dk + dk_heads[:, h * 128:(h + 1) * 128]
    return (jnp.concatenate(heads, axis=1), _rope128_t(dk, cos_p, sin_p))


GATE_LANES = 256


def _gate_fwd(ycat, proj, z_col, name, tb=1024):
    T, width = ycat.shape
    zb = z_col // GATE_LANES

    def body(y_ref, z_ref, o_ref):
        z = z_ref[...]
        o_ref[...] = (y_ref[...] * (z * _sig(z))).astype(o_ref.dtype)

    blk = pl.BlockSpec((tb, GATE_LANES), lambda i, c: (i, c))
    return _call(body, name, jax.ShapeDtypeStruct((T, width), BF16), grid=(T // tb, width // GATE_LANES),
                 in_specs=[blk, pl.BlockSpec((tb, GATE_LANES), lambda i, c: (i, zb + c))], out_specs=blk,
                 dims=("parallel", "parallel"))(ycat, proj)


def _gate_bwd(ycat, proj, z_col, dy, name, tb=1024, carry=None):
    T, width = ycat.shape
    zb = z_col // GATE_LANES

    def body(y_ref, z_ref, dy_ref, dycat_ref, dz_ref):
        z, d = z_ref[...], dy_ref[...]
        s = _sig(z)
        dycat_ref[...] = d * (z * s)
        dz_ref[...] = (d * y_ref[...] * (s * (1.0 + z * (1.0 - s)))).astype(dz_ref.dtype)

    blk = pl.BlockSpec((tb, GATE_LANES), lambda i, c: (i, c))
    zblk = pl.BlockSpec((tb, GATE_LANES), lambda i, c: (i, zb + c))
    return _call(body, name, [jax.ShapeDtypeStruct((T, width), F32), jax.ShapeDtypeStruct(proj.shape, BF16)],
                 grid=(T // tb, width // GATE_LANES), in_specs=[blk, zblk, blk], out_specs=[blk, zblk],
                 dims=("parallel", "parallel"), carry=carry)(ycat, proj, dy)


def _glu_bwd(proj, d_glu, d_proj, name, tb=256):
    T, w = d_glu.shape

    def body(a_ref, g_ref, d_ref, _, o_ref):
        s = _sig(g_ref[...])

        @pl.when(pl.program_id(1) == 0)
        def _():
            o_ref[...] = (d_ref[...] * s).astype(o_ref.dtype)

        @pl.when(pl.program_id(1) == 1)
        def _():
            o_ref[...] = (d_ref[...] * a_ref[...] * (s * (1.0 - s))).astype(o_ref.dtype)

    return _call(body, name, jax.ShapeDtypeStruct(d_proj.shape, d_proj.dtype), grid=(T // tb, 2),
                 in_specs=[pl.BlockSpec((tb, w), lambda i, c: (i, 0)), pl.BlockSpec((tb, w), lambda i, c: (i, 1)),
                           pl.BlockSpec((tb, w), lambda i, c: (i, 0)), pl.BlockSpec(memory_space=pl.ANY)],
                 out_specs=pl.BlockSpec((tb, w), lambda i, c: (i, c)), dims=("parallel", "arbitrary"),
                 aliases={3: 0})(proj, proj, d_glu, d_proj)


def _final_loss(h, tgt, g, name, tb=256):
    T, D = h.shape

    def body(h_ref, t_ref, g_ref, dh_ref, dg_ref, loss_ref):
        tv = t_ref[...]

        def rowloss(hh, gg):
            e = _rms(hh, gg) - tv
            return 0.5 * jnp.mean(e * e, axis=-1, keepdims=True)

        lr, vjp = jax.vjp(rowloss, h_ref[...], g_ref[...])
        dh, dg = vjp(jnp.ones_like(lr))
        dh_ref[...] = dh

        @pl.when(pl.program_id(0) == 0)
        def _():
            dg_ref[...] = jnp.zeros_like(dg_ref)
            loss_ref[...] = jnp.zeros_like(loss_ref)

        dg_ref[...] += dg
        loss_ref[...] += jnp.broadcast_to(jnp.sum(lr, axis=0, keepdims=True), loss_ref.shape)

    row = pl.BlockSpec((tb, D), lambda i: (i, 0))
    par = pl.BlockSpec((1, D), lambda i: (0, 0))
    return _call(body, name,
                 [jax.ShapeDtypeStruct((T, D), F32), jax.ShapeDtypeStruct((1, D), F32), jax.ShapeDtypeStruct((1, 128), F32)],
                 grid=(T // tb,), in_specs=[row, row, par],
                 out_specs=[row, par, pl.BlockSpec((1, 128), lambda i: (0, 0))], dims=("arbitrary",))(h, tgt, g)


CONV_ROWS = 128
CONV_LANES = 256


def _sublane_phases(pad, n):
    for r in range(1, 8):
        for c0 in range(0, n - 8, 256):
            rows = min(256, n - 8 - c0)
            pad[r, c0:c0 + rows, :] = pad[0, c0 + r:c0 + r + rows, :]


def _dwconv_fwd(x, w, b, name, carry=None):
    B, S, C = x.shape
    cb = CONV_LANES
    off = CONV_PAD - (CONV_KERNEL - 1)

    def body(x_ref, w_ref, b_ref, o_ref, pad):
        pad[0, 0:CONV_PAD, :] = jnp.zeros((CONV_PAD, cb), F32)
        pad[0, CONV_PAD:, :] = x_ref[...]
        _sublane_phases(pad, S + CONV_PAD)
        for t0 in range(0, S, CONV_ROWS):
            acc = jnp.broadcast_to(b_ref[...], (CONV_ROWS, cb))
            for k in range(CONV_KERNEL):
                r, base = (off + k) % 8, t0 + (off + k) // 8 * 8
                acc = acc + w_ref[k:k + 1, :] * pad[r, base:base + CONV_ROWS, :]
            o_ref[t0:t0 + CONV_ROWS, :] = acc

    return _call(body, name, jax.ShapeDtypeStruct((B, S, C), F32), grid=(B, C // cb),
                 in_specs=[pl.BlockSpec((None, S, cb), lambda i, j: (i, 0, j)),
                           pl.BlockSpec((CONV_KERNEL, cb), lambda i, j: (0, j)),
                           pl.BlockSpec((1, cb), lambda i, j: (0, j))],
                 out_specs=pl.BlockSpec((None, S, cb), lambda i, j: (i, 0, j)),
                 scratch=[pltpu.VMEM((8, S + CONV_PAD, cb), F32)], dims=("parallel", "parallel"), carry=carry)(x, w, b)


def _dwconv_bwd(x, w, dy, name, carry=None):
    B, S, C = x.shape
    cb = CONV_LANES
    off = CONV_PAD - (CONV_KERNEL - 1)
    groups = CONV_ROWS // 8

    def body(x_ref, w_ref, dy_ref, dx_ref, dw_ref, db_ref, dypad, wacc):
        dypad[0, 0:S, :] = dy_ref[...]
        dypad[0, S:, :] = jnp.zeros((CONV_PAD, cb), F32)
        _sublane_phases(dypad, S + CONV_PAD)
        wacc[...] = jnp.zeros_like(wacc)
        for t0 in range(0, S, CONV_ROWS):
            xc = x_ref[t0:t0 + CONV_ROWS, :]
            acc = jnp.zeros((CONV_ROWS, cb), F32)
            for k in range(CONV_KERNEL):
                o = (CONV_KERNEL - 1) - k
                dys = dypad[o % 8, t0 + o // 8 * 8:t0 + o // 8 * 8 + CONV_ROWS, :]
                acc = acc + w_ref[k:k + 1, :] * dys
                wacc[k] += jnp.sum((dys * xc).reshape(groups, 8, cb), axis=0)
            wacc[CONV_KERNEL] += jnp.sum(dy_ref[t0:t0 + CONV_ROWS, :].reshape(groups, 8, cb), axis=0)
            dx_ref[t0:t0 + CONV_ROWS, :] = acc

        @pl.when(pl.program_id(1) == 0)
        def _():
            dw_ref[...] = jnp.zeros_like(dw_ref)
            db_ref[...] = jnp.zeros_like(db_ref)

        for k in range(CONV_KERNEL):
            dw_ref[k:k + 1, :] += jnp.sum(wacc[k], axis=0, keepdims=True)
        db_ref[...] += jnp.sum(wacc[CONV_KERNEL], axis=0, keepdims=True)

    blk = pl.BlockSpec((None, S, cb), lambda j, i: (i, 0, j))
    return _call(body, name,
                 [jax.ShapeDtypeStruct((B, S, C), F32), jax.ShapeDtypeStruct((CONV_KERNEL, C), F32),
                  jax.ShapeDtypeStruct((1, C), F32)],
                 grid=(C // cb, B),
                 in_specs=[blk, pl.BlockSpec((CONV_KERNEL, cb), lambda j, i: (0, j)), blk],
                 out_specs=[blk, pl.BlockSpec((CONV_KERNEL, cb), lambda j, i: (0, j)),
                            pl.BlockSpec((1, cb), lambda j, i: (0, j))],
                 scratch=[pltpu.VMEM((8, S + CONV_PAD, cb), F32), pltpu.VMEM((CONV_KERNEL + 1, 8, cb), F32)],
                 dims=("parallel", "arbitrary"), carry=carry)(x, w, dy)


ATTN_TILE = {"fwd": 1024, "bwd": 1024}
ATTN_SUB = {"fwd": 256, "bwd": 512}


def _attn_shapes(Sq, Sk, causal, pass_):
    tq = min(Sq, ATTN_TILE[pass_])
    tk = tq if causal else min(Sk, ATTN_TILE[pass_])
    return tq, tk, min(ATTN_SUB[pass_], tq)


def _mask(row0, col0, rows, cols):
    r = row0 + lax.broadcasted_iota(jnp.int32, (rows, cols), 0)
    c = col0 + lax.broadcasted_iota(jnp.int32, (rows, cols), 1)
    return c <= r


def _attn_fwd(q, q_c0, qr, k, k_c0, kr, v, v_c0, B, Sq, Sk, H, causal, scale, name, into=None, o_c0=0, o_width=None,
              kv_stride=1):
    tq, tk, sub = _attn_shapes(Sq, Sk, causal, "fwd")
    nq, nk, nsub = Sq // tq, Sk // tk, tq // sub
    rope = qr is not None

    def body(*refs):
        refs = list(refs)
        qn_ref = refs.pop(0)
        qr_ref = refs.pop(0) if rope else None
        kn_ref = refs.pop(0)
        kr_ref = refs.pop(0) if rope else None
        v_ref = refs.pop(0)
        if into is not None:
            refs.pop(0)
        o_ref, lse_ref, m_s, l_s, acc = refs
        qi = pl.program_id(2)
        m_s[...] = jnp.full_like(m_s, NEG)
        l_s[...] = jnp.zeros_like(l_s)
        acc[...] = jnp.zeros_like(acc)
        qs = []
        for r in range(nsub):
            qn = qn_ref[r * sub:(r + 1) * sub, :].astype(BF16)
            qs.append(jnp.concatenate([qn, qr_ref[r * sub:(r + 1) * sub, :]], axis=1) if rope else qn)

        def step(j, masked):
            ks = pl.ds(pl.multiple_of(j * tk, tk), tk)
            kk = jnp.concatenate([kn_ref[ks, :], kr_ref[ks, :]], axis=1) if rope else kn_ref[ks, :]
            vv = v_ref[ks, :]
            for r in range(nsub):
                rows = slice(r * sub, (r + 1) * sub)
                s = lax.dot_general(qs[r], kk, _DOT_DIMS["nt"], preferred_element_type=F32) * scale
                if masked:
                    s = jnp.where(_mask(qi * tq + r * sub, j * tk, sub, tk), s, NEG)
                m_old = m_s[rows, :]
                m_new = jnp.maximum(m_old, jnp.max(s, axis=-1, keepdims=True))
                p = jnp.exp(s - m_new)
                alpha = jnp.exp(m_old - m_new)
                l_s[rows, :] = alpha * l_s[rows, :] + jnp.sum(p, axis=-1, keepdims=True)
                acc[rows, :] = alpha * acc[rows, :] + jnp.dot(p.astype(BF16), vv, preferred_element_type=F32)
                m_s[rows, :] = m_new

        def unmasked(j, carry):
            step(j, False)
            return carry

        if causal:
            lax.fori_loop(0, qi, unmasked, 0)
            step(qi, True)
        else:
            lax.fori_loop(0, nk, unmasked, 0)
        o_ref[...] = (acc[...] / l_s[...]).astype(o_ref.dtype)
        lse_ref[...] = m_s[...] + jnp.log(l_s[...])

    qspec = lambda c0: pl.BlockSpec((tq, 128), lambda b, h, i: (b * nq + i, c0 + h))
    kspec = lambda c0: pl.BlockSpec((Sk, 128), lambda b, h, i: (b, c0 + kv_stride * h))
    in_specs, args = [qspec(q_c0)], [q]
    if rope:
        in_specs.append(qspec(0)); args.append(qr)
    in_specs.append(kspec(k_c0)); args.append(k)
    if rope:
        in_specs.append(pl.BlockSpec((Sk, 128), lambda b, h, i: (b, 0))); args.append(kr)
    in_specs.append(kspec(v_c0)); args.append(v)
    aliases = {}
    if into is not None:
        aliases = {len(args): 0}
        in_specs.append(pl.BlockSpec(memory_space=pl.ANY)); args.append(into)
        o_shape = jax.ShapeDtypeStruct(into.shape, into.dtype)
    else:
        o_shape = jax.ShapeDtypeStruct((B * Sq, o_width), F32)
    return _call(body, name, [o_shape, jax.ShapeDtypeStruct((B * H, Sq, 1), F32)], grid=(B, H, nq), in_specs=in_specs,
                 out_specs=[qspec(o_c0), pl.BlockSpec((None, tq, 1), lambda b, h, i: (b * H + h, i, 0))],
                 scratch=[pltpu.VMEM((tq, 1), F32), pltpu.VMEM((tq, 1), F32), pltpu.VMEM((tq, 128), F32)],
                 dims=("parallel", "parallel", "arbitrary"), aliases=aliases)(*args)


def _attn_bwd(q, q_c0, qr, k, k_c0, kr, v, v_c0, o, do, o_c0, lse, B, Sq, Sk, H, causal, scale, name, dq_into=None,
              kv_stride=1):
    tq, tk, sub = _attn_shapes(Sq, Sk, causal, "bwd")
    nq, nk, nsub = Sq // tq, Sk // tk, tq // sub
    rope = qr is not None
    dk_w = 256 if rope else 128

    def body(*refs):
        refs = list(refs)
        qn_ref = refs.pop(0)
        qr_ref = refs.pop(0) if rope else None
        kn_ref = refs.pop(0)
        kr_ref = refs.pop(0) if rope else None
        v_ref, o_ref, do_ref, lse_ref = refs[:4]
        refs = refs[4 + (0 if dq_into is None else 1):]
        dqn_ref = refs.pop(0)
        dqr_ref = refs.pop(0) if rope else None
        dkn_ref = refs.pop(0)
        dkr_ref = refs.pop(0) if rope else None
        dv_ref = None if rope else refs.pop(0)
        q_s, do_s, dl_s, dq_acc, dk_acc, dv_acc = refs
        kj = pl.program_id(2)

        @pl.when(kj == 0)
        def _():
            qn = qn_ref[...].astype(BF16)
            q_s[...] = jnp.concatenate([qn, qr_ref[...]], axis=1) if rope else qn
            dof = do_ref[...]
            do_s[...] = dof.astype(BF16)
            dl_s[...] = jnp.sum(dof * o_ref[...], axis=-1, keepdims=True)
            dq_acc[...] = jnp.zeros_like(dq_acc)

        kk = jnp.concatenate([kn_ref[...], kr_ref[...]], axis=1) if rope else kn_ref[...]
        vv = v_ref[...]
        dk_acc[...] = jnp.zeros_like(dk_acc)
        dv_acc[...] = jnp.zeros_like(dv_acc)

        def step(i, masked):
            for r in range(nsub):
                rows = pl.ds(pl.multiple_of(i * tq + r * sub, sub), sub)
                qq, dob = q_s[rows, :], do_s[rows, :]
                s = lax.dot_general(qq, kk, _DOT_DIMS["nt"], preferred_element_type=F32) * scale
                if masked:
                    s = jnp.where(_mask(i * tq + r * sub, kj * tk, sub, tk), s, NEG)
                p = jnp.exp(s - lse_ref[rows, :])
                dp = lax.dot_general(dob, vv, _DOT_DIMS["nt"], preferred_element_type=F32)
                ds = (p * (dp - dl_s[rows, :]) * scale).astype(BF16)
                dv_acc[...] += lax.dot_general(p.astype(BF16), dob, _DOT_DIMS["tn"], preferred_element_type=F32)
                dk_acc[...] += lax.dot_general(ds, qq, _DOT_DIMS["tn"], preferred_element_type=F32)
                dq_acc[rows, :] += jnp.dot(ds, kk, preferred_element_type=F32)

        def unmasked(i, carry):
            step(i, False)
            return carry

        if causal:
            step(kj, True)
            lax.fori_loop(kj + 1, nq, unmasked, 0)
        else:
            lax.fori_loop(0, nq, unmasked, 0)
        if rope:
            dkn_ref[...] = jnp.concatenate([dk_acc[:, 0:128], dv_acc[...]], axis=1).astype(dkn_ref.dtype)
            dkr_ref[...] = dk_acc[:, 128:256]
        else:
            dkn_ref[...] = dk_acc[...]
            dv_ref[...] = dv_acc[...]

        @pl.when(kj == nk - 1)
        def _():
            dqn_ref[...] = dq_acc[:, 0:128].astype(dqn_ref.dtype)
            if rope:
                dqr_ref[...] = dq_acc[:, 128:256]

    qspec = lambda c0: pl.BlockSpec((Sq, 128), lambda b, h, j: (b, c0 + h))
    kspec = lambda c0: pl.BlockSpec((tk, 128), lambda b, h, j: (b * nk + j, c0 + kv_stride * h))
    in_specs, args = [qspec(q_c0)], [q]
    if rope:
        in_specs.append(qspec(0)); args.append(qr)
    in_specs.append(kspec(k_c0)); args.append(k)
    if rope:
        in_specs.append(pl.BlockSpec((tk, 128), lambda b, h, j: (b * nk + j, 0))); args.append(kr)
    in_specs += [kspec(v_c0), qspec(o_c0), qspec(o_c0), pl.BlockSpec((None, Sq, 1), lambda b, h, j: (b * H + h, 0, 0))]
    args += [v, o, do, lse]
    h_rows_q = jax.ShapeDtypeStruct((B * Sq, H * 128), F32)
    h_rows_k = jax.ShapeDtypeStruct((B * Sk, H * 128), F32)
    out_shape, out_specs, aliases = [h_rows_q], [qspec(0)], None
    if rope:
        out_shape = [jax.ShapeDtypeStruct((B * Sq, 2 * H * 128), BF16)]
    if dq_into is not None:
        aliases = {len(args): 0}
        in_specs.append(pl.BlockSpec(memory_space=pl.ANY)); args.append(dq_into[0])
        out_shape, out_specs = [jax.ShapeDtypeStruct(dq_into[0].shape, dq_into[0].dtype)], [qspec(dq_into[1])]
    if rope:
        out_shape.append(h_rows_q); out_specs.append(qspec(0))
    hspec = lambda w: pl.BlockSpec((tk, w), lambda b, h, j: (b * nk + j, h))
    if rope:
        out_shape += [jax.ShapeDtypeStruct((B * Sk, H * 256), BF16), h_rows_k]
        out_specs += [hspec(256), hspec(128)]
    else:
        out_shape += [h_rows_k, h_rows_k]
        out_specs += [hspec(128), hspec(128)]
    return _call(body, name, out_shape, grid=(B, H, nk), in_specs=in_specs, out_specs=out_specs,
                 scratch=[pltpu.VMEM((Sq, dk_w), BF16), pltpu.VMEM((Sq, 128), BF16), pltpu.VMEM((Sq, 1), F32),
                          pltpu.VMEM((Sq, dk_w), F32), pltpu.VMEM((tk, dk_w), F32), pltpu.VMEM((tk, 128), F32)],
                 dims=("parallel", "parallel", "arbitrary"), aliases=aliases)(*args)


def _mem_attention_fwd(proj, q_col, ycat, mem2, mem_g, w_mem, B, S, tag):
    M = mem2.shape[0] // B
    (memn,) = _rowwise(_f_rms, [mem2], [mem_g], [(mem2.shape[1], BF16)], tag + "_memnorm")
    kvm = _mm(memn, w_mem, "nn", BF16, tag + "_memkv")
    o_c0 = ycat.shape[1] // 128 - MEM_HEADS
    ycat, lse = _attn_fwd(proj, q_col // 128, None, kvm, 0, None, kvm, MEM_HEADS, B, S, M, MEM_HEADS, False,
                          MEM_HEAD_DIM ** -0.5, tag + "_memattn", into=ycat, o_c0=o_c0)
    return ycat, (memn, kvm, lse)


def _mem_attention_bwd(proj, q_col, ycat, d_ycat, d_proj, saved, mem2, mem_g, w_mem, B, S, tag):
    memn, kvm, lse = saved
    M = mem2.shape[0] // B
    o_c0 = ycat.shape[1] // 128 - MEM_HEADS
    d_q, d_k, d_v = _attn_bwd(proj, q_col // 128, None, kvm, 0, None, kvm, MEM_HEADS, ycat, d_ycat, o_c0, lse, B, S, M,
                              MEM_HEADS, False, MEM_HEAD_DIM ** -0.5, tag + "_memattn_bwd", dq_into=(d_proj, q_col // 128))
    d_kvm = jnp.concatenate([d_k, d_v], axis=1).astype(BF16)
    d_w_mem = _mm(memn, d_kvm, "tn", F32, tag + "_memkv_dw")
    d_memn = _mm(d_kvm, w_mem, "nt", F32, tag + "_memkv_dx")
    _, d_mem_g = _rowwise_bwd(_f_rms, [mem2], [mem_g], [d_memn], 1, tag + "_memnorm_bwd")
    return d_q, d_w_mem, d_mem_g


def _rope_tables(positions):
    inv_freq = 1.0 / (ROPE_THETA ** (jnp.arange(0, MLA_ROPE, 2, dtype=F32) / MLA_ROPE))
    ang = positions.astype(F32).reshape(-1, 1) * inv_freq
    cos, sin, zero = jnp.cos(ang), jnp.sin(ang), jnp.zeros_like(ang)
    return jnp.concatenate([cos, zero, cos, zero], axis=1), jnp.concatenate([-sin, zero, sin, zero], axis=1)


def _forward_backward(x, mem, positions, target, W):
    B, S, D = x.shape
    T = B * S
    conv_w = W["conv_dw"].shape[1]
    mix_w = 2 * D
    h0 = x.reshape(T, D)
    mem2 = mem.reshape(-1, D)
    tgt = target.reshape(T, D)
    row = lambda v: v.reshape(1, -1)
    n_nope = MLA_HEADS * MLA_NOPE

    g0 = row(W["norm_g"][0])
    (u0,) = _rowwise(_f_rms, [h0], [g0], [(D, BF16)], "l0_norm", carry=W.carry("l0_norm"))
    proj0 = _mm(u0, W["conv_w_in"], "nn", F32, "l0_in", carry=W.carry("l0_in"))
    a0, gate0 = (proj0, conv_w, 0), (proj0, conv_w, 1)
    qm0_col, z0_col = 2 * conv_w, 2 * conv_w + MEM_WIDTH
    (glu,) = _rowwise(_f_glu, [a0, gate0], [], [(conv_w, F32)], "l0_glu", carry=W.carry("l0_glu"))
    dw, dwb = W["conv_dw"], row(W["conv_dw_b"][0])
    cv = _dwconv_fwd(glu.reshape(B, S, conv_w), dw, dwb, "l0_dwconv", carry=W.carry("l0_dwconv")).reshape(T, conv_w)
    ln_g, ln_b = row(W["conv_ln_g"][0]), row(W["conv_ln_b"][0])
    (ycat0,) = _rowwise(_f_ln_silu, [cv], [ln_g, ln_b], [(conv_w, F32, mix_w)], "l0_ln", carry=W.carry("l0_ln"))
    mg0 = row(W["mem_norm_g"][0])
    ycat0, mem_saved0 = _mem_attention_fwd(proj0, qm0_col, ycat0, mem2, mg0, W["w_mem_kv"][0], B, S, "l0")
    y0 = _gate_fwd(ycat0, proj0, z0_col, "l0_gate")
    h1 = _mm(y0, W["w_out"][0], "nn", F32, "l0_out", res=h0)

    g1 = row(W["norm_g"][1])
    (u1,) = _rowwise(_f_rms, [h1], [g1], [(D, BF16)], "l1_norm")
    proj1 = _mm(u1, W["mla_w_in"], "nn", F32, "l1_in")
    cq, ckv = (proj1, Q_RANK, 0), (proj1, KV_RANK, Q_RANK // KV_RANK)
    qm1_col = Q_RANK + KV_RANK
    z1_col = qm1_col + MEM_WIDTH
    kr_col = z1_col + mix_w
    qg, kvg = row(W["mla_q_norm_g"]), row(W["mla_kv_norm_g"])
    (cqn,) = _rowwise(_f_rms, [cq], [qg], [(Q_RANK, BF16)], "l1_qnorm")
    (ckvn,) = _rowwise(_f_rms, [ckv], [kvg], [(KV_RANK, BF16)], "l1_kvnorm")
    qf = _mm(cqn, W["mla_w_uq"], "nn", F32, "l1_uq")
    kvf = _mm(ckvn, W["mla_w_ukv"], "nn", BF16, "l1_ukv")
    cos_p, sin_p = _rope_tables(positions)
    qr, kr = _rowwise(_f_rope, [(qf, n_nope, 1), (proj1, 128, kr_col // 128), cos_p, sin_p], [],
                      [(n_nope, BF16), (128, BF16)], "l1_rope")
    scale1 = MLA_QK ** -0.5
    ycat1, lse1 = _attn_fwd(qf, 0, qr, kvf, 0, kr, kvf, 1, B, S, S, MLA_HEADS, True, scale1, "l1_attn",
                            o_width=mix_w, kv_stride=2)
    mg1 = row(W["mem_norm_g"][1])
    ycat1, mem_saved1 = _mem_attention_fwd(proj1, qm1_col, ycat1, mem2, mg1, W["w_mem_kv"][1], B, S, "l1")
    y1 = _gate_fwd(ycat1, proj1, z1_col, "l1_gate")
    h2 = _mm(y1, W["w_out"][1], "nn", F32, "l1_out", res=h1)

    gf = row(W["final_norm_g"])
    dh2, d_gf, loss128 = _final_loss(h2, tgt, gf, "final_loss")
    G = {"final_norm_g": d_gf.reshape(-1)}
    L1 = {}

    dy1 = _mm(dh2, W["w_out"][1], "nt", F32, "l1_out_dx")
    d_wout1 = _mm(y1, dh2, "tn", F32, "l1_out_dw")
    d_ycat1, d_proj1 = _gate_bwd(ycat1, proj1, z1_col, dy1, "l1_gate_bwd")
    d_proj1, d_wmem1, d_mg1 = _mem_attention_bwd(proj1, qm1_col, ycat1, d_ycat1, d_proj1, mem_saved1, mem2, mg1,
                                                 W["w_mem_kv"][1], B, S, "l1")
    d_qf, d_qr, d_kvf, d_kr_heads = _attn_bwd(qf, 0, qr, kvf, 0, kr, kvf, 1, ycat1, d_ycat1, 0, lse1, B, S, S,
                                              MLA_HEADS, True, scale1, "l1_attn_bwd", kv_stride=2)
    d_qf, d_proj1 = _rowwise(_f_rope_t, [d_qr, d_kr_heads, cos_p, sin_p], [], [(n_nope, F32), (128, F32)], "l1_rope_bwd",
                             into=[(0, d_qf, 1), (1, d_proj1, kr_col // 128)])
    d_cqn = _mm(d_qf, W["mla_w_uq"], "nt", F32, "l1_uq_dx")
    L1[("mla_w_uq", None)] = _mm(cqn, d_qf, "tn", F32, "l1_uq_dw")
    d_ckvn = _mm(d_kvf, W["mla_w_ukv"], "nt", F32, "l1_ukv_dx")
    L1[("mla_w_ukv", None)] = _mm(ckvn, d_kvf, "tn", F32, "l1_ukv_dw")
    d_proj1, d_qg = _rowwise_bwd(_f_rms, [cq], [qg], [d_cqn], 1, "l1_qnorm_bwd", into=(d_proj1, cq[2]))
    d_proj1, d_kvg = _rowwise_bwd(_f_rms, [ckv], [kvg], [d_ckvn], 1, "l1_kvnorm_bwd", into=(d_proj1, ckv[2]))
    L1[("w_mem_kv", 1)] = d_wmem1
    L1[("mla_w_in", None)] = _mm(u1, d_proj1, "tn", F32, "l1_in_dw")
    L1[("w_out", 1)] = d_wout1
    W.ready("l1", L1)
    d_u1 = _mm(d_proj1, W["mla_w_in"], "nt", F32, "l1_in_dx", carry=W.carry("l1_in_dx"))
    dh1, d_g1 = _rowwise_bwd(_f_rms, [h1], [g1], [d_u1], 1, "l1_norm_bwd", add=dh2)

    dy0 = _mm(dh1, W["w_out"][0], "nt", F32, "l0_out_dx")
    d_wout0 = _mm(y0, dh1, "tn", F32, "l0_out_dw")
    d_ycat0, d_proj0 = _gate_bwd(ycat0, proj0, z0_col, dy0, "l0_gate_bwd", carry=W.carry("l0_gate_bwd"))
    d_proj0, d_wmem0, d_mg0 = _mem_attention_bwd(proj0, qm0_col, ycat0, d_ycat0, d_proj0, mem_saved0, mem2, mg0,
                                                 W["w_mem_kv"][0], B, S, "l0")
    W.ready("l0a", {("w_mem_kv", 0): d_wmem0, ("w_out", 0): d_wout0})
    d_cv, d_ln_g, d_ln_b = _rowwise_bwd(_f_ln_silu, [cv], [ln_g, ln_b], [(d_ycat0, conv_w, 0)], 1, "l0_ln_bwd",
                                        carry=W.carry("l0_ln_bwd"))
    d_glu, d_dw, d_dwb = _dwconv_bwd(glu.reshape(B, S, conv_w), dw, d_cv.reshape(B, S, conv_w), "l0_dwconv_bwd",
                                     carry=W.carry("l0_dwconv_bwd"))
    d_proj0 = _glu_bwd(proj0, d_glu.reshape(T, conv_w), d_proj0, "l0_glu_bwd")
    d_conv_w_in = _mm(u0, d_proj0, "tn", F32, "l0_in_dw", carry=W.carry("l0_in_dw"))
    W.ready("l0b", {("conv_w_in", None): d_conv_w_in, ("conv_dw", None): d_dw,
                    ("mla_q_norm_g", None): d_qg.reshape(-1), ("mla_kv_norm_g", None): d_kvg.reshape(-1)})
    d_u0 = _mm(d_proj0, W["conv_w_in"], "nt", F32, "l0_in_dx", carry=W.carry("l0_in_dx"))
    dx, d_g0 = _rowwise_bwd(_f_rms, [h0], [g0], [d_u0], 1, "l0_norm_bwd", add=dh1)
    dx = dx.reshape(B, S, D)

    G["norm_g"] = jnp.concatenate([d_g0, d_g1], axis=0)
    G["mem_norm_g"] = jnp.concatenate([d_mg0, d_mg1], axis=0)
    G["conv_dw_b"] = d_dwb
    G["conv_ln_g"], G["conv_ln_b"] = d_ln_g, d_ln_b
    return loss128[0, 0], dx, G


def _mla_in_perm(w):
    c2 = Q_RANK + KV_RANK
    zero = jnp.zeros((w.shape[0], HALF_ROPE), w.dtype)
    return jnp.concatenate([w[:, :c2], w[:, c2 + MLA_ROPE:], w[:, c2:c2 + HALF_ROPE], zero,
                            w[:, c2 + HALF_ROPE:c2 + MLA_ROPE], zero], axis=1)


def _mla_in_unperm(g):
    c2 = Q_RANK + KV_RANK
    r = g.shape[1] - 128
    return jnp.concatenate([g[:, :c2], g[:, r:r + HALF_ROPE], g[:, r + 64:r + 64 + HALF_ROPE], g[:, c2:r]], axis=1)


def _uq_perm(w):
    n = w.shape[0]
    w3 = w.reshape(n, MLA_HEADS, MLA_QK)
    zero = jnp.zeros((n, MLA_HEADS, HALF_ROPE), w.dtype)
    rope = jnp.concatenate([w3[:, :, MLA_NOPE:MLA_NOPE + HALF_ROPE], zero, w3[:, :, MLA_NOPE + HALF_ROPE:], zero], axis=2)
    return jnp.concatenate([w3[:, :, :MLA_NOPE].reshape(n, -1), rope.reshape(n, -1)], axis=1)


def _uq_unperm(g):
    n = g.shape[0]
    n_nope = MLA_HEADS * MLA_NOPE
    rope = g[:, n_nope:].reshape(n, MLA_HEADS, 128)
    return jnp.concatenate([g[:, :n_nope].reshape(n, MLA_HEADS, MLA_NOPE), rope[:, :, :HALF_ROPE],
                            rope[:, :, 64:64 + HALF_ROPE]], axis=2).reshape(n, -1)


_ROW_CUT = ("w_mem_kv", "w_out")
_COL_CUT = ("conv_w_in", "mla_w_in", "mla_w_uq", "mla_w_ukv", "conv_dw")
_BIG = ("w_mem_kv", "w_out", "conv_w_in", "mla_w_in", "mla_w_uq", "mla_w_ukv")
_SMALL_SHARDED = ("conv_dw", "mla_q_norm_g", "mla_kv_norm_g")
_REPLICATED = ("norm_g", "mem_norm_g", "conv_dw_b", "conv_ln_g", "conv_ln_b", "final_norm_g")
_PERM = {"mla_w_in": (_mla_in_perm, _mla_in_unperm), "mla_w_uq": (_uq_perm, _uq_unperm)}


def _join(n, blocks):
    if n in _ROW_CUT:
        _, L, r, c = blocks.shape
        return blocks.transpose(1, 0, 2, 3).reshape(L, N_DEV * r, c)
    if n in _COL_CUT:
        _, _, r, c = blocks.shape
        return blocks.reshape(N_DEV, r, c).transpose(1, 0, 2).reshape(r, N_DEV * c)
    return blocks.reshape(-1)


def _cut(n, full, shard_shape):
    if n in _ROW_CUT:
        L, r, c = shard_shape
        return full.reshape(L, N_DEV, r, c).transpose(1, 0, 2, 3)
    if n in _COL_CUT:
        _, r, c = shard_shape
        return full.reshape(r, N_DEV, c).transpose(1, 0, 2).reshape(N_DEV, 1, r, c)
    return full.reshape(N_DEV, 1, -1)


def _flat_pad(parts, size):
    flat = jnp.concatenate([p.reshape(-1) for p in parts])
    return jnp.concatenate([flat, jnp.zeros((size - flat.shape[0],), flat.dtype)])


SMALL_LANES = 128 * 8


def _as_tiles(flat_parts):
    total = sum(p.size for p in flat_parts)
    size = -(-total // SMALL_LANES) * SMALL_LANES
    return _flat_pad(flat_parts, size).reshape(8, size // 8)


def _split_flat(flat, like):
    out, o = [], 0
    for a in like:
        out.append(flat[o:o + a.size].reshape(a.shape))
        o += a.size
    return out


_HBM = pl.BlockSpec(memory_space=pltpu.HBM)
_VMEM = pl.BlockSpec(memory_space=pltpu.VMEM)


def _position():
    return lax.axis_index("x"), lax.axis_index("y"), lax.axis_index("c")


def _dma_sems(n):
    return [pltpu.SemaphoreType.DMA((n,)), pltpu.SemaphoreType.DMA((n,))]


def _run_stage(stage, name):
    n_in, n_out = len(stage.ins), len(stage.out_shapes)

    def body(*refs):
        ins, outs, sems = refs[:n_in], refs[n_in:n_in + n_out], refs[n_in + n_out:]
        stage.start(ins, outs, sems)
        stage.wait(ins, outs, sems)

    outs = _call(body, name, stage.out_shapes, in_specs=[_HBM] * n_in, out_specs=[_HBM] * n_out, scratch=stage.sems,
                 aliases=stage.aliases)(*stage.ins)
    stage.outs = list(outs)
    return stage.outs


def _gather_chips_stage(shards):
    n = len(shards)

    def copies(x_refs, out_refs, sems):
        send_sems, recv_sems, _ = sems
        x, y, c = _position()
        peers = [(x, y, 1 - c), (1 - x, y, c), (x, 1 - y, c), (1 - x, 1 - y, c)]
        out = []
        for a in range(n):
            for k, (px, py, pc) in enumerate(peers):
                send = pltpu.make_async_remote_copy(src_ref=x_refs[a], dst_ref=out_refs[a].at[4 * x + 2 * y + c],
                                                    send_sem=send_sems.at[4 * a + k], recv_sem=recv_sems.at[4 * a + k],
                                                    device_id=(px, py, pc), device_id_type=MESH)
                recv = pltpu.make_async_remote_copy(src_ref=x_refs[a], dst_ref=out_refs[a].at[4 * px + 2 * py + pc],
                                                    send_sem=send_sems.at[4 * a + k], recv_sem=recv_sems.at[4 * a + k],
                                                    device_id=(px, py, pc), device_id_type=MESH)
                out.append((send, recv))
        return out

    def local(x_refs, out_refs, sems):
        x, y, c = _position()
        return [pltpu.make_async_copy(x_refs[a], out_refs[a].at[4 * x + 2 * y + c], sems[2].at[a]) for a in range(n)]

    def start(x_refs, out_refs, sems):
        for cp in local(x_refs, out_refs, sems):
            cp.start()
        for send, _ in copies(x_refs, out_refs, sems):
            send.start()

    def wait(x_refs, out_refs, sems):
        for send, recv in copies(x_refs, out_refs, sems):
            recv.wait_recv()
            send.wait_send()
        for cp in local(x_refs, out_refs, sems):
            cp.wait()

    return _Stage(shards, [jax.ShapeDtypeStruct((N_DEV,) + a.shape, a.dtype) for a in shards],
                  _dma_sems(4 * n) + [pltpu.SemaphoreType.DMA((n,))], start, wait)


def _gather_sibling_stage(bufs):
    n = len(bufs)

    def copies(out_refs, sems):
        send_sems, recv_sems = sems
        x, y, c = _position()
        out = []
        for a in range(n):
            for j, (px, py) in enumerate([(1 - x, y), (x, 1 - y), (1 - x, 1 - y)]):
                mine, theirs = out_refs[a].at[4 * px + 2 * py + c], out_refs[a].at[4 * px + 2 * py + (1 - c)]
                send = pltpu.make_async_remote_copy(src_ref=mine, dst_ref=mine, send_sem=send_sems.at[3 * a + j],
                                                    recv_sem=recv_sems.at[3 * a + j], device_id=(x, y, 1 - c),
                                                    device_id_type=MESH)
                recv = pltpu.make_async_remote_copy(src_ref=mine, dst_ref=theirs, send_sem=send_sems.at[3 * a + j],
                                                    recv_sem=recv_sems.at[3 * a + j], device_id=(x, y, 1 - c),
                                                    device_id_type=MESH)
                out.append((send, recv))
        return out

    def start(_, out_refs, sems):
        for send, _r in copies(out_refs, sems):
            send.start()

    def wait(_, out_refs, sems):
        for send, recv in copies(out_refs, sems):
            recv.wait_recv()
            send.wait_send()

    return _Stage(bufs, [jax.ShapeDtypeStruct(b.shape, b.dtype) for b in bufs], _dma_sems(3 * n), start, wait,
                  aliases={a: a for a in range(n)})


def _all_gather_small(v, name):
    r, n = v.shape

    def body(x_ref, out_ref, send_sems, recv_sems, local_sem):
        x, y, c = _position()
        me = 4 * x + 2 * y + c
        mine = pltpu.make_async_copy(x_ref, out_ref.at[me], local_sem)
        mine.start()
        flips = [(fx, fy, fc) for fx in (0, 1) for fy in (0, 1) for fc in (0, 1)][1:]
        copies = []
        for k, (fx, fy, fc) in enumerate(flips):
            peer = (x ^ fx, y ^ fy, c ^ fc)
            cp = pltpu.make_async_remote_copy(src_ref=x_ref, dst_ref=out_ref.at[me], send_sem=send_sems.at[k],
                                              recv_sem=recv_sems.at[k], device_id=peer, device_id_type=MESH)
            cp.start()
            copies.append(cp)
        for k, (fx, fy, fc) in enumerate(flips):
            px, py, pc = x ^ fx, y ^ fy, c ^ fc
            src = out_ref.at[4 * px + 2 * py + pc]
            pltpu.make_async_remote_copy(src_ref=x_ref, dst_ref=src, send_sem=send_sems.at[k], recv_sem=recv_sems.at[k],
                                         device_id=(px, py, pc), device_id_type=MESH).wait_recv()
        for cp in copies:
            cp.wait_send()
        mine.wait()

    return _call(body, name, jax.ShapeDtypeStruct((N_DEV, r, n), v.dtype), in_specs=[_VMEM], out_specs=_VMEM,
                 scratch=_dma_sems(7) + [pltpu.SemaphoreType.DMA(())])(v)


def _reduce_sibling_stage(gs):
    n = len(gs)

    def copies(g_refs, out_refs, sems):
        send_sems, recv_sems = sems
        x, y, c = _position()
        return [pltpu.make_async_remote_copy(src_ref=g_refs[a].at[2 * k + (1 - c)], dst_ref=out_refs[a].at[k],
                                             send_sem=send_sems.at[4 * a + k], recv_sem=recv_sems.at[4 * a + k],
                                             device_id=(x, y, 1 - c), device_id_type=MESH)
                for a in range(n) for k in range(4)]

    def start(g_refs, out_refs, sems):
        for cp in copies(g_refs, out_refs, sems):
            cp.start()

    def wait(g_refs, out_refs, sems):
        for cp in copies(g_refs, out_refs, sems):
            cp.wait()

    return _Stage(gs, [jax.ShapeDtypeStruct((4,) + g.shape[1:], g.dtype) for g in gs], _dma_sems(4 * n), start, wait)


def _rows2d(shape):
    cols = shape[-1]
    rows = 1
    for s in shape[:-1]:
        rows *= s
    return rows, cols


def _add_own(g, recv, name):
    rows, cols = _rows2d(g.shape[1:])
    tr = _pick(rows, 256, 8)
    c = lax.axis_index("c").astype(jnp.int32).reshape(1)

    def body(c_ref, g_ref, r_ref, o_ref):
        o_ref[...] = g_ref[...] + r_ref[...]

    grid_spec = pltpu.PrefetchScalarGridSpec(
        num_scalar_prefetch=1, grid=(4, rows // tr),
        in_specs=[pl.BlockSpec((None, None, tr, cols), lambda k, i, c_ref: (k, c_ref[0], i, 0)),
                  pl.BlockSpec((None, tr, cols), lambda k, i, c_ref: (k, i, 0))],
        out_specs=pl.BlockSpec((None, tr, cols), lambda k, i, c_ref: (k, i, 0)))
    return _call(body, name, jax.ShapeDtypeStruct((4, rows, cols), F32), grid_spec=grid_spec,
                 dims=("parallel", "parallel"))(c, g.reshape(4, 2, rows, cols), recv.reshape(4, rows, cols))


def _reduce_chips_stage(pas):
    n = len(pas)

    def copies(pa_refs, out_refs, sems):
        send_sems, recv_sems, _ = sems
        x, y, c = _position()
        my_chip = 2 * x + y
        out = []
        for a in range(n):
            for j, (px, py) in enumerate([(1 - x, y), (x, 1 - y), (1 - x, 1 - y)]):
                send = pltpu.make_async_remote_copy(src_ref=pa_refs[a].at[2 * px + py], dst_ref=out_refs[a].at[my_chip],
                                                    send_sem=send_sems.at[3 * a + j], recv_sem=recv_sems.at[3 * a + j],
                                                    device_id=(px, py, c), device_id_type=MESH)
                recv = pltpu.make_async_remote_copy(src_ref=pa_refs[a].at[2 * px + py], dst_ref=out_refs[a].at[2 * px + py],
                                                    send_sem=send_sems.at[3 * a + j], recv_sem=recv_sems.at[3 * a + j],
                                                    device_id=(px, py, c), device_id_type=MESH)
                out.append((send, recv))
        return out

    def local(pa_refs, out_refs, sems):
        x, y, _ = _position()
        return [pltpu.make_async_copy(pa_refs[a].at[2 * x + y], out_refs[a].at[2 * x + y], sems[2].at[a]) for a in range(n)]

    def start(pa_refs, out_refs, sems):
        for cp in local(pa_refs, out_refs, sems):
            cp.start()
        for send, _r in copies(pa_refs, out_refs, sems):
            send.start()

    def wait(pa_refs, out_refs, sems):
        for send, recv in copies(pa_refs, out_refs, sems):
            recv.wait_recv()
            send.wait_send()
        for cp in local(pa_refs, out_refs, sems):
            cp.wait()

    return _Stage(pas, [jax.ShapeDtypeStruct(pa.shape, pa.dtype) for pa in pas],
                  _dma_sems(3 * n) + [pltpu.SemaphoreType.DMA((n,))], start, wait)


def _adamw_math(w, g, m, v):
    m = ADAM_B1 * m + (1.0 - ADAM_B1) * g
    v = ADAM_B2 * v + (1.0 - ADAM_B2) * (g * g)
    m_hat = m / (1.0 - ADAM_B1 ** ADAM_STEP)
    v_hat = v / (1.0 - ADAM_B2 ** ADAM_STEP)
    delta = -ADAM_LR * (m_hat / (jnp.sqrt(v_hat) + ADAM_EPS) + ADAM_WD * w)
    return delta, m, v


def _sum_adamw(parts, w, m, v, name):
    n, rows, cols = parts.shape
    tr = _pick(rows, 128, 8)

    def body(p_ref, w_ref, m_ref, v_ref, g_ref, d_ref, nm_ref, nv_ref):
        g = p_ref[0]
        for k in range(1, n):
            g = g + p_ref[k]
        d, nm, nv = _adamw_math(w_ref[...], g, m_ref[...], v_ref[...])
        g_ref[...], d_ref[...], nm_ref[...], nv_ref[...] = g, d, nm, nv

    blk = pl.BlockSpec((tr, cols), lambda i: (i, 0))
    return _call(body, name, [jax.ShapeDtypeStruct((rows, cols), F32)] * 4, grid=(rows // tr,),
                 in_specs=[pl.BlockSpec((n, tr, cols), lambda i: (0, i, 0)), blk, blk, blk],
                 out_specs=[blk] * 4, dims=("parallel",))(parts, w, m, v)


_WEIGHTS = ("norm_g", "mem_norm_g", "w_mem_kv", "w_out", "conv_w_in", "conv_dw", "conv_dw_b", "conv_ln_g", "conv_ln_b",
            "mla_w_in", "mla_q_norm_g", "mla_w_uq", "mla_kv_norm_g", "mla_w_ukv", "final_norm_g")


_GATHER_GROUPS = {"a": ("conv_w_in",), "b": ("w_mem_kv", "w_out"), "c": ("mla_w_in", "mla_w_uq", "mla_w_ukv")}
_CARRIERS = {"l0_norm": ("gather chips", ("a",)), "l0_in": ("gather chips", ("b",)), "l0_glu": ("gather sibling", ("b",)),
             "l0_dwconv": ("gather chips", ("c",)), "l0_ln": ("gather sibling", ("c",)),
             "l1_in_dx": ("reduce sibling", ("l1",)), "l0_ln_bwd": ("reduce sibling", ("l0a",)),
             "l0_gate_bwd": ("reduce chips", ("l1", 0, 3)), "l0_dwconv_bwd": ("reduce chips", ("l1", 3, 5)),
             "l0_in_dw": ("reduce chips", ("l0a",)), "l0_in_dx": ("reduce sibling alone, then chips", ("l0b",))}


class _Schedule:
    def __init__(self, w):
        self.w, self.full, self.gather, self.reduce, self.reduced = w, {}, {}, {}, {}
        small = _all_gather_small(_as_tiles([w[n] for n in _SMALL_SHARDED]), "gather_small_weights").reshape(N_DEV, -1)
        o = 0
        for n in _SMALL_SHARDED:
            self.full[n] = _join(n, small[:, o:o + w[n].size].reshape((N_DEV,) + w[n].shape))
            o += w[n].size
        for n in _REPLICATED:
            self.full[n] = w[n]

    def carry(self, call):
        kind, (g, *part) = _CARRIERS[call]
        if kind == "gather chips":
            self.gather[g] = [_gather_chips_stage([self.w[n].astype(BF16) for n in _GATHER_GROUPS[g]])]
            return self.gather[g][0]
        if kind == "gather sibling":
            self.gather[g].append(_gather_sibling_stage(self.gather[g][0].outs))
            return self.gather[g][1]
        r = self.reduce[g]
        if kind == "reduce sibling":
            r["sibling"] = _reduce_sibling_stage(r["cut"])
            return r["sibling"]
        if kind != "reduce chips":
            r["sibling"] = _reduce_sibling_stage(r["cut"])
            _run_stage(r["sibling"], "reduce_sibling_" + g)
        if "partial" not in r:
            r["partial"] = [_add_own(c, s, "reduce_add_%s_%d" % (g, i))
                            for i, (c, s) in enumerate(zip(r["cut"], r["sibling"].outs))]
        lo, hi = part if part else (0, len(r["keys"]))
        stage = _reduce_chips_stage(r["partial"][lo:hi])
        r.setdefault("chips", []).append((r["keys"][lo:hi], stage))
        return stage

    def __getitem__(self, name):
        if name not in self.full:
            g = [k for k, names in _GATHER_GROUPS.items() if name in names][0]
            if len(self.gather[g]) == 1:
                self.gather[g].append(_gather_sibling_stage(self.gather[g][0].outs))
                _run_stage(self.gather[g][1], "gather_sibling_" + g)
            for n, buf in zip(_GATHER_GROUPS[g], self.gather[g][1].outs):
                self.full[n] = _PERM[n][0](_join(n, buf)) if n in _PERM else _join(n, buf)
        return self.full[name]

    def ready(self, group, grads):
        keys, cut, small = [], [], []
        for (n, layer), g in grads.items():
            if n in _SMALL_SHARDED:
                small.append(_cut(n, g, self.w[n].shape).reshape(N_DEV, -1))
                continue
            keys.append((n, layer))
            if layer is not None:
                cut.append(g.reshape((N_DEV,) + self.w[n].shape[1:]))
            else:
                cut.append(_cut(n, _PERM[n][1](g) if n in _PERM else g, self.w[n].shape))
        if small:
            keys.append(("small", None))
            cut.append(jax.vmap(lambda r: _as_tiles([r]))(jnp.concatenate(small, axis=1)))
        self.reduce[group] = {"keys": keys, "cut": cut}

    def finish(self):
        out = {}
        for r in self.reduce.values():
            for keys, stage in r["chips"]:
                out.update(dict(zip(keys, stage.outs)))
        return out


def kernel(x, mem, positions, norm_g, mem_norm_g, w_mem_kv, w_out, conv_w_in, conv_dw, conv_dw_b, conv_ln_g, conv_ln_b, mla_w_in, mla_q_norm_g, mla_w_uq, mla_kv_norm_g, mla_w_ukv, final_norm_g, loss_target, m_norm_g, m_mem_norm_g, m_w_mem_kv, m_w_out, m_conv_w_in, m_conv_dw, m_conv_dw_b, m_conv_ln_g, m_conv_ln_b, m_mla_w_in, m_mla_q_norm_g, m_mla_w_uq, m_mla_kv_norm_g, m_mla_w_ukv, m_final_norm_g, v_norm_g, v_mem_norm_g, v_w_mem_kv, v_w_out, v_conv_w_in, v_conv_dw, v_conv_dw_b, v_conv_ln_g, v_conv_ln_b, v_mla_w_in, v_mla_q_norm_g, v_mla_w_uq, v_mla_kv_norm_g, v_mla_w_ukv, v_final_norm_g):
    w = dict(zip(_WEIGHTS, (norm_g, mem_norm_g, w_mem_kv, w_out, conv_w_in, conv_dw, conv_dw_b, conv_ln_g, conv_ln_b,
                            mla_w_in, mla_q_norm_g, mla_w_uq, mla_kv_norm_g, mla_w_ukv, final_norm_g)))
    m = dict(zip(_WEIGHTS, (m_norm_g, m_mem_norm_g, m_w_mem_kv, m_w_out, m_conv_w_in, m_conv_dw, m_conv_dw_b, m_conv_ln_g,
                            m_conv_ln_b, m_mla_w_in, m_mla_q_norm_g, m_mla_w_uq, m_mla_kv_norm_g, m_mla_w_ukv, m_final_norm_g)))
    v = dict(zip(_WEIGHTS, (v_norm_g, v_mem_norm_g, v_w_mem_kv, v_w_out, v_conv_w_in, v_conv_dw, v_conv_dw_b, v_conv_ln_g,
                            v_conv_ln_b, v_mla_w_in, v_mla_q_norm_g, v_mla_w_uq, v_mla_kv_norm_g, v_mla_w_ukv, v_final_norm_g)))

    sched = _Schedule(w)
    loss_local, dx, G = _forward_backward(x, mem, positions, loss_target, sched)
    loss = lax.psum(loss_local, ("x", "y", "c"))

    from_chips = sched.finish()
    out = [{}, {}, {}, {}]
    for n in _BIG:
        if n in _ROW_CUT:
            res = [_sum_adamw(from_chips[(n, l)], w[n][l], m[n][l], v[n][l], "adamw_%s_%d" % (n, l)) for l in range(w[n].shape[0])]
            res = [jnp.stack(r) for r in zip(*res)]
        else:
            rows, cols = _rows2d(w[n].shape)
            res = _sum_adamw(from_chips[(n, None)], w[n].reshape(rows, cols), m[n].reshape(rows, cols),
                             v[n].reshape(rows, cols), "adamw_" + n)
        for o, r in zip(out, res):
            o[n] = r.reshape(w[n].shape)
    small_like = [w[n] for n in _SMALL_SHARDED]
    res = _sum_adamw(from_chips[("small", None)], _as_tiles(small_like), _as_tiles([m[n] for n in _SMALL_SHARDED]),
                     _as_tiles([v[n] for n in _SMALL_SHARDED]), "adamw_small")
    for o, r in zip(out, res):
        for n, a in zip(_SMALL_SHARDED, _split_flat(r.reshape(-1), small_like)):
            o[n] = a

    rep_like = [w[n] for n in _REPLICATED]
    rep_parts = _all_gather_small(_as_tiles([G[n] for n in _REPLICATED]), "gather_replicated_grads")
    res = _sum_adamw(rep_parts, _as_tiles(rep_like), _as_tiles([m[n] for n in _REPLICATED]),
                     _as_tiles([v[n] for n in _REPLICATED]), "adamw_replicated")
    for o, r in zip(out, res):
        for n, a in zip(_REPLICATED, _split_flat(r.reshape(-1), rep_like)):
            o[n] = a

    return (loss, dx, *[out[0][n] for n in _WEIGHTS], *[out[1][n] for n in _WEIGHTS],
            *[out[2][n] for n in _WEIGHTS], *[out[3][n] for n in _WEIGHTS])
```

```python
import jax
import jax.numpy as jnp
from jax import lax
from jax.experimental import pallas as pl
from jax.experimental.pallas import tpu as pltpu

F32 = jnp.float32
BF16 = jnp.bfloat16
MESH = pl.DeviceIdType.MESH
N_DEV = 8
VMEM_LIMIT_BYTES = 48 * 1024 * 1024

MEM_HEADS, MEM_HEAD_DIM = 4, 128
MEM_WIDTH = MEM_HEADS * MEM_HEAD_DIM
CONV_KERNEL = 31
CONV_PAD = 32
MLA_HEADS, MLA_NOPE, MLA_ROPE, MLA_V = 12, 128, 64, 128
MLA_QK = MLA_NOPE + MLA_ROPE
HALF_ROPE = MLA_ROPE // 2
Q_RANK, KV_RANK = 512, 256
ROPE_THETA = 10000.0
RMS_EPS = 1e-6
LN_EPS = 1e-5
ADAM_LR, ADAM_B1, ADAM_B2, ADAM_EPS, ADAM_WD, ADAM_STEP = 0.001, 0.9, 0.999, 1e-08, 0.01, 10
NEG = -1e30


class _Stage:
    def __init__(self, ins, out_shapes, sems, start, wait, aliases=None):
        self.ins, self.out_shapes, self.sems = list(ins), list(out_shapes), list(sems)
        self.start, self.wait, self.aliases, self.outs = start, wait, dict(aliases or {}), None


def _call(body, name, out_shape, grid=None, in_specs=None, out_specs=None, scratch=(), dims=None, grid_spec=None, aliases=None,
          carry=None):
    params = dict(vmem_limit_bytes=VMEM_LIMIT_BYTES)
    if dims is not None:
        params["dimension_semantics"] = dims
    kw = {}
    if carry is not None:
        single = not isinstance(out_shape, (list, tuple))
        main_out = [out_shape] if single else list(out_shape)
        main_specs = [out_specs] if single else list(out_specs)
        n_in, n_out, n_scr = len(in_specs), len(main_out), len(scratch)
        x_in, x_out = len(carry.ins), len(carry.out_shapes)
        inner, steps = body, tuple(grid)

        def body(*refs):
            ins, xin = refs[:n_in], refs[n_in:n_in + x_in]
            outs = refs[n_in + x_in:n_in + x_in + n_out]
            xout = refs[n_in + x_in + n_out:n_in + x_in + n_out + x_out]
            scr = refs[n_in + x_in + n_out + x_out:n_in + x_in + n_out + x_out + n_scr]
            xsem = refs[n_in + x_in + n_out + x_out + n_scr:]
            ids = [pl.program_id(a) for a in range(len(steps))]
            first, last = ids[0] == 0, ids[0] == steps[0] - 1
            for a in range(1, len(steps)):
                first = jnp.logical_and(first, ids[a] == 0)
                last = jnp.logical_and(last, ids[a] == steps[a] - 1)
            pl.when(first)(lambda: carry.start(xin, xout, xsem))
            inner(*ins, *outs, *scr)
            pl.when(last)(lambda: carry.wait(xin, xout, xsem))

        hbm = pl.BlockSpec(memory_space=pltpu.HBM)
        aliases = dict(aliases or {})
        aliases.update({n_in + k: n_out + v for k, v in carry.aliases.items()})
        res = _call(body, name, main_out + carry.out_shapes, grid=grid, in_specs=list(in_specs) + [hbm] * x_in,
                    out_specs=main_specs + [hbm] * x_out, scratch=list(scratch) + carry.sems, dims=dims, aliases=aliases)

        def run(*args):
            outs = res(*args, *carry.ins)
            carry.outs = list(outs[n_out:])
            return outs[0] if single else outs[:n_out]

        return run
    if aliases:
        kw["input_output_aliases"] = aliases
    if grid_spec is not None:
        kw["grid_spec"] = grid_spec
    else:
        if grid is not None:
            kw["grid"] = grid
        kw["in_specs"] = in_specs
        kw["out_specs"] = out_specs
        kw["scratch_shapes"] = list(scratch)
    return pl.pallas_call(body, name=name, out_shape=out_shape, compiler_params=pltpu.CompilerParams(**params), **kw)


def _pick(n, target, mult):
    best = None
    for d in range(mult, min(n, target) + 1, mult):
        if n % d == 0:
            best = d
    return n if best is None else best


_DOT_DIMS = {"nn": (((1,), (0,)), ((), ())), "nt": (((1,), (1,)), ((), ())), "tn": (((0,), (0,)), ((), ()))}


def _mm(a, b, mode, out_dtype, name, res=None, carry=None):
    if mode == "tn":
        a, mode = a.T, "nn"
    if mode == "nn":
        (M, K), N = a.shape, b.shape[1]
    else:
        (M, K), N = a.shape, b.shape[0]
    tm = _pick(M, 1024, 8)
    tn = _pick(N, 512, 128)
    tk = _pick(K, 1024, 128)
    nk = K // tk
    has_res = res is not None

    def body(*refs):
        if has_res:
            a_ref, b_ref, r_ref, o_ref, acc = refs
        else:
            a_ref, b_ref, o_ref, acc = refs
        k = pl.program_id(2)

        @pl.when(k == 0)
        def _():
            acc[...] = jnp.zeros_like(acc)

        acc[...] += lax.dot_general(a_ref[...].astype(BF16), b_ref[...].astype(BF16), _DOT_DIMS[mode],
                                    preferred_element_type=F32)

        @pl.when(k == nk - 1)
        def _():
            r = acc[...]
            if has_res:
                r = r + r_ref[...]
            o_ref[...] = r.astype(o_ref.dtype)

    a_spec = pl.BlockSpec((tm, tk), lambda i, j, k: (i, k))
    b_spec = {"nn": pl.BlockSpec((tk, tn), lambda i, j, k: (k, j)),
              "nt": pl.BlockSpec((tn, tk), lambda i, j, k: (j, k))}[mode]
    o_spec = pl.BlockSpec((tm, tn), lambda i, j, k: (i, j))
    in_specs = [a_spec, b_spec] + ([o_spec] if has_res else [])
    args = (a, b) + ((res,) if has_res else ())
    return _call(body, name, jax.ShapeDtypeStruct((M, N), out_dtype), grid=(M // tm, N // tn, nk),
                 in_specs=in_specs, out_specs=o_spec, scratch=[pltpu.VMEM((tm, tn), F32)],
                 dims=("parallel", "parallel", "arbitrary"), carry=carry)(*args)


def _views(rows):
    return [r if isinstance(r, tuple) else (r, r.shape[1], 0) for r in rows]


def _rowwise(f, rows, params, outs, name, tb=256, carry=None, into=None):
    rows = _views(rows)
    T = rows[0][0].shape[0]
    tb = min(tb, T)
    nr, npar = len(rows), len(params)
    outs = [o if len(o) == 3 else (o[0], o[1], o[0]) for o in outs]
    into = into or []

    def body(*refs):
        vals = f(*[r[...].astype(F32) for r in refs[:nr]], *[p[...] for p in refs[nr:nr + npar]])
        for o_ref, v in zip(refs[nr + npar + len(into):], vals):
            o_ref[...] = v.astype(o_ref.dtype)

    row_spec = lambda w, cb=0: pl.BlockSpec((tb, w), lambda i: (i, cb))
    par_spec = lambda w: pl.BlockSpec((1, w), lambda i: (0, 0))
    out_shape = [jax.ShapeDtypeStruct((T, tw), dt) for _, dt, tw in outs]
    out_specs = [row_spec(w) for w, _, _ in outs]
    in_specs = [row_spec(w, cb) for _, w, cb in rows] + [par_spec(p.shape[1]) for p in params]
    args = [r[0] for r in rows] + list(params)
    aliases = {}
    for k, arr, cb in into:
        aliases[len(args)] = k
        in_specs.append(pl.BlockSpec(memory_space=pl.ANY))
        args.append(arr)
        out_shape[k] = jax.ShapeDtypeStruct(arr.shape, arr.dtype)
        out_specs[k] = row_spec(outs[k][0], cb)
    return _call(body, name, out_shape, grid=(T // tb,), in_specs=in_specs, out_specs=out_specs, dims=("parallel",),
                 carry=carry, aliases=aliases)(*args)


def _rowwise_bwd(f, rows, params, douts, n_diff, name, tb=256, carry=None, add=None, into=None):
    rows, douts = _views(rows), _views(douts)
    T = rows[0][0].shape[0]
    tb = min(tb, T)
    nr, npar, nd = len(rows), len(params), len(douts)
    n_add = 0 if add is None else 1

    def body(*refs):
        rv = [r[...].astype(F32) for r in refs[:nr]]
        pv = [p[...] for p in refs[nr:nr + npar]]
        dv = [d[...].astype(F32) for d in refs[nr + npar:nr + npar + nd]]
        o_refs = refs[nr + npar + nd + n_add + (0 if into is None else 1):]
        fixed = rv[n_diff:]

        def g(*xs):
            return tuple(f(*xs[:n_diff], *fixed, *xs[n_diff:]))

        _, vjp = jax.vjp(g, *rv[:n_diff], *pv)
        grads = list(vjp(tuple(dv)))
        if add is not None:
            grads[0] = grads[0] + refs[nr + npar + nd][...]
        for o_ref, gr in zip(o_refs[:n_diff], grads[:n_diff]):
            o_ref[...] = gr.astype(o_ref.dtype)
        first = pl.program_id(0) == 0
        for o_ref, gr in zip(o_refs[n_diff:], grads[n_diff:]):
            @pl.when(first)
            def _(o_ref=o_ref):
                o_ref[...] = jnp.zeros_like(o_ref)

            o_ref[...] += gr

    row_spec = lambda w, cb=0: pl.BlockSpec((tb, w), lambda i: (i, cb))
    par_spec = lambda w: pl.BlockSpec((1, w), lambda i: (0, 0))
    out_shape = ([jax.ShapeDtypeStruct((T, w), F32) for _, w, _ in rows[:n_diff]]
                 + [jax.ShapeDtypeStruct((1, p.shape[1]), F32) for p in params])
    out_specs = [row_spec(w) for _, w, _ in rows[:n_diff]] + [par_spec(p.shape[1]) for p in params]
    in_specs = ([row_spec(w, cb) for _, w, cb in rows] + [par_spec(p.shape[1]) for p in params]
                + [row_spec(w, cb) for _, w, cb in douts])
    args = [r[0] for r in rows] + list(params) + [d[0] for d in douts]
    aliases = None
    if add is not None:
        in_specs.append(row_spec(add.shape[1]))
        args.append(add)
    if into is not None:
        aliases = {len(args): 0}
        in_specs.append(pl.BlockSpec(memory_space=pl.ANY))
        args.append(into[0])
        out_shape[0] = jax.ShapeDtypeStruct(into[0].shape, into[0].dtype)
        out_specs[0] = row_spec(rows[0][1], into[1])
    return _call(body, name, out_shape, grid=(T // tb,), in_specs=in_specs, out_specs=out_specs,
                 dims=("arbitrary",), carry=carry, aliases=aliases)(*args)


def _sig(x):
    return 1.0 / (1.0 + jnp.exp(-x))


def _rms(x, g):
    return x * lax.rsqrt(jnp.mean(x * x, axis=-1, keepdims=True) + RMS_EPS) * g


def _f_rms(x, g):
    return (_rms(x, g),)


def _f_glu(a, gate):
    return (a * _sig(gate),)


def _f_ln_silu(x, g, b):
    mu = jnp.mean(x, axis=-1, keepdims=True)
    xc = x - mu
    var = jnp.mean(xc * xc, axis=-1, keepdims=True)
    y = xc * lax.rsqrt(var + LN_EPS) * g + b
    return (y * _sig(y),)


def _rope128(x, cos_p, sin_p):
    return x * cos_p + pltpu.roll(x, 64, 1) * sin_p


def _rope128_t(d, cos_p, sin_p):
    return d * cos_p + pltpu.roll(d * sin_p, 64, 1)


def _f_rope(xq, xk, cos_p, sin_p):
    heads = [_rope128(xq[:, h * 128:(h + 1) * 128], cos_p, sin_p) for h in range(MLA_HEADS)]
    return (jnp.concatenate(heads, axis=1), _rope128(xk, cos_p, sin_p))


def _f_rope_t(dq, dk_heads, cos_p, sin_p):
    heads = [_rope128_t(dq[:, h * 128:(h + 1) * 128], cos_p, sin_p) for h in range(MLA_HEADS)]
    dk = dk_heads[:, 0:128]
    for h in range(1, MLA_HEADS):
        dk = dk + dk_heads[:, h * 128:(h + 1) * 128]
    return (jnp.concatenate(heads, axis=1), _rope128_t(dk, cos_p, sin_p))


GATE_LANES = 256


def _gate_fwd(ycat, proj, z_col, name, tb=1024):
    T, width = ycat.shape
    zb = z_col // GATE_LANES

    def body(y_ref, z_ref, o_ref):
        z = z_ref[...]
        o_ref[...] = (y_ref[...] * (z * _sig(z))).astype(o_ref.dtype)

    blk = pl.BlockSpec((tb, GATE_LANES), lambda i, c: (i, c))
    return _call(body, name, jax.ShapeDtypeStruct((T, width), BF16), grid=(T // tb, width // GATE_LANES),
                 in_specs=[blk, pl.BlockSpec((tb, GATE_LANES), lambda i, c: (i, zb + c))], out_specs=blk,
                 dims=("parallel", "parallel"))(ycat, proj)


def _gate_bwd(ycat, proj, z_col, dy, name, tb=1024, carry=None):
    T, width = ycat.shape
    zb = z_col // GATE_LANES

    def body(y_ref, z_ref, dy_ref, dycat_ref, dz_ref):
        z, d = z_ref[...], dy_ref[...]
        s = _sig(z)
        dycat_ref[...] = d * (z * s)
        dz_ref[...] = (d * y_ref[...] * (s * (1.0 + z * (1.0 - s)))).astype(dz_ref.dtype)

    blk = pl.BlockSpec((tb, GATE_LANES), lambda i, c: (i, c))
    zblk = pl.BlockSpec((tb, GATE_LANES), lambda i, c: (i, zb + c))
    return _call(body, name, [jax.ShapeDtypeStruct((T, width), F32), jax.ShapeDtypeStruct(proj.shape, BF16)],
                 grid=(T // tb, width // GATE_LANES), in_specs=[blk, zblk, blk], out_specs=[blk, zblk],
                 dims=("parallel", "parallel"), carry=carry)(ycat, proj, dy)


def _glu_bwd(proj, d_glu, d_proj, name, tb=256):
    T, w = d_glu.shape

    def body(a_ref, g_ref, d_ref, _, o_ref):
        s = _sig(g_ref[...])

        @pl.when(pl.program_id(1) == 0)
        def _():
            o_ref[...] = (d_ref[...] * s).astype(o_ref.dtype)

        @pl.when(pl.program_id(1) == 1)
        def _():
            o_ref[...] = (d_ref[...] * a_ref[...] * (s * (1.0 - s))).astype(o_ref.dtype)

    return _call(body, name, jax.ShapeDtypeStruct(d_proj.shape, d_proj.dtype), grid=(T // tb, 2),
                 in_specs=[pl.BlockSpec((tb, w), lambda i, c: (i, 0)), pl.BlockSpec((tb, w), lambda i, c: (i, 1)),
                           pl.BlockSpec((tb, w), lambda i, c: (i, 0)), pl.BlockSpec(memory_space=pl.ANY)],
                 out_specs=pl.BlockSpec((tb, w), lambda i, c: (i, c)), dims=("parallel", "arbitrary"),
                 aliases={3: 0})(proj, proj, d_glu, d_proj)


def _final_loss(h, tgt, g, name, tb=256):
    T, D = h.shape

    def body(h_ref, t_ref, g_ref, dh_ref, dg_ref, loss_ref):
        tv = t_ref[...]

        def rowloss(hh, gg):
            e = _rms(hh, gg) - tv
            return 0.5 * jnp.mean(e * e, axis=-1, keepdims=True)

        lr, vjp = jax.vjp(rowloss, h_ref[...], g_ref[...])
        dh, dg = vjp(jnp.ones_like(lr))
        dh_ref[...] = dh

        @pl.when(pl.program_id(0) == 0)
        def _():
            dg_ref[...] = jnp.zeros_like(dg_ref)
            loss_ref[...] = jnp.zeros_like(loss_ref)

        dg_ref[...] += dg
        loss_ref[...] += jnp.broadcast_to(jnp.sum(lr, axis=0, keepdims=True), loss_ref.shape)

    row = pl.BlockSpec((tb, D), lambda i: (i, 0))
    par = pl.BlockSpec((1, D), lambda i: (0, 0))
    return _call(body, name,
                 [jax.ShapeDtypeStruct((T, D), F32), jax.ShapeDtypeStruct((1, D), F32), jax.ShapeDtypeStruct((1, 128), F32)],
                 grid=(T // tb,), in_specs=[row, row, par],
                 out_specs=[row, par, pl.BlockSpec((1, 128), lambda i: (0, 0))], dims=("arbitrary",))(h, tgt, g)


CONV_ROWS = 128
CONV_LANES = 256


def _sublane_phases(pad, n):
    for r in range(1, 8):
        for c0 in range(0, n - 8, 256):
            rows = min(256, n - 8 - c0)
            pad[r, c0:c0 + rows, :] = pad[0, c0 + r:c0 + r + rows, :]


def _dwconv_fwd(x, w, b, name, carry=None):
    B, S, C = x.shape
    cb = CONV_LANES
    off = CONV_PAD - (CONV_KERNEL - 1)

    def body(x_ref, w_ref, b_ref, o_ref, pad):
        pad[0, 0:CONV_PAD, :] = jnp.zeros((CONV_PAD, cb), F32)
        pad[0, CONV_PAD:, :] = x_ref[...]
        _sublane_phases(pad, S + CONV_PAD)
        for t0 in range(0, S, CONV_ROWS):
            acc = jnp.broadcast_to(b_ref[...], (CONV_ROWS, cb))
            for k in range(CONV_KERNEL):
                r, base = (off + k) % 8, t0 + (off + k) // 8 * 8
                acc = acc + w_ref[k:k + 1, :] * pad[r, base:base + CONV_ROWS, :]
            o_ref[t0:t0 + CONV_ROWS, :] = acc

    return _call(body, name, jax.ShapeDtypeStruct((B, S, C), F32), grid=(B, C // cb),
                 in_specs=[pl.BlockSpec((None, S, cb), lambda i, j: (i, 0, j)),
                           pl.BlockSpec((CONV_KERNEL, cb), lambda i, j: (0, j)),
                           pl.BlockSpec((1, cb), lambda i, j: (0, j))],
                 out_specs=pl.BlockSpec((None, S, cb), lambda i, j: (i, 0, j)),
                 scratch=[pltpu.VMEM((8, S + CONV_PAD, cb), F32)], dims=("parallel", "parallel"), carry=carry)(x, w, b)


def _dwconv_bwd(x, w, dy, name, carry=None):
    B, S, C = x.shape
    cb = CONV_LANES
    off = CONV_PAD - (CONV_KERNEL - 1)
    groups = CONV_ROWS // 8

    def body(x_ref, w_ref, dy_ref, dx_ref, dw_ref, db_ref, dypad, wacc):
        dypad[0, 0:S, :] = dy_ref[...]
        dypad[0, S:, :] = jnp.zeros((CONV_PAD, cb), F32)
        _sublane_phases(dypad, S + CONV_PAD)
        wacc[...] = jnp.zeros_like(wacc)
        for t0 in range(0, S, CONV_ROWS):
            xc = x_ref[t0:t0 + CONV_ROWS, :]
            acc = jnp.zeros((CONV_ROWS, cb), F32)
            for k in range(CONV_KERNEL):
                o = (CONV_KERNEL - 1) - k
                dys = dypad[o % 8, t0 + o // 8 * 8:t0 + o // 8 * 8 + CONV_ROWS, :]
                acc = acc + w_ref[k:k + 1, :] * dys
                wacc[k] += jnp.sum((dys * xc).reshape(groups, 8, cb), axis=0)
            wacc[CONV_KERNEL] += jnp.sum(dy_ref[t0:t0 + CONV_ROWS, :].reshape(groups, 8, cb), axis=0)
            dx_ref[t0:t0 + CONV_ROWS, :] = acc

        @pl.when(pl.program_id(1) == 0)
        def _():
            dw_ref[...] = jnp.zeros_like(dw_ref)
            db_ref[...] = jnp.zeros_like(db_ref)

        for k in range(CONV_KERNEL):
            dw_ref[k:k + 1, :] += jnp.sum(wacc[k], axis=0, keepdims=True)
        db_ref[...] += jnp.sum(wacc[CONV_KERNEL], axis=0, keepdims=True)

    blk = pl.BlockSpec((None, S, cb), lambda j, i: (i, 0, j))
    return _call(body, name,
                 [jax.ShapeDtypeStruct((B, S, C), F32), jax.ShapeDtypeStruct((CONV_KERNEL, C), F32),
                  jax.ShapeDtypeStruct((1, C), F32)],
                 grid=(C // cb, B),
                 in_specs=[blk, pl.BlockSpec((CONV_KERNEL, cb), lambda j, i: (0, j)), blk],
                 out_specs=[blk, pl.BlockSpec((CONV_KERNEL, cb), lambda j, i: (0, j)),
                            pl.BlockSpec((1, cb), lambda j, i: (0, j))],
                 scratch=[pltpu.VMEM((8, S + CONV_PAD, cb), F32), pltpu.VMEM((CONV_KERNEL + 1, 8, cb), F32)],
                 dims=("parallel", "arbitrary"), carry=carry)(x, w, dy)


ATTN_TILE = {"fwd": 1024, "bwd": 1024, "cross fwd": 512}
ATTN_SUB = {"fwd": 256, "bwd": 512}


def _attn_shapes(Sq, Sk, causal, pass_):
    tq = min(Sq, ATTN_TILE[pass_ if causal or pass_ == "bwd" else "cross fwd"])
    tk = tq if causal else min(Sk, ATTN_TILE[pass_])
    return tq, tk, min(ATTN_SUB[pass_], tq)


def _mask(row0, col0, rows, cols):
    r = row0 + lax.broadcasted_iota(jnp.int32, (rows, cols), 0)
    c = col0 + lax.broadcasted_iota(jnp.int32, (rows, cols), 1)
    return c <= r


def _attn_fwd(q, q_c0, qr, k, k_c0, kr, v, v_c0, B, Sq, Sk, H, causal, scale, name, into=None, o_c0=0, o_width=None,
              kv_stride=1):
    tq, tk, sub = _attn_shapes(Sq, Sk, causal, "fwd")
    nq, nk, nsub = Sq // tq, Sk // tk, tq // sub
    rope = qr is not None

    def body(*refs):
        refs = list(refs)
        qn_ref = refs.pop(0)
        qr_ref = refs.pop(0) if rope else None
        kn_ref = refs.pop(0)
        kr_ref = refs.pop(0) if rope else None
        v_ref = refs.pop(0)
        if into is not None:
            refs.pop(0)
        o_ref, lse_ref, m_s, l_s, acc = refs
        qi = pl.program_id(2)
        m_s[...] = jnp.full_like(m_s, NEG)
        l_s[...] = jnp.zeros_like(l_s)
        acc[...] = jnp.zeros_like(acc)
        qs = []
        for r in range(nsub):
            qn = qn_ref[r * sub:(r + 1) * sub, :].astype(BF16)
            qs.append(jnp.concatenate([qn, qr_ref[r * sub:(r + 1) * sub, :]], axis=1) if rope else qn)

        def step(j, masked):
            ks = pl.ds(pl.multiple_of(j * tk, tk), tk)
            kk = jnp.concatenate([kn_ref[ks, :], kr_ref[ks, :]], axis=1) if rope else kn_ref[ks, :]
            vv = v_ref[ks, :]
            for r in range(nsub):
                rows = slice(r * sub, (r + 1) * sub)
                nc = (r + 1) * sub if masked else tk
                s = lax.dot_general(qs[r], kk[:nc], _DOT_DIMS["nt"], preferred_element_type=F32) * scale
                if masked:
                    s = jnp.where(_mask(qi * tq + r * sub, j * tk, sub, nc), s, NEG)
                m_old = m_s[rows, :]
                m_new = jnp.maximum(m_old, jnp.max(s, axis=-1, keepdims=True))
                p = jnp.exp(s - m_new)
                alpha = jnp.exp(m_old - m_new)
                l_s[rows, :] = alpha * l_s[rows, :] + jnp.sum(p, axis=-1, keepdims=True)
                acc[rows, :] = alpha * acc[rows, :] + jnp.dot(p.astype(BF16), vv[:nc], preferred_element_type=F32)
                m_s[rows, :] = m_new

        def unmasked(j, carry):
            step(j, False)
            return carry

        if causal:
            lax.fori_loop(0, qi, unmasked, 0)
            step(qi, True)
        else:
            lax.fori_loop(0, nk, unmasked, 0)
        o_ref[...] = (acc[...] / l_s[...]).astype(o_ref.dtype)
        lse_ref[...] = m_s[...] + jnp.log(l_s[...])

    qspec = lambda c0: pl.BlockSpec((tq, 128), lambda b, h, i: (b * nq + i, c0 + h))
    kspec = lambda c0: pl.BlockSpec((Sk, 128), lambda b, h, i: (b, c0 + kv_stride * h))
    in_specs, args = [qspec(q_c0)], [q]
    if rope:
        in_specs.append(qspec(0)); args.append(qr)
    in_specs.append(kspec(k_c0)); args.append(k)
    if rope:
        in_specs.append(pl.BlockSpec((Sk, 128), lambda b, h, i: (b, 0))); args.append(kr)
    in_specs.append(kspec(v_c0)); args.append(v)
    aliases = {}
    if into is not None:
        aliases = {len(args): 0}
        in_specs.append(pl.BlockSpec(memory_space=pl.ANY)); args.append(into)
        o_shape = jax.ShapeDtypeStruct(into.shape, into.dtype)
    else:
        o_shape = jax.ShapeDtypeStruct((B * Sq, o_width), F32)
    return _call(body, name, [o_shape, jax.ShapeDtypeStruct((B * H, Sq, 1), F32)], grid=(B, H, nq), in_specs=in_specs,
                 out_specs=[qspec(o_c0), pl.BlockSpec((None, tq, 1), lambda b, h, i: (b * H + h, i, 0))],
                 scratch=[pltpu.VMEM((tq, 1), F32), pltpu.VMEM((tq, 1), F32), pltpu.VMEM((tq, 128), F32)],
                 dims=("parallel", "parallel", "arbitrary"), aliases=aliases)(*args)


def _attn_bwd(q, q_c0, qr, k, k_c0, kr, v, v_c0, o, do, o_c0, lse, B, Sq, Sk, H, causal, scale, name, dq_into=None,
              kv_stride=1):
    tq, tk, sub = _attn_shapes(Sq, Sk, causal, "bwd")
    nq, nk, nsub = Sq // tq, Sk // tk, tq // sub
    rope = qr is not None
    dk_w = 256 if rope else 128

    def body(*refs):
        refs = list(refs)
        qn_ref = refs.pop(0)
        qr_ref = refs.pop(0) if rope else None
        kn_ref = refs.pop(0)
        kr_ref = refs.pop(0) if rope else None
        v_ref, o_ref, do_ref, lse_ref = refs[:4]
        refs = refs[4 + (0 if dq_into is None else 1):]
        dqn_ref = refs.pop(0)
        dqr_ref = refs.pop(0) if rope else None
        dkn_ref = refs.pop(0)
        dkr_ref = refs.pop(0) if rope else None
        dv_ref = None if rope else refs.pop(0)
        q_s, do_s, dl_s, dq_acc, dk_acc, dv_acc = refs
        kj = pl.program_id(2)

        @pl.when(kj == 0)
        def _():
            qn = qn_ref[...].astype(BF16)
            q_s[...] = jnp.concatenate([qn, qr_ref[...]], axis=1) if rope else qn
            dof = do_ref[...]
            do_s[...] = dof.astype(BF16)
            dl_s[...] = jnp.sum(dof * o_ref[...], axis=-1, keepdims=True)
            dq_acc[...] = jnp.zeros_like(dq_acc)

        kk = jnp.concatenate([kn_ref[...], kr_ref[...]], axis=1) if rope else kn_ref[...]
        vv = v_ref[...]
        dk_acc[...] = jnp.zeros_like(dk_acc)
        dv_acc[...] = jnp.zeros_like(dv_acc)

        def step(i, masked):
            for r in range(nsub):
                rows = pl.ds(pl.multiple_of(i * tq + r * sub, sub), sub)
                qq, dob = q_s[rows, :], do_s[rows, :]
                nc = (r + 1) * sub if masked else tk
                kc, vc = kk[:nc], vv[:nc]
                s = lax.dot_general(qq, kc, _DOT_DIMS["nt"], preferred_element_type=F32) * scale
                if masked:
                    s = jnp.where(_mask(i * tq + r * sub, kj * tk, sub, nc), s, NEG)
                p = jnp.exp(s - lse_ref[rows, :])
                dp = lax.dot_general(dob, vc, _DOT_DIMS["nt"], preferred_element_type=F32)
                ds = (p * (dp - dl_s[rows, :]) * scale).astype(BF16)
                dv_acc[0:nc, :] += lax.dot_general(p.astype(BF16), dob, _DOT_DIMS["tn"], preferred_element_type=F32)
                dk_acc[0:nc, :] += lax.dot_general(ds, qq, _DOT_DIMS["tn"], preferred_element_type=F32)
                dq_acc[rows, :] += jnp.dot(ds, kc, preferred_element_type=F32)

        def unmasked(i, carry):
            step(i, False)
            return carry

        if causal:
            step(kj, True)
            lax.fori_loop(kj + 1, nq, unmasked, 0)
        else:
            lax.fori_loop(0, nq, unmasked, 0)
        if rope:
            dkn_ref[...] = jnp.concatenate([dk_acc[:, 0:128], dv_acc[...]], axis=1).astype(dkn_ref.dtype)
            dkr_ref[...] = dk_acc[:, 128:256]
        else:
            dkn_ref[...] = dk_acc[...]
            dv_ref[...] = dv_acc[...]

        @pl.when(kj == nk - 1)
        def _():
            dqn_ref[...] = dq_acc[:, 0:128].astype(dqn_ref.dtype)
            if rope:
                dqr_ref[...] = dq_acc[:, 128:256]

    qspec = lambda c0: pl.BlockSpec((Sq, 128), lambda b, h, j: (b, c0 + h))
    kspec = lambda c0: pl.BlockSpec((tk, 128), lambda b, h, j: (b * nk + j, c0 + kv_stride * h))
    in_specs, args = [qspec(q_c0)], [q]
    if rope:
        in_specs.append(qspec(0)); args.append(qr)
    in_specs.append(kspec(k_c0)); args.append(k)
    if rope:
        in_specs.append(pl.BlockSpec((tk, 128), lambda b, h, j: (b * nk + j, 0))); args.append(kr)
    in_specs += [kspec(v_c0), qspec(o_c0), qspec(o_c0), pl.BlockSpec((None, Sq, 1), lambda b, h, j: (b * H + h, 0, 0))]
    args += [v, o, do, lse]
    h_rows_q = jax.ShapeDtypeStruct((B * Sq, H * 128), F32)
    h_rows_k = jax.ShapeDtypeStruct((B * Sk, H * 128), F32)
    out_shape, out_specs, aliases = [h_rows_q], [qspec(0)], None
    if rope:
        out_shape = [jax.ShapeDtypeStruct((B * Sq, 2 * H * 128), BF16)]
    if dq_into is not None:
        aliases = {len(args): 0}
        in_specs.append(pl.BlockSpec(memory_space=pl.ANY)); args.append(dq_into[0])
        out_shape, out_specs = [jax.ShapeDtypeStruct(dq_into[0].shape, dq_into[0].dtype)], [qspec(dq_into[1])]
    if rope:
        out_shape.append(h_rows_q); out_specs.append(qspec(0))
    hspec = lambda w: pl.BlockSpec((tk, w), lambda b, h, j: (b * nk + j, h))
    if rope:
        out_shape += [jax.ShapeDtypeStruct((B * Sk, H * 256), BF16), h_rows_k]
        out_specs += [hspec(256), hspec(128)]
    else:
        out_shape += [h_rows_k, h_rows_k]
        out_specs += [hspec(128), hspec(128)]
    return _call(body, name, out_shape, grid=(B, H, nk), in_specs=in_specs, out_specs=out_specs,
                 scratch=[pltpu.VMEM((Sq, dk_w), BF16), pltpu.VMEM((Sq, 128), BF16), pltpu.VMEM((Sq, 1), F32),
                          pltpu.VMEM((Sq, dk_w), F32), pltpu.VMEM((tk, dk_w), F32), pltpu.VMEM((tk, 128), F32)],
                 dims=("parallel", "parallel", "arbitrary"), aliases=aliases)(*args)


def _mem_attention_fwd(proj, q_col, ycat, mem2, mem_g, w_mem, B, S, tag):
    M = mem2.shape[0] // B
    (memn,) = _rowwise(_f_rms, [mem2], [mem_g], [(mem2.shape[1], BF16)], tag + "_memnorm")
    kvm = _mm(memn, w_mem, "nn", BF16, tag + "_memkv")
    o_c0 = ycat.shape[1] // 128 - MEM_HEADS
    ycat, lse = _attn_fwd(proj, q_col // 128, None, kvm, 0, None, kvm, MEM_HEADS, B, S, M, MEM_HEADS, False,
                          MEM_HEAD_DIM ** -0.5, tag + "_memattn", into=ycat, o_c0=o_c0)
    return ycat, (memn, kvm, lse)


def _mem_attention_bwd(proj, q_col, ycat, d_ycat, d_proj, saved, mem2, mem_g, w_mem, B, S, tag):
    memn, kvm, lse = saved
    M = mem2.shape[0] // B
    o_c0 = ycat.shape[1] // 128 - MEM_HEADS
    d_q, d_k, d_v = _attn_bwd(proj, q_col // 128, None, kvm, 0, None, kvm, MEM_HEADS, ycat, d_ycat, o_c0, lse, B, S, M,
                              MEM_HEADS, False, MEM_HEAD_DIM ** -0.5, tag + "_memattn_bwd", dq_into=(d_proj, q_col // 128))
    d_kvm = jnp.concatenate([d_k, d_v], axis=1).astype(BF16)
    d_w_mem = _mm(memn, d_kvm, "tn", F32, tag + "_memkv_dw")
    d_memn = _mm(d_kvm, w_mem, "nt", F32, tag + "_memkv_dx")
    _, d_mem_g = _rowwise_bwd(_f_rms, [mem2], [mem_g], [d_memn], 1, tag + "_memnorm_bwd")
    return d_q, d_w_mem, d_mem_g


def _rope_tables(positions):
    inv_freq = 1.0 / (ROPE_THETA ** (jnp.arange(0, MLA_ROPE, 2, dtype=F32) / MLA_ROPE))
    ang = positions.astype(F32).reshape(-1, 1) * inv_freq
    cos, sin, zero = jnp.cos(ang), jnp.sin(ang), jnp.zeros_like(ang)
    return jnp.concatenate([cos, zero, cos, zero], axis=1), jnp.concatenate([-sin, zero, sin, zero], axis=1)


def _forward_backward(x, mem, positions, target, W):
    B, S, D = x.shape
    T = B * S
    conv_w = W["conv_dw"].shape[1]
    mix_w = 2 * D
    h0 = x.reshape(T, D)
    mem2 = mem.reshape(-1, D)
    tgt = target.reshape(T, D)
    row = lambda v: v.reshape(1, -1)
    n_nope = MLA_HEADS * MLA_NOPE

    g0 = row(W["norm_g"][0])
    (u0,) = _rowwise(_f_rms, [h0], [g0], [(D, BF16)], "l0_norm", carry=W.carry("l0_norm"))
    proj0 = _mm(u0, W["conv_w_in"], "nn", F32, "l0_in", carry=W.carry("l0_in"))
    a0, gate0 = (proj0, conv_w, 0), (proj0, conv_w, 1)
    qm0_col, z0_col = 2 * conv_w, 2 * conv_w + MEM_WIDTH
    (glu,) = _rowwise(_f_glu, [a0, gate0], [], [(conv_w, F32)], "l0_glu", carry=W.carry("l0_glu"))
    dw, dwb = W["conv_dw"], row(W["conv_dw_b"][0])
    cv = _dwconv_fwd(glu.reshape(B, S, conv_w), dw, dwb, "l0_dwconv", carry=W.carry("l0_dwconv")).reshape(T, conv_w)
    ln_g, ln_b = row(W["conv_ln_g"][0]), row(W["conv_ln_b"][0])
    (ycat0,) = _rowwise(_f_ln_silu, [cv], [ln_g, ln_b], [(conv_w, F32, mix_w)], "l0_ln", carry=W.carry("l0_ln"))
    mg0 = row(W["mem_norm_g"][0])
    ycat0, mem_saved0 = _mem_attention_fwd(proj0, qm0_col, ycat0, mem2, mg0, W["w_mem_kv"][0], B, S, "l0")
    y0 = _gate_fwd(ycat0, proj0, z0_col, "l0_gate")
    h1 = _mm(y0, W["w_out"][0], "nn", F32, "l0_out", res=h0)

    g1 = row(W["norm_g"][1])
    (u1,) = _rowwise(_f_rms, [h1], [g1], [(D, BF16)], "l1_norm")
    proj1 = _mm(u1, W["mla_w_in"], "nn", F32, "l1_in")
    cq, ckv = (proj1, Q_RANK, 0), (proj1, KV_RANK, Q_RANK // KV_RANK)
    qm1_col = Q_RANK + KV_RANK
    z1_col = qm1_col + MEM_WIDTH
    kr_col = z1_col + mix_w
    qg, kvg = row(W["mla_q_norm_g"]), row(W["mla_kv_norm_g"])
    (cqn,) = _rowwise(_f_rms, [cq], [qg], [(Q_RANK, BF16)], "l1_qnorm")
    (ckvn,) = _rowwise(_f_rms, [ckv], [kvg], [(KV_RANK, BF16)], "l1_kvnorm")
    qf = _mm(cqn, W["mla_w_uq"], "nn", F32, "l1_uq")
    kvf = _mm(ckvn, W["mla_w_ukv"], "nn", BF16, "l1_ukv")
    cos_p, sin_p = _rope_tables(positions)
    qr, kr = _rowwise(_f_rope, [(qf, n_nope, 1), (proj1, 128, kr_col // 128), cos_p, sin_p], [],
                      [(n_nope, BF16), (128, BF16)], "l1_rope")
    scale1 = MLA_QK ** -0.5
    ycat1, lse1 = _attn_fwd(qf, 0, qr, kvf, 0, kr, kvf, 1, B, S, S, MLA_HEADS, True, scale1, "l1_attn",
                            o_width=mix_w, kv_stride=2)
    mg1 = row(W["mem_norm_g"][1])
    ycat1, mem_saved1 = _mem_attention_fwd(proj1, qm1_col, ycat1, mem2, mg1, W["w_mem_kv"][1], B, S, "l1")
    y1 = _gate_fwd(ycat1, proj1, z1_col, "l1_gate")
    h2 = _mm(y1, W["w_out"][1], "nn", F32, "l1_out", res=h1)

    gf = row(W["final_norm_g"])
    dh2, d_gf, loss128 = _final_loss(h2, tgt, gf, "final_loss")
    G = {"final_norm_g": d_gf.reshape(-1)}
    L1 = {}

    dy1 = _mm(dh2, W["w_out"][1], "nt", F32, "l1_out_dx")
    d_wout1 = _mm(y1, dh2, "tn", F32, "l1_out_dw")
    d_ycat1, d_proj1 = _gate_bwd(ycat1, proj1, z1_col, dy1, "l1_gate_bwd")
    d_proj1, d_wmem1, d_mg1 = _mem_attention_bwd(proj1, qm1_col, ycat1, d_ycat1, d_proj1, mem_saved1, mem2, mg1,
                                                 W["w_mem_kv"][1], B, S, "l1")
    d_qf, d_qr, d_kvf, d_kr_heads = _attn_bwd(qf, 0, qr, kvf, 0, kr, kvf, 1, ycat1, d_ycat1, 0, lse1, B, S, S,
                                              MLA_HEADS, True, scale1, "l1_attn_bwd", kv_stride=2)
    d_qf, d_proj1 = _rowwise(_f_rope_t, [d_qr, d_kr_heads, cos_p, sin_p], [], [(n_nope, F32), (128, F32)], "l1_rope_bwd",
                             into=[(0, d_qf, 1), (1, d_proj1, kr_col // 128)])
    d_cqn = _mm(d_qf, W["mla_w_uq"], "nt", F32, "l1_uq_dx")
    L1[("mla_w_uq", None)] = _mm(cqn, d_qf, "tn", F32, "l1_uq_dw")
    d_ckvn = _mm(d_kvf, W["mla_w_ukv"], "nt", F32, "l1_ukv_dx")
    L1[("mla_w_ukv", None)] = _mm(ckvn, d_kvf, "tn", F32, "l1_ukv_dw")
    d_proj1, d_qg = _rowwise_bwd(_f_rms, [cq], [qg], [d_cqn], 1, "l1_qnorm_bwd", into=(d_proj1, cq[2]))
    d_proj1, d_kvg = _rowwise_bwd(_f_rms, [ckv], [kvg], [d_ckvn], 1, "l1_kvnorm_bwd", into=(d_proj1, ckv[2]))
    L1[("w_mem_kv", 1)] = d_wmem1
    L1[("mla_w_in", None)] = _mm(u1, d_proj1, "tn", F32, "l1_in_dw")
    L1[("w_out", 1)] = d_wout1
    W.ready("l1", L1)
    d_u1 = _mm(d_proj1, W["mla_w_in"], "nt", F32, "l1_in_dx", carry=W.carry("l1_in_dx"))
    dh1, d_g1 = _rowwise_bwd(_f_rms, [h1], [g1], [d_u1], 1, "l1_norm_bwd", add=dh2)

    dy0 = _mm(dh1, W["w_out"][0], "nt", F32, "l0_out_dx")
    d_wout0 = _mm(y0, dh1, "tn", F32, "l0_out_dw")
    d_ycat0, d_proj0 = _gate_bwd(ycat0, proj0, z0_col, dy0, "l0_gate_bwd", carry=W.carry("l0_gate_bwd"))
    d_proj0, d_wmem0, d_mg0 = _mem_attention_bwd(proj0, qm0_col, ycat0, d_ycat0, d_proj0, mem_saved0, mem2, mg0,
                                                 W["w_mem_kv"][0], B, S, "l0")
    W.ready("l0a", {("w_mem_kv", 0): d_wmem0, ("w_out", 0): d_wout0})
    d_cv, d_ln_g, d_ln_b = _rowwise_bwd(_f_ln_silu, [cv], [ln_g, ln_b], [(d_ycat0, conv_w, 0)], 1, "l0_ln_bwd",
                                        carry=W.carry("l0_ln_bwd"))
    d_glu, d_dw, d_dwb = _dwconv_bwd(glu.reshape(B, S, conv_w), dw, d_cv.reshape(B, S, conv_w), "l0_dwconv_bwd",
                                     carry=W.carry("l0_dwconv_bwd"))
    d_proj0 = _glu_bwd(proj0, d_glu.reshape(T, conv_w), d_proj0, "l0_glu_bwd")
    d_conv_w_in = _mm(u0, d_proj0, "tn", F32, "l0_in_dw", carry=W.carry("l0_in_dw"))
    W.ready("l0b", {("conv_w_in", None): d_conv_w_in, ("conv_dw", None): d_dw,
                    ("mla_q_norm_g", None): d_qg.reshape(-1), ("mla_kv_norm_g", None): d_kvg.reshape(-1)})
    d_u0 = _mm(d_proj0, W["conv_w_in"], "nt", F32, "l0_in_dx", carry=W.carry("l0_in_dx"))
    dx, d_g0 = _rowwise_bwd(_f_rms, [h0], [g0], [d_u0], 1, "l0_norm_bwd", add=dh1)
    dx = dx.reshape(B, S, D)

    G["norm_g"] = jnp.concatenate([d_g0, d_g1], axis=0)
    G["mem_norm_g"] = jnp.concatenate([d_mg0, d_mg1], axis=0)
    G["conv_dw_b"] = d_dwb
    G["conv_ln_g"], G["conv_ln_b"] = d_ln_g, d_ln_b
    return loss128[0, 0], dx, G


def _mla_in_perm(w):
    c2 = Q_RANK + KV_RANK
    zero = jnp.zeros((w.shape[0], HALF_ROPE), w.dtype)
    return jnp.concatenate([w[:, :c2], w[:, c2 + MLA_ROPE:], w[:, c2:c2 + HALF_ROPE], zero,
                            w[:, c2 + HALF_ROPE:c2 + MLA_ROPE], zero], axis=1)


def _mla_in_unperm(g):
    c2 = Q_RANK + KV_RANK
    r = g.shape[1] - 128
    return jnp.concatenate([g[:, :c2], g[:, r:r + HALF_ROPE], g[:, r + 64:r + 64 + HALF_ROPE], g[:, c2:r]], axis=1)


def _uq_perm(w):
    n = w.shape[0]
    w3 = w.reshape(n, MLA_HEADS, MLA_QK)
    zero = jnp.zeros((n, MLA_HEADS, HALF_ROPE), w.dtype)
    rope = jnp.concatenate([w3[:, :, MLA_NOPE:MLA_NOPE + HALF_ROPE], zero, w3[:, :, MLA_NOPE + HALF_ROPE:], zero], axis=2)
    return jnp.concatenate([w3[:, :, :MLA_NOPE].reshape(n, -1), rope.reshape(n, -1)], axis=1)


def _uq_unperm(g):
    n = g.shape[0]
    n_nope = MLA_HEADS * MLA_NOPE
    rope = g[:, n_nope:].reshape(n, MLA_HEADS, 128)
    return jnp.concatenate([g[:, :n_nope].reshape(n, MLA_HEADS, MLA_NOPE), rope[:, :, :HALF_ROPE],
                            rope[:, :, 64:64 + HALF_ROPE]], axis=2).reshape(n, -1)


_ROW_CUT = ("w_mem_kv", "w_out")
_COL_CUT = ("conv_w_in", "mla_w_in", "mla_w_uq", "mla_w_ukv", "conv_dw")
_BIG = ("w_mem_kv", "w_out", "conv_w_in", "mla_w_in", "mla_w_uq", "mla_w_ukv")
_SMALL_SHARDED = ("conv_dw", "mla_q_norm_g", "mla_kv_norm_g")
_REPLICATED = ("norm_g", "mem_norm_g", "conv_dw_b", "conv_ln_g", "conv_ln_b", "final_norm_g")
_PERM = {"mla_w_in": (_mla_in_perm, _mla_in_unperm), "mla_w_uq": (_uq_perm, _uq_unperm)}


def _join(n, blocks):
    if n in _ROW_CUT:
        _, L, r, c = blocks.shape
        return blocks.transpose(1, 0, 2, 3).reshape(L, N_DEV * r, c)
    if n in _COL_CUT:
        _, _, r, c = blocks.shape
        return blocks.reshape(N_DEV, r, c).transpose(1, 0, 2).reshape(r, N_DEV * c)
    return blocks.reshape(-1)


def _cut(n, full, shard_shape):
    if n in _ROW_CUT:
        L, r, c = shard_shape
        return full.reshape(L, N_DEV, r, c).transpose(1, 0, 2, 3)
    if n in _COL_CUT:
        _, r, c = shard_shape
        return full.reshape(r, N_DEV, c).transpose(1, 0, 2).reshape(N_DEV, 1, r, c)
    return full.reshape(N_DEV, 1, -1)


def _flat_pad(parts, size):
    flat = jnp.concatenate([p.reshape(-1) for p in parts])
    return jnp.concatenate([flat, jnp.zeros((size - flat.shape[0],), flat.dtype)])


SMALL_LANES = 128 * 8


def _as_tiles(flat_parts):
    total = sum(p.size for p in flat_parts)
    size = -(-total // SMALL_LANES) * SMALL_LANES
    return _flat_pad(flat_parts, size).reshape(8, size // 8)


def _split_flat(flat, like):
    out, o = [], 0
    for a in like:
        out.append(flat[o:o + a.size].reshape(a.shape))
        o += a.size
    return out


_HBM = pl.BlockSpec(memory_space=pltpu.HBM)
_VMEM = pl.BlockSpec(memory_space=pltpu.VMEM)


def _position():
    return lax.axis_index("x"), lax.axis_index("y"), lax.axis_index("c")


def _dma_sems(n):
    return [pltpu.SemaphoreType.DMA((n,)), pltpu.SemaphoreType.DMA((n,))]


def _run_stage(stage, name):
    n_in, n_out = len(stage.ins), len(stage.out_shapes)

    def body(*refs):
        ins, outs, sems = refs[:n_in], refs[n_in:n_in + n_out], refs[n_in + n_out:]
        stage.start(ins, outs, sems)
        stage.wait(ins, outs, sems)

    outs = _call(body, name, stage.out_shapes, in_specs=[_HBM] * n_in, out_specs=[_HBM] * n_out, scratch=stage.sems,
                 aliases=stage.aliases)(*stage.ins)
    stage.outs = list(outs)
    return stage.outs


def _gather_chips_stage(shards):
    n = len(shards)

    def copies(x_refs, out_refs, sems):
        send_sems, recv_sems, _ = sems
        x, y, c = _position()
        peers = [(x, y, 1 - c), (1 - x, y, c), (x, 1 - y, c), (1 - x, 1 - y, c)]
        out = []
        for a in range(n):
            for k, (px, py, pc) in enumerate(peers):
                send = pltpu.make_async_remote_copy(src_ref=x_refs[a], dst_ref=out_refs[a].at[4 * x + 2 * y + c],
                                                    send_sem=send_sems.at[4 * a + k], recv_sem=recv_sems.at[4 * a + k],
                                                    device_id=(px, py, pc), device_id_type=MESH)
                recv = pltpu.make_async_remote_copy(src_ref=x_refs[a], dst_ref=out_refs[a].at[4 * px + 2 * py + pc],
                                                    send_sem=send_sems.at[4 * a + k], recv_sem=recv_sems.at[4 * a + k],
                                                    device_id=(px, py, pc), device_id_type=MESH)
                out.append((send, recv))
        return out

    def local(x_refs, out_refs, sems):
        x, y, c = _position()
        return [pltpu.make_async_copy(x_refs[a], out_refs[a].at[4 * x + 2 * y + c], sems[2].at[a]) for a in range(n)]

    def start(x_refs, out_refs, sems):
        for cp in local(x_refs, out_refs, sems):
            cp.start()
        for send, _ in copies(x_refs, out_refs, sems):
            send.start()

    def wait(x_refs, out_refs, sems):
        for send, recv in copies(x_refs, out_refs, sems):
            recv.wait_recv()
            send.wait_send()
        for cp in local(x_refs, out_refs, sems):
            cp.wait()

    return _Stage(shards, [jax.ShapeDtypeStruct((N_DEV,) + a.shape, a.dtype) for a in shards],
                  _dma_sems(4 * n) + [pltpu.SemaphoreType.DMA((n,))], start, wait)


def _gather_sibling_stage(bufs):
    n = len(bufs)

    def copies(out_refs, sems):
        send_sems, recv_sems = sems
        x, y, c = _position()
        out = []
        for a in range(n):
            for j, (px, py) in enumerate([(1 - x, y), (x, 1 - y), (1 - x, 1 - y)]):
                mine, theirs = out_refs[a].at[4 * px + 2 * py + c], out_refs[a].at[4 * px + 2 * py + (1 - c)]
                send = pltpu.make_async_remote_copy(src_ref=mine, dst_ref=mine, send_sem=send_sems.at[3 * a + j],
                                                    recv_sem=recv_sems.at[3 * a + j], device_id=(x, y, 1 - c),
                                                    device_id_type=MESH)
                recv = pltpu.make_async_remote_copy(src_ref=mine, dst_ref=theirs, send_sem=send_sems.at[3 * a + j],
                                                    recv_sem=recv_sems.at[3 * a + j], device_id=(x, y, 1 - c),
                                                    device_id_type=MESH)
                out.append((send, recv))
        return out

    def start(_, out_refs, sems):
        for send, _r in copies(out_refs, sems):
            send.start()

    def wait(_, out_refs, sems):
        for send, recv in copies(out_refs, sems):
            recv.wait_recv()
            send.wait_send()

    return _Stage(bufs, [jax.ShapeDtypeStruct(b.shape, b.dtype) for b in bufs], _dma_sems(3 * n), start, wait,
                  aliases={a: a for a in range(n)})


def _all_gather_small(v, name):
    r, n = v.shape

    def body(x_ref, out_ref, send_sems, recv_sems, local_sem):
        x, y, c = _position()
        me = 4 * x + 2 * y + c
        mine = pltpu.make_async_copy(x_ref, out_ref.at[me], local_sem)
        mine.start()
        flips = [(fx, fy, fc) for fx in (0, 1) for fy in (0, 1) for fc in (0, 1)][1:]
        copies = []
        for k, (fx, fy, fc) in enumerate(flips):
            peer = (x ^ fx, y ^ fy, c ^ fc)
            cp = pltpu.make_async_remote_copy(src_ref=x_ref, dst_ref=out_ref.at[me], send_sem=send_sems.at[k],
                                              recv_sem=recv_sems.at[k], device_id=peer, device_id_type=MESH)
            cp.start()
            copies.append(cp)
        for k, (fx, fy, fc) in enumerate(flips):
            px, py, pc = x ^ fx, y ^ fy, c ^ fc
            src = out_ref.at[4 * px + 2 * py + pc]
            pltpu.make_async_remote_copy(src_ref=x_ref, dst_ref=src, send_sem=send_sems.at[k], recv_sem=recv_sems.at[k],
                                         device_id=(px, py, pc), device_id_type=MESH).wait_recv()
        for cp in copies:
            cp.wait_send()
        mine.wait()

    return _call(body, name, jax.ShapeDtypeStruct((N_DEV, r, n), v.dtype), in_specs=[_VMEM], out_specs=_VMEM,
                 scratch=_dma_sems(7) + [pltpu.SemaphoreType.DMA(())])(v)


def _reduce_sibling_stage(gs):
    n = len(gs)

    def copies(g_refs, out_refs, sems):
        send_sems, recv_sems = sems
        x, y, c = _position()
        return [pltpu.make_async_remote_copy(src_ref=g_refs[a].at[2 * k + (1 - c)], dst_ref=out_refs[a].at[k],
                                             send_sem=send_sems.at[4 * a + k], recv_sem=recv_sems.at[4 * a + k],
                                             device_id=(x, y, 1 - c), device_id_type=MESH)
                for a in range(n) for k in range(4)]

    def start(g_refs, out_refs, sems):
        for cp in copies(g_refs, out_refs, sems):
            cp.start()

    def wait(g_refs, out_refs, sems):
        for cp in copies(g_refs, out_refs, sems):
            cp.wait()

    return _Stage(gs, [jax.ShapeDtypeStruct((4,) + g.shape[1:], g.dtype) for g in gs], _dma_sems(4 * n), start, wait)


def _rows2d(shape):
    cols = shape[-1]
    rows = 1
    for s in shape[:-1]:
        rows *= s
    return rows, cols


def _add_own(g, recv, name):
    rows, cols = _rows2d(g.shape[1:])
    tr = _pick(rows, 256, 8)
    c = lax.axis_index("c").astype(jnp.int32).reshape(1)

    def body(c_ref, g_ref, r_ref, o_ref):
        o_ref[...] = g_ref[...] + r_ref[...]

    grid_spec = pltpu.PrefetchScalarGridSpec(
        num_scalar_prefetch=1, grid=(4, rows // tr),
        in_specs=[pl.BlockSpec((None, None, tr, cols), lambda k, i, c_ref: (k, c_ref[0], i, 0)),
                  pl.BlockSpec((None, tr, cols), lambda k, i, c_ref: (k, i, 0))],
        out_specs=pl.BlockSpec((None, tr, cols), lambda k, i, c_ref: (k, i, 0)))
    return _call(body, name, jax.ShapeDtypeStruct((4, rows, cols), F32), grid_spec=grid_spec,
                 dims=("parallel", "parallel"))(c, g.reshape(4, 2, rows, cols), recv.reshape(4, rows, cols))


def _reduce_chips_stage(pas):
    n = len(pas)

    def copies(pa_refs, out_refs, sems):
        send_sems, recv_sems, _ = sems
        x, y, c = _position()
        my_chip = 2 * x + y
        out = []
        for a in range(n):
            for j, (px, py) in enumerate([(1 - x, y), (x, 1 - y), (1 - x, 1 - y)]):
                send = pltpu.make_async_remote_copy(src_ref=pa_refs[a].at[2 * px + py], dst_ref=out_refs[a].at[my_chip],
                                                    send_sem=send_sems.at[3 * a + j], recv_sem=recv_sems.at[3 * a + j],
                                                    device_id=(px, py, c), device_id_type=MESH)
                recv = pltpu.make_async_remote_copy(src_ref=pa_refs[a].at[2 * px + py], dst_ref=out_refs[a].at[2 * px + py],
                                                    send_sem=send_sems.at[3 * a + j], recv_sem=recv_sems.at[3 * a + j],
                                                    device_id=(px, py, c), device_id_type=MESH)
                out.append((send, recv))
        return out

    def local(pa_refs, out_refs, sems):
        x, y, _ = _position()
        return [pltpu.make_async_copy(pa_refs[a].at[2 * x + y], out_refs[a].at[2 * x + y], sems[2].at[a]) for a in range(n)]

    def start(pa_refs, out_refs, sems):
        for cp in local(pa_refs, out_refs, sems):
            cp.start()
        for send, _r in copies(pa_refs, out_refs, sems):
            send.start()

    def wait(pa_refs, out_refs, sems):
        for send, recv in copies(pa_refs, out_refs, sems):
            recv.wait_recv()
            send.wait_send()
        for cp in local(pa_refs, out_refs, sems):
            cp.wait()

    return _Stage(pas, [jax.ShapeDtypeStruct(pa.shape, pa.dtype) for pa in pas],
                  _dma_sems(3 * n) + [pltpu.SemaphoreType.DMA((n,))], start, wait)


def _adamw_math(w, g, m, v):
    m = ADAM_B1 * m + (1.0 - ADAM_B1) * g
    v = ADAM_B2 * v + (1.0 - ADAM_B2) * (g * g)
    m_hat = m / (1.0 - ADAM_B1 ** ADAM_STEP)
    v_hat = v / (1.0 - ADAM_B2 ** ADAM_STEP)
    delta = -ADAM_LR * (m_hat / (jnp.sqrt(v_hat) + ADAM_EPS) + ADAM_WD * w)
    return delta, m, v


def _sum_adamw(parts, w, m, v, name):
    n, rows, cols = parts.shape
    tr = _pick(rows, 128, 8)

    def body(p_ref, w_ref, m_ref, v_ref, g_ref, d_ref, nm_ref, nv_ref):
        g = p_ref[0]
        for k in range(1, n):
            g = g + p_ref[k]
        d, nm, nv = _adamw_math(w_ref[...], g, m_ref[...], v_ref[...])
        g_ref[...], d_ref[...], nm_ref[...], nv_ref[...] = g, d, nm, nv

    blk = pl.BlockSpec((tr, cols), lambda i: (i, 0))
    return _call(body, name, [jax.ShapeDtypeStruct((rows, cols), F32)] * 4, grid=(rows // tr,),
                 in_specs=[pl.BlockSpec((n, tr, cols), lambda i: (0, i, 0)), blk, blk, blk],
                 out_specs=[blk] * 4, dims=("parallel",))(parts, w, m, v)


_WEIGHTS = ("norm_g", "mem_norm_g", "w_mem_kv", "w_out", "conv_w_in", "conv_dw", "conv_dw_b", "conv_ln_g", "conv_ln_b",
            "mla_w_in", "mla_q_norm_g", "mla_w_uq", "mla_kv_norm_g", "mla_w_ukv", "final_norm_g")


_GATHER_GROUPS = {"a": ("conv_w_in",), "b": ("w_mem_kv", "w_out"), "c": ("mla_w_in", "mla_w_uq", "mla_w_ukv")}
_CARRIERS = {"l0_norm": ("gather chips", ("a",)), "l0_in": ("gather chips", ("b",)), "l0_glu": ("gather sibling", ("b",)),
             "l0_dwconv": ("gather chips", ("c",)), "l0_ln": ("gather sibling", ("c",)),
             "l1_in_dx": ("reduce sibling", ("l1",)), "l0_ln_bwd": ("reduce sibling", ("l0a",)),
             "l0_gate_bwd": ("reduce chips", ("l1", 0, 2)), "l0_dwconv_bwd": ("reduce chips", ("l1", 2, 5)),
             "l0_in_dw": ("reduce chips", ("l0a",)), "l0_in_dx": ("reduce sibling alone, then chips", ("l0b",))}


class _Schedule:
    def __init__(self, w):
        self.w, self.full, self.gather, self.reduce, self.reduced = w, {}, {}, {}, {}
        small = _all_gather_small(_as_tiles([w[n] for n in _SMALL_SHARDED]), "gather_small_weights").reshape(N_DEV, -1)
        o = 0
        for n in _SMALL_SHARDED:
            self.full[n] = _join(n, small[:, o:o + w[n].size].reshape((N_DEV,) + w[n].shape))
            o += w[n].size
        for n in _REPLICATED:
            self.full[n] = w[n]

    def carry(self, call):
        kind, (g, *part) = _CARRIERS[call]
        if kind == "gather chips":
            self.gather[g] = [_gather_chips_stage([self.w[n].astype(BF16) for n in _GATHER_GROUPS[g]])]
            return self.gather[g][0]
        if kind == "gather sibling":
            self.gather[g].append(_gather_sibling_stage(self.gather[g][0].outs))
            return self.gather[g][1]
        r = self.reduce[g]
        if kind == "reduce sibling":
            r["sibling"] = _reduce_sibling_stage(r["cut"])
            return r["sibling"]
        if kind != "reduce chips":
            r["sibling"] = _reduce_sibling_stage(r["cut"])
            _run_stage(r["sibling"], "reduce_sibling_" + g)
        if "partial" not in r:
            r["partial"] = [_add_own(c, s, "reduce_add_%s_%d" % (g, i))
                            for i, (c, s) in enumerate(zip(r["cut"], r["sibling"].outs))]
        lo, hi = part if part else (0, len(r["keys"]))
        stage = _reduce_chips_stage(r["partial"][lo:hi])
        r.setdefault("chips", []).append((r["keys"][lo:hi], stage))
        return stage

    def __getitem__(self, name):
        if name not in self.full:
            g = [k for k, names in _GATHER_GROUPS.items() if name in names][0]
            if len(self.gather[g]) == 1:
                self.gather[g].append(_gather_sibling_stage(self.gather[g][0].outs))
                _run_stage(self.gather[g][1], "gather_sibling_" + g)
            for n, buf in zip(_GATHER_GROUPS[g], self.gather[g][1].outs):
                self.full[n] = _PERM[n][0](_join(n, buf)) if n in _PERM else _join(n, buf)
        return self.full[name]

    def ready(self, group, grads):
        keys, cut, small = [], [], []
        for (n, layer), g in grads.items():
            if n in _SMALL_SHARDED:
                small.append(_cut(n, g, self.w[n].shape).reshape(N_DEV, -1))
                continue
            keys.append((n, layer))
            if layer is not None:
                cut.append(g.reshape((N_DEV,) + self.w[n].shape[1:]))
            else:
                cut.append(_cut(n, _PERM[n][1](g) if n in _PERM else g, self.w[n].shape))
        if small:
            keys.append(("small", None))
            cut.append(jax.vmap(lambda r: _as_tiles([r]))(jnp.concatenate(small, axis=1)))
        self.reduce[group] = {"keys": keys, "cut": cut}

    def finish(self):
        out = {}
        for r in self.reduce.values():
            for keys, stage in r["chips"]:
                out.update(dict(zip(keys, stage.outs)))
        return out


def kernel(x, mem, positions, norm_g, mem_norm_g, w_mem_kv, w_out, conv_w_in, conv_dw, conv_dw_b, conv_ln_g, conv_ln_b, mla_w_in, mla_q_norm_g, mla_w_uq, mla_kv_norm_g, mla_w_ukv, final_norm_g, loss_target, m_norm_g, m_mem_norm_g, m_w_mem_kv, m_w_out, m_conv_w_in, m_conv_dw, m_conv_dw_b, m_conv_ln_g, m_conv_ln_b, m_mla_w_in, m_mla_q_norm_g, m_mla_w_uq, m_mla_kv_norm_g, m_mla_w_ukv, m_final_norm_g, v_norm_g, v_mem_norm_g, v_w_mem_kv, v_w_out, v_conv_w_in, v_conv_dw, v_conv_dw_b, v_conv_ln_g, v_conv_ln_b, v_mla_w_in, v_mla_q_norm_g, v_mla_w_uq, v_mla_kv_norm_g, v_mla_w_ukv, v_final_norm_g):
    w = dict(zip(_WEIGHTS, (norm_g, mem_norm_g, w_mem_kv, w_out, conv_w_in, conv_dw, conv_dw_b, conv_ln_g, conv_ln_b,
                            mla_w_in, mla_q_norm_g, mla_w_uq, mla_kv_norm_g, mla_w_ukv, final_norm_g)))
    m = dict(zip(_WEIGHTS, (m_norm_g, m_mem_norm_g, m_w_mem_kv, m_w_out, m_conv_w_in, m_conv_dw, m_conv_dw_b, m_conv_ln_g,
                            m_conv_ln_b, m_mla_w_in, m_mla_q_norm_g, m_mla_w_uq, m_mla_kv_norm_g, m_mla_w_ukv, m_final_norm_g)))
    v = dict(zip(_WEIGHTS, (v_norm_g, v_mem_norm_g, v_w_mem_kv, v_w_out, v_conv_w_in, v_conv_dw, v_conv_dw_b, v_conv_ln_g,
                            v_conv_ln_b, v_mla_w_in, v_mla_q_norm_g, v_mla_w_uq, v_mla_kv_norm_g, v_mla_w_ukv, v_final_norm_g)))

    sched = _Schedule(w)
    loss_local, dx, G = _forward_backward(x, mem, positions, loss_target, sched)
    loss = lax.psum(loss_local, ("x", "y", "c"))

    from_chips = sched.finish()
    out = [{}, {}, {}, {}]
    for n in _BIG:
        if n in _ROW_CUT:
            res = [_sum_adamw(from_chips[(n, l)], w[n][l], m[n][l], v[n][l], "adamw_%s_%d" % (n, l)) for l in range(w[n].shape[0])]
            res = [jnp.stack(r) for r in zip(*res)]
        else:
            rows, cols = _rows2d(w[n].shape)
            res = _sum_adamw(from_chips[(n, None)], w[n].reshape(rows, cols), m[n].reshape(rows, cols),
                             v[n].reshape(rows, cols), "adamw_" + n)
        for o, r in zip(out, res):
            o[n] = r.reshape(w[n].shape)
    small_like = [w[n] for n in _SMALL_SHARDED]
    res = _sum_adamw(from_chips[("small", None)], _as_tiles(small_like), _as_tiles([m[n] for n in _SMALL_SHARDED]),
                     _as_tiles([v[n] for n in _SMALL_SHARDED]), "adamw_small")
    for o, r in zip(out, res):
        for n, a in zip(_SMALL_SHARDED, _split_flat(r.reshape(-1), small_like)):
            o[n] = a

    rep_like = [w[n] for n in _REPLICATED]
    rep_parts = _all_gather_small(_as_tiles([G[n] for n in _REPLICATED]), "gather_replicated_grads")
    res = _sum_adamw(rep_parts, _as_tiles(rep_like), _as_tiles([m[n] for n in _REPLICATED]),
                     _as_tiles([v[n] for n in _REPLICATED]), "adamw_replicated")
    for o, r in zip(out, res):
        for n, a in zip(_REPLICATED, _split_flat(r.reshape(-1), rep_like)):
            o[n] = a

    return (loss, dx, *[out[0][n] for n in _WEIGHTS], *[out[1][n] for n in _WEIGHTS],
            *[out[2][n] for n in _WEIGHTS], *[out[3][n] for n in _WEIGHTS])
```

```python
import jax
import jax.numpy as jnp
from jax import lax
from jax.experimental import pallas as pl
from jax.experimental.pallas import tpu as pltpu

F32 = jnp.float32
BF16 = jnp.bfloat16
MESH = pl.DeviceIdType.MESH
N_DEV = 8
VMEM_LIMIT_BYTES = 48 * 1024 * 1024

MEM_HEADS, MEM_HEAD_DIM = 4, 128
MEM_WIDTH = MEM_HEADS * MEM_HEAD_DIM
CONV_KERNEL = 31
CONV_PAD = 32
MLA_HEADS, MLA_NOPE, MLA_ROPE, MLA_V = 12, 128, 64, 128
MLA_QK = MLA_NOPE + MLA_ROPE
HALF_ROPE = MLA_ROPE // 2
Q_RANK, KV_RANK = 512, 256
ROPE_THETA = 10000.0
RMS_EPS = 1e-6
LN_EPS = 1e-5
ADAM_LR, ADAM_B1, ADAM_B2, ADAM_EPS, ADAM_WD, ADAM_STEP = 0.001, 0.9, 0.999, 1e-08, 0.01, 10
NEG = -1e30


class _Stage:
    def __init__(self, ins, out_shapes, sems, start, wait, aliases=None):
        self.ins, self.out_shapes, self.sems = list(ins), list(out_shapes), list(sems)
        self.start, self.wait, self.aliases, self.outs = start, wait, dict(aliases or {}), None


def _call(body, name, out_shape, grid=None, in_specs=None, out_specs=None, scratch=(), dims=None, grid_spec=None, aliases=None,
          carry=None):
    params = dict(vmem_limit_bytes=VMEM_LIMIT_BYTES)
    if dims is not None:
        params["dimension_semantics"] = dims
    kw = {}
    if carry is not None:
        single = not isinstance(out_shape, (list, tuple))
        main_out = [out_shape] if single else list(out_shape)
        main_specs = [out_specs] if single else list(out_specs)
        n_in, n_out, n_scr = len(in_specs), len(main_out), len(scratch)
        x_in, x_out = len(carry.ins), len(carry.out_shapes)
        inner, steps = body, tuple(grid)

        def body(*refs):
            ins, xin = refs[:n_in], refs[n_in:n_in + x_in]
            outs = refs[n_in + x_in:n_in + x_in + n_out]
            xout = refs[n_in + x_in + n_out:n_in + x_in + n_out + x_out]
            scr = refs[n_in + x_in + n_out + x_out:n_in + x_in + n_out + x_out + n_scr]
            xsem = refs[n_in + x_in + n_out + x_out + n_scr:]
            ids = [pl.program_id(a) for a in range(len(steps))]
            first, last = ids[0] == 0, ids[0] == steps[0] - 1
            for a in range(1, len(steps)):
                first = jnp.logical_and(first, ids[a] == 0)
                last = jnp.logical_and(last, ids[a] == steps[a] - 1)
            pl.when(first)(lambda: carry.start(xin, xout, xsem))
            inner(*ins, *outs, *scr)
            pl.when(last)(lambda: carry.wait(xin, xout, xsem))

        hbm = pl.BlockSpec(memory_space=pltpu.HBM)
        aliases = dict(aliases or {})
        aliases.update({n_in + k: n_out + v for k, v in carry.aliases.items()})
        res = _call(body, name, main_out + carry.out_shapes, grid=grid, in_specs=list(in_specs) + [hbm] * x_in,
                    out_specs=main_specs + [hbm] * x_out, scratch=list(scratch) + carry.sems, dims=dims, aliases=aliases)

        def run(*args):
            outs = res(*args, *carry.ins)
            carry.outs = list(outs[n_out:])
            return outs[0] if single else outs[:n_out]

        return run
    if aliases:
        kw["input_output_aliases"] = aliases
    if grid_spec is not None:
        kw["grid_spec"] = grid_spec
    else:
        if grid is not None:
            kw["grid"] = grid
        kw["in_specs"] = in_specs
        kw["out_specs"] = out_specs
        kw["scratch_shapes"] = list(scratch)
    return pl.pallas_call(body, name=name, out_shape=out_shape, compiler_params=pltpu.CompilerParams(**params), **kw)


def _pick(n, target, mult):
    best = None
    for d in range(mult, min(n, target) + 1, mult):
        if n % d == 0:
            best = d
    return n if best is None else best


_DOT_DIMS = {"nn": (((1,), (0,)), ((), ())), "nt": (((1,), (1,)), ((), ())), "tn": (((0,), (0,)), ((), ()))}


def _mm(a, b, mode, out_dtype, name, res=None, carry=None):
    if mode == "tn":
        a, mode = a.T, "nn"
    if mode == "nn":
        (M, K), N = a.shape, b.shape[1]
    else:
        (M, K), N = a.shape, b.shape[0]
    tm = _pick(M, 1024, 8)
    tn = _pick(N, 512, 128)
    tk = _pick(K, 1024, 128)
    nk = K // tk
    has_res = res is not None

    def body(*refs):
        if has_res:
            a_ref, b_ref, r_ref, o_ref, acc = refs
        else:
            a_ref, b_ref, o_ref, acc = refs
        k = pl.program_id(2)

        @pl.when(k == 0)
        def _():
            acc[...] = jnp.zeros_like(acc)

        acc[...] += lax.dot_general(a_ref[...].astype(BF16), b_ref[...].astype(BF16), _DOT_DIMS[mode],
                                    preferred_element_type=F32)

        @pl.when(k == nk - 1)
        def _():
            r = acc[...]
            if has_res:
                r = r + r_ref[...]
            o_ref[...] = r.astype(o_ref.dtype)

    a_spec = pl.BlockSpec((tm, tk), lambda i, j, k: (i, k))
    b_spec = {"nn": pl.BlockSpec((tk, tn), lambda i, j, k: (k, j)),
              "nt": pl.BlockSpec((tn, tk), lambda i, j, k: (j, k))}[mode]
    o_spec = pl.BlockSpec((tm, tn), lambda i, j, k: (i, j))
    in_specs = [a_spec, b_spec] + ([o_spec] if has_res else [])
    args = (a, b) + ((res,) if has_res else ())
    return _call(body, name, jax.ShapeDtypeStruct((M, N), out_dtype), grid=(M // tm, N // tn, nk),
                 in_specs=in_specs, out_specs=o_spec, scratch=[pltpu.VMEM((tm, tn), F32)],
                 dims=("parallel", "parallel", "arbitrary"), carry=carry)(*args)


def _views(rows):
    return [r if isinstance(r, tuple) else (r, r.shape[1], 0) for r in rows]


def _rowwise(f, rows, params, outs, name, tb=256, carry=None, into=None):
    rows = _views(rows)
    T = rows[0][0].shape[0]
    tb = min(tb, T)
    nr, npar = len(rows), len(params)
    outs = [o if len(o) == 3 else (o[0], o[1], o[0]) for o in outs]
    into = into or []

    def body(*refs):
        vals = f(*[r[...].astype(F32) for r in refs[:nr]], *[p[...] for p in refs[nr:nr + npar]])
        for o_ref, v in zip(refs[nr + npar + len(into):], vals):
            o_ref[...] = v.astype(o_ref.dtype)

    row_spec = lambda w, cb=0: pl.BlockSpec((tb, w), lambda i: (i, cb))
    par_spec = lambda w: pl.BlockSpec((1, w), lambda i: (0, 0))
    out_shape = [jax.ShapeDtypeStruct((T, tw), dt) for _, dt, tw in outs]
    out_specs = [row_spec(w) for w, _, _ in outs]
    in_specs = [row_spec(w, cb) for _, w, cb in rows] + [par_spec(p.shape[1]) for p in params]
    args = [r[0] for r in rows] + list(params)
    aliases = {}
    for k, arr, cb in into:
        aliases[len(args)] = k
        in_specs.append(pl.BlockSpec(memory_space=pl.ANY))
        args.append(arr)
        out_shape[k] = jax.ShapeDtypeStruct(arr.shape, arr.dtype)
        out_specs[k] = row_spec(outs[k][0], cb)
    return _call(body, name, out_shape, grid=(T // tb,), in_specs=in_specs, out_specs=out_specs, dims=("parallel",),
                 carry=carry, aliases=aliases)(*args)


def _rowwise_bwd(f, rows, params, douts, n_diff, name, tb=256, carry=None, add=None, into=None):
    rows, douts = _views(rows), _views(douts)
    T = rows[0][0].shape[0]
    tb = min(tb, T)
    nr, npar, nd = len(rows), len(params), len(douts)
    n_add = 0 if add is None else 1

    def body(*refs):
        rv = [r[...].astype(F32) for r in refs[:nr]]
        pv = [p[...] for p in refs[nr:nr + npar]]
        dv = [d[...].astype(F32) for d in refs[nr + npar:nr + npar + nd]]
        o_refs = refs[nr + npar + nd + n_add + (0 if into is None else 1):]
        fixed = rv[n_diff:]

        def g(*xs):
            return tuple(f(*xs[:n_diff], *fixed, *xs[n_diff:]))

        _, vjp = jax.vjp(g, *rv[:n_diff], *pv)
        grads = list(vjp(tuple(dv)))
        if add is not None:
            grads[0] = grads[0] + refs[nr + npar + nd][...]
        for o_ref, gr in zip(o_refs[:n_diff], grads[:n_diff]):
            o_ref[...] = gr.astype(o_ref.dtype)
        first = pl.program_id(0) == 0
        for o_ref, gr in zip(o_refs[n_diff:], grads[n_diff:]):
            @pl.when(first)
            def _(o_ref=o_ref):
                o_ref[...] = jnp.zeros_like(o_ref)

            o_ref[...] += gr

    row_spec = lambda w, cb=0: pl.BlockSpec((tb, w), lambda i: (i, cb))
    par_spec = lambda w: pl.BlockSpec((1, w), lambda i: (0, 0))
    out_shape = ([jax.ShapeDtypeStruct((T, w), F32) for _, w, _ in rows[:n_diff]]
                 + [jax.ShapeDtypeStruct((1, p.shape[1]), F32) for p in params])
    out_specs = [row_spec(w) for _, w, _ in rows[:n_diff]] + [par_spec(p.shape[1]) for p in params]
    in_specs = ([row_spec(w, cb) for _, w, cb in rows] + [par_spec(p.shape[1]) for p in params]
                + [row_spec(w, cb) for _, w, cb in douts])
    args = [r[0] for r in rows] + list(params) + [d[0] for d in douts]
    aliases = None
    if add is not None:
        in_specs.append(row_spec(add.shape[1]))
        args.append(add)
    if into is not None:
        aliases = {len(args): 0}
        in_specs.append(pl.BlockSpec(memory_space=pl.ANY))
        args.append(into[0])
        out_shape[0] = jax.ShapeDtypeStruct(into[0].shape, into[0].dtype)
        out_specs[0] = row_spec(rows[0][1], into[1])
    return _call(body, name, out_shape, grid=(T // tb,), in_specs=in_specs, out_specs=out_specs,
                 dims=("arbitrary",), carry=carry, aliases=aliases)(*args)


def _sig(x):
    return 1.0 / (1.0 + jnp.exp(-x))


def _rms(x, g):
    return x * lax.rsqrt(jnp.mean(x * x, axis=-1, keepdims=True) + RMS_EPS) * g


def _f_rms(x, g):
    return (_rms(x, g),)


def _f_glu(a, gate):
    return (a * _sig(gate),)


def _f_ln_silu(x, g, b):
    mu = jnp.mean(x, axis=-1, keepdims=True)
    xc = x - mu
    var = jnp.mean(xc * xc, axis=-1, keepdims=True)
    y = xc * lax.rsqrt(var + LN_EPS) * g + b
    return (y * _sig(y),)


def _rope128(x, cos_p, sin_p):
    return x * cos_p + pltpu.roll(x, 64, 1) * sin_p


def _rope128_t(d, cos_p, sin_p):
    return d * cos_p + pltpu.roll(d * sin_p, 64, 1)


def _f_rope(xq, xk, cos_p, sin_p):
    heads = [_rope128(xq[:, h * 128:(h + 1) * 128], cos_p, sin_p) for h in range(MLA_HEADS)]
    return (jnp.concatenate(heads, axis=1), _rope128(xk, cos_p, sin_p))


def _f_rope_t(dq, dk_heads, cos_p, sin_p):
    heads = [_rope128_t(dq[:, h * 128:(h + 1) * 128], cos_p, sin_p) for h in range(MLA_HEADS)]
    dk = dk_heads[:, 0:128]
    for h in range(1, MLA_HEADS):
        dk = dk + dk_heads[:, h * 128:(h + 1) * 128]
    return (jnp.concatenate(heads, axis=1), _rope128_t(dk, cos_p, sin_p))


GATE_LANES = 256


def _gate_fwd(ycat, proj, z_col, name, tb=1024):
    T, width = ycat.shape
    zb = z_col // GATE_LANES

    def body(y_ref, z_ref, o_ref):
        z = z_ref[...]
        o_ref[...] = (y_ref[...] * (z * _sig(z))).astype(o_ref.dtype)

    blk = pl.BlockSpec((tb, GATE_LANES), lambda i, c: (i, c))
    return _call(body, name, jax.ShapeDtypeStruct((T, width), BF16), grid=(T // tb, width // GATE_LANES),
                 in_specs=[blk, pl.BlockSpec((tb, GATE_LANES), lambda i, c: (i, zb + c))], out_specs=blk,
                 dims=("parallel", "parallel"))(ycat, proj)


def _gate_bwd(ycat, proj, z_col, dy, name, tb=1024, carry=None):
    T, width = ycat.shape
    zb = z_col // GATE_LANES

    def body(y_ref, z_ref, dy_ref, dycat_ref, dz_ref):
        z, d = z_ref[...], dy_ref[...]
        s = _sig(z)
        dycat_ref[...] = d * (z * s)
        dz_ref[...] = (d * y_ref[...] * (s * (1.0 + z * (1.0 - s)))).astype(dz_ref.dtype)

    blk = pl.BlockSpec((tb, GATE_LANES), lambda i, c: (i, c))
    zblk = pl.BlockSpec((tb, GATE_LANES), lambda i, c: (i, zb + c))
    return _call(body, name, [jax.ShapeDtypeStruct((T, width), F32), jax.ShapeDtypeStruct(proj.shape, BF16)],
                 grid=(T // tb, width // GATE_LANES), in_specs=[blk, zblk, blk], out_specs=[blk, zblk],
                 dims=("parallel", "parallel"), carry=carry)(ycat, proj, dy)


def _glu_bwd(proj, d_glu, d_proj, name, tb=256):
    T, w = d_glu.shape

    def body(a_ref, g_ref, d_ref, _, o_ref):
        s = _sig(g_ref[...])

        @pl.when(pl.program_id(1) == 0)
        def _():
            o_ref[...] = (d_ref[...] * s).astype(o_ref.dtype)

        @pl.when(pl.program_id(1) == 1)
        def _():
            o_ref[...] = (d_ref[...] * a_ref[...] * (s * (1.0 - s))).astype(o_ref.dtype)

    return _call(body, name, jax.ShapeDtypeStruct(d_proj.shape, d_proj.dtype), grid=(T // tb, 2),
                 in_specs=[pl.BlockSpec((tb, w), lambda i, c: (i, 0)), pl.BlockSpec((tb, w), lambda i, c: (i, 1)),
                           pl.BlockSpec((tb, w), lambda i, c: (i, 0)), pl.BlockSpec(memory_space=pl.ANY)],
                 out_specs=pl.BlockSpec((tb, w), lambda i, c: (i, c)), dims=("parallel", "arbitrary"),
                 aliases={3: 0})(proj, proj, d_glu, d_proj)


def _final_loss(h, tgt, g, name, tb=256):
    T, D = h.shape

    def body(h_ref, t_ref, g_ref, dh_ref, dg_ref, loss_ref):
        tv = t_ref[...]

        def rowloss(hh, gg):
            e = _rms(hh, gg) - tv
            return 0.5 * jnp.mean(e * e, axis=-1, keepdims=True)

        lr, vjp = jax.vjp(rowloss, h_ref[...], g_ref[...])
        dh, dg = vjp(jnp.ones_like(lr))
        dh_ref[...] = dh

        @pl.when(pl.program_id(0) == 0)
        def _():
            dg_ref[...] = jnp.zeros_like(dg_ref)
            loss_ref[...] = jnp.zeros_like(loss_ref)

        dg_ref[...] += dg
        loss_ref[...] += jnp.broadcast_to(jnp.sum(lr, axis=0, keepdims=True), loss_ref.shape)

    row = pl.BlockSpec((tb, D), lambda i: (i, 0))
    par = pl.BlockSpec((1, D), lambda i: (0, 0))
    return _call(body, name,
                 [jax.ShapeDtypeStruct((T, D), F32), jax.ShapeDtypeStruct((1, D), F32), jax.ShapeDtypeStruct((1, 128), F32)],
                 grid=(T // tb,), in_specs=[row, row, par],
                 out_specs=[row, par, pl.BlockSpec((1, 128), lambda i: (0, 0))], dims=("arbitrary",))(h, tgt, g)


CONV_ROWS = 128
CONV_LANES = 256


def _sublane_phases(pad, n):
    for r in range(1, 8):
        for c0 in range(0, n - 8, 256):
            rows = min(256, n - 8 - c0)
            pad[r, c0:c0 + rows, :] = pad[0, c0 + r:c0 + r + rows, :]


def _dwconv_fwd(x, w, b, name, carry=None):
    B, S, C = x.shape
    cb = CONV_LANES
    off = CONV_PAD - (CONV_KERNEL - 1)

    def body(x_ref, w_ref, b_ref, o_ref, pad):
        pad[0, 0:CONV_PAD, :] = jnp.zeros((CONV_PAD, cb), F32)
        pad[0, CONV_PAD:, :] = x_ref[...]
        _sublane_phases(pad, S + CONV_PAD)
        for t0 in range(0, S, CONV_ROWS):
            acc = jnp.broadcast_to(b_ref[...], (CONV_ROWS, cb))
            for k in range(CONV_KERNEL):
                r, base = (off + k) % 8, t0 + (off + k) // 8 * 8
                acc = acc + w_ref[k:k + 1, :] * pad[r, base:base + CONV_ROWS, :]
            o_ref[t0:t0 + CONV_ROWS, :] = acc

    return _call(body, name, jax.ShapeDtypeStruct((B, S, C), F32), grid=(B, C // cb),
                 in_specs=[pl.BlockSpec((None, S, cb), lambda i, j: (i, 0, j)),
                           pl.BlockSpec((CONV_KERNEL, cb), lambda i, j: (0, j)),
                           pl.BlockSpec((1, cb), lambda i, j: (0, j))],
                 out_specs=pl.BlockSpec((None, S, cb), lambda i, j: (i, 0, j)),
                 scratch=[pltpu.VMEM((8, S + CONV_PAD, cb), F32)], dims=("parallel", "parallel"), carry=carry)(x, w, b)


def _dwconv_bwd(x, w, dy, name, carry=None):
    B, S, C = x.shape
    cb = CONV_LANES
    off = CONV_PAD - (CONV_KERNEL - 1)
    groups = CONV_ROWS // 8

    def body(x_ref, w_ref, dy_ref, dx_ref, dw_ref, db_ref, dypad, wacc):
        dypad[0, 0:S, :] = dy_ref[...]
        dypad[0, S:, :] = jnp.zeros((CONV_PAD, cb), F32)
        _sublane_phases(dypad, S + CONV_PAD)
        wacc[...] = jnp.zeros_like(wacc)
        for t0 in range(0, S, CONV_ROWS):
            xc = x_ref[t0:t0 + CONV_ROWS, :]
            acc = jnp.zeros((CONV_ROWS, cb), F32)
            for k in range(CONV_KERNEL):
                o = (CONV_KERNEL - 1) - k
                dys = dypad[o % 8, t0 + o // 8 * 8:t0 + o // 8 * 8 + CONV_ROWS, :]
                acc = acc + w_ref[k:k + 1, :] * dys
                wacc[k] += jnp.sum((dys * xc).reshape(groups, 8, cb), axis=0)
            wacc[CONV_KERNEL] += jnp.sum(dy_ref[t0:t0 + CONV_ROWS, :].reshape(groups, 8, cb), axis=0)
            dx_ref[t0:t0 + CONV_ROWS, :] = acc

        @pl.when(pl.program_id(1) == 0)
        def _():
            dw_ref[...] = jnp.zeros_like(dw_ref)
            db_ref[...] = jnp.zeros_like(db_ref)

        for k in range(CONV_KERNEL):
            dw_ref[k:k + 1, :] += jnp.sum(wacc[k], axis=0, keepdims=True)
        db_ref[...] += jnp.sum(wacc[CONV_KERNEL], axis=0, keepdims=True)

    blk = pl.BlockSpec((None, S, cb), lambda j, i: (i, 0, j))
    return _call(body, name,
                 [jax.ShapeDtypeStruct((B, S, C), F32), jax.ShapeDtypeStruct((CONV_KERNEL, C), F32),
                  jax.ShapeDtypeStruct((1, C), F32)],
                 grid=(C // cb, B),
                 in_specs=[blk, pl.BlockSpec((CONV_KERNEL, cb), lambda j, i: (0, j)), blk],
                 out_specs=[blk, pl.BlockSpec((CONV_KERNEL, cb), lambda j, i: (0, j)),
                            pl.BlockSpec((1, cb), lambda j, i: (0, j))],
                 scratch=[pltpu.VMEM((8, S + CONV_PAD, cb), F32), pltpu.VMEM((CONV_KERNEL + 1, 8, cb), F32)],
                 dims=("parallel", "arbitrary"), carry=carry)(x, w, dy)


ATTN_TILE = {"fwd": 1024, "bwd": 1024, "cross fwd": 512}
ATTN_SUB = {"fwd": 256, "bwd": 512}


def _attn_shapes(Sq, Sk, causal, pass_):
    tq = min(Sq, ATTN_TILE[pass_ if causal or pass_ == "bwd" else "cross fwd"])
    tk = tq if causal else min(Sk, ATTN_TILE[pass_])
    return tq, tk, min(ATTN_SUB[pass_], tq)


def _mask(row0, col0, rows, cols):
    r = row0 + lax.broadcasted_iota(jnp.int32, (rows, cols), 0)
    c = col0 + lax.broadcasted_iota(jnp.int32, (rows, cols), 1)
    return c <= r


def _attn_fwd(q, q_c0, qr, k, k_c0, kr, v, v_c0, B, Sq, Sk, H, causal, scale, name, into=None, o_c0=0, o_width=None,
              kv_stride=1):
    tq, tk, sub = _attn_shapes(Sq, Sk, causal, "fwd")
    nq, nk, nsub = Sq // tq, Sk // tk, tq // sub
    rope = qr is not None

    def body(*refs):
        refs = list(refs)
        qn_ref = refs.pop(0)
        qr_ref = refs.pop(0) if rope else None
        kn_ref = refs.pop(0)
        kr_ref = refs.pop(0) if rope else None
        v_ref = refs.pop(0)
        if into is not None:
            refs.pop(0)
        o_ref, lse_ref, m_s, l_s, acc = refs
        qi = pl.program_id(2)
        m_s[...] = jnp.full_like(m_s, NEG)
        l_s[...] = jnp.zeros_like(l_s)
        acc[...] = jnp.zeros_like(acc)
        qs = []
        for r in range(nsub):
            qn = qn_ref[r * sub:(r + 1) * sub, :].astype(BF16)
            qs.append(jnp.concatenate([qn, qr_ref[r * sub:(r + 1) * sub, :]], axis=1) if rope else qn)

        def step(j, masked):
            ks = pl.ds(pl.multiple_of(j * tk, tk), tk)
            kk = jnp.concatenate([kn_ref[ks, :], kr_ref[ks, :]], axis=1) if rope else kn_ref[ks, :]
            vv = v_ref[ks, :]
            for r in range(nsub):
                rows = slice(r * sub, (r + 1) * sub)
                nc = (r + 1) * sub if masked else tk
                s = lax.dot_general(qs[r], kk[:nc], _DOT_DIMS["nt"], preferred_element_type=F32) * scale
                if masked:
                    s = jnp.where(_mask(qi * tq + r * sub, j * tk, sub, nc), s, NEG)
                m_old = m_s[rows, :]
                m_new = jnp.maximum(m_old, jnp.max(s, axis=-1, keepdims=True))
                p = jnp.exp(s - m_new)
                alpha = jnp.exp(m_old - m_new)
                l_s[rows, :] = alpha * l_s[rows, :] + jnp.sum(p, axis=-1, keepdims=True)
                acc[rows, :] = alpha * acc[rows, :] + jnp.dot(p.astype(BF16), vv[:nc], preferred_element_type=F32)
                m_s[rows, :] = m_new

        def unmasked(j, carry):
            step(j, False)
            return carry

        if causal:
            lax.fori_loop(0, qi, unmasked, 0)
            step(qi, True)
        else:
            lax.fori_loop(0, nk, unmasked, 0)
        o_ref[...] = (acc[...] / l_s[...]).astype(o_ref.dtype)
        lse_ref[...] = m_s[...] + jnp.log(l_s[...])

    qspec = lambda c0: pl.BlockSpec((tq, 128), lambda b, h, i: (b * nq + i, c0 + h))
    kspec = lambda c0: pl.BlockSpec((Sk, 128), lambda b, h, i: (b, c0 + kv_stride * h))
    in_specs, args = [qspec(q_c0)], [q]
    if rope:
        in_specs.append(qspec(0)); args.append(qr)
    in_specs.append(kspec(k_c0)); args.append(k)
    if rope:
        in_specs.append(pl.BlockSpec((Sk, 128), lambda b, h, i: (b, 0))); args.append(kr)
    in_specs.append(kspec(v_c0)); args.append(v)
    aliases = {}
    if into is not None:
        aliases = {len(args): 0}
        in_specs.append(pl.BlockSpec(memory_space=pl.ANY)); args.append(into)
        o_shape = jax.ShapeDtypeStruct(into.shape, into.dtype)
    else:
        o_shape = jax.ShapeDtypeStruct((B * Sq, o_width), F32)
    return _call(body, name, [o_shape, jax.ShapeDtypeStruct((B * H, Sq, 1), F32)], grid=(B, H, nq), in_specs=in_specs,
                 out_specs=[qspec(o_c0), pl.BlockSpec((None, tq, 1), lambda b, h, i: (b * H + h, i, 0))],
                 scratch=[pltpu.VMEM((tq, 1), F32), pltpu.VMEM((tq, 1), F32), pltpu.VMEM((tq, 128), F32)],
                 dims=("parallel", "parallel", "arbitrary"), aliases=aliases)(*args)


def _attn_bwd(q, q_c0, qr, k, k_c0, kr, v, v_c0, o, do, o_c0, lse, B, Sq, Sk, H, causal, scale, name, dq_into=None,
              kv_stride=1):
    tq, tk, sub = _attn_shapes(Sq, Sk, causal, "bwd")
    nq, nk, nsub = Sq // tq, Sk // tk, tq // sub
    rope = qr is not None
    dk_w = 256 if rope else 128

    def body(*refs):
        refs = list(refs)
        qn_ref = refs.pop(0)
        qr_ref = refs.pop(0) if rope else None
        kn_ref = refs.pop(0)
        kr_ref = refs.pop(0) if rope else None
        v_ref, o_ref, do_ref, lse_ref = refs[:4]
        refs = refs[4 + (0 if dq_into is None else 1):]
        dqn_ref = refs.pop(0)
        dqr_ref = refs.pop(0) if rope else None
        dkn_ref = refs.pop(0)
        dkr_ref = refs.pop(0) if rope else None
        dv_ref = None if rope else refs.pop(0)
        q_s, do_s, dl_s, dq_acc, dk_acc, dv_acc = refs
        kj = pl.program_id(2)

        @pl.when(kj == 0)
        def _():
            qn = qn_ref[...].astype(BF16)
            q_s[...] = jnp.concatenate([qn, qr_ref[...]], axis=1) if rope else qn
            dof = do_ref[...]
            do_s[...] = dof.astype(BF16)
            dl_s[...] = jnp.sum(dof * o_ref[...], axis=-1, keepdims=True)
            dq_acc[...] = jnp.zeros_like(dq_acc)

        kk = jnp.concatenate([kn_ref[...], kr_ref[...]], axis=1) if rope else kn_ref[...]
        vv = v_ref[...]
        dk_acc[...] = jnp.zeros_like(dk_acc)
        dv_acc[...] = jnp.zeros_like(dv_acc)

        def step(i, masked):
            for r in range(nsub):
                rows = pl.ds(pl.multiple_of(i * tq + r * sub, sub), sub)
                qq, dob = q_s[rows, :], do_s[rows, :]
                nc = (r + 1) * sub if masked else tk
                kc, vc = kk[:nc], vv[:nc]
                s = lax.dot_general(qq, kc, _DOT_DIMS["nt"], preferred_element_type=F32) * scale
                if masked:
                    s = jnp.where(_mask(i * tq + r * sub, kj * tk, sub, nc), s, NEG)
                p = jnp.exp(s - lse_ref[rows, :])
                dp = lax.dot_general(dob, vc, _DOT_DIMS["nt"], preferred_element_type=F32)
                ds = (p * (dp - dl_s[rows, :]) * scale).astype(BF16)
                dv_acc[0:nc, :] += lax.dot_general(p.astype(BF16), dob, _DOT_DIMS["tn"], preferred_element_type=F32)
                dk_acc[0:nc, :] += lax.dot_general(ds, qq, _DOT_DIMS["tn"], preferred_element_type=F32)
                dq_acc[rows, :] += jnp.dot(ds, kc, preferred_element_type=F32)

        def unmasked(i, carry):
            step(i, False)
            return carry

        if causal:
            step(kj, True)
            lax.fori_loop(kj + 1, nq, unmasked, 0)
        else:
            lax.fori_loop(0, nq, unmasked, 0)
        if rope:
            dkn_ref[...] = jnp.concatenate([dk_acc[:, 0:128], dv_acc[...]], axis=1).astype(dkn_ref.dtype)
            dkr_ref[...] = dk_acc[:, 128:256]
        else:
            dkn_ref[...] = dk_acc[...]
            dv_ref[...] = dv_acc[...]

        @pl.when(kj == nk - 1)
        def _():
            dqn_ref[...] = dq_acc[:, 0:128].astype(dqn_ref.dtype)
            if rope:
                dqr_ref[...] = dq_acc[:, 128:256]

    qspec = lambda c0: pl.BlockSpec((Sq, 128), lambda b, h, j: (b, c0 + h))
    kspec = lambda c0: pl.BlockSpec((tk, 128), lambda b, h, j: (b * nk + j, c0 + kv_stride * h))
    in_specs, args = [qspec(q_c0)], [q]
    if rope:
        in_specs.append(qspec(0)); args.append(qr)
    in_specs.append(kspec(k_c0)); args.append(k)
    if rope:
        in_specs.append(pl.BlockSpec((tk, 128), lambda b, h, j: (b * nk + j, 0))); args.append(kr)
    in_specs += [kspec(v_c0), qspec(o_c0), qspec(o_c0), pl.BlockSpec((None, Sq, 1), lambda b, h, j: (b * H + h, 0, 0))]
    args += [v, o, do, lse]
    h_rows_q = jax.ShapeDtypeStruct((B * Sq, H * 128), F32)
    h_rows_k = jax.ShapeDtypeStruct((B * Sk, H * 128), F32)
    out_shape, out_specs, aliases = [h_rows_q], [qspec(0)], None
    if rope:
        out_shape = [jax.ShapeDtypeStruct((B * Sq, 2 * H * 128), BF16)]
    if dq_into is not None:
        aliases = {len(args): 0}
        in_specs.append(pl.BlockSpec(memory_space=pl.ANY)); args.append(dq_into[0])
        out_shape, out_specs = [jax.ShapeDtypeStruct(dq_into[0].shape, dq_into[0].dtype)], [qspec(dq_into[1])]
    if rope:
        out_shape.append(h_rows_q); out_specs.append(qspec(0))
    hspec = lambda w: pl.BlockSpec((tk, w), lambda b, h, j: (b * nk + j, h))
    if rope:
        out_shape += [jax.ShapeDtypeStruct((B * Sk, H * 256), BF16), h_rows_k]
        out_specs += [hspec(256), hspec(128)]
    else:
        out_shape += [h_rows_k, h_rows_k]
        out_specs += [hspec(128), hspec(128)]
    return _call(body, name, out_shape, grid=(B, H, nk), in_specs=in_specs, out_specs=out_specs,
                 scratch=[pltpu.VMEM((Sq, dk_w), BF16), pltpu.VMEM((Sq, 128), BF16), pltpu.VMEM((Sq, 1), F32),
                          pltpu.VMEM((Sq, dk_w), F32), pltpu.VMEM((tk, dk_w), F32), pltpu.VMEM((tk, 128), F32)],
                 dims=("parallel", "parallel", "arbitrary"), aliases=aliases)(*args)


def _mem_attention_fwd(proj, q_col, ycat, mem2, mem_g, w_mem, B, S, tag):
    M = mem2.shape[0] // B
    (memn,) = _rowwise(_f_rms, [mem2], [mem_g], [(mem2.shape[1], BF16)], tag + "_memnorm")
    kvm = _mm(memn, w_mem, "nn", BF16, tag + "_memkv")
    o_c0 = ycat.shape[1] // 128 - MEM_HEADS
    ycat, lse = _attn_fwd(proj, q_col // 128, None, kvm, 0, None, kvm, MEM_HEADS, B, S, M, MEM_HEADS, False,
                          MEM_HEAD_DIM ** -0.5, tag + "_memattn", into=ycat, o_c0=o_c0)
    return ycat, (memn, kvm, lse)


def _mem_attention_bwd(proj, q_col, ycat, d_ycat, d_proj, saved, mem2, mem_g, w_mem, B, S, tag):
    memn, kvm, lse = saved
    M = mem2.shape[0] // B
    o_c0 = ycat.shape[1] // 128 - MEM_HEADS
    d_q, d_k, d_v = _attn_bwd(proj, q_col // 128, None, kvm, 0, None, kvm, MEM_HEADS, ycat, d_ycat, o_c0, lse, B, S, M,
                              MEM_HEADS, False, MEM_HEAD_DIM ** -0.5, tag + "_memattn_bwd", dq_into=(d_proj, q_col // 128))
    d_kvm = jnp.concatenate([d_k, d_v], axis=1).astype(BF16)
    d_w_mem = _mm(memn, d_kvm, "tn", F32, tag + "_memkv_dw")
    d_memn = _mm(d_kvm, w_mem, "nt", F32, tag + "_memkv_dx")
    _, d_mem_g = _rowwise_bwd(_f_rms, [mem2], [mem_g], [d_memn], 1, tag + "_memnorm_bwd")
    return d_q, d_w_mem, d_mem_g


def _rope_tables(positions):
    inv_freq = 1.0 / (ROPE_THETA ** (jnp.arange(0, MLA_ROPE, 2, dtype=F32) / MLA_ROPE))
    ang = positions.astype(F32).reshape(-1, 1) * inv_freq
    cos, sin, zero = jnp.cos(ang), jnp.sin(ang), jnp.zeros_like(ang)
    return jnp.concatenate([cos, zero, cos, zero], axis=1), jnp.concatenate([-sin, zero, sin, zero], axis=1)


def _forward_backward(x, mem, positions, target, W):
    B, S, D = x.shape
    T = B * S
    conv_w = W["conv_dw"].shape[1]
    mix_w = 2 * D
    h0 = x.reshape(T, D)
    mem2 = mem.reshape(-1, D)
    tgt = target.reshape(T, D)
    row = lambda v: v.reshape(1, -1)
    n_nope = MLA_HEADS * MLA_NOPE

    g0 = row(W["norm_g"][0])
    (u0,) = _rowwise(_f_rms, [h0], [g0], [(D, BF16)], "l0_norm", carry=W.carry("l0_norm"))
    proj0 = _mm(u0, W["conv_w_in"], "nn", F32, "l0_in", carry=W.carry("l0_in"))
    a0, gate0 = (proj0, conv_w, 0), (proj0, conv_w, 1)
    qm0_col, z0_col = 2 * conv_w, 2 * conv_w + MEM_WIDTH
    (glu,) = _rowwise(_f_glu, [a0, gate0], [], [(conv_w, F32)], "l0_glu", carry=W.carry("l0_glu"))
    dw, dwb = W["conv_dw"], row(W["conv_dw_b"][0])
    cv = _dwconv_fwd(glu.reshape(B, S, conv_w), dw, dwb, "l0_dwconv", carry=W.carry("l0_dwconv")).reshape(T, conv_w)
    ln_g, ln_b = row(W["conv_ln_g"][0]), row(W["conv_ln_b"][0])
    (ycat0,) = _rowwise(_f_ln_silu, [cv], [ln_g, ln_b], [(conv_w, F32, mix_w)], "l0_ln", carry=W.carry("l0_ln"))
    mg0 = row(W["mem_norm_g"][0])
    ycat0, mem_saved0 = _mem_attention_fwd(proj0, qm0_col, ycat0, mem2, mg0, W["w_mem_kv"][0], B, S, "l0")
    y0 = _gate_fwd(ycat0, proj0, z0_col, "l0_gate")
    h1 = _mm(y0, W["w_out"][0], "nn", F32, "l0_out", res=h0)

    g1 = row(W["norm_g"][1])
    (u1,) = _rowwise(_f_rms, [h1], [g1], [(D, BF16)], "l1_norm")
    proj1 = _mm(u1, W["mla_w_in"], "nn", F32, "l1_in")
    cq, ckv = (proj1, Q_RANK, 0), (proj1, KV_RANK, Q_RANK // KV_RANK)
    qm1_col = Q_RANK + KV_RANK
    z1_col = qm1_col + MEM_WIDTH
    kr_col = z1_col + mix_w
    qg, kvg = row(W["mla_q_norm_g"]), row(W["mla_kv_norm_g"])
    (cqn,) = _rowwise(_f_rms, [cq], [qg], [(Q_RANK, BF16)], "l1_qnorm")
    (ckvn,) = _rowwise(_f_rms, [ckv], [kvg], [(KV_RANK, BF16)], "l1_kvnorm")
    qf = _mm(cqn, W["mla_w_uq"], "nn", F32, "l1_uq")
    kvf = _mm(ckvn, W["mla_w_ukv"], "nn", BF16, "l1_ukv")
    cos_p, sin_p = _rope_tables(positions)
    qr, kr = _rowwise(_f_rope, [(qf, n_nope, 1), (proj1, 128, kr_col // 128), cos_p, sin_p], [],
                      [(n_nope, BF16), (128, BF16)], "l1_rope")
    scale1 = MLA_QK ** -0.5
    ycat1, lse1 = _attn_fwd(qf, 0, qr, kvf, 0, kr, kvf, 1, B, S, S, MLA_HEADS, True, scale1, "l1_attn",
                            o_width=mix_w, kv_stride=2)
    mg1 = row(W["mem_norm_g"][1])
    ycat1, mem_saved1 = _mem_attention_fwd(proj1, qm1_col, ycat1, mem2, mg1, W["w_mem_kv"][1], B, S, "l1")
    y1 = _gate_fwd(ycat1, proj1, z1_col, "l1_gate")
    h2 = _mm(y1, W["w_out"][1], "nn", F32, "l1_out", res=h1)

    gf = row(W["final_norm_g"])
    dh2, d_gf, loss128 = _final_loss(h2, tgt, gf, "final_loss")
    G = {"final_norm_g": d_gf.reshape(-1)}
    L1 = {}

    dy1 = _mm(dh2, W["w_out"][1], "nt", F32, "l1_out_dx")
    d_wout1 = _mm(y1, dh2, "tn", F32, "l1_out_dw")
    d_ycat1, d_proj1 = _gate_bwd(ycat1, proj1, z1_col, dy1, "l1_gate_bwd")
    d_proj1, d_wmem1, d_mg1 = _mem_attention_bwd(proj1, qm1_col, ycat1, d_ycat1, d_proj1, mem_saved1, mem2, mg1,
                                                 W["w_mem_kv"][1], B, S, "l1")
    d_qf, d_qr, d_kvf, d_kr_heads = _attn_bwd(qf, 0, qr, kvf, 0, kr, kvf, 1, ycat1, d_ycat1, 0, lse1, B, S, S,
                                              MLA_HEADS, True, scale1, "l1_attn_bwd", kv_stride=2)
    d_qf, d_proj1 = _rowwise(_f_rope_t, [d_qr, d_kr_heads, cos_p, sin_p], [], [(n_nope, F32), (128, F32)], "l1_rope_bwd",
                             into=[(0, d_qf, 1), (1, d_proj1, kr_col // 128)])
    d_cqn = _mm(d_qf, W["mla_w_uq"], "nt", F32, "l1_uq_dx")
    L1[("mla_w_uq", None)] = _mm(cqn, d_qf, "tn", F32, "l1_uq_dw")
    d_ckvn = _mm(d_kvf, W["mla_w_ukv"], "nt", F32, "l1_ukv_dx")
    L1[("mla_w_ukv", None)] = _mm(ckvn, d_kvf, "tn", F32, "l1_ukv_dw")
    d_proj1, d_qg = _rowwise_bwd(_f_rms, [cq], [qg], [d_cqn], 1, "l1_qnorm_bwd", into=(d_proj1, cq[2]))
    d_proj1, d_kvg = _rowwise_bwd(_f_rms, [ckv], [kvg], [d_ckvn], 1, "l1_kvnorm_bwd", into=(d_proj1, ckv[2]))
    L1[("w_mem_kv", 1)] = d_wmem1
    L1[("mla_w_in", None)] = _mm(u1, d_proj1, "tn", F32, "l1_in_dw")
    L1[("w_out", 1)] = d_wout1
    W.ready("l1", L1)
    d_u1 = _mm(d_proj1, W["mla_w_in"], "nt", F32, "l1_in_dx", carry=W.carry("l1_in_dx"))
    dh1, d_g1 = _rowwise_bwd(_f_rms, [h1], [g1], [d_u1], 1, "l1_norm_bwd", add=dh2)

    dy0 = _mm(dh1, W["w_out"][0], "nt", F32, "l0_out_dx")
    d_wout0 = _mm(y0, dh1, "tn", F32, "l0_out_dw")
    d_ycat0, d_proj0 = _gate_bwd(ycat0, proj0, z0_col, dy0, "l0_gate_bwd", carry=W.carry("l0_gate_bwd"))
    d_proj0, d_wmem0, d_mg0 = _mem_attention_bwd(proj0, qm0_col, ycat0, d_ycat0, d_proj0, mem_saved0, mem2, mg0,
                                                 W["w_mem_kv"][0], B, S, "l0")
    W.ready("l0a", {("w_mem_kv", 0): d_wmem0, ("w_out", 0): d_wout0})
    d_cv, d_ln_g, d_ln_b = _rowwise_bwd(_f_ln_silu, [cv], [ln_g, ln_b], [(d_ycat0, conv_w, 0)], 1, "l0_ln_bwd",
                                        carry=W.carry("l0_ln_bwd"))
    d_glu, d_dw, d_dwb = _dwconv_bwd(glu.reshape(B, S, conv_w), dw, d_cv.reshape(B, S, conv_w), "l0_dwconv_bwd",
                                     carry=W.carry("l0_dwconv_bwd"))
    d_proj0 = _glu_bwd(proj0, d_glu.reshape(T, conv_w), d_proj0, "l0_glu_bwd")
    d_conv_w_in = _mm(u0, d_proj0, "tn", F32, "l0_in_dw", carry=W.carry("l0_in_dw"))
    W.ready("l0b", {("conv_w_in", None): d_conv_w_in, ("conv_dw", None): d_dw,
                    ("mla_q_norm_g", None): d_qg.reshape(-1), ("mla_kv_norm_g", None): d_kvg.reshape(-1)},
            payload=BF16)
    d_u0 = _mm(d_proj0, W["conv_w_in"], "nt", F32, "l0_in_dx", carry=W.carry("l0_in_dx"))
    dx, d_g0 = _rowwise_bwd(_f_rms, [h0], [g0], [d_u0], 1, "l0_norm_bwd", add=dh1)
    dx = dx.reshape(B, S, D)

    G["norm_g"] = jnp.concatenate([d_g0, d_g1], axis=0)
    G["mem_norm_g"] = jnp.concatenate([d_mg0, d_mg1], axis=0)
    G["conv_dw_b"] = d_dwb
    G["conv_ln_g"], G["conv_ln_b"] = d_ln_g, d_ln_b
    return loss128[0, 0], dx, G


def _mla_in_perm(w):
    c2 = Q_RANK + KV_RANK
    zero = jnp.zeros((w.shape[0], HALF_ROPE), w.dtype)
    return jnp.concatenate([w[:, :c2], w[:, c2 + MLA_ROPE:], w[:, c2:c2 + HALF_ROPE], zero,
                            w[:, c2 + HALF_ROPE:c2 + MLA_ROPE], zero], axis=1)


def _mla_in_unperm(g):
    c2 = Q_RANK + KV_RANK
    r = g.shape[1] - 128
    return jnp.concatenate([g[:, :c2], g[:, r:r + HALF_ROPE], g[:, r + 64:r + 64 + HALF_ROPE], g[:, c2:r]], axis=1)


def _uq_perm(w):
    n = w.shape[0]
    w3 = w.reshape(n, MLA_HEADS, MLA_QK)
    zero = jnp.zeros((n, MLA_HEADS, HALF_ROPE), w.dtype)
    rope = jnp.concatenate([w3[:, :, MLA_NOPE:MLA_NOPE + HALF_ROPE], zero, w3[:, :, MLA_NOPE + HALF_ROPE:], zero], axis=2)
    return jnp.concatenate([w3[:, :, :MLA_NOPE].reshape(n, -1), rope.reshape(n, -1)], axis=1)


def _uq_unperm(g):
    n = g.shape[0]
    n_nope = MLA_HEADS * MLA_NOPE
    rope = g[:, n_nope:].reshape(n, MLA_HEADS, 128)
    return jnp.concatenate([g[:, :n_nope].reshape(n, MLA_HEADS, MLA_NOPE), rope[:, :, :HALF_ROPE],
                            rope[:, :, 64:64 + HALF_ROPE]], axis=2).reshape(n, -1)


_ROW_CUT = ("w_mem_kv", "w_out")
_COL_CUT = ("conv_w_in", "mla_w_in", "mla_w_uq", "mla_w_ukv", "conv_dw")
_BIG = ("w_mem_kv", "w_out", "conv_w_in", "mla_w_in", "mla_w_uq", "mla_w_ukv")
_SMALL_SHARDED = ("conv_dw", "mla_q_norm_g", "mla_kv_norm_g")
_REPLICATED = ("norm_g", "mem_norm_g", "conv_dw_b", "conv_ln_g", "conv_ln_b", "final_norm_g")
_PERM = {"mla_w_in": (_mla_in_perm, _mla_in_unperm), "mla_w_uq": (_uq_perm, _uq_unperm)}


def _join(n, blocks):
    if n in _ROW_CUT:
        _, L, r, c = blocks.shape
        return blocks.transpose(1, 0, 2, 3).reshape(L, N_DEV * r, c)
    if n in _COL_CUT:
        _, _, r, c = blocks.shape
        return blocks.reshape(N_DEV, r, c).transpose(1, 0, 2).reshape(r, N_DEV * c)
    return blocks.reshape(-1)


def _cut(n, full, shard_shape):
    if n in _ROW_CUT:
        L, r, c = shard_shape
        return full.reshape(L, N_DEV, r, c).transpose(1, 0, 2, 3)
    if n in _COL_CUT:
        _, r, c = shard_shape
        return full.reshape(r, N_DEV, c).transpose(1, 0, 2).reshape(N_DEV, 1, r, c)
    return full.reshape(N_DEV, 1, -1)


def _flat_pad(parts, size):
    flat = jnp.concatenate([p.reshape(-1) for p in parts])
    return jnp.concatenate([flat, jnp.zeros((size - flat.shape[0],), flat.dtype)])


SMALL_LANES = 128 * 8


def _as_tiles(flat_parts):
    total = sum(p.size for p in flat_parts)
    size = -(-total // SMALL_LANES) * SMALL_LANES
    return _flat_pad(flat_parts, size).reshape(8, size // 8)


def _split_flat(flat, like):
    out, o = [], 0
    for a in like:
        out.append(flat[o:o + a.size].reshape(a.shape))
        o += a.size
    return out


_HBM = pl.BlockSpec(memory_space=pltpu.HBM)
_VMEM = pl.BlockSpec(memory_space=pltpu.VMEM)


def _position():
    return lax.axis_index("x"), lax.axis_index("y"), lax.axis_index("c")


def _dma_sems(n):
    return [pltpu.SemaphoreType.DMA((n,)), pltpu.SemaphoreType.DMA((n,))]


def _run_stage(stage, name):
    n_in, n_out = len(stage.ins), len(stage.out_shapes)

    def body(*refs):
        ins, outs, sems = refs[:n_in], refs[n_in:n_in + n_out], refs[n_in + n_out:]
        stage.start(ins, outs, sems)
        stage.wait(ins, outs, sems)

    outs = _call(body, name, stage.out_shapes, in_specs=[_HBM] * n_in, out_specs=[_HBM] * n_out, scratch=stage.sems,
                 aliases=stage.aliases)(*stage.ins)
    stage.outs = list(outs)
    return stage.outs


def _gather_chips_stage(shards):
    n = len(shards)

    def copies(x_refs, out_refs, sems):
        send_sems, recv_sems, _ = sems
        x, y, c = _position()
        peers = [(x, y, 1 - c), (1 - x, y, c), (x, 1 - y, c), (1 - x, 1 - y, c)]
        out = []
        for a in range(n):
            for k, (px, py, pc) in enumerate(peers):
                send = pltpu.make_async_remote_copy(src_ref=x_refs[a], dst_ref=out_refs[a].at[4 * x + 2 * y + c],
                                                    send_sem=send_sems.at[4 * a + k], recv_sem=recv_sems.at[4 * a + k],
                                                    device_id=(px, py, pc), device_id_type=MESH)
                recv = pltpu.make_async_remote_copy(src_ref=x_refs[a], dst_ref=out_refs[a].at[4 * px + 2 * py + pc],
                                                    send_sem=send_sems.at[4 * a + k], recv_sem=recv_sems.at[4 * a + k],
                                                    device_id=(px, py, pc), device_id_type=MESH)
                out.append((send, recv))
        return out

    def local(x_refs, out_refs, sems):
        x, y, c = _position()
        return [pltpu.make_async_copy(x_refs[a], out_refs[a].at[4 * x + 2 * y + c], sems[2].at[a]) for a in range(n)]

    def start(x_refs, out_refs, sems):
        for cp in local(x_refs, out_refs, sems):
            cp.start()
        for send, _ in copies(x_refs, out_refs, sems):
            send.start()

    def wait(x_refs, out_refs, sems):
        for send, recv in copies(x_refs, out_refs, sems):
            recv.wait_recv()
            send.wait_send()
        for cp in local(x_refs, out_refs, sems):
            cp.wait()

    return _Stage(shards, [jax.ShapeDtypeStruct((N_DEV,) + a.shape, a.dtype) for a in shards],
                  _dma_sems(4 * n) + [pltpu.SemaphoreType.DMA((n,))], start, wait)


def _gather_sibling_stage(bufs):
    n = len(bufs)

    def copies(out_refs, sems):
        send_sems, recv_sems = sems
        x, y, c = _position()
        out = []
        for a in range(n):
            for j, (px, py) in enumerate([(1 - x, y), (x, 1 - y), (1 - x, 1 - y)]):
                mine, theirs = out_refs[a].at[4 * px + 2 * py + c], out_refs[a].at[4 * px + 2 * py + (1 - c)]
                send = pltpu.make_async_remote_copy(src_ref=mine, dst_ref=mine, send_sem=send_sems.at[3 * a + j],
                                                    recv_sem=recv_sems.at[3 * a + j], device_id=(x, y, 1 - c),
                                                    device_id_type=MESH)
                recv = pltpu.make_async_remote_copy(src_ref=mine, dst_ref=theirs, send_sem=send_sems.at[3 * a + j],
                                                    recv_sem=recv_sems.at[3 * a + j], device_id=(x, y, 1 - c),
                                                    device_id_type=MESH)
                out.append((send, recv))
        return out

    def start(_, out_refs, sems):
        for send, _r in copies(out_refs, sems):
            send.start()

    def wait(_, out_refs, sems):
        for send, recv in copies(out_refs, sems):
            recv.wait_recv()
            send.wait_send()

    return _Stage(bufs, [jax.ShapeDtypeStruct(b.shape, b.dtype) for b in bufs], _dma_sems(3 * n), start, wait,
                  aliases={a: a for a in range(n)})


def _all_gather_small(v, name):
    r, n = v.shape

    def body(x_ref, out_ref, send_sems, recv_sems, local_sem):
        x, y, c = _position()
        me = 4 * x + 2 * y + c
        mine = pltpu.make_async_copy(x_ref, out_ref.at[me], local_sem)
        mine.start()
        flips = [(fx, fy, fc) for fx in (0, 1) for fy in (0, 1) for fc in (0, 1)][1:]
        copies = []
        for k, (fx, fy, fc) in enumerate(flips):
            peer = (x ^ fx, y ^ fy, c ^ fc)
            cp = pltpu.make_async_remote_copy(src_ref=x_ref, dst_ref=out_ref.at[me], send_sem=send_sems.at[k],
                                              recv_sem=recv_sems.at[k], device_id=peer, device_id_type=MESH)
            cp.start()
            copies.append(cp)
        for k, (fx, fy, fc) in enumerate(flips):
            px, py, pc = x ^ fx, y ^ fy, c ^ fc
            src = out_ref.at[4 * px + 2 * py + pc]
            pltpu.make_async_remote_copy(src_ref=x_ref, dst_ref=src, send_sem=send_sems.at[k], recv_sem=recv_sems.at[k],
                                         device_id=(px, py, pc), device_id_type=MESH).wait_recv()
        for cp in copies:
            cp.wait_send()
        mine.wait()

    return _call(body, name, jax.ShapeDtypeStruct((N_DEV, r, n), v.dtype), in_specs=[_VMEM], out_specs=_VMEM,
                 scratch=_dma_sems(7) + [pltpu.SemaphoreType.DMA(())])(v)


def _reduce_sibling_stage(gs):
    n = len(gs)

    def copies(g_refs, out_refs, sems):
        send_sems, recv_sems = sems
        x, y, c = _position()
        return [pltpu.make_async_remote_copy(src_ref=g_refs[a].at[2 * k + (1 - c)], dst_ref=out_refs[a].at[k],
                                             send_sem=send_sems.at[4 * a + k], recv_sem=recv_sems.at[4 * a + k],
                                             device_id=(x, y, 1 - c), device_id_type=MESH)
                for a in range(n) for k in range(4)]

    def start(g_refs, out_refs, sems):
        for cp in copies(g_refs, out_refs, sems):
            cp.start()

    def wait(g_refs, out_refs, sems):
        for cp in copies(g_refs, out_refs, sems):
            cp.wait()

    return _Stage(gs, [jax.ShapeDtypeStruct((4,) + g.shape[1:], g.dtype) for g in gs], _dma_sems(4 * n), start, wait)


def _rows2d(shape):
    cols = shape[-1]
    rows = 1
    for s in shape[:-1]:
        rows *= s
    return rows, cols


def _add_own(g, recv, name):
    rows, cols = _rows2d(g.shape[1:])
    tr = _pick(rows, 256, 8)
    c = lax.axis_index("c").astype(jnp.int32).reshape(1)

    def body(c_ref, g_ref, r_ref, o_ref):
        o_ref[...] = (g_ref[...].astype(F32) + r_ref[...].astype(F32)).astype(o_ref.dtype)

    grid_spec = pltpu.PrefetchScalarGridSpec(
        num_scalar_prefetch=1, grid=(4, rows // tr),
        in_specs=[pl.BlockSpec((None, None, tr, cols), lambda k, i, c_ref: (k, c_ref[0], i, 0)),
                  pl.BlockSpec((None, tr, cols), lambda k, i, c_ref: (k, i, 0))],
        out_specs=pl.BlockSpec((None, tr, cols), lambda k, i, c_ref: (k, i, 0)))
    return _call(body, name, jax.ShapeDtypeStruct((4, rows, cols), g.dtype), grid_spec=grid_spec,
                 dims=("parallel", "parallel"))(c, g.reshape(4, 2, rows, cols), recv.reshape(4, rows, cols))


def _reduce_chips_stage(pas):
    n = len(pas)

    def copies(pa_refs, out_refs, sems):
        send_sems, recv_sems, _ = sems
        x, y, c = _position()
        my_chip = 2 * x + y
        out = []
        for a in range(n):
            for j, (px, py) in enumerate([(1 - x, y), (x, 1 - y), (1 - x, 1 - y)]):
                send = pltpu.make_async_remote_copy(src_ref=pa_refs[a].at[2 * px + py], dst_ref=out_refs[a].at[my_chip],
                                                    send_sem=send_sems.at[3 * a + j], recv_sem=recv_sems.at[3 * a + j],
                                                    device_id=(px, py, c), device_id_type=MESH)
                recv = pltpu.make_async_remote_copy(src_ref=pa_refs[a].at[2 * px + py], dst_ref=out_refs[a].at[2 * px + py],
                                                    send_sem=send_sems.at[3 * a + j], recv_sem=recv_sems.at[3 * a + j],
                                                    device_id=(px, py, c), device_id_type=MESH)
                out.append((send, recv))
        return out

    def local(pa_refs, out_refs, sems):
        x, y, _ = _position()
        return [pltpu.make_async_copy(pa_refs[a].at[2 * x + y], out_refs[a].at[2 * x + y], sems[2].at[a]) for a in range(n)]

    def start(pa_refs, out_refs, sems):
        for cp in local(pa_refs, out_refs, sems):
            cp.start()
        for send, _r in copies(pa_refs, out_refs, sems):
            send.start()

    def wait(pa_refs, out_refs, sems):
        for send, recv in copies(pa_refs, out_refs, sems):
            recv.wait_recv()
            send.wait_send()
        for cp in local(pa_refs, out_refs, sems):
            cp.wait()

    return _Stage(pas, [jax.ShapeDtypeStruct(pa.shape, pa.dtype) for pa in pas],
                  _dma_sems(3 * n) + [pltpu.SemaphoreType.DMA((n,))], start, wait)


def _adamw_math(w, g, m, v):
    m = ADAM_B1 * m + (1.0 - ADAM_B1) * g
    v = ADAM_B2 * v + (1.0 - ADAM_B2) * (g * g)
    m_hat = m / (1.0 - ADAM_B1 ** ADAM_STEP)
    v_hat = v / (1.0 - ADAM_B2 ** ADAM_STEP)
    delta = -ADAM_LR * (m_hat / (jnp.sqrt(v_hat) + ADAM_EPS) + ADAM_WD * w)
    return delta, m, v


def _sum_adamw(parts, w, m, v, name):
    n, rows, cols = parts.shape
    tr = _pick(rows, 128, 8)

    def body(p_ref, w_ref, m_ref, v_ref, g_ref, d_ref, nm_ref, nv_ref):
        g = p_ref[0].astype(F32)
        for k in range(1, n):
            g = g + p_ref[k].astype(F32)
        d, nm, nv = _adamw_math(w_ref[...], g, m_ref[...], v_ref[...])
        g_ref[...], d_ref[...], nm_ref[...], nv_ref[...] = g, d, nm, nv

    blk = pl.BlockSpec((tr, cols), lambda i: (i, 0))
    return _call(body, name, [jax.ShapeDtypeStruct((rows, cols), F32)] * 4, grid=(rows // tr,),
                 in_specs=[pl.BlockSpec((n, tr, cols), lambda i: (0, i, 0)), blk, blk, blk],
                 out_specs=[blk] * 4, dims=("parallel",))(parts, w, m, v)


_WEIGHTS = ("norm_g", "mem_norm_g", "w_mem_kv", "w_out", "conv_w_in", "conv_dw", "conv_dw_b", "conv_ln_g", "conv_ln_b",
            "mla_w_in", "mla_q_norm_g", "mla_w_uq", "mla_kv_norm_g", "mla_w_ukv", "final_norm_g")


_GATHER_GROUPS = {"a": ("conv_w_in",), "b": ("w_mem_kv", "w_out"), "c": ("mla_w_in", "mla_w_uq", "mla_w_ukv")}
_CARRIERS = {"l0_norm": ("gather chips", ("a",)), "l0_in": ("gather chips", ("b",)), "l0_glu": ("gather sibling", ("b",)),
             "l0_dwconv": ("gather chips", ("c",)), "l0_ln": ("gather sibling", ("c",)),
             "l1_in_dx": ("reduce sibling", ("l1",)), "l0_ln_bwd": ("reduce sibling", ("l0a",)),
             "l0_gate_bwd": ("reduce chips", ("l1", 0, 2)), "l0_dwconv_bwd": ("reduce chips", ("l1", 2, 5)),
             "l0_in_dw": ("reduce chips", ("l0a",)), "l0_in_dx": ("reduce sibling alone, then chips", ("l0b",))}


class _Schedule:
    def __init__(self, w):
        self.w, self.full, self.gather, self.reduce, self.reduced = w, {}, {}, {}, {}
        small = _all_gather_small(_as_tiles([w[n] for n in _SMALL_SHARDED]), "gather_small_weights").reshape(N_DEV, -1)
        o = 0
        for n in _SMALL_SHARDED:
            self.full[n] = _join(n, small[:, o:o + w[n].size].reshape((N_DEV,) + w[n].shape))
            o += w[n].size
        for n in _REPLICATED:
            self.full[n] = w[n]

    def carry(self, call):
        kind, (g, *part) = _CARRIERS[call]
        if kind == "gather chips":
            self.gather[g] = [_gather_chips_stage([self.w[n].astype(BF16) for n in _GATHER_GROUPS[g]])]
            return self.gather[g][0]
        if kind == "gather sibling":
            self.gather[g].append(_gather_sibling_stage(self.gather[g][0].outs))
            return self.gather[g][1]
        r = self.reduce[g]
        if kind == "reduce sibling":
            r["sibling"] = _reduce_sibling_stage(r["cut"])
            return r["sibling"]
        if kind != "reduce chips":
            r["sibling"] = _reduce_sibling_stage(r["cut"])
            _run_stage(r["sibling"], "reduce_sibling_" + g)
        if "partial" not in r:
            r["partial"] = [_add_own(c, s, "reduce_add_%s_%d" % (g, i))
                            for i, (c, s) in enumerate(zip(r["cut"], r["sibling"].outs))]
        lo, hi = part if part else (0, len(r["keys"]))
        stage = _reduce_chips_stage(r["partial"][lo:hi])
        r.setdefault("chips", []).append((r["keys"][lo:hi], stage))
        return stage

    def __getitem__(self, name):
        if name not in self.full:
            g = [k for k, names in _GATHER_GROUPS.items() if name in names][0]
            if len(self.gather[g]) == 1:
                self.gather[g].append(_gather_sibling_stage(self.gather[g][0].outs))
                _run_stage(self.gather[g][1], "gather_sibling_" + g)
            for n, buf in zip(_GATHER_GROUPS[g], self.gather[g][1].outs):
                self.full[n] = _PERM[n][0](_join(n, buf)) if n in _PERM else _join(n, buf)
        return self.full[name]

    def ready(self, group, grads, payload=F32):
        keys, cut, small = [], [], []
        for (n, layer), g in grads.items():
            if n in _SMALL_SHARDED:
                small.append(_cut(n, g, self.w[n].shape).reshape(N_DEV, -1))
                continue
            keys.append((n, layer))
            if layer is not None:
                cut.append(g.reshape((N_DEV,) + self.w[n].shape[1:]).astype(payload))
            else:
                cut.append(_cut(n, _PERM[n][1](g) if n in _PERM else g, self.w[n].shape).astype(payload))
        if small:
            keys.append(("small", None))
            cut.append(jax.vmap(lambda r: _as_tiles([r]))(jnp.concatenate(small, axis=1)))
        self.reduce[group] = {"keys": keys, "cut": cut}

    def finish(self):
        out = {}
        for r in self.reduce.values():
            for keys, stage in r["chips"]:
                out.update(dict(zip(keys, stage.outs)))
        return out


def kernel(x, mem, positions, norm_g, mem_norm_g, w_mem_kv, w_out, conv_w_in, conv_dw, conv_dw_b, conv_ln_g, conv_ln_b, mla_w_in, mla_q_norm_g, mla_w_uq, mla_kv_norm_g, mla_w_ukv, final_norm_g, loss_target, m_norm_g, m_mem_norm_g, m_w_mem_kv, m_w_out, m_conv_w_in, m_conv_dw, m_conv_dw_b, m_conv_ln_g, m_conv_ln_b, m_mla_w_in, m_mla_q_norm_g, m_mla_w_uq, m_mla_kv_norm_g, m_mla_w_ukv, m_final_norm_g, v_norm_g, v_mem_norm_g, v_w_mem_kv, v_w_out, v_conv_w_in, v_conv_dw, v_conv_dw_b, v_conv_ln_g, v_conv_ln_b, v_mla_w_in, v_mla_q_norm_g, v_mla_w_uq, v_mla_kv_norm_g, v_mla_w_ukv, v_final_norm_g):
    w = dict(zip(_WEIGHTS, (norm_g, mem_norm_g, w_mem_kv, w_out, conv_w_in, conv_dw, conv_dw_b, conv_ln_g, conv_ln_b,
                            mla_w_in, mla_q_norm_g, mla_w_uq, mla_kv_norm_g, mla_w_ukv, final_norm_g)))
    m = dict(zip(_WEIGHTS, (m_norm_g, m_mem_norm_g, m_w_mem_kv, m_w_out, m_conv_w_in, m_conv_dw, m_conv_dw_b, m_conv_ln_g,
                            m_conv_ln_b, m_mla_w_in, m_mla_q_norm_g, m_mla_w_uq, m_mla_kv_norm_g, m_mla_w_ukv, m_final_norm_g)))
    v = dict(zip(_WEIGHTS, (v_norm_g, v_mem_norm_g, v_w_mem_kv, v_w_out, v_conv_w_in, v_conv_dw, v_conv_dw_b, v_conv_ln_g,
                            v_conv_ln_b, v_mla_w_in, v_mla_q_norm_g, v_mla_w_uq, v_mla_kv_norm_g, v_mla_w_ukv, v_final_norm_g)))

    sched = _Schedule(w)
    loss_local, dx, G = _forward_backward(x, mem, positions, loss_target, sched)
    loss = lax.psum(loss_local, ("x", "y", "c"))

    from_chips = sched.finish()
    out = [{}, {}, {}, {}]
    for n in _BIG:
        if n in _ROW_CUT:
            res = [_sum_adamw(from_chips[(n, l)], w[n][l], m[n][l], v[n][l], "adamw_%s_%d" % (n, l)) for l in range(w[n].shape[0])]
            res = [jnp.stack(r) for r in zip(*res)]
        else:
            rows, cols = _rows2d(w[n].shape)
            res = _sum_adamw(from_chips[(n, None)], w[n].reshape(rows, cols), m[n].reshape(rows, cols),
                             v[n].reshape(rows, cols), "adamw_" + n)
        for o, r in zip(out, res):
            o[n] = r.reshape(w[n].shape)
    small_like = [w[n] for n in _SMALL_SHARDED]
    res = _sum_adamw(from_chips[("small", None)], _as_tiles(small_like), _as_tiles([m[n] for n in _SMALL_SHARDED]),
                     _as_tiles([v[n] for n in _SMALL_SHARDED]), "adamw_small")
    for o, r in zip(out, res):
        for n, a in zip(_SMALL_SHARDED, _split_flat(r.reshape(-1), small_like)):
            o[n] = a

    rep_like = [w[n] for n in _REPLICATED]
    rep_parts = _all_gather_small(_as_tiles([G[n] for n in _REPLICATED]), "gather_replicated_grads")
    res = _sum_adamw(rep_parts, _as_tiles(rep_like), _as_tiles([m[n] for n in _REPLICATED]),
                     _as_tiles([v[n] for n in _REPLICATED]), "adamw_replicated")
    for o, r in zip(out, res):
        for n, a in zip(_REPLICATED, _split_flat(r.reshape(-1), rep_like)):
            o[n] = a

    return (loss, dx, *[out[0][n] for n in _WEIGHTS], *[out[1][n] for n in _WEIGHTS],
            *[out[2][n] for n in _WEIGHTS], *[out[3][n] for n in _WEIGHTS])
```

```python
import jax
import jax.numpy as jnp
from jax import lax
from jax.experimental import pallas as pl
from jax.experimental.pallas import tpu as pltpu

F32 = jnp.float32
BF16 = jnp.bfloat16
MESH = pl.DeviceIdType.MESH
N_DEV = 8
VMEM_LIMIT_BYTES = 48 * 1024 * 1024

MEM_HEADS, MEM_HEAD_DIM = 4, 128
MEM_WIDTH = MEM_HEADS * MEM_HEAD_DIM
CONV_KERNEL = 31
CONV_PAD = 32
MLA_HEADS, MLA_NOPE, MLA_ROPE, MLA_V = 12, 128, 64, 128
MLA_QK = MLA_NOPE + MLA_ROPE
HALF_ROPE = MLA_ROPE // 2
Q_RANK, KV_RANK = 512, 256
ROPE_THETA = 10000.0
RMS_EPS = 1e-6
LN_EPS = 1e-5
ADAM_LR, ADAM_B1, ADAM_B2, ADAM_EPS, ADAM_WD, ADAM_STEP = 0.001, 0.9, 0.999, 1e-08, 0.01, 10
NEG = -1e30


class _Stage:
    def __init__(self, ins, out_shapes, sems, start, wait, aliases=None):
        self.ins, self.out_shapes, self.sems = list(ins), list(out_shapes), list(sems)
        self.start, self.wait, self.aliases, self.outs = start, wait, dict(aliases or {}), None


def _call(body, name, out_shape, grid=None, in_specs=None, out_specs=None, scratch=(), dims=None, grid_spec=None, aliases=None,
          carry=None):
    params = dict(vmem_limit_bytes=VMEM_LIMIT_BYTES)
    if dims is not None:
        params["dimension_semantics"] = dims
    kw = {}
    if carry is not None:
        single = not isinstance(out_shape, (list, tuple))
        main_out = [out_shape] if single else list(out_shape)
        main_specs = [out_specs] if single else list(out_specs)
        n_in, n_out, n_scr = len(in_specs), len(main_out), len(scratch)
        x_in, x_out = len(carry.ins), len(carry.out_shapes)
        inner, steps = body, tuple(grid)

        def body(*refs):
            ins, xin = refs[:n_in], refs[n_in:n_in + x_in]
            outs = refs[n_in + x_in:n_in + x_in + n_out]
            xout = refs[n_in + x_in + n_out:n_in + x_in + n_out + x_out]
            scr = refs[n_in + x_in + n_out + x_out:n_in + x_in + n_out + x_out + n_scr]
            xsem = refs[n_in + x_in + n_out + x_out + n_scr:]
            ids = [pl.program_id(a) for a in range(len(steps))]
            first, last = ids[0] == 0, ids[0] == steps[0] - 1
            for a in range(1, len(steps)):
                first = jnp.logical_and(first, ids[a] == 0)
                last = jnp.logical_and(last, ids[a] == steps[a] - 1)
            pl.when(first)(lambda: carry.start(xin, xout, xsem))
            inner(*ins, *outs, *scr)
            pl.when(last)(lambda: carry.wait(xin, xout, xsem))

        hbm = pl.BlockSpec(memory_space=pltpu.HBM)
        aliases = dict(aliases or {})
        aliases.update({n_in + k: n_out + v for k, v in carry.aliases.items()})
        res = _call(body, name, main_out + carry.out_shapes, grid=grid, in_specs=list(in_specs) + [hbm] * x_in,
                    out_specs=main_specs + [hbm] * x_out, scratch=list(scratch) + carry.sems, dims=dims, aliases=aliases)

        def run(*args):
            outs = res(*args, *carry.ins)
            carry.outs = list(outs[n_out:])
            return outs[0] if single else outs[:n_out]

        return run
    if aliases:
        kw["input_output_aliases"] = aliases
    if grid_spec is not None:
        kw["grid_spec"] = grid_spec
    else:
        if grid is not None:
            kw["grid"] = grid
        kw["in_specs"] = in_specs
        kw["out_specs"] = out_specs
        kw["scratch_shapes"] = list(scratch)
    return pl.pallas_call(body, name=name, out_shape=out_shape, compiler_params=pltpu.CompilerParams(**params), **kw)


def _pick(n, target, mult):
    best = None
    for d in range(mult, min(n, target) + 1, mult):
        if n % d == 0:
            best = d
    return n if best is None else best


_DOT_DIMS = {"nn": (((1,), (0,)), ((), ())), "nt": (((1,), (1,)), ((), ())), "tn": (((0,), (0,)), ((), ()))}


def _mm(a, b, mode, out_dtype, name, res=None, carry=None):
    if mode == "tn":
        a, mode = a.T, "nn"
    if mode == "nn":
        (M, K), N = a.shape, b.shape[1]
    else:
        (M, K), N = a.shape, b.shape[0]
    tm = _pick(M, 1024, 8)
    tn = _pick(N, 1536, 128)
    tk = _pick(K, 1536, 128)
    nk = K // tk
    has_res = res is not None

    def body(*refs):
        if has_res:
            a_ref, b_ref, r_ref, o_ref, acc = refs
        else:
            a_ref, b_ref, o_ref, acc = refs
        k = pl.program_id(2)

        @pl.when(k == 0)
        def _():
            acc[...] = jnp.zeros_like(acc)

        acc[...] += lax.dot_general(a_ref[...].astype(BF16), b_ref[...].astype(BF16), _DOT_DIMS[mode],
                                    preferred_element_type=F32)

        @pl.when(k == nk - 1)
        def _():
            r = acc[...]
            if has_res:
                r = r + r_ref[...]
            o_ref[...] = r.astype(o_ref.dtype)

    a_spec = pl.BlockSpec((tm, tk), lambda i, j, k: (i, k))
    b_spec = {"nn": pl.BlockSpec((tk, tn), lambda i, j, k: (k, j)),
              "nt": pl.BlockSpec((tn, tk), lambda i, j, k: (j, k))}[mode]
    o_spec = pl.BlockSpec((tm, tn), lambda i, j, k: (i, j))
    in_specs = [a_spec, b_spec] + ([o_spec] if has_res else [])
    args = (a, b) + ((res,) if has_res else ())
    return _call(body, name, jax.ShapeDtypeStruct((M, N), out_dtype), grid=(M // tm, N // tn, nk),
                 in_specs=in_specs, out_specs=o_spec, scratch=[pltpu.VMEM((tm, tn), F32)],
                 dims=("parallel", "parallel", "arbitrary"), carry=carry)(*args)


def _views(rows):
    return [r if isinstance(r, tuple) else (r, r.shape[1], 0) for r in rows]


def _rowwise(f, rows, params, outs, name, tb=256, carry=None, into=None):
    rows = _views(rows)
    T = rows[0][0].shape[0]
    tb = min(tb, T)
    nr, npar = len(rows), len(params)
    outs = [o if len(o) == 3 else (o[0], o[1], o[0]) for o in outs]
    into = into or []

    def body(*refs):
        vals = f(*[r[...].astype(F32) for r in refs[:nr]], *[p[...] for p in refs[nr:nr + npar]])
        for o_ref, v in zip(refs[nr + npar + len(into):], vals):
            o_ref[...] = v.astype(o_ref.dtype)

    row_spec = lambda w, cb=0: pl.BlockSpec((tb, w), lambda i: (i, cb))
    par_spec = lambda w: pl.BlockSpec((1, w), lambda i: (0, 0))
    out_shape = [jax.ShapeDtypeStruct((T, tw), dt) for _, dt, tw in outs]
    out_specs = [row_spec(w) for w, _, _ in outs]
    in_specs = [row_spec(w, cb) for _, w, cb in rows] + [par_spec(p.shape[1]) for p in params]
    args = [r[0] for r in rows] + list(params)
    aliases = {}
    for k, arr, cb in into:
        aliases[len(args)] = k
        in_specs.append(pl.BlockSpec(memory_space=pl.ANY))
        args.append(arr)
        out_shape[k] = jax.ShapeDtypeStruct(arr.shape, arr.dtype)
        out_specs[k] = row_spec(outs[k][0], cb)
    return _call(body, name, out_shape, grid=(T // tb,), in_specs=in_specs, out_specs=out_specs, dims=("parallel",),
                 carry=carry, aliases=aliases)(*args)


def _rowwise_bwd(f, rows, params, douts, n_diff, name, tb=256, carry=None, add=None, into=None):
    rows, douts = _views(rows), _views(douts)
    T = rows[0][0].shape[0]
    tb = min(tb, T)
    nr, npar, nd = len(rows), len(params), len(douts)
    n_add = 0 if add is None else 1

    def body(*refs):
        rv = [r[...].astype(F32) for r in refs[:nr]]
        pv = [p[...] for p in refs[nr:nr + npar]]
        dv = [d[...].astype(F32) for d in refs[nr + npar:nr + npar + nd]]
        o_refs = refs[nr + npar + nd + n_add + (0 if into is None else 1):]
        fixed = rv[n_diff:]

        def g(*xs):
            return tuple(f(*xs[:n_diff], *fixed, *xs[n_diff:]))

        _, vjp = jax.vjp(g, *rv[:n_diff], *pv)
        grads = list(vjp(tuple(dv)))
        if add is not None:
            grads[0] = grads[0] + refs[nr + npar + nd][...]
        for o_ref, gr in zip(o_refs[:n_diff], grads[:n_diff]):
            o_ref[...] = gr.astype(o_ref.dtype)
        first = pl.program_id(0) == 0
        for o_ref, gr in zip(o_refs[n_diff:], grads[n_diff:]):
            @pl.when(first)
            def _(o_ref=o_ref):
                o_ref[...] = jnp.zeros_like(o_ref)

            o_ref[...] += gr

    row_spec = lambda w, cb=0: pl.BlockSpec((tb, w), lambda i: (i, cb))
    par_spec = lambda w: pl.BlockSpec((1, w), lambda i: (0, 0))
    out_shape = ([jax.ShapeDtypeStruct((T, w), F32) for _, w, _ in rows[:n_diff]]
                 + [jax.ShapeDtypeStruct((1, p.shape[1]), F32) for p in params])
    out_specs = [row_spec(w) for _, w, _ in rows[:n_diff]] + [par_spec(p.shape[1]) for p in params]
    in_specs = ([row_spec(w, cb) for _, w, cb in rows] + [par_spec(p.shape[1]) for p in params]
                + [row_spec(w, cb) for _, w, cb in douts])
    args = [r[0] for r in rows] + list(params) + [d[0] for d in douts]
    aliases = None
    if add is not None:
        in_specs.append(row_spec(add.shape[1]))
        args.append(add)
    if into is not None:
        aliases = {len(args): 0}
        in_specs.append(pl.BlockSpec(memory_space=pl.ANY))
        args.append(into[0])
        out_shape[0] = jax.ShapeDtypeStruct(into[0].shape, into[0].dtype)
        out_specs[0] = row_spec(rows[0][1], into[1])
    return _call(body, name, out_shape, grid=(T // tb,), in_specs=in_specs, out_specs=out_specs,
                 dims=("arbitrary",), carry=carry, aliases=aliases)(*args)


def _sig(x):
    return 1.0 / (1.0 + jnp.exp(-x))


def _rms(x, g):
    return x * lax.rsqrt(jnp.mean(x * x, axis=-1, keepdims=True) + RMS_EPS) * g


def _f_rms(x, g):
    return (_rms(x, g),)


def _f_glu(a, gate):
    return (a * _sig(gate),)


def _f_ln_silu(x, g, b):
    mu = jnp.mean(x, axis=-1, keepdims=True)
    xc = x - mu
    var = jnp.mean(xc * xc, axis=-1, keepdims=True)
    y = xc * lax.rsqrt(var + LN_EPS) * g + b
    return (y * _sig(y),)


def _rope128(x, cos_p, sin_p):
    return x * cos_p + pltpu.roll(x, 64, 1) * sin_p


def _rope128_t(d, cos_p, sin_p):
    return d * cos_p + pltpu.roll(d * sin_p, 64, 1)


def _f_rope(xq, xk, cos_p, sin_p):
    heads = [_rope128(xq[:, h * 128:(h + 1) * 128], cos_p, sin_p) for h in range(MLA_HEADS)]
    return (jnp.concatenate(heads, axis=1), _rope128(xk, cos_p, sin_p))


def _f_rope_t(dq, dk_heads, cos_p, sin_p):
    heads = [_rope128_t(dq[:, h * 128:(h + 1) * 128], cos_p, sin_p) for h in range(MLA_HEADS)]
    dk = dk_heads[:, 0:128]
    for h in range(1, MLA_HEADS):
        dk = dk + dk_heads[:, h * 128:(h + 1) * 128]
    return (jnp.concatenate(heads, axis=1), _rope128_t(dk, cos_p, sin_p))


GATE_LANES = 256


def _gate_fwd(ycat, proj, z_col, name, tb=1024):
    T, width = ycat.shape
    zb = z_col // GATE_LANES

    def body(y_ref, z_ref, o_ref):
        z = z_ref[...]
        o_ref[...] = (y_ref[...] * (z * _sig(z))).astype(o_ref.dtype)

    blk = pl.BlockSpec((tb, GATE_LANES), lambda i, c: (i, c))
    return _call(body, name, jax.ShapeDtypeStruct((T, width), BF16), grid=(T // tb, width // GATE_LANES),
                 in_specs=[blk, pl.BlockSpec((tb, GATE_LANES), lambda i, c: (i, zb + c))], out_specs=blk,
                 dims=("parallel", "parallel"))(ycat, proj)


def _gate_bwd(ycat, proj, z_col, dy, name, tb=1024, carry=None):
    T, width = ycat.shape
    zb = z_col // GATE_LANES

    def body(y_ref, z_ref, dy_ref, dycat_ref, dz_ref):
        z, d = z_ref[...], dy_ref[...]
        s = _sig(z)
        dycat_ref[...] = d * (z * s)
        dz_ref[...] = (d * y_ref[...] * (s * (1.0 + z * (1.0 - s)))).astype(dz_ref.dtype)

    blk = pl.BlockSpec((tb, GATE_LANES), lambda i, c: (i, c))
    zblk = pl.BlockSpec((tb, GATE_LANES), lambda i, c: (i, zb + c))
    return _call(body, name, [jax.ShapeDtypeStruct((T, width), F32), jax.ShapeDtypeStruct(proj.shape, BF16)],
                 grid=(T // tb, width // GATE_LANES), in_specs=[blk, zblk, blk], out_specs=[blk, zblk],
                 dims=("parallel", "parallel"), carry=carry)(ycat, proj, dy)


def _glu_bwd(proj, d_glu, d_proj, name, tb=256):
    T, w = d_glu.shape

    def body(a_ref, g_ref, d_ref, _, o_ref):
        s, d = _sig(g_ref[...]), d_ref[...]
        o_ref[:, 0:w] = (d * s).astype(o_ref.dtype)
        o_ref[:, w:2 * w] = (d * a_ref[...] * (s * (1.0 - s))).astype(o_ref.dtype)

    return _call(body, name, jax.ShapeDtypeStruct(d_proj.shape, d_proj.dtype), grid=(T // tb,),
                 in_specs=[pl.BlockSpec((tb, w), lambda i: (i, 0)), pl.BlockSpec((tb, w), lambda i: (i, 1)),
                           pl.BlockSpec((tb, w), lambda i: (i, 0)), pl.BlockSpec(memory_space=pl.ANY)],
                 out_specs=pl.BlockSpec((tb, 2 * w), lambda i: (i, 0)), dims=("parallel",),
                 aliases={3: 0})(proj, proj, d_glu, d_proj)


def _final_loss(h, tgt, g, name, tb=256):
    T, D = h.shape

    def body(h_ref, t_ref, g_ref, dh_ref, dg_ref, loss_ref):
        tv = t_ref[...]

        def rowloss(hh, gg):
            e = _rms(hh, gg) - tv
            return 0.5 * jnp.mean(e * e, axis=-1, keepdims=True)

        lr, vjp = jax.vjp(rowloss, h_ref[...], g_ref[...])
        dh, dg = vjp(jnp.ones_like(lr))
        dh_ref[...] = dh

        @pl.when(pl.program_id(0) == 0)
        def _():
            dg_ref[...] = jnp.zeros_like(dg_ref)
            loss_ref[...] = jnp.zeros_like(loss_ref)

        dg_ref[...] += dg
        loss_ref[...] += jnp.broadcast_to(jnp.sum(lr, axis=0, keepdims=True), loss_ref.shape)

    row = pl.BlockSpec((tb, D), lambda i: (i, 0))
    par = pl.BlockSpec((1, D), lambda i: (0, 0))
    return _call(body, name,
                 [jax.ShapeDtypeStruct((T, D), F32), jax.ShapeDtypeStruct((1, D), F32), jax.ShapeDtypeStruct((1, 128), F32)],
                 grid=(T // tb,), in_specs=[row, row, par],
                 out_specs=[row, par, pl.BlockSpec((1, 128), lambda i: (0, 0))], dims=("arbitrary",))(h, tgt, g)


CONV_ROWS = 128
CONV_LANES = 256


def _sublane_phases(pad, n):
    for r in range(1, 8):
        for c0 in range(0, n - 8, 256):
            rows = min(256, n - 8 - c0)
            pad[r, c0:c0 + rows, :] = pad[0, c0 + r:c0 + r + rows, :]


def _dwconv_fwd(x, w, b, name, carry=None):
    B, S, C = x.shape
    cb = CONV_LANES
    off = CONV_PAD - (CONV_KERNEL - 1)

    def body(x_ref, w_ref, b_ref, o_ref, pad):
        pad[0, 0:CONV_PAD, :] = jnp.zeros((CONV_PAD, cb), F32)
        pad[0, CONV_PAD:, :] = x_ref[...]
        _sublane_phases(pad, S + CONV_PAD)
        for t0 in range(0, S, CONV_ROWS):
            acc = jnp.broadcast_to(b_ref[...], (CONV_ROWS, cb))
            for k in range(CONV_KERNEL):
                r, base = (off + k) % 8, t0 + (off + k) // 8 * 8
                acc = acc + w_ref[k:k + 1, :] * pad[r, base:base + CONV_ROWS, :]
            o_ref[t0:t0 + CONV_ROWS, :] = acc

    return _call(body, name, jax.ShapeDtypeStruct((B, S, C), F32), grid=(B, C // cb),
                 in_specs=[pl.BlockSpec((None, S, cb), lambda i, j: (i, 0, j)),
                           pl.BlockSpec((CONV_KERNEL, cb), lambda i, j: (0, j)),
                           pl.BlockSpec((1, cb), lambda i, j: (0, j))],
                 out_specs=pl.BlockSpec((None, S, cb), lambda i, j: (i, 0, j)),
                 scratch=[pltpu.VMEM((8, S + CONV_PAD, cb), F32)], dims=("parallel", "parallel"), carry=carry)(x, w, b)


def _dwconv_bwd(x, w, dy, name, carry=None):
    B, S, C = x.shape
    cb = CONV_LANES
    off = CONV_PAD - (CONV_KERNEL - 1)
    groups = CONV_ROWS // 8

    def body(x_ref, w_ref, dy_ref, dx_ref, dw_ref, db_ref, dypad, wacc):
        dypad[0, 0:S, :] = dy_ref[...]
        dypad[0, S:, :] = jnp.zeros((CONV_PAD, cb), F32)
        _sublane_phases(dypad, S + CONV_PAD)
        wacc[...] = jnp.zeros_like(wacc)
        for t0 in range(0, S, CONV_ROWS):
            xc = x_ref[t0:t0 + CONV_ROWS, :]
            acc = jnp.zeros((CONV_ROWS, cb), F32)
            for k in range(CONV_KERNEL):
                o = (CONV_KERNEL - 1) - k
                dys = dypad[o % 8, t0 + o // 8 * 8:t0 + o // 8 * 8 + CONV_ROWS, :]
                acc = acc + w_ref[k:k + 1, :] * dys
                wacc[k] += jnp.sum((dys * xc).reshape(groups, 8, cb), axis=0)
            wacc[CONV_KERNEL] += jnp.sum(dy_ref[t0:t0 + CONV_ROWS, :].reshape(groups, 8, cb), axis=0)
            dx_ref[t0:t0 + CONV_ROWS, :] = acc

        @pl.when(pl.program_id(1) == 0)
        def _():
            dw_ref[...] = jnp.zeros_like(dw_ref)
            db_ref[...] = jnp.zeros_like(db_ref)

        for k in range(CONV_KERNEL):
            dw_ref[k:k + 1, :] += jnp.sum(wacc[k], axis=0, keepdims=True)
        db_ref[...] += jnp.sum(wacc[CONV_KERNEL], axis=0, keepdims=True)

    blk = pl.BlockSpec((None, S, cb), lambda j, i: (i, 0, j))
    return _call(body, name,
                 [jax.ShapeDtypeStruct((B, S, C), F32), jax.ShapeDtypeStruct((CONV_KERNEL, C), F32),
                  jax.ShapeDtypeStruct((1, C), F32)],
                 grid=(C // cb, B),
                 in_specs=[blk, pl.BlockSpec((CONV_KERNEL, cb), lambda j, i: (0, j)), blk],
                 out_specs=[blk, pl.BlockSpec((CONV_KERNEL, cb), lambda j, i: (0, j)),
                            pl.BlockSpec((1, cb), lambda j, i: (0, j))],
                 scratch=[pltpu.VMEM((8, S + CONV_PAD, cb), F32), pltpu.VMEM((CONV_KERNEL + 1, 8, cb), F32)],
                 dims=("parallel", "arbitrary"), carry=carry)(x, w, dy)


ATTN_TILE = {"fwd": 1024, "bwd": 1024, "cross fwd": 512}
ATTN_SUB = {"fwd": 256, "bwd": 512}


def _attn_shapes(Sq, Sk, causal, pass_):
    tq = min(Sq, ATTN_TILE[pass_ if causal or pass_ == "bwd" else "cross fwd"])
    tk = tq if causal else min(Sk, ATTN_TILE[pass_])
    return tq, tk, min(ATTN_SUB[pass_], tq)


def _mask(row0, col0, rows, cols):
    r = row0 + lax.broadcasted_iota(jnp.int32, (rows, cols), 0)
    c = col0 + lax.broadcasted_iota(jnp.int32, (rows, cols), 1)
    return c <= r


def _attn_fwd(q, q_c0, qr, k, k_c0, kr, v, v_c0, B, Sq, Sk, H, causal, scale, name, into=None, o_c0=0, o_width=None,
              kv_stride=1):
    tq, tk, sub = _attn_shapes(Sq, Sk, causal, "fwd")
    nq, nk, nsub = Sq // tq, Sk // tk, tq // sub
    rope = qr is not None

    def body(*refs):
        refs = list(refs)
        qn_ref = refs.pop(0)
        qr_ref = refs.pop(0) if rope else None
        kn_ref = refs.pop(0)
        kr_ref = refs.pop(0) if rope else None
        v_ref = refs.pop(0)
        if into is not None:
            refs.pop(0)
        o_ref, lse_ref, m_s, l_s, acc = refs
        qi = pl.program_id(2)
        m_s[...] = jnp.full_like(m_s, NEG)
        l_s[...] = jnp.zeros_like(l_s)
        acc[...] = jnp.zeros_like(acc)
        qs = []
        for r in range(nsub):
            qn = qn_ref[r * sub:(r + 1) * sub, :].astype(BF16)
            qs.append(jnp.concatenate([qn, qr_ref[r * sub:(r + 1) * sub, :]], axis=1) if rope else qn)

        def step(j, masked):
            ks = pl.ds(pl.multiple_of(j * tk, tk), tk)
            kk = jnp.concatenate([kn_ref[ks, :], kr_ref[ks, :]], axis=1) if rope else kn_ref[ks, :]
            vv = v_ref[ks, :]
            for r in range(nsub):
                rows = slice(r * sub, (r + 1) * sub)
                nc = (r + 1) * sub if masked else tk
                s = lax.dot_general(qs[r], kk[:nc], _DOT_DIMS["nt"], preferred_element_type=F32) * scale
                if masked:
                    s = jnp.where(_mask(qi * tq + r * sub, j * tk, sub, nc), s, NEG)
                m_old = m_s[rows, :]
                m_new = jnp.maximum(m_old, jnp.max(s, axis=-1, keepdims=True))
                p = jnp.exp(s - m_new)
                alpha = jnp.exp(m_old - m_new)
                l_s[rows, :] = alpha * l_s[rows, :] + jnp.sum(p, axis=-1, keepdims=True)
                acc[rows, :] = alpha * acc[rows, :] + jnp.dot(p.astype(BF16), vv[:nc], preferred_element_type=F32)
                m_s[rows, :] = m_new

        def unmasked(j, carry):
            step(j, False)
            return carry

        if causal:
            lax.fori_loop(0, qi, unmasked, 0)
            step(qi, True)
        else:
            lax.fori_loop(0, nk, unmasked, 0)
        o_ref[...] = (acc[...] / l_s[...]).astype(o_ref.dtype)
        lse_ref[...] = m_s[...] + jnp.log(l_s[...])

    qspec = lambda c0: pl.BlockSpec((tq, 128), lambda b, h, i: (b * nq + i, c0 + h))
    kspec = lambda c0: pl.BlockSpec((Sk, 128), lambda b, h, i: (b, c0 + kv_stride * h))
    in_specs, args = [qspec(q_c0)], [q]
    if rope:
        in_specs.append(qspec(0)); args.append(qr)
    in_specs.append(kspec(k_c0)); args.append(k)
    if rope:
        in_specs.append(pl.BlockSpec((Sk, 128), lambda b, h, i: (b, 0))); args.append(kr)
    in_specs.append(kspec(v_c0)); args.append(v)
    aliases = {}
    if into is not None:
        aliases = {len(args): 0}
        in_specs.append(pl.BlockSpec(memory_space=pl.ANY)); args.append(into)
        o_shape = jax.ShapeDtypeStruct(into.shape, into.dtype)
    else:
        o_shape = jax.ShapeDtypeStruct((B * Sq, o_width), F32)
    return _call(body, name, [o_shape, jax.ShapeDtypeStruct((B * H, Sq, 1), F32)], grid=(B, H, nq), in_specs=in_specs,
                 out_specs=[qspec(o_c0), pl.BlockSpec((None, tq, 1), lambda b, h, i: (b * H + h, i, 0))],
                 scratch=[pltpu.VMEM((tq, 1), F32), pltpu.VMEM((tq, 1), F32), pltpu.VMEM((tq, 128), F32)],
                 dims=("parallel", "parallel", "arbitrary"), aliases=aliases)(*args)


def _attn_bwd(q, q_c0, qr, k, k_c0, kr, v, v_c0, o, do, o_c0, lse, B, Sq, Sk, H, causal, scale, name, dq_into=None,
              kv_stride=1):
    tq, tk, sub = _attn_shapes(Sq, Sk, causal, "bwd")
    nq, nk, nsub = Sq // tq, Sk // tk, tq // sub
    rope = qr is not None
    dk_w = 256 if rope else 128

    def body(*refs):
        refs = list(refs)
        qn_ref = refs.pop(0)
        qr_ref = refs.pop(0) if rope else None
        kn_ref = refs.pop(0)
        kr_ref = refs.pop(0) if rope else None
        v_ref, o_ref, do_ref, lse_ref = refs[:4]
        refs = refs[4 + (0 if dq_into is None else 1):]
        dqn_ref = refs.pop(0)
        dqr_ref = refs.pop(0) if rope else None
        dkn_ref = refs.pop(0)
        dkr_ref = refs.pop(0) if rope else None
        dv_ref = None if rope else refs.pop(0)
        q_s, do_s, dl_s, dq_acc, dk_acc, dv_acc = refs
        kj = pl.program_id(2)

        @pl.when(kj == 0)
        def _():
            qn = qn_ref[...].astype(BF16)
            q_s[...] = jnp.concatenate([qn, qr_ref[...]], axis=1) if rope else qn
            dof = do_ref[...]
            do_s[...] = dof.astype(BF16)
            dl_s[...] = jnp.sum(dof * o_ref[...], axis=-1, keepdims=True)
            dq_acc[...] = jnp.zeros_like(dq_acc)

        kk = jnp.concatenate([kn_ref[...], kr_ref[...]], axis=1) if rope else kn_ref[...]
        vv = v_ref[...]
        dk_acc[...] = jnp.zeros_like(dk_acc)
        dv_acc[...] = jnp.zeros_like(dv_acc)

        def step(i, masked):
            for r in range(nsub):
                rows = pl.ds(pl.multiple_of(i * tq + r * sub, sub), sub)
                qq, dob = q_s[rows, :], do_s[rows, :]
                nc = (r + 1) * sub if masked else tk
                kc, vc = kk[:nc], vv[:nc]
                s = lax.dot_general(qq, kc, _DOT_DIMS["nt"], preferred_element_type=F32) * scale
                if masked:
                    s = jnp.where(_mask(i * tq + r * sub, kj * tk, sub, nc), s, NEG)
                p = jnp.exp(s - lse_ref[rows, :])
                dp = lax.dot_general(dob, vc, _DOT_DIMS["nt"], preferred_element_type=F32)
                ds = (p * (dp - dl_s[rows, :]) * scale).astype(BF16)
                dv_acc[0:nc, :] += lax.dot_general(p.astype(BF16), dob, _DOT_DIMS["tn"], preferred_element_type=F32)
                dk_acc[0:nc, :] += lax.dot_general(ds, qq, _DOT_DIMS["tn"], preferred_element_type=F32)
                dq_acc[rows, :] += jnp.dot(ds, kc, preferred_element_type=F32)

        def unmasked(i, carry):
            step(i, False)
            return carry

        if causal:
            step(kj, True)
            lax.fori_loop(kj + 1, nq, unmasked, 0)
        else:
            lax.fori_loop(0, nq, unmasked, 0)
        if rope:
            dkn_ref[...] = jnp.concatenate([dk_acc[:, 0:128], dv_acc[...]], axis=1).astype(dkn_ref.dtype)
            dkr_ref[...] = dk_acc[:, 128:256]
        else:
            dkn_ref[...] = dk_acc[...]
            dv_ref[...] = dv_acc[...]

        @pl.when(kj == nk - 1)
        def _():
            dqn_ref[...] = dq_acc[:, 0:128].astype(dqn_ref.dtype)
            if rope:
                dqr_ref[...] = dq_acc[:, 128:256]

    qspec = lambda c0: pl.BlockSpec((Sq, 128), lambda b, h, j: (b, c0 + h))
    kspec = lambda c0: pl.BlockSpec((tk, 128), lambda b, h, j: (b * nk + j, c0 + kv_stride * h))
    in_specs, args = [qspec(q_c0)], [q]
    if rope:
        in_specs.append(qspec(0)); args.append(qr)
    in_specs.append(kspec(k_c0)); args.append(k)
    if rope:
        in_specs.append(pl.BlockSpec((tk, 128), lambda b, h, j: (b * nk + j, 0))); args.append(kr)
    in_specs += [kspec(v_c0), qspec(o_c0), qspec(o_c0), pl.BlockSpec((None, Sq, 1), lambda b, h, j: (b * H + h, 0, 0))]
    args += [v, o, do, lse]
    h_rows_q = jax.ShapeDtypeStruct((B * Sq, H * 128), F32)
    h_rows_k = jax.ShapeDtypeStruct((B * Sk, H * 128), F32)
    out_shape, out_specs, aliases = [h_rows_q], [qspec(0)], None
    if rope:
        out_shape = [jax.ShapeDtypeStruct((B * Sq, 2 * H * 128), BF16)]
    if dq_into is not None:
        aliases = {len(args): 0}
        in_specs.append(pl.BlockSpec(memory_space=pl.ANY)); args.append(dq_into[0])
        out_shape, out_specs = [jax.ShapeDtypeStruct(dq_into[0].shape, dq_into[0].dtype)], [qspec(dq_into[1])]
    if rope:
        out_shape.append(h_rows_q); out_specs.append(qspec(0))
    hspec = lambda w: pl.BlockSpec((tk, w), lambda b, h, j: (b * nk + j, h))
    if rope:
        out_shape += [jax.ShapeDtypeStruct((B * Sk, H * 256), BF16), h_rows_k]
        out_specs += [hspec(256), hspec(128)]
    else:
        out_shape += [h_rows_k, h_rows_k]
        out_specs += [hspec(128), hspec(128)]
    return _call(body, name, out_shape, grid=(B, H, nk), in_specs=in_specs, out_specs=out_specs,
                 scratch=[pltpu.VMEM((Sq, dk_w), BF16), pltpu.VMEM((Sq, 128), BF16), pltpu.VMEM((Sq, 1), F32),
                          pltpu.VMEM((Sq, dk_w), F32), pltpu.VMEM((tk, dk_w), F32), pltpu.VMEM((tk, 128), F32)],
                 dims=("parallel", "parallel", "arbitrary"), aliases=aliases)(*args)


def _mem_attention_fwd(proj, q_col, ycat, mem2, mem_g, w_mem, B, S, tag):
    M = mem2.shape[0] // B
    (memn,) = _rowwise(_f_rms, [mem2], [mem_g], [(mem2.shape[1], BF16)], tag + "_memnorm")
    kvm = _mm(memn, w_mem, "nn", BF16, tag + "_memkv")
    o_c0 = ycat.shape[1] // 128 - MEM_HEADS
    ycat, lse = _attn_fwd(proj, q_col // 128, None, kvm, 0, None, kvm, MEM_HEADS, B, S, M, MEM_HEADS, False,
                          MEM_HEAD_DIM ** -0.5, tag + "_memattn", into=ycat, o_c0=o_c0)
    return ycat, (memn, kvm, lse)


def _mem_attention_bwd(proj, q_col, ycat, d_ycat, d_proj, saved, mem2, mem_g, w_mem, B, S, tag):
    memn, kvm, lse = saved
    M = mem2.shape[0] // B
    o_c0 = ycat.shape[1] // 128 - MEM_HEADS
    d_q, d_k, d_v = _attn_bwd(proj, q_col // 128, None, kvm, 0, None, kvm, MEM_HEADS, ycat, d_ycat, o_c0, lse, B, S, M,
                              MEM_HEADS, False, MEM_HEAD_DIM ** -0.5, tag + "_memattn_bwd", dq_into=(d_proj, q_col // 128))
    d_kvm = jnp.concatenate([d_k, d_v], axis=1).astype(BF16)
    d_w_mem = _mm(memn, d_kvm, "tn", F32, tag + "_memkv_dw")
    d_memn = _mm(d_kvm, w_mem, "nt", F32, tag + "_memkv_dx")
    _, d_mem_g = _rowwise_bwd(_f_rms, [mem2], [mem_g], [d_memn], 1, tag + "_memnorm_bwd")
    return d_q, d_w_mem, d_mem_g


def _rope_tables(positions):
    inv_freq = 1.0 / (ROPE_THETA ** (jnp.arange(0, MLA_ROPE, 2, dtype=F32) / MLA_ROPE))
    ang = positions.astype(F32).reshape(-1, 1) * inv_freq
    cos, sin, zero = jnp.cos(ang), jnp.sin(ang), jnp.zeros_like(ang)
    return jnp.concatenate([cos, zero, cos, zero], axis=1), jnp.concatenate([-sin, zero, sin, zero], axis=1)


def _forward_backward(x, mem, positions, target, W):
    B, S, D = x.shape
    T = B * S
    conv_w = W["conv_dw"].shape[1]
    mix_w = 2 * D
    h0 = x.reshape(T, D)
    mem2 = mem.reshape(-1, D)
    tgt = target.reshape(T, D)
    row = lambda v: v.reshape(1, -1)
    n_nope = MLA_HEADS * MLA_NOPE

    g0 = row(W["norm_g"][0])
    (u0,) = _rowwise(_f_rms, [h0], [g0], [(D, BF16)], "l0_norm", carry=W.carry("l0_norm"))
    proj0 = _mm(u0, W["conv_w_in"], "nn", F32, "l0_in", carry=W.carry("l0_in"))
    a0, gate0 = (proj0, conv_w, 0), (proj0, conv_w, 1)
    qm0_col, z0_col = 2 * conv_w, 2 * conv_w + MEM_WIDTH
    (glu,) = _rowwise(_f_glu, [a0, gate0], [], [(conv_w, F32)], "l0_glu", carry=W.carry("l0_glu"))
    dw, dwb = W["conv_dw"], row(W["conv_dw_b"][0])
    cv = _dwconv_fwd(glu.reshape(B, S, conv_w), dw, dwb, "l0_dwconv", carry=W.carry("l0_dwconv")).reshape(T, conv_w)
    ln_g, ln_b = row(W["conv_ln_g"][0]), row(W["conv_ln_b"][0])
    (ycat0,) = _rowwise(_f_ln_silu, [cv], [ln_g, ln_b], [(conv_w, F32, mix_w)], "l0_ln", carry=W.carry("l0_ln"))
    mg0 = row(W["mem_norm_g"][0])
    ycat0, mem_saved0 = _mem_attention_fwd(proj0, qm0_col, ycat0, mem2, mg0, W["w_mem_kv"][0], B, S, "l0")
    y0 = _gate_fwd(ycat0, proj0, z0_col, "l0_gate")
    h1 = _mm(y0, W["w_out"][0], "nn", F32, "l0_out", res=h0)

    g1 = row(W["norm_g"][1])
    (u1,) = _rowwise(_f_rms, [h1], [g1], [(D, BF16)], "l1_norm")
    proj1 = _mm(u1, W["mla_w_in"], "nn", F32, "l1_in")
    cq, ckv = (proj1, Q_RANK, 0), (proj1, KV_RANK, Q_RANK // KV_RANK)
    qm1_col = Q_RANK + KV_RANK
    z1_col = qm1_col + MEM_WIDTH
    kr_col = z1_col + mix_w
    qg, kvg = row(W["mla_q_norm_g"]), row(W["mla_kv_norm_g"])
    (cqn,) = _rowwise(_f_rms, [cq], [qg], [(Q_RANK, BF16)], "l1_qnorm")
    (ckvn,) = _rowwise(_f_rms, [ckv], [kvg], [(KV_RANK, BF16)], "l1_kvnorm")
    qf = _mm(cqn, W["mla_w_uq"], "nn", F32, "l1_uq")
    kvf = _mm(ckvn, W["mla_w_ukv"], "nn", BF16, "l1_ukv")
    cos_p, sin_p = _rope_tables(positions)
    qr, kr = _rowwise(_f_rope, [(qf, n_nope, 1), (proj1, 128, kr_col // 128), cos_p, sin_p], [],
                      [(n_nope, BF16), (128, BF16)], "l1_rope")
    scale1 = MLA_QK ** -0.5
    ycat1, lse1 = _attn_fwd(qf, 0, qr, kvf, 0, kr, kvf, 1, B, S, S, MLA_HEADS, True, scale1, "l1_attn",
                            o_width=mix_w, kv_stride=2)
    mg1 = row(W["mem_norm_g"][1])
    ycat1, mem_saved1 = _mem_attention_fwd(proj1, qm1_col, ycat1, mem2, mg1, W["w_mem_kv"][1], B, S, "l1")
    y1 = _gate_fwd(ycat1, proj1, z1_col, "l1_gate")
    h2 = _mm(y1, W["w_out"][1], "nn", F32, "l1_out", res=h1)

    gf = row(W["final_norm_g"])
    dh2, d_gf, loss128 = _final_loss(h2, tgt, gf, "final_loss")
    G = {"final_norm_g": d_gf.reshape(-1)}
    L1 = {}

    dy1 = _mm(dh2, W["w_out"][1], "nt", F32, "l1_out_dx")
    d_wout1 = _mm(y1, dh2, "tn", F32, "l1_out_dw")
    d_ycat1, d_proj1 = _gate_bwd(ycat1, proj1, z1_col, dy1, "l1_gate_bwd")
    d_proj1, d_wmem1, d_mg1 = _mem_attention_bwd(proj1, qm1_col, ycat1, d_ycat1, d_proj1, mem_saved1, mem2, mg1,
                                                 W["w_mem_kv"][1], B, S, "l1")
    d_qf, d_qr, d_kvf, d_kr_heads = _attn_bwd(qf, 0, qr, kvf, 0, kr, kvf, 1, ycat1, d_ycat1, 0, lse1, B, S, S,
                                              MLA_HEADS, True, scale1, "l1_attn_bwd", kv_stride=2)
    d_qf, d_proj1 = _rowwise(_f_rope_t, [d_qr, d_kr_heads, cos_p, sin_p], [], [(n_nope, F32), (128, F32)], "l1_rope_bwd",
                             into=[(0, d_qf, 1), (1, d_proj1, kr_col // 128)])
    d_cqn = _mm(d_qf, W["mla_w_uq"], "nt", F32, "l1_uq_dx")
    L1[("mla_w_uq", None)] = _mm(cqn, d_qf, "tn", F32, "l1_uq_dw")
    d_ckvn = _mm(d_kvf, W["mla_w_ukv"], "nt", F32, "l1_ukv_dx")
    L1[("mla_w_ukv", None)] = _mm(ckvn, d_kvf, "tn", F32, "l1_ukv_dw")
    d_proj1, d_qg = _rowwise_bwd(_f_rms, [cq], [qg], [d_cqn], 1, "l1_qnorm_bwd", into=(d_proj1, cq[2]))
    d_proj1, d_kvg = _rowwise_bwd(_f_rms, [ckv], [kvg], [d_ckvn], 1, "l1_kvnorm_bwd", into=(d_proj1, ckv[2]))
    L1[("w_mem_kv", 1)] = d_wmem1
    L1[("mla_w_in", None)] = _mm(u1, d_proj1, "tn", F32, "l1_in_dw")
    L1[("w_out", 1)] = d_wout1
    W.ready("l1", L1)
    d_u1 = _mm(d_proj1, W["mla_w_in"], "nt", F32, "l1_in_dx", carry=W.carry("l1_in_dx"))
    dh1, d_g1 = _rowwise_bwd(_f_rms, [h1], [g1], [d_u1], 1, "l1_norm_bwd", add=dh2)

    dy0 = _mm(dh1, W["w_out"][0], "nt", F32, "l0_out_dx")
    d_wout0 = _mm(y0, dh1, "tn", F32, "l0_out_dw")
    d_ycat0, d_proj0 = _gate_bwd(ycat0, proj0, z0_col, dy0, "l0_gate_bwd", carry=W.carry("l0_gate_bwd"))
    d_proj0, d_wmem0, d_mg0 = _mem_attention_bwd(proj0, qm0_col, ycat0, d_ycat0, d_proj0, mem_saved0, mem2, mg0,
                                                 W["w_mem_kv"][0], B, S, "l0")
    W.ready("l0a", {("w_mem_kv", 0): d_wmem0, ("w_out", 0): d_wout0})
    d_cv, d_ln_g, d_ln_b = _rowwise_bwd(_f_ln_silu, [cv], [ln_g, ln_b], [(d_ycat0, conv_w, 0)], 1, "l0_ln_bwd",
                                        carry=W.carry("l0_ln_bwd"))
    d_glu, d_dw, d_dwb = _dwconv_bwd(glu.reshape(B, S, conv_w), dw, d_cv.reshape(B, S, conv_w), "l0_dwconv_bwd",
                                     carry=W.carry("l0_dwconv_bwd"))
    d_proj0 = _glu_bwd(proj0, d_glu.reshape(T, conv_w), d_proj0, "l0_glu_bwd")
    d_conv_w_in = _mm(u0, d_proj0, "tn", F32, "l0_in_dw", carry=W.carry("l0_in_dw"))
    W.ready("l0b", {("conv_w_in", None): d_conv_w_in, ("conv_dw", None): d_dw,
                    ("mla_q_norm_g", None): d_qg.reshape(-1), ("mla_kv_norm_g", None): d_kvg.reshape(-1)})
    d_u0 = _mm(d_proj0, W["conv_w_in"], "nt", F32, "l0_in_dx", carry=W.carry("l0_in_dx"))
    dx, d_g0 = _rowwise_bwd(_f_rms, [h0], [g0], [d_u0], 1, "l0_norm_bwd", add=dh1)
    dx = dx.reshape(B, S, D)

    G["norm_g"] = jnp.concatenate([d_g0, d_g1], axis=0)
    G["mem_norm_g"] = jnp.concatenate([d_mg0, d_mg1], axis=0)
    G["conv_dw_b"] = d_dwb
    G["conv_ln_g"], G["conv_ln_b"] = d_ln_g, d_ln_b
    return loss128[0, 0], dx, G


def _mla_in_perm(w):
    c2 = Q_RANK + KV_RANK
    zero = jnp.zeros((w.shape[0], HALF_ROPE), w.dtype)
    return jnp.concatenate([w[:, :c2], w[:, c2 + MLA_ROPE:], w[:, c2:c2 + HALF_ROPE], zero,
                            w[:, c2 + HALF_ROPE:c2 + MLA_ROPE], zero], axis=1)


def _mla_in_unperm(g):
    c2 = Q_RANK + KV_RANK
    r = g.shape[1] - 128
    return jnp.concatenate([g[:, :c2], g[:, r:r + HALF_ROPE], g[:, r + 64:r + 64 + HALF_ROPE], g[:, c2:r]], axis=1)


def _uq_perm(w):
    n = w.shape[0]
    w3 = w.reshape(n, MLA_HEADS, MLA_QK)
    zero = jnp.zeros((n, MLA_HEADS, HALF_ROPE), w.dtype)
    rope = jnp.concatenate([w3[:, :, MLA_NOPE:MLA_NOPE + HALF_ROPE], zero, w3[:, :, MLA_NOPE + HALF_ROPE:], zero], axis=2)
    return jnp.concatenate([w3[:, :, :MLA_NOPE].reshape(n, -1), rope.reshape(n, -1)], axis=1)


def _uq_unperm(g):
    n = g.shape[0]
    n_nope = MLA_HEADS * MLA_NOPE
    rope = g[:, n_nope:].reshape(n, MLA_HEADS, 128)
    return jnp.concatenate([g[:, :n_nope].reshape(n, MLA_HEADS, MLA_NOPE), rope[:, :, :HALF_ROPE],
                            rope[:, :, 64:64 + HALF_ROPE]], axis=2).reshape(n, -1)


_ROW_CUT = ("w_mem_kv", "w_out")
_COL_CUT = ("conv_w_in", "mla_w_in", "mla_w_uq", "mla_w_ukv", "conv_dw")
_BIG = ("w_mem_kv", "w_out", "conv_w_in", "mla_w_in", "mla_w_uq", "mla_w_ukv")
_SMALL_SHARDED = ("conv_dw", "mla_q_norm_g", "mla_kv_norm_g")
_REPLICATED = ("norm_g", "mem_norm_g", "conv_dw_b", "conv_ln_g", "conv_ln_b", "final_norm_g")
_PERM = {"mla_w_in": (_mla_in_perm, _mla_in_unperm), "mla_w_uq": (_uq_perm, _uq_unperm)}


def _join(n, blocks):
    if n in _ROW_CUT:
        _, L, r, c = blocks.shape
        return blocks.transpose(1, 0, 2, 3).reshape(L, N_DEV * r, c)
    if n in _COL_CUT:
        _, _, r, c = blocks.shape
        return blocks.reshape(N_DEV, r, c).transpose(1, 0, 2).reshape(r, N_DEV * c)
    return blocks.reshape(-1)


def _cut(n, full, shard_shape):
    if n in _ROW_CUT:
        L, r, c = shard_shape
        return full.reshape(L, N_DEV, r, c).transpose(1, 0, 2, 3)
    if n in _COL_CUT:
        _, r, c = shard_shape
        return full.reshape(r, N_DEV, c).transpose(1, 0, 2).reshape(N_DEV, 1, r, c)
    return full.reshape(N_DEV, 1, -1)


def _flat_pad(parts, size):
    flat = jnp.concatenate([p.reshape(-1) for p in parts])
    return jnp.concatenate([flat, jnp.zeros((size - flat.shape[0],), flat.dtype)])


SMALL_LANES = 128 * 8


def _as_tiles(flat_parts):
    total = sum(p.size for p in flat_parts)
    size = -(-total // SMALL_LANES) * SMALL_LANES
    return _flat_pad(flat_parts, size).reshape(8, size // 8)


def _split_flat(flat, like):
    out, o = [], 0
    for a in like:
        out.append(flat[o:o + a.size].reshape(a.shape))
        o += a.size
    return out


_HBM = pl.BlockSpec(memory_space=pltpu.HBM)
_VMEM = pl.BlockSpec(memory_space=pltpu.VMEM)


def _position():
    return lax.axis_index("x"), lax.axis_index("y"), lax.axis_index("c")


def _dma_sems(n):
    return [pltpu.SemaphoreType.DMA((n,)), pltpu.SemaphoreType.DMA((n,))]


def _run_stage(stage, name):
    n_in, n_out = len(stage.ins), len(stage.out_shapes)

    def body(*refs):
        ins, outs, sems = refs[:n_in], refs[n_in:n_in + n_out], refs[n_in + n_out:]
        stage.start(ins, outs, sems)
        stage.wait(ins, outs, sems)

    outs = _call(body, name, stage.out_shapes, in_specs=[_HBM] * n_in, out_specs=[_HBM] * n_out, scratch=stage.sems,
                 aliases=stage.aliases)(*stage.ins)
    stage.outs = list(outs)
    return stage.outs


def _gather_chips_stage(shards):
    n = len(shards)

    def copies(x_refs, out_refs, sems):
        send_sems, recv_sems, _ = sems
        x, y, c = _position()
        peers = [(x, y, 1 - c), (1 - x, y, c), (x, 1 - y, c), (1 - x, 1 - y, c)]
        out = []
        for a in range(n):
            for k, (px, py, pc) in enumerate(peers):
                send = pltpu.make_async_remote_copy(src_ref=x_refs[a], dst_ref=out_refs[a].at[4 * x + 2 * y + c],
                                                    send_sem=send_sems.at[4 * a + k], recv_sem=recv_sems.at[4 * a + k],
                                                    device_id=(px, py, pc), device_id_type=MESH)
                recv = pltpu.make_async_remote_copy(src_ref=x_refs[a], dst_ref=out_refs[a].at[4 * px + 2 * py + pc],
                                                    send_sem=send_sems.at[4 * a + k], recv_sem=recv_sems.at[4 * a + k],
                                                    device_id=(px, py, pc), device_id_type=MESH)
                out.append((send, recv))
        return out

    def local(x_refs, out_refs, sems):
        x, y, c = _position()
        return [pltpu.make_async_copy(x_refs[a], out_refs[a].at[4 * x + 2 * y + c], sems[2].at[a]) for a in range(n)]

    def start(x_refs, out_refs, sems):
        for cp in local(x_refs, out_refs, sems):
            cp.start()
        for send, _ in copies(x_refs, out_refs, sems):
            send.start()

    def wait(x_refs, out_refs, sems):
        for send, recv in copies(x_refs, out_refs, sems):
            recv.wait_recv()
            send.wait_send()
        for cp in local(x_refs, out_refs, sems):
            cp.wait()

    return _Stage(shards, [jax.ShapeDtypeStruct((N_DEV,) + a.shape, a.dtype) for a in shards],
                  _dma_sems(4 * n) + [pltpu.SemaphoreType.DMA((n,))], start, wait)


def _gather_sibling_stage(bufs):
    n = len(bufs)

    def copies(out_refs, sems):
        send_sems, recv_sems = sems
        x, y, c = _position()
        out = []
        for a in range(n):
            for j, (px, py) in enumerate([(1 - x, y), (x, 1 - y), (1 - x, 1 - y)]):
                mine, theirs = out_refs[a].at[4 * px + 2 * py + c], out_refs[a].at[4 * px + 2 * py + (1 - c)]
                send = pltpu.make_async_remote_copy(src_ref=mine, dst_ref=mine, send_sem=send_sems.at[3 * a + j],
                                                    recv_sem=recv_sems.at[3 * a + j], device_id=(x, y, 1 - c),
                                                    device_id_type=MESH)
                recv = pltpu.make_async_remote_copy(src_ref=mine, dst_ref=theirs, send_sem=send_sems.at[3 * a + j],
                                                    recv_sem=recv_sems.at[3 * a + j], device_id=(x, y, 1 - c),
                                                    device_id_type=MESH)
                out.append((send, recv))
        return out

    def start(_, out_refs, sems):
        for send, _r in copies(out_refs, sems):
            send.start()

    def wait(_, out_refs, sems):
        for send, recv in copies(out_refs, sems):
            recv.wait_recv()
            send.wait_send()

    return _Stage(bufs, [jax.ShapeDtypeStruct(b.shape, b.dtype) for b in bufs], _dma_sems(3 * n), start, wait,
                  aliases={a: a for a in range(n)})


def _all_gather_small(v, name):
    r, n = v.shape

    def body(x_ref, out_ref, send_sems, recv_sems, local_sem):
        x, y, c = _position()
        me = 4 * x + 2 * y + c
        mine = pltpu.make_async_copy(x_ref, out_ref.at[me], local_sem)
        mine.start()
        flips = [(fx, fy, fc) for fx in (0, 1) for fy in (0, 1) for fc in (0, 1)][1:]
        copies = []
        for k, (fx, fy, fc) in enumerate(flips):
            peer = (x ^ fx, y ^ fy, c ^ fc)
            cp = pltpu.make_async_remote_copy(src_ref=x_ref, dst_ref=out_ref.at[me], send_sem=send_sems.at[k],
                                              recv_sem=recv_sems.at[k], device_id=peer, device_id_type=MESH)
            cp.start()
            copies.append(cp)
        for k, (fx, fy, fc) in enumerate(flips):
            px, py, pc = x ^ fx, y ^ fy, c ^ fc
            src = out_ref.at[4 * px + 2 * py + pc]
            pltpu.make_async_remote_copy(src_ref=x_ref, dst_ref=src, send_sem=send_sems.at[k], recv_sem=recv_sems.at[k],
                                         device_id=(px, py, pc), device_id_type=MESH).wait_recv()
        for cp in copies:
            cp.wait_send()
        mine.wait()

    return _call(body, name, jax.ShapeDtypeStruct((N_DEV, r, n), v.dtype), in_specs=[_VMEM], out_specs=_VMEM,
                 scratch=_dma_sems(7) + [pltpu.SemaphoreType.DMA(())])(v)


def _reduce_sibling_stage(gs):
    n = len(gs)

    def copies(g_refs, out_refs, sems):
        send_sems, recv_sems = sems
        x, y, c = _position()
        return [pltpu.make_async_remote_copy(src_ref=g_refs[a].at[2 * k + (1 - c)], dst_ref=out_refs[a].at[k],
                                             send_sem=send_sems.at[4 * a + k], recv_sem=recv_sems.at[4 * a + k],
                                             device_id=(x, y, 1 - c), device_id_type=MESH)
                for a in range(n) for k in range(4)]

    def start(g_refs, out_refs, sems):
        for cp in copies(g_refs, out_refs, sems):
            cp.start()

    def wait(g_refs, out_refs, sems):
        for cp in copies(g_refs, out_refs, sems):
            cp.wait()

    return _Stage(gs, [jax.ShapeDtypeStruct((4,) + g.shape[1:], g.dtype) for g in gs], _dma_sems(4 * n), start, wait)


def _rows2d(shape):
    cols = shape[-1]
    rows = 1
    for s in shape[:-1]:
        rows *= s
    return rows, cols


def _add_own(g, recv, name):
    rows, cols = _rows2d(g.shape[1:])
    tr = _pick(rows, 256, 8)
    c = lax.axis_index("c").astype(jnp.int32).reshape(1)

    def body(c_ref, g_ref, r_ref, o_ref):
        o_ref[...] = (g_ref[...].astype(F32) + r_ref[...].astype(F32)).astype(o_ref.dtype)

    grid_spec = pltpu.PrefetchScalarGridSpec(
        num_scalar_prefetch=1, grid=(4, rows // tr),
        in_specs=[pl.BlockSpec((None, None, tr, cols), lambda k, i, c_ref: (k, c_ref[0], i, 0)),
                  pl.BlockSpec((None, tr, cols), lambda k, i, c_ref: (k, i, 0))],
        out_specs=pl.BlockSpec((None, tr, cols), lambda k, i, c_ref: (k, i, 0)))
    return _call(body, name, jax.ShapeDtypeStruct((4, rows, cols), g.dtype), grid_spec=grid_spec,
                 dims=("parallel", "parallel"))(c, g.reshape(4, 2, rows, cols), recv.reshape(4, rows, cols))


def _reduce_chips_stage(pas):
    n = len(pas)

    def copies(pa_refs, out_refs, sems):
        send_sems, recv_sems, _ = sems
        x, y, c = _position()
        my_chip = 2 * x + y
        out = []
        for a in range(n):
            for j, (px, py) in enumerate([(1 - x, y), (x, 1 - y), (1 - x, 1 - y)]):
                send = pltpu.make_async_remote_copy(src_ref=pa_refs[a].at[2 * px + py], dst_ref=out_refs[a].at[my_chip],
                                                    send_sem=send_sems.at[3 * a + j], recv_sem=recv_sems.at[3 * a + j],
                                                    device_id=(px, py, c), device_id_type=MESH)
                recv = pltpu.make_async_remote_copy(src_ref=pa_refs[a].at[2 * px + py], dst_ref=out_refs[a].at[2 * px + py],
                                                    send_sem=send_sems.at[3 * a + j], recv_sem=recv_sems.at[3 * a + j],
                                                    device_id=(px, py, c), device_id_type=MESH)
                out.append((send, recv))
        return out

    def local(pa_refs, out_refs, sems):
        x, y, _ = _position()
        return [pltpu.make_async_copy(pa_refs[a].at[2 * x + y], out_refs[a].at[2 * x + y], sems[2].at[a]) for a in range(n)]

    def start(pa_refs, out_refs, sems):
        for cp in local(pa_refs, out_refs, sems):
            cp.start()
        for send, _r in copies(pa_refs, out_refs, sems):
            send.start()

    def wait(pa_refs, out_refs, sems):
        for send, recv in copies(pa_refs, out_refs, sems):
            recv.wait_recv()
            send.wait_send()
        for cp in local(pa_refs, out_refs, sems):
            cp.wait()

    return _Stage(pas, [jax.ShapeDtypeStruct(pa.shape, pa.dtype) for pa in pas],
                  _dma_sems(3 * n) + [pltpu.SemaphoreType.DMA((n,))], start, wait)


def _adamw_math(w, g, m, v):
    m = ADAM_B1 * m + (1.0 - ADAM_B1) * g
    v = ADAM_B2 * v + (1.0 - ADAM_B2) * (g * g)
    m_hat = m / (1.0 - ADAM_B1 ** ADAM_STEP)
    v_hat = v / (1.0 - ADAM_B2 ** ADAM_STEP)
    delta = -ADAM_LR * (m_hat / (jnp.sqrt(v_hat) + ADAM_EPS) + ADAM_WD * w)
    return delta, m, v


def _sum_adamw(parts, w, m, v, name):
    n, rows, cols = parts.shape
    tr = _pick(rows, 128, 8)

    def body(p_ref, w_ref, m_ref, v_ref, g_ref, d_ref, nm_ref, nv_ref):
        g = p_ref[0].astype(F32)
        for k in range(1, n):
            g = g + p_ref[k].astype(F32)
        d, nm, nv = _adamw_math(w_ref[...], g, m_ref[...], v_ref[...])
        g_ref[...], d_ref[...], nm_ref[...], nv_ref[...] = g, d, nm, nv

    blk = pl.BlockSpec((tr, cols), lambda i: (i, 0))
    return _call(body, name, [jax.ShapeDtypeStruct((rows, cols), F32)] * 4, grid=(rows // tr,),
                 in_specs=[pl.BlockSpec((n, tr, cols), lambda i: (0, i, 0)), blk, blk, blk],
                 out_specs=[blk] * 4, dims=("parallel",))(parts, w, m, v)


_WEIGHTS = ("norm_g", "mem_norm_g", "w_mem_kv", "w_out", "conv_w_in", "conv_dw", "conv_dw_b", "conv_ln_g", "conv_ln_b",
            "mla_w_in", "mla_q_norm_g", "mla_w_uq", "mla_kv_norm_g", "mla_w_ukv", "final_norm_g")


_GATHER_GROUPS = {"a": ("conv_w_in",), "b": ("w_mem_kv", "w_out"), "c": ("mla_w_in", "mla_w_uq", "mla_w_ukv")}
_CARRIERS = {"l0_norm": ("gather chips", ("a",)), "l0_in": ("gather chips", ("b",)), "l0_glu": ("gather sibling", ("b",)),
             "l0_dwconv": ("gather chips", ("c",)), "l0_ln": ("gather sibling", ("c",)),
             "l1_in_dx": ("reduce sibling", ("l1",)), "l0_ln_bwd": ("reduce sibling", ("l0a",)),
             "l0_gate_bwd": ("reduce chips", ("l1", 0, 2)), "l0_dwconv_bwd": ("reduce chips", ("l1", 2, 5)),
             "l0_in_dw": ("reduce chips", ("l0a",)), "l0_in_dx": ("reduce sibling alone, then chips", ("l0b",))}


class _Schedule:
    def __init__(self, w):
        self.w, self.full, self.gather, self.reduce, self.reduced = w, {}, {}, {}, {}
        small = _all_gather_small(_as_tiles([w[n] for n in _SMALL_SHARDED]), "gather_small_weights").reshape(N_DEV, -1)
        o = 0
        for n in _SMALL_SHARDED:
            self.full[n] = _join(n, small[:, o:o + w[n].size].reshape((N_DEV,) + w[n].shape))
            o += w[n].size
        for n in _REPLICATED:
            self.full[n] = w[n]

    def carry(self, call):
        kind, (g, *part) = _CARRIERS[call]
        if kind == "gather chips":
            self.gather[g] = [_gather_chips_stage([self.w[n].astype(BF16) for n in _GATHER_GROUPS[g]])]
            return self.gather[g][0]
        if kind == "gather sibling":
            self.gather[g].append(_gather_sibling_stage(self.gather[g][0].outs))
            return self.gather[g][1]
        r = self.reduce[g]
        if kind == "reduce sibling":
            r["sibling"] = _reduce_sibling_stage(r["cut"])
            return r["sibling"]
        if kind != "reduce chips":
            r["sibling"] = _reduce_sibling_stage(r["cut"])
            _run_stage(r["sibling"], "reduce_sibling_" + g)
        if "partial" not in r:
            r["partial"] = [_add_own(c, s, "reduce_add_%s_%d" % (g, i))
                            for i, (c, s) in enumerate(zip(r["cut"], r["sibling"].outs))]
        lo, hi = part if part else (0, len(r["keys"]))
        stage = _reduce_chips_stage(r["partial"][lo:hi])
        r.setdefault("chips", []).append((r["keys"][lo:hi], stage))
        return stage

    def __getitem__(self, name):
        if name not in self.full:
            g = [k for k, names in _GATHER_GROUPS.items() if name in names][0]
            if len(self.gather[g]) == 1:
                self.gather[g].append(_gather_sibling_stage(self.gather[g][0].outs))
                _run_stage(self.gather[g][1], "gather_sibling_" + g)
            for n, buf in zip(_GATHER_GROUPS[g], self.gather[g][1].outs):
                self.full[n] = _PERM[n][0](_join(n, buf)) if n in _PERM else _join(n, buf)
        return self.full[name]

    def ready(self, group, grads, payload=BF16):
        keys, cut, small = [], [], []
        for (n, layer), g in grads.items():
            if n in _SMALL_SHARDED:
                small.append(_cut(n, g, self.w[n].shape).reshape(N_DEV, -1))
                continue
            keys.append((n, layer))
            if layer is not None:
                cut.append(g.reshape((N_DEV,) + self.w[n].shape[1:]).astype(payload))
            else:
                cut.append(_cut(n, _PERM[n][1](g) if n in _PERM else g, self.w[n].shape).astype(payload))
        if small:
            keys.append(("small", None))
            cut.append(jax.vmap(lambda r: _as_tiles([r]))(jnp.concatenate(small, axis=1)))
        self.reduce[group] = {"keys": keys, "cut": cut}

    def finish(self):
        out = {}
        for r in self.reduce.values():
            for keys, stage in r["chips"]:
                out.update(dict(zip(keys, stage.outs)))
        return out


def kernel(x, mem, positions, norm_g, mem_norm_g, w_mem_kv, w_out, conv_w_in, conv_dw, conv_dw_b, conv_ln_g, conv_ln_b, mla_w_in, mla_q_norm_g, mla_w_uq, mla_kv_norm_g, mla_w_ukv, final_norm_g, loss_target, m_norm_g, m_mem_norm_g, m_w_mem_kv, m_w_out, m_conv_w_in, m_conv_dw, m_conv_dw_b, m_conv_ln_g, m_conv_ln_b, m_mla_w_in, m_mla_q_norm_g, m_mla_w_uq, m_mla_kv_norm_g, m_mla_w_ukv, m_final_norm_g, v_norm_g, v_mem_norm_g, v_w_mem_kv, v_w_out, v_conv_w_in, v_conv_dw, v_conv_dw_b, v_conv_ln_g, v_conv_ln_b, v_mla_w_in, v_mla_q_norm_g, v_mla_w_uq, v_mla_kv_norm_g, v_mla_w_ukv, v_final_norm_g):
    w = dict(zip(_WEIGHTS, (norm_g, mem_norm_g, w_mem_kv, w_out, conv_w_in, conv_dw, conv_dw_b, conv_ln_g, conv_ln_b,
                            mla_w_in, mla_q_norm_g, mla_w_uq, mla_kv_norm_g, mla_w_ukv, final_norm_g)))
    m = dict(zip(_WEIGHTS, (m_norm_g, m_mem_norm_g, m_w_mem_kv, m_w_out, m_conv_w_in, m_conv_dw, m_conv_dw_b, m_conv_ln_g,
                            m_conv_ln_b, m_mla_w_in, m_mla_q_norm_g, m_mla_w_uq, m_mla_kv_norm_g, m_mla_w_ukv, m_final_norm_g)))
    v = dict(zip(_WEIGHTS, (v_norm_g, v_mem_norm_g, v_w_mem_kv, v_w_out, v_conv_w_in, v_conv_dw, v_conv_dw_b, v_conv_ln_g,
                            v_conv_ln_b, v_mla_w_in, v_mla_q_norm_g, v_mla_w_uq, v_mla_kv_norm_g, v_mla_w_ukv, v_final_norm_g)))

    sched = _Schedule(w)
    loss_local, dx, G = _forward_backward(x, mem, positions, loss_target, sched)
    loss = lax.psum(loss_local, ("x", "y", "c"))

    from_chips = sched.finish()
    out = [{}, {}, {}, {}]
    for n in _BIG:
        if n in _ROW_CUT:
            res = [_sum_adamw(from_chips[(n, l)], w[n][l], m[n][l], v[n][l], "adamw_%s_%d" % (n, l)) for l in range(w[n].shape[0])]
            res = [jnp.stack(r) for r in zip(*res)]
        else:
            rows, cols = _rows2d(w[n].shape)
            res = _sum_adamw(from_chips[(n, None)], w[n].reshape(rows, cols), m[n].reshape(rows, cols),
                             v[n].reshape(rows, cols), "adamw_" + n)
        for o, r in zip(out, res):
            o[n] = r.reshape(w[n].shape)
    small_like = [w[n] for n in _SMALL_SHARDED]
    res = _sum_adamw(from_chips[("small", None)], _as_tiles(small_like), _as_tiles([m[n] for n in _SMALL_SHARDED]),
                     _as_tiles([v[n] for n in _SMALL_SHARDED]), "adamw_small")
    for o, r in zip(out, res):
        for n, a in zip(_SMALL_SHARDED, _split_flat(r.reshape(-1), small_like)):
            o[n] = a

    rep_like = [w[n] for n in _REPLICATED]
    rep_parts = _all_gather_small(_as_tiles([G[n] for n in _REPLICATED]), "gather_replicated_grads")
    res = _sum_adamw(rep_parts, _as_tiles(rep_like), _as_tiles([m[n] for n in _REPLICATED]),
                     _as_tiles([v[n] for n in _REPLICATED]), "adamw_replicated")
    for o, r in zip(out, res):
        for n, a in zip(_REPLICATED, _split_flat(r.reshape(-1), rep_like)):
            o[n] = a

    return (loss, dx, *[out[0][n] for n in _WEIGHTS], *[out[1][n] for n in _WEIGHTS],
            *[out[2][n] for n in _WEIGHTS], *[out[3][n] for n in _WEIGHTS])
```

```python
import jax
import jax.numpy as jnp
from jax import lax
from jax.experimental import pallas as pl
from jax.experimental.pallas import tpu as pltpu

F32 = jnp.float32
BF16 = jnp.bfloat16
MESH = pl.DeviceIdType.MESH
N_DEV = 8
VMEM_LIMIT_BYTES = 48 * 1024 * 1024

MEM_HEADS, MEM_HEAD_DIM = 4, 128
MEM_WIDTH = MEM_HEADS * MEM_HEAD_DIM
CONV_KERNEL = 31
CONV_PAD = 32
MLA_HEADS, MLA_NOPE, MLA_ROPE, MLA_V = 12, 128, 64, 128
MLA_QK = MLA_NOPE + MLA_ROPE
HALF_ROPE = MLA_ROPE // 2
Q_RANK, KV_RANK = 512, 256
ROPE_THETA = 10000.0
RMS_EPS = 1e-6
LN_EPS = 1e-5
ADAM_LR, ADAM_B1, ADAM_B2, ADAM_EPS, ADAM_WD, ADAM_STEP = 0.001, 0.9, 0.999, 1e-08, 0.01, 10
NEG = -1e30


class _Stage:
    def __init__(self, ins, out_shapes, sems, start, wait, aliases=None):
        self.ins, self.out_shapes, self.sems = list(ins), list(out_shapes), list(sems)
        self.start, self.wait, self.aliases, self.outs = start, wait, dict(aliases or {}), None


def _call(body, name, out_shape, grid=None, in_specs=None, out_specs=None, scratch=(), dims=None, grid_spec=None, aliases=None,
          carry=None):
    params = dict(vmem_limit_bytes=VMEM_LIMIT_BYTES)
    if dims is not None:
        params["dimension_semantics"] = dims
    kw = {}
    if carry is not None:
        single = not isinstance(out_shape, (list, tuple))
        main_out = [out_shape] if single else list(out_shape)
        main_specs = [out_specs] if single else list(out_specs)
        n_in, n_out, n_scr = len(in_specs), len(main_out), len(scratch)
        x_in, x_out = len(carry.ins), len(carry.out_shapes)
        inner, steps = body, tuple(grid)

        def body(*refs):
            ins, xin = refs[:n_in], refs[n_in:n_in + x_in]
            outs = refs[n_in + x_in:n_in + x_in + n_out]
            xout = refs[n_in + x_in + n_out:n_in + x_in + n_out + x_out]
            scr = refs[n_in + x_in + n_out + x_out:n_in + x_in + n_out + x_out + n_scr]
            xsem = refs[n_in + x_in + n_out + x_out + n_scr:]
            ids = [pl.program_id(a) for a in range(len(steps))]
            first, last = ids[0] == 0, ids[0] == steps[0] - 1
            for a in range(1, len(steps)):
                first = jnp.logical_and(first, ids[a] == 0)
                last = jnp.logical_and(last, ids[a] == steps[a] - 1)
            pl.when(first)(lambda: carry.start(xin, xout, xsem))
            inner(*ins, *outs, *scr)
            pl.when(last)(lambda: carry.wait(xin, xout, xsem))

        hbm = pl.BlockSpec(memory_space=pltpu.HBM)
        aliases = dict(aliases or {})
        aliases.update({n_in + k: n_out + v for k, v in carry.aliases.items()})
        res = _call(body, name, main_out + carry.out_shapes, grid=grid, in_specs=list(in_specs) + [hbm] * x_in,
                    out_specs=main_specs + [hbm] * x_out, scratch=list(scratch) + carry.sems, dims=dims, aliases=aliases)

        def run(*args):
            outs = res(*args, *carry.ins)
            carry.outs = list(outs[n_out:])
            return outs[0] if single else outs[:n_out]

        return run
    if aliases:
        kw["input_output_aliases"] = aliases
    if grid_spec is not None:
        kw["grid_spec"] = grid_spec
    else:
        if grid is not None:
            kw["grid"] = grid
        kw["in_specs"] = in_specs
        kw["out_specs"] = out_specs
        kw["scratch_shapes"] = list(scratch)
    return pl.pallas_call(body, name=name, out_shape=out_shape, compiler_params=pltpu.CompilerParams(**params), **kw)


def _pick(n, target, mult):
    best = None
    for d in range(mult, min(n, target) + 1, mult):
        if n % d == 0:
            best = d
    return n if best is None else best


_DOT_DIMS = {"nn": (((1,), (0,)), ((), ())), "nt": (((1,), (1,)), ((), ())), "tn": (((0,), (0,)), ((), ()))}


def _mm(a, b, mode, out_dtype, name, res=None, carry=None):
    if mode == "tn":
        a, mode = a.T, "nn"
    if mode == "nn":
        (M, K), N = a.shape, b.shape[1]
    else:
        (M, K), N = a.shape, b.shape[0]
    tm = _pick(M, 1024, 8)
    tn = _pick(N, 1536, 128)
    tk = _pick(K, 1536, 128)
    nk = K // tk
    has_res = res is not None

    def body(*refs):
        if has_res:
            a_ref, b_ref, r_ref, o_ref, acc = refs
        else:
            a_ref, b_ref, o_ref, acc = refs
        k = pl.program_id(2)
        part = lax.dot_general(a_ref[...].astype(BF16), b_ref[...].astype(BF16), _DOT_DIMS[mode],
                               preferred_element_type=F32)
        if nk == 1:
            o_ref[...] = (part + r_ref[...] if has_res else part).astype(o_ref.dtype)
            return

        @pl.when(k == 0)
        def _():
            acc[...] = part

        @pl.when(k > 0)
        def _():
            acc[...] += part

        @pl.when(k == nk - 1)
        def _():
            r = acc[...]
            if has_res:
                r = r + r_ref[...]
            o_ref[...] = r.astype(o_ref.dtype)

    a_spec = pl.BlockSpec((tm, tk), lambda i, j, k: (i, k))
    b_spec = {"nn": pl.BlockSpec((tk, tn), lambda i, j, k: (k, j)),
              "nt": pl.BlockSpec((tn, tk), lambda i, j, k: (j, k))}[mode]
    o_spec = pl.BlockSpec((tm, tn), lambda i, j, k: (i, j))
    in_specs = [a_spec, b_spec] + ([o_spec] if has_res else [])
    args = (a, b) + ((res,) if has_res else ())
    return _call(body, name, jax.ShapeDtypeStruct((M, N), out_dtype), grid=(M // tm, N // tn, nk),
                 in_specs=in_specs, out_specs=o_spec, scratch=[pltpu.VMEM((tm, tn), F32)],
                 dims=("parallel", "parallel", "arbitrary"), carry=carry)(*args)


def _views(rows):
    return [r if isinstance(r, tuple) else (r, r.shape[1], 0) for r in rows]


def _rowwise(f, rows, params, outs, name, tb=256, carry=None, into=None):
    rows = _views(rows)
    T = rows[0][0].shape[0]
    tb = min(tb, T)
    nr, npar = len(rows), len(params)
    outs = [o if len(o) == 3 else (o[0], o[1], o[0]) for o in outs]
    into = into or []

    def body(*refs):
        vals = f(*[r[...].astype(F32) for r in refs[:nr]], *[p[...] for p in refs[nr:nr + npar]])
        for o_ref, v in zip(refs[nr + npar + len(into):], vals):
            o_ref[...] = v.astype(o_ref.dtype)

    row_spec = lambda w, cb=0: pl.BlockSpec((tb, w), lambda i: (i, cb))
    par_spec = lambda w: pl.BlockSpec((1, w), lambda i: (0, 0))
    out_shape = [jax.ShapeDtypeStruct((T, tw), dt) for _, dt, tw in outs]
    out_specs = [row_spec(w) for w, _, _ in outs]
    in_specs = [row_spec(w, cb) for _, w, cb in rows] + [par_spec(p.shape[1]) for p in params]
    args = [r[0] for r in rows] + list(params)
    aliases = {}
    for k, arr, cb in into:
        aliases[len(args)] = k
        in_specs.append(pl.BlockSpec(memory_space=pl.ANY))
        args.append(arr)
        out_shape[k] = jax.ShapeDtypeStruct(arr.shape, arr.dtype)
        out_specs[k] = row_spec(outs[k][0], cb)
    return _call(body, name, out_shape, grid=(T // tb,), in_specs=in_specs, out_specs=out_specs, dims=("parallel",),
                 carry=carry, aliases=aliases)(*args)


def _rowwise_bwd(f, rows, params, douts, n_diff, name, tb=256, carry=None, add=None, into=None):
    rows, douts = _views(rows), _views(douts)
    T = rows[0][0].shape[0]
    tb = min(tb, T)
    nr, npar, nd = len(rows), len(params), len(douts)
    n_add = 0 if add is None else 1

    def body(*refs):
        rv = [r[...].astype(F32) for r in refs[:nr]]
        pv = [p[...] for p in refs[nr:nr + npar]]
        dv = [d[...].astype(F32) for d in refs[nr + npar:nr + npar + nd]]
        o_refs = refs[nr + npar + nd + n_add + (0 if into is None else 1):]
        fixed = rv[n_diff:]

        def g(*xs):
            return tuple(f(*xs[:n_diff], *fixed, *xs[n_diff:]))

        _, vjp = jax.vjp(g, *rv[:n_diff], *pv)
        grads = list(vjp(tuple(dv)))
        if add is not None:
            grads[0] = grads[0] + refs[nr + npar + nd][...]
        for o_ref, gr in zip(o_refs[:n_diff], grads[:n_diff]):
            o_ref[...] = gr.astype(o_ref.dtype)
        first = pl.program_id(0) == 0
        for o_ref, gr in zip(o_refs[n_diff:], grads[n_diff:]):
            @pl.when(first)
            def _(o_ref=o_ref):
                o_ref[...] = jnp.zeros_like(o_ref)

            o_ref[...] += gr

    row_spec = lambda w, cb=0: pl.BlockSpec((tb, w), lambda i: (i, cb))
    par_spec = lambda w: pl.BlockSpec((1, w), lambda i: (0, 0))
    out_shape = ([jax.ShapeDtypeStruct((T, w), F32) for _, w, _ in rows[:n_diff]]
                 + [jax.ShapeDtypeStruct((1, p.shape[1]), F32) for p in params])
    out_specs = [row_spec(w) for _, w, _ in rows[:n_diff]] + [par_spec(p.shape[1]) for p in params]
    in_specs = ([row_spec(w, cb) for _, w, cb in rows] + [par_spec(p.shape[1]) for p in params]
                + [row_spec(w, cb) for _, w, cb in douts])
    args = [r[0] for r in rows] + list(params) + [d[0] for d in douts]
    aliases = None
    if add is not None:
        in_specs.append(row_spec(add.shape[1]))
        args.append(add)
    if into is not None:
        aliases = {len(args): 0}
        in_specs.append(pl.BlockSpec(memory_space=pl.ANY))
        args.append(into[0])
        out_shape[0] = jax.ShapeDtypeStruct(into[0].shape, into[0].dtype)
        out_specs[0] = row_spec(rows[0][1], into[1])
    return _call(body, name, out_shape, grid=(T // tb,), in_specs=in_specs, out_specs=out_specs,
                 dims=("arbitrary",), carry=carry, aliases=aliases)(*args)


def _sig(x):
    return 1.0 / (1.0 + jnp.exp(-x))


def _rms(x, g):
    return x * lax.rsqrt(jnp.mean(x * x, axis=-1, keepdims=True) + RMS_EPS) * g


def _f_rms(x, g):
    return (_rms(x, g),)


def _f_glu(a, gate):
    return (a * _sig(gate),)


def _f_ln_silu(x, g, b):
    mu = jnp.mean(x, axis=-1, keepdims=True)
    xc = x - mu
    var = jnp.mean(xc * xc, axis=-1, keepdims=True)
    y = xc * lax.rsqrt(var + LN_EPS) * g + b
    return (y * _sig(y),)


def _rope128(x, cos_p, sin_p):
    return x * cos_p + pltpu.roll(x, 64, 1) * sin_p


def _rope128_t(d, cos_p, sin_p):
    return d * cos_p + pltpu.roll(d * sin_p, 64, 1)


def _f_rope(xq, xk, cos_p, sin_p):
    heads = [_rope128(xq[:, h * 128:(h + 1) * 128], cos_p, sin_p) for h in range(MLA_HEADS)]
    return (jnp.concatenate(heads, axis=1), _rope128(xk, cos_p, sin_p))


def _f_rope_t(dq, dk_heads, cos_p, sin_p):
    heads = [_rope128_t(dq[:, h * 128:(h + 1) * 128], cos_p, sin_p) for h in range(MLA_HEADS)]
    dk = dk_heads[:, 0:128]
    for h in range(1, MLA_HEADS):
        dk = dk + dk_heads[:, h * 128:(h + 1) * 128]
    return (jnp.concatenate(heads, axis=1), _rope128_t(dk, cos_p, sin_p))


GATE_LANES = 256


def _gate_fwd(ycat, proj, z_col, name, tb=1024):
    T, width = ycat.shape
    zb = z_col // GATE_LANES

    def body(y_ref, z_ref, o_ref, ot_ref):
        z = z_ref[...]
        y = y_ref[...] * (z * _sig(z))
        o_ref[...] = y.astype(o_ref.dtype)
        ot_ref[...] = y.T.astype(ot_ref.dtype)

    blk = pl.BlockSpec((tb, GATE_LANES), lambda i, c: (i, c))
    return _call(body, name, [jax.ShapeDtypeStruct((T, width), BF16), jax.ShapeDtypeStruct((width, T), BF16)],
                 grid=(T // tb, width // GATE_LANES),
                 in_specs=[blk, pl.BlockSpec((tb, GATE_LANES), lambda i, c: (i, zb + c))],
                 out_specs=[blk, pl.BlockSpec((GATE_LANES, tb), lambda i, c: (c, i))],
                 dims=("parallel", "parallel"))(ycat, proj)


def _gate_bwd(ycat, proj, z_col, dy, name, tb=1024, carry=None):
    T, width = ycat.shape
    zb = z_col // GATE_LANES

    def body(y_ref, z_ref, dy_ref, dycat_ref, dz_ref):
        z, d = z_ref[...], dy_ref[...]
        s = _sig(z)
        dycat_ref[...] = d * (z * s)
        dz_ref[...] = (d * y_ref[...] * (s * (1.0 + z * (1.0 - s)))).astype(dz_ref.dtype)

    blk = pl.BlockSpec((tb, GATE_LANES), lambda i, c: (i, c))
    zblk = pl.BlockSpec((tb, GATE_LANES), lambda i, c: (i, zb + c))
    return _call(body, name, [jax.ShapeDtypeStruct((T, width), F32), jax.ShapeDtypeStruct(proj.shape, BF16)],
                 grid=(T // tb, width // GATE_LANES), in_specs=[blk, zblk, blk], out_specs=[blk, zblk],
                 dims=("parallel", "parallel"), carry=carry)(ycat, proj, dy)


def _glu_bwd(proj, d_glu, d_proj, name, tb=256):
    T, w = d_glu.shape

    def body(a_ref, g_ref, d_ref, _, o_ref):
        s, d = _sig(g_ref[...]), d_ref[...]
        o_ref[:, 0:w] = (d * s).astype(o_ref.dtype)
        o_ref[:, w:2 * w] = (d * a_ref[...] * (s * (1.0 - s))).astype(o_ref.dtype)

    return _call(body, name, jax.ShapeDtypeStruct(d_proj.shape, d_proj.dtype), grid=(T // tb,),
                 in_specs=[pl.BlockSpec((tb, w), lambda i: (i, 0)), pl.BlockSpec((tb, w), lambda i: (i, 1)),
                           pl.BlockSpec((tb, w), lambda i: (i, 0)), pl.BlockSpec(memory_space=pl.ANY)],
                 out_specs=pl.BlockSpec((tb, 2 * w), lambda i: (i, 0)), dims=("parallel",),
                 aliases={3: 0})(proj, proj, d_glu, d_proj)


def _final_loss(h, tgt, g, name, tb=256):
    T, D = h.shape

    def body(h_ref, t_ref, g_ref, dh_ref, dg_ref, loss_ref):
        tv = t_ref[...]

        def rowloss(hh, gg):
            e = _rms(hh, gg) - tv
            return 0.5 * jnp.mean(e * e, axis=-1, keepdims=True)

        lr, vjp = jax.vjp(rowloss, h_ref[...], g_ref[...])
        dh, dg = vjp(jnp.ones_like(lr))
        dh_ref[...] = dh

        @pl.when(pl.program_id(0) == 0)
        def _():
            dg_ref[...] = jnp.zeros_like(dg_ref)
            loss_ref[...] = jnp.zeros_like(loss_ref)

        dg_ref[...] += dg
        loss_ref[...] += jnp.broadcast_to(jnp.sum(lr, axis=0, keepdims=True), loss_ref.shape)

    row = pl.BlockSpec((tb, D), lambda i: (i, 0))
    par = pl.BlockSpec((1, D), lambda i: (0, 0))
    return _call(body, name,
                 [jax.ShapeDtypeStruct((T, D), F32), jax.ShapeDtypeStruct((1, D), F32), jax.ShapeDtypeStruct((1, 128), F32)],
                 grid=(T // tb,), in_specs=[row, row, par],
                 out_specs=[row, par, pl.BlockSpec((1, 128), lambda i: (0, 0))], dims=("arbitrary",))(h, tgt, g)


CONV_ROWS = 128
CONV_LANES = 256


def _sublane_phases(pad, n):
    for r in range(1, 8):
        for c0 in range(0, n - 8, 256):
            rows = min(256, n - 8 - c0)
            pad[r, c0:c0 + rows, :] = pad[0, c0 + r:c0 + r + rows, :]


def _dwconv_fwd(x, w, b, name, carry=None):
    B, S, C = x.shape
    cb = CONV_LANES
    off = CONV_PAD - (CONV_KERNEL - 1)

    def body(x_ref, w_ref, b_ref, o_ref, pad):
        pad[0, 0:CONV_PAD, :] = jnp.zeros((CONV_PAD, cb), F32)
        pad[0, CONV_PAD:, :] = x_ref[...]
        _sublane_phases(pad, S + CONV_PAD)
        for t0 in range(0, S, CONV_ROWS):
            acc = jnp.broadcast_to(b_ref[...], (CONV_ROWS, cb))
            for k in range(CONV_KERNEL):
                r, base = (off + k) % 8, t0 + (off + k) // 8 * 8
                acc = acc + w_ref[k:k + 1, :] * pad[r, base:base + CONV_ROWS, :]
            o_ref[t0:t0 + CONV_ROWS, :] = acc

    return _call(body, name, jax.ShapeDtypeStruct((B, S, C), F32), grid=(B, C // cb),
                 in_specs=[pl.BlockSpec((None, S, cb), lambda i, j: (i, 0, j)),
                           pl.BlockSpec((CONV_KERNEL, cb), lambda i, j: (0, j)),
                           pl.BlockSpec((1, cb), lambda i, j: (0, j))],
                 out_specs=pl.BlockSpec((None, S, cb), lambda i, j: (i, 0, j)),
                 scratch=[pltpu.VMEM((8, S + CONV_PAD, cb), F32)], dims=("parallel", "parallel"), carry=carry)(x, w, b)


def _dwconv_bwd(x, w, dy, name, carry=None):
    B, S, C = x.shape
    cb = CONV_LANES
    off = CONV_PAD - (CONV_KERNEL - 1)
    groups = CONV_ROWS // 8

    def body(x_ref, w_ref, dy_ref, dx_ref, dw_ref, db_ref, dypad, wacc):
        dypad[0, 0:S, :] = dy_ref[...]
        dypad[0, S:, :] = jnp.zeros((CONV_PAD, cb), F32)
        _sublane_phases(dypad, S + CONV_PAD)
        wacc[...] = jnp.zeros_like(wacc)
        for t0 in range(0, S, CONV_ROWS):
            xc = x_ref[t0:t0 + CONV_ROWS, :]
            acc = jnp.zeros((CONV_ROWS, cb), F32)
            for k in range(CONV_KERNEL):
                o = (CONV_KERNEL - 1) - k
                dys = dypad[o % 8, t0 + o // 8 * 8:t0 + o // 8 * 8 + CONV_ROWS, :]
                acc = acc + w_ref[k:k + 1, :] * dys
                wacc[k] += jnp.sum((dys * xc).reshape(groups, 8, cb), axis=0)
            wacc[CONV_KERNEL] += jnp.sum(dy_ref[t0:t0 + CONV_ROWS, :].reshape(groups, 8, cb), axis=0)
            dx_ref[t0:t0 + CONV_ROWS, :] = acc

        @pl.when(pl.program_id(1) == 0)
        def _():
            dw_ref[...] = jnp.zeros_like(dw_ref)
            db_ref[...] = jnp.zeros_like(db_ref)

        for k in range(CONV_KERNEL):
            dw_ref[k:k + 1, :] += jnp.sum(wacc[k], axis=0, keepdims=True)
        db_ref[...] += jnp.sum(wacc[CONV_KERNEL], axis=0, keepdims=True)

    blk = pl.BlockSpec((None, S, cb), lambda j, i: (i, 0, j))
    return _call(body, name,
                 [jax.ShapeDtypeStruct((B, S, C), F32), jax.ShapeDtypeStruct((CONV_KERNEL, C), F32),
                  jax.ShapeDtypeStruct((1, C), F32)],
                 grid=(C // cb, B),
                 in_specs=[blk, pl.BlockSpec((CONV_KERNEL, cb), lambda j, i: (0, j)), blk],
                 out_specs=[blk, pl.BlockSpec((CONV_KERNEL, cb), lambda j, i: (0, j)),
                            pl.BlockSpec((1, cb), lambda j, i: (0, j))],
                 scratch=[pltpu.VMEM((8, S + CONV_PAD, cb), F32), pltpu.VMEM((CONV_KERNEL + 1, 8, cb), F32)],
                 dims=("parallel", "arbitrary"), carry=carry)(x, w, dy)


ATTN_TILE = {"fwd": 1024, "bwd": 1024, "cross fwd": 512}
ATTN_SUB = {"fwd": 256, "bwd": 512}


def _attn_shapes(Sq, Sk, causal, pass_):
    tq = min(Sq, ATTN_TILE[pass_ if causal or pass_ == "bwd" else "cross fwd"])
    tk = tq if causal else min(Sk, ATTN_TILE[pass_])
    return tq, tk, min(ATTN_SUB[pass_], tq)


def _mask(row0, col0, rows, cols):
    r = row0 + lax.broadcasted_iota(jnp.int32, (rows, cols), 0)
    c = col0 + lax.broadcasted_iota(jnp.int32, (rows, cols), 1)
    return c <= r


def _attn_fwd(q, q_c0, qr, k, k_c0, kr, v, v_c0, B, Sq, Sk, H, causal, scale, name, into=None, o_c0=0, o_width=None,
              kv_stride=1):
    tq, tk, sub = _attn_shapes(Sq, Sk, causal, "fwd")
    nq, nk, nsub = Sq // tq, Sk // tk, tq // sub
    rope = qr is not None

    def body(*refs):
        refs = list(refs)
        qn_ref = refs.pop(0)
        qr_ref = refs.pop(0) if rope else None
        kn_ref = refs.pop(0)
        kr_ref = refs.pop(0) if rope else None
        v_ref = refs.pop(0)
        if into is not None:
            refs.pop(0)
        o_ref, lse_ref, m_s, l_s, acc = refs
        qi = pl.program_id(2)
        m_s[...] = jnp.full_like(m_s, NEG)
        l_s[...] = jnp.zeros_like(l_s)
        acc[...] = jnp.zeros_like(acc)
        qs = []
        for r in range(nsub):
            qn = qn_ref[r * sub:(r + 1) * sub, :].astype(BF16)
            qs.append(jnp.concatenate([qn, qr_ref[r * sub:(r + 1) * sub, :]], axis=1) if rope else qn)

        def step(j, masked):
            ks = pl.ds(pl.multiple_of(j * tk, tk), tk)
            kk = jnp.concatenate([kn_ref[ks, :], kr_ref[ks, :]], axis=1) if rope else kn_ref[ks, :]
            vv = v_ref[ks, :]
            for r in range(nsub):
                rows = slice(r * sub, (r + 1) * sub)
                nc = (r + 1) * sub if masked else tk
                s = lax.dot_general(qs[r], kk[:nc], _DOT_DIMS["nt"], preferred_element_type=F32) * scale
                if masked:
                    s = jnp.where(_mask(qi * tq + r * sub, j * tk, sub, nc), s, NEG)
                m_old = m_s[rows, :]
                m_new = jnp.maximum(m_old, jnp.max(s, axis=-1, keepdims=True))
                p = jnp.exp(s - m_new)
                alpha = jnp.exp(m_old - m_new)
                l_s[rows, :] = alpha * l_s[rows, :] + jnp.sum(p, axis=-1, keepdims=True)
                acc[rows, :] = alpha * acc[rows, :] + jnp.dot(p.astype(BF16), vv[:nc], preferred_element_type=F32)
                m_s[rows, :] = m_new

        def unmasked(j, carry):
            step(j, False)
            return carry

        if causal:
            lax.fori_loop(0, qi, unmasked, 0)
            step(qi, True)
        else:
            lax.fori_loop(0, nk, unmasked, 0)
        o_ref[...] = (acc[...] / l_s[...]).astype(o_ref.dtype)
        lse_ref[...] = m_s[...] + jnp.log(l_s[...])

    qspec = lambda c0: pl.BlockSpec((tq, 128), lambda b, h, i: (b * nq + i, c0 + h))
    kspec = lambda c0: pl.BlockSpec((Sk, 128), lambda b, h, i: (b, c0 + kv_stride * h))
    in_specs, args = [qspec(q_c0)], [q]
    if rope:
        in_specs.append(qspec(0)); args.append(qr)
    in_specs.append(kspec(k_c0)); args.append(k)
    if rope:
        in_specs.append(pl.BlockSpec((Sk, 128), lambda b, h, i: (b, 0))); args.append(kr)
    in_specs.append(kspec(v_c0)); args.append(v)
    aliases = {}
    if into is not None:
        aliases = {len(args): 0}
        in_specs.append(pl.BlockSpec(memory_space=pl.ANY)); args.append(into)
        o_shape = jax.ShapeDtypeStruct(into.shape, into.dtype)
    else:
        o_shape = jax.ShapeDtypeStruct((B * Sq, o_width), F32)
    return _call(body, name, [o_shape, jax.ShapeDtypeStruct((B * H, Sq, 1), F32)], grid=(B, H, nq), in_specs=in_specs,
                 out_specs=[qspec(o_c0), pl.BlockSpec((None, tq, 1), lambda b, h, i: (b * H + h, i, 0))],
                 scratch=[pltpu.VMEM((tq, 1), F32), pltpu.VMEM((tq, 1), F32), pltpu.VMEM((tq, 128), F32)],
                 dims=("parallel", "parallel", "arbitrary"), aliases=aliases)(*args)


def _attn_bwd(q, q_c0, qr, k, k_c0, kr, v, v_c0, o, do, o_c0, lse, B, Sq, Sk, H, causal, scale, name, dq_into=None,
              kv_stride=1):
    tq, tk, sub = _attn_shapes(Sq, Sk, causal, "bwd")
    nq, nk, nsub = Sq // tq, Sk // tk, tq // sub
    rope = qr is not None
    dk_w = 256 if rope else 128

    def body(*refs):
        refs = list(refs)
        qn_ref = refs.pop(0)
        qr_ref = refs.pop(0) if rope else None
        kn_ref = refs.pop(0)
        kr_ref = refs.pop(0) if rope else None
        v_ref, o_ref, do_ref, lse_ref = refs[:4]
        refs = refs[4 + (0 if dq_into is None else 1):]
        dqn_ref = refs.pop(0)
        dqr_ref = refs.pop(0) if rope else None
        dkn_ref = refs.pop(0)
        dkr_ref = refs.pop(0) if rope else None
        dv_ref = None if rope else refs.pop(0)
        q_s, do_s, dl_s, dq_acc, dk_acc, dv_acc = refs
        kj = pl.program_id(2)

        @pl.when(kj == 0)
        def _():
            qn = qn_ref[...].astype(BF16)
            q_s[...] = jnp.concatenate([qn, qr_ref[...]], axis=1) if rope else qn
            dof = do_ref[...]
            do_s[...] = dof.astype(BF16)
            dl_s[...] = jnp.sum(dof * o_ref[...], axis=-1, keepdims=True)
            dq_acc[...] = jnp.zeros_like(dq_acc)

        kk = jnp.concatenate([kn_ref[...], kr_ref[...]], axis=1) if rope else kn_ref[...]
        vv = v_ref[...]
        dk_acc[...] = jnp.zeros_like(dk_acc)
        dv_acc[...] = jnp.zeros_like(dv_acc)

        def step(i, masked):
            for r in range(nsub):
                rows = pl.ds(pl.multiple_of(i * tq + r * sub, sub), sub)
                qq, dob = q_s[rows, :], do_s[rows, :]
                nc = (r + 1) * sub if masked else tk
                kc, vc = kk[:nc], vv[:nc]
                s = lax.dot_general(qq, kc, _DOT_DIMS["nt"], preferred_element_type=F32) * scale
                if masked:
                    s = jnp.where(_mask(i * tq + r * sub, kj * tk, sub, nc), s, NEG)
                p = jnp.exp(s - lse_ref[rows, :])
                dp = lax.dot_general(dob, vc, _DOT_DIMS["nt"], preferred_element_type=F32)
                ds = (p * (dp - dl_s[rows, :]) * scale).astype(BF16)
                dv_acc[0:nc, :] += lax.dot_general(p.astype(BF16), dob, _DOT_DIMS["tn"], preferred_element_type=F32)
                dk_acc[0:nc, :] += lax.dot_general(ds, qq, _DOT_DIMS["tn"], preferred_element_type=F32)
                dq_acc[rows, :] += jnp.dot(ds, kc, preferred_element_type=F32)

        def unmasked(i, carry):
            step(i, False)
            return carry

        if causal:
            step(kj, True)
            lax.fori_loop(kj + 1, nq, unmasked, 0)
        else:
            lax.fori_loop(0, nq, unmasked, 0)
        if rope:
            dkn_ref[...] = jnp.concatenate([dk_acc[:, 0:128], dv_acc[...]], axis=1).astype(dkn_ref.dtype)
            dkr_ref[...] = dk_acc[:, 128:256]
        else:
            dkn_ref[...] = dk_acc[...]
            dv_ref[...] = dv_acc[...]

        @pl.when(kj == nk - 1)
        def _():
            dqn_ref[...] = dq_acc[:, 0:128].astype(dqn_ref.dtype)
            if rope:
                dqr_ref[...] = dq_acc[:, 128:256]

    qspec = lambda c0: pl.BlockSpec((Sq, 128), lambda b, h, j: (b, c0 + h))
    kspec = lambda c0: pl.BlockSpec((tk, 128), lambda b, h, j: (b * nk + j, c0 + kv_stride * h))
    in_specs, args = [qspec(q_c0)], [q]
    if rope:
        in_specs.append(qspec(0)); args.append(qr)
    in_specs.append(kspec(k_c0)); args.append(k)
    if rope:
        in_specs.append(pl.BlockSpec((tk, 128), lambda b, h, j: (b * nk + j, 0))); args.append(kr)
    in_specs += [kspec(v_c0), qspec(o_c0), qspec(o_c0), pl.BlockSpec((None, Sq, 1), lambda b, h, j: (b * H + h, 0, 0))]
    args += [v, o, do, lse]
    h_rows_q = jax.ShapeDtypeStruct((B * Sq, H * 128), F32)
    h_rows_k = jax.ShapeDtypeStruct((B * Sk, H * 128), F32)
    out_shape, out_specs, aliases = [h_rows_q], [qspec(0)], None
    if rope:
        out_shape = [jax.ShapeDtypeStruct((B * Sq, 2 * H * 128), BF16)]
    if dq_into is not None:
        aliases = {len(args): 0}
        in_specs.append(pl.BlockSpec(memory_space=pl.ANY)); args.append(dq_into[0])
        out_shape, out_specs = [jax.ShapeDtypeStruct(dq_into[0].shape, dq_into[0].dtype)], [qspec(dq_into[1])]
    if rope:
        out_shape.append(h_rows_q); out_specs.append(qspec(0))
    hspec = lambda w: pl.BlockSpec((tk, w), lambda b, h, j: (b * nk + j, h))
    if rope:
        out_shape += [jax.ShapeDtypeStruct((B * Sk, H * 256), BF16), h_rows_k]
        out_specs += [hspec(256), hspec(128)]
    else:
        out_shape += [h_rows_k, h_rows_k]
        out_specs += [hspec(128), hspec(128)]
    return _call(body, name, out_shape, grid=(B, H, nk), in_specs=in_specs, out_specs=out_specs,
                 scratch=[pltpu.VMEM((Sq, dk_w), BF16), pltpu.VMEM((Sq, 128), BF16), pltpu.VMEM((Sq, 1), F32),
                          pltpu.VMEM((Sq, dk_w), F32), pltpu.VMEM((tk, dk_w), F32), pltpu.VMEM((tk, 128), F32)],
                 dims=("parallel", "parallel", "arbitrary"), aliases=aliases)(*args)


def _mem_attention_fwd(proj, q_col, ycat, mem2, mem_g, w_mem, B, S, tag):
    M = mem2.shape[0] // B
    (memn,) = _rowwise(_f_rms, [mem2], [mem_g], [(mem2.shape[1], BF16)], tag + "_memnorm")
    kvm = _mm(memn, w_mem, "nn", BF16, tag + "_memkv")
    o_c0 = ycat.shape[1] // 128 - MEM_HEADS
    ycat, lse = _attn_fwd(proj, q_col // 128, None, kvm, 0, None, kvm, MEM_HEADS, B, S, M, MEM_HEADS, False,
                          MEM_HEAD_DIM ** -0.5, tag + "_memattn", into=ycat, o_c0=o_c0)
    return ycat, (memn, kvm, lse)


def _mem_attention_bwd(proj, q_col, ycat, d_ycat, d_proj, saved, mem2, mem_g, w_mem, B, S, tag):
    memn, kvm, lse = saved
    M = mem2.shape[0] // B
    o_c0 = ycat.shape[1] // 128 - MEM_HEADS
    d_q, d_k, d_v = _attn_bwd(proj, q_col // 128, None, kvm, 0, None, kvm, MEM_HEADS, ycat, d_ycat, o_c0, lse, B, S, M,
                              MEM_HEADS, False, MEM_HEAD_DIM ** -0.5, tag + "_memattn_bwd", dq_into=(d_proj, q_col // 128))
    d_kvm = jnp.concatenate([d_k, d_v], axis=1).astype(BF16)
    d_w_mem = _mm(memn, d_kvm, "tn", F32, tag + "_memkv_dw")
    d_memn = _mm(d_kvm, w_mem, "nt", F32, tag + "_memkv_dx")
    _, d_mem_g = _rowwise_bwd(_f_rms, [mem2], [mem_g], [d_memn], 1, tag + "_memnorm_bwd")
    return d_q, d_w_mem, d_mem_g


def _rope_tables(positions):
    inv_freq = 1.0 / (ROPE_THETA ** (jnp.arange(0, MLA_ROPE, 2, dtype=F32) / MLA_ROPE))
    ang = positions.astype(F32).reshape(-1, 1) * inv_freq
    cos, sin, zero = jnp.cos(ang), jnp.sin(ang), jnp.zeros_like(ang)
    return jnp.concatenate([cos, zero, cos, zero], axis=1), jnp.concatenate([-sin, zero, sin, zero], axis=1)


def _forward_backward(x, mem, positions, target, W):
    B, S, D = x.shape
    T = B * S
    conv_w = W["conv_dw"].shape[1]
    mix_w = 2 * D
    h0 = x.reshape(T, D)
    mem2 = mem.reshape(-1, D)
    tgt = target.reshape(T, D)
    row = lambda v: v.reshape(1, -1)
    n_nope = MLA_HEADS * MLA_NOPE

    g0 = row(W["norm_g"][0])
    (u0,) = _rowwise(_f_rms, [h0], [g0], [(D, BF16)], "l0_norm", carry=W.carry("l0_norm"))
    proj0 = _mm(u0, W["conv_w_in"], "nn", F32, "l0_in", carry=W.carry("l0_in"))
    a0, gate0 = (proj0, conv_w, 0), (proj0, conv_w, 1)
    qm0_col, z0_col = 2 * conv_w, 2 * conv_w + MEM_WIDTH
    (glu,) = _rowwise(_f_glu, [a0, gate0], [], [(conv_w, F32)], "l0_glu", carry=W.carry("l0_glu"))
    dw, dwb = W["conv_dw"], row(W["conv_dw_b"][0])
    cv = _dwconv_fwd(glu.reshape(B, S, conv_w), dw, dwb, "l0_dwconv", carry=W.carry("l0_dwconv")).reshape(T, conv_w)
    ln_g, ln_b = row(W["conv_ln_g"][0]), row(W["conv_ln_b"][0])
    (ycat0,) = _rowwise(_f_ln_silu, [cv], [ln_g, ln_b], [(conv_w, F32, mix_w)], "l0_ln", carry=W.carry("l0_ln"))
    mg0 = row(W["mem_norm_g"][0])
    ycat0, mem_saved0 = _mem_attention_fwd(proj0, qm0_col, ycat0, mem2, mg0, W["w_mem_kv"][0], B, S, "l0")
    y0, y0_t = _gate_fwd(ycat0, proj0, z0_col, "l0_gate")
    h1 = _mm(y0, W["w_out"][0], "nn", F32, "l0_out", res=h0)

    g1 = row(W["norm_g"][1])
    (u1,) = _rowwise(_f_rms, [h1], [g1], [(D, BF16)], "l1_norm")
    proj1 = _mm(u1, W["mla_w_in"], "nn", F32, "l1_in")
    cq, ckv = (proj1, Q_RANK, 0), (proj1, KV_RANK, Q_RANK // KV_RANK)
    qm1_col = Q_RANK + KV_RANK
    z1_col = qm1_col + MEM_WIDTH
    kr_col = z1_col + mix_w
    qg, kvg = row(W["mla_q_norm_g"]), row(W["mla_kv_norm_g"])
    (cqn,) = _rowwise(_f_rms, [cq], [qg], [(Q_RANK, BF16)], "l1_qnorm")
    (ckvn,) = _rowwise(_f_rms, [ckv], [kvg], [(KV_RANK, BF16)], "l1_kvnorm")
    qf = _mm(cqn, W["mla_w_uq"], "nn", F32, "l1_uq")
    kvf = _mm(ckvn, W["mla_w_ukv"], "nn", BF16, "l1_ukv")
    cos_p, sin_p = _rope_tables(positions)
    qr, kr = _rowwise(_f_rope, [(qf, n_nope, 1), (proj1, 128, kr_col // 128), cos_p, sin_p], [],
                      [(n_nope, BF16), (128, BF16)], "l1_rope")
    scale1 = MLA_QK ** -0.5
    ycat1, lse1 = _attn_fwd(qf, 0, qr, kvf, 0, kr, kvf, 1, B, S, S, MLA_HEADS, True, scale1, "l1_attn",
                            o_width=mix_w, kv_stride=2)
    mg1 = row(W["mem_norm_g"][1])
    ycat1, mem_saved1 = _mem_attention_fwd(proj1, qm1_col, ycat1, mem2, mg1, W["w_mem_kv"][1], B, S, "l1")
    y1, y1_t = _gate_fwd(ycat1, proj1, z1_col, "l1_gate")
    h2 = _mm(y1, W["w_out"][1], "nn", F32, "l1_out", res=h1)

    gf = row(W["final_norm_g"])
    dh2, d_gf, loss128 = _final_loss(h2, tgt, gf, "final_loss")
    G = {"final_norm_g": d_gf.reshape(-1)}
    L1 = {}

    dy1 = _mm(dh2, W["w_out"][1], "nt", F32, "l1_out_dx")
    d_wout1 = _mm(y1_t, dh2, "nn", F32, "l1_out_dw")
    d_ycat1, d_proj1 = _gate_bwd(ycat1, proj1, z1_col, dy1, "l1_gate_bwd")
    d_proj1, d_wmem1, d_mg1 = _mem_attention_bwd(proj1, qm1_col, ycat1, d_ycat1, d_proj1, mem_saved1, mem2, mg1,
                                                 W["w_mem_kv"][1], B, S, "l1")
    d_qf, d_qr, d_kvf, d_kr_heads = _attn_bwd(qf, 0, qr, kvf, 0, kr, kvf, 1, ycat1, d_ycat1, 0, lse1, B, S, S,
                                              MLA_HEADS, True, scale1, "l1_attn_bwd", kv_stride=2)
    d_qf, d_proj1 = _rowwise(_f_rope_t, [d_qr, d_kr_heads, cos_p, sin_p], [], [(n_nope, F32), (128, F32)], "l1_rope_bwd",
                             into=[(0, d_qf, 1), (1, d_proj1, kr_col // 128)])
    d_cqn = _mm(d_qf, W["mla_w_uq"], "nt", F32, "l1_uq_dx")
    L1[("mla_w_uq", None)] = _mm(cqn, d_qf, "tn", F32, "l1_uq_dw")
    d_ckvn = _mm(d_kvf, W["mla_w_ukv"], "nt", F32, "l1_ukv_dx")
    L1[("mla_w_ukv", None)] = _mm(ckvn, d_kvf, "tn", F32, "l1_ukv_dw")
    d_proj1, d_qg = _rowwise_bwd(_f_rms, [cq], [qg], [d_cqn], 1, "l1_qnorm_bwd", into=(d_proj1, cq[2]))
    d_proj1, d_kvg = _rowwise_bwd(_f_rms, [ckv], [kvg], [d_ckvn], 1, "l1_kvnorm_bwd", into=(d_proj1, ckv[2]))
    L1[("w_mem_kv", 1)] = d_wmem1
    L1[("mla_w_in", None)] = _mm(u1, d_proj1, "tn", F32, "l1_in_dw")
    L1[("w_out", 1)] = d_wout1
    W.ready("l1", L1)
    d_u1 = _mm(d_proj1, W["mla_w_in"], "nt", F32, "l1_in_dx", carry=W.carry("l1_in_dx"))
    dh1, d_g1 = _rowwise_bwd(_f_rms, [h1], [g1], [d_u1], 1, "l1_norm_bwd", add=dh2)

    dy0 = _mm(dh1, W["w_out"][0], "nt", F32, "l0_out_dx")
    d_wout0 = _mm(y0_t, dh1, "nn", F32, "l0_out_dw")
    d_ycat0, d_proj0 = _gate_bwd(ycat0, proj0, z0_col, dy0, "l0_gate_bwd", carry=W.carry("l0_gate_bwd"))
    d_proj0, d_wmem0, d_mg0 = _mem_attention_bwd(proj0, qm0_col, ycat0, d_ycat0, d_proj0, mem_saved0, mem2, mg0,
                                                 W["w_mem_kv"][0], B, S, "l0")
    W.ready("l0a", {("w_mem_kv", 0): d_wmem0, ("w_out", 0): d_wout0})
    d_cv, d_ln_g, d_ln_b = _rowwise_bwd(_f_ln_silu, [cv], [ln_g, ln_b], [(d_ycat0, conv_w, 0)], 1, "l0_ln_bwd",
                                        carry=W.carry("l0_ln_bwd"))
    d_glu, d_dw, d_dwb = _dwconv_bwd(glu.reshape(B, S, conv_w), dw, d_cv.reshape(B, S, conv_w), "l0_dwconv_bwd",
                                     carry=W.carry("l0_dwconv_bwd"))
    d_proj0 = _glu_bwd(proj0, d_glu.reshape(T, conv_w), d_proj0, "l0_glu_bwd")
    d_conv_w_in = _mm(u0, d_proj0, "tn", F32, "l0_in_dw", carry=W.carry("l0_in_dw"))
    W.ready("l0b", {("conv_w_in", None): d_conv_w_in, ("conv_dw", None): d_dw,
                    ("mla_q_norm_g", None): d_qg.reshape(-1), ("mla_kv_norm_g", None): d_kvg.reshape(-1)})
    d_u0 = _mm(d_proj0, W["conv_w_in"], "nt", F32, "l0_in_dx", carry=W.carry("l0_in_dx"))
    dx, d_g0 = _rowwise_bwd(_f_rms, [h0], [g0], [d_u0], 1, "l0_norm_bwd", add=dh1)
    dx = dx.reshape(B, S, D)

    G["norm_g"] = jnp.concatenate([d_g0, d_g1], axis=0)
    G["mem_norm_g"] = jnp.concatenate([d_mg0, d_mg1], axis=0)
    G["conv_dw_b"] = d_dwb
    G["conv_ln_g"], G["conv_ln_b"] = d_ln_g, d_ln_b
    return loss128[0, 0], dx, G


def _mla_in_perm(w):
    c2 = Q_RANK + KV_RANK
    zero = jnp.zeros((w.shape[0], HALF_ROPE), w.dtype)
    return jnp.concatenate([w[:, :c2], w[:, c2 + MLA_ROPE:], w[:, c2:c2 + HALF_ROPE], zero,
                            w[:, c2 + HALF_ROPE:c2 + MLA_ROPE], zero], axis=1)


def _mla_in_unperm(g):
    c2 = Q_RANK + KV_RANK
    r = g.shape[1] - 128
    return jnp.concatenate([g[:, :c2], g[:, r:r + HALF_ROPE], g[:, r + 64:r + 64 + HALF_ROPE], g[:, c2:r]], axis=1)


def _uq_perm(w):
    n = w.shape[0]
    w3 = w.reshape(n, MLA_HEADS, MLA_QK)
    zero = jnp.zeros((n, MLA_HEADS, HALF_ROPE), w.dtype)
    rope = jnp.concatenate([w3[:, :, MLA_NOPE:MLA_NOPE + HALF_ROPE], zero, w3[:, :, MLA_NOPE + HALF_ROPE:], zero], axis=2)
    return jnp.concatenate([w3[:, :, :MLA_NOPE].reshape(n, -1), rope.reshape(n, -1)], axis=1)


def _uq_unperm(g):
    n = g.shape[0]
    n_nope = MLA_HEADS * MLA_NOPE
    rope = g[:, n_nope:].reshape(n, MLA_HEADS, 128)
    return jnp.concatenate([g[:, :n_nope].reshape(n, MLA_HEADS, MLA_NOPE), rope[:, :, :HALF_ROPE],
                            rope[:, :, 64:64 + HALF_ROPE]], axis=2).reshape(n, -1)


_ROW_CUT = ("w_mem_kv", "w_out")
_COL_CUT = ("conv_w_in", "mla_w_in", "mla_w_uq", "mla_w_ukv", "conv_dw")
_BIG = ("w_mem_kv", "w_out", "conv_w_in", "mla_w_in", "mla_w_uq", "mla_w_ukv")
_SMALL_SHARDED = ("conv_dw", "mla_q_norm_g", "mla_kv_norm_g")
_REPLICATED = ("norm_g", "mem_norm_g", "conv_dw_b", "conv_ln_g", "conv_ln_b", "final_norm_g")
_PERM = {"mla_w_in": (_mla_in_perm, _mla_in_unperm), "mla_w_uq": (_uq_perm, _uq_unperm)}


def _join(n, blocks):
    if n in _ROW_CUT:
        _, L, r, c = blocks.shape
        return blocks.transpose(1, 0, 2, 3).reshape(L, N_DEV * r, c)
    if n in _COL_CUT:
        _, _, r, c = blocks.shape
        return blocks.reshape(N_DEV, r, c).transpose(1, 0, 2).reshape(r, N_DEV * c)
    return blocks.reshape(-1)


def _cut(n, full, shard_shape):
    if n in _ROW_CUT:
        L, r, c = shard_shape
        return full.reshape(L, N_DEV, r, c).transpose(1, 0, 2, 3)
    if n in _COL_CUT:
        _, r, c = shard_shape
        return full.reshape(r, N_DEV, c).transpose(1, 0, 2).reshape(N_DEV, 1, r, c)
    return full.reshape(N_DEV, 1, -1)


def _flat_pad(parts, size):
    flat = jnp.concatenate([p.reshape(-1) for p in parts])
    return jnp.concatenate([flat, jnp.zeros((size - flat.shape[0],), flat.dtype)])


SMALL_LANES = 128 * 8


def _as_tiles(flat_parts):
    total = sum(p.size for p in flat_parts)
    size = -(-total // SMALL_LANES) * SMALL_LANES
    return _flat_pad(flat_parts, size).reshape(8, size // 8)


def _split_flat(flat, like):
    out, o = [], 0
    for a in like:
        out.append(flat[o:o + a.size].reshape(a.shape))
        o += a.size
    return out


_HBM = pl.BlockSpec(memory_space=pltpu.HBM)
_VMEM = pl.BlockSpec(memory_space=pltpu.VMEM)


def _position():
    return lax.axis_index("x"), lax.axis_index("y"), lax.axis_index("c")


def _dma_sems(n):
    return [pltpu.SemaphoreType.DMA((n,)), pltpu.SemaphoreType.DMA((n,))]


def _run_stage(stage, name):
    n_in, n_out = len(stage.ins), len(stage.out_shapes)

    def body(*refs):
        ins, outs, sems = refs[:n_in], refs[n_in:n_in + n_out], refs[n_in + n_out:]
        stage.start(ins, outs, sems)
        stage.wait(ins, outs, sems)

    outs = _call(body, name, stage.out_shapes, in_specs=[_HBM] * n_in, out_specs=[_HBM] * n_out, scratch=stage.sems,
                 aliases=stage.aliases)(*stage.ins)
    stage.outs = list(outs)
    return stage.outs


def _gather_chips_stage(shards):
    n = len(shards)

    def copies(x_refs, out_refs, sems):
        send_sems, recv_sems, _ = sems
        x, y, c = _position()
        peers = [(x, y, 1 - c), (1 - x, y, c), (x, 1 - y, c), (1 - x, 1 - y, c)]
        out = []
        for a in range(n):
            for k, (px, py, pc) in enumerate(peers):
                send = pltpu.make_async_remote_copy(src_ref=x_refs[a], dst_ref=out_refs[a].at[4 * x + 2 * y + c],
                                                    send_sem=send_sems.at[4 * a + k], recv_sem=recv_sems.at[4 * a + k],
                                                    device_id=(px, py, pc), device_id_type=MESH)
                recv = pltpu.make_async_remote_copy(src_ref=x_refs[a], dst_ref=out_refs[a].at[4 * px + 2 * py + pc],
                                                    send_sem=send_sems.at[4 * a + k], recv_sem=recv_sems.at[4 * a + k],
                                                    device_id=(px, py, pc), device_id_type=MESH)
                out.append((send, recv))
        return out

    def local(x_refs, out_refs, sems):
        x, y, c = _position()
        return [pltpu.make_async_copy(x_refs[a], out_refs[a].at[4 * x + 2 * y + c], sems[2].at[a]) for a in range(n)]

    def start(x_refs, out_refs, sems):
        for cp in local(x_refs, out_refs, sems):
            cp.start()
        for send, _ in copies(x_refs, out_refs, sems):
            send.start()

    def wait(x_refs, out_refs, sems):
        for send, recv in copies(x_refs, out_refs, sems):
            recv.wait_recv()
            send.wait_send()
        for cp in local(x_refs, out_refs, sems):
            cp.wait()

    return _Stage(shards, [jax.ShapeDtypeStruct((N_DEV,) + a.shape, a.dtype) for a in shards],
                  _dma_sems(4 * n) + [pltpu.SemaphoreType.DMA((n,))], start, wait)


def _gather_sibling_stage(bufs):
    n = len(bufs)

    def copies(out_refs, sems):
        send_sems, recv_sems = sems
        x, y, c = _position()
        out = []
        for a in range(n):
            for j, (px, py) in enumerate([(1 - x, y), (x, 1 - y), (1 - x, 1 - y)]):
                mine, theirs = out_refs[a].at[4 * px + 2 * py + c], out_refs[a].at[4 * px + 2 * py + (1 - c)]
                send = pltpu.make_async_remote_copy(src_ref=mine, dst_ref=mine, send_sem=send_sems.at[3 * a + j],
                                                    recv_sem=recv_sems.at[3 * a + j], device_id=(x, y, 1 - c),
                                                    device_id_type=MESH)
                recv = pltpu.make_async_remote_copy(src_ref=mine, dst_ref=theirs, send_sem=send_sems.at[3 * a + j],
                                                    recv_sem=recv_sems.at[3 * a + j], device_id=(x, y, 1 - c),
                                                    device_id_type=MESH)
                out.append((send, recv))
        return out

    def start(_, out_refs, sems):
        for send, _r in copies(out_refs, sems):
            send.start()

    def wait(_, out_refs, sems):
        for send, recv in copies(out_refs, sems):
            recv.wait_recv()
            send.wait_send()

    return _Stage(bufs, [jax.ShapeDtypeStruct(b.shape, b.dtype) for b in bufs], _dma_sems(3 * n), start, wait,
                  aliases={a: a for a in range(n)})


def _all_gather_small(v, name):
    r, n = v.shape

    def body(x_ref, out_ref, send_sems, recv_sems, local_sem):
        x, y, c = _position()
        me = 4 * x + 2 * y + c
        mine = pltpu.make_async_copy(x_ref, out_ref.at[me], local_sem)
        mine.start()
        flips = [(fx, fy, fc) for fx in (0, 1) for fy in (0, 1) for fc in (0, 1)][1:]
        copies = []
        for k, (fx, fy, fc) in enumerate(flips):
            peer = (x ^ fx, y ^ fy, c ^ fc)
            cp = pltpu.make_async_remote_copy(src_ref=x_ref, dst_ref=out_ref.at[me], send_sem=send_sems.at[k],
                                              recv_sem=recv_sems.at[k], device_id=peer, device_id_type=MESH)
            cp.start()
            copies.append(cp)
        for k, (fx, fy, fc) in enumerate(flips):
            px, py, pc = x ^ fx, y ^ fy, c ^ fc
            src = out_ref.at[4 * px + 2 * py + pc]
            pltpu.make_async_remote_copy(src_ref=x_ref, dst_ref=src, send_sem=send_sems.at[k], recv_sem=recv_sems.at[k],
                                         device_id=(px, py, pc), device_id_type=MESH).wait_recv()
        for cp in copies:
            cp.wait_send()
        mine.wait()

    return _call(body, name, jax.ShapeDtypeStruct((N_DEV, r, n), v.dtype), in_specs=[_VMEM], out_specs=_VMEM,
                 scratch=_dma_sems(7) + [pltpu.SemaphoreType.DMA(())])(v)


def _reduce_sibling_stage(gs):
    n = len(gs)

    def copies(g_refs, out_refs, sems):
        send_sems, recv_sems = sems
        x, y, c = _position()
        return [pltpu.make_async_remote_copy(src_ref=g_refs[a].at[2 * k + (1 - c)], dst_ref=out_refs[a].at[k],
                                             send_sem=send_sems.at[4 * a + k], recv_sem=recv_sems.at[4 * a + k],
                                             device_id=(x, y, 1 - c), device_id_type=MESH)
                for a in range(n) for k in range(4)]

    def start(g_refs, out_refs, sems):
        for cp in copies(g_refs, out_refs, sems):
            cp.start()

    def wait(g_refs, out_refs, sems):
        for cp in copies(g_refs, out_refs, sems):
            cp.wait()

    return _Stage(gs, [jax.ShapeDtypeStruct((4,) + g.shape[1:], g.dtype) for g in gs], _dma_sems(4 * n), start, wait)


def _rows2d(shape):
    cols = shape[-1]
    rows = 1
    for s in shape[:-1]:
        rows *= s
    return rows, cols


def _add_own(g, recv, name):
    rows, cols = _rows2d(g.shape[1:])
    tr = _pick(rows, 256, 8)
    c = lax.axis_index("c").astype(jnp.int32).reshape(1)

    def body(c_ref, g_ref, r_ref, o_ref):
        o_ref[...] = (g_ref[...].astype(F32) + r_ref[...].astype(F32)).astype(o_ref.dtype)

    grid_spec = pltpu.PrefetchScalarGridSpec(
        num_scalar_prefetch=1, grid=(4, rows // tr),
        in_specs=[pl.BlockSpec((None, None, tr, cols), lambda k, i, c_ref: (k, c_ref[0], i, 0)),
                  pl.BlockSpec((None, tr, cols), lambda k, i, c_ref: (k, i, 0))],
        out_specs=pl.BlockSpec((None, tr, cols), lambda k, i, c_ref: (k, i, 0)))
    return _call(body, name, jax.ShapeDtypeStruct((4, rows, cols), g.dtype), grid_spec=grid_spec,
                 dims=("parallel", "parallel"))(c, g.reshape(4, 2, rows, cols), recv.reshape(4, rows, cols))


def _reduce_chips_stage(pas):
    n = len(pas)

    def copies(pa_refs, out_refs, sems):
        send_sems, recv_sems, _ = sems
        x, y, c = _position()
        my_chip = 2 * x + y
        out = []
        for a in range(n):
            for j, (px, py) in enumerate([(1 - x, y), (x, 1 - y), (1 - x, 1 - y)]):
                send = pltpu.make_async_remote_copy(src_ref=pa_refs[a].at[2 * px + py], dst_ref=out_refs[a].at[my_chip],
                                                    send_sem=send_sems.at[3 * a + j], recv_sem=recv_sems.at[3 * a + j],
                                                    device_id=(px, py, c), device_id_type=MESH)
                recv = pltpu.make_async_remote_copy(src_ref=pa_refs[a].at[2 * px + py], dst_ref=out_refs[a].at[2 * px + py],
                                                    send_sem=send_sems.at[3 * a + j], recv_sem=recv_sems.at[3 * a + j],
                                                    device_id=(px, py, c), device_id_type=MESH)
                out.append((send, recv))
        return out

    def local(pa_refs, out_refs, sems):
        x, y, _ = _position()
        return [pltpu.make_async_copy(pa_refs[a].at[2 * x + y], out_refs[a].at[2 * x + y], sems[2].at[a]) for a in range(n)]

    def start(pa_refs, out_refs, sems):
        for cp in local(pa_refs, out_refs, sems):
            cp.start()
        for send, _r in copies(pa_refs, out_refs, sems):
            send.start()

    def wait(pa_refs, out_refs, sems):
        for send, recv in copies(pa_refs, out_refs, sems):
            recv.wait_recv()
            send.wait_send()
        for cp in local(pa_refs, out_refs, sems):
            cp.wait()

    return _Stage(pas, [jax.ShapeDtypeStruct(pa.shape, pa.dtype) for pa in pas],
                  _dma_sems(3 * n) + [pltpu.SemaphoreType.DMA((n,))], start, wait)


def _adamw_math(w, g, m, v):
    m = ADAM_B1 * m + (1.0 - ADAM_B1) * g
    v = ADAM_B2 * v + (1.0 - ADAM_B2) * (g * g)
    m_hat = m / (1.0 - ADAM_B1 ** ADAM_STEP)
    v_hat = v / (1.0 - ADAM_B2 ** ADAM_STEP)
    delta = -ADAM_LR * (m_hat / (jnp.sqrt(v_hat) + ADAM_EPS) + ADAM_WD * w)
    return delta, m, v


def _sum_adamw(parts, w, m, v, name):
    n, rows, cols = parts.shape
    tr = _pick(rows, 128, 8)

    def body(p_ref, w_ref, m_ref, v_ref, g_ref, d_ref, nm_ref, nv_ref):
        g = p_ref[0].astype(F32)
        for k in range(1, n):
            g = g + p_ref[k].astype(F32)
        d, nm, nv = _adamw_math(w_ref[...], g, m_ref[...], v_ref[...])
        g_ref[...], d_ref[...], nm_ref[...], nv_ref[...] = g, d, nm, nv

    blk = pl.BlockSpec((tr, cols), lambda i: (i, 0))
    return _call(body, name, [jax.ShapeDtypeStruct((rows, cols), F32)] * 4, grid=(rows // tr,),
                 in_specs=[pl.BlockSpec((n, tr, cols), lambda i: (0, i, 0)), blk, blk, blk],
                 out_specs=[blk] * 4, dims=("parallel",))(parts, w, m, v)


_WEIGHTS = ("norm_g", "mem_norm_g", "w_mem_kv", "w_out", "conv_w_in", "conv_dw", "conv_dw_b", "conv_ln_g", "conv_ln_b",
            "mla_w_in", "mla_q_norm_g", "mla_w_uq", "mla_kv_norm_g", "mla_w_ukv", "final_norm_g")


_GATHER_GROUPS = {"a": ("conv_w_in",), "b": ("w_mem_kv", "w_out"), "c": ("mla_w_in", "mla_w_uq", "mla_w_ukv")}
_CARRIERS = {"l0_norm": ("gather chips", ("a",)), "l0_in": ("gather chips", ("b",)), "l0_glu": ("gather sibling", ("b",)),
             "l0_dwconv": ("gather chips", ("c",)), "l0_ln": ("gather sibling", ("c",)),
             "l1_in_dx": ("reduce sibling", ("l1",)), "l0_ln_bwd": ("reduce sibling", ("l0a",)),
             "l0_gate_bwd": ("reduce chips", ("l1", 0, 2)), "l0_dwconv_bwd": ("reduce chips", ("l1", 2, 5)),
             "l0_in_dw": ("reduce chips", ("l0a",)), "l0_in_dx": ("reduce sibling alone, then chips", ("l0b",))}


class _Schedule:
    def __init__(self, w):
        self.w, self.full, self.gather, self.reduce, self.reduced = w, {}, {}, {}, {}
        small = _all_gather_small(_as_tiles([w[n] for n in _SMALL_SHARDED]), "gather_small_weights").reshape(N_DEV, -1)
        o = 0
        for n in _SMALL_SHARDED:
            self.full[n] = _join(n, small[:, o:o + w[n].size].reshape((N_DEV,) + w[n].shape))
            o += w[n].size
        for n in _REPLICATED:
            self.full[n] = w[n]

    def carry(self, call):
        kind, (g, *part) = _CARRIERS[call]
        if kind == "gather chips":
            self.gather[g] = [_gather_chips_stage([self.w[n].astype(BF16) for n in _GATHER_GROUPS[g]])]
            return self.gather[g][0]
        if kind == "gather sibling":
            self.gather[g].append(_gather_sibling_stage(self.gather[g][0].outs))
            return self.gather[g][1]
        r = self.reduce[g]
        if kind == "reduce sibling":
            r["sibling"] = _reduce_sibling_stage(r["cut"])
            return r["sibling"]
        if kind != "reduce chips":
            r["sibling"] = _reduce_sibling_stage(r["cut"])
            _run_stage(r["sibling"], "reduce_sibling_" + g)
        if "partial" not in r:
            r["partial"] = [_add_own(c, s, "reduce_add_%s_%d" % (g, i))
                            for i, (c, s) in enumerate(zip(r["cut"], r["sibling"].outs))]
        lo, hi = part if part else (0, len(r["keys"]))
        stage = _reduce_chips_stage(r["partial"][lo:hi])
        r.setdefault("chips", []).append((r["keys"][lo:hi], stage))
        return stage

    def __getitem__(self, name):
        if name not in self.full:
            g = [k for k, names in _GATHER_GROUPS.items() if name in names][0]
            if len(self.gather[g]) == 1:
                self.gather[g].append(_gather_sibling_stage(self.gather[g][0].outs))
                _run_stage(self.gather[g][1], "gather_sibling_" + g)
            for n, buf in zip(_GATHER_GROUPS[g], self.gather[g][1].outs):
                self.full[n] = _PERM[n][0](_join(n, buf)) if n in _PERM else _join(n, buf)
        return self.full[name]

    def ready(self, group, grads, payload=BF16):
        keys, cut, small = [], [], []
        for (n, layer), g in grads.items():
            if n in _SMALL_SHARDED:
                small.append(_cut(n, g, self.w[n].shape).reshape(N_DEV, -1))
                continue
            keys.append((n, layer))
            if layer is not None:
                cut.append(g.reshape((N_DEV,) + self.w[n].shape[1:]).astype(payload))
            else:
                cut.append(_cut(n, _PERM[n][1](g) if n in _PERM else g, self.w[n].shape).astype(payload))
        if small:
            keys.append(("small", None))
            cut.append(jax.vmap(lambda r: _as_tiles([r]))(jnp.concatenate(small, axis=1)))
        self.reduce[group] = {"keys": keys, "cut": cut}

    def finish(self):
        out = {}
        for r in self.reduce.values():
            for keys, stage in r["chips"]:
                out.update(dict(zip(keys, stage.outs)))
        return out


def kernel(x, mem, positions, norm_g, mem_norm_g, w_mem_kv, w_out, conv_w_in, conv_dw, conv_dw_b, conv_ln_g, conv_ln_b, mla_w_in, mla_q_norm_g, mla_w_uq, mla_kv_norm_g, mla_w_ukv, final_norm_g, loss_target, m_norm_g, m_mem_norm_g, m_w_mem_kv, m_w_out, m_conv_w_in, m_conv_dw, m_conv_dw_b, m_conv_ln_g, m_conv_ln_b, m_mla_w_in, m_mla_q_norm_g, m_mla_w_uq, m_mla_kv_norm_g, m_mla_w_ukv, m_final_norm_g, v_norm_g, v_mem_norm_g, v_w_mem_kv, v_w_out, v_conv_w_in, v_conv_dw, v_conv_dw_b, v_conv_ln_g, v_conv_ln_b, v_mla_w_in, v_mla_q_norm_g, v_mla_w_uq, v_mla_kv_norm_g, v_mla_w_ukv, v_final_norm_g):
    w = dict(zip(_WEIGHTS, (norm_g, mem_norm_g, w_mem_kv, w_out, conv_w_in, conv_dw, conv_dw_b, conv_ln_g, conv_ln_b,
                            mla_w_in, mla_q_norm_g, mla_w_uq, mla_kv_norm_g, mla_w_ukv, final_norm_g)))
    m = dict(zip(_WEIGHTS, (m_norm_g, m_mem_norm_g, m_w_mem_kv, m_w_out, m_conv_w_in, m_conv_dw, m_conv_dw_b, m_conv_ln_g,
                            m_conv_ln_b, m_mla_w_in, m_mla_q_norm_g, m_mla_w_uq, m_mla_kv_norm_g, m_mla_w_ukv, m_final_norm_g)))
    v = dict(zip(_WEIGHTS, (v_norm_g, v_mem_norm_g, v_w_mem_kv, v_w_out, v_conv_w_in, v_conv_dw, v_conv_dw_b, v_conv_ln_g,
                            v_conv_ln_b, v_mla_w_in, v_mla_q_norm_g, v_mla_w_uq, v_mla_kv_norm_g, v_mla_w_ukv, v_final_norm_g)))

    sched = _Schedule(w)
    loss_local, dx, G = _forward_backward(x, mem, positions, loss_target, sched)
    loss = lax.psum(loss_local, ("x", "y", "c"))

    from_chips = sched.finish()
    out = [{}, {}, {}, {}]
    for n in _BIG:
        if n in _ROW_CUT:
            res = [_sum_adamw(from_chips[(n, l)], w[n][l], m[n][l], v[n][l], "adamw_%s_%d" % (n, l)) for l in range(w[n].shape[0])]
            res = [jnp.stack(r) for r in zip(*res)]
        else:
            rows, cols = _rows2d(w[n].shape)
            res = _sum_adamw(from_chips[(n, None)], w[n].reshape(rows, cols), m[n].reshape(rows, cols),
                             v[n].reshape(rows, cols), "adamw_" + n)
        for o, r in zip(out, res):
            o[n] = r.reshape(w[n].shape)
    small_like = [w[n] for n in _SMALL_SHARDED]
    res = _sum_adamw(from_chips[("small", None)], _as_tiles(small_like), _as_tiles([m[n] for n in _SMALL_SHARDED]),
                     _as_tiles([v[n] for n in _SMALL_SHARDED]), "adamw_small")
    for o, r in zip(out, res):
        for n, a in zip(_SMALL_SHARDED, _split_flat(r.reshape(-1), small_like)):
            o[n] = a

    rep_like = [w[n] for n in _REPLICATED]
    rep_parts = _all_gather_small(_as_tiles([G[n] for n in _REPLICATED]), "gather_replicated_grads")
    res = _sum_adamw(rep_parts, _as_tiles(rep_like), _as_tiles([m[n] for n in _REPLICATED]),
                     _as_tiles([v[n] for n in _REPLICATED]), "adamw_replicated")
    for o, r in zip(out, res):
        for n, a in zip(_REPLICATED, _split_flat(r.reshape(-1), rep_like)):
            o[n] = a

    return (loss, dx, *[out[0][n] for n in _WEIGHTS], *[out[1][n] for n in _WEIGHTS],
            *[out[2][n] for n in _WEIGHTS], *[out[3][n] for n in _WEIGHTS])
```

```python
import jax
import jax.numpy as jnp
from jax import lax
from jax.experimental import pallas as pl
from jax.experimental.pallas import tpu as pltpu

F32 = jnp.float32
BF16 = jnp.bfloat16
MESH = pl.DeviceIdType.MESH
N_DEV = 8
VMEM_LIMIT_BYTES = 48 * 1024 * 1024

MEM_HEADS, MEM_HEAD_DIM = 4, 128
MEM_WIDTH = MEM_HEADS * MEM_HEAD_DIM
CONV_KERNEL = 31
CONV_PAD = 32
MLA_HEADS, MLA_NOPE, MLA_ROPE, MLA_V = 12, 128, 64, 128
MLA_QK = MLA_NOPE + MLA_ROPE
HALF_ROPE = MLA_ROPE // 2
Q_RANK, KV_RANK = 512, 256
ROPE_THETA = 10000.0
RMS_EPS = 1e-6
LN_EPS = 1e-5
ADAM_LR, ADAM_B1, ADAM_B2, ADAM_EPS, ADAM_WD, ADAM_STEP = 0.001, 0.9, 0.999, 1e-08, 0.01, 10
NEG = -1e30


class _Stage:
    def __init__(self, ins, out_shapes, sems, start, wait, aliases=None):
        self.ins, self.out_shapes, self.sems = list(ins), list(out_shapes), list(sems)
        self.start, self.wait, self.aliases, self.outs = start, wait, dict(aliases or {}), None


def _call(body, name, out_shape, grid=None, in_specs=None, out_specs=None, scratch=(), dims=None, grid_spec=None, aliases=None,
          carry=None):
    params = dict(vmem_limit_bytes=VMEM_LIMIT_BYTES)
    if dims is not None:
        params["dimension_semantics"] = dims
    kw = {}
    if carry is not None:
        single = not isinstance(out_shape, (list, tuple))
        main_out = [out_shape] if single else list(out_shape)
        main_specs = [out_specs] if single else list(out_specs)
        n_in, n_out, n_scr = len(in_specs), len(main_out), len(scratch)
        x_in, x_out = len(carry.ins), len(carry.out_shapes)
        inner, steps = body, tuple(grid)

        def body(*refs):
            ins, xin = refs[:n_in], refs[n_in:n_in + x_in]
            outs = refs[n_in + x_in:n_in + x_in + n_out]
            xout = refs[n_in + x_in + n_out:n_in + x_in + n_out + x_out]
            scr = refs[n_in + x_in + n_out + x_out:n_in + x_in + n_out + x_out + n_scr]
            xsem = refs[n_in + x_in + n_out + x_out + n_scr:]
            ids = [pl.program_id(a) for a in range(len(steps))]
            first, last = ids[0] == 0, ids[0] == steps[0] - 1
            for a in range(1, len(steps)):
                first = jnp.logical_and(first, ids[a] == 0)
                last = jnp.logical_and(last, ids[a] == steps[a] - 1)
            pl.when(first)(lambda: carry.start(xin, xout, xsem))
            inner(*ins, *outs, *scr)
            pl.when(last)(lambda: carry.wait(xin, xout, xsem))

        hbm = pl.BlockSpec(memory_space=pltpu.HBM)
        aliases = dict(aliases or {})
        aliases.update({n_in + k: n_out + v for k, v in carry.aliases.items()})
        res = _call(body, name, main_out + carry.out_shapes, grid=grid, in_specs=list(in_specs) + [hbm] * x_in,
                    out_specs=main_specs + [hbm] * x_out, scratch=list(scratch) + carry.sems, dims=dims, aliases=aliases)

        def run(*args):
            outs = res(*args, *carry.ins)
            carry.outs = list(outs[n_out:])
            return outs[0] if single else outs[:n_out]

        return run
    if aliases:
        kw["input_output_aliases"] = aliases
    if grid_spec is not None:
        kw["grid_spec"] = grid_spec
    else:
        if grid is not None:
            kw["grid"] = grid
        kw["in_specs"] = in_specs
        kw["out_specs"] = out_specs
        kw["scratch_shapes"] = list(scratch)
    return pl.pallas_call(body, name=name, out_shape=out_shape, compiler_params=pltpu.CompilerParams(**params), **kw)


def _pick(n, target, mult):
    best = None
    for d in range(mult, min(n, target) + 1, mult):
        if n % d == 0:
            best = d
    return n if best is None else best


_DOT_DIMS = {"nn": (((1,), (0,)), ((), ())), "nt": (((1,), (1,)), ((), ())), "tn": (((0,), (0,)), ((), ()))}


def _mm(a, b, mode, out_dtype, name, res=None, carry=None):
    if mode == "tn":
        a, mode = a.T, "nn"
    if mode == "nn":
        (M, K), N = a.shape, b.shape[1]
    else:
        (M, K), N = a.shape, b.shape[0]
    tm = _pick(M, 1024, 8)
    tn = _pick(N, 1536, 128)
    tk = _pick(K, 1536, 128)
    nk = K // tk
    has_res = res is not None

    def body(*refs):
        if has_res:
            a_ref, b_ref, r_ref, o_ref, acc = refs
        else:
            a_ref, b_ref, o_ref, acc = refs
        k = pl.program_id(2)
        part = lax.dot_general(a_ref[...].astype(BF16), b_ref[...].astype(BF16), _DOT_DIMS[mode],
                               preferred_element_type=F32)
        if nk == 1:
            o_ref[...] = (part + r_ref[...] if has_res else part).astype(o_ref.dtype)
            return

        @pl.when(k == 0)
        def _():
            acc[...] = part

        @pl.when(k > 0)
        def _():
            acc[...] += part

        @pl.when(k == nk - 1)
        def _():
            r = acc[...]
            if has_res:
                r = r + r_ref[...]
            o_ref[...] = r.astype(o_ref.dtype)

    a_spec = pl.BlockSpec((tm, tk), lambda i, j, k: (i, k))
    b_spec = {"nn": pl.BlockSpec((tk, tn), lambda i, j, k: (k, j)),
              "nt": pl.BlockSpec((tn, tk), lambda i, j, k: (j, k))}[mode]
    o_spec = pl.BlockSpec((tm, tn), lambda i, j, k: (i, j))
    in_specs = [a_spec, b_spec] + ([o_spec] if has_res else [])
    args = (a, b) + ((res,) if has_res else ())
    return _call(body, name, jax.ShapeDtypeStruct((M, N), out_dtype), grid=(M // tm, N // tn, nk),
                 in_specs=in_specs, out_specs=o_spec, scratch=[pltpu.VMEM((tm, tn), F32)],
                 dims=("parallel", "parallel", "arbitrary"), carry=carry)(*args)


def _views(rows):
    return [r if isinstance(r, tuple) else (r, r.shape[1], 0) for r in rows]


def _rowwise(f, rows, params, outs, name, tb=256, carry=None, into=None):
    rows = _views(rows)
    T = rows[0][0].shape[0]
    tb = min(tb, T)
    nr, npar = len(rows), len(params)
    outs = [o if len(o) == 3 else (o[0], o[1], o[0]) for o in outs]
    into = into or []

    def body(*refs):
        vals = f(*[r[...].astype(F32) for r in refs[:nr]], *[p[...] for p in refs[nr:nr + npar]])
        for o_ref, v in zip(refs[nr + npar + len(into):], vals):
            o_ref[...] = v.astype(o_ref.dtype)

    row_spec = lambda w, cb=0: pl.BlockSpec((tb, w), lambda i: (i, cb))
    par_spec = lambda w: pl.BlockSpec((1, w), lambda i: (0, 0))
    out_shape = [jax.ShapeDtypeStruct((T, tw), dt) for _, dt, tw in outs]
    out_specs = [row_spec(w) for w, _, _ in outs]
    in_specs = [row_spec(w, cb) for _, w, cb in rows] + [par_spec(p.shape[1]) for p in params]
    args = [r[0] for r in rows] + list(params)
    aliases = {}
    for k, arr, cb in into:
        aliases[len(args)] = k
        in_specs.append(pl.BlockSpec(memory_space=pl.ANY))
        args.append(arr)
        out_shape[k] = jax.ShapeDtypeStruct(arr.shape, arr.dtype)
        out_specs[k] = row_spec(outs[k][0], cb)
    return _call(body, name, out_shape, grid=(T // tb,), in_specs=in_specs, out_specs=out_specs, dims=("parallel",),
                 carry=carry, aliases=aliases)(*args)


def _rowwise_bwd(f, rows, params, douts, n_diff, name, tb=256, carry=None, add=None, into=None):
    rows, douts = _views(rows), _views(douts)
    T = rows[0][0].shape[0]
    tb = min(tb, T)
    nr, npar, nd = len(rows), len(params), len(douts)
    n_add = 0 if add is None else 1

    def body(*refs):
        rv = [r[...].astype(F32) for r in refs[:nr]]
        pv = [p[...] for p in refs[nr:nr + npar]]
        dv = [d[...].astype(F32) for d in refs[nr + npar:nr + npar + nd]]
        o_refs = refs[nr + npar + nd + n_add + (0 if into is None else 1):]
        fixed = rv[n_diff:]

        def g(*xs):
            return tuple(f(*xs[:n_diff], *fixed, *xs[n_diff:]))

        _, vjp = jax.vjp(g, *rv[:n_diff], *pv)
        grads = list(vjp(tuple(dv)))
        if add is not None:
            grads[0] = grads[0] + refs[nr + npar + nd][...]
        for o_ref, gr in zip(o_refs[:n_diff], grads[:n_diff]):
            o_ref[...] = gr.astype(o_ref.dtype)
        first = pl.program_id(0) == 0
        for o_ref, gr in zip(o_refs[n_diff:], grads[n_diff:]):
            @pl.when(first)
            def _(o_ref=o_ref):
                o_ref[...] = jnp.zeros_like(o_ref)

            o_ref[...] += gr

    row_spec = lambda w, cb=0: pl.BlockSpec((tb, w), lambda i: (i, cb))
    par_spec = lambda w: pl.BlockSpec((1, w), lambda i: (0, 0))
    out_shape = ([jax.ShapeDtypeStruct((T, w), F32) for _, w, _ in rows[:n_diff]]
                 + [jax.ShapeDtypeStruct((1, p.shape[1]), F32) for p in params])
    out_specs = [row_spec(w) for _, w, _ in rows[:n_diff]] + [par_spec(p.shape[1]) for p in params]
    in_specs = ([row_spec(w, cb) for _, w, cb in rows] + [par_spec(p.shape[1]) for p in params]
                + [row_spec(w, cb) for _, w, cb in douts])
    args = [r[0] for r in rows] + list(params) + [d[0] for d in douts]
    aliases = None
    if add is not None:
        in_specs.append(row_spec(add.shape[1]))
        args.append(add)
    if into is not None:
        aliases = {len(args): 0}
        in_specs.append(pl.BlockSpec(memory_space=pl.ANY))
        args.append(into[0])
        out_shape[0] = jax.ShapeDtypeStruct(into[0].shape, into[0].dtype)
        out_specs[0] = row_spec(rows[0][1], into[1])
    return _call(body, name, out_shape, grid=(T // tb,), in_specs=in_specs, out_specs=out_specs,
                 dims=("arbitrary",), carry=carry, aliases=aliases)(*args)


def _sig(x):
    return 1.0 / (1.0 + jnp.exp(-x))


def _rms(x, g):
    return x * lax.rsqrt(jnp.mean(x * x, axis=-1, keepdims=True) + RMS_EPS) * g


def _f_rms(x, g):
    return (_rms(x, g),)


def _f_glu(a, gate):
    return (a * _sig(gate),)


def _f_ln_silu(x, g, b):
    mu = jnp.mean(x, axis=-1, keepdims=True)
    xc = x - mu
    var = jnp.mean(xc * xc, axis=-1, keepdims=True)
    y = xc * lax.rsqrt(var + LN_EPS) * g + b
    return (y * _sig(y),)


def _rope128(x, cos_p, sin_p):
    return x * cos_p + pltpu.roll(x, 64, 1) * sin_p


def _rope128_t(d, cos_p, sin_p):
    return d * cos_p + pltpu.roll(d * sin_p, 64, 1)


def _f_rope(xq, xk, cos_p, sin_p):
    heads = [_rope128(xq[:, h * 128:(h + 1) * 128], cos_p, sin_p) for h in range(MLA_HEADS)]
    return (jnp.concatenate(heads, axis=1), _rope128(xk, cos_p, sin_p))


def _f_rope_t(dq, dk_heads, cos_p, sin_p):
    heads = [_rope128_t(dq[:, h * 128:(h + 1) * 128], cos_p, sin_p) for h in range(MLA_HEADS)]
    dk = dk_heads[:, 0:128]
    for h in range(1, MLA_HEADS):
        dk = dk + dk_heads[:, h * 128:(h + 1) * 128]
    return (jnp.concatenate(heads, axis=1), _rope128_t(dk, cos_p, sin_p))


GATE_LANES = 256


def _gate_fwd(ycat, proj, z_col, name, tb=1024):
    T, width = ycat.shape
    zb = z_col // GATE_LANES

    def body(y_ref, z_ref, o_ref, ot_ref):
        z = z_ref[...]
        y = y_ref[...] * (z * _sig(z))
        o_ref[...] = y.astype(o_ref.dtype)
        ot_ref[...] = y.T.astype(ot_ref.dtype)

    blk = pl.BlockSpec((tb, GATE_LANES), lambda i, c: (i, c))
    return _call(body, name, [jax.ShapeDtypeStruct((T, width), BF16), jax.ShapeDtypeStruct((width, T), BF16)],
                 grid=(T // tb, width // GATE_LANES),
                 in_specs=[blk, pl.BlockSpec((tb, GATE_LANES), lambda i, c: (i, zb + c))],
                 out_specs=[blk, pl.BlockSpec((GATE_LANES, tb), lambda i, c: (c, i))],
                 dims=("parallel", "parallel"))(ycat, proj)


def _gate_bwd(ycat, proj, z_col, dy, name, tb=1024, carry=None):
    T, width = ycat.shape
    zb = z_col // GATE_LANES

    def body(y_ref, z_ref, dy_ref, dycat_ref, dz_ref):
        z, d = z_ref[...], dy_ref[...]
        s = _sig(z)
        dycat_ref[...] = d * (z * s)
        dz_ref[...] = (d * y_ref[...] * (s * (1.0 + z * (1.0 - s)))).astype(dz_ref.dtype)

    blk = pl.BlockSpec((tb, GATE_LANES), lambda i, c: (i, c))
    zblk = pl.BlockSpec((tb, GATE_LANES), lambda i, c: (i, zb + c))
    return _call(body, name, [jax.ShapeDtypeStruct((T, width), F32), jax.ShapeDtypeStruct(proj.shape, BF16)],
                 grid=(T // tb, width // GATE_LANES), in_specs=[blk, zblk, blk], out_specs=[blk, zblk],
                 dims=("parallel", "parallel"), carry=carry)(ycat, proj, dy)


def _glu_bwd(proj, d_glu, d_proj, name, tb=256):
    T, w = d_glu.shape

    def body(a_ref, g_ref, d_ref, _, o_ref):
        s, d = _sig(g_ref[...]), d_ref[...]
        o_ref[:, 0:w] = (d * s).astype(o_ref.dtype)
        o_ref[:, w:2 * w] = (d * a_ref[...] * (s * (1.0 - s))).astype(o_ref.dtype)

    return _call(body, name, jax.ShapeDtypeStruct(d_proj.shape, d_proj.dtype), grid=(T // tb,),
                 in_specs=[pl.BlockSpec((tb, w), lambda i: (i, 0)), pl.BlockSpec((tb, w), lambda i: (i, 1)),
                           pl.BlockSpec((tb, w), lambda i: (i, 0)), pl.BlockSpec(memory_space=pl.ANY)],
                 out_specs=pl.BlockSpec((tb, 2 * w), lambda i: (i, 0)), dims=("parallel",),
                 aliases={3: 0})(proj, proj, d_glu, d_proj)


def _final_loss(h, tgt, g, name, tb=256):
    T, D = h.shape

    def body(h_ref, t_ref, g_ref, dh_ref, dg_ref, loss_ref):
        tv = t_ref[...]

        def rowloss(hh, gg):
            e = _rms(hh, gg) - tv
            return 0.5 * jnp.mean(e * e, axis=-1, keepdims=True)

        lr, vjp = jax.vjp(rowloss, h_ref[...], g_ref[...])
        dh, dg = vjp(jnp.ones_like(lr))
        dh_ref[...] = dh

        @pl.when(pl.program_id(0) == 0)
        def _():
            dg_ref[...] = jnp.zeros_like(dg_ref)
            loss_ref[...] = jnp.zeros_like(loss_ref)

        dg_ref[...] += dg
        loss_ref[...] += jnp.broadcast_to(jnp.sum(lr, axis=0, keepdims=True), loss_ref.shape)

    row = pl.BlockSpec((tb, D), lambda i: (i, 0))
    par = pl.BlockSpec((1, D), lambda i: (0, 0))
    return _call(body, name,
                 [jax.ShapeDtypeStruct((T, D), F32), jax.ShapeDtypeStruct((1, D), F32), jax.ShapeDtypeStruct((1, 128), F32)],
                 grid=(T // tb,), in_specs=[row, row, par],
                 out_specs=[row, par, pl.BlockSpec((1, 128), lambda i: (0, 0))], dims=("arbitrary",))(h, tgt, g)


CONV_ROWS = 128
CONV_LANES = 256


def _sublane_phases(pad, n):
    for r in range(1, 8):
        for c0 in range(0, n - 8, 256):
            rows = min(256, n - 8 - c0)
            pad[r, c0:c0 + rows, :] = pad[0, c0 + r:c0 + r + rows, :]


def _dwconv_fwd(x, w, b, name, carry=None):
    B, S, C = x.shape
    cb = CONV_LANES
    off = CONV_PAD - (CONV_KERNEL - 1)

    def body(x_ref, w_ref, b_ref, o_ref, pad):
        pad[0, 0:CONV_PAD, :] = jnp.zeros((CONV_PAD, cb), F32)
        pad[0, CONV_PAD:, :] = x_ref[...]
        _sublane_phases(pad, S + CONV_PAD)
        for t0 in range(0, S, CONV_ROWS):
            acc = jnp.broadcast_to(b_ref[...], (CONV_ROWS, cb))
            for k in range(CONV_KERNEL):
                r, base = (off + k) % 8, t0 + (off + k) // 8 * 8
                acc = acc + w_ref[k:k + 1, :] * pad[r, base:base + CONV_ROWS, :]
            o_ref[t0:t0 + CONV_ROWS, :] = acc

    return _call(body, name, jax.ShapeDtypeStruct((B, S, C), F32), grid=(B, C // cb),
                 in_specs=[pl.BlockSpec((None, S, cb), lambda i, j: (i, 0, j)),
                           pl.BlockSpec((CONV_KERNEL, cb), lambda i, j: (0, j)),
                           pl.BlockSpec((1, cb), lambda i, j: (0, j))],
                 out_specs=pl.BlockSpec((None, S, cb), lambda i, j: (i, 0, j)),
                 scratch=[pltpu.VMEM((8, S + CONV_PAD, cb), F32)], dims=("parallel", "parallel"), carry=carry)(x, w, b)


def _dwconv_bwd(x, w, dy, name, carry=None):
    B, S, C = x.shape
    cb = CONV_LANES
    off = CONV_PAD - (CONV_KERNEL - 1)
    groups = CONV_ROWS // 8

    def body(x_ref, w_ref, dy_ref, dx_ref, dw_ref, db_ref, dypad, wacc):
        dypad[0, 0:S, :] = dy_ref[...]
        dypad[0, S:, :] = jnp.zeros((CONV_PAD, cb), F32)
        _sublane_phases(dypad, S + CONV_PAD)
        wacc[...] = jnp.zeros_like(wacc)
        for t0 in range(0, S, CONV_ROWS):
            xc = x_ref[t0:t0 + CONV_ROWS, :]
            acc = jnp.zeros((CONV_ROWS, cb), F32)
            for k in range(CONV_KERNEL):
                o = (CONV_KERNEL - 1) - k
                dys = dypad[o % 8, t0 + o // 8 * 8:t0 + o // 8 * 8 + CONV_ROWS, :]
                acc = acc + w_ref[k:k + 1, :] * dys
                wacc[k] += jnp.sum((dys * xc).reshape(groups, 8, cb), axis=0)
            wacc[CONV_KERNEL] += jnp.sum(dy_ref[t0:t0 + CONV_ROWS, :].reshape(groups, 8, cb), axis=0)
            dx_ref[t0:t0 + CONV_ROWS, :] = acc

        @pl.when(pl.program_id(1) == 0)
        def _():
            dw_ref[...] = jnp.zeros_like(dw_ref)
            db_ref[...] = jnp.zeros_like(db_ref)

        for k in range(CONV_KERNEL):
            dw_ref[k:k + 1, :] += jnp.sum(wacc[k], axis=0, keepdims=True)
        db_ref[...] += jnp.sum(wacc[CONV_KERNEL], axis=0, keepdims=True)

    blk = pl.BlockSpec((None, S, cb), lambda j, i: (i, 0, j))
    return _call(body, name,
                 [jax.ShapeDtypeStruct((B, S, C), F32), jax.ShapeDtypeStruct((CONV_KERNEL, C), F32),
                  jax.ShapeDtypeStruct((1, C), F32)],
                 grid=(C // cb, B),
                 in_specs=[blk, pl.BlockSpec((CONV_KERNEL, cb), lambda j, i: (0, j)), blk],
                 out_specs=[blk, pl.BlockSpec((CONV_KERNEL, cb), lambda j, i: (0, j)),
                            pl.BlockSpec((1, cb), lambda j, i: (0, j))],
                 scratch=[pltpu.VMEM((8, S + CONV_PAD, cb), F32), pltpu.VMEM((CONV_KERNEL + 1, 8, cb), F32)],
                 dims=("parallel", "arbitrary"), carry=carry)(x, w, dy)


ATTN_TILE = {"fwd": 1024, "bwd": 1024, "cross fwd": 512}
ATTN_SUB = {"fwd": 256, "bwd": 512}


def _attn_shapes(Sq, Sk, causal, pass_):
    tq = min(Sq, ATTN_TILE[pass_ if causal or pass_ == "bwd" else "cross fwd"])
    tk = tq if causal else min(Sk, ATTN_TILE[pass_])
    return tq, tk, min(ATTN_SUB[pass_], tq)


def _causal_bias(n):
    r = lax.broadcasted_iota(jnp.int32, (n, n), 0)
    c = lax.broadcasted_iota(jnp.int32, (n, n), 1)
    return jnp.where(c <= r, 0.0, NEG).astype(F32)


def _mask_diagonal(s, bias):
    n, nc = s.shape
    if nc == n:
        return s + bias
    return jnp.concatenate([s[:, :nc - n], s[:, nc - n:] + bias], axis=1)


def _attn_fwd(q, q_c0, qr, k, k_c0, kr, v, v_c0, B, Sq, Sk, H, causal, scale, name, into=None, o_c0=0, o_width=None,
              kv_stride=1):
    tq, tk, sub = _attn_shapes(Sq, Sk, causal, "fwd")
    nq, nk, nsub = Sq // tq, Sk // tk, tq // sub
    rope = qr is not None

    def body(*refs):
        refs = list(refs)
        qn_ref = refs.pop(0)
        qr_ref = refs.pop(0) if rope else None
        kn_ref = refs.pop(0)
        kr_ref = refs.pop(0) if rope else None
        v_ref = refs.pop(0)
        if into is not None:
            refs.pop(0)
        o_ref, lse_ref, m_s, l_s, acc = refs
        qi = pl.program_id(2)
        m_s[...] = jnp.full_like(m_s, NEG)
        l_s[...] = jnp.zeros_like(l_s)
        acc[...] = jnp.zeros_like(acc)
        bias = _causal_bias(sub) if causal else None
        qs = []
        for r in range(nsub):
            qn = qn_ref[r * sub:(r + 1) * sub, :].astype(BF16)
            qs.append(jnp.concatenate([qn, qr_ref[r * sub:(r + 1) * sub, :]], axis=1) if rope else qn)

        def step(j, masked):
            ks = pl.ds(pl.multiple_of(j * tk, tk), tk)
            kk = jnp.concatenate([kn_ref[ks, :], kr_ref[ks, :]], axis=1) if rope else kn_ref[ks, :]
            vv = v_ref[ks, :]
            for r in range(nsub):
                rows = slice(r * sub, (r + 1) * sub)
                nc = (r + 1) * sub if masked else tk
                s = lax.dot_general(qs[r], kk[:nc], _DOT_DIMS["nt"], preferred_element_type=F32) * scale
                if masked:
                    s = _mask_diagonal(s, bias)
                m_old = m_s[rows, :]
                m_new = jnp.maximum(m_old, jnp.max(s, axis=-1, keepdims=True))
                p = jnp.exp(s - m_new)
                alpha = jnp.exp(m_old - m_new)
                l_s[rows, :] = alpha * l_s[rows, :] + jnp.sum(p, axis=-1, keepdims=True)
                acc[rows, :] = alpha * acc[rows, :] + jnp.dot(p.astype(BF16), vv[:nc], preferred_element_type=F32)
                m_s[rows, :] = m_new

        def unmasked(j, carry):
            step(j, False)
            return carry

        if causal:
            lax.fori_loop(0, qi, unmasked, 0)
            step(qi, True)
        else:
            lax.fori_loop(0, nk, unmasked, 0)
        o_ref[...] = (acc[...] / l_s[...]).astype(o_ref.dtype)
        lse_ref[...] = m_s[...] + jnp.log(l_s[...])

    qspec = lambda c0: pl.BlockSpec((tq, 128), lambda b, h, i: (b * nq + i, c0 + h))
    kspec = lambda c0: pl.BlockSpec((Sk, 128), lambda b, h, i: (b, c0 + kv_stride * h))
    in_specs, args = [qspec(q_c0)], [q]
    if rope:
        in_specs.append(qspec(0)); args.append(qr)
    in_specs.append(kspec(k_c0)); args.append(k)
    if rope:
        in_specs.append(pl.BlockSpec((Sk, 128), lambda b, h, i: (b, 0))); args.append(kr)
    in_specs.append(kspec(v_c0)); args.append(v)
    aliases = {}
    if into is not None:
        aliases = {len(args): 0}
        in_specs.append(pl.BlockSpec(memory_space=pl.ANY)); args.append(into)
        o_shape = jax.ShapeDtypeStruct(into.shape, into.dtype)
    else:
        o_shape = jax.ShapeDtypeStruct((B * Sq, o_width), F32)
    return _call(body, name, [o_shape, jax.ShapeDtypeStruct((B * H, Sq, 1), F32)], grid=(B, H, nq), in_specs=in_specs,
                 out_specs=[qspec(o_c0), pl.BlockSpec((None, tq, 1), lambda b, h, i: (b * H + h, i, 0))],
                 scratch=[pltpu.VMEM((tq, 1), F32), pltpu.VMEM((tq, 1), F32), pltpu.VMEM((tq, 128), F32)],
                 dims=("parallel", "parallel", "arbitrary"), aliases=aliases)(*args)


def _attn_bwd(q, q_c0, qr, k, k_c0, kr, v, v_c0, o, do, o_c0, lse, B, Sq, Sk, H, causal, scale, name, dq_into=None,
              kv_stride=1):
    tq, tk, sub = _attn_shapes(Sq, Sk, causal, "bwd")
    nq, nk, nsub = Sq // tq, Sk // tk, tq // sub
    rope = qr is not None
    dk_w = 256 if rope else 128

    def body(*refs):
        refs = list(refs)
        qn_ref = refs.pop(0)
        qr_ref = refs.pop(0) if rope else None
        kn_ref = refs.pop(0)
        kr_ref = refs.pop(0) if rope else None
        v_ref, o_ref, do_ref, lse_ref = refs[:4]
        refs = refs[4 + (0 if dq_into is None else 1):]
        dqn_ref = refs.pop(0)
        dqr_ref = refs.pop(0) if rope else None
        dkn_ref = refs.pop(0)
        dkr_ref = refs.pop(0) if rope else None
        dv_ref = None if rope else refs.pop(0)
        q_s, do_s, dl_s, dq_acc, dk_acc, dv_acc = refs
        kj = pl.program_id(2)

        @pl.when(kj == 0)
        def _():
            qn = qn_ref[...].astype(BF16)
            q_s[...] = jnp.concatenate([qn, qr_ref[...]], axis=1) if rope else qn
            dof = do_ref[...]
            do_s[...] = dof.astype(BF16)
            dl_s[...] = jnp.sum(dof * o_ref[...], axis=-1, keepdims=True)
            dq_acc[...] = jnp.zeros_like(dq_acc)

        kk = jnp.concatenate([kn_ref[...], kr_ref[...]], axis=1) if rope else kn_ref[...]
        vv = v_ref[...]
        bias = _causal_bias(sub) if causal else None
        dk_acc[...] = jnp.zeros_like(dk_acc)
        dv_acc[...] = jnp.zeros_like(dv_acc)

        def step(i, masked):
            for r in range(nsub):
                rows = pl.ds(pl.multiple_of(i * tq + r * sub, sub), sub)
                qq, dob = q_s[rows, :], do_s[rows, :]
                nc = (r + 1) * sub if masked else tk
                kc, vc = kk[:nc], vv[:nc]
                s = lax.dot_general(qq, kc, _DOT_DIMS["nt"], preferred_element_type=F32) * scale
                if masked:
                    s = _mask_diagonal(s, bias)
                p = jnp.exp(s - lse_ref[rows, :])
                dp = lax.dot_general(dob, vc, _DOT_DIMS["nt"], preferred_element_type=F32)
                ds = (p * (dp - dl_s[rows, :]) * scale).astype(BF16)
                dv_acc[0:nc, :] += lax.dot_general(p.astype(BF16), dob, _DOT_DIMS["tn"], preferred_element_type=F32)
                dk_acc[0:nc, :] += lax.dot_general(ds, qq, _DOT_DIMS["tn"], preferred_element_type=F32)
                dq_acc[rows, :] += jnp.dot(ds, kc, preferred_element_type=F32)

        def unmasked(i, carry):
            step(i, False)
            return carry

        if causal:
            step(kj, True)
            lax.fori_loop(kj + 1, nq, unmasked, 0)
        else:
            lax.fori_loop(0, nq, unmasked, 0)
        if rope:
            dkn_ref[...] = jnp.concatenate([dk_acc[:, 0:128], dv_acc[...]], axis=1).astype(dkn_ref.dtype)
            dkr_ref[...] = dk_acc[:, 128:256]
        else:
            dkn_ref[...] = dk_acc[...]
            dv_ref[...] = dv_acc[...]

        @pl.when(kj == nk - 1)
        def _():
            dqn_ref[...] = dq_acc[:, 0:128].astype(dqn_ref.dtype)
            if rope:
                dqr_ref[...] = dq_acc[:, 128:256]

    qspec = lambda c0: pl.BlockSpec((Sq, 128), lambda b, h, j: (b, c0 + h))
    kspec = lambda c0: pl.BlockSpec((tk, 128), lambda b, h, j: (b * nk + j, c0 + kv_stride * h))
    in_specs, args = [qspec(q_c0)], [q]
    if rope:
        in_specs.append(qspec(0)); args.append(qr)
    in_specs.append(kspec(k_c0)); args.append(k)
    if rope:
        in_specs.append(pl.BlockSpec((tk, 128), lambda b, h, j: (b * nk + j, 0))); args.append(kr)
    in_specs += [kspec(v_c0), qspec(o_c0), qspec(o_c0), pl.BlockSpec((None, Sq, 1), lambda b, h, j: (b * H + h, 0, 0))]
    args += [v, o, do, lse]
    h_rows_q = jax.ShapeDtypeStruct((B * Sq, H * 128), F32)
    h_rows_k = jax.ShapeDtypeStruct((B * Sk, H * 128), F32)
    out_shape, out_specs, aliases = [h_rows_q], [qspec(0)], None
    if rope:
        out_shape = [jax.ShapeDtypeStruct((B * Sq, 2 * H * 128), BF16)]
    if dq_into is not None:
        aliases = {len(args): 0}
        in_specs.append(pl.BlockSpec(memory_space=pl.ANY)); args.append(dq_into[0])
        out_shape, out_specs = [jax.ShapeDtypeStruct(dq_into[0].shape, dq_into[0].dtype)], [qspec(dq_into[1])]
    if rope:
        out_shape.append(h_rows_q); out_specs.append(qspec(0))
    hspec = lambda w: pl.BlockSpec((tk, w), lambda b, h, j: (b * nk + j, h))
    if rope:
        out_shape += [jax.ShapeDtypeStruct((B * Sk, H * 256), BF16), h_rows_k]
        out_specs += [hspec(256), hspec(128)]
    else:
        out_shape += [h_rows_k, h_rows_k]
        out_specs += [hspec(128), hspec(128)]
    return _call(body, name, out_shape, grid=(B, H, nk), in_specs=in_specs, out_specs=out_specs,
                 scratch=[pltpu.VMEM((Sq, dk_w), BF16), pltpu.VMEM((Sq, 128), BF16), pltpu.VMEM((Sq, 1), F32),
                          pltpu.VMEM((Sq, dk_w), F32), pltpu.VMEM((tk, dk_w), F32), pltpu.VMEM((tk, 128), F32)],
                 dims=("parallel", "parallel", "arbitrary"), aliases=aliases)(*args)


def _mem_attention_fwd(proj, q_col, ycat, mem2, mem_g, w_mem, B, S, tag):
    M = mem2.shape[0] // B
    (memn,) = _rowwise(_f_rms, [mem2], [mem_g], [(mem2.shape[1], BF16)], tag + "_memnorm")
    kvm = _mm(memn, w_mem, "nn", BF16, tag + "_memkv")
    o_c0 = ycat.shape[1] // 128 - MEM_HEADS
    ycat, lse = _attn_fwd(proj, q_col // 128, None, kvm, 0, None, kvm, MEM_HEADS, B, S, M, MEM_HEADS, False,
                          MEM_HEAD_DIM ** -0.5, tag + "_memattn", into=ycat, o_c0=o_c0)
    return ycat, (memn, kvm, lse)


def _mem_attention_bwd(proj, q_col, ycat, d_ycat, d_proj, saved, mem2, mem_g, w_mem, B, S, tag):
    memn, kvm, lse = saved
    M = mem2.shape[0] // B
    o_c0 = ycat.shape[1] // 128 - MEM_HEADS
    d_q, d_k, d_v = _attn_bwd(proj, q_col // 128, None, kvm, 0, None, kvm, MEM_HEADS, ycat, d_ycat, o_c0, lse, B, S, M,
                              MEM_HEADS, False, MEM_HEAD_DIM ** -0.5, tag + "_memattn_bwd", dq_into=(d_proj, q_col // 128))
    d_kvm = jnp.concatenate([d_k, d_v], axis=1).astype(BF16)
    d_w_mem = _mm(memn, d_kvm, "tn", F32, tag + "_memkv_dw")
    d_memn = _mm(d_kvm, w_mem, "nt", F32, tag + "_memkv_dx")
    _, d_mem_g = _rowwise_bwd(_f_rms, [mem2], [mem_g], [d_memn], 1, tag + "_memnorm_bwd")
    return d_q, d_w_mem, d_mem_g


def _rope_tables(positions):
    inv_freq = 1.0 / (ROPE_THETA ** (jnp.arange(0, MLA_ROPE, 2, dtype=F32) / MLA_ROPE))
    ang = positions.astype(F32).reshape(-1, 1) * inv_freq
    cos, sin, zero = jnp.cos(ang), jnp.sin(ang), jnp.zeros_like(ang)
    return jnp.concatenate([cos, zero, cos, zero], axis=1), jnp.concatenate([-sin, zero, sin, zero], axis=1)


def _forward_backward(x, mem, positions, target, W):
    B, S, D = x.shape
    T = B * S
    conv_w = W["conv_dw"].shape[1]
    mix_w = 2 * D
    h0 = x.reshape(T, D)
    mem2 = mem.reshape(-1, D)
    tgt = target.reshape(T, D)
    row = lambda v: v.reshape(1, -1)
    n_nope = MLA_HEADS * MLA_NOPE

    g0 = row(W["norm_g"][0])
    (u0,) = _rowwise(_f_rms, [h0], [g0], [(D, BF16)], "l0_norm", carry=W.carry("l0_norm"))
    proj0 = _mm(u0, W["conv_w_in"], "nn", F32, "l0_in", carry=W.carry("l0_in"))
    a0, gate0 = (proj0, conv_w, 0), (proj0, conv_w, 1)
    qm0_col, z0_col = 2 * conv_w, 2 * conv_w + MEM_WIDTH
    (glu,) = _rowwise(_f_glu, [a0, gate0], [], [(conv_w, F32)], "l0_glu", carry=W.carry("l0_glu"))
    dw, dwb = W["conv_dw"], row(W["conv_dw_b"][0])
    cv = _dwconv_fwd(glu.reshape(B, S, conv_w), dw, dwb, "l0_dwconv", carry=W.carry("l0_dwconv")).reshape(T, conv_w)
    ln_g, ln_b = row(W["conv_ln_g"][0]), row(W["conv_ln_b"][0])
    (ycat0,) = _rowwise(_f_ln_silu, [cv], [ln_g, ln_b], [(conv_w, F32, mix_w)], "l0_ln", carry=W.carry("l0_ln"))
    mg0 = row(W["mem_norm_g"][0])
    ycat0, mem_saved0 = _mem_attention_fwd(proj0, qm0_col, ycat0, mem2, mg0, W["w_mem_kv"][0], B, S, "l0")
    y0, y0_t = _gate_fwd(ycat0, proj0, z0_col, "l0_gate")
    h1 = _mm(y0, W["w_out"][0], "nn", F32, "l0_out", res=h0)

    g1 = row(W["norm_g"][1])
    (u1,) = _rowwise(_f_rms, [h1], [g1], [(D, BF16)], "l1_norm")
    proj1 = _mm(u1, W["mla_w_in"], "nn", F32, "l1_in")
    cq, ckv = (proj1, Q_RANK, 0), (proj1, KV_RANK, Q_RANK // KV_RANK)
    qm1_col = Q_RANK + KV_RANK
    z1_col = qm1_col + MEM_WIDTH
    kr_col = z1_col + mix_w
    qg, kvg = row(W["mla_q_norm_g"]), row(W["mla_kv_norm_g"])
    (cqn,) = _rowwise(_f_rms, [cq], [qg], [(Q_RANK, BF16)], "l1_qnorm")
    (ckvn,) = _rowwise(_f_rms, [ckv], [kvg], [(KV_RANK, BF16)], "l1_kvnorm")
    qf = _mm(cqn, W["mla_w_uq"], "nn", F32, "l1_uq")
    kvf = _mm(ckvn, W["mla_w_ukv"], "nn", BF16, "l1_ukv")
    cos_p, sin_p = _rope_tables(positions)
    qr, kr = _rowwise(_f_rope, [(qf, n_nope, 1), (proj1, 128, kr_col // 128), cos_p, sin_p], [],
                      [(n_nope, BF16), (128, BF16)], "l1_rope")
    scale1 = MLA_QK ** -0.5
    ycat1, lse1 = _attn_fwd(qf, 0, qr, kvf, 0, kr, kvf, 1, B, S, S, MLA_HEADS, True, scale1, "l1_attn",
                            o_width=mix_w, kv_stride=2)
    mg1 = row(W["mem_norm_g"][1])
    ycat1, mem_saved1 = _mem_attention_fwd(proj1, qm1_col, ycat1, mem2, mg1, W["w_mem_kv"][1], B, S, "l1")
    y1, y1_t = _gate_fwd(ycat1, proj1, z1_col, "l1_gate")
    h2 = _mm(y1, W["w_out"][1], "nn", F32, "l1_out", res=h1)

    gf = row(W["final_norm_g"])
    dh2, d_gf, loss128 = _final_loss(h2, tgt, gf, "final_loss")
    G = {"final_norm_g": d_gf.reshape(-1)}
    L1 = {}

    dy1 = _mm(dh2, W["w_out"][1], "nt", F32, "l1_out_dx")
    d_wout1 = _mm(y1_t, dh2, "nn", F32, "l1_out_dw")
    d_ycat1, d_proj1 = _gate_bwd(ycat1, proj1, z1_col, dy1, "l1_gate_bwd")
    d_proj1, d_wmem1, d_mg1 = _mem_attention_bwd(proj1, qm1_col, ycat1, d_ycat1, d_proj1, mem_saved1, mem2, mg1,
                                                 W["w_mem_kv"][1], B, S, "l1")
    d_qf, d_qr, d_kvf, d_kr_heads = _attn_bwd(qf, 0, qr, kvf, 0, kr, kvf, 1, ycat1, d_ycat1, 0, lse1, B, S, S,
                                              MLA_HEADS, True, scale1, "l1_attn_bwd", kv_stride=2)
    d_qf, d_proj1 = _rowwise(_f_rope_t, [d_qr, d_kr_heads, cos_p, sin_p], [], [(n_nope, F32), (128, F32)], "l1_rope_bwd",
                             into=[(0, d_qf, 1), (1, d_proj1, kr_col // 128)])
    d_cqn = _mm(d_qf, W["mla_w_uq"], "nt", F32, "l1_uq_dx")
    L1[("mla_w_uq", None)] = _mm(cqn, d_qf, "tn", F32, "l1_uq_dw")
    d_ckvn = _mm(d_kvf, W["mla_w_ukv"], "nt", F32, "l1_ukv_dx")
    L1[("mla_w_ukv", None)] = _mm(ckvn, d_kvf, "tn", F32, "l1_ukv_dw")
    d_proj1, d_qg = _rowwise_bwd(_f_rms, [cq], [qg], [d_cqn], 1, "l1_qnorm_bwd", into=(d_proj1, cq[2]))
    d_proj1, d_kvg = _rowwise_bwd(_f_rms, [ckv], [kvg], [d_ckvn], 1, "l1_kvnorm_bwd", into=(d_proj1, ckv[2]))
    L1[("w_mem_kv", 1)] = d_wmem1
    L1[("mla_w_in", None)] = _mm(u1, d_proj1, "tn", F32, "l1_in_dw")
    L1[("w_out", 1)] = d_wout1
    W.ready("l1", L1)
    d_u1 = _mm(d_proj1, W["mla_w_in"], "nt", F32, "l1_in_dx", carry=W.carry("l1_in_dx"))
    dh1, d_g1 = _rowwise_bwd(_f_rms, [h1], [g1], [d_u1], 1, "l1_norm_bwd", add=dh2)

    dy0 = _mm(dh1, W["w_out"][0], "nt", F32, "l0_out_dx")
    d_wout0 = _mm(y0_t, dh1, "nn", F32, "l0_out_dw")
    d_ycat0, d_proj0 = _gate_bwd(ycat0, proj0, z0_col, dy0, "l0_gate_bwd", carry=W.carry("l0_gate_bwd"))
    d_proj0, d_wmem0, d_mg0 = _mem_attention_bwd(proj0, qm0_col, ycat0, d_ycat0, d_proj0, mem_saved0, mem2, mg0,
                                                 W["w_mem_kv"][0], B, S, "l0")
    W.ready("l0a", {("w_mem_kv", 0): d_wmem0, ("w_out", 0): d_wout0})
    d_cv, d_ln_g, d_ln_b = _rowwise_bwd(_f_ln_silu, [cv], [ln_g, ln_b], [(d_ycat0, conv_w, 0)], 1, "l0_ln_bwd",
                                        carry=W.carry("l0_ln_bwd"))
    d_glu, d_dw, d_dwb = _dwconv_bwd(glu.reshape(B, S, conv_w), dw, d_cv.reshape(B, S, conv_w), "l0_dwconv_bwd",
                                     carry=W.carry("l0_dwconv_bwd"))
    d_proj0 = _glu_bwd(proj0, d_glu.reshape(T, conv_w), d_proj0, "l0_glu_bwd")
    d_conv_w_in = _mm(u0, d_proj0, "tn", F32, "l0_in_dw", carry=W.carry("l0_in_dw"))
    W.ready("l0b", {("conv_w_in", None): d_conv_w_in, ("conv_dw", None): d_dw,
                    ("mla_q_norm_g", None): d_qg.reshape(-1), ("mla_kv_norm_g", None): d_kvg.reshape(-1)})
    d_u0 = _mm(d_proj0, W["conv_w_in"], "nt", F32, "l0_in_dx", carry=W.carry("l0_in_dx"))
    dx, d_g0 = _rowwise_bwd(_f_rms, [h0], [g0], [d_u0], 1, "l0_norm_bwd", add=dh1)
    dx = dx.reshape(B, S, D)

    G["norm_g"] = jnp.concatenate([d_g0, d_g1], axis=0)
    G["mem_norm_g"] = jnp.concatenate([d_mg0, d_mg1], axis=0)
    G["conv_dw_b"] = d_dwb
    G["conv_ln_g"], G["conv_ln_b"] = d_ln_g, d_ln_b
    return loss128[0, 0], dx, G


def _mla_in_perm(w):
    c2 = Q_RANK + KV_RANK
    zero = jnp.zeros((w.shape[0], HALF_ROPE), w.dtype)
    return jnp.concatenate([w[:, :c2], w[:, c2 + MLA_ROPE:], w[:, c2:c2 + HALF_ROPE], zero,
                            w[:, c2 + HALF_ROPE:c2 + MLA_ROPE], zero], axis=1)


def _mla_in_unperm(g):
    c2 = Q_RANK + KV_RANK
    r = g.shape[1] - 128
    return jnp.concatenate([g[:, :c2], g[:, r:r + HALF_ROPE], g[:, r + 64:r + 64 + HALF_ROPE], g[:, c2:r]], axis=1)


def _uq_perm(w):
    n = w.shape[0]
    w3 = w.reshape(n, MLA_HEADS, MLA_QK)
    zero = jnp.zeros((n, MLA_HEADS, HALF_ROPE), w.dtype)
    rope = jnp.concatenate([w3[:, :, MLA_NOPE:MLA_NOPE + HALF_ROPE], zero, w3[:, :, MLA_NOPE + HALF_ROPE:], zero], axis=2)
    return jnp.concatenate([w3[:, :, :MLA_NOPE].reshape(n, -1), rope.reshape(n, -1)], axis=1)


def _uq_unperm(g):
    n = g.shape[0]
    n_nope = MLA_HEADS * MLA_NOPE
    rope = g[:, n_nope:].reshape(n, MLA_HEADS, 128)
    return jnp.concatenate([g[:, :n_nope].reshape(n, MLA_HEADS, MLA_NOPE), rope[:, :, :HALF_ROPE],
                            rope[:, :, 64:64 + HALF_ROPE]], axis=2).reshape(n, -1)


_ROW_CUT = ("w_mem_kv", "w_out")
_COL_CUT = ("conv_w_in", "mla_w_in", "mla_w_uq", "mla_w_ukv", "conv_dw")
_BIG = ("w_mem_kv", "w_out", "conv_w_in", "mla_w_in", "mla_w_uq", "mla_w_ukv")
_SMALL_SHARDED = ("conv_dw", "mla_q_norm_g", "mla_kv_norm_g")
_REPLICATED = ("norm_g", "mem_norm_g", "conv_dw_b", "conv_ln_g", "conv_ln_b", "final_norm_g")
_PERM = {"mla_w_in": (_mla_in_perm, _mla_in_unperm), "mla_w_uq": (_uq_perm, _uq_unperm)}


def _join(n, blocks):
    if n in _ROW_CUT:
        _, L, r, c = blocks.shape
        return blocks.transpose(1, 0, 2, 3).reshape(L, N_DEV * r, c)
    if n in _COL_CUT:
        _, _, r, c = blocks.shape
        return blocks.reshape(N_DEV, r, c).transpose(1, 0, 2).reshape(r, N_DEV * c)
    return blocks.reshape(-1)


def _cut(n, full, shard_shape):
    if n in _ROW_CUT:
        L, r, c = shard_shape
        return full.reshape(L, N_DEV, r, c).transpose(1, 0, 2, 3)
    if n in _COL_CUT:
        _, r, c = shard_shape
        return full.reshape(r, N_DEV, c).transpose(1, 0, 2).reshape(N_DEV, 1, r, c)
    return full.reshape(N_DEV, 1, -1)


def _flat_pad(parts, size):
    flat = jnp.concatenate([p.reshape(-1) for p in parts])
    return jnp.concatenate([flat, jnp.zeros((size - flat.shape[0],), flat.dtype)])


SMALL_LANES = 128 * 8


def _as_tiles(flat_parts):
    total = sum(p.size for p in flat_parts)
    size = -(-total // SMALL_LANES) * SMALL_LANES
    return _flat_pad(flat_parts, size).reshape(8, size // 8)


def _split_flat(flat, like):
    out, o = [], 0
    for a in like:
        out.append(flat[o:o + a.size].reshape(a.shape))
        o += a.size
    return out


_HBM = pl.BlockSpec(memory_space=pltpu.HBM)
_VMEM = pl.BlockSpec(memory_space=pltpu.VMEM)


def _position():
    return lax.axis_index("x"), lax.axis_index("y"), lax.axis_index("c")


def _dma_sems(n):
    return [pltpu.SemaphoreType.DMA((n,)), pltpu.SemaphoreType.DMA((n,))]


def _run_stage(stage, name):
    n_in, n_out = len(stage.ins), len(stage.out_shapes)

    def body(*refs):
        ins, outs, sems = refs[:n_in], refs[n_in:n_in + n_out], refs[n_in + n_out:]
        stage.start(ins, outs, sems)
        stage.wait(ins, outs, sems)

    outs = _call(body, name, stage.out_shapes, in_specs=[_HBM] * n_in, out_specs=[_HBM] * n_out, scratch=stage.sems,
                 aliases=stage.aliases)(*stage.ins)
    stage.outs = list(outs)
    return stage.outs


def _gather_chips_stage(shards):
    n = len(shards)

    def copies(x_refs, out_refs, sems):
        send_sems, recv_sems, _ = sems
        x, y, c = _position()
        peers = [(x, y, 1 - c), (1 - x, y, c), (x, 1 - y, c), (1 - x, 1 - y, c)]
        out = []
        for a in range(n):
            for k, (px, py, pc) in enumerate(peers):
                send = pltpu.make_async_remote_copy(src_ref=x_refs[a], dst_ref=out_refs[a].at[4 * x + 2 * y + c],
                                                    send_sem=send_sems.at[4 * a + k], recv_sem=recv_sems.at[4 * a + k],
                                                    device_id=(px, py, pc), device_id_type=MESH)
                recv = pltpu.make_async_remote_copy(src_ref=x_refs[a], dst_ref=out_refs[a].at[4 * px + 2 * py + pc],
                                                    send_sem=send_sems.at[4 * a + k], recv_sem=recv_sems.at[4 * a + k],
                                                    device_id=(px, py, pc), device_id_type=MESH)
                out.append((send, recv))
        return out

    def local(x_refs, out_refs, sems):
        x, y, c = _position()
        return [pltpu.make_async_copy(x_refs[a], out_refs[a].at[4 * x + 2 * y + c], sems[2].at[a]) for a in range(n)]

    def start(x_refs, out_refs, sems):
        for cp in local(x_refs, out_refs, sems):
            cp.start()
        for send, _ in copies(x_refs, out_refs, sems):
            send.start()

    def wait(x_refs, out_refs, sems):
        for send, recv in copies(x_refs, out_refs, sems):
            recv.wait_recv()
            send.wait_send()
        for cp in local(x_refs, out_refs, sems):
            cp.wait()

    return _Stage(shards, [jax.ShapeDtypeStruct((N_DEV,) + a.shape, a.dtype) for a in shards],
                  _dma_sems(4 * n) + [pltpu.SemaphoreType.DMA((n,))], start, wait)


def _gather_sibling_stage(bufs):
    n = len(bufs)

    def copies(out_refs, sems):
        send_sems, recv_sems = sems
        x, y, c = _position()
        out = []
        for a in range(n):
            for j, (px, py) in enumerate([(1 - x, y), (x, 1 - y), (1 - x, 1 - y)]):
                mine, theirs = out_refs[a].at[4 * px + 2 * py + c], out_refs[a].at[4 * px + 2 * py + (1 - c)]
                send = pltpu.make_async_remote_copy(src_ref=mine, dst_ref=mine, send_sem=send_sems.at[3 * a + j],
                                                    recv_sem=recv_sems.at[3 * a + j], device_id=(x, y, 1 - c),
                                                    device_id_type=MESH)
                recv = pltpu.make_async_remote_copy(src_ref=mine, dst_ref=theirs, send_sem=send_sems.at[3 * a + j],
                                                    recv_sem=recv_sems.at[3 * a + j], device_id=(x, y, 1 - c),
                                                    device_id_type=MESH)
                out.append((send, recv))
        return out

    def start(_, out_refs, sems):
        for send, _r in copies(out_refs, sems):
            send.start()

    def wait(_, out_refs, sems):
        for send, recv in copies(out_refs, sems):
            recv.wait_recv()
            send.wait_send()

    return _Stage(bufs, [jax.ShapeDtypeStruct(b.shape, b.dtype) for b in bufs], _dma_sems(3 * n), start, wait,
                  aliases={a: a for a in range(n)})


def _all_gather_small(v, name):
    r, n = v.shape

    def body(x_ref, out_ref, send_sems, recv_sems, local_sem):
        x, y, c = _position()
        me = 4 * x + 2 * y + c
        mine = pltpu.make_async_copy(x_ref, out_ref.at[me], local_sem)
        mine.start()
        flips = [(fx, fy, fc) for fx in (0, 1) for fy in (0, 1) for fc in (0, 1)][1:]
        copies = []
        for k, (fx, fy, fc) in enumerate(flips):
            peer = (x ^ fx, y ^ fy, c ^ fc)
            cp = pltpu.make_async_remote_copy(src_ref=x_ref, dst_ref=out_ref.at[me], send_sem=send_sems.at[k],
                                              recv_sem=recv_sems.at[k], device_id=peer, device_id_type=MESH)
            cp.start()
            copies.append(cp)
        for k, (fx, fy, fc) in enumerate(flips):
            px, py, pc = x ^ fx, y ^ fy, c ^ fc
            src = out_ref.at[4 * px + 2 * py + pc]
            pltpu.make_async_remote_copy(src_ref=x_ref, dst_ref=src, send_sem=send_sems.at[k], recv_sem=recv_sems.at[k],
                                         device_id=(px, py, pc), device_id_type=MESH).wait_recv()
        for cp in copies:
            cp.wait_send()
        mine.wait()

    return _call(body, name, jax.ShapeDtypeStruct((N_DEV, r, n), v.dtype), in_specs=[_VMEM], out_specs=_VMEM,
                 scratch=_dma_sems(7) + [pltpu.SemaphoreType.DMA(())])(v)


def _reduce_sibling_stage(gs):
    n = len(gs)

    def copies(g_refs, out_refs, sems):
        send_sems, recv_sems = sems
        x, y, c = _position()
        return [pltpu.make_async_remote_copy(src_ref=g_refs[a].at[2 * k + (1 - c)], dst_ref=out_refs[a].at[k],
                                             send_sem=send_sems.at[4 * a + k], recv_sem=recv_sems.at[4 * a + k],
                                             device_id=(x, y, 1 - c), device_id_type=MESH)
                for a in range(n) for k in range(4)]

    def start(g_refs, out_refs, sems):
        for cp in copies(g_refs, out_refs, sems):
            cp.start()

    def wait(g_refs, out_refs, sems):
        for cp in copies(g_refs, out_refs, sems):
            cp.wait()

    return _Stage(gs, [jax.ShapeDtypeStruct((4,) + g.shape[1:], g.dtype) for g in gs], _dma_sems(4 * n), start, wait)


def _rows2d(shape):
    cols = shape[-1]
    rows = 1
    for s in shape[:-1]:
        rows *= s
    return rows, cols


def _add_own(g, recv, name):
    rows, cols = _rows2d(g.shape[1:])
    tr = _pick(rows, 256, 8)
    c = lax.axis_index("c").astype(jnp.int32).reshape(1)

    def body(c_ref, g_ref, r_ref, o_ref):
        o_ref[...] = (g_ref[...].astype(F32) + r_ref[...].astype(F32)).astype(o_ref.dtype)

    grid_spec = pltpu.PrefetchScalarGridSpec(
        num_scalar_prefetch=1, grid=(4, rows // tr),
        in_specs=[pl.BlockSpec((None, None, tr, cols), lambda k, i, c_ref: (k, c_ref[0], i, 0)),
                  pl.BlockSpec((None, tr, cols), lambda k, i, c_ref: (k, i, 0))],
        out_specs=pl.BlockSpec((None, tr, cols), lambda k, i, c_ref: (k, i, 0)))
    return _call(body, name, jax.ShapeDtypeStruct((4, rows, cols), g.dtype), grid_spec=grid_spec,
                 dims=("parallel", "parallel"))(c, g.reshape(4, 2, rows, cols), recv.reshape(4, rows, cols))


def _reduce_chips_stage(pas):
    n = len(pas)

    def copies(pa_refs, out_refs, sems):
        send_sems, recv_sems, _ = sems
        x, y, c = _position()
        my_chip = 2 * x + y
        out = []
        for a in range(n):
            for j, (px, py) in enumerate([(1 - x, y), (x, 1 - y), (1 - x, 1 - y)]):
                send = pltpu.make_async_remote_copy(src_ref=pa_refs[a].at[2 * px + py], dst_ref=out_refs[a].at[my_chip],
                                                    send_sem=send_sems.at[3 * a + j], recv_sem=recv_sems.at[3 * a + j],
                                                    device_id=(px, py, c), device_id_type=MESH)
                recv = pltpu.make_async_remote_copy(src_ref=pa_refs[a].at[2 * px + py], dst_ref=out_refs[a].at[2 * px + py],
                                                    send_sem=send_sems.at[3 * a + j], recv_sem=recv_sems.at[3 * a + j],
                                                    device_id=(px, py, c), device_id_type=MESH)
                out.append((send, recv))
        return out

    def local(pa_refs, out_refs, sems):
        x, y, _ = _position()
        return [pltpu.make_async_copy(pa_refs[a].at[2 * x + y], out_refs[a].at[2 * x + y], sems[2].at[a]) for a in range(n)]

    def start(pa_refs, out_refs, sems):
        for cp in local(pa_refs, out_refs, sems):
            cp.start()
        for send, _r in copies(pa_refs, out_refs, sems):
            send.start()

    def wait(pa_refs, out_refs, sems):
        for send, recv in copies(pa_refs, out_refs, sems):
            recv.wait_recv()
            send.wait_send()
        for cp in local(pa_refs, out_refs, sems):
            cp.wait()

    return _Stage(pas, [jax.ShapeDtypeStruct(pa.shape, pa.dtype) for pa in pas],
                  _dma_sems(3 * n) + [pltpu.SemaphoreType.DMA((n,))], start, wait)


def _adamw_math(w, g, m, v):
    m = ADAM_B1 * m + (1.0 - ADAM_B1) * g
    v = ADAM_B2 * v + (1.0 - ADAM_B2) * (g * g)
    m_hat = m / (1.0 - ADAM_B1 ** ADAM_STEP)
    v_hat = v / (1.0 - ADAM_B2 ** ADAM_STEP)
    delta = -ADAM_LR * (m_hat / (jnp.sqrt(v_hat) + ADAM_EPS) + ADAM_WD * w)
    return delta, m, v


def _sum_adamw(parts, w, m, v, name):
    n, rows, cols = parts.shape
    tr = _pick(rows, 128, 8)

    def body(p_ref, w_ref, m_ref, v_ref, g_ref, d_ref, nm_ref, nv_ref):
        g = p_ref[0].astype(F32)
        for k in range(1, n):
            g = g + p_ref[k].astype(F32)
        d, nm, nv = _adamw_math(w_ref[...], g, m_ref[...], v_ref[...])
        g_ref[...], d_ref[...], nm_ref[...], nv_ref[...] = g, d, nm, nv

    blk = pl.BlockSpec((tr, cols), lambda i: (i, 0))
    return _call(body, name, [jax.ShapeDtypeStruct((rows, cols), F32)] * 4, grid=(rows // tr,),
                 in_specs=[pl.BlockSpec((n, tr, cols), lambda i: (0, i, 0)), blk, blk, blk],
                 out_specs=[blk] * 4, dims=("parallel",))(parts, w, m, v)


_WEIGHTS = ("norm_g", "mem_norm_g", "w_mem_kv", "w_out", "conv_w_in", "conv_dw", "conv_dw_b", "conv_ln_g", "conv_ln_b",
            "mla_w_in", "mla_q_norm_g", "mla_w_uq", "mla_kv_norm_g", "mla_w_ukv", "final_norm_g")


_GATHER_GROUPS = {"a": ("conv_w_in",), "b": ("w_mem_kv", "w_out"), "c": ("mla_w_in", "mla_w_uq", "mla_w_ukv")}
_CARRIERS = {"l0_norm": ("gather chips", ("a",)), "l0_in": ("gather chips", ("b",)), "l0_glu": ("gather sibling", ("b",)),
             "l0_dwconv": ("gather chips", ("c",)), "l0_ln": ("gather sibling", ("c",)),
             "l1_in_dx": ("reduce sibling", ("l1",)), "l0_ln_bwd": ("reduce sibling", ("l0a",)),
             "l0_gate_bwd": ("reduce chips", ("l1", 0, 2)), "l0_dwconv_bwd": ("reduce chips", ("l1", 2, 5)),
             "l0_in_dw": ("reduce chips", ("l0a",)), "l0_in_dx": ("reduce sibling alone, then chips", ("l0b",))}


class _Schedule:
    def __init__(self, w):
        self.w, self.full, self.gather, self.reduce, self.reduced = w, {}, {}, {}, {}
        small = _all_gather_small(_as_tiles([w[n] for n in _SMALL_SHARDED]), "gather_small_weights").reshape(N_DEV, -1)
        o = 0
        for n in _SMALL_SHARDED:
            self.full[n] = _join(n, small[:, o:o + w[n].size].reshape((N_DEV,) + w[n].shape))
            o += w[n].size
        for n in _REPLICATED:
            self.full[n] = w[n]

    def carry(self, call):
        kind, (g, *part) = _CARRIERS[call]
        if kind == "gather chips":
            self.gather[g] = [_gather_chips_stage([self.w[n].astype(BF16) for n in _GATHER_GROUPS[g]])]
            return self.gather[g][0]
        if kind == "gather sibling":
            self.gather[g].append(_gather_sibling_stage(self.gather[g][0].outs))
            return self.gather[g][1]
        r = self.reduce[g]
        if kind == "reduce sibling":
            r["sibling"] = _reduce_sibling_stage(r["cut"])
            return r["sibling"]
        if kind != "reduce chips":
            r["sibling"] = _reduce_sibling_stage(r["cut"])
            _run_stage(r["sibling"], "reduce_sibling_" + g)
        if "partial" not in r:
            r["partial"] = [_add_own(c, s, "reduce_add_%s_%d" % (g, i))
                            for i, (c, s) in enumerate(zip(r["cut"], r["sibling"].outs))]
        lo, hi = part if part else (0, len(r["keys"]))
        stage = _reduce_chips_stage(r["partial"][lo:hi])
        r.setdefault("chips", []).append((r["keys"][lo:hi], stage))
        return stage

    def __getitem__(self, name):
        if name not in self.full:
            g = [k for k, names in _GATHER_GROUPS.items() if name in names][0]
            if len(self.gather[g]) == 1:
                self.gather[g].append(_gather_sibling_stage(self.gather[g][0].outs))
                _run_stage(self.gather[g][1], "gather_sibling_" + g)
            for n, buf in zip(_GATHER_GROUPS[g], self.gather[g][1].outs):
                self.full[n] = _PERM[n][0](_join(n, buf)) if n in _PERM else _join(n, buf)
        return self.full[name]

    def ready(self, group, grads, payload=BF16):
        keys, cut, small = [], [], []
        for (n, layer), g in grads.items():
            if n in _SMALL_SHARDED:
                small.append(_cut(n, g, self.w[n].shape).reshape(N_DEV, -1))
                continue
            keys.append((n, layer))
            if layer is not None:
                cut.append(g.reshape((N_DEV,) + self.w[n].shape[1:]).astype(payload))
            else:
                cut.append(_cut(n, _PERM[n][1](g) if n in _PERM else g, self.w[n].shape).astype(payload))
        if small:
            keys.append(("small", None))
            cut.append(jax.vmap(lambda r: _as_tiles([r]))(jnp.concatenate(small, axis=1)))
        self.reduce[group] = {"keys": keys, "cut": cut}

    def finish(self):
        out = {}
        for r in self.reduce.values():
            for keys, stage in r["chips"]:
                out.update(dict(zip(keys, stage.outs)))
        return out


def kernel(x, mem, positions, norm_g, mem_norm_g, w_mem_kv, w_out, conv_w_in, conv_dw, conv_dw_b, conv_ln_g, conv_ln_b, mla_w_in, mla_q_norm_g, mla_w_uq, mla_kv_norm_g, mla_w_ukv, final_norm_g, loss_target, m_norm_g, m_mem_norm_g, m_w_mem_kv, m_w_out, m_conv_w_in, m_conv_dw, m_conv_dw_b, m_conv_ln_g, m_conv_ln_b, m_mla_w_in, m_mla_q_norm_g, m_mla_w_uq, m_mla_kv_norm_g, m_mla_w_ukv, m_final_norm_g, v_norm_g, v_mem_norm_g, v_w_mem_kv, v_w_out, v_conv_w_in, v_conv_dw, v_conv_dw_b, v_conv_ln_g, v_conv_ln_b, v_mla_w_in, v_mla_q_norm_g, v_mla_w_uq, v_mla_kv_norm_g, v_mla_w_ukv, v_final_norm_g):
    w = dict(zip(_WEIGHTS, (norm_g, mem_norm_g, w_mem_kv, w_out, conv_w_in, conv_dw, conv_dw_b, conv_ln_g, conv_ln_b,
                            mla_w_in, mla_q_norm_g, mla_w_uq, mla_kv_norm_g, mla_w_ukv, final_norm_g)))
    m = dict(zip(_WEIGHTS, (m_norm_g, m_mem_norm_g, m_w_mem_kv, m_w_out, m_conv_w_in, m_conv_dw, m_conv_dw_b, m_conv_ln_g,
                            m_conv_ln_b, m_mla_w_in, m_mla_q_norm_g, m_mla_w_uq, m_mla_kv_norm_g, m_mla_w_ukv, m_final_norm_g)))
    v = dict(zip(_WEIGHTS, (v_norm_g, v_mem_norm_g, v_w_mem_kv, v_w_out, v_conv_w_in, v_conv_dw, v_conv_dw_b, v_conv_ln_g,
                            v_conv_ln_b, v_mla_w_in, v_mla_q_norm_g, v_mla_w_uq, v_mla_kv_norm_g, v_mla_w_ukv, v_final_norm_g)))

    sched = _Schedule(w)
    loss_local, dx, G = _forward_backward(x, mem, positions, loss_target, sched)
    loss = lax.psum(loss_local, ("x", "y", "c"))

    from_chips = sched.finish()
    out = [{}, {}, {}, {}]
    for n in _BIG:
        if n in _ROW_CUT:
            res = [_sum_adamw(from_chips[(n, l)], w[n][l], m[n][l], v[n][l], "adamw_%s_%d" % (n, l)) for l in range(w[n].shape[0])]
            res = [jnp.stack(r) for r in zip(*res)]
        else:
            rows, cols = _rows2d(w[n].shape)
            res = _sum_adamw(from_chips[(n, None)], w[n].reshape(rows, cols), m[n].reshape(rows, cols),
                             v[n].reshape(rows, cols), "adamw_" + n)
        for o, r in zip(out, res):
            o[n] = r.reshape(w[n].shape)
    small_like = [w[n] for n in _SMALL_SHARDED]
    res = _sum_adamw(from_chips[("small", None)], _as_tiles(small_like), _as_tiles([m[n] for n in _SMALL_SHARDED]),
                     _as_tiles([v[n] for n in _SMALL_SHARDED]), "adamw_small")
    for o, r in zip(out, res):
        for n, a in zip(_SMALL_SHARDED, _split_flat(r.reshape(-1), small_like)):
            o[n] = a

    rep_like = [w[n] for n in _REPLICATED]
    rep_parts = _all_gather_small(_as_tiles([G[n] for n in _REPLICATED]), "gather_replicated_grads")
    res = _sum_adamw(rep_parts, _as_tiles(rep_like), _as_tiles([m[n] for n in _REPLICATED]),
                     _as_tiles([v[n] for n in _REPLICATED]), "adamw_replicated")
    for o, r in zip(out, res):
        for n, a in zip(_REPLICATED, _split_flat(r.reshape(-1), rep_like)):
            o[n] = a

    return (loss, dx, *[out[0][n] for n in _WEIGHTS], *[out[1][n] for n in _WEIGHTS],
            *[out[2][n] for n in _WEIGHTS], *[out[3][n] for n in _WEIGHTS])
```

```python
import jax
import jax.numpy as jnp
from jax import lax
from jax.experimental import pallas as pl
from jax.experimental.pallas import tpu as pltpu

F32 = jnp.float32
BF16 = jnp.bfloat16
MESH = pl.DeviceIdType.MESH
N_DEV = 8
VMEM_LIMIT_BYTES = 48 * 1024 * 1024

MEM_HEADS, MEM_HEAD_DIM = 4, 128
MEM_WIDTH = MEM_HEADS * MEM_HEAD_DIM
CONV_KERNEL = 31
CONV_PAD = 32
MLA_HEADS, MLA_NOPE, MLA_ROPE, MLA_V = 12, 128, 64, 128
MLA_QK = MLA_NOPE + MLA_ROPE
HALF_ROPE = MLA_ROPE // 2
Q_RANK, KV_RANK = 512, 256
ROPE_THETA = 10000.0
RMS_EPS = 1e-6
LN_EPS = 1e-5
ADAM_LR, ADAM_B1, ADAM_B2, ADAM_EPS, ADAM_WD, ADAM_STEP = 0.001, 0.9, 0.999, 1e-08, 0.01, 10
NEG = -1e30


class _Stage:
    def __init__(self, ins, out_shapes, sems, start, wait, aliases=None):
        self.ins, self.out_shapes, self.sems = list(ins), list(out_shapes), list(sems)
        self.start, self.wait, self.aliases, self.outs = start, wait, dict(aliases or {}), None


def _call(body, name, out_shape, grid=None, in_specs=None, out_specs=None, scratch=(), dims=None, grid_spec=None, aliases=None,
          carry=None):
    params = dict(vmem_limit_bytes=VMEM_LIMIT_BYTES)
    if dims is not None:
        params["dimension_semantics"] = dims
    kw = {}
    if carry is not None:
        single = not isinstance(out_shape, (list, tuple))
        main_out = [out_shape] if single else list(out_shape)
        main_specs = [out_specs] if single else list(out_specs)
        n_in, n_out, n_scr = len(in_specs), len(main_out), len(scratch)
        x_in, x_out = len(carry.ins), len(carry.out_shapes)
        inner, steps = body, tuple(grid)

        def body(*refs):
            ins, xin = refs[:n_in], refs[n_in:n_in + x_in]
            outs = refs[n_in + x_in:n_in + x_in + n_out]
            xout = refs[n_in + x_in + n_out:n_in + x_in + n_out + x_out]
            scr = refs[n_in + x_in + n_out + x_out:n_in + x_in + n_out + x_out + n_scr]
            xsem = refs[n_in + x_in + n_out + x_out + n_scr:]
            ids = [pl.program_id(a) for a in range(len(steps))]
            first, last = ids[0] == 0, ids[0] == steps[0] - 1
            for a in range(1, len(steps)):
                first = jnp.logical_and(first, ids[a] == 0)
                last = jnp.logical_and(last, ids[a] == steps[a] - 1)
            pl.when(first)(lambda: carry.start(xin, xout, xsem))
            inner(*ins, *outs, *scr)
            pl.when(last)(lambda: carry.wait(xin, xout, xsem))

        hbm = pl.BlockSpec(memory_space=pltpu.HBM)
        aliases = dict(aliases or {})
        aliases.update({n_in + k: n_out + v for k, v in carry.aliases.items()})
        res = _call(body, name, main_out + carry.out_shapes, grid=grid, in_specs=list(in_specs) + [hbm] * x_in,
                    out_specs=main_specs + [hbm] * x_out, scratch=list(scratch) + carry.sems, dims=dims, aliases=aliases)

        def run(*args):
            outs = res(*args, *carry.ins)
            carry.outs = list(outs[n_out:])
            return outs[0] if single else outs[:n_out]

        return run
    if aliases:
        kw["input_output_aliases"] = aliases
    if grid_spec is not None:
        kw["grid_spec"] = grid_spec
    else:
        if grid is not None:
            kw["grid"] = grid
        kw["in_specs"] = in_specs
        kw["out_specs"] = out_specs
        kw["scratch_shapes"] = list(scratch)
    return pl.pallas_call(body, name=name, out_shape=out_shape, compiler_params=pltpu.CompilerParams(**params), **kw)


def _pick(n, target, mult):
    best = None
    for d in range(mult, min(n, target) + 1, mult):
        if n % d == 0:
            best = d
    return n if best is None else best


_DOT_DIMS = {"nn": (((1,), (0,)), ((), ())), "nt": (((1,), (1,)), ((), ())), "tn": (((0,), (0,)), ((), ()))}


def _mm(a, b, mode, out_dtype, name, res=None, carry=None):
    if mode == "tn":
        a, mode = a.T, "nn"
    if mode == "nn":
        (M, K), N = a.shape, b.shape[1]
    else:
        (M, K), N = a.shape, b.shape[0]
    tm = _pick(M, 1024, 8)
    tn = _pick(N, 1536, 128)
    tk = _pick(K, 1536, 128)
    nk = K // tk
    has_res = res is not None

    def body(*refs):
        if has_res:
            a_ref, b_ref, r_ref, o_ref, acc = refs
        else:
            a_ref, b_ref, o_ref, acc = refs
        k = pl.program_id(2)
        part = lax.dot_general(a_ref[...].astype(BF16), b_ref[...].astype(BF16), _DOT_DIMS[mode],
                               preferred_element_type=F32)
        if nk == 1:
            o_ref[...] = (part + r_ref[...] if has_res else part).astype(o_ref.dtype)
            return

        @pl.when(k == 0)
        def _():
            acc[...] = part

        @pl.when(k > 0)
        def _():
            acc[...] += part

        @pl.when(k == nk - 1)
        def _():
            r = acc[...]
            if has_res:
                r = r + r_ref[...]
            o_ref[...] = r.astype(o_ref.dtype)

    a_spec = pl.BlockSpec((tm, tk), lambda i, j, k: (i, k))
    b_spec = {"nn": pl.BlockSpec((tk, tn), lambda i, j, k: (k, j)),
              "nt": pl.BlockSpec((tn, tk), lambda i, j, k: (j, k))}[mode]
    o_spec = pl.BlockSpec((tm, tn), lambda i, j, k: (i, j))
    in_specs = [a_spec, b_spec] + ([o_spec] if has_res else [])
    args = (a, b) + ((res,) if has_res else ())
    return _call(body, name, jax.ShapeDtypeStruct((M, N), out_dtype), grid=(M // tm, N // tn, nk),
                 in_specs=in_specs, out_specs=o_spec, scratch=[pltpu.VMEM((tm, tn), F32)],
                 dims=("parallel", "parallel", "arbitrary"), carry=carry)(*args)


def _views(rows):
    return [r if isinstance(r, tuple) else (r, r.shape[1], 0) for r in rows]


def _rowwise(f, rows, params, outs, name, tb=256, carry=None, into=None):
    rows = _views(rows)
    T = rows[0][0].shape[0]
    tb = min(tb, T)
    nr, npar = len(rows), len(params)
    outs = [o if len(o) == 3 else (o[0], o[1], o[0]) for o in outs]
    into = into or []

    def body(*refs):
        vals = f(*[r[...].astype(F32) for r in refs[:nr]], *[p[...] for p in refs[nr:nr + npar]])
        for o_ref, v in zip(refs[nr + npar + len(into):], vals):
            o_ref[...] = v.astype(o_ref.dtype)

    row_spec = lambda w, cb=0: pl.BlockSpec((tb, w), lambda i: (i, cb))
    par_spec = lambda w: pl.BlockSpec((1, w), lambda i: (0, 0))
    out_shape = [jax.ShapeDtypeStruct((T, tw), dt) for _, dt, tw in outs]
    out_specs = [row_spec(w) for w, _, _ in outs]
    in_specs = [row_spec(w, cb) for _, w, cb in rows] + [par_spec(p.shape[1]) for p in params]
    args = [r[0] for r in rows] + list(params)
    aliases = {}
    for k, arr, cb in into:
        aliases[len(args)] = k
        in_specs.append(pl.BlockSpec(memory_space=pl.ANY))
        args.append(arr)
        out_shape[k] = jax.ShapeDtypeStruct(arr.shape, arr.dtype)
        out_specs[k] = row_spec(outs[k][0], cb)
    return _call(body, name, out_shape, grid=(T // tb,), in_specs=in_specs, out_specs=out_specs, dims=("parallel",),
                 carry=carry, aliases=aliases)(*args)


def _rowwise_bwd(f, rows, params, douts, n_diff, name, tb=256, carry=None, add=None, into=None):
    rows, douts = _views(rows), _views(douts)
    T = rows[0][0].shape[0]
    tb = min(tb, T)
    nr, npar, nd = len(rows), len(params), len(douts)
    n_add = 0 if add is None else 1

    def body(*refs):
        rv = [r[...].astype(F32) for r in refs[:nr]]
        pv = [p[...] for p in refs[nr:nr + npar]]
        dv = [d[...].astype(F32) for d in refs[nr + npar:nr + npar + nd]]
        o_refs = refs[nr + npar + nd + n_add + (0 if into is None else 1):]
        fixed = rv[n_diff:]

        def g(*xs):
            return tuple(f(*xs[:n_diff], *fixed, *xs[n_diff:]))

        _, vjp = jax.vjp(g, *rv[:n_diff], *pv)
        grads = list(vjp(tuple(dv)))
        if add is not None:
            grads[0] = grads[0] + refs[nr + npar + nd][...]
        for o_ref, gr in zip(o_refs[:n_diff], grads[:n_diff]):
            o_ref[...] = gr.astype(o_ref.dtype)
        first = pl.program_id(0) == 0
        for o_ref, gr in zip(o_refs[n_diff:], grads[n_diff:]):
            @pl.when(first)
            def _(o_ref=o_ref):
                o_ref[...] = jnp.zeros_like(o_ref)

            o_ref[...] += gr

    row_spec = lambda w, cb=0: pl.BlockSpec((tb, w), lambda i: (i, cb))
    par_spec = lambda w: pl.BlockSpec((1, w), lambda i: (0, 0))
    out_shape = ([jax.ShapeDtypeStruct((T, w), F32) for _, w, _ in rows[:n_diff]]
                 + [jax.ShapeDtypeStruct((1, p.shape[1]), F32) for p in params])
    out_specs = [row_spec(w) for _, w, _ in rows[:n_diff]] + [par_spec(p.shape[1]) for p in params]
    in_specs = ([row_spec(w, cb) for _, w, cb in rows] + [par_spec(p.shape[1]) for p in params]
                + [row_spec(w, cb) for _, w, cb in douts])
    args = [r[0] for r in rows] + list(params) + [d[0] for d in douts]
    aliases = None
    if add is not None:
        in_specs.append(row_spec(add.shape[1]))
        args.append(add)
    if into is not None:
        aliases = {len(args): 0}
        in_specs.append(pl.BlockSpec(memory_space=pl.ANY))
        args.append(into[0])
        out_shape[0] = jax.ShapeDtypeStruct(into[0].shape, into[0].dtype)
        out_specs[0] = row_spec(rows[0][1], into[1])
    return _call(body, name, out_shape, grid=(T // tb,), in_specs=in_specs, out_specs=out_specs,
                 dims=("arbitrary",), carry=carry, aliases=aliases)(*args)


def _sig(x):
    return 1.0 / (1.0 + jnp.exp(-x))


def _rms(x, g):
    return x * lax.rsqrt(jnp.mean(x * x, axis=-1, keepdims=True) + RMS_EPS) * g


def _f_rms(x, g):
    return (_rms(x, g),)


def _f_ln_silu(x, g, b):
    mu = jnp.mean(x, axis=-1, keepdims=True)
    xc = x - mu
    var = jnp.mean(xc * xc, axis=-1, keepdims=True)
    y = xc * lax.rsqrt(var + LN_EPS) * g + b
    return (y * _sig(y),)


def _rope128(x, cos_p, sin_p):
    return x * cos_p + pltpu.roll(x, 64, 1) * sin_p


def _rope128_t(d, cos_p, sin_p):
    return d * cos_p + pltpu.roll(d * sin_p, 64, 1)


def _f_rope(xq, xk, cos_p, sin_p):
    heads = [_rope128(xq[:, h * 128:(h + 1) * 128], cos_p, sin_p) for h in range(MLA_HEADS)]
    return (jnp.concatenate(heads, axis=1), _rope128(xk, cos_p, sin_p))


def _f_rope_t(dq, dk_heads, cos_p, sin_p):
    heads = [_rope128_t(dq[:, h * 128:(h + 1) * 128], cos_p, sin_p) for h in range(MLA_HEADS)]
    dk = dk_heads[:, 0:128]
    for h in range(1, MLA_HEADS):
        dk = dk + dk_heads[:, h * 128:(h + 1) * 128]
    return (jnp.concatenate(heads, axis=1), _rope128_t(dk, cos_p, sin_p))


GATE_LANES = 256


def _gate_fwd(ycat, proj, z_col, name, tb=1024):
    T, width = ycat.shape
    zb = z_col // GATE_LANES

    def body(y_ref, z_ref, o_ref, ot_ref):
        z = z_ref[...]
        y = y_ref[...] * (z * _sig(z))
        o_ref[...] = y.astype(o_ref.dtype)
        ot_ref[...] = y.T.astype(ot_ref.dtype)

    blk = pl.BlockSpec((tb, GATE_LANES), lambda i, c: (i, c))
    return _call(body, name, [jax.ShapeDtypeStruct((T, width), BF16), jax.ShapeDtypeStruct((width, T), BF16)],
                 grid=(T // tb, width // GATE_LANES),
                 in_specs=[blk, pl.BlockSpec((tb, GATE_LANES), lambda i, c: (i, zb + c))],
                 out_specs=[blk, pl.BlockSpec((GATE_LANES, tb), lambda i, c: (c, i))],
                 dims=("parallel", "parallel"))(ycat, proj)


def _gate_bwd(ycat, proj, z_col, dy, name, tb=1024, carry=None):
    T, width = ycat.shape
    zb = z_col // GATE_LANES

    def body(y_ref, z_ref, dy_ref, dycat_ref, dz_ref):
        z, d = z_ref[...], dy_ref[...]
        s = _sig(z)
        dycat_ref[...] = d * (z * s)
        dz_ref[...] = (d * y_ref[...] * (s * (1.0 + z * (1.0 - s)))).astype(dz_ref.dtype)

    blk = pl.BlockSpec((tb, GATE_LANES), lambda i, c: (i, c))
    zblk = pl.BlockSpec((tb, GATE_LANES), lambda i, c: (i, zb + c))
    return _call(body, name, [jax.ShapeDtypeStruct((T, width), F32), jax.ShapeDtypeStruct(proj.shape, BF16)],
                 grid=(T // tb, width // GATE_LANES), in_specs=[blk, zblk, blk], out_specs=[blk, zblk],
                 dims=("parallel", "parallel"), carry=carry)(ycat, proj, dy)


def _glu_bwd(proj, d_glu, d_proj, name, tb=256):
    T, w = d_glu.shape

    def body(a_ref, g_ref, d_ref, _, o_ref):
        s, d = _sig(g_ref[...]), d_ref[...]
        o_ref[:, 0:w] = (d * s).astype(o_ref.dtype)
        o_ref[:, w:2 * w] = (d * a_ref[...] * (s * (1.0 - s))).astype(o_ref.dtype)

    return _call(body, name, jax.ShapeDtypeStruct(d_proj.shape, d_proj.dtype), grid=(T // tb,),
                 in_specs=[pl.BlockSpec((tb, w), lambda i: (i, 0)), pl.BlockSpec((tb, w), lambda i: (i, 1)),
                           pl.BlockSpec((tb, w), lambda i: (i, 0)), pl.BlockSpec(memory_space=pl.ANY)],
                 out_specs=pl.BlockSpec((tb, 2 * w), lambda i: (i, 0)), dims=("parallel",),
                 aliases={3: 0})(proj, proj, d_glu, d_proj)


def _final_loss(h, tgt, g, name, tb=256):
    T, D = h.shape

    def body(h_ref, t_ref, g_ref, dh_ref, dg_ref, loss_ref):
        tv = t_ref[...]

        def rowloss(hh, gg):
            e = _rms(hh, gg) - tv
            return 0.5 * jnp.mean(e * e, axis=-1, keepdims=True)

        lr, vjp = jax.vjp(rowloss, h_ref[...], g_ref[...])
        dh, dg = vjp(jnp.ones_like(lr))
        dh_ref[...] = dh

        @pl.when(pl.program_id(0) == 0)
        def _():
            dg_ref[...] = jnp.zeros_like(dg_ref)
            loss_ref[...] = jnp.zeros_like(loss_ref)

        dg_ref[...] += dg
        loss_ref[...] += jnp.broadcast_to(jnp.sum(lr, axis=0, keepdims=True), loss_ref.shape)

    row = pl.BlockSpec((tb, D), lambda i: (i, 0))
    par = pl.BlockSpec((1, D), lambda i: (0, 0))
    return _call(body, name,
                 [jax.ShapeDtypeStruct((T, D), F32), jax.ShapeDtypeStruct((1, D), F32), jax.ShapeDtypeStruct((1, 128), F32)],
                 grid=(T // tb,), in_specs=[row, row, par],
                 out_specs=[row, par, pl.BlockSpec((1, 128), lambda i: (0, 0))], dims=("arbitrary",))(h, tgt, g)


CONV_ROWS = 128
CONV_LANES = 256


def _sublane_phases(pad, n):
    for r in range(1, 8):
        for c0 in range(0, n - 8, 256):
            rows = min(256, n - 8 - c0)
            pad[r, c0:c0 + rows, :] = pad[0, c0 + r:c0 + r + rows, :]


def _dwconv_fwd(proj, C, w, b, B, S, name, carry=None):
    cb = CONV_LANES
    off = CONV_PAD - (CONV_KERNEL - 1)

    def body(a_ref, g_ref, w_ref, b_ref, o_ref, pad):
        pad[0, 0:CONV_PAD, :] = jnp.zeros((CONV_PAD, cb), F32)
        for c0 in range(0, S, 256):
            pad[0, CONV_PAD + c0:CONV_PAD + c0 + 256, :] = a_ref[c0:c0 + 256, :] * _sig(g_ref[c0:c0 + 256, :])
        _sublane_phases(pad, S + CONV_PAD)
        for t0 in range(0, S, CONV_ROWS):
            acc = jnp.broadcast_to(b_ref[...], (CONV_ROWS, cb))
            for k in range(CONV_KERNEL):
                r, base = (off + k) % 8, t0 + (off + k) // 8 * 8
                acc = acc + w_ref[k:k + 1, :] * pad[r, base:base + CONV_ROWS, :]
            o_ref[t0:t0 + CONV_ROWS, :] = acc

    return _call(body, name, jax.ShapeDtypeStruct((B, S, C), F32), grid=(B, C // cb),
                 in_specs=[pl.BlockSpec((S, cb), lambda i, j: (i, j)), pl.BlockSpec((S, cb), lambda i, j: (i, C // cb + j)),
                           pl.BlockSpec((CONV_KERNEL, cb), lambda i, j: (0, j)),
                           pl.BlockSpec((1, cb), lambda i, j: (0, j))],
                 out_specs=pl.BlockSpec((None, S, cb), lambda i, j: (i, 0, j)),
                 scratch=[pltpu.VMEM((8, S + CONV_PAD, cb), F32)], dims=("parallel", "parallel"),
                 carry=carry)(proj, proj, w, b)


def _dwconv_bwd(proj, w, dy, name, carry=None):
    B, S, C = dy.shape
    cb = CONV_LANES
    groups = CONV_ROWS // 8

    def body(a_ref, g_ref, w_ref, dy_ref, dx_ref, dw_ref, db_ref, dypad, wacc):
        dypad[0, 0:S, :] = dy_ref[...]
        dypad[0, S:, :] = jnp.zeros((CONV_PAD, cb), F32)
        _sublane_phases(dypad, S + CONV_PAD)
        wacc[...] = jnp.zeros_like(wacc)
        for t0 in range(0, S, CONV_ROWS):
            xc = a_ref[t0:t0 + CONV_ROWS, :] * _sig(g_ref[t0:t0 + CONV_ROWS, :])
            acc = jnp.zeros((CONV_ROWS, cb), F32)
            for k in range(CONV_KERNEL):
                o = (CONV_KERNEL - 1) - k
                dys = dypad[o % 8, t0 + o // 8 * 8:t0 + o // 8 * 8 + CONV_ROWS, :]
                acc = acc + w_ref[k:k + 1, :] * dys
                wacc[k] += jnp.sum((dys * xc).reshape(groups, 8, cb), axis=0)
            wacc[CONV_KERNEL] += jnp.sum(dy_ref[t0:t0 + CONV_ROWS, :].reshape(groups, 8, cb), axis=0)
            dx_ref[t0:t0 + CONV_ROWS, :] = acc

        @pl.when(pl.program_id(1) == 0)
        def _():
            dw_ref[...] = jnp.zeros_like(dw_ref)
            db_ref[...] = jnp.zeros_like(db_ref)

        for k in range(CONV_KERNEL):
            dw_ref[k:k + 1, :] += jnp.sum(wacc[k], axis=0, keepdims=True)
        db_ref[...] += jnp.sum(wacc[CONV_KERNEL], axis=0, keepdims=True)

    blk = pl.BlockSpec((None, S, cb), lambda j, i: (i, 0, j))
    return _call(body, name,
                 [jax.ShapeDtypeStruct((B, S, C), F32), jax.ShapeDtypeStruct((CONV_KERNEL, C), F32),
                  jax.ShapeDtypeStruct((1, C), F32)],
                 grid=(C // cb, B),
                 in_specs=[pl.BlockSpec((S, cb), lambda j, i: (i, j)), pl.BlockSpec((S, cb), lambda j, i: (i, C // cb + j)),
                           pl.BlockSpec((CONV_KERNEL, cb), lambda j, i: (0, j)), blk],
                 out_specs=[blk, pl.BlockSpec((CONV_KERNEL, cb), lambda j, i: (0, j)),
                            pl.BlockSpec((1, cb), lambda j, i: (0, j))],
                 scratch=[pltpu.VMEM((8, S + CONV_PAD, cb), F32), pltpu.VMEM((CONV_KERNEL + 1, 8, cb), F32)],
                 dims=("parallel", "arbitrary"), carry=carry)(proj, proj, w, dy)


ATTN_TILE = {"fwd": 1024, "bwd": 1024, "cross fwd": 512}
ATTN_SUB = {"fwd": 256, "bwd": 512}


def _attn_shapes(Sq, Sk, causal, pass_):
    tq = min(Sq, ATTN_TILE[pass_ if causal or pass_ == "bwd" else "cross fwd"])
    tk = tq if causal else min(Sk, ATTN_TILE[pass_])
    return tq, tk, min(ATTN_SUB[pass_], tq)


def _causal_bias(n):
    r = lax.broadcasted_iota(jnp.int32, (n, n), 0)
    c = lax.broadcasted_iota(jnp.int32, (n, n), 1)
    return jnp.where(c <= r, 0.0, NEG).astype(F32)


def _mask_diagonal(s, bias):
    n, nc = s.shape
    if nc == n:
        return s + bias
    return jnp.concatenate([s[:, :nc - n], s[:, nc - n:] + bias], axis=1)


def _attn_fwd(q, q_c0, qr, k, k_c0, kr, v, v_c0, B, Sq, Sk, H, causal, scale, name, into=None, o_c0=0, o_width=None,
              kv_stride=1):
    tq, tk, sub = _attn_shapes(Sq, Sk, causal, "fwd")
    nq, nk, nsub = Sq // tq, Sk // tk, tq // sub
    rope = qr is not None

    def body(*refs):
        refs = list(refs)
        qn_ref = refs.pop(0)
        qr_ref = refs.pop(0) if rope else None
        kn_ref = refs.pop(0)
        kr_ref = refs.pop(0) if rope else None
        v_ref = refs.pop(0)
        if into is not None:
            refs.pop(0)
        o_ref, lse_ref, m_s, l_s, acc = refs
        qi = pl.program_id(2)
        m_s[...] = jnp.full_like(m_s, NEG)
        l_s[...] = jnp.zeros_like(l_s)
        acc[...] = jnp.zeros_like(acc)
        bias = _causal_bias(sub) if causal else None
        qs = []
        for r in range(nsub):
            qn = qn_ref[r * sub:(r + 1) * sub, :].astype(BF16)
            qs.append(jnp.concatenate([qn, qr_ref[r * sub:(r + 1) * sub, :]], axis=1) if rope else qn)

        def step(j, masked):
            ks = pl.ds(pl.multiple_of(j * tk, tk), tk)
            kk = jnp.concatenate([kn_ref[ks, :], kr_ref[ks, :]], axis=1) if rope else kn_ref[ks, :]
            vv = v_ref[ks, :]
            for r in range(nsub):
                rows = slice(r * sub, (r + 1) * sub)
                nc = (r + 1) * sub if masked else tk
                s = lax.dot_general(qs[r], kk[:nc], _DOT_DIMS["nt"], preferred_element_type=F32) * scale
                if masked:
                    s = _mask_diagonal(s, bias)
                m_old = m_s[rows, :]
                m_new = jnp.maximum(m_old, jnp.max(s, axis=-1, keepdims=True))
                p = jnp.exp(s - m_new)
                alpha = jnp.exp(m_old - m_new)
                l_s[rows, :] = alpha * l_s[rows, :] + jnp.sum(p, axis=-1, keepdims=True)
                acc[rows, :] = alpha * acc[rows, :] + jnp.dot(p.astype(BF16), vv[:nc], preferred_element_type=F32)
                m_s[rows, :] = m_new

        def unmasked(j, carry):
            step(j, False)
            return carry

        if causal:
            lax.fori_loop(0, qi, unmasked, 0)
            step(qi, True)
        else:
            lax.fori_loop(0, nk, unmasked, 0)
        o_ref[...] = (acc[...] / l_s[...]).astype(o_ref.dtype)
        lse_ref[...] = m_s[...] + jnp.log(l_s[...])

    qspec = lambda c0: pl.BlockSpec((tq, 128), lambda b, h, i: (b * nq + i, c0 + h))
    kspec = lambda c0: pl.BlockSpec((Sk, 128), lambda b, h, i: (b, c0 + kv_stride * h))
    in_specs, args = [qspec(q_c0)], [q]
    if rope:
        in_specs.append(qspec(0)); args.append(qr)
    in_specs.append(kspec(k_c0)); args.append(k)
    if rope:
        in_specs.append(pl.BlockSpec((Sk, 128), lambda b, h, i: (b, 0))); args.append(kr)
    in_specs.append(kspec(v_c0)); args.append(v)
    aliases = {}
    if into is not None:
        aliases = {len(args): 0}
        in_specs.append(pl.BlockSpec(memory_space=pl.ANY)); args.append(into)
        o_shape = jax.ShapeDtypeStruct(into.shape, into.dtype)
    else:
        o_shape = jax.ShapeDtypeStruct((B * Sq, o_width), F32)
    return _call(body, name, [o_shape, jax.ShapeDtypeStruct((B * H, Sq, 1), F32)], grid=(B, H, nq), in_specs=in_specs,
                 out_specs=[qspec(o_c0), pl.BlockSpec((None, tq, 1), lambda b, h, i: (b * H + h, i, 0))],
                 scratch=[pltpu.VMEM((tq, 1), F32), pltpu.VMEM((tq, 1), F32), pltpu.VMEM((tq, 128), F32)],
                 dims=("parallel", "parallel", "arbitrary"), aliases=aliases)(*args)


def _attn_bwd(q, q_c0, qr, k, k_c0, kr, v, v_c0, o, do, o_c0, lse, B, Sq, Sk, H, causal, scale, name, dq_into=None,
              kv_stride=1):
    tq, tk, sub = _attn_shapes(Sq, Sk, causal, "bwd")
    nq, nk, nsub = Sq // tq, Sk // tk, tq // sub
    rope = qr is not None
    dk_w = 256 if rope else 128

    def body(*refs):
        refs = list(refs)
        qn_ref = refs.pop(0)
        qr_ref = refs.pop(0) if rope else None
        kn_ref = refs.pop(0)
        kr_ref = refs.pop(0) if rope else None
        v_ref, o_ref, do_ref, lse_ref = refs[:4]
        refs = refs[4 + (0 if dq_into is None else 1):]
        dqn_ref = refs.pop(0)
        dqr_ref = refs.pop(0) if rope else None
        dkn_ref = refs.pop(0)
        dkr_ref = refs.pop(0) if rope else None
        dv_ref = None if rope else refs.pop(0)
        q_s, do_s, dl_s, dq_acc, dk_acc, dv_acc = refs
        kj = pl.program_id(2)

        @pl.when(kj == 0)
        def _():
            qn = qn_ref[...].astype(BF16)
            q_s[...] = jnp.concatenate([qn, qr_ref[...]], axis=1) if rope else qn
            dof = do_ref[...]
            do_s[...] = dof.astype(BF16)
            dl_s[...] = jnp.sum(dof * o_ref[...], axis=-1, keepdims=True)
            dq_acc[...] = jnp.zeros_like(dq_acc)

        kk = jnp.concatenate([kn_ref[...], kr_ref[...]], axis=1) if rope else kn_ref[...]
        vv = v_ref[...]
        bias = _causal_bias(sub) if causal else None
        dk_acc[...] = jnp.zeros_like(dk_acc)
        dv_acc[...] = jnp.zeros_like(dv_acc)

        def step(i, masked):
            for r in range(nsub):
                rows = pl.ds(pl.multiple_of(i * tq + r * sub, sub), sub)
                qq, dob = q_s[rows, :], do_s[rows, :]
                nc = (r + 1) * sub if masked else tk
                kc, vc = kk[:nc], vv[:nc]
                s = lax.dot_general(qq, kc, _DOT_DIMS["nt"], preferred_element_type=F32) * scale
                if masked:
                    s = _mask_diagonal(s, bias)
                p = jnp.exp(s - lse_ref[rows, :])
                dp = lax.dot_general(dob, vc, _DOT_DIMS["nt"], preferred_element_type=F32)
                ds = (p * (dp - dl_s[rows, :]) * scale).astype(BF16)
                dv_acc[0:nc, :] += lax.dot_general(p.astype(BF16), dob, _DOT_DIMS["tn"], preferred_element_type=F32)
                dk_acc[0:nc, :] += lax.dot_general(ds, qq, _DOT_DIMS["tn"], preferred_element_type=F32)
                dq_acc[rows, :] += jnp.dot(ds, kc, preferred_element_type=F32)

        def unmasked(i, carry):
            step(i, False)
            return carry

        if causal:
            step(kj, True)
            lax.fori_loop(kj + 1, nq, unmasked, 0)
        else:
            lax.fori_loop(0, nq, unmasked, 0)
        if rope:
            dkn_ref[...] = jnp.concatenate([dk_acc[:, 0:128], dv_acc[...]], axis=1).astype(dkn_ref.dtype)
            dkr_ref[...] = dk_acc[:, 128:256]
        else:
            dkn_ref[...] = dk_acc[...]
            dv_ref[...] = dv_acc[...]

        @pl.when(kj == nk - 1)
        def _():
            dqn_ref[...] = dq_acc[:, 0:128].astype(dqn_ref.dtype)
            if rope:
                dqr_ref[...] = dq_acc[:, 128:256]

    qspec = lambda c0: pl.BlockSpec((Sq, 128), lambda b, h, j: (b, c0 + h))
    kspec = lambda c0: pl.BlockSpec((tk, 128), lambda b, h, j: (b * nk + j, c0 + kv_stride * h))
    in_specs, args = [qspec(q_c0)], [q]
    if rope:
        in_specs.append(qspec(0)); args.append(qr)
    in_specs.append(kspec(k_c0)); args.append(k)
    if rope:
        in_specs.append(pl.BlockSpec((tk, 128), lambda b, h, j: (b * nk + j, 0))); args.append(kr)
    in_specs += [kspec(v_c0), qspec(o_c0), qspec(o_c0), pl.BlockSpec((None, Sq, 1), lambda b, h, j: (b * H + h, 0, 0))]
    args += [v, o, do, lse]
    h_rows_q = jax.ShapeDtypeStruct((B * Sq, H * 128), F32)
    h_rows_k = jax.ShapeDtypeStruct((B * Sk, H * 128), F32)
    out_shape, out_specs, aliases = [h_rows_q], [qspec(0)], None
    if rope:
        out_shape = [jax.ShapeDtypeStruct((B * Sq, 2 * H * 128), BF16)]
    if dq_into is not None:
        aliases = {len(args): 0}
        in_specs.append(pl.BlockSpec(memory_space=pl.ANY)); args.append(dq_into[0])
        out_shape, out_specs = [jax.ShapeDtypeStruct(dq_into[0].shape, dq_into[0].dtype)], [qspec(dq_into[1])]
    if rope:
        out_shape.append(h_rows_q); out_specs.append(qspec(0))
    hspec = lambda w: pl.BlockSpec((tk, w), lambda b, h, j: (b * nk + j, h))
    if rope:
        out_shape += [jax.ShapeDtypeStruct((B * Sk, H * 256), BF16), h_rows_k]
        out_specs += [hspec(256), hspec(128)]
    else:
        out_shape += [h_rows_k, h_rows_k]
        out_specs += [hspec(128), hspec(128)]
    return _call(body, name, out_shape, grid=(B, H, nk), in_specs=in_specs, out_specs=out_specs,
                 scratch=[pltpu.VMEM((Sq, dk_w), BF16), pltpu.VMEM((Sq, 128), BF16), pltpu.VMEM((Sq, 1), F32),
                          pltpu.VMEM((Sq, dk_w), F32), pltpu.VMEM((tk, dk_w), F32), pltpu.VMEM((tk, 128), F32)],
                 dims=("parallel", "parallel", "arbitrary"), aliases=aliases)(*args)


def _mem_attention_fwd(proj, q_col, ycat, mem2, mem_g, w_mem, B, S, tag):
    M = mem2.shape[0] // B
    (memn,) = _rowwise(_f_rms, [mem2], [mem_g], [(mem2.shape[1], BF16)], tag + "_memnorm")
    kvm = _mm(memn, w_mem, "nn", BF16, tag + "_memkv")
    o_c0 = ycat.shape[1] // 128 - MEM_HEADS
    ycat, lse = _attn_fwd(proj, q_col // 128, None, kvm, 0, None, kvm, MEM_HEADS, B, S, M, MEM_HEADS, False,
                          MEM_HEAD_DIM ** -0.5, tag + "_memattn", into=ycat, o_c0=o_c0)
    return ycat, (memn, kvm, lse)


def _mem_attention_bwd(proj, q_col, ycat, d_ycat, d_proj, saved, mem2, mem_g, w_mem, B, S, tag):
    memn, kvm, lse = saved
    M = mem2.shape[0] // B
    o_c0 = ycat.shape[1] // 128 - MEM_HEADS
    d_q, d_k, d_v = _attn_bwd(proj, q_col // 128, None, kvm, 0, None, kvm, MEM_HEADS, ycat, d_ycat, o_c0, lse, B, S, M,
                              MEM_HEADS, False, MEM_HEAD_DIM ** -0.5, tag + "_memattn_bwd", dq_into=(d_proj, q_col // 128))
    d_kvm = jnp.concatenate([d_k, d_v], axis=1).astype(BF16)
    d_w_mem = _mm(memn, d_kvm, "tn", F32, tag + "_memkv_dw")
    d_memn = _mm(d_kvm, w_mem, "nt", F32, tag + "_memkv_dx")
    _, d_mem_g = _rowwise_bwd(_f_rms, [mem2], [mem_g], [d_memn], 1, tag + "_memnorm_bwd")
    return d_q, d_w_mem, d_mem_g


def _rope_tables(positions):
    inv_freq = 1.0 / (ROPE_THETA ** (jnp.arange(0, MLA_ROPE, 2, dtype=F32) / MLA_ROPE))
    ang = positions.astype(F32).reshape(-1, 1) * inv_freq
    cos, sin, zero = jnp.cos(ang), jnp.sin(ang), jnp.zeros_like(ang)
    return jnp.concatenate([cos, zero, cos, zero], axis=1), jnp.concatenate([-sin, zero, sin, zero], axis=1)


def _forward_backward(x, mem, positions, target, W):
    B, S, D = x.shape
    T = B * S
    conv_w = W["conv_dw"].shape[1]
    mix_w = 2 * D
    h0 = x.reshape(T, D)
    mem2 = mem.reshape(-1, D)
    tgt = target.reshape(T, D)
    row = lambda v: v.reshape(1, -1)
    n_nope = MLA_HEADS * MLA_NOPE

    g0 = row(W["norm_g"][0])
    (u0,) = _rowwise(_f_rms, [h0], [g0], [(D, BF16)], "l0_norm", carry=W.carry("l0_norm"))
    proj0 = _mm(u0, W["conv_w_in"], "nn", F32, "l0_in", carry=W.carry("l0_in"))
    qm0_col, z0_col = 2 * conv_w, 2 * conv_w + MEM_WIDTH
    dw, dwb = W["conv_dw"], row(W["conv_dw_b"][0])
    cv = _dwconv_fwd(proj0, conv_w, dw, dwb, B, S, "l0_dwconv", carry=W.carry("l0_dwconv")).reshape(T, conv_w)
    ln_g, ln_b = row(W["conv_ln_g"][0]), row(W["conv_ln_b"][0])
    (ycat0,) = _rowwise(_f_ln_silu, [cv], [ln_g, ln_b], [(conv_w, F32, mix_w)], "l0_ln", carry=W.carry("l0_ln"))
    mg0 = row(W["mem_norm_g"][0])
    ycat0, mem_saved0 = _mem_attention_fwd(proj0, qm0_col, ycat0, mem2, mg0, W["w_mem_kv"][0], B, S, "l0")
    y0, y0_t = _gate_fwd(ycat0, proj0, z0_col, "l0_gate")
    h1 = _mm(y0, W["w_out"][0], "nn", F32, "l0_out", res=h0, carry=W.carry("l0_out"))

    g1 = row(W["norm_g"][1])
    (u1,) = _rowwise(_f_rms, [h1], [g1], [(D, BF16)], "l1_norm")
    proj1 = _mm(u1, W["mla_w_in"], "nn", F32, "l1_in")
    cq, ckv = (proj1, Q_RANK, 0), (proj1, KV_RANK, Q_RANK // KV_RANK)
    qm1_col = Q_RANK + KV_RANK
    z1_col = qm1_col + MEM_WIDTH
    kr_col = z1_col + mix_w
    qg, kvg = row(W["mla_q_norm_g"]), row(W["mla_kv_norm_g"])
    (cqn,) = _rowwise(_f_rms, [cq], [qg], [(Q_RANK, BF16)], "l1_qnorm")
    (ckvn,) = _rowwise(_f_rms, [ckv], [kvg], [(KV_RANK, BF16)], "l1_kvnorm")
    qf = _mm(cqn, W["mla_w_uq"], "nn", F32, "l1_uq")
    kvf = _mm(ckvn, W["mla_w_ukv"], "nn", BF16, "l1_ukv")
    cos_p, sin_p = _rope_tables(positions)
    qr, kr = _rowwise(_f_rope, [(qf, n_nope, 1), (proj1, 128, kr_col // 128), cos_p, sin_p], [],
                      [(n_nope, BF16), (128, BF16)], "l1_rope")
    scale1 = MLA_QK ** -0.5
    ycat1, lse1 = _attn_fwd(qf, 0, qr, kvf, 0, kr, kvf, 1, B, S, S, MLA_HEADS, True, scale1, "l1_attn",
                            o_width=mix_w, kv_stride=2)
    mg1 = row(W["mem_norm_g"][1])
    ycat1, mem_saved1 = _mem_attention_fwd(proj1, qm1_col, ycat1, mem2, mg1, W["w_mem_kv"][1], B, S, "l1")
    y1, y1_t = _gate_fwd(ycat1, proj1, z1_col, "l1_gate")
    h2 = _mm(y1, W["w_out"][1], "nn", F32, "l1_out", res=h1)

    gf = row(W["final_norm_g"])
    dh2, d_gf, loss128 = _final_loss(h2, tgt, gf, "final_loss")
    G = {"final_norm_g": d_gf.reshape(-1)}
    L1 = {}

    dy1 = _mm(dh2, W["w_out"][1], "nt", F32, "l1_out_dx")
    d_wout1 = _mm(y1_t, dh2, "nn", F32, "l1_out_dw")
    d_ycat1, d_proj1 = _gate_bwd(ycat1, proj1, z1_col, dy1, "l1_gate_bwd")
    d_proj1, d_wmem1, d_mg1 = _mem_attention_bwd(proj1, qm1_col, ycat1, d_ycat1, d_proj1, mem_saved1, mem2, mg1,
                                                 W["w_mem_kv"][1], B, S, "l1")
    d_qf, d_qr, d_kvf, d_kr_heads = _attn_bwd(qf, 0, qr, kvf, 0, kr, kvf, 1, ycat1, d_ycat1, 0, lse1, B, S, S,
                                              MLA_HEADS, True, scale1, "l1_attn_bwd", kv_stride=2)
    d_qf, d_proj1 = _rowwise(_f_rope_t, [d_qr, d_kr_heads, cos_p, sin_p], [], [(n_nope, F32), (128, F32)], "l1_rope_bwd",
                             into=[(0, d_qf, 1), (1, d_proj1, kr_col // 128)])
    d_cqn = _mm(d_qf, W["mla_w_uq"], "nt", F32, "l1_uq_dx")
    L1[("mla_w_uq", None)] = _mm(cqn, d_qf, "tn", F32, "l1_uq_dw")
    d_ckvn = _mm(d_kvf, W["mla_w_ukv"], "nt", F32, "l1_ukv_dx")
    L1[("mla_w_ukv", None)] = _mm(ckvn, d_kvf, "tn", F32, "l1_ukv_dw")
    d_proj1, d_qg = _rowwise_bwd(_f_rms, [cq], [qg], [d_cqn], 1, "l1_qnorm_bwd", into=(d_proj1, cq[2]))
    d_proj1, d_kvg = _rowwise_bwd(_f_rms, [ckv], [kvg], [d_ckvn], 1, "l1_kvnorm_bwd", into=(d_proj1, ckv[2]))
    L1[("w_mem_kv", 1)] = d_wmem1
    L1[("mla_w_in", None)] = _mm(u1, d_proj1, "tn", F32, "l1_in_dw")
    L1[("w_out", 1)] = d_wout1
    W.ready("l1", L1)
    d_u1 = _mm(d_proj1, W["mla_w_in"], "nt", F32, "l1_in_dx", carry=W.carry("l1_in_dx"))
    dh1, d_g1 = _rowwise_bwd(_f_rms, [h1], [g1], [d_u1], 1, "l1_norm_bwd", add=dh2)

    dy0 = _mm(dh1, W["w_out"][0], "nt", F32, "l0_out_dx")
    d_wout0 = _mm(y0_t, dh1, "nn", F32, "l0_out_dw")
    d_ycat0, d_proj0 = _gate_bwd(ycat0, proj0, z0_col, dy0, "l0_gate_bwd", carry=W.carry("l0_gate_bwd"))
    d_proj0, d_wmem0, d_mg0 = _mem_attention_bwd(proj0, qm0_col, ycat0, d_ycat0, d_proj0, mem_saved0, mem2, mg0,
                                                 W["w_mem_kv"][0], B, S, "l0")
    W.ready("l0a", {("w_mem_kv", 0): d_wmem0, ("w_out", 0): d_wout0})
    d_cv, d_ln_g, d_ln_b = _rowwise_bwd(_f_ln_silu, [cv], [ln_g, ln_b], [(d_ycat0, conv_w, 0)], 1, "l0_ln_bwd",
                                        carry=W.carry("l0_ln_bwd"))
    d_glu, d_dw, d_dwb = _dwconv_bwd(proj0, dw, d_cv.reshape(B, S, conv_w), "l0_dwconv_bwd",
                                     carry=W.carry("l0_dwconv_bwd"))
    d_proj0 = _glu_bwd(proj0, d_glu.reshape(T, conv_w), d_proj0, "l0_glu_bwd")
    d_conv_w_in = _mm(u0, d_proj0, "tn", F32, "l0_in_dw", carry=W.carry("l0_in_dw"))
    W.ready("l0b", {("conv_w_in", None): d_conv_w_in, ("conv_dw", None): d_dw,
                    ("mla_q_norm_g", None): d_qg.reshape(-1), ("mla_kv_norm_g", None): d_kvg.reshape(-1)})
    d_u0 = _mm(d_proj0, W["conv_w_in"], "nt", F32, "l0_in_dx", carry=W.carry("l0_in_dx"))
    dx, d_g0 = _rowwise_bwd(_f_rms, [h0], [g0], [d_u0], 1, "l0_norm_bwd", add=dh1)
    dx = dx.reshape(B, S, D)

    G["norm_g"] = jnp.concatenate([d_g0, d_g1], axis=0)
    G["mem_norm_g"] = jnp.concatenate([d_mg0, d_mg1], axis=0)
    G["conv_dw_b"] = d_dwb
    G["conv_ln_g"], G["conv_ln_b"] = d_ln_g, d_ln_b
    return loss128[0, 0], dx, G


def _mla_in_perm(w):
    c2 = Q_RANK + KV_RANK
    zero = jnp.zeros((w.shape[0], HALF_ROPE), w.dtype)
    return jnp.concatenate([w[:, :c2], w[:, c2 + MLA_ROPE:], w[:, c2:c2 + HALF_ROPE], zero,
                            w[:, c2 + HALF_ROPE:c2 + MLA_ROPE], zero], axis=1)


def _mla_in_unperm(g):
    c2 = Q_RANK + KV_RANK
    r = g.shape[1] - 128
    return jnp.concatenate([g[:, :c2], g[:, r:r + HALF_ROPE], g[:, r + 64:r + 64 + HALF_ROPE], g[:, c2:r]], axis=1)


def _uq_perm(w):
    n = w.shape[0]
    w3 = w.reshape(n, MLA_HEADS, MLA_QK)
    zero = jnp.zeros((n, MLA_HEADS, HALF_ROPE), w.dtype)
    rope = jnp.concatenate([w3[:, :, MLA_NOPE:MLA_NOPE + HALF_ROPE], zero, w3[:, :, MLA_NOPE + HALF_ROPE:], zero], axis=2)
    return jnp.concatenate([w3[:, :, :MLA_NOPE].reshape(n, -1), rope.reshape(n, -1)], axis=1)


def _uq_unperm(g):
    n = g.shape[0]
    n_nope = MLA_HEADS * MLA_NOPE
    rope = g[:, n_nope:].reshape(n, MLA_HEADS, 128)
    return jnp.concatenate([g[:, :n_nope].reshape(n, MLA_HEADS, MLA_NOPE), rope[:, :, :HALF_ROPE],
                            rope[:, :, 64:64 + HALF_ROPE]], axis=2).reshape(n, -1)


_ROW_CUT = ("w_mem_kv", "w_out")
_COL_CUT = ("conv_w_in", "mla_w_in", "mla_w_uq", "mla_w_ukv", "conv_dw")
_BIG = ("w_mem_kv", "w_out", "conv_w_in", "mla_w_in", "mla_w_uq", "mla_w_ukv")
_SMALL_SHARDED = ("conv_dw", "mla_q_norm_g", "mla_kv_norm_g")
_REPLICATED = ("norm_g", "mem_norm_g", "conv_dw_b", "conv_ln_g", "conv_ln_b", "final_norm_g")
_PERM = {"mla_w_in": (_mla_in_perm, _mla_in_unperm), "mla_w_uq": (_uq_perm, _uq_unperm)}


def _join(n, blocks):
    if n in _ROW_CUT:
        _, L, r, c = blocks.shape
        return blocks.transpose(1, 0, 2, 3).reshape(L, N_DEV * r, c)
    if n in _COL_CUT:
        _, _, r, c = blocks.shape
        return blocks.reshape(N_DEV, r, c).transpose(1, 0, 2).reshape(r, N_DEV * c)
    return blocks.reshape(-1)


def _cut(n, full, shard_shape):
    if n in _ROW_CUT:
        L, r, c = shard_shape
        return full.reshape(L, N_DEV, r, c).transpose(1, 0, 2, 3)
    if n in _COL_CUT:
        _, r, c = shard_shape
        return full.reshape(r, N_DEV, c).transpose(1, 0, 2).reshape(N_DEV, 1, r, c)
    return full.reshape(N_DEV, 1, -1)


def _flat_pad(parts, size):
    flat = jnp.concatenate([p.reshape(-1) for p in parts])
    return jnp.concatenate([flat, jnp.zeros((size - flat.shape[0],), flat.dtype)])


SMALL_LANES = 128 * 8


def _as_tiles(flat_parts):
    total = sum(p.size for p in flat_parts)
    size = -(-total // SMALL_LANES) * SMALL_LANES
    return _flat_pad(flat_parts, size).reshape(8, size // 8)


def _split_flat(flat, like):
    out, o = [], 0
    for a in like:
        out.append(flat[o:o + a.size].reshape(a.shape))
        o += a.size
    return out


_HBM = pl.BlockSpec(memory_space=pltpu.HBM)
_VMEM = pl.BlockSpec(memory_space=pltpu.VMEM)


def _position():
    return lax.axis_index("x"), lax.axis_index("y"), lax.axis_index("c")


def _dma_sems(n):
    return [pltpu.SemaphoreType.DMA((n,)), pltpu.SemaphoreType.DMA((n,))]


def _run_stage(stage, name):
    n_in, n_out = len(stage.ins), len(stage.out_shapes)

    def body(*refs):
        ins, outs, sems = refs[:n_in], refs[n_in:n_in + n_out], refs[n_in + n_out:]
        stage.start(ins, outs, sems)
        stage.wait(ins, outs, sems)

    outs = _call(body, name, stage.out_shapes, in_specs=[_HBM] * n_in, out_specs=[_HBM] * n_out, scratch=stage.sems,
                 aliases=stage.aliases)(*stage.ins)
    stage.outs = list(outs)
    return stage.outs


def _gather_chips_stage(shards):
    n = len(shards)

    def copies(x_refs, out_refs, sems):
        send_sems, recv_sems, _ = sems
        x, y, c = _position()
        peers = [(x, y, 1 - c), (1 - x, y, c), (x, 1 - y, c), (1 - x, 1 - y, c)]
        out = []
        for a in range(n):
            for k, (px, py, pc) in enumerate(peers):
                send = pltpu.make_async_remote_copy(src_ref=x_refs[a], dst_ref=out_refs[a].at[4 * x + 2 * y + c],
                                                    send_sem=send_sems.at[4 * a + k], recv_sem=recv_sems.at[4 * a + k],
                                                    device_id=(px, py, pc), device_id_type=MESH)
                recv = pltpu.make_async_remote_copy(src_ref=x_refs[a], dst_ref=out_refs[a].at[4 * px + 2 * py + pc],
                                                    send_sem=send_sems.at[4 * a + k], recv_sem=recv_sems.at[4 * a + k],
                                                    device_id=(px, py, pc), device_id_type=MESH)
                out.append((send, recv))
        return out

    def local(x_refs, out_refs, sems):
        x, y, c = _position()
        return [pltpu.make_async_copy(x_refs[a], out_refs[a].at[4 * x + 2 * y + c], sems[2].at[a]) for a in range(n)]

    def start(x_refs, out_refs, sems):
        for cp in local(x_refs, out_refs, sems):
            cp.start()
        for send, _ in copies(x_refs, out_refs, sems):
            send.start()

    def wait(x_refs, out_refs, sems):
        for send, recv in copies(x_refs, out_refs, sems):
            recv.wait_recv()
            send.wait_send()
        for cp in local(x_refs, out_refs, sems):
            cp.wait()

    return _Stage(shards, [jax.ShapeDtypeStruct((N_DEV,) + a.shape, a.dtype) for a in shards],
                  _dma_sems(4 * n) + [pltpu.SemaphoreType.DMA((n,))], start, wait)


def _gather_sibling_stage(bufs):
    n = len(bufs)

    def copies(out_refs, sems):
        send_sems, recv_sems = sems
        x, y, c = _position()
        out = []
        for a in range(n):
            for j, (px, py) in enumerate([(1 - x, y), (x, 1 - y), (1 - x, 1 - y)]):
                mine, theirs = out_refs[a].at[4 * px + 2 * py + c], out_refs[a].at[4 * px + 2 * py + (1 - c)]
                send = pltpu.make_async_remote_copy(src_ref=mine, dst_ref=mine, send_sem=send_sems.at[3 * a + j],
                                                    recv_sem=recv_sems.at[3 * a + j], device_id=(x, y, 1 - c),
                                                    device_id_type=MESH)
                recv = pltpu.make_async_remote_copy(src_ref=mine, dst_ref=theirs, send_sem=send_sems.at[3 * a + j],
                                                    recv_sem=recv_sems.at[3 * a + j], device_id=(x, y, 1 - c),
                                                    device_id_type=MESH)
                out.append((send, recv))
        return out

    def start(_, out_refs, sems):
        for send, _r in copies(out_refs, sems):
            send.start()

    def wait(_, out_refs, sems):
        for send, recv in copies(out_refs, sems):
            recv.wait_recv()
            send.wait_send()

    return _Stage(bufs, [jax.ShapeDtypeStruct(b.shape, b.dtype) for b in bufs], _dma_sems(3 * n), start, wait,
                  aliases={a: a for a in range(n)})


def _all_gather_small(v, name):
    r, n = v.shape

    def body(x_ref, out_ref, send_sems, recv_sems, local_sem):
        x, y, c = _position()
        me = 4 * x + 2 * y + c
        mine = pltpu.make_async_copy(x_ref, out_ref.at[me], local_sem)
        mine.start()
        flips = [(fx, fy, fc) for fx in (0, 1) for fy in (0, 1) for fc in (0, 1)][1:]
        copies = []
        for k, (fx, fy, fc) in enumerate(flips):
            peer = (x ^ fx, y ^ fy, c ^ fc)
            cp = pltpu.make_async_remote_copy(src_ref=x_ref, dst_ref=out_ref.at[me], send_sem=send_sems.at[k],
                                              recv_sem=recv_sems.at[k], device_id=peer, device_id_type=MESH)
            cp.start()
            copies.append(cp)
        for k, (fx, fy, fc) in enumerate(flips):
            px, py, pc = x ^ fx, y ^ fy, c ^ fc
            src = out_ref.at[4 * px + 2 * py + pc]
            pltpu.make_async_remote_copy(src_ref=x_ref, dst_ref=src, send_sem=send_sems.at[k], recv_sem=recv_sems.at[k],
                                         device_id=(px, py, pc), device_id_type=MESH).wait_recv()
        for cp in copies:
            cp.wait_send()
        mine.wait()

    return _call(body, name, jax.ShapeDtypeStruct((N_DEV, r, n), v.dtype), in_specs=[_VMEM], out_specs=_VMEM,
                 scratch=_dma_sems(7) + [pltpu.SemaphoreType.DMA(())])(v)


def _reduce_sibling_stage(gs):
    n = len(gs)

    def copies(g_refs, out_refs, sems):
        send_sems, recv_sems = sems
        x, y, c = _position()
        return [pltpu.make_async_remote_copy(src_ref=g_refs[a].at[2 * k + (1 - c)], dst_ref=out_refs[a].at[k],
                                             send_sem=send_sems.at[4 * a + k], recv_sem=recv_sems.at[4 * a + k],
                                             device_id=(x, y, 1 - c), device_id_type=MESH)
                for a in range(n) for k in range(4)]

    def start(g_refs, out_refs, sems):
        for cp in copies(g_refs, out_refs, sems):
            cp.start()

    def wait(g_refs, out_refs, sems):
        for cp in copies(g_refs, out_refs, sems):
            cp.wait()

    return _Stage(gs, [jax.ShapeDtypeStruct((4,) + g.shape[1:], g.dtype) for g in gs], _dma_sems(4 * n), start, wait)


def _rows2d(shape):
    cols = shape[-1]
    rows = 1
    for s in shape[:-1]:
        rows *= s
    return rows, cols


def _add_own(g, recv, name):
    rows, cols = _rows2d(g.shape[1:])
    tr = _pick(rows, 256, 8)
    c = lax.axis_index("c").astype(jnp.int32).reshape(1)

    def body(c_ref, g_ref, r_ref, o_ref):
        o_ref[...] = (g_ref[...].astype(F32) + r_ref[...].astype(F32)).astype(o_ref.dtype)

    grid_spec = pltpu.PrefetchScalarGridSpec(
        num_scalar_prefetch=1, grid=(4, rows // tr),
        in_specs=[pl.BlockSpec((None, None, tr, cols), lambda k, i, c_ref: (k, c_ref[0], i, 0)),
                  pl.BlockSpec((None, tr, cols), lambda k, i, c_ref: (k, i, 0))],
        out_specs=pl.BlockSpec((None, tr, cols), lambda k, i, c_ref: (k, i, 0)))
    return _call(body, name, jax.ShapeDtypeStruct((4, rows, cols), g.dtype), grid_spec=grid_spec,
                 dims=("parallel", "parallel"))(c, g.reshape(4, 2, rows, cols), recv.reshape(4, rows, cols))


def _reduce_chips_stage(pas):
    n = len(pas)

    def copies(pa_refs, out_refs, sems):
        send_sems, recv_sems, _ = sems
        x, y, c = _position()
        my_chip = 2 * x + y
        out = []
        for a in range(n):
            for j, (px, py) in enumerate([(1 - x, y), (x, 1 - y), (1 - x, 1 - y)]):
                send = pltpu.make_async_remote_copy(src_ref=pa_refs[a].at[2 * px + py], dst_ref=out_refs[a].at[my_chip],
                                                    send_sem=send_sems.at[3 * a + j], recv_sem=recv_sems.at[3 * a + j],
                                                    device_id=(px, py, c), device_id_type=MESH)
                recv = pltpu.make_async_remote_copy(src_ref=pa_refs[a].at[2 * px + py], dst_ref=out_refs[a].at[2 * px + py],
                                                    send_sem=send_sems.at[3 * a + j], recv_sem=recv_sems.at[3 * a + j],
                                                    device_id=(px, py, c), device_id_type=MESH)
                out.append((send, recv))
        return out

    def local(pa_refs, out_refs, sems):
        x, y, _ = _position()
        return [pltpu.make_async_copy(pa_refs[a].at[2 * x + y], out_refs[a].at[2 * x + y], sems[2].at[a]) for a in range(n)]

    def start(pa_refs, out_refs, sems):
        for cp in local(pa_refs, out_refs, sems):
            cp.start()
        for send, _r in copies(pa_refs, out_refs, sems):
            send.start()

    def wait(pa_refs, out_refs, sems):
        for send, recv in copies(pa_refs, out_refs, sems):
            recv.wait_recv()
            send.wait_send()
        for cp in local(pa_refs, out_refs, sems):
            cp.wait()

    return _Stage(pas, [jax.ShapeDtypeStruct(pa.shape, pa.dtype) for pa in pas],
                  _dma_sems(3 * n) + [pltpu.SemaphoreType.DMA((n,))], start, wait)


def _adamw_math(w, g, m, v):
    m = ADAM_B1 * m + (1.0 - ADAM_B1) * g
    v = ADAM_B2 * v + (1.0 - ADAM_B2) * (g * g)
    m_hat = m / (1.0 - ADAM_B1 ** ADAM_STEP)
    v_hat = v / (1.0 - ADAM_B2 ** ADAM_STEP)
    delta = -ADAM_LR * (m_hat / (jnp.sqrt(v_hat) + ADAM_EPS) + ADAM_WD * w)
    return delta, m, v


def _sum_adamw(parts, w, m, v, name):
    n, rows, cols = parts.shape
    tr = _pick(rows, 128, 8)

    def body(p_ref, w_ref, m_ref, v_ref, g_ref, d_ref, nm_ref, nv_ref):
        g = p_ref[0].astype(F32)
        for k in range(1, n):
            g = g + p_ref[k].astype(F32)
        d, nm, nv = _adamw_math(w_ref[...], g, m_ref[...], v_ref[...])
        g_ref[...], d_ref[...], nm_ref[...], nv_ref[...] = g, d, nm, nv

    blk = pl.BlockSpec((tr, cols), lambda i: (i, 0))
    return _call(body, name, [jax.ShapeDtypeStruct((rows, cols), F32)] * 4, grid=(rows // tr,),
                 in_specs=[pl.BlockSpec((n, tr, cols), lambda i: (0, i, 0)), blk, blk, blk],
                 out_specs=[blk] * 4, dims=("parallel",))(parts, w, m, v)


_WEIGHTS = ("norm_g", "mem_norm_g", "w_mem_kv", "w_out", "conv_w_in", "conv_dw", "conv_dw_b", "conv_ln_g", "conv_ln_b",
            "mla_w_in", "mla_q_norm_g", "mla_w_uq", "mla_kv_norm_g", "mla_w_ukv", "final_norm_g")


_GATHER_GROUPS = {"a": ("conv_w_in",), "b": ("w_mem_kv", "w_out"), "c": ("mla_w_in", "mla_w_uq", "mla_w_ukv")}
_CARRIERS = {"l0_norm": ("gather chips", ("a",)), "l0_in": ("gather chips", ("b",)), "l0_dwconv": ("gather chips", ("c",)),
             "l0_ln": ("gather sibling", ("b",)), "l0_out": ("gather sibling", ("c",)),
             "l1_in_dx": ("reduce sibling", ("l1",)), "l0_ln_bwd": ("reduce sibling", ("l0a",)),
             "l0_gate_bwd": ("reduce chips", ("l1", 0, 2)), "l0_dwconv_bwd": ("reduce chips", ("l1", 2, 5)),
             "l0_in_dw": ("reduce chips", ("l0a",)), "l0_in_dx": ("reduce sibling alone, then chips", ("l0b",))}


class _Schedule:
    def __init__(self, w):
        self.w, self.full, self.gather, self.reduce, self.reduced = w, {}, {}, {}, {}
        small = _all_gather_small(_as_tiles([w[n] for n in _SMALL_SHARDED]), "gather_small_weights").reshape(N_DEV, -1)
        o = 0
        for n in _SMALL_SHARDED:
            self.full[n] = _join(n, small[:, o:o + w[n].size].reshape((N_DEV,) + w[n].shape))
            o += w[n].size
        for n in _REPLICATED:
            self.full[n] = w[n]

    def carry(self, call):
        kind, (g, *part) = _CARRIERS[call]
        if kind == "gather chips":
            self.gather[g] = [_gather_chips_stage([self.w[n].astype(BF16) for n in _GATHER_GROUPS[g]])]
            return self.gather[g][0]
        if kind == "gather sibling":
            self.gather[g].append(_gather_sibling_stage(self.gather[g][0].outs))
            return self.gather[g][1]
        r = self.reduce[g]
        if kind == "reduce sibling":
            r["sibling"] = _reduce_sibling_stage(r["cut"])
            return r["sibling"]
        if kind != "reduce chips":
            r["sibling"] = _reduce_sibling_stage(r["cut"])
            _run_stage(r["sibling"], "reduce_sibling_" + g)
        if "partial" not in r:
            r["partial"] = [_add_own(c, s, "reduce_add_%s_%d" % (g, i))
                            for i, (c, s) in enumerate(zip(r["cut"], r["sibling"].outs))]
        lo, hi = part if part else (0, len(r["keys"]))
        stage = _reduce_chips_stage(r["partial"][lo:hi])
        r.setdefault("chips", []).append((r["keys"][lo:hi], stage))
        return stage

    def __getitem__(self, name):
        if name not in self.full:
            g = [k for k, names in _GATHER_GROUPS.items() if name in names][0]
            if len(self.gather[g]) == 1:
                self.gather[g].append(_gather_sibling_stage(self.gather[g][0].outs))
                _run_stage(self.gather[g][1], "gather_sibling_" + g)
            for n, buf in zip(_GATHER_GROUPS[g], self.gather[g][1].outs):
                self.full[n] = _PERM[n][0](_join(n, buf)) if n in _PERM else _join(n, buf)
        return self.full[name]

    def ready(self, group, grads, payload=BF16):
        keys, cut, small = [], [], []
        for (n, layer), g in grads.items():
            if n in _SMALL_SHARDED:
                small.append(_cut(n, g, self.w[n].shape).reshape(N_DEV, -1))
                continue
            keys.append((n, layer))
            if layer is not None:
                cut.append(g.reshape((N_DEV,) + self.w[n].shape[1:]).astype(payload))
            else:
                cut.append(_cut(n, _PERM[n][1](g) if n in _PERM else g, self.w[n].shape).astype(payload))
        if small:
            keys.append(("small", None))
            cut.append(jax.vmap(lambda r: _as_tiles([r]))(jnp.concatenate(small, axis=1)))
        self.reduce[group] = {"keys": keys, "cut": cut}

    def finish(self):
        out = {}
        for r in self.reduce.values():
            for keys, stage in r["chips"]:
                out.update(dict(zip(keys, stage.outs)))
        return out


def kernel(x, mem, positions, norm_g, mem_norm_g, w_mem_kv, w_out, conv_w_in, conv_dw, conv_dw_b, conv_ln_g, conv_ln_b, mla_w_in, mla_q_norm_g, mla_w_uq, mla_kv_norm_g, mla_w_ukv, final_norm_g, loss_target, m_norm_g, m_mem_norm_g, m_w_mem_kv, m_w_out, m_conv_w_in, m_conv_dw, m_conv_dw_b, m_conv_ln_g, m_conv_ln_b, m_mla_w_in, m_mla_q_norm_g, m_mla_w_uq, m_mla_kv_norm_g, m_mla_w_ukv, m_final_norm_g, v_norm_g, v_mem_norm_g, v_w_mem_kv, v_w_out, v_conv_w_in, v_conv_dw, v_conv_dw_b, v_conv_ln_g, v_conv_ln_b, v_mla_w_in, v_mla_q_norm_g, v_mla_w_uq, v_mla_kv_norm_g, v_mla_w_ukv, v_final_norm_g):
    w = dict(zip(_WEIGHTS, (norm_g, mem_norm_g, w_mem_kv, w_out, conv_w_in, conv_dw, conv_dw_b, conv_ln_g, conv_ln_b,
                            mla_w_in, mla_q_norm_g, mla_w_uq, mla_kv_norm_g, mla_w_ukv, final_norm_g)))
    m = dict(zip(_WEIGHTS, (m_norm_g, m_mem_norm_g, m_w_mem_kv, m_w_out, m_conv_w_in, m_conv_dw, m_conv_dw_b, m_conv_ln_g,
                            m_conv_ln_b, m_mla_w_in, m_mla_q_norm_g, m_mla_w_uq, m_mla_kv_norm_g, m_mla_w_ukv, m_final_norm_g)))
    v = dict(zip(_WEIGHTS, (v_norm_g, v_mem_norm_g, v_w_mem_kv, v_w_out, v_conv_w_in, v_conv_dw, v_conv_dw_b, v_conv_ln_g,
                            v_conv_ln_b, v_mla_w_in, v_mla_q_norm_g, v_mla_w_uq, v_mla_kv_norm_g, v_mla_w_ukv, v_final_norm_g)))

    sched = _Schedule(w)
    loss_local, dx, G = _forward_backward(x, mem, positions, loss_target, sched)
    loss = lax.psum(loss_local, ("x", "y", "c"))

    from_chips = sched.finish()
    out = [{}, {}, {}, {}]
    for n in _BIG:
        if n in _ROW_CUT:
            res = [_sum_adamw(from_chips[(n, l)], w[n][l], m[n][l], v[n][l], "adamw_%s_%d" % (n, l)) for l in range(w[n].shape[0])]
            res = [jnp.stack(r) for r in zip(*res)]
        else:
            rows, cols = _rows2d(w[n].shape)
            res = _sum_adamw(from_chips[(n, None)], w[n].reshape(rows, cols), m[n].reshape(rows, cols),
                             v[n].reshape(rows, cols), "adamw_" + n)
        for o, r in zip(out, res):
            o[n] = r.reshape(w[n].shape)
    small_like = [w[n] for n in _SMALL_SHARDED]
    res = _sum_adamw(from_chips[("small", None)], _as_tiles(small_like), _as_tiles([m[n] for n in _SMALL_SHARDED]),
                     _as_tiles([v[n] for n in _SMALL_SHARDED]), "adamw_small")
    for o, r in zip(out, res):
        for n, a in zip(_SMALL_SHARDED, _split_flat(r.reshape(-1), small_like)):
            o[n] = a

    rep_like = [w[n] for n in _REPLICATED]
    rep_parts = _all_gather_small(_as_tiles([G[n] for n in _REPLICATED]), "gather_replicated_grads")
    res = _sum_adamw(rep_parts, _as_tiles(rep_like), _as_tiles([m[n] for n in _REPLICATED]),
                     _as_tiles([v[n] for n in _REPLICATED]), "adamw_replicated")
    for o, r in zip(out, res):
        for n, a in zip(_REPLICATED, _split_flat(r.reshape(-1), rep_like)):
            o[n] = a

    return (loss, dx, *[out[0][n] for n in _WEIGHTS], *[out[1][n] for n in _WEIGHTS],
            *[out[2][n] for n in _WEIGHTS], *[out[3][n] for n in _WEIGHTS])
```

```python
import jax
import jax.numpy as jnp
from jax import lax
from jax.experimental import pallas as pl
from jax.experimental.pallas import tpu as pltpu

F32 = jnp.float32
BF16 = jnp.bfloat16
MESH = pl.DeviceIdType.MESH
N_DEV = 8
VMEM_LIMIT_BYTES = 48 * 1024 * 1024

MEM_HEADS, MEM_HEAD_DIM = 4, 128
MEM_WIDTH = MEM_HEADS * MEM_HEAD_DIM
CONV_KERNEL = 31
CONV_PAD = 32
MLA_HEADS, MLA_NOPE, MLA_ROPE, MLA_V = 12, 128, 64, 128
MLA_QK = MLA_NOPE + MLA_ROPE
HALF_ROPE = MLA_ROPE // 2
Q_RANK, KV_RANK = 512, 256
ROPE_THETA = 10000.0
RMS_EPS = 1e-6
LN_EPS = 1e-5
ADAM_LR, ADAM_B1, ADAM_B2, ADAM_EPS, ADAM_WD, ADAM_STEP = 0.001, 0.9, 0.999, 1e-08, 0.01, 10
NEG = -1e30


class _Stage:
    def __init__(self, ins, out_shapes, sems, start, wait, aliases=None):
        self.ins, self.out_shapes, self.sems = list(ins), list(out_shapes), list(sems)
        self.start, self.wait, self.aliases, self.outs = start, wait, dict(aliases or {}), None


def _call(body, name, out_shape, grid=None, in_specs=None, out_specs=None, scratch=(), dims=None, grid_spec=None, aliases=None,
          carry=None):
    params = dict(vmem_limit_bytes=VMEM_LIMIT_BYTES)
    if dims is not None:
        params["dimension_semantics"] = dims
    kw = {}
    if carry is not None:
        single = not isinstance(out_shape, (list, tuple))
        main_out = [out_shape] if single else list(out_shape)
        main_specs = [out_specs] if single else list(out_specs)
        n_in, n_out, n_scr = len(in_specs), len(main_out), len(scratch)
        x_in, x_out = len(carry.ins), len(carry.out_shapes)
        inner, steps = body, tuple(grid)

        def body(*refs):
            ins, xin = refs[:n_in], refs[n_in:n_in + x_in]
            outs = refs[n_in + x_in:n_in + x_in + n_out]
            xout = refs[n_in + x_in + n_out:n_in + x_in + n_out + x_out]
            scr = refs[n_in + x_in + n_out + x_out:n_in + x_in + n_out + x_out + n_scr]
            xsem = refs[n_in + x_in + n_out + x_out + n_scr:]
            ids = [pl.program_id(a) for a in range(len(steps))]
            first, last = ids[0] == 0, ids[0] == steps[0] - 1
            for a in range(1, len(steps)):
                first = jnp.logical_and(first, ids[a] == 0)
                last = jnp.logical_and(last, ids[a] == steps[a] - 1)
            pl.when(first)(lambda: carry.start(xin, xout, xsem))
            inner(*ins, *outs, *scr)
            pl.when(last)(lambda: carry.wait(xin, xout, xsem))

        hbm = pl.BlockSpec(memory_space=pltpu.HBM)
        aliases = dict(aliases or {})
        aliases.update({n_in + k: n_out + v for k, v in carry.aliases.items()})
        res = _call(body, name, main_out + carry.out_shapes, grid=grid, in_specs=list(in_specs) + [hbm] * x_in,
                    out_specs=main_specs + [hbm] * x_out, scratch=list(scratch) + carry.sems, dims=dims, aliases=aliases)

        def run(*args):
            outs = res(*args, *carry.ins)
            carry.outs = list(outs[n_out:])
            return outs[0] if single else outs[:n_out]

        return run
    if aliases:
        kw["input_output_aliases"] = aliases
    if grid_spec is not None:
        kw["grid_spec"] = grid_spec
    else:
        if grid is not None:
            kw["grid"] = grid
        kw["in_specs"] = in_specs
        kw["out_specs"] = out_specs
        kw["scratch_shapes"] = list(scratch)
    return pl.pallas_call(body, name=name, out_shape=out_shape, compiler_params=pltpu.CompilerParams(**params), **kw)


def _pick(n, target, mult):
    best = None
    for d in range(mult, min(n, target) + 1, mult):
        if n % d == 0:
            best = d
    return n if best is None else best


_DOT_DIMS = {"nn": (((1,), (0,)), ((), ())), "nt": (((1,), (1,)), ((), ())), "tn": (((0,), (0,)), ((), ()))}


def _mm(a, b, mode, out_dtype, name, res=None, carry=None):
    if mode == "tn":
        a, mode = a.T, "nn"
    if mode == "nn":
        (M, K), N = a.shape, b.shape[1]
    else:
        (M, K), N = a.shape, b.shape[0]
    tm = _pick(M, 1024, 8)
    tn = _pick(N, 1536, 128)
    tk = _pick(K, 1536, 128)
    nk = K // tk
    has_res = res is not None

    def body(*refs):
        if has_res:
            a_ref, b_ref, r_ref, o_ref, acc = refs
        else:
            a_ref, b_ref, o_ref, acc = refs
        k = pl.program_id(2)
        part = lax.dot_general(a_ref[...].astype(BF16), b_ref[...].astype(BF16), _DOT_DIMS[mode],
                               preferred_element_type=F32)
        if nk == 1:
            o_ref[...] = (part + r_ref[...] if has_res else part).astype(o_ref.dtype)
            return

        @pl.when(k == 0)
        def _():
            acc[...] = part

        @pl.when(k > 0)
        def _():
            acc[...] += part

        @pl.when(k == nk - 1)
        def _():
            r = acc[...]
            if has_res:
                r = r + r_ref[...]
            o_ref[...] = r.astype(o_ref.dtype)

    a_spec = pl.BlockSpec((tm, tk), lambda i, j, k: (i, k))
    b_spec = {"nn": pl.BlockSpec((tk, tn), lambda i, j, k: (k, j)),
              "nt": pl.BlockSpec((tn, tk), lambda i, j, k: (j, k))}[mode]
    o_spec = pl.BlockSpec((tm, tn), lambda i, j, k: (i, j))
    in_specs = [a_spec, b_spec] + ([o_spec] if has_res else [])
    args = (a, b) + ((res,) if has_res else ())
    return _call(body, name, jax.ShapeDtypeStruct((M, N), out_dtype), grid=(M // tm, N // tn, nk),
                 in_specs=in_specs, out_specs=o_spec, scratch=[pltpu.VMEM((tm, tn), F32)],
                 dims=("parallel", "parallel", "arbitrary"), carry=carry)(*args)


def _views(rows):
    return [r if isinstance(r, tuple) else (r, r.shape[1], 0) for r in rows]


def _row_tile(T, rows):
    return min(T, 512 if max(w for _, w, _ in rows) <= 1024 else 256)


def _rowwise(f, rows, params, outs, name, carry=None, into=None):
    rows = _views(rows)
    T = rows[0][0].shape[0]
    tb = _row_tile(T, rows)
    nr, npar = len(rows), len(params)
    outs = [o if len(o) == 3 else (o[0], o[1], o[0]) for o in outs]
    into = into or []

    def body(*refs):
        vals = f(*[r[...].astype(F32) for r in refs[:nr]], *[p[...] for p in refs[nr:nr + npar]])
        for o_ref, v in zip(refs[nr + npar + len(into):], vals):
            o_ref[...] = v.astype(o_ref.dtype)

    row_spec = lambda w, cb=0: pl.BlockSpec((tb, w), lambda i: (i, cb))
    par_spec = lambda w: pl.BlockSpec((1, w), lambda i: (0, 0))
    out_shape = [jax.ShapeDtypeStruct((T, tw), dt) for _, dt, tw in outs]
    out_specs = [row_spec(w) for w, _, _ in outs]
    in_specs = [row_spec(w, cb) for _, w, cb in rows] + [par_spec(p.shape[1]) for p in params]
    args = [r[0] for r in rows] + list(params)
    aliases = {}
    for k, arr, cb in into:
        aliases[len(args)] = k
        in_specs.append(pl.BlockSpec(memory_space=pl.ANY))
        args.append(arr)
        out_shape[k] = jax.ShapeDtypeStruct(arr.shape, arr.dtype)
        out_specs[k] = row_spec(outs[k][0], cb)
    return _call(body, name, out_shape, grid=(T // tb,), in_specs=in_specs, out_specs=out_specs, dims=("parallel",),
                 carry=carry, aliases=aliases)(*args)


def _rowwise_bwd(f, rows, params, douts, n_diff, name, carry=None, add=None, into=None):
    rows, douts = _views(rows), _views(douts)
    T = rows[0][0].shape[0]
    tb = _row_tile(T, rows)
    nr, npar, nd = len(rows), len(params), len(douts)
    n_add = 0 if add is None else 1

    def body(*refs):
        rv = [r[...].astype(F32) for r in refs[:nr]]
        pv = [p[...] for p in refs[nr:nr + npar]]
        dv = [d[...].astype(F32) for d in refs[nr + npar:nr + npar + nd]]
        o_refs = refs[nr + npar + nd + n_add + (0 if into is None else 1):]
        fixed = rv[n_diff:]

        def g(*xs):
            return tuple(f(*xs[:n_diff], *fixed, *xs[n_diff:]))

        _, vjp = jax.vjp(g, *rv[:n_diff], *pv)
        grads = list(vjp(tuple(dv)))
        if add is not None:
            grads[0] = grads[0] + refs[nr + npar + nd][...]
        for o_ref, gr in zip(o_refs[:n_diff], grads[:n_diff]):
            o_ref[...] = gr.astype(o_ref.dtype)
        first = pl.program_id(0) == 0
        for o_ref, gr in zip(o_refs[n_diff:], grads[n_diff:]):
            @pl.when(first)
            def _(o_ref=o_ref):
                o_ref[...] = jnp.zeros_like(o_ref)

            o_ref[...] += gr

    row_spec = lambda w, cb=0: pl.BlockSpec((tb, w), lambda i: (i, cb))
    par_spec = lambda w: pl.BlockSpec((1, w), lambda i: (0, 0))
    out_shape = ([jax.ShapeDtypeStruct((T, w), F32) for _, w, _ in rows[:n_diff]]
                 + [jax.ShapeDtypeStruct((1, p.shape[1]), F32) for p in params])
    out_specs = [row_spec(w) for _, w, _ in rows[:n_diff]] + [par_spec(p.shape[1]) for p in params]
    in_specs = ([row_spec(w, cb) for _, w, cb in rows] + [par_spec(p.shape[1]) for p in params]
                + [row_spec(w, cb) for _, w, cb in douts])
    args = [r[0] for r in rows] + list(params) + [d[0] for d in douts]
    aliases = None
    if add is not None:
        in_specs.append(row_spec(add.shape[1]))
        args.append(add)
    if into is not None:
        aliases = {len(args): 0}
        in_specs.append(pl.BlockSpec(memory_space=pl.ANY))
        args.append(into[0])
        out_shape[0] = jax.ShapeDtypeStruct(into[0].shape, into[0].dtype)
        out_specs[0] = row_spec(rows[0][1], into[1])
    return _call(body, name, out_shape, grid=(T // tb,), in_specs=in_specs, out_specs=out_specs,
                 dims=("arbitrary",), carry=carry, aliases=aliases)(*args)


def _sig(x):
    return 1.0 / (1.0 + jnp.exp(-x))


def _rms(x, g):
    return x * lax.rsqrt(jnp.mean(x * x, axis=-1, keepdims=True) + RMS_EPS) * g


def _f_rms(x, g):
    return (_rms(x, g),)


def _f_ln_silu(x, g, b):
    mu = jnp.mean(x, axis=-1, keepdims=True)
    xc = x - mu
    var = jnp.mean(xc * xc, axis=-1, keepdims=True)
    y = xc * lax.rsqrt(var + LN_EPS) * g + b
    return (y * _sig(y),)


def _rope128(x, cos_p, sin_p):
    return x * cos_p + pltpu.roll(x, 64, 1) * sin_p


def _rope128_t(d, cos_p, sin_p):
    return d * cos_p + pltpu.roll(d * sin_p, 64, 1)


def _f_rope(xq, xk, cos_p, sin_p):
    heads = [_rope128(xq[:, h * 128:(h + 1) * 128], cos_p, sin_p) for h in range(MLA_HEADS)]
    return (jnp.concatenate(heads, axis=1), _rope128(xk, cos_p, sin_p))


def _f_rope_t(dq, dk_heads, cos_p, sin_p):
    heads = [_rope128_t(dq[:, h * 128:(h + 1) * 128], cos_p, sin_p) for h in range(MLA_HEADS)]
    dk = dk_heads[:, 0:128]
    for h in range(1, MLA_HEADS):
        dk = dk + dk_heads[:, h * 128:(h + 1) * 128]
    return (jnp.concatenate(heads, axis=1), _rope128_t(dk, cos_p, sin_p))


GATE_LANES = 512


def _gate_fwd(ycat, proj, z_col, name, tb=1024):
    T, width = ycat.shape
    zb = z_col // GATE_LANES

    def body(y_ref, z_ref, o_ref, ot_ref):
        z = z_ref[...]
        y = y_ref[...] * (z * _sig(z))
        o_ref[...] = y.astype(o_ref.dtype)
        ot_ref[...] = y.T.astype(ot_ref.dtype)

    blk = pl.BlockSpec((tb, GATE_LANES), lambda i, c: (i, c))
    return _call(body, name, [jax.ShapeDtypeStruct((T, width), BF16), jax.ShapeDtypeStruct((width, T), BF16)],
                 grid=(T // tb, width // GATE_LANES),
                 in_specs=[blk, pl.BlockSpec((tb, GATE_LANES), lambda i, c: (i, zb + c))],
                 out_specs=[blk, pl.BlockSpec((GATE_LANES, tb), lambda i, c: (c, i))],
                 dims=("parallel", "parallel"))(ycat, proj)


def _gate_bwd(ycat, proj, z_col, dy, name, tb=1024, carry=None):
    T, width = ycat.shape
    zb = z_col // GATE_LANES

    def body(y_ref, z_ref, dy_ref, dycat_ref, dz_ref):
        z, d = z_ref[...], dy_ref[...]
        s = _sig(z)
        dycat_ref[...] = d * (z * s)
        dz_ref[...] = (d * y_ref[...] * (s * (1.0 + z * (1.0 - s)))).astype(dz_ref.dtype)

    blk = pl.BlockSpec((tb, GATE_LANES), lambda i, c: (i, c))
    zblk = pl.BlockSpec((tb, GATE_LANES), lambda i, c: (i, zb + c))
    return _call(body, name, [jax.ShapeDtypeStruct((T, width), F32), jax.ShapeDtypeStruct(proj.shape, BF16)],
                 grid=(T // tb, width // GATE_LANES), in_specs=[blk, zblk, blk], out_specs=[blk, zblk],
                 dims=("parallel", "parallel"), carry=carry)(ycat, proj, dy)


def _glu_bwd(proj, d_glu, d_proj, name, tb=256):
    T, w = d_glu.shape

    def body(a_ref, g_ref, d_ref, _, o_ref):
        s, d = _sig(g_ref[...]), d_ref[...]
        o_ref[:, 0:w] = (d * s).astype(o_ref.dtype)
        o_ref[:, w:2 * w] = (d * a_ref[...] * (s * (1.0 - s))).astype(o_ref.dtype)

    return _call(body, name, jax.ShapeDtypeStruct(d_proj.shape, d_proj.dtype), grid=(T // tb,),
                 in_specs=[pl.BlockSpec((tb, w), lambda i: (i, 0)), pl.BlockSpec((tb, w), lambda i: (i, 1)),
                           pl.BlockSpec((tb, w), lambda i: (i, 0)), pl.BlockSpec(memory_space=pl.ANY)],
                 out_specs=pl.BlockSpec((tb, 2 * w), lambda i: (i, 0)), dims=("parallel",),
                 aliases={3: 0})(proj, proj, d_glu, d_proj)


def _final_loss(h, tgt, g, name, tb=512):
    T, D = h.shape

    def body(h_ref, t_ref, g_ref, dh_ref, dg_ref, loss_ref):
        tv = t_ref[...]

        def rowloss(hh, gg):
            e = _rms(hh, gg) - tv
            return 0.5 * jnp.mean(e * e, axis=-1, keepdims=True)

        lr, vjp = jax.vjp(rowloss, h_ref[...], g_ref[...])
        dh, dg = vjp(jnp.ones_like(lr))
        dh_ref[...] = dh

        @pl.when(pl.program_id(0) == 0)
        def _():
            dg_ref[...] = jnp.zeros_like(dg_ref)
            loss_ref[...] = jnp.zeros_like(loss_ref)

        dg_ref[...] += dg
        loss_ref[...] += jnp.broadcast_to(jnp.sum(lr, axis=0, keepdims=True), loss_ref.shape)

    row = pl.BlockSpec((tb, D), lambda i: (i, 0))
    par = pl.BlockSpec((1, D), lambda i: (0, 0))
    return _call(body, name,
                 [jax.ShapeDtypeStruct((T, D), F32), jax.ShapeDtypeStruct((1, D), F32), jax.ShapeDtypeStruct((1, 128), F32)],
                 grid=(T // tb,), in_specs=[row, row, par],
                 out_specs=[row, par, pl.BlockSpec((1, 128), lambda i: (0, 0))], dims=("arbitrary",))(h, tgt, g)


CONV_ROWS = 128
CONV_LANES = 256


def _sublane_phases(pad, n):
    for r in range(1, 8):
        for c0 in range(0, n - 8, 256):
            rows = min(256, n - 8 - c0)
            pad[r, c0:c0 + rows, :] = pad[0, c0 + r:c0 + r + rows, :]


def _dwconv_fwd(proj, C, w, b, B, S, name, carry=None):
    cb = CONV_LANES
    off = CONV_PAD - (CONV_KERNEL - 1)

    def body(a_ref, g_ref, w_ref, b_ref, o_ref, pad):
        pad[0, 0:CONV_PAD, :] = jnp.zeros((CONV_PAD, cb), F32)
        for c0 in range(0, S, 256):
            pad[0, CONV_PAD + c0:CONV_PAD + c0 + 256, :] = a_ref[c0:c0 + 256, :] * _sig(g_ref[c0:c0 + 256, :])
        _sublane_phases(pad, S + CONV_PAD)
        for t0 in range(0, S, CONV_ROWS):
            acc = jnp.broadcast_to(b_ref[...], (CONV_ROWS, cb))
            for k in range(CONV_KERNEL):
                r, base = (off + k) % 8, t0 + (off + k) // 8 * 8
                acc = acc + w_ref[k:k + 1, :] * pad[r, base:base + CONV_ROWS, :]
            o_ref[t0:t0 + CONV_ROWS, :] = acc

    return _call(body, name, jax.ShapeDtypeStruct((B, S, C), F32), grid=(B, C // cb),
                 in_specs=[pl.BlockSpec((S, cb), lambda i, j: (i, j)), pl.BlockSpec((S, cb), lambda i, j: (i, C // cb + j)),
                           pl.BlockSpec((CONV_KERNEL, cb), lambda i, j: (0, j)),
                           pl.BlockSpec((1, cb), lambda i, j: (0, j))],
                 out_specs=pl.BlockSpec((None, S, cb), lambda i, j: (i, 0, j)),
                 scratch=[pltpu.VMEM((8, S + CONV_PAD, cb), F32)], dims=("parallel", "parallel"),
                 carry=carry)(proj, proj, w, b)


def _dwconv_bwd(proj, w, dy, name, carry=None):
    B, S, C = dy.shape
    cb = CONV_LANES
    groups = CONV_ROWS // 8

    def body(a_ref, g_ref, w_ref, dy_ref, dx_ref, dw_ref, db_ref, dypad, wacc):
        dypad[0, 0:S, :] = dy_ref[...]
        dypad[0, S:, :] = jnp.zeros((CONV_PAD, cb), F32)
        _sublane_phases(dypad, S + CONV_PAD)
        wacc[...] = jnp.zeros_like(wacc)
        for t0 in range(0, S, CONV_ROWS):
            xc = a_ref[t0:t0 + CONV_ROWS, :] * _sig(g_ref[t0:t0 + CONV_ROWS, :])
            acc = jnp.zeros((CONV_ROWS, cb), F32)
            for k in range(CONV_KERNEL):
                o = (CONV_KERNEL - 1) - k
                dys = dypad[o % 8, t0 + o // 8 * 8:t0 + o // 8 * 8 + CONV_ROWS, :]
                acc = acc + w_ref[k:k + 1, :] * dys
                wacc[k] += jnp.sum((dys * xc).reshape(groups, 8, cb), axis=0)
            wacc[CONV_KERNEL] += jnp.sum(dy_ref[t0:t0 + CONV_ROWS, :].reshape(groups, 8, cb), axis=0)
            dx_ref[t0:t0 + CONV_ROWS, :] = acc

        @pl.when(pl.program_id(1) == 0)
        def _():
            dw_ref[...] = jnp.zeros_like(dw_ref)
            db_ref[...] = jnp.zeros_like(db_ref)

        for k in range(CONV_KERNEL):
            dw_ref[k:k + 1, :] += jnp.sum(wacc[k], axis=0, keepdims=True)
        db_ref[...] += jnp.sum(wacc[CONV_KERNEL], axis=0, keepdims=True)

    blk = pl.BlockSpec((None, S, cb), lambda j, i: (i, 0, j))
    return _call(body, name,
                 [jax.ShapeDtypeStruct((B, S, C), F32), jax.ShapeDtypeStruct((CONV_KERNEL, C), F32),
                  jax.ShapeDtypeStruct((1, C), F32)],
                 grid=(C // cb, B),
                 in_specs=[pl.BlockSpec((S, cb), lambda j, i: (i, j)), pl.BlockSpec((S, cb), lambda j, i: (i, C // cb + j)),
                           pl.BlockSpec((CONV_KERNEL, cb), lambda j, i: (0, j)), blk],
                 out_specs=[blk, pl.BlockSpec((CONV_KERNEL, cb), lambda j, i: (0, j)),
                            pl.BlockSpec((1, cb), lambda j, i: (0, j))],
                 scratch=[pltpu.VMEM((8, S + CONV_PAD, cb), F32), pltpu.VMEM((CONV_KERNEL + 1, 8, cb), F32)],
                 dims=("parallel", "arbitrary"), carry=carry)(proj, proj, w, dy)


ATTN_TILE = {"fwd": 1024, "bwd": 1024, "cross fwd": 512}
ATTN_SUB = {"fwd": 256, "bwd": 512}


def _attn_shapes(Sq, Sk, causal, pass_):
    tq = min(Sq, ATTN_TILE[pass_ if causal or pass_ == "bwd" else "cross fwd"])
    tk = tq if causal else min(Sk, ATTN_TILE[pass_])
    return tq, tk, min(ATTN_SUB[pass_], tq)


def _causal_bias(n):
    r = lax.broadcasted_iota(jnp.int32, (n, n), 0)
    c = lax.broadcasted_iota(jnp.int32, (n, n), 1)
    return jnp.where(c <= r, 0.0, NEG).astype(F32)


def _mask_diagonal(s, bias):
    n, nc = s.shape
    if nc == n:
        return s + bias
    return jnp.concatenate([s[:, :nc - n], s[:, nc - n:] + bias], axis=1)


def _attn_fwd(q, q_c0, qr, k, k_c0, kr, v, v_c0, B, Sq, Sk, H, causal, scale, name, into=None, o_c0=0, o_width=None,
              kv_stride=1):
    tq, tk, sub = _attn_shapes(Sq, Sk, causal, "fwd")
    nq, nk, nsub = Sq // tq, Sk // tk, tq // sub
    rope = qr is not None

    def body(*refs):
        refs = list(refs)
        qn_ref = refs.pop(0)
        qr_ref = refs.pop(0) if rope else None
        kn_ref = refs.pop(0)
        kr_ref = refs.pop(0) if rope else None
        v_ref = refs.pop(0)
        if into is not None:
            refs.pop(0)
        o_ref, lse_ref, m_s, l_s, acc = refs
        qi = pl.program_id(2)
        m_s[...] = jnp.full_like(m_s, NEG)
        l_s[...] = jnp.zeros_like(l_s)
        acc[...] = jnp.zeros_like(acc)
        bias = _causal_bias(sub) if causal else None
        qs = []
        for r in range(nsub):
            qn = qn_ref[r * sub:(r + 1) * sub, :].astype(BF16)
            qs.append(jnp.concatenate([qn, qr_ref[r * sub:(r + 1) * sub, :]], axis=1) if rope else qn)

        def step(j, masked):
            ks = pl.ds(pl.multiple_of(j * tk, tk), tk)
            kk = jnp.concatenate([kn_ref[ks, :], kr_ref[ks, :]], axis=1) if rope else kn_ref[ks, :]
            vv = v_ref[ks, :]
            for r in range(nsub):
                rows = slice(r * sub, (r + 1) * sub)
                nc = (r + 1) * sub if masked else tk
                s = lax.dot_general(qs[r], kk[:nc], _DOT_DIMS["nt"], preferred_element_type=F32) * scale
                if masked:
                    s = _mask_diagonal(s, bias)
                m_old = m_s[rows, :]
                m_new = jnp.maximum(m_old, jnp.max(s, axis=-1, keepdims=True))
                p = jnp.exp(s - m_new)
                alpha = jnp.exp(m_old - m_new)
                l_s[rows, :] = alpha * l_s[rows, :] + jnp.sum(p, axis=-1, keepdims=True)
                acc[rows, :] = alpha * acc[rows, :] + jnp.dot(p.astype(BF16), vv[:nc], preferred_element_type=F32)
                m_s[rows, :] = m_new

        def unmasked(j, carry):
            step(j, False)
            return carry

        if causal:
            lax.fori_loop(0, qi, unmasked, 0)
            step(qi, True)
        else:
            lax.fori_loop(0, nk, unmasked, 0)
        o_ref[...] = (acc[...] / l_s[...]).astype(o_ref.dtype)
        lse_ref[...] = m_s[...] + jnp.log(l_s[...])

    qspec = lambda c0: pl.BlockSpec((tq, 128), lambda b, h, i: (b * nq + i, c0 + h))
    kspec = lambda c0: pl.BlockSpec((Sk, 128), lambda b, h, i: (b, c0 + kv_stride * h))
    in_specs, args = [qspec(q_c0)], [q]
    if rope:
        in_specs.append(qspec(0)); args.append(qr)
    in_specs.append(kspec(k_c0)); args.append(k)
    if rope:
        in_specs.append(pl.BlockSpec((Sk, 128), lambda b, h, i: (b, 0))); args.append(kr)
    in_specs.append(kspec(v_c0)); args.append(v)
    aliases = {}
    if into is not None:
        aliases = {len(args): 0}
        in_specs.append(pl.BlockSpec(memory_space=pl.ANY)); args.append(into)
        o_shape = jax.ShapeDtypeStruct(into.shape, into.dtype)
    else:
        o_shape = jax.ShapeDtypeStruct((B * Sq, o_width), F32)
    return _call(body, name, [o_shape, jax.ShapeDtypeStruct((B * H, Sq, 1), F32)], grid=(B, H, nq), in_specs=in_specs,
                 out_specs=[qspec(o_c0), pl.BlockSpec((None, tq, 1), lambda b, h, i: (b * H + h, i, 0))],
                 scratch=[pltpu.VMEM((tq, 1), F32), pltpu.VMEM((tq, 1), F32), pltpu.VMEM((tq, 128), F32)],
                 dims=("parallel", "parallel", "arbitrary"), aliases=aliases)(*args)


def _attn_bwd(q, q_c0, qr, k, k_c0, kr, v, v_c0, o, do, o_c0, lse, B, Sq, Sk, H, causal, scale, name, dq_into=None,
              kv_stride=1):
    tq, tk, sub = _attn_shapes(Sq, Sk, causal, "bwd")
    nq, nk, nsub = Sq // tq, Sk // tk, tq // sub
    rope = qr is not None
    dk_w = 256 if rope else 128

    def body(*refs):
        refs = list(refs)
        qn_ref = refs.pop(0)
        qr_ref = refs.pop(0) if rope else None
        kn_ref = refs.pop(0)
        kr_ref = refs.pop(0) if rope else None
        v_ref, o_ref, do_ref, lse_ref = refs[:4]
        refs = refs[4 + (0 if dq_into is None else 1):]
        dqn_ref = refs.pop(0)
        dqr_ref = refs.pop(0) if rope else None
        dkn_ref = refs.pop(0)
        dkr_ref = refs.pop(0) if rope else None
        dv_ref = None if rope else refs.pop(0)
        q_s, do_s, dl_s, dq_acc, dk_acc, dv_acc = refs
        kj = pl.program_id(2)

        @pl.when(kj == 0)
        def _():
            qn = qn_ref[...].astype(BF16)
            q_s[...] = jnp.concatenate([qn, qr_ref[...]], axis=1) if rope else qn
            dof = do_ref[...]
            do_s[...] = dof.astype(BF16)
            dl_s[...] = jnp.sum(dof * o_ref[...], axis=-1, keepdims=True)
            dq_acc[...] = jnp.zeros_like(dq_acc)

        kk = jnp.concatenate([kn_ref[...], kr_ref[...]], axis=1) if rope else kn_ref[...]
        vv = v_ref[...]
        bias = _causal_bias(sub) if causal else None
        dk_acc[...] = jnp.zeros_like(dk_acc)
        dv_acc[...] = jnp.zeros_like(dv_acc)

        def step(i, masked):
            for r in range(nsub):
                rows = pl.ds(pl.multiple_of(i * tq + r * sub, sub), sub)
                qq, dob = q_s[rows, :], do_s[rows, :]
                nc = (r + 1) * sub if masked else tk
                kc, vc = kk[:nc], vv[:nc]
                s = lax.dot_general(qq, kc, _DOT_DIMS["nt"], preferred_element_type=F32) * scale
                if masked:
                    s = _mask_diagonal(s, bias)
                p = jnp.exp(s - lse_ref[rows, :])
                dp = lax.dot_general(dob, vc, _DOT_DIMS["nt"], preferred_element_type=F32)
                ds = (p * (dp - dl_s[rows, :]) * scale).astype(BF16)
                dv_acc[0:nc, :] += lax.dot_general(p.astype(BF16), dob, _DOT_DIMS["tn"], preferred_element_type=F32)
                dk_acc[0:nc, :] += lax.dot_general(ds, qq, _DOT_DIMS["tn"], preferred_element_type=F32)
                dq_acc[rows, :] += jnp.dot(ds, kc, preferred_element_type=F32)

        def unmasked(i, carry):
            step(i, False)
            return carry

        if causal:
            step(kj, True)
            lax.fori_loop(kj + 1, nq, unmasked, 0)
        else:
            lax.fori_loop(0, nq, unmasked, 0)
        if rope:
            dkn_ref[...] = jnp.concatenate([dk_acc[:, 0:128], dv_acc[...]], axis=1).astype(dkn_ref.dtype)
            dkr_ref[...] = dk_acc[:, 128:256]
        else:
            dkn_ref[...] = dk_acc[...]
            dv_ref[...] = dv_acc[...]

        @pl.when(kj == nk - 1)
        def _():
            dqn_ref[...] = dq_acc[:, 0:128].astype(dqn_ref.dtype)
            if rope:
                dqr_ref[...] = dq_acc[:, 128:256]

    qspec = lambda c0: pl.BlockSpec((Sq, 128), lambda b, h, j: (b, c0 + h))
    kspec = lambda c0: pl.BlockSpec((tk, 128), lambda b, h, j: (b * nk + j, c0 + kv_stride * h))
    in_specs, args = [qspec(q_c0)], [q]
    if rope:
        in_specs.append(qspec(0)); args.append(qr)
    in_specs.append(kspec(k_c0)); args.append(k)
    if rope:
        in_specs.append(pl.BlockSpec((tk, 128), lambda b, h, j: (b * nk + j, 0))); args.append(kr)
    in_specs += [kspec(v_c0), qspec(o_c0), qspec(o_c0), pl.BlockSpec((None, Sq, 1), lambda b, h, j: (b * H + h, 0, 0))]
    args += [v, o, do, lse]
    h_rows_q = jax.ShapeDtypeStruct((B * Sq, H * 128), F32)
    h_rows_k = jax.ShapeDtypeStruct((B * Sk, H * 128), F32)
    out_shape, out_specs, aliases = [h_rows_q], [qspec(0)], None
    if rope:
        out_shape = [jax.ShapeDtypeStruct((B * Sq, 2 * H * 128), BF16)]
    if dq_into is not None:
        aliases = {len(args): 0}
        in_specs.append(pl.BlockSpec(memory_space=pl.ANY)); args.append(dq_into[0])
        out_shape, out_specs = [jax.ShapeDtypeStruct(dq_into[0].shape, dq_into[0].dtype)], [qspec(dq_into[1])]
    if rope:
        out_shape.append(h_rows_q); out_specs.append(qspec(0))
    hspec = lambda w: pl.BlockSpec((tk, w), lambda b, h, j: (b * nk + j, h))
    if rope:
        out_shape += [jax.ShapeDtypeStruct((B * Sk, H * 256), BF16), h_rows_k]
        out_specs += [hspec(256), hspec(128)]
    else:
        out_shape += [h_rows_k, h_rows_k]
        out_specs += [hspec(128), hspec(128)]
    return _call(body, name, out_shape, grid=(B, H, nk), in_specs=in_specs, out_specs=out_specs,
                 scratch=[pltpu.VMEM((Sq, dk_w), BF16), pltpu.VMEM((Sq, 128), BF16), pltpu.VMEM((Sq, 1), F32),
                          pltpu.VMEM((Sq, dk_w), F32), pltpu.VMEM((tk, dk_w), F32), pltpu.VMEM((tk, 128), F32)],
                 dims=("parallel", "parallel", "arbitrary"), aliases=aliases)(*args)


def _mem_attention_fwd(proj, q_col, ycat, mem2, mem_g, w_mem, B, S, tag):
    M = mem2.shape[0] // B
    (memn,) = _rowwise(_f_rms, [mem2], [mem_g], [(mem2.shape[1], BF16)], tag + "_memnorm")
    kvm = _mm(memn, w_mem, "nn", BF16, tag + "_memkv")
    o_c0 = ycat.shape[1] // 128 - MEM_HEADS
    ycat, lse = _attn_fwd(proj, q_col // 128, None, kvm, 0, None, kvm, MEM_HEADS, B, S, M, MEM_HEADS, False,
                          MEM_HEAD_DIM ** -0.5, tag + "_memattn", into=ycat, o_c0=o_c0)
    return ycat, (memn, kvm, lse)


def _mem_attention_bwd(proj, q_col, ycat, d_ycat, d_proj, saved, mem2, mem_g, w_mem, B, S, tag):
    memn, kvm, lse = saved
    M = mem2.shape[0] // B
    o_c0 = ycat.shape[1] // 128 - MEM_HEADS
    d_q, d_k, d_v = _attn_bwd(proj, q_col // 128, None, kvm, 0, None, kvm, MEM_HEADS, ycat, d_ycat, o_c0, lse, B, S, M,
                              MEM_HEADS, False, MEM_HEAD_DIM ** -0.5, tag + "_memattn_bwd", dq_into=(d_proj, q_col // 128))
    d_kvm = jnp.concatenate([d_k, d_v], axis=1).astype(BF16)
    d_w_mem = _mm(memn, d_kvm, "tn", F32, tag + "_memkv_dw")
    d_memn = _mm(d_kvm, w_mem, "nt", F32, tag + "_memkv_dx")
    _, d_mem_g = _rowwise_bwd(_f_rms, [mem2], [mem_g], [d_memn], 1, tag + "_memnorm_bwd")
    return d_q, d_w_mem, d_mem_g


def _rope_tables(positions):
    inv_freq = 1.0 / (ROPE_THETA ** (jnp.arange(0, MLA_ROPE, 2, dtype=F32) / MLA_ROPE))
    ang = positions.astype(F32).reshape(-1, 1) * inv_freq
    cos, sin, zero = jnp.cos(ang), jnp.sin(ang), jnp.zeros_like(ang)
    return jnp.concatenate([cos, zero, cos, zero], axis=1), jnp.concatenate([-sin, zero, sin, zero], axis=1)


def _forward_backward(x, mem, positions, target, W):
    B, S, D = x.shape
    T = B * S
    conv_w = W["conv_dw"].shape[1]
    mix_w = 2 * D
    h0 = x.reshape(T, D)
    mem2 = mem.reshape(-1, D)
    tgt = target.reshape(T, D)
    row = lambda v: v.reshape(1, -1)
    n_nope = MLA_HEADS * MLA_NOPE

    g0 = row(W["norm_g"][0])
    (u0,) = _rowwise(_f_rms, [h0], [g0], [(D, BF16)], "l0_norm", carry=W.carry("l0_norm"))
    proj0 = _mm(u0, W["conv_w_in"], "nn", F32, "l0_in", carry=W.carry("l0_in"))
    qm0_col, z0_col = 2 * conv_w, 2 * conv_w + MEM_WIDTH
    dw, dwb = W["conv_dw"], row(W["conv_dw_b"][0])
    cv = _dwconv_fwd(proj0, conv_w, dw, dwb, B, S, "l0_dwconv", carry=W.carry("l0_dwconv")).reshape(T, conv_w)
    ln_g, ln_b = row(W["conv_ln_g"][0]), row(W["conv_ln_b"][0])
    (ycat0,) = _rowwise(_f_ln_silu, [cv], [ln_g, ln_b], [(conv_w, F32, mix_w)], "l0_ln", carry=W.carry("l0_ln"))
    mg0 = row(W["mem_norm_g"][0])
    ycat0, mem_saved0 = _mem_attention_fwd(proj0, qm0_col, ycat0, mem2, mg0, W["w_mem_kv"][0], B, S, "l0")
    y0, y0_t = _gate_fwd(ycat0, proj0, z0_col, "l0_gate")
    h1 = _mm(y0, W["w_out"][0], "nn", F32, "l0_out", res=h0, carry=W.carry("l0_out"))

    g1 = row(W["norm_g"][1])
    (u1,) = _rowwise(_f_rms, [h1], [g1], [(D, BF16)], "l1_norm")
    proj1 = _mm(u1, W["mla_w_in"], "nn", F32, "l1_in")
    z1_col = Q_RANK
    qm1_col = z1_col + mix_w
    ckv_col = qm1_col + MEM_WIDTH
    kr_col = ckv_col + KV_RANK
    cq, ckv = (proj1, Q_RANK, 0), (proj1, KV_RANK, ckv_col // KV_RANK)
    qg, kvg = row(W["mla_q_norm_g"]), row(W["mla_kv_norm_g"])
    (cqn,) = _rowwise(_f_rms, [cq], [qg], [(Q_RANK, BF16)], "l1_qnorm")
    (ckvn,) = _rowwise(_f_rms, [ckv], [kvg], [(KV_RANK, BF16)], "l1_kvnorm")
    qf = _mm(cqn, W["mla_w_uq"], "nn", F32, "l1_uq")
    kvf = _mm(ckvn, W["mla_w_ukv"], "nn", BF16, "l1_ukv")
    cos_p, sin_p = _rope_tables(positions)
    qr, kr = _rowwise(_f_rope, [(qf, n_nope, 1), (proj1, 128, kr_col // 128), cos_p, sin_p], [],
                      [(n_nope, BF16), (128, BF16)], "l1_rope")
    scale1 = MLA_QK ** -0.5
    ycat1, lse1 = _attn_fwd(qf, 0, qr, kvf, 0, kr, kvf, 1, B, S, S, MLA_HEADS, True, scale1, "l1_attn",
                            o_width=mix_w, kv_stride=2)
    mg1 = row(W["mem_norm_g"][1])
    ycat1, mem_saved1 = _mem_attention_fwd(proj1, qm1_col, ycat1, mem2, mg1, W["w_mem_kv"][1], B, S, "l1")
    y1, y1_t = _gate_fwd(ycat1, proj1, z1_col, "l1_gate")
    h2 = _mm(y1, W["w_out"][1], "nn", F32, "l1_out", res=h1)

    gf = row(W["final_norm_g"])
    dh2, d_gf, loss128 = _final_loss(h2, tgt, gf, "final_loss")
    G = {"final_norm_g": d_gf.reshape(-1)}
    L1 = {}

    dy1 = _mm(dh2, W["w_out"][1], "nt", F32, "l1_out_dx")
    d_wout1 = _mm(y1_t, dh2, "nn", F32, "l1_out_dw")
    d_ycat1, d_proj1 = _gate_bwd(ycat1, proj1, z1_col, dy1, "l1_gate_bwd")
    d_proj1, d_wmem1, d_mg1 = _mem_attention_bwd(proj1, qm1_col, ycat1, d_ycat1, d_proj1, mem_saved1, mem2, mg1,
                                                 W["w_mem_kv"][1], B, S, "l1")
    d_qf, d_qr, d_kvf, d_kr_heads = _attn_bwd(qf, 0, qr, kvf, 0, kr, kvf, 1, ycat1, d_ycat1, 0, lse1, B, S, S,
                                              MLA_HEADS, True, scale1, "l1_attn_bwd", kv_stride=2)
    d_qf, d_proj1 = _rowwise(_f_rope_t, [d_qr, d_kr_heads, cos_p, sin_p], [], [(n_nope, F32), (128, F32)], "l1_rope_bwd",
                             into=[(0, d_qf, 1), (1, d_proj1, kr_col // 128)])
    d_cqn = _mm(d_qf, W["mla_w_uq"], "nt", F32, "l1_uq_dx")
    L1[("mla_w_uq", None)] = _mm(cqn, d_qf, "tn", F32, "l1_uq_dw")
    d_ckvn = _mm(d_kvf, W["mla_w_ukv"], "nt", F32, "l1_ukv_dx")
    L1[("mla_w_ukv", None)] = _mm(ckvn, d_kvf, "tn", F32, "l1_ukv_dw")
    d_proj1, d_qg = _rowwise_bwd(_f_rms, [cq], [qg], [d_cqn], 1, "l1_qnorm_bwd", into=(d_proj1, cq[2]))
    d_proj1, d_kvg = _rowwise_bwd(_f_rms, [ckv], [kvg], [d_ckvn], 1, "l1_kvnorm_bwd", into=(d_proj1, ckv[2]))
    L1[("w_mem_kv", 1)] = d_wmem1
    L1[("mla_w_in", None)] = _mm(u1, d_proj1, "tn", F32, "l1_in_dw")
    L1[("w_out", 1)] = d_wout1
    W.ready("l1", L1)
    d_u1 = _mm(d_proj1, W["mla_w_in"], "nt", F32, "l1_in_dx", carry=W.carry("l1_in_dx"))
    dh1, d_g1 = _rowwise_bwd(_f_rms, [h1], [g1], [d_u1], 1, "l1_norm_bwd", add=dh2)

    dy0 = _mm(dh1, W["w_out"][0], "nt", F32, "l0_out_dx")
    d_wout0 = _mm(y0_t, dh1, "nn", F32, "l0_out_dw")
    d_ycat0, d_proj0 = _gate_bwd(ycat0, proj0, z0_col, dy0, "l0_gate_bwd", carry=W.carry("l0_gate_bwd"))
    d_proj0, d_wmem0, d_mg0 = _mem_attention_bwd(proj0, qm0_col, ycat0, d_ycat0, d_proj0, mem_saved0, mem2, mg0,
                                                 W["w_mem_kv"][0], B, S, "l0")
    W.ready("l0a", {("w_mem_kv", 0): d_wmem0, ("w_out", 0): d_wout0})
    d_cv, d_ln_g, d_ln_b = _rowwise_bwd(_f_ln_silu, [cv], [ln_g, ln_b], [(d_ycat0, conv_w, 0)], 1, "l0_ln_bwd",
                                        carry=W.carry("l0_ln_bwd"))
    d_glu, d_dw, d_dwb = _dwconv_bwd(proj0, dw, d_cv.reshape(B, S, conv_w), "l0_dwconv_bwd",
                                     carry=W.carry("l0_dwconv_bwd"))
    d_proj0 = _glu_bwd(proj0, d_glu.reshape(T, conv_w), d_proj0, "l0_glu_bwd")
    d_conv_w_in = _mm(u0, d_proj0, "tn", F32, "l0_in_dw", carry=W.carry("l0_in_dw"))
    W.ready("l0b", {("conv_w_in", None): d_conv_w_in, ("conv_dw", None): d_dw,
                    ("mla_q_norm_g", None): d_qg.reshape(-1), ("mla_kv_norm_g", None): d_kvg.reshape(-1)})
    d_u0 = _mm(d_proj0, W["conv_w_in"], "nt", F32, "l0_in_dx", carry=W.carry("l0_in_dx"))
    dx, d_g0 = _rowwise_bwd(_f_rms, [h0], [g0], [d_u0], 1, "l0_norm_bwd", add=dh1)
    dx = dx.reshape(B, S, D)

    G["norm_g"] = jnp.concatenate([d_g0, d_g1], axis=0)
    G["mem_norm_g"] = jnp.concatenate([d_mg0, d_mg1], axis=0)
    G["conv_dw_b"] = d_dwb
    G["conv_ln_g"], G["conv_ln_b"] = d_ln_g, d_ln_b
    return loss128[0, 0], dx, G


def _mla_in_perm(w):
    c1, c2 = Q_RANK, Q_RANK + KV_RANK
    c3 = c2 + MLA_ROPE
    c4 = c3 + MEM_WIDTH
    zero = jnp.zeros((w.shape[0], HALF_ROPE), w.dtype)
    return jnp.concatenate([w[:, :c1], w[:, c4:], w[:, c3:c4], w[:, c1:c2], w[:, c2:c2 + HALF_ROPE], zero,
                            w[:, c2 + HALF_ROPE:c3], zero], axis=1)


def _mla_in_unperm(g):
    z_w = g.shape[1] - (Q_RANK + MEM_WIDTH + KV_RANK + 128)
    z0, q0 = Q_RANK, Q_RANK + z_w
    k0 = q0 + MEM_WIDTH
    r = k0 + KV_RANK
    return jnp.concatenate([g[:, :Q_RANK], g[:, k0:r], g[:, r:r + HALF_ROPE], g[:, r + 64:r + 64 + HALF_ROPE],
                            g[:, q0:k0], g[:, z0:q0]], axis=1)


def _uq_perm(w):
    n = w.shape[0]
    w3 = w.reshape(n, MLA_HEADS, MLA_QK)
    zero = jnp.zeros((n, MLA_HEADS, HALF_ROPE), w.dtype)
    rope = jnp.concatenate([w3[:, :, MLA_NOPE:MLA_NOPE + HALF_ROPE], zero, w3[:, :, MLA_NOPE + HALF_ROPE:], zero], axis=2)
    return jnp.concatenate([w3[:, :, :MLA_NOPE].reshape(n, -1), rope.reshape(n, -1)], axis=1)


def _uq_unperm(g):
    n = g.shape[0]
    n_nope = MLA_HEADS * MLA_NOPE
    rope = g[:, n_nope:].reshape(n, MLA_HEADS, 128)
    return jnp.concatenate([g[:, :n_nope].reshape(n, MLA_HEADS, MLA_NOPE), rope[:, :, :HALF_ROPE],
                            rope[:, :, 64:64 + HALF_ROPE]], axis=2).reshape(n, -1)


_ROW_CUT = ("w_mem_kv", "w_out")
_COL_CUT = ("conv_w_in", "mla_w_in", "mla_w_uq", "mla_w_ukv", "conv_dw")
_BIG = ("w_mem_kv", "w_out", "conv_w_in", "mla_w_in", "mla_w_uq", "mla_w_ukv")
_SMALL_SHARDED = ("conv_dw", "mla_q_norm_g", "mla_kv_norm_g")
_REPLICATED = ("norm_g", "mem_norm_g", "conv_dw_b", "conv_ln_g", "conv_ln_b", "final_norm_g")
_PERM = {"mla_w_in": (_mla_in_perm, _mla_in_unperm), "mla_w_uq": (_uq_perm, _uq_unperm)}


def _join(n, blocks):
    if n in _ROW_CUT:
        _, L, r, c = blocks.shape
        return blocks.transpose(1, 0, 2, 3).reshape(L, N_DEV * r, c)
    if n in _COL_CUT:
        _, _, r, c = blocks.shape
        return blocks.reshape(N_DEV, r, c).transpose(1, 0, 2).reshape(r, N_DEV * c)
    return blocks.reshape(-1)


def _cut(n, full, shard_shape):
    if n in _ROW_CUT:
        L, r, c = shard_shape
        return full.reshape(L, N_DEV, r, c).transpose(1, 0, 2, 3)
    if n in _COL_CUT:
        _, r, c = shard_shape
        return full.reshape(r, N_DEV, c).transpose(1, 0, 2).reshape(N_DEV, 1, r, c)
    return full.reshape(N_DEV, 1, -1)


def _flat_pad(parts, size):
    flat = jnp.concatenate([p.reshape(-1) for p in parts])
    return jnp.concatenate([flat, jnp.zeros((size - flat.shape[0],), flat.dtype)])


SMALL_LANES = 128 * 8


def _as_tiles(flat_parts):
    total = sum(p.size for p in flat_parts)
    size = -(-total // SMALL_LANES) * SMALL_LANES
    return _flat_pad(flat_parts, size).reshape(8, size // 8)


def _split_flat(flat, like):
    out, o = [], 0
    for a in like:
        out.append(flat[o:o + a.size].reshape(a.shape))
        o += a.size
    return out


_HBM = pl.BlockSpec(memory_space=pltpu.HBM)
_VMEM = pl.BlockSpec(memory_space=pltpu.VMEM)


def _position():
    return lax.axis_index("x"), lax.axis_index("y"), lax.axis_index("c")


def _dma_sems(n):
    return [pltpu.SemaphoreType.DMA((n,)), pltpu.SemaphoreType.DMA((n,))]


def _run_stage(stage, name):
    n_in, n_out = len(stage.ins), len(stage.out_shapes)

    def body(*refs):
        ins, outs, sems = refs[:n_in], refs[n_in:n_in + n_out], refs[n_in + n_out:]
        stage.start(ins, outs, sems)
        stage.wait(ins, outs, sems)

    outs = _call(body, name, stage.out_shapes, in_specs=[_HBM] * n_in, out_specs=[_HBM] * n_out, scratch=stage.sems,
                 aliases=stage.aliases)(*stage.ins)
    stage.outs = list(outs)
    return stage.outs


def _gather_chips_stage(shards):
    n = len(shards)

    def copies(x_refs, out_refs, sems):
        send_sems, recv_sems, _ = sems
        x, y, c = _position()
        peers = [(x, y, 1 - c), (1 - x, y, c), (x, 1 - y, c), (1 - x, 1 - y, c)]
        out = []
        for a in range(n):
            for k, (px, py, pc) in enumerate(peers):
                send = pltpu.make_async_remote_copy(src_ref=x_refs[a], dst_ref=out_refs[a].at[4 * x + 2 * y + c],
                                                    send_sem=send_sems.at[4 * a + k], recv_sem=recv_sems.at[4 * a + k],
                                                    device_id=(px, py, pc), device_id_type=MESH)
                recv = pltpu.make_async_remote_copy(src_ref=x_refs[a], dst_ref=out_refs[a].at[4 * px + 2 * py + pc],
                                                    send_sem=send_sems.at[4 * a + k], recv_sem=recv_sems.at[4 * a + k],
                                                    device_id=(px, py, pc), device_id_type=MESH)
                out.append((send, recv))
        return out

    def local(x_refs, out_refs, sems):
        x, y, c = _position()
        return [pltpu.make_async_copy(x_refs[a], out_refs[a].at[4 * x + 2 * y + c], sems[2].at[a]) for a in range(n)]

    def start(x_refs, out_refs, sems):
        for cp in local(x_refs, out_refs, sems):
            cp.start()
        for send, _ in copies(x_refs, out_refs, sems):
            send.start()

    def wait(x_refs, out_refs, sems):
        for send, recv in copies(x_refs, out_refs, sems):
            recv.wait_recv()
            send.wait_send()
        for cp in local(x_refs, out_refs, sems):
            cp.wait()

    return _Stage(shards, [jax.ShapeDtypeStruct((N_DEV,) + a.shape, a.dtype) for a in shards],
                  _dma_sems(4 * n) + [pltpu.SemaphoreType.DMA((n,))], start, wait)


def _gather_sibling_stage(bufs):
    n = len(bufs)

    def copies(out_refs, sems):
        send_sems, recv_sems = sems
        x, y, c = _position()
        out = []
        for a in range(n):
            for j, (px, py) in enumerate([(1 - x, y), (x, 1 - y), (1 - x, 1 - y)]):
                mine, theirs = out_refs[a].at[4 * px + 2 * py + c], out_refs[a].at[4 * px + 2 * py + (1 - c)]
                send = pltpu.make_async_remote_copy(src_ref=mine, dst_ref=mine, send_sem=send_sems.at[3 * a + j],
                                                    recv_sem=recv_sems.at[3 * a + j], device_id=(x, y, 1 - c),
                                                    device_id_type=MESH)
                recv = pltpu.make_async_remote_copy(src_ref=mine, dst_ref=theirs, send_sem=send_sems.at[3 * a + j],
                                                    recv_sem=recv_sems.at[3 * a + j], device_id=(x, y, 1 - c),
                                                    device_id_type=MESH)
                out.append((send, recv))
        return out

    def start(_, out_refs, sems):
        for send, _r in copies(out_refs, sems):
            send.start()

    def wait(_, out_refs, sems):
        for send, recv in copies(out_refs, sems):
            recv.wait_recv()
            send.wait_send()

    return _Stage(bufs, [jax.ShapeDtypeStruct(b.shape, b.dtype) for b in bufs], _dma_sems(3 * n), start, wait,
                  aliases={a: a for a in range(n)})


def _all_gather_small(v, name):
    r, n = v.shape

    def body(x_ref, out_ref, send_sems, recv_sems, local_sem):
        x, y, c = _position()
        me = 4 * x + 2 * y + c
        mine = pltpu.make_async_copy(x_ref, out_ref.at[me], local_sem)
        mine.start()
        flips = [(fx, fy, fc) for fx in (0, 1) for fy in (0, 1) for fc in (0, 1)][1:]
        copies = []
        for k, (fx, fy, fc) in enumerate(flips):
            peer = (x ^ fx, y ^ fy, c ^ fc)
            cp = pltpu.make_async_remote_copy(src_ref=x_ref, dst_ref=out_ref.at[me], send_sem=send_sems.at[k],
                                              recv_sem=recv_sems.at[k], device_id=peer, device_id_type=MESH)
            cp.start()
            copies.append(cp)
        for k, (fx, fy, fc) in enumerate(flips):
            px, py, pc = x ^ fx, y ^ fy, c ^ fc
            src = out_ref.at[4 * px + 2 * py + pc]
            pltpu.make_async_remote_copy(src_ref=x_ref, dst_ref=src, send_sem=send_sems.at[k], recv_sem=recv_sems.at[k],
                                         device_id=(px, py, pc), device_id_type=MESH).wait_recv()
        for cp in copies:
            cp.wait_send()
        mine.wait()

    return _call(body, name, jax.ShapeDtypeStruct((N_DEV, r, n), v.dtype), in_specs=[_VMEM], out_specs=_VMEM,
                 scratch=_dma_sems(7) + [pltpu.SemaphoreType.DMA(())])(v)


def _reduce_sibling_stage(gs):
    n = len(gs)

    def copies(g_refs, out_refs, sems):
        send_sems, recv_sems = sems
        x, y, c = _position()
        return [pltpu.make_async_remote_copy(src_ref=g_refs[a].at[2 * k + (1 - c)], dst_ref=out_refs[a].at[k],
                                             send_sem=send_sems.at[4 * a + k], recv_sem=recv_sems.at[4 * a + k],
                                             device_id=(x, y, 1 - c), device_id_type=MESH)
                for a in range(n) for k in range(4)]

    def start(g_refs, out_refs, sems):
        for cp in copies(g_refs, out_refs, sems):
            cp.start()

    def wait(g_refs, out_refs, sems):
        for cp in copies(g_refs, out_refs, sems):
            cp.wait()

    return _Stage(gs, [jax.ShapeDtypeStruct((4,) + g.shape[1:], g.dtype) for g in gs], _dma_sems(4 * n), start, wait)


def _rows2d(shape):
    cols = shape[-1]
    rows = 1
    for s in shape[:-1]:
        rows *= s
    return rows, cols


def _add_own(g, recv, name):
    rows, cols = _rows2d(g.shape[1:])
    tr = _pick(rows, 256, 8)
    c = lax.axis_index("c").astype(jnp.int32).reshape(1)

    def body(c_ref, g_ref, r_ref, o_ref):
        o_ref[...] = (g_ref[...].astype(F32) + r_ref[...].astype(F32)).astype(o_ref.dtype)

    grid_spec = pltpu.PrefetchScalarGridSpec(
        num_scalar_prefetch=1, grid=(4, rows // tr),
        in_specs=[pl.BlockSpec((None, None, tr, cols), lambda k, i, c_ref: (k, c_ref[0], i, 0)),
                  pl.BlockSpec((None, tr, cols), lambda k, i, c_ref: (k, i, 0))],
        out_specs=pl.BlockSpec((None, tr, cols), lambda k, i, c_ref: (k, i, 0)))
    return _call(body, name, jax.ShapeDtypeStruct((4, rows, cols), g.dtype), grid_spec=grid_spec,
                 dims=("parallel", "parallel"))(c, g.reshape(4, 2, rows, cols), recv.reshape(4, rows, cols))


def _reduce_chips_stage(pas):
    n = len(pas)

    def copies(pa_refs, out_refs, sems):
        send_sems, recv_sems, _ = sems
        x, y, c = _position()
        my_chip = 2 * x + y
        out = []
        for a in range(n):
            for j, (px, py) in enumerate([(1 - x, y), (x, 1 - y), (1 - x, 1 - y)]):
                send = pltpu.make_async_remote_copy(src_ref=pa_refs[a].at[2 * px + py], dst_ref=out_refs[a].at[my_chip],
                                                    send_sem=send_sems.at[3 * a + j], recv_sem=recv_sems.at[3 * a + j],
                                                    device_id=(px, py, c), device_id_type=MESH)
                recv = pltpu.make_async_remote_copy(src_ref=pa_refs[a].at[2 * px + py], dst_ref=out_refs[a].at[2 * px + py],
                                                    send_sem=send_sems.at[3 * a + j], recv_sem=recv_sems.at[3 * a + j],
                                                    device_id=(px, py, c), device_id_type=MESH)
                out.append((send, recv))
        return out

    def local(pa_refs, out_refs, sems):
        x, y, _ = _position()
        return [pltpu.make_async_copy(pa_refs[a].at[2 * x + y], out_refs[a].at[2 * x + y], sems[2].at[a]) for a in range(n)]

    def start(pa_refs, out_refs, sems):
        for cp in local(pa_refs, out_refs, sems):
            cp.start()
        for send, _r in copies(pa_refs, out_refs, sems):
            send.start()

    def wait(pa_refs, out_refs, sems):
        for send, recv in copies(pa_refs, out_refs, sems):
            recv.wait_recv()
            send.wait_send()
        for cp in local(pa_refs, out_refs, sems):
            cp.wait()

    return _Stage(pas, [jax.ShapeDtypeStruct(pa.shape, pa.dtype) for pa in pas],
                  _dma_sems(3 * n) + [pltpu.SemaphoreType.DMA((n,))], start, wait)


def _adamw_math(w, g, m, v):
    m = ADAM_B1 * m + (1.0 - ADAM_B1) * g
    v = ADAM_B2 * v + (1.0 - ADAM_B2) * (g * g)
    m_hat = m / (1.0 - ADAM_B1 ** ADAM_STEP)
    v_hat = v / (1.0 - ADAM_B2 ** ADAM_STEP)
    delta = -ADAM_LR * (m_hat / (jnp.sqrt(v_hat) + ADAM_EPS) + ADAM_WD * w)
    return delta, m, v


def _sum_adamw(parts, w, m, v, name):
    n, rows, cols = parts.shape
    tr = _pick(rows, 128, 8)

    def body(p_ref, w_ref, m_ref, v_ref, g_ref, d_ref, nm_ref, nv_ref):
        g = p_ref[0].astype(F32)
        for k in range(1, n):
            g = g + p_ref[k].astype(F32)
        d, nm, nv = _adamw_math(w_ref[...], g, m_ref[...], v_ref[...])
        g_ref[...], d_ref[...], nm_ref[...], nv_ref[...] = g, d, nm, nv

    blk = pl.BlockSpec((tr, cols), lambda i: (i, 0))
    return _call(body, name, [jax.ShapeDtypeStruct((rows, cols), F32)] * 4, grid=(rows // tr,),
                 in_specs=[pl.BlockSpec((n, tr, cols), lambda i: (0, i, 0)), blk, blk, blk],
                 out_specs=[blk] * 4, dims=("parallel",))(parts, w, m, v)


_WEIGHTS = ("norm_g", "mem_norm_g", "w_mem_kv", "w_out", "conv_w_in", "conv_dw", "conv_dw_b", "conv_ln_g", "conv_ln_b",
            "mla_w_in", "mla_q_norm_g", "mla_w_uq", "mla_kv_norm_g", "mla_w_ukv", "final_norm_g")


_GATHER_GROUPS = {"a": ("conv_w_in",), "b": ("w_mem_kv", "w_out"), "c": ("mla_w_in", "mla_w_uq", "mla_w_ukv")}
_CARRIERS = {"l0_norm": ("gather chips", ("a",)), "l0_in": ("gather chips", ("b",)), "l0_dwconv": ("gather chips", ("c",)),
             "l0_ln": ("gather sibling", ("b",)), "l0_out": ("gather sibling", ("c",)),
             "l1_in_dx": ("reduce sibling", ("l1",)), "l0_ln_bwd": ("reduce sibling", ("l0a",)),
             "l0_gate_bwd": ("reduce chips", ("l1", 0, 2)), "l0_dwconv_bwd": ("reduce chips", ("l1", 2, 5)),
             "l0_in_dw": ("reduce chips", ("l0a",)), "l0_in_dx": ("reduce sibling alone, then chips", ("l0b",))}


class _Schedule:
    def __init__(self, w):
        self.w, self.full, self.gather, self.reduce, self.reduced = w, {}, {}, {}, {}
        small = _all_gather_small(_as_tiles([w[n] for n in _SMALL_SHARDED]), "gather_small_weights").reshape(N_DEV, -1)
        o = 0
        for n in _SMALL_SHARDED:
            self.full[n] = _join(n, small[:, o:o + w[n].size].reshape((N_DEV,) + w[n].shape))
            o += w[n].size
        for n in _REPLICATED:
            self.full[n] = w[n]

    def carry(self, call):
        kind, (g, *part) = _CARRIERS[call]
        if kind == "gather chips":
            self.gather[g] = [_gather_chips_stage([self.w[n].astype(BF16) for n in _GATHER_GROUPS[g]])]
            return self.gather[g][0]
        if kind == "gather sibling":
            self.gather[g].append(_gather_sibling_stage(self.gather[g][0].outs))
            return self.gather[g][1]
        r = self.reduce[g]
        if kind == "reduce sibling":
            r["sibling"] = _reduce_sibling_stage(r["cut"])
            return r["sibling"]
        if kind != "reduce chips":
            r["sibling"] = _reduce_sibling_stage(r["cut"])
            _run_stage(r["sibling"], "reduce_sibling_" + g)
        if "partial" not in r:
            r["partial"] = [_add_own(c, s, "reduce_add_%s_%d" % (g, i))
                            for i, (c, s) in enumerate(zip(r["cut"], r["sibling"].outs))]
        lo, hi = part if part else (0, len(r["keys"]))
        stage = _reduce_chips_stage(r["partial"][lo:hi])
        r.setdefault("chips", []).append((r["keys"][lo:hi], stage))
        return stage

    def __getitem__(self, name):
        if name not in self.full:
            g = [k for k, names in _GATHER_GROUPS.items() if name in names][0]
            if len(self.gather[g]) == 1:
                self.gather[g].append(_gather_sibling_stage(self.gather[g][0].outs))
                _run_stage(self.gather[g][1], "gather_sibling_" + g)
            for n, buf in zip(_GATHER_GROUPS[g], self.gather[g][1].outs):
                self.full[n] = _PERM[n][0](_join(n, buf)) if n in _PERM else _join(n, buf)
        return self.full[name]

    def ready(self, group, grads, payload=BF16):
        keys, cut, small = [], [], []
        for (n, layer), g in grads.items():
            if n in _SMALL_SHARDED:
                small.append(_cut(n, g, self.w[n].shape).reshape(N_DEV, -1))
                continue
            keys.append((n, layer))
            if layer is not None:
                cut.append(g.reshape((N_DEV,) + self.w[n].shape[1:]).astype(payload))
            else:
                cut.append(_cut(n, _PERM[n][1](g) if n in _PERM else g, self.w[n].shape).astype(payload))
        if small:
            keys.append(("small", None))
            cut.append(jax.vmap(lambda r: _as_tiles([r]))(jnp.concatenate(small, axis=1)))
        self.reduce[group] = {"keys": keys, "cut": cut}

    def finish(self):
        out = {}
        for r in self.reduce.values():
            for keys, stage in r["chips"]:
                out.update(dict(zip(keys, stage.outs)))
        return out


def kernel(x, mem, positions, norm_g, mem_norm_g, w_mem_kv, w_out, conv_w_in, conv_dw, conv_dw_b, conv_ln_g, conv_ln_b, mla_w_in, mla_q_norm_g, mla_w_uq, mla_kv_norm_g, mla_w_ukv, final_norm_g, loss_target, m_norm_g, m_mem_norm_g, m_w_mem_kv, m_w_out, m_conv_w_in, m_conv_dw, m_conv_dw_b, m_conv_ln_g, m_conv_ln_b, m_mla_w_in, m_mla_q_norm_g, m_mla_w_uq, m_mla_kv_norm_g, m_mla_w_ukv, m_final_norm_g, v_norm_g, v_mem_norm_g, v_w_mem_kv, v_w_out, v_conv_w_in, v_conv_dw, v_conv_dw_b, v_conv_ln_g, v_conv_ln_b, v_mla_w_in, v_mla_q_norm_g, v_mla_w_uq, v_mla_kv_norm_g, v_mla_w_ukv, v_final_norm_g):
    w = dict(zip(_WEIGHTS, (norm_g, mem_norm_g, w_mem_kv, w_out, conv_w_in, conv_dw, conv_dw_b, conv_ln_g, conv_ln_b,
                            mla_w_in, mla_q_norm_g, mla_w_uq, mla_kv_norm_g, mla_w_ukv, final_norm_g)))
    m = dict(zip(_WEIGHTS, (m_norm_g, m_mem_norm_g, m_w_mem_kv, m_w_out, m_conv_w_in, m_conv_dw, m_conv_dw_b, m_conv_ln_g,
                            m_conv_ln_b, m_mla_w_in, m_mla_q_norm_g, m_mla_w_uq, m_mla_kv_norm_g, m_mla_w_ukv, m_final_norm_g)))
    v = dict(zip(_WEIGHTS, (v_norm_g, v_mem_norm_g, v_w_mem_kv, v_w_out, v_conv_w_in, v_conv_dw, v_conv_dw_b, v_conv_ln_g,
                            v_conv_ln_b, v_mla_w_in, v_mla_q_norm_g, v_mla_w_uq, v_mla_kv_norm_g, v_mla_w_ukv, v_final_norm_g)))

    sched = _Schedule(w)
    loss_local, dx, G = _forward_backward(x, mem, positions, loss_target, sched)
    loss = lax.psum(loss_local, ("x", "y", "c"))

    from_chips = sched.finish()
    out = [{}, {}, {}, {}]
    for n in _BIG:
        if n in _ROW_CUT:
            res = [_sum_adamw(from_chips[(n, l)], w[n][l], m[n][l], v[n][l], "adamw_%s_%d" % (n, l)) for l in range(w[n].shape[0])]
            res = [jnp.stack(r) for r in zip(*res)]
        else:
            rows, cols = _rows2d(w[n].shape)
            res = _sum_adamw(from_chips[(n, None)], w[n].reshape(rows, cols), m[n].reshape(rows, cols),
                             v[n].reshape(rows, cols), "adamw_" + n)
        for o, r in zip(out, res):
            o[n] = r.reshape(w[n].shape)
    small_like = [w[n] for n in _SMALL_SHARDED]
    res = _sum_adamw(from_chips[("small", None)], _as_tiles(small_like), _as_tiles([m[n] for n in _SMALL_SHARDED]),
                     _as_tiles([v[n] for n in _SMALL_SHARDED]), "adamw_small")
    for o, r in zip(out, res):
        for n, a in zip(_SMALL_SHARDED, _split_flat(r.reshape(-1), small_like)):
            o[n] = a

    rep_like = [w[n] for n in _REPLICATED]
    rep_parts = _all_gather_small(_as_tiles([G[n] for n in _REPLICATED]), "gather_replicated_grads")
    res = _sum_adamw(rep_parts, _as_tiles(rep_like), _as_tiles([m[n] for n in _REPLICATED]),
                     _as_tiles([v[n] for n in _REPLICATED]), "adamw_replicated")
    for o, r in zip(out, res):
        for n, a in zip(_REPLICATED, _split_flat(r.reshape(-1), rep_like)):
            o[n] = a

    return (loss, dx, *[out[0][n] for n in _WEIGHTS], *[out[1][n] for n in _WEIGHTS],
            *[out[2][n] for n in _WEIGHTS], *[out[3][n] for n in _WEIGHTS])
```

```python
import jax
import jax.numpy as jnp
from jax import lax
from jax.experimental import pallas as pl
from jax.experimental.pallas import tpu as pltpu

F32 = jnp.float32
BF16 = jnp.bfloat16
MESH = pl.DeviceIdType.MESH
N_DEV = 8
VMEM_LIMIT_BYTES = 48 * 1024 * 1024

MEM_HEADS, MEM_HEAD_DIM = 4, 128
MEM_WIDTH = MEM_HEADS * MEM_HEAD_DIM
CONV_KERNEL = 31
CONV_PAD = 32
MLA_HEADS, MLA_NOPE, MLA_ROPE = 12, 128, 64
MLA_QK = MLA_NOPE + MLA_ROPE
HALF_ROPE = MLA_ROPE // 2
Q_RANK, KV_RANK = 512, 256
ROPE_THETA = 10000.0
RMS_EPS = 1e-6
LN_EPS = 1e-5
ADAM_LR, ADAM_B1, ADAM_B2, ADAM_EPS, ADAM_WD, ADAM_STEP = 0.001, 0.9, 0.999, 1e-08, 0.01, 10
NEG = -1e30


class _Stage:
    def __init__(self, ins, out_shapes, sems, start, wait, aliases=None):
        self.ins, self.out_shapes, self.sems = list(ins), list(out_shapes), list(sems)
        self.start, self.wait, self.aliases, self.outs = start, wait, dict(aliases or {}), None


def _call(body, name, out_shape, grid=None, in_specs=None, out_specs=None, scratch=(), dims=None, grid_spec=None, aliases=None,
          carry=None):
    params = dict(vmem_limit_bytes=VMEM_LIMIT_BYTES)
    if dims is not None:
        params["dimension_semantics"] = dims
    kw = {}
    if carry is not None:
        single = not isinstance(out_shape, (list, tuple))
        main_out = [out_shape] if single else list(out_shape)
        main_specs = [out_specs] if single else list(out_specs)
        n_in, n_out, n_scr = len(in_specs), len(main_out), len(scratch)
        x_in, x_out = len(carry.ins), len(carry.out_shapes)
        inner, steps = body, tuple(grid)

        def body(*refs):
            ins, xin = refs[:n_in], refs[n_in:n_in + x_in]
            outs = refs[n_in + x_in:n_in + x_in + n_out]
            xout = refs[n_in + x_in + n_out:n_in + x_in + n_out + x_out]
            scr = refs[n_in + x_in + n_out + x_out:n_in + x_in + n_out + x_out + n_scr]
            xsem = refs[n_in + x_in + n_out + x_out + n_scr:]
            ids = [pl.program_id(a) for a in range(len(steps))]
            first, last = ids[0] == 0, ids[0] == steps[0] - 1
            for a in range(1, len(steps)):
                first = jnp.logical_and(first, ids[a] == 0)
                last = jnp.logical_and(last, ids[a] == steps[a] - 1)
            pl.when(first)(lambda: carry.start(xin, xout, xsem))
            inner(*ins, *outs, *scr)
            pl.when(last)(lambda: carry.wait(xin, xout, xsem))

        hbm = pl.BlockSpec(memory_space=pltpu.HBM)
        aliases = dict(aliases or {})
        aliases.update({n_in + k: n_out + v for k, v in carry.aliases.items()})
        res = _call(body, name, main_out + carry.out_shapes, grid=grid, in_specs=list(in_specs) + [hbm] * x_in,
                    out_specs=main_specs + [hbm] * x_out, scratch=list(scratch) + carry.sems, dims=dims, aliases=aliases)

        def run(*args):
            outs = res(*args, *carry.ins)
            carry.outs = list(outs[n_out:])
            return outs[0] if single else outs[:n_out]

        return run
    if aliases:
        kw["input_output_aliases"] = aliases
    if grid_spec is not None:
        kw["grid_spec"] = grid_spec
    else:
        if grid is not None:
            kw["grid"] = grid
        kw["in_specs"] = in_specs
        kw["out_specs"] = out_specs
        kw["scratch_shapes"] = list(scratch)
    return pl.pallas_call(body, name=name, out_shape=out_shape, compiler_params=pltpu.CompilerParams(**params), **kw)


def _pick(n, target, mult):
    best = None
    for d in range(mult, min(n, target) + 1, mult):
        if n % d == 0:
            best = d
    return n if best is None else best


_DOT_DIMS = {"nn": (((1,), (0,)), ((), ())), "nt": (((1,), (1,)), ((), ())), "tn": (((0,), (0,)), ((), ()))}


def _mm(a, b, mode, out_dtype, name, res=None, carry=None):
    if mode == "tn":
        a, mode = a.T, "nn"
    if mode == "nn":
        (M, K), N = a.shape, b.shape[1]
    else:
        (M, K), N = a.shape, b.shape[0]
    tm = _pick(M, 1024, 8)
    tn = _pick(N, 1536, 128)
    tk = _pick(K, 1536, 128)
    nk = K // tk
    has_res = res is not None

    def body(*refs):
        if has_res:
            a_ref, b_ref, r_ref, o_ref, acc = refs
        else:
            a_ref, b_ref, o_ref, acc = refs
        k = pl.program_id(2)
        part = lax.dot_general(a_ref[...].astype(BF16), b_ref[...].astype(BF16), _DOT_DIMS[mode],
                               preferred_element_type=F32)
        if nk == 1:
            o_ref[...] = (part + r_ref[...] if has_res else part).astype(o_ref.dtype)
            return

        @pl.when(k == 0)
        def _():
            acc[...] = part

        @pl.when(k > 0)
        def _():
            acc[...] += part

        @pl.when(k == nk - 1)
        def _():
            r = acc[...]
            if has_res:
                r = r + r_ref[...]
            o_ref[...] = r.astype(o_ref.dtype)

    a_spec = pl.BlockSpec((tm, tk), lambda i, j, k: (i, k))
    b_spec = {"nn": pl.BlockSpec((tk, tn), lambda i, j, k: (k, j)),
              "nt": pl.BlockSpec((tn, tk), lambda i, j, k: (j, k))}[mode]
    o_spec = pl.BlockSpec((tm, tn), lambda i, j, k: (i, j))
    in_specs = [a_spec, b_spec] + ([o_spec] if has_res else [])
    args = (a, b) + ((res,) if has_res else ())
    return _call(body, name, jax.ShapeDtypeStruct((M, N), out_dtype), grid=(M // tm, N // tn, nk),
                 in_specs=in_specs, out_specs=o_spec, scratch=[pltpu.VMEM((tm, tn), F32)],
                 dims=("parallel", "parallel", "arbitrary"), carry=carry)(*args)


def _views(rows):
    return [r if isinstance(r, tuple) else (r, r.shape[1], 0) for r in rows]


def _row_tile(T, rows):
    return min(T, 512 if max(w for _, w, _ in rows) <= 1024 else 256)


def _rowwise(f, rows, params, outs, name, carry=None, into=None):
    rows = _views(rows)
    T = rows[0][0].shape[0]
    tb = _row_tile(T, rows)
    nr, npar = len(rows), len(params)
    outs = [o if len(o) == 3 else (o[0], o[1], o[0]) for o in outs]
    into = into or []

    def body(*refs):
        vals = f(*[r[...].astype(F32) for r in refs[:nr]], *[p[...] for p in refs[nr:nr + npar]])
        for o_ref, v in zip(refs[nr + npar + len(into):], vals):
            o_ref[...] = v.astype(o_ref.dtype)

    row_spec = lambda w, cb=0: pl.BlockSpec((tb, w), lambda i: (i, cb))
    par_spec = lambda w: pl.BlockSpec((1, w), lambda i: (0, 0))
    out_shape = [jax.ShapeDtypeStruct((T, tw), dt) for _, dt, tw in outs]
    out_specs = [row_spec(w) for w, _, _ in outs]
    in_specs = [row_spec(w, cb) for _, w, cb in rows] + [par_spec(p.shape[1]) for p in params]
    args = [r[0] for r in rows] + list(params)
    aliases = {}
    for k, arr, cb in into:
        aliases[len(args)] = k
        in_specs.append(pl.BlockSpec(memory_space=pl.ANY))
        args.append(arr)
        out_shape[k] = jax.ShapeDtypeStruct(arr.shape, arr.dtype)
        out_specs[k] = row_spec(outs[k][0], cb)
    return _call(body, name, out_shape, grid=(T // tb,), in_specs=in_specs, out_specs=out_specs, dims=("parallel",),
                 carry=carry, aliases=aliases)(*args)


def _rowwise_bwd(f, rows, params, douts, n_diff, name, carry=None, add=None, into=None):
    rows, douts = _views(rows), _views(douts)
    T = rows[0][0].shape[0]
    tb = _row_tile(T, rows)
    nr, npar, nd = len(rows), len(params), len(douts)
    n_add = 0 if add is None else 1

    def body(*refs):
        rv = [r[...].astype(F32) for r in refs[:nr]]
        pv = [p[...] for p in refs[nr:nr + npar]]
        dv = [d[...].astype(F32) for d in refs[nr + npar:nr + npar + nd]]
        o_refs = refs[nr + npar + nd + n_add + (0 if into is None else 1):]
        fixed = rv[n_diff:]

        def g(*xs):
            return tuple(f(*xs[:n_diff], *fixed, *xs[n_diff:]))

        _, vjp = jax.vjp(g, *rv[:n_diff], *pv)
        grads = list(vjp(tuple(dv)))
        if add is not None:
            grads[0] = grads[0] + refs[nr + npar + nd][...]
        for o_ref, gr in zip(o_refs[:n_diff], grads[:n_diff]):
            o_ref[...] = gr.astype(o_ref.dtype)
        first = pl.program_id(0) == 0
        for o_ref, gr in zip(o_refs[n_diff:], grads[n_diff:]):
            @pl.when(first)
            def _(o_ref=o_ref):
                o_ref[...] = jnp.zeros_like(o_ref)

            o_ref[...] += gr

    row_spec = lambda w, cb=0: pl.BlockSpec((tb, w), lambda i: (i, cb))
    par_spec = lambda w: pl.BlockSpec((1, w), lambda i: (0, 0))
    out_shape = ([jax.ShapeDtypeStruct((T, w), F32) for _, w, _ in rows[:n_diff]]
                 + [jax.ShapeDtypeStruct((1, p.shape[1]), F32) for p in params])
    out_specs = [row_spec(w) for _, w, _ in rows[:n_diff]] + [par_spec(p.shape[1]) for p in params]
    in_specs = ([row_spec(w, cb) for _, w, cb in rows] + [par_spec(p.shape[1]) for p in params]
                + [row_spec(w, cb) for _, w, cb in douts])
    args = [r[0] for r in rows] + list(params) + [d[0] for d in douts]
    aliases = None
    if add is not None:
        in_specs.append(row_spec(add.shape[1]))
        args.append(add)
    if into is not None:
        aliases = {len(args): 0}
        in_specs.append(pl.BlockSpec(memory_space=pl.ANY))
        args.append(into[0])
        out_shape[0] = jax.ShapeDtypeStruct(into[0].shape, into[0].dtype)
        out_specs[0] = row_spec(rows[0][1], into[1])
    return _call(body, name, out_shape, grid=(T // tb,), in_specs=in_specs, out_specs=out_specs,
                 dims=("arbitrary",), carry=carry, aliases=aliases)(*args)


def _sig(x):
    return 1.0 / (1.0 + jnp.exp(-x))


def _rms(x, g):
    return x * lax.rsqrt(jnp.mean(x * x, axis=-1, keepdims=True) + RMS_EPS) * g


def _f_rms(x, g):
    return (_rms(x, g),)


def _f_ln_silu(x, g, b):
    mu = jnp.mean(x, axis=-1, keepdims=True)
    xc = x - mu
    var = jnp.mean(xc * xc, axis=-1, keepdims=True)
    y = xc * lax.rsqrt(var + LN_EPS) * g + b
    return (y * _sig(y),)


def _rope128(x, cos_p, sin_p):
    return x * cos_p + pltpu.roll(x, 64, 1) * sin_p


def _rope128_t(d, cos_p, sin_p):
    return d * cos_p + pltpu.roll(d * sin_p, 64, 1)


def _f_rope(xq, xk, cos_p, sin_p):
    heads = [_rope128(xq[:, h * 128:(h + 1) * 128], cos_p, sin_p) for h in range(MLA_HEADS)]
    return (jnp.concatenate(heads, axis=1), _rope128(xk, cos_p, sin_p))


def _f_rope_t(dq, dk_heads, cos_p, sin_p):
    heads = [_rope128_t(dq[:, h * 128:(h + 1) * 128], cos_p, sin_p) for h in range(MLA_HEADS)]
    dk = dk_heads[:, 0:128]
    for h in range(1, MLA_HEADS):
        dk = dk + dk_heads[:, h * 128:(h + 1) * 128]
    return (jnp.concatenate(heads, axis=1), _rope128_t(dk, cos_p, sin_p))


GATE_LANES = 512


def _gate_fwd(ycat, proj, z_col, name, tb=1024):
    T, width = ycat.shape
    zb = z_col // GATE_LANES

    def body(y_ref, z_ref, o_ref, ot_ref):
        z = z_ref[...]
        y = y_ref[...] * (z * _sig(z))
        o_ref[...] = y.astype(o_ref.dtype)
        ot_ref[...] = y.T.astype(ot_ref.dtype)

    blk = pl.BlockSpec((tb, GATE_LANES), lambda i, c: (i, c))
    return _call(body, name, [jax.ShapeDtypeStruct((T, width), BF16), jax.ShapeDtypeStruct((width, T), BF16)],
                 grid=(T // tb, width // GATE_LANES),
                 in_specs=[blk, pl.BlockSpec((tb, GATE_LANES), lambda i, c: (i, zb + c))],
                 out_specs=[blk, pl.BlockSpec((GATE_LANES, tb), lambda i, c: (c, i))],
                 dims=("parallel", "parallel"))(ycat, proj)


def _out_dx_gate_bwd(dh, w_out, ycat, proj, z_col, name, tb=1024, carry=None):
    T, width = ycat.shape
    D = dh.shape[1]
    zb = z_col // GATE_LANES

    def body(dh_ref, w_ref, y_ref, z_ref, dycat_ref, dz_ref):
        d = lax.dot_general(dh_ref[...].astype(BF16), w_ref[...], _DOT_DIMS["nt"], preferred_element_type=F32)
        z = z_ref[...]
        s = _sig(z)
        dycat_ref[...] = d * (z * s)
        dz_ref[...] = (d * y_ref[...] * (s * (1.0 + z * (1.0 - s)))).astype(dz_ref.dtype)

    blk = pl.BlockSpec((tb, GATE_LANES), lambda i, c: (i, c))
    zblk = pl.BlockSpec((tb, GATE_LANES), lambda i, c: (i, zb + c))
    return _call(body, name, [jax.ShapeDtypeStruct((T, width), F32), jax.ShapeDtypeStruct(proj.shape, BF16)],
                 grid=(T // tb, width // GATE_LANES),
                 in_specs=[pl.BlockSpec((tb, D), lambda i, c: (i, 0)), pl.BlockSpec((GATE_LANES, D), lambda i, c: (c, 0)),
                           blk, zblk],
                 out_specs=[blk, zblk], dims=("parallel", "parallel"), carry=carry)(dh, w_out, ycat, proj)


def _glu_bwd(proj, d_glu, d_proj, name, tb=256):
    T, w = d_glu.shape

    def body(a_ref, g_ref, d_ref, _, o_ref):
        s, d = _sig(g_ref[...]), d_ref[...]
        o_ref[:, 0:w] = (d * s).astype(o_ref.dtype)
        o_ref[:, w:2 * w] = (d * a_ref[...] * (s * (1.0 - s))).astype(o_ref.dtype)

    return _call(body, name, jax.ShapeDtypeStruct(d_proj.shape, d_proj.dtype), grid=(T // tb,),
                 in_specs=[pl.BlockSpec((tb, w), lambda i: (i, 0)), pl.BlockSpec((tb, w), lambda i: (i, 1)),
                           pl.BlockSpec((tb, w), lambda i: (i, 0)), pl.BlockSpec(memory_space=pl.ANY)],
                 out_specs=pl.BlockSpec((tb, 2 * w), lambda i: (i, 0)), dims=("parallel",),
                 aliases={3: 0})(proj, proj, d_glu, d_proj)


def _final_loss(h, tgt, g, name, tb=512):
    T, D = h.shape

    def body(h_ref, t_ref, g_ref, dh_ref, dg_ref, loss_ref):
        tv = t_ref[...]

        def rowloss(hh, gg):
            e = _rms(hh, gg) - tv
            return 0.5 * jnp.mean(e * e, axis=-1, keepdims=True)

        lr, vjp = jax.vjp(rowloss, h_ref[...], g_ref[...])
        dh, dg = vjp(jnp.ones_like(lr))
        dh_ref[...] = dh

        @pl.when(pl.program_id(0) == 0)
        def _():
            dg_ref[...] = jnp.zeros_like(dg_ref)
            loss_ref[...] = jnp.zeros_like(loss_ref)

        dg_ref[...] += dg
        loss_ref[...] += jnp.broadcast_to(jnp.sum(lr, axis=0, keepdims=True), loss_ref.shape)

    row = pl.BlockSpec((tb, D), lambda i: (i, 0))
    par = pl.BlockSpec((1, D), lambda i: (0, 0))
    return _call(body, name,
                 [jax.ShapeDtypeStruct((T, D), F32), jax.ShapeDtypeStruct((1, D), F32), jax.ShapeDtypeStruct((1, 128), F32)],
                 grid=(T // tb,), in_specs=[row, row, par],
                 out_specs=[row, par, pl.BlockSpec((1, 128), lambda i: (0, 0))], dims=("arbitrary",))(h, tgt, g)


CONV_ROWS = 128
CONV_LANES = 256


def _sublane_phases(pad, n):
    for r in range(1, 8):
        for c0 in range(0, n - 8, 256):
            rows = min(256, n - 8 - c0)
            pad[r, c0:c0 + rows, :] = pad[0, c0 + r:c0 + r + rows, :]


def _dwconv_fwd(proj, C, w, b, B, S, name, carry=None):
    cb = CONV_LANES
    off = CONV_PAD - (CONV_KERNEL - 1)

    def body(a_ref, g_ref, w_ref, b_ref, o_ref, pad):
        pad[0, 0:CONV_PAD, :] = jnp.zeros((CONV_PAD, cb), F32)
        for c0 in range(0, S, 256):
            pad[0, CONV_PAD + c0:CONV_PAD + c0 + 256, :] = a_ref[c0:c0 + 256, :] * _sig(g_ref[c0:c0 + 256, :])
        _sublane_phases(pad, S + CONV_PAD)
        for t0 in range(0, S, CONV_ROWS):
            acc = jnp.broadcast_to(b_ref[...], (CONV_ROWS, cb))
            for k in range(CONV_KERNEL):
                r, base = (off + k) % 8, t0 + (off + k) // 8 * 8
                acc = acc + w_ref[k:k + 1, :] * pad[r, base:base + CONV_ROWS, :]
            o_ref[t0:t0 + CONV_ROWS, :] = acc

    return _call(body, name, jax.ShapeDtypeStruct((B, S, C), F32), grid=(B, C // cb),
                 in_specs=[pl.BlockSpec((S, cb), lambda i, j: (i, j)), pl.BlockSpec((S, cb), lambda i, j: (i, C // cb + j)),
                           pl.BlockSpec((CONV_KERNEL, cb), lambda i, j: (0, j)),
                           pl.BlockSpec((1, cb), lambda i, j: (0, j))],
                 out_specs=pl.BlockSpec((None, S, cb), lambda i, j: (i, 0, j)),
                 scratch=[pltpu.VMEM((8, S + CONV_PAD, cb), F32)], dims=("parallel", "parallel"),
                 carry=carry)(proj, proj, w, b)


def _dwconv_bwd(proj, w, dy, name, carry=None):
    B, S, C = dy.shape
    cb = CONV_LANES
    groups = CONV_ROWS // 8

    def body(a_ref, g_ref, w_ref, dy_ref, dx_ref, dw_ref, db_ref, dypad, wacc):
        dypad[0, 0:S, :] = dy_ref[...]
        dypad[0, S:, :] = jnp.zeros((CONV_PAD, cb), F32)
        _sublane_phases(dypad, S + CONV_PAD)
        wacc[...] = jnp.zeros_like(wacc)
        for t0 in range(0, S, CONV_ROWS):
            xc = a_ref[t0:t0 + CONV_ROWS, :] * _sig(g_ref[t0:t0 + CONV_ROWS, :])
            acc = jnp.zeros((CONV_ROWS, cb), F32)
            for k in range(CONV_KERNEL):
                o = (CONV_KERNEL - 1) - k
                dys = dypad[o % 8, t0 + o // 8 * 8:t0 + o // 8 * 8 + CONV_ROWS, :]
                acc = acc + w_ref[k:k + 1, :] * dys
                wacc[k] += jnp.sum((dys * xc).reshape(groups, 8, cb), axis=0)
            wacc[CONV_KERNEL] += jnp.sum(dy_ref[t0:t0 + CONV_ROWS, :].reshape(groups, 8, cb), axis=0)
            dx_ref[t0:t0 + CONV_ROWS, :] = acc

        @pl.when(pl.program_id(1) == 0)
        def _():
            dw_ref[...] = jnp.zeros_like(dw_ref)
            db_ref[...] = jnp.zeros_like(db_ref)

        for k in range(CONV_KERNEL):
            dw_ref[k:k + 1, :] += jnp.sum(wacc[k], axis=0, keepdims=True)
        db_ref[...] += jnp.sum(wacc[CONV_KERNEL], axis=0, keepdims=True)

    blk = pl.BlockSpec((None, S, cb), lambda j, i: (i, 0, j))
    return _call(body, name,
                 [jax.ShapeDtypeStruct((B, S, C), F32), jax.ShapeDtypeStruct((CONV_KERNEL, C), F32),
                  jax.ShapeDtypeStruct((1, C), F32)],
                 grid=(C // cb, B),
                 in_specs=[pl.BlockSpec((S, cb), lambda j, i: (i, j)), pl.BlockSpec((S, cb), lambda j, i: (i, C // cb + j)),
                           pl.BlockSpec((CONV_KERNEL, cb), lambda j, i: (0, j)), blk],
                 out_specs=[blk, pl.BlockSpec((CONV_KERNEL, cb), lambda j, i: (0, j)),
                            pl.BlockSpec((1, cb), lambda j, i: (0, j))],
                 scratch=[pltpu.VMEM((8, S + CONV_PAD, cb), F32), pltpu.VMEM((CONV_KERNEL + 1, 8, cb), F32)],
                 dims=("parallel", "arbitrary"), carry=carry)(proj, proj, w, dy)


ATTN_TILE = {"fwd": 1024, "bwd": 1024, "cross fwd": 512}
ATTN_SUB = {"fwd": 256, "bwd": 512}


def _attn_shapes(Sq, Sk, causal, pass_):
    tq = min(Sq, ATTN_TILE[pass_ if causal or pass_ == "bwd" else "cross fwd"])
    tk = tq if causal else min(Sk, ATTN_TILE[pass_])
    return tq, tk, min(ATTN_SUB[pass_], tq)


def _causal_bias(n):
    r = lax.broadcasted_iota(jnp.int32, (n, n), 0)
    c = lax.broadcasted_iota(jnp.int32, (n, n), 1)
    return jnp.where(c <= r, 0.0, NEG).astype(F32)


def _mask_diagonal(s, bias):
    n, nc = s.shape
    if nc == n:
        return s + bias
    return jnp.concatenate([s[:, :nc - n], s[:, nc - n:] + bias], axis=1)


def _attn_fwd(q, q_c0, qr, k, k_c0, kr, v, v_c0, B, Sq, Sk, H, causal, scale, name, into=None, o_c0=0, o_width=None,
              kv_stride=1):
    tq, tk, sub = _attn_shapes(Sq, Sk, causal, "fwd")
    nq, nk, nsub = Sq // tq, Sk // tk, tq // sub
    rope = qr is not None

    def body(*refs):
        refs = list(refs)
        qn_ref = refs.pop(0)
        qr_ref = refs.pop(0) if rope else None
        kn_ref = refs.pop(0)
        kr_ref = refs.pop(0) if rope else None
        v_ref = refs.pop(0)
        if into is not None:
            refs.pop(0)
        o_ref, lse_ref, m_s, l_s, acc = refs
        qi = pl.program_id(2)
        m_s[...] = jnp.full_like(m_s, NEG)
        l_s[...] = jnp.zeros_like(l_s)
        acc[...] = jnp.zeros_like(acc)
        bias = _causal_bias(sub) if causal else None
        qs = []
        for r in range(nsub):
            qn = qn_ref[r * sub:(r + 1) * sub, :].astype(BF16)
            qs.append(jnp.concatenate([qn, qr_ref[r * sub:(r + 1) * sub, :]], axis=1) if rope else qn)

        def step(j, masked):
            ks = pl.ds(pl.multiple_of(j * tk, tk), tk)
            kk = jnp.concatenate([kn_ref[ks, :], kr_ref[ks, :]], axis=1) if rope else kn_ref[ks, :]
            vv = v_ref[ks, :]
            for r in range(nsub):
                rows = slice(r * sub, (r + 1) * sub)
                nc = (r + 1) * sub if masked else tk
                s = lax.dot_general(qs[r], kk[:nc], _DOT_DIMS["nt"], preferred_element_type=F32) * scale
                if masked:
                    s = _mask_diagonal(s, bias)
                m_old = m_s[rows, :]
                m_new = jnp.maximum(m_old, jnp.max(s, axis=-1, keepdims=True))
                p = jnp.exp(s - m_new)
                alpha = jnp.exp(m_old - m_new)
                l_s[rows, :] = alpha * l_s[rows, :] + jnp.sum(p, axis=-1, keepdims=True)
                acc[rows, :] = alpha * acc[rows, :] + jnp.dot(p.astype(BF16), vv[:nc], preferred_element_type=F32)
                m_s[rows, :] = m_new

        def unmasked(j, carry):
            step(j, False)
            return carry

        if causal:
            lax.fori_loop(0, qi, unmasked, 0)
            step(qi, True)
        else:
            lax.fori_loop(0, nk, unmasked, 0)
        o_ref[...] = (acc[...] / l_s[...]).astype(o_ref.dtype)
        lse_ref[...] = m_s[...] + jnp.log(l_s[...])

    qspec = lambda c0: pl.BlockSpec((tq, 128), lambda b, h, i: (b * nq + i, c0 + h))
    kspec = lambda c0: pl.BlockSpec((Sk, 128), lambda b, h, i: (b, c0 + kv_stride * h))
    in_specs, args = [qspec(q_c0)], [q]
    if rope:
        in_specs.append(qspec(0)); args.append(qr)
    in_specs.append(kspec(k_c0)); args.append(k)
    if rope:
        in_specs.append(pl.BlockSpec((Sk, 128), lambda b, h, i: (b, 0))); args.append(kr)
    in_specs.append(kspec(v_c0)); args.append(v)
    aliases = {}
    if into is not None:
        aliases = {len(args): 0}
        in_specs.append(pl.BlockSpec(memory_space=pl.ANY)); args.append(into)
        o_shape = jax.ShapeDtypeStruct(into.shape, into.dtype)
    else:
        o_shape = jax.ShapeDtypeStruct((B * Sq, o_width), F32)
    return _call(body, name, [o_shape, jax.ShapeDtypeStruct((B * H, Sq, 1), F32)], grid=(B, H, nq), in_specs=in_specs,
                 out_specs=[qspec(o_c0), pl.BlockSpec((None, tq, 1), lambda b, h, i: (b * H + h, i, 0))],
                 scratch=[pltpu.VMEM((tq, 1), F32), pltpu.VMEM((tq, 1), F32), pltpu.VMEM((tq, 128), F32)],
                 dims=("parallel", "parallel", "arbitrary"), aliases=aliases)(*args)


def _attn_bwd(q, q_c0, qr, k, k_c0, kr, v, v_c0, o, do, o_c0, lse, B, Sq, Sk, H, causal, scale, name, dq_into=None,
              kv_stride=1):
    tq, tk, sub = _attn_shapes(Sq, Sk, causal, "bwd")
    nq, nk, nsub = Sq // tq, Sk // tk, tq // sub
    rope = qr is not None
    dk_w = 256 if rope else 128

    def body(*refs):
        refs = list(refs)
        qn_ref = refs.pop(0)
        qr_ref = refs.pop(0) if rope else None
        kn_ref = refs.pop(0)
        kr_ref = refs.pop(0) if rope else None
        v_ref, o_ref, do_ref, lse_ref = refs[:4]
        refs = refs[4 + (0 if dq_into is None else 1):]
        dqn_ref = refs.pop(0)
        dqr_ref = refs.pop(0) if rope else None
        dkn_ref = refs.pop(0)
        dkr_ref = refs.pop(0) if rope else None
        dv_ref = None if rope else refs.pop(0)
        q_s, do_s, dl_s, dq_acc, dk_acc, dv_acc = refs
        kj = pl.program_id(2)

        @pl.when(kj == 0)
        def _():
            qn = qn_ref[...].astype(BF16)
            q_s[...] = jnp.concatenate([qn, qr_ref[...]], axis=1) if rope else qn
            dof = do_ref[...]
            do_s[...] = dof.astype(BF16)
            dl_s[...] = jnp.sum(dof * o_ref[...], axis=-1, keepdims=True)
            dq_acc[...] = jnp.zeros_like(dq_acc)

        kk = jnp.concatenate([kn_ref[...], kr_ref[...]], axis=1) if rope else kn_ref[...]
        vv = v_ref[...]
        bias = _causal_bias(sub) if causal else None
        dk_acc[...] = jnp.zeros_like(dk_acc)
        dv_acc[...] = jnp.zeros_like(dv_acc)

        def step(i, masked):
            for r in range(nsub):
                rows = pl.ds(pl.multiple_of(i * tq + r * sub, sub), sub)
                qq, dob = q_s[rows, :], do_s[rows, :]
                nc = (r + 1) * sub if masked else tk
                kc, vc = kk[:nc], vv[:nc]
                s = lax.dot_general(qq, kc, _DOT_DIMS["nt"], preferred_element_type=F32) * scale
                if masked:
                    s = _mask_diagonal(s, bias)
                p = jnp.exp(s - lse_ref[rows, :])
                dp = lax.dot_general(dob, vc, _DOT_DIMS["nt"], preferred_element_type=F32)
                ds = (p * (dp - dl_s[rows, :]) * scale).astype(BF16)
                dv_acc[0:nc, :] += lax.dot_general(p.astype(BF16), dob, _DOT_DIMS["tn"], preferred_element_type=F32)
                dk_acc[0:nc, :] += lax.dot_general(ds, qq, _DOT_DIMS["tn"], preferred_element_type=F32)
                dq_acc[rows, :] += jnp.dot(ds, kc, preferred_element_type=F32)

        def unmasked(i, carry):
            step(i, False)
            return carry

        if causal:
            step(kj, True)
            lax.fori_loop(kj + 1, nq, unmasked, 0)
        else:
            lax.fori_loop(0, nq, unmasked, 0)
        if rope:
            dkn_ref[...] = jnp.concatenate([dk_acc[:, 0:128], dv_acc[...]], axis=1).astype(dkn_ref.dtype)
            dkr_ref[...] = dk_acc[:, 128:256]
        else:
            dkn_ref[...] = dk_acc[...]
            dv_ref[...] = dv_acc[...]

        @pl.when(kj == nk - 1)
        def _():
            dqn_ref[...] = dq_acc[:, 0:128].astype(dqn_ref.dtype)
            if rope:
                dqr_ref[...] = dq_acc[:, 128:256]

    qspec = lambda c0: pl.BlockSpec((Sq, 128), lambda b, h, j: (b, c0 + h))
    kspec = lambda c0: pl.BlockSpec((tk, 128), lambda b, h, j: (b * nk + j, c0 + kv_stride * h))
    in_specs, args = [qspec(q_c0)], [q]
    if rope:
        in_specs.append(qspec(0)); args.append(qr)
    in_specs.append(kspec(k_c0)); args.append(k)
    if rope:
        in_specs.append(pl.BlockSpec((tk, 128), lambda b, h, j: (b * nk + j, 0))); args.append(kr)
    in_specs += [kspec(v_c0), qspec(o_c0), qspec(o_c0), pl.BlockSpec((None, Sq, 1), lambda b, h, j: (b * H + h, 0, 0))]
    args += [v, o, do, lse]
    h_rows_q = jax.ShapeDtypeStruct((B * Sq, H * 128), F32)
    h_rows_k = jax.ShapeDtypeStruct((B * Sk, H * 128), F32)
    out_shape, out_specs, aliases = [h_rows_q], [qspec(0)], None
    if rope:
        out_shape = [jax.ShapeDtypeStruct((B * Sq, 2 * H * 128), BF16)]
    if dq_into is not None:
        aliases = {len(args): 0}
        in_specs.append(pl.BlockSpec(memory_space=pl.ANY)); args.append(dq_into[0])
        out_shape, out_specs = [jax.ShapeDtypeStruct(dq_into[0].shape, dq_into[0].dtype)], [qspec(dq_into[1])]
    if rope:
        out_shape.append(h_rows_q); out_specs.append(qspec(0))
    hspec = lambda w: pl.BlockSpec((tk, w), lambda b, h, j: (b * nk + j, h))
    if rope:
        out_shape += [jax.ShapeDtypeStruct((B * Sk, H * 256), BF16), h_rows_k]
        out_specs += [hspec(256), hspec(128)]
    else:
        out_shape += [h_rows_k, h_rows_k]
        out_specs += [hspec(128), hspec(128)]
    return _call(body, name, out_shape, grid=(B, H, nk), in_specs=in_specs, out_specs=out_specs,
                 scratch=[pltpu.VMEM((Sq, dk_w), BF16), pltpu.VMEM((Sq, 128), BF16), pltpu.VMEM((Sq, 1), F32),
                          pltpu.VMEM((Sq, dk_w), F32), pltpu.VMEM((tk, dk_w), F32), pltpu.VMEM((tk, 128), F32)],
                 dims=("parallel", "parallel", "arbitrary"), aliases=aliases)(*args)


def _mem_attention_fwd(proj, q_col, ycat, mem2, mem_g, w_mem, B, S, tag):
    M = mem2.shape[0] // B
    (memn,) = _rowwise(_f_rms, [mem2], [mem_g], [(mem2.shape[1], BF16)], tag + "_memnorm")
    kvm = _mm(memn, w_mem, "nn", BF16, tag + "_memkv")
    o_c0 = ycat.shape[1] // 128 - MEM_HEADS
    ycat, lse = _attn_fwd(proj, q_col // 128, None, kvm, 0, None, kvm, MEM_HEADS, B, S, M, MEM_HEADS, False,
                          MEM_HEAD_DIM ** -0.5, tag + "_memattn", into=ycat, o_c0=o_c0)
    return ycat, (memn, kvm, lse)


def _mem_attention_bwd(proj, q_col, ycat, d_ycat, d_proj, saved, mem2, mem_g, w_mem, B, S, tag):
    memn, kvm, lse = saved
    M = mem2.shape[0] // B
    o_c0 = ycat.shape[1] // 128 - MEM_HEADS
    d_q, d_k, d_v = _attn_bwd(proj, q_col // 128, None, kvm, 0, None, kvm, MEM_HEADS, ycat, d_ycat, o_c0, lse, B, S, M,
                              MEM_HEADS, False, MEM_HEAD_DIM ** -0.5, tag + "_memattn_bwd", dq_into=(d_proj, q_col // 128))
    d_kvm = jnp.concatenate([d_k, d_v], axis=1).astype(BF16)
    d_w_mem = _mm(memn, d_kvm, "tn", F32, tag + "_memkv_dw")
    d_memn = _mm(d_kvm, w_mem, "nt", F32, tag + "_memkv_dx")
    _, d_mem_g = _rowwise_bwd(_f_rms, [mem2], [mem_g], [d_memn], 1, tag + "_memnorm_bwd")
    return d_q, d_w_mem, d_mem_g


def _rope_tables(positions):
    inv_freq = 1.0 / (ROPE_THETA ** (jnp.arange(0, MLA_ROPE, 2, dtype=F32) / MLA_ROPE))
    ang = positions.astype(F32).reshape(-1, 1) * inv_freq
    cos, sin, zero = jnp.cos(ang), jnp.sin(ang), jnp.zeros_like(ang)
    return jnp.concatenate([cos, zero, cos, zero], axis=1), jnp.concatenate([-sin, zero, sin, zero], axis=1)


def _forward_backward(x, mem, positions, target, W):
    B, S, D = x.shape
    T = B * S
    conv_w = W["conv_dw"].shape[1]
    mix_w = 2 * D
    h0 = x.reshape(T, D)
    mem2 = mem.reshape(-1, D)
    tgt = target.reshape(T, D)
    row = lambda v: v.reshape(1, -1)
    n_nope = MLA_HEADS * MLA_NOPE

    g0 = row(W["norm_g"][0])
    (u0,) = _rowwise(_f_rms, [h0], [g0], [(D, BF16)], "l0_norm", carry=W.carry("l0_norm"))
    proj0 = _mm(u0, W["conv_w_in"], "nn", F32, "l0_in", carry=W.carry("l0_in"))
    qm0_col, z0_col = 2 * conv_w, 2 * conv_w + MEM_WIDTH
    dw, dwb = W["conv_dw"], row(W["conv_dw_b"][0])
    cv = _dwconv_fwd(proj0, conv_w, dw, dwb, B, S, "l0_dwconv", carry=W.carry("l0_dwconv")).reshape(T, conv_w)
    ln_g, ln_b = row(W["conv_ln_g"][0]), row(W["conv_ln_b"][0])
    (ycat0,) = _rowwise(_f_ln_silu, [cv], [ln_g, ln_b], [(conv_w, F32, mix_w)], "l0_ln", carry=W.carry("l0_ln"))
    mg0 = row(W["mem_norm_g"][0])
    ycat0, mem_saved0 = _mem_attention_fwd(proj0, qm0_col, ycat0, mem2, mg0, W["w_mem_kv"][0], B, S, "l0")
    y0, y0_t = _gate_fwd(ycat0, proj0, z0_col, "l0_gate")
    h1 = _mm(y0, W["w_out"][0], "nn", F32, "l0_out", res=h0, carry=W.carry("l0_out"))

    g1 = row(W["norm_g"][1])
    (u1,) = _rowwise(_f_rms, [h1], [g1], [(D, BF16)], "l1_norm")
    proj1 = _mm(u1, W["mla_w_in"], "nn", F32, "l1_in")
    z1_col = Q_RANK
    qm1_col = z1_col + mix_w
    ckv_col = qm1_col + MEM_WIDTH
    kr_col = ckv_col + KV_RANK
    cq, ckv = (proj1, Q_RANK, 0), (proj1, KV_RANK, ckv_col // KV_RANK)
    qg, kvg = row(W["mla_q_norm_g"]), row(W["mla_kv_norm_g"])
    (cqn,) = _rowwise(_f_rms, [cq], [qg], [(Q_RANK, BF16)], "l1_qnorm")
    (ckvn,) = _rowwise(_f_rms, [ckv], [kvg], [(KV_RANK, BF16)], "l1_kvnorm")
    qf = _mm(cqn, W["mla_w_uq"], "nn", F32, "l1_uq")
    kvf = _mm(ckvn, W["mla_w_ukv"], "nn", BF16, "l1_ukv")
    cos_p, sin_p = _rope_tables(positions)
    qr, kr = _rowwise(_f_rope, [(qf, n_nope, 1), (proj1, 128, kr_col // 128), cos_p, sin_p], [],
                      [(n_nope, BF16), (128, BF16)], "l1_rope")
    scale1 = MLA_QK ** -0.5
    ycat1, lse1 = _attn_fwd(qf, 0, qr, kvf, 0, kr, kvf, 1, B, S, S, MLA_HEADS, True, scale1, "l1_attn",
                            o_width=mix_w, kv_stride=2)
    mg1 = row(W["mem_norm_g"][1])
    ycat1, mem_saved1 = _mem_attention_fwd(proj1, qm1_col, ycat1, mem2, mg1, W["w_mem_kv"][1], B, S, "l1")
    y1, y1_t = _gate_fwd(ycat1, proj1, z1_col, "l1_gate")
    h2 = _mm(y1, W["w_out"][1], "nn", F32, "l1_out", res=h1)

    gf = row(W["final_norm_g"])
    dh2, d_gf, loss128 = _final_loss(h2, tgt, gf, "final_loss")
    G = {"final_norm_g": d_gf.reshape(-1)}
    L1 = {}

    d_wout1 = _mm(y1_t, dh2, "nn", F32, "l1_out_dw")
    d_ycat1, d_proj1 = _out_dx_gate_bwd(dh2, W["w_out"][1], ycat1, proj1, z1_col, "l1_out_dx")
    d_proj1, d_wmem1, d_mg1 = _mem_attention_bwd(proj1, qm1_col, ycat1, d_ycat1, d_proj1, mem_saved1, mem2, mg1,
                                                 W["w_mem_kv"][1], B, S, "l1")
    d_qf, d_qr, d_kvf, d_kr_heads = _attn_bwd(qf, 0, qr, kvf, 0, kr, kvf, 1, ycat1, d_ycat1, 0, lse1, B, S, S,
                                              MLA_HEADS, True, scale1, "l1_attn_bwd", kv_stride=2)
    d_qf, d_proj1 = _rowwise(_f_rope_t, [d_qr, d_kr_heads, cos_p, sin_p], [], [(n_nope, F32), (128, F32)], "l1_rope_bwd",
                             into=[(0, d_qf, 1), (1, d_proj1, kr_col // 128)])
    d_cqn = _mm(d_qf, W["mla_w_uq"], "nt", F32, "l1_uq_dx")
    L1[("mla_w_uq", None)] = _mm(cqn, d_qf, "tn", F32, "l1_uq_dw")
    d_ckvn = _mm(d_kvf, W["mla_w_ukv"], "nt", F32, "l1_ukv_dx")
    L1[("mla_w_ukv", None)] = _mm(ckvn, d_kvf, "tn", F32, "l1_ukv_dw")
    d_proj1, d_qg = _rowwise_bwd(_f_rms, [cq], [qg], [d_cqn], 1, "l1_qnorm_bwd", into=(d_proj1, cq[2]))
    d_proj1, d_kvg = _rowwise_bwd(_f_rms, [ckv], [kvg], [d_ckvn], 1, "l1_kvnorm_bwd", into=(d_proj1, ckv[2]))
    L1[("w_mem_kv", 1)] = d_wmem1
    L1[("mla_w_in", None)] = _mm(u1, d_proj1, "tn", F32, "l1_in_dw")
    L1[("w_out", 1)] = d_wout1
    W.ready("l1", L1)
    d_u1 = _mm(d_proj1, W["mla_w_in"], "nt", F32, "l1_in_dx", carry=W.carry("l1_in_dx"))
    dh1, d_g1 = _rowwise_bwd(_f_rms, [h1], [g1], [d_u1], 1, "l1_norm_bwd", add=dh2)

    d_wout0 = _mm(y0_t, dh1, "nn", F32, "l0_out_dw")
    d_ycat0, d_proj0 = _out_dx_gate_bwd(dh1, W["w_out"][0], ycat0, proj0, z0_col, "l0_out_dx", carry=W.carry("l0_out_dx"))
    d_proj0, d_wmem0, d_mg0 = _mem_attention_bwd(proj0, qm0_col, ycat0, d_ycat0, d_proj0, mem_saved0, mem2, mg0,
                                                 W["w_mem_kv"][0], B, S, "l0")
    W.ready("l0a", {("w_mem_kv", 0): d_wmem0, ("w_out", 0): d_wout0})
    d_cv, d_ln_g, d_ln_b = _rowwise_bwd(_f_ln_silu, [cv], [ln_g, ln_b], [(d_ycat0, conv_w, 0)], 1, "l0_ln_bwd",
                                        carry=W.carry("l0_ln_bwd"))
    d_glu, d_dw, d_dwb = _dwconv_bwd(proj0, dw, d_cv.reshape(B, S, conv_w), "l0_dwconv_bwd",
                                     carry=W.carry("l0_dwconv_bwd"))
    d_proj0 = _glu_bwd(proj0, d_glu.reshape(T, conv_w), d_proj0, "l0_glu_bwd")
    d_conv_w_in = _mm(u0, d_proj0, "tn", F32, "l0_in_dw", carry=W.carry("l0_in_dw"))
    W.ready("l0b", {("conv_w_in", None): d_conv_w_in, ("conv_dw", None): d_dw,
                    ("mla_q_norm_g", None): d_qg.reshape(-1), ("mla_kv_norm_g", None): d_kvg.reshape(-1)})
    d_u0 = _mm(d_proj0, W["conv_w_in"], "nt", F32, "l0_in_dx", carry=W.carry("l0_in_dx"))
    dx, d_g0 = _rowwise_bwd(_f_rms, [h0], [g0], [d_u0], 1, "l0_norm_bwd", add=dh1)
    dx = dx.reshape(B, S, D)

    G["norm_g"] = jnp.concatenate([d_g0, d_g1], axis=0)
    G["mem_norm_g"] = jnp.concatenate([d_mg0, d_mg1], axis=0)
    G["conv_dw_b"] = d_dwb
    G["conv_ln_g"], G["conv_ln_b"] = d_ln_g, d_ln_b
    return loss128[0, 0], dx, G


def _mla_in_perm(w):
    c1, c2 = Q_RANK, Q_RANK + KV_RANK
    c3 = c2 + MLA_ROPE
    c4 = c3 + MEM_WIDTH
    zero = jnp.zeros((w.shape[0], HALF_ROPE), w.dtype)
    return jnp.concatenate([w[:, :c1], w[:, c4:], w[:, c3:c4], w[:, c1:c2], w[:, c2:c2 + HALF_ROPE], zero,
                            w[:, c2 + HALF_ROPE:c3], zero], axis=1)


def _mla_in_unperm(g):
    z_w = g.shape[1] - (Q_RANK + MEM_WIDTH + KV_RANK + 128)
    z0, q0 = Q_RANK, Q_RANK + z_w
    k0 = q0 + MEM_WIDTH
    r = k0 + KV_RANK
    return jnp.concatenate([g[:, :Q_RANK], g[:, k0:r], g[:, r:r + HALF_ROPE], g[:, r + 64:r + 64 + HALF_ROPE],
                            g[:, q0:k0], g[:, z0:q0]], axis=1)


def _uq_perm(w):
    n = w.shape[0]
    w3 = w.reshape(n, MLA_HEADS, MLA_QK)
    zero = jnp.zeros((n, MLA_HEADS, HALF_ROPE), w.dtype)
    rope = jnp.concatenate([w3[:, :, MLA_NOPE:MLA_NOPE + HALF_ROPE], zero, w3[:, :, MLA_NOPE + HALF_ROPE:], zero], axis=2)
    return jnp.concatenate([w3[:, :, :MLA_NOPE].reshape(n, -1), rope.reshape(n, -1)], axis=1)


def _uq_unperm(g):
    n = g.shape[0]
    n_nope = MLA_HEADS * MLA_NOPE
    rope = g[:, n_nope:].reshape(n, MLA_HEADS, 128)
    return jnp.concatenate([g[:, :n_nope].reshape(n, MLA_HEADS, MLA_NOPE), rope[:, :, :HALF_ROPE],
                            rope[:, :, 64:64 + HALF_ROPE]], axis=2).reshape(n, -1)


_ROW_CUT = ("w_mem_kv", "w_out")
_COL_CUT = ("conv_w_in", "mla_w_in", "mla_w_uq", "mla_w_ukv", "conv_dw")
_BIG = ("w_mem_kv", "w_out", "conv_w_in", "mla_w_in", "mla_w_uq", "mla_w_ukv")
_SMALL_SHARDED = ("conv_dw", "mla_q_norm_g", "mla_kv_norm_g")
_REPLICATED = ("norm_g", "mem_norm_g", "conv_dw_b", "conv_ln_g", "conv_ln_b", "final_norm_g")
_PERM = {"mla_w_in": (_mla_in_perm, _mla_in_unperm), "mla_w_uq": (_uq_perm, _uq_unperm)}


def _join(n, blocks):
    if n in _ROW_CUT:
        _, L, r, c = blocks.shape
        return blocks.transpose(1, 0, 2, 3).reshape(L, N_DEV * r, c)
    if n in _COL_CUT:
        _, _, r, c = blocks.shape
        return blocks.reshape(N_DEV, r, c).transpose(1, 0, 2).reshape(r, N_DEV * c)
    return blocks.reshape(-1)


def _cut(n, full, shard_shape):
    if n in _ROW_CUT:
        L, r, c = shard_shape
        return full.reshape(L, N_DEV, r, c).transpose(1, 0, 2, 3)
    if n in _COL_CUT:
        _, r, c = shard_shape
        return full.reshape(r, N_DEV, c).transpose(1, 0, 2).reshape(N_DEV, 1, r, c)
    return full.reshape(N_DEV, 1, -1)


def _flat_pad(parts, size):
    flat = jnp.concatenate([p.reshape(-1) for p in parts])
    return jnp.concatenate([flat, jnp.zeros((size - flat.shape[0],), flat.dtype)])


SMALL_LANES = 128 * 8


def _as_tiles(flat_parts):
    total = sum(p.size for p in flat_parts)
    size = -(-total // SMALL_LANES) * SMALL_LANES
    return _flat_pad(flat_parts, size).reshape(8, size // 8)


def _split_flat(flat, like):
    out, o = [], 0
    for a in like:
        out.append(flat[o:o + a.size].reshape(a.shape))
        o += a.size
    return out


_HBM = pl.BlockSpec(memory_space=pltpu.HBM)
_VMEM = pl.BlockSpec(memory_space=pltpu.VMEM)


def _position():
    return lax.axis_index("x"), lax.axis_index("y"), lax.axis_index("c")


def _dma_sems(n):
    return [pltpu.SemaphoreType.DMA((n,)), pltpu.SemaphoreType.DMA((n,))]


def _run_stage(stage, name):
    n_in, n_out = len(stage.ins), len(stage.out_shapes)

    def body(*refs):
        ins, outs, sems = refs[:n_in], refs[n_in:n_in + n_out], refs[n_in + n_out:]
        stage.start(ins, outs, sems)
        stage.wait(ins, outs, sems)

    outs = _call(body, name, stage.out_shapes, in_specs=[_HBM] * n_in, out_specs=[_HBM] * n_out, scratch=stage.sems,
                 aliases=stage.aliases)(*stage.ins)
    stage.outs = list(outs)
    return stage.outs


def _gather_chips_stage(shards):
    n = len(shards)

    def copies(x_refs, out_refs, sems):
        send_sems, recv_sems, _ = sems
        x, y, c = _position()
        peers = [(x, y, 1 - c), (1 - x, y, c), (x, 1 - y, c), (1 - x, 1 - y, c)]
        out = []
        for a in range(n):
            for k, (px, py, pc) in enumerate(peers):
                send = pltpu.make_async_remote_copy(src_ref=x_refs[a], dst_ref=out_refs[a].at[4 * x + 2 * y + c],
                                                    send_sem=send_sems.at[4 * a + k], recv_sem=recv_sems.at[4 * a + k],
                                                    device_id=(px, py, pc), device_id_type=MESH)
                recv = pltpu.make_async_remote_copy(src_ref=x_refs[a], dst_ref=out_refs[a].at[4 * px + 2 * py + pc],
                                                    send_sem=send_sems.at[4 * a + k], recv_sem=recv_sems.at[4 * a + k],
                                                    device_id=(px, py, pc), device_id_type=MESH)
                out.append((send, recv))
        return out

    def local(x_refs, out_refs, sems):
        x, y, c = _position()
        return [pltpu.make_async_copy(x_refs[a], out_refs[a].at[4 * x + 2 * y + c], sems[2].at[a]) for a in range(n)]

    def start(x_refs, out_refs, sems):
        for cp in local(x_refs, out_refs, sems):
            cp.start()
        for send, _ in copies(x_refs, out_refs, sems):
            send.start()

    def wait(x_refs, out_refs, sems):
        for send, recv in copies(x_refs, out_refs, sems):
            recv.wait_recv()
            send.wait_send()
        for cp in local(x_refs, out_refs, sems):
            cp.wait()

    return _Stage(shards, [jax.ShapeDtypeStruct((N_DEV,) + a.shape, a.dtype) for a in shards],
                  _dma_sems(4 * n) + [pltpu.SemaphoreType.DMA((n,))], start, wait)


def _gather_sibling_stage(bufs):
    n = len(bufs)

    def copies(out_refs, sems):
        send_sems, recv_sems = sems
        x, y, c = _position()
        out = []
        for a in range(n):
            for j, (px, py) in enumerate([(1 - x, y), (x, 1 - y), (1 - x, 1 - y)]):
                mine, theirs = out_refs[a].at[4 * px + 2 * py + c], out_refs[a].at[4 * px + 2 * py + (1 - c)]
                send = pltpu.make_async_remote_copy(src_ref=mine, dst_ref=mine, send_sem=send_sems.at[3 * a + j],
                                                    recv_sem=recv_sems.at[3 * a + j], device_id=(x, y, 1 - c),
                                                    device_id_type=MESH)
                recv = pltpu.make_async_remote_copy(src_ref=mine, dst_ref=theirs, send_sem=send_sems.at[3 * a + j],
                                                    recv_sem=recv_sems.at[3 * a + j], device_id=(x, y, 1 - c),
                                                    device_id_type=MESH)
                out.append((send, recv))
        return out

    def start(_, out_refs, sems):
        for send, _r in copies(out_refs, sems):
            send.start()

    def wait(_, out_refs, sems):
        for send, recv in copies(out_refs, sems):
            recv.wait_recv()
            send.wait_send()

    return _Stage(bufs, [jax.ShapeDtypeStruct(b.shape, b.dtype) for b in bufs], _dma_sems(3 * n), start, wait,
                  aliases={a: a for a in range(n)})


def _all_gather_small(v, name):
    r, n = v.shape

    def body(x_ref, out_ref, send_sems, recv_sems, local_sem):
        x, y, c = _position()
        me = 4 * x + 2 * y + c
        mine = pltpu.make_async_copy(x_ref, out_ref.at[me], local_sem)
        mine.start()
        flips = [(fx, fy, fc) for fx in (0, 1) for fy in (0, 1) for fc in (0, 1)][1:]
        copies = []
        for k, (fx, fy, fc) in enumerate(flips):
            peer = (x ^ fx, y ^ fy, c ^ fc)
            cp = pltpu.make_async_remote_copy(src_ref=x_ref, dst_ref=out_ref.at[me], send_sem=send_sems.at[k],
                                              recv_sem=recv_sems.at[k], device_id=peer, device_id_type=MESH)
            cp.start()
            copies.append(cp)
        for k, (fx, fy, fc) in enumerate(flips):
            px, py, pc = x ^ fx, y ^ fy, c ^ fc
            src = out_ref.at[4 * px + 2 * py + pc]
            pltpu.make_async_remote_copy(src_ref=x_ref, dst_ref=src, send_sem=send_sems.at[k], recv_sem=recv_sems.at[k],
                                         device_id=(px, py, pc), device_id_type=MESH).wait_recv()
        for cp in copies:
            cp.wait_send()
        mine.wait()

    return _call(body, name, jax.ShapeDtypeStruct((N_DEV, r, n), v.dtype), in_specs=[_VMEM], out_specs=_VMEM,
                 scratch=_dma_sems(7) + [pltpu.SemaphoreType.DMA(())])(v)


def _reduce_sibling_stage(gs):
    n = len(gs)

    def copies(g_refs, out_refs, sems):
        send_sems, recv_sems = sems
        x, y, c = _position()
        return [pltpu.make_async_remote_copy(src_ref=g_refs[a].at[2 * k + (1 - c)], dst_ref=out_refs[a].at[k],
                                             send_sem=send_sems.at[4 * a + k], recv_sem=recv_sems.at[4 * a + k],
                                             device_id=(x, y, 1 - c), device_id_type=MESH)
                for a in range(n) for k in range(4)]

    def start(g_refs, out_refs, sems):
        for cp in copies(g_refs, out_refs, sems):
            cp.start()

    def wait(g_refs, out_refs, sems):
        for cp in copies(g_refs, out_refs, sems):
            cp.wait()

    return _Stage(gs, [jax.ShapeDtypeStruct((4,) + g.shape[1:], g.dtype) for g in gs], _dma_sems(4 * n), start, wait)


def _rows2d(shape):
    cols = shape[-1]
    rows = 1
    for s in shape[:-1]:
        rows *= s
    return rows, cols


def _add_own(g, recv, name):
    rows, cols = _rows2d(g.shape[1:])
    tr = _pick(rows, 256, 8)
    c = lax.axis_index("c").astype(jnp.int32).reshape(1)

    def body(c_ref, g_ref, r_ref, o_ref):
        o_ref[...] = (g_ref[...].astype(F32) + r_ref[...].astype(F32)).astype(o_ref.dtype)

    grid_spec = pltpu.PrefetchScalarGridSpec(
        num_scalar_prefetch=1, grid=(4, rows // tr),
        in_specs=[pl.BlockSpec((None, None, tr, cols), lambda k, i, c_ref: (k, c_ref[0], i, 0)),
                  pl.BlockSpec((None, tr, cols), lambda k, i, c_ref: (k, i, 0))],
        out_specs=pl.BlockSpec((None, tr, cols), lambda k, i, c_ref: (k, i, 0)))
    return _call(body, name, jax.ShapeDtypeStruct((4, rows, cols), g.dtype), grid_spec=grid_spec,
                 dims=("parallel", "parallel"))(c, g.reshape(4, 2, rows, cols), recv.reshape(4, rows, cols))


def _reduce_chips_stage(pas):
    n = len(pas)

    def copies(pa_refs, out_refs, sems):
        send_sems, recv_sems, _ = sems
        x, y, c = _position()
        my_chip = 2 * x + y
        out = []
        for a in range(n):
            for j, (px, py) in enumerate([(1 - x, y), (x, 1 - y), (1 - x, 1 - y)]):
                send = pltpu.make_async_remote_copy(src_ref=pa_refs[a].at[2 * px + py], dst_ref=out_refs[a].at[my_chip],
                                                    send_sem=send_sems.at[3 * a + j], recv_sem=recv_sems.at[3 * a + j],
                                                    device_id=(px, py, c), device_id_type=MESH)
                recv = pltpu.make_async_remote_copy(src_ref=pa_refs[a].at[2 * px + py], dst_ref=out_refs[a].at[2 * px + py],
                                                    send_sem=send_sems.at[3 * a + j], recv_sem=recv_sems.at[3 * a + j],
                                                    device_id=(px, py, c), device_id_type=MESH)
                out.append((send, recv))
        return out

    def local(pa_refs, out_refs, sems):
        x, y, _ = _position()
        return [pltpu.make_async_copy(pa_refs[a].at[2 * x + y], out_refs[a].at[2 * x + y], sems[2].at[a]) for a in range(n)]

    def start(pa_refs, out_refs, sems):
        for cp in local(pa_refs, out_refs, sems):
            cp.start()
        for send, _r in copies(pa_refs, out_refs, sems):
            send.start()

    def wait(pa_refs, out_refs, sems):
        for send, recv in copies(pa_refs, out_refs, sems):
            recv.wait_recv()
            send.wait_send()
        for cp in local(pa_refs, out_refs, sems):
            cp.wait()

    return _Stage(pas, [jax.ShapeDtypeStruct(pa.shape, pa.dtype) for pa in pas],
                  _dma_sems(3 * n) + [pltpu.SemaphoreType.DMA((n,))], start, wait)


def _adamw_math(w, g, m, v):
    m = ADAM_B1 * m + (1.0 - ADAM_B1) * g
    v = ADAM_B2 * v + (1.0 - ADAM_B2) * (g * g)
    m_hat = m / (1.0 - ADAM_B1 ** ADAM_STEP)
    v_hat = v / (1.0 - ADAM_B2 ** ADAM_STEP)
    delta = -ADAM_LR * (m_hat / (jnp.sqrt(v_hat) + ADAM_EPS) + ADAM_WD * w)
    return delta, m, v


def _sum_adamw(parts, w, m, v, name):
    n, rows, cols = parts.shape
    tr = _pick(rows, 128, 8)

    def body(p_ref, w_ref, m_ref, v_ref, g_ref, d_ref, nm_ref, nv_ref):
        g = p_ref[0].astype(F32)
        for k in range(1, n):
            g = g + p_ref[k].astype(F32)
        d, nm, nv = _adamw_math(w_ref[...], g, m_ref[...], v_ref[...])
        g_ref[...], d_ref[...], nm_ref[...], nv_ref[...] = g, d, nm, nv

    blk = pl.BlockSpec((tr, cols), lambda i: (i, 0))
    return _call(body, name, [jax.ShapeDtypeStruct((rows, cols), F32)] * 4, grid=(rows // tr,),
                 in_specs=[pl.BlockSpec((n, tr, cols), lambda i: (0, i, 0)), blk, blk, blk],
                 out_specs=[blk] * 4, dims=("parallel",))(parts, w, m, v)


_WEIGHTS = ("norm_g", "mem_norm_g", "w_mem_kv", "w_out", "conv_w_in", "conv_dw", "conv_dw_b", "conv_ln_g", "conv_ln_b",
            "mla_w_in", "mla_q_norm_g", "mla_w_uq", "mla_kv_norm_g", "mla_w_ukv", "final_norm_g")


_GATHER_GROUPS = {"a": ("conv_w_in",), "b": ("w_mem_kv", "w_out"), "c": ("mla_w_in", "mla_w_uq", "mla_w_ukv")}
_CARRIERS = {"l0_norm": ("gather chips", ("a",)), "l0_in": ("gather chips", ("b",)), "l0_dwconv": ("gather chips", ("c",)),
             "l0_ln": ("gather sibling", ("b",)), "l0_out": ("gather sibling", ("c",)),
             "l1_in_dx": ("reduce sibling", ("l1",)), "l0_ln_bwd": ("reduce sibling", ("l0a",)),
             "l0_out_dx": ("reduce chips", ("l1", 0, 2)), "l0_dwconv_bwd": ("reduce chips", ("l1", 2, 5)),
             "l0_in_dw": ("reduce chips", ("l0a",)), "l0_in_dx": ("reduce sibling alone, then chips", ("l0b",))}


class _Schedule:
    def __init__(self, w):
        self.w, self.full, self.gather, self.reduce = w, {}, {}, {}
        small = _all_gather_small(_as_tiles([w[n] for n in _SMALL_SHARDED]), "gather_small_weights").reshape(N_DEV, -1)
        o = 0
        for n in _SMALL_SHARDED:
            self.full[n] = _join(n, small[:, o:o + w[n].size].reshape((N_DEV,) + w[n].shape))
            o += w[n].size
        for n in _REPLICATED:
            self.full[n] = w[n]

    def carry(self, call):
        kind, (g, *part) = _CARRIERS[call]
        if kind == "gather chips":
            self.gather[g] = [_gather_chips_stage([self.w[n].astype(BF16) for n in _GATHER_GROUPS[g]])]
            return self.gather[g][0]
        if kind == "gather sibling":
            self.gather[g].append(_gather_sibling_stage(self.gather[g][0].outs))
            return self.gather[g][1]
        r = self.reduce[g]
        if kind == "reduce sibling":
            r["sibling"] = _reduce_sibling_stage(r["cut"])
            return r["sibling"]
        if kind != "reduce chips":
            r["sibling"] = _reduce_sibling_stage(r["cut"])
            _run_stage(r["sibling"], "reduce_sibling_" + g)
        if "partial" not in r:
            r["partial"] = [_add_own(c, s, "reduce_add_%s_%d" % (g, i))
                            for i, (c, s) in enumerate(zip(r["cut"], r["sibling"].outs))]
        lo, hi = part if part else (0, len(r["keys"]))
        stage = _reduce_chips_stage(r["partial"][lo:hi])
        r.setdefault("chips", []).append((r["keys"][lo:hi], stage))
        return stage

    def __getitem__(self, name):
        if name not in self.full:
            g = [k for k, names in _GATHER_GROUPS.items() if name in names][0]
            if len(self.gather[g]) == 1:
                self.gather[g].append(_gather_sibling_stage(self.gather[g][0].outs))
                _run_stage(self.gather[g][1], "gather_sibling_" + g)
            for n, buf in zip(_GATHER_GROUPS[g], self.gather[g][1].outs):
                self.full[n] = _PERM[n][0](_join(n, buf)) if n in _PERM else _join(n, buf)
        return self.full[name]

    def ready(self, group, grads, payload=BF16):
        keys, cut, small = [], [], []
        for (n, layer), g in grads.items():
            if n in _SMALL_SHARDED:
                small.append(_cut(n, g, self.w[n].shape).reshape(N_DEV, -1))
                continue
            keys.append((n, layer))
            if layer is not None:
                cut.append(g.reshape((N_DEV,) + self.w[n].shape[1:]).astype(payload))
            else:
                cut.append(_cut(n, _PERM[n][1](g) if n in _PERM else g, self.w[n].shape).astype(payload))
        if small:
            keys.append(("small", None))
            cut.append(jax.vmap(lambda r: _as_tiles([r]))(jnp.concatenate(small, axis=1)))
        self.reduce[group] = {"keys": keys, "cut": cut}

    def finish(self):
        out = {}
        for r in self.reduce.values():
            for keys, stage in r["chips"]:
                out.update(dict(zip(keys, stage.outs)))
        return out


def kernel(x, mem, positions, norm_g, mem_norm_g, w_mem_kv, w_out, conv_w_in, conv_dw, conv_dw_b, conv_ln_g, conv_ln_b, mla_w_in, mla_q_norm_g, mla_w_uq, mla_kv_norm_g, mla_w_ukv, final_norm_g, loss_target, m_norm_g, m_mem_norm_g, m_w_mem_kv, m_w_out, m_conv_w_in, m_conv_dw, m_conv_dw_b, m_conv_ln_g, m_conv_ln_b, m_mla_w_in, m_mla_q_norm_g, m_mla_w_uq, m_mla_kv_norm_g, m_mla_w_ukv, m_final_norm_g, v_norm_g, v_mem_norm_g, v_w_mem_kv, v_w_out, v_conv_w_in, v_conv_dw, v_conv_dw_b, v_conv_ln_g, v_conv_ln_b, v_mla_w_in, v_mla_q_norm_g, v_mla_w_uq, v_mla_kv_norm_g, v_mla_w_ukv, v_final_norm_g):
    w = dict(zip(_WEIGHTS, (norm_g, mem_norm_g, w_mem_kv, w_out, conv_w_in, conv_dw, conv_dw_b, conv_ln_g, conv_ln_b,
                            mla_w_in, mla_q_norm_g, mla_w_uq, mla_kv_norm_g, mla_w_ukv, final_norm_g)))
    m = dict(zip(_WEIGHTS, (m_norm_g, m_mem_norm_g, m_w_mem_kv, m_w_out, m_conv_w_in, m_conv_dw, m_conv_dw_b, m_conv_ln_g,
                            m_conv_ln_b, m_mla_w_in, m_mla_q_norm_g, m_mla_w_uq, m_mla_kv_norm_g, m_mla_w_ukv, m_final_norm_g)))
    v = dict(zip(_WEIGHTS, (v_norm_g, v_mem_norm_g, v_w_mem_kv, v_w_out, v_conv_w_in, v_conv_dw, v_conv_dw_b, v_conv_ln_g,
                            v_conv_ln_b, v_mla_w_in, v_mla_q_norm_g, v_mla_w_uq, v_mla_kv_norm_g, v_mla_w_ukv, v_final_norm_g)))

    sched = _Schedule(w)
    loss_local, dx, G = _forward_backward(x, mem, positions, loss_target, sched)
    loss = lax.psum(loss_local, ("x", "y", "c"))

    from_chips = sched.finish()
    out = [{}, {}, {}, {}]
    for n in _BIG:
        if n in _ROW_CUT:
            res = [_sum_adamw(from_chips[(n, l)], w[n][l], m[n][l], v[n][l], "adamw_%s_%d" % (n, l)) for l in range(w[n].shape[0])]
            res = [jnp.stack(r) for r in zip(*res)]
        else:
            rows, cols = _rows2d(w[n].shape)
            res = _sum_adamw(from_chips[(n, None)], w[n].reshape(rows, cols), m[n].reshape(rows, cols),
                             v[n].reshape(rows, cols), "adamw_" + n)
        for o, r in zip(out, res):
            o[n] = r.reshape(w[n].shape)
    small_like = [w[n] for n in _SMALL_SHARDED]
    res = _sum_adamw(from_chips[("small", None)], _as_tiles(small_like), _as_tiles([m[n] for n in _SMALL_SHARDED]),
                     _as_tiles([v[n] for n in _SMALL_SHARDED]), "adamw_small")
    for o, r in zip(out, res):
        for n, a in zip(_SMALL_SHARDED, _split_flat(r.reshape(-1), small_like)):
            o[n] = a

    rep_like = [w[n] for n in _REPLICATED]
    rep_parts = _all_gather_small(_as_tiles([G[n] for n in _REPLICATED]), "gather_replicated_grads")
    res = _sum_adamw(rep_parts, _as_tiles(rep_like), _as_tiles([m[n] for n in _REPLICATED]),
                     _as_tiles([v[n] for n in _REPLICATED]), "adamw_replicated")
    for o, r in zip(out, res):
        for n, a in zip(_REPLICATED, _split_flat(r.reshape(-1), rep_like)):
            o[n] = a

    return (loss, dx, *[out[0][n] for n in _WEIGHTS], *[out[1][n] for n in _WEIGHTS],
            *[out[2][n] for n in _WEIGHTS], *[out[3][n] for n in _WEIGHTS])
```

```python
import jax
import jax.numpy as jnp
from jax import lax
from jax.experimental import pallas as pl
from jax.experimental.pallas import tpu as pltpu

F32 = jnp.float32
BF16 = jnp.bfloat16
MESH = pl.DeviceIdType.MESH
N_DEV = 8
VMEM_LIMIT_BYTES = 48 * 1024 * 1024

MEM_HEADS, MEM_HEAD_DIM = 4, 128
MEM_WIDTH = MEM_HEADS * MEM_HEAD_DIM
CONV_KERNEL = 31
CONV_PAD = 32
MLA_HEADS, MLA_NOPE, MLA_ROPE = 12, 128, 64
MLA_QK = MLA_NOPE + MLA_ROPE
HALF_ROPE = MLA_ROPE // 2
Q_RANK, KV_RANK = 512, 256
ROPE_THETA = 10000.0
RMS_EPS = 1e-6
LN_EPS = 1e-5
ADAM_LR, ADAM_B1, ADAM_B2, ADAM_EPS, ADAM_WD, ADAM_STEP = 0.001, 0.9, 0.999, 1e-08, 0.01, 10
NEG = -1e30


class _Stage:
    def __init__(self, ins, out_shapes, sems, start, wait, aliases=None):
        self.ins, self.out_shapes, self.sems = list(ins), list(out_shapes), list(sems)
        self.start, self.wait, self.aliases, self.outs = start, wait, dict(aliases or {}), None


def _call(body, name, out_shape, grid=None, in_specs=None, out_specs=None, scratch=(), dims=None, grid_spec=None, aliases=None,
          carry=None):
    params = dict(vmem_limit_bytes=VMEM_LIMIT_BYTES)
    if dims is not None:
        params["dimension_semantics"] = dims
    kw = {}
    if carry is not None:
        single = not isinstance(out_shape, (list, tuple))
        main_out = [out_shape] if single else list(out_shape)
        main_specs = [out_specs] if single else list(out_specs)
        n_in, n_out, n_scr = len(in_specs), len(main_out), len(scratch)
        x_in, x_out = len(carry.ins), len(carry.out_shapes)
        inner, steps = body, tuple(grid)

        def body(*refs):
            ins, xin = refs[:n_in], refs[n_in:n_in + x_in]
            outs = refs[n_in + x_in:n_in + x_in + n_out]
            xout = refs[n_in + x_in + n_out:n_in + x_in + n_out + x_out]
            scr = refs[n_in + x_in + n_out + x_out:n_in + x_in + n_out + x_out + n_scr]
            xsem = refs[n_in + x_in + n_out + x_out + n_scr:]
            ids = [pl.program_id(a) for a in range(len(steps))]
            first, last = ids[0] == 0, ids[0] == steps[0] - 1
            for a in range(1, len(steps)):
                first = jnp.logical_and(first, ids[a] == 0)
                last = jnp.logical_and(last, ids[a] == steps[a] - 1)
            pl.when(first)(lambda: carry.start(xin, xout, xsem))
            inner(*ins, *outs, *scr)
            pl.when(last)(lambda: carry.wait(xin, xout, xsem))

        hbm = pl.BlockSpec(memory_space=pltpu.HBM)
        aliases = dict(aliases or {})
        aliases.update({n_in + k: n_out + v for k, v in carry.aliases.items()})
        res = _call(body, name, main_out + carry.out_shapes, grid=grid, in_specs=list(in_specs) + [hbm] * x_in,
                    out_specs=main_specs + [hbm] * x_out, scratch=list(scratch) + carry.sems, dims=dims, aliases=aliases)

        def run(*args):
            outs = res(*args, *carry.ins)
            carry.outs = list(outs[n_out:])
            return outs[0] if single else outs[:n_out]

        return run
    if aliases:
        kw["input_output_aliases"] = aliases
    if grid_spec is not None:
        kw["grid_spec"] = grid_spec
    else:
        if grid is not None:
            kw["grid"] = grid
        kw["in_specs"] = in_specs
        kw["out_specs"] = out_specs
        kw["scratch_shapes"] = list(scratch)
    return pl.pallas_call(body, name=name, out_shape=out_shape, compiler_params=pltpu.CompilerParams(**params), **kw)


def _pick(n, target, mult):
    best = None
    for d in range(mult, min(n, target) + 1, mult):
        if n % d == 0:
            best = d
    return n if best is None else best


_DOT_DIMS = {"nn": (((1,), (0,)), ((), ())), "nt": (((1,), (1,)), ((), ())), "tn": (((0,), (0,)), ((), ()))}


def _mm(a, b, mode, out_dtype, name, res=None, carry=None):
    if mode == "tn":
        a, mode = a.T, "nn"
    if mode == "nn":
        (M, K), N = a.shape, b.shape[1]
    else:
        (M, K), N = a.shape, b.shape[0]
    tm = _pick(M, 1024, 8)
    tn = _pick(N, 1536, 128)
    tk = _pick(K, 1536, 128)
    nk = K // tk
    has_res = res is not None

    def body(*refs):
        if has_res:
            a_ref, b_ref, r_ref, o_ref, acc = refs
        else:
            a_ref, b_ref, o_ref, acc = refs
        k = pl.program_id(2)
        part = lax.dot_general(a_ref[...].astype(BF16), b_ref[...].astype(BF16), _DOT_DIMS[mode],
                               preferred_element_type=F32)
        if nk == 1:
            o_ref[...] = (part + r_ref[...] if has_res else part).astype(o_ref.dtype)
            return

        @pl.when(k == 0)
        def _():
            acc[...] = part

        @pl.when(k > 0)
        def _():
            acc[...] += part

        @pl.when(k == nk - 1)
        def _():
            r = acc[...]
            if has_res:
                r = r + r_ref[...]
            o_ref[...] = r.astype(o_ref.dtype)

    a_spec = pl.BlockSpec((tm, tk), lambda i, j, k: (i, k))
    b_spec = {"nn": pl.BlockSpec((tk, tn), lambda i, j, k: (k, j)),
              "nt": pl.BlockSpec((tn, tk), lambda i, j, k: (j, k))}[mode]
    o_spec = pl.BlockSpec((tm, tn), lambda i, j, k: (i, j))
    in_specs = [a_spec, b_spec] + ([o_spec] if has_res else [])
    args = (a, b) + ((res,) if has_res else ())
    return _call(body, name, jax.ShapeDtypeStruct((M, N), out_dtype), grid=(M // tm, N // tn, nk),
                 in_specs=in_specs, out_specs=o_spec, scratch=[pltpu.VMEM((tm, tn), F32)],
                 dims=("parallel", "parallel", "arbitrary"), carry=carry)(*args)


def _views(rows):
    return [r if isinstance(r, tuple) else (r, r.shape[1], 0) for r in rows]


def _row_tile(T, rows):
    return min(T, 512 if max(w for _, w, _ in rows) <= 1024 else 256)


def _rowwise(f, rows, params, outs, name, carry=None, into=None):
    rows = _views(rows)
    T = rows[0][0].shape[0]
    tb = _row_tile(T, rows)
    nr, npar = len(rows), len(params)
    outs = [o if len(o) == 3 else (o[0], o[1], o[0]) for o in outs]
    into = into or []

    def body(*refs):
        vals = f(*[r[...].astype(F32) for r in refs[:nr]], *[p[...] for p in refs[nr:nr + npar]])
        for o_ref, v in zip(refs[nr + npar + len(into):], vals):
            o_ref[...] = v.astype(o_ref.dtype)

    row_spec = lambda w, cb=0: pl.BlockSpec((tb, w), lambda i: (i, cb))
    par_spec = lambda w: pl.BlockSpec((1, w), lambda i: (0, 0))
    out_shape = [jax.ShapeDtypeStruct((T, tw), dt) for _, dt, tw in outs]
    out_specs = [row_spec(w) for w, _, _ in outs]
    in_specs = [row_spec(w, cb) for _, w, cb in rows] + [par_spec(p.shape[1]) for p in params]
    args = [r[0] for r in rows] + list(params)
    aliases = {}
    for k, arr, cb in into:
        aliases[len(args)] = k
        in_specs.append(pl.BlockSpec(memory_space=pl.ANY))
        args.append(arr)
        out_shape[k] = jax.ShapeDtypeStruct(arr.shape, arr.dtype)
        out_specs[k] = row_spec(outs[k][0], cb)
    return _call(body, name, out_shape, grid=(T // tb,), in_specs=in_specs, out_specs=out_specs, dims=("parallel",),
                 carry=carry, aliases=aliases)(*args)


def _rowwise_bwd(f, rows, params, douts, n_diff, name, carry=None, into=None):
    rows, douts = _views(rows), _views(douts)
    T = rows[0][0].shape[0]
    tb = _row_tile(T, rows)
    nr, npar, nd = len(rows), len(params), len(douts)

    def body(*refs):
        rv = [r[...].astype(F32) for r in refs[:nr]]
        pv = [p[...] for p in refs[nr:nr + npar]]
        dv = [d[...].astype(F32) for d in refs[nr + npar:nr + npar + nd]]
        o_refs = refs[nr + npar + nd + (0 if into is None else 1):]
        fixed = rv[n_diff:]

        def g(*xs):
            return tuple(f(*xs[:n_diff], *fixed, *xs[n_diff:]))

        _, vjp = jax.vjp(g, *rv[:n_diff], *pv)
        grads = vjp(tuple(dv))
        for o_ref, gr in zip(o_refs[:n_diff], grads[:n_diff]):
            o_ref[...] = gr.astype(o_ref.dtype)
        first = pl.program_id(0) == 0
        for o_ref, gr in zip(o_refs[n_diff:], grads[n_diff:]):
            @pl.when(first)
            def _(o_ref=o_ref):
                o_ref[...] = jnp.zeros_like(o_ref)

            o_ref[...] += gr

    row_spec = lambda w, cb=0: pl.BlockSpec((tb, w), lambda i: (i, cb))
    par_spec = lambda w: pl.BlockSpec((1, w), lambda i: (0, 0))
    out_shape = ([jax.ShapeDtypeStruct((T, w), F32) for _, w, _ in rows[:n_diff]]
                 + [jax.ShapeDtypeStruct((1, p.shape[1]), F32) for p in params])
    out_specs = [row_spec(w) for _, w, _ in rows[:n_diff]] + [par_spec(p.shape[1]) for p in params]
    in_specs = ([row_spec(w, cb) for _, w, cb in rows] + [par_spec(p.shape[1]) for p in params]
                + [row_spec(w, cb) for _, w, cb in douts])
    args = [r[0] for r in rows] + list(params) + [d[0] for d in douts]
    aliases = None
    if into is not None:
        aliases = {len(args): 0}
        in_specs.append(pl.BlockSpec(memory_space=pl.ANY))
        args.append(into[0])
        out_shape[0] = jax.ShapeDtypeStruct(into[0].shape, into[0].dtype)
        out_specs[0] = row_spec(rows[0][1], into[1])
    return _call(body, name, out_shape, grid=(T // tb,), in_specs=in_specs, out_specs=out_specs,
                 dims=("arbitrary",), carry=carry, aliases=aliases)(*args)


def _sig(x):
    return 1.0 / (1.0 + jnp.exp(-x))


def _rms(x, g):
    return x * lax.rsqrt(jnp.mean(x * x, axis=-1, keepdims=True) + RMS_EPS) * g


def _f_rms(x, g):
    return (_rms(x, g),)


def _f_ln_silu(x, g, b):
    mu = jnp.mean(x, axis=-1, keepdims=True)
    xc = x - mu
    var = jnp.mean(xc * xc, axis=-1, keepdims=True)
    y = xc * lax.rsqrt(var + LN_EPS) * g + b
    return (y * _sig(y),)


def _rope128(x, cos_p, sin_p):
    return x * cos_p + pltpu.roll(x, 64, 1) * sin_p


def _rope128_t(d, cos_p, sin_p):
    return d * cos_p + pltpu.roll(d * sin_p, 64, 1)


def _f_rope(xq, xk, cos_p, sin_p):
    heads = [_rope128(xq[:, h * 128:(h + 1) * 128], cos_p, sin_p) for h in range(MLA_HEADS)]
    return (jnp.concatenate(heads, axis=1), _rope128(xk, cos_p, sin_p))


def _f_rope_t(dq, dk_heads, cos_p, sin_p):
    heads = [_rope128_t(dq[:, h * 128:(h + 1) * 128], cos_p, sin_p) for h in range(MLA_HEADS)]
    dk = dk_heads[:, 0:128]
    for h in range(1, MLA_HEADS):
        dk = dk + dk_heads[:, h * 128:(h + 1) * 128]
    return (jnp.concatenate(heads, axis=1), _rope128_t(dk, cos_p, sin_p))


GATE_LANES = 512


def _gate_fwd(ycat, proj, z_col, name, tb=1024):
    T, width = ycat.shape
    zb = z_col // GATE_LANES

    def body(y_ref, z_ref, o_ref, ot_ref):
        z = z_ref[...]
        y = y_ref[...] * (z * _sig(z))
        o_ref[...] = y.astype(o_ref.dtype)
        ot_ref[...] = y.T.astype(ot_ref.dtype)

    blk = pl.BlockSpec((tb, GATE_LANES), lambda i, c: (i, c))
    return _call(body, name, [jax.ShapeDtypeStruct((T, width), BF16), jax.ShapeDtypeStruct((width, T), BF16)],
                 grid=(T // tb, width // GATE_LANES),
                 in_specs=[blk, pl.BlockSpec((tb, GATE_LANES), lambda i, c: (i, zb + c))],
                 out_specs=[blk, pl.BlockSpec((GATE_LANES, tb), lambda i, c: (c, i))],
                 dims=("parallel", "parallel"))(ycat, proj)


def _out_dx_gate_bwd(dh, w_out, ycat, proj, z_col, name, tb=1024, carry=None):
    T, width = ycat.shape
    D = dh.shape[1]
    zb = z_col // GATE_LANES

    def body(dh_ref, w_ref, y_ref, z_ref, dycat_ref, dz_ref):
        d = lax.dot_general(dh_ref[...].astype(BF16), w_ref[...], _DOT_DIMS["nt"], preferred_element_type=F32)
        z = z_ref[...]
        s = _sig(z)
        dycat_ref[...] = d * (z * s)
        dz_ref[...] = (d * y_ref[...] * (s * (1.0 + z * (1.0 - s)))).astype(dz_ref.dtype)

    blk = pl.BlockSpec((tb, GATE_LANES), lambda i, c: (i, c))
    zblk = pl.BlockSpec((tb, GATE_LANES), lambda i, c: (i, zb + c))
    return _call(body, name, [jax.ShapeDtypeStruct((T, width), F32), jax.ShapeDtypeStruct(proj.shape, BF16)],
                 grid=(T // tb, width // GATE_LANES),
                 in_specs=[pl.BlockSpec((tb, D), lambda i, c: (i, 0)), pl.BlockSpec((GATE_LANES, D), lambda i, c: (c, 0)),
                           blk, zblk],
                 out_specs=[blk, zblk], dims=("parallel", "parallel"), carry=carry)(dh, w_out, ycat, proj)


def _in_dx_norm_bwd(d_proj, w_in, h, g, add, name, tm=512, carry=None):
    T, K = d_proj.shape
    D = w_in.shape[0]
    tk = _pick(K, 1536, 128)
    nk = K // tk

    def body(a_ref, b_ref, h_ref, g_ref, add_ref, dx_ref, dg_ref, acc):
        i, k = pl.program_id(0), pl.program_id(1)
        part = lax.dot_general(a_ref[...], b_ref[...], _DOT_DIMS["nt"], preferred_element_type=F32)

        @pl.when(jnp.logical_and(i == 0, k == 0))
        def _():
            dg_ref[...] = jnp.zeros_like(dg_ref)

        @pl.when(k == 0)
        def _():
            acc[...] = part

        @pl.when(k > 0)
        def _():
            acc[...] += part

        @pl.when(k == nk - 1)
        def _():
            _, vjp = jax.vjp(_rms, h_ref[...], g_ref[...])
            dh, dg = vjp(acc[...])
            dx_ref[...] = dh + add_ref[...]
            dg_ref[...] += dg

    row = pl.BlockSpec((tm, D), lambda i, k: (i, 0))
    par = pl.BlockSpec((1, D), lambda i, k: (0, 0))
    return _call(body, name, [jax.ShapeDtypeStruct((T, D), F32), jax.ShapeDtypeStruct((1, D), F32)], grid=(T // tm, nk),
                 in_specs=[pl.BlockSpec((tm, tk), lambda i, k: (i, k)), pl.BlockSpec((D, tk), lambda i, k: (0, k)),
                           row, par, row],
                 out_specs=[row, par], scratch=[pltpu.VMEM((tm, D), F32)], dims=("arbitrary", "arbitrary"),
                 carry=carry)(d_proj, w_in, h, g, add)


def _glu_bwd(proj, d_glu, d_proj, name, tb=256):
    T, w = d_glu.shape

    def body(a_ref, g_ref, d_ref, _, o_ref):
        s, d = _sig(g_ref[...]), d_ref[...]
        o_ref[:, 0:w] = (d * s).astype(o_ref.dtype)
        o_ref[:, w:2 * w] = (d * a_ref[...] * (s * (1.0 - s))).astype(o_ref.dtype)

    return _call(body, name, jax.ShapeDtypeStruct(d_proj.shape, d_proj.dtype), grid=(T // tb,),
                 in_specs=[pl.BlockSpec((tb, w), lambda i: (i, 0)), pl.BlockSpec((tb, w), lambda i: (i, 1)),
                           pl.BlockSpec((tb, w), lambda i: (i, 0)), pl.BlockSpec(memory_space=pl.ANY)],
                 out_specs=pl.BlockSpec((tb, 2 * w), lambda i: (i, 0)), dims=("parallel",),
                 aliases={3: 0})(proj, proj, d_glu, d_proj)


def _final_loss(h, tgt, g, name, tb=512):
    T, D = h.shape

    def body(h_ref, t_ref, g_ref, dh_ref, dg_ref, loss_ref):
        tv = t_ref[...]

        def rowloss(hh, gg):
            e = _rms(hh, gg) - tv
            return 0.5 * jnp.mean(e * e, axis=-1, keepdims=True)

        lr, vjp = jax.vjp(rowloss, h_ref[...], g_ref[...])
        dh, dg = vjp(jnp.ones_like(lr))
        dh_ref[...] = dh

        @pl.when(pl.program_id(0) == 0)
        def _():
            dg_ref[...] = jnp.zeros_like(dg_ref)
            loss_ref[...] = jnp.zeros_like(loss_ref)

        dg_ref[...] += dg
        loss_ref[...] += jnp.broadcast_to(jnp.sum(lr, axis=0, keepdims=True), loss_ref.shape)

    row = pl.BlockSpec((tb, D), lambda i: (i, 0))
    par = pl.BlockSpec((1, D), lambda i: (0, 0))
    return _call(body, name,
                 [jax.ShapeDtypeStruct((T, D), F32), jax.ShapeDtypeStruct((1, D), F32), jax.ShapeDtypeStruct((1, 128), F32)],
                 grid=(T // tb,), in_specs=[row, row, par],
                 out_specs=[row, par, pl.BlockSpec((1, 128), lambda i: (0, 0))], dims=("arbitrary",))(h, tgt, g)


CONV_ROWS = 128
CONV_LANES = 256


def _sublane_phases(pad, n):
    for r in range(1, 8):
        for c0 in range(0, n - 8, 256):
            rows = min(256, n - 8 - c0)
            pad[r, c0:c0 + rows, :] = pad[0, c0 + r:c0 + r + rows, :]


def _dwconv_fwd(proj, C, w, b, B, S, name, carry=None):
    cb = CONV_LANES
    off = CONV_PAD - (CONV_KERNEL - 1)

    def body(a_ref, g_ref, w_ref, b_ref, o_ref, pad):
        pad[0, 0:CONV_PAD, :] = jnp.zeros((CONV_PAD, cb), F32)
        for c0 in range(0, S, 256):
            pad[0, CONV_PAD + c0:CONV_PAD + c0 + 256, :] = a_ref[c0:c0 + 256, :] * _sig(g_ref[c0:c0 + 256, :])
        _sublane_phases(pad, S + CONV_PAD)
        for t0 in range(0, S, CONV_ROWS):
            acc = jnp.broadcast_to(b_ref[...], (CONV_ROWS, cb))
            for k in range(CONV_KERNEL):
                r, base = (off + k) % 8, t0 + (off + k) // 8 * 8
                acc = acc + w_ref[k:k + 1, :] * pad[r, base:base + CONV_ROWS, :]
            o_ref[t0:t0 + CONV_ROWS, :] = acc

    return _call(body, name, jax.ShapeDtypeStruct((B, S, C), F32), grid=(B, C // cb),
                 in_specs=[pl.BlockSpec((S, cb), lambda i, j: (i, j)), pl.BlockSpec((S, cb), lambda i, j: (i, C // cb + j)),
                           pl.BlockSpec((CONV_KERNEL, cb), lambda i, j: (0, j)),
                           pl.BlockSpec((1, cb), lambda i, j: (0, j))],
                 out_specs=pl.BlockSpec((None, S, cb), lambda i, j: (i, 0, j)),
                 scratch=[pltpu.VMEM((8, S + CONV_PAD, cb), F32)], dims=("parallel", "parallel"),
                 carry=carry)(proj, proj, w, b)


def _dwconv_bwd(proj, w, dy, name, carry=None):
    B, S, C = dy.shape
    cb = CONV_LANES
    groups = CONV_ROWS // 8

    def body(a_ref, g_ref, w_ref, dy_ref, dx_ref, dw_ref, db_ref, dypad, wacc):
        dypad[0, 0:S, :] = dy_ref[...]
        dypad[0, S:, :] = jnp.zeros((CONV_PAD, cb), F32)
        _sublane_phases(dypad, S + CONV_PAD)
        wacc[...] = jnp.zeros_like(wacc)
        for t0 in range(0, S, CONV_ROWS):
            xc = a_ref[t0:t0 + CONV_ROWS, :] * _sig(g_ref[t0:t0 + CONV_ROWS, :])
            acc = jnp.zeros((CONV_ROWS, cb), F32)
            for k in range(CONV_KERNEL):
                o = (CONV_KERNEL - 1) - k
                dys = dypad[o % 8, t0 + o // 8 * 8:t0 + o // 8 * 8 + CONV_ROWS, :]
                acc = acc + w_ref[k:k + 1, :] * dys
                wacc[k] += jnp.sum((dys * xc).reshape(groups, 8, cb), axis=0)
            wacc[CONV_KERNEL] += jnp.sum(dy_ref[t0:t0 + CONV_ROWS, :].reshape(groups, 8, cb), axis=0)
            dx_ref[t0:t0 + CONV_ROWS, :] = acc

        @pl.when(pl.program_id(1) == 0)
        def _():
            dw_ref[...] = jnp.zeros_like(dw_ref)
            db_ref[...] = jnp.zeros_like(db_ref)

        for k in range(CONV_KERNEL):
            dw_ref[k:k + 1, :] += jnp.sum(wacc[k], axis=0, keepdims=True)
        db_ref[...] += jnp.sum(wacc[CONV_KERNEL], axis=0, keepdims=True)

    blk = pl.BlockSpec((None, S, cb), lambda j, i: (i, 0, j))
    return _call(body, name,
                 [jax.ShapeDtypeStruct((B, S, C), F32), jax.ShapeDtypeStruct((CONV_KERNEL, C), F32),
                  jax.ShapeDtypeStruct((1, C), F32)],
                 grid=(C // cb, B),
                 in_specs=[pl.BlockSpec((S, cb), lambda j, i: (i, j)), pl.BlockSpec((S, cb), lambda j, i: (i, C // cb + j)),
                           pl.BlockSpec((CONV_KERNEL, cb), lambda j, i: (0, j)), blk],
                 out_specs=[blk, pl.BlockSpec((CONV_KERNEL, cb), lambda j, i: (0, j)),
                            pl.BlockSpec((1, cb), lambda j, i: (0, j))],
                 scratch=[pltpu.VMEM((8, S + CONV_PAD, cb), F32), pltpu.VMEM((CONV_KERNEL + 1, 8, cb), F32)],
                 dims=("parallel", "arbitrary"), carry=carry)(proj, proj, w, dy)


ATTN_TILE = {"fwd": 1024, "bwd": 1024, "cross fwd": 512}
ATTN_SUB = {"fwd": 256, "bwd": 512}


def _attn_shapes(Sq, Sk, causal, pass_):
    tq = min(Sq, ATTN_TILE[pass_ if causal or pass_ == "bwd" else "cross fwd"])
    tk = tq if causal else min(Sk, ATTN_TILE[pass_])
    return tq, tk, min(ATTN_SUB[pass_], tq)


def _causal_bias(n):
    r = lax.broadcasted_iota(jnp.int32, (n, n), 0)
    c = lax.broadcasted_iota(jnp.int32, (n, n), 1)
    return jnp.where(c <= r, 0.0, NEG).astype(F32)


def _mask_diagonal(s, bias):
    n, nc = s.shape
    if nc == n:
        return s + bias
    return jnp.concatenate([s[:, :nc - n], s[:, nc - n:] + bias], axis=1)


def _attn_fwd(q, q_c0, qr, k, k_c0, kr, v, v_c0, B, Sq, Sk, H, causal, scale, name, into=None, o_c0=0, o_width=None,
              kv_stride=1):
    tq, tk, sub = _attn_shapes(Sq, Sk, causal, "fwd")
    nq, nk, nsub = Sq // tq, Sk // tk, tq // sub
    rope = qr is not None

    def body(*refs):
        refs = list(refs)
        qn_ref = refs.pop(0)
        qr_ref = refs.pop(0) if rope else None
        kn_ref = refs.pop(0)
        kr_ref = refs.pop(0) if rope else None
        v_ref = refs.pop(0)
        if into is not None:
            refs.pop(0)
        o_ref, lse_ref, m_s, l_s, acc = refs
        qi = pl.program_id(2)
        m_s[...] = jnp.full_like(m_s, NEG)
        l_s[...] = jnp.zeros_like(l_s)
        acc[...] = jnp.zeros_like(acc)
        bias = _causal_bias(sub) if causal else None
        qs = []
        for r in range(nsub):
            qn = qn_ref[r * sub:(r + 1) * sub, :].astype(BF16)
            qs.append(jnp.concatenate([qn, qr_ref[r * sub:(r + 1) * sub, :]], axis=1) if rope else qn)

        def step(j, masked):
            ks = pl.ds(pl.multiple_of(j * tk, tk), tk)
            kk = jnp.concatenate([kn_ref[ks, :], kr_ref[ks, :]], axis=1) if rope else kn_ref[ks, :]
            vv = v_ref[ks, :]
            for r in range(nsub):
                rows = slice(r * sub, (r + 1) * sub)
                nc = (r + 1) * sub if masked else tk
                s = lax.dot_general(qs[r], kk[:nc], _DOT_DIMS["nt"], preferred_element_type=F32) * scale
                if masked:
                    s = _mask_diagonal(s, bias)
                m_old = m_s[rows, :]
                m_new = jnp.maximum(m_old, jnp.max(s, axis=-1, keepdims=True))
                p = jnp.exp(s - m_new)
                alpha = jnp.exp(m_old - m_new)
                l_s[rows, :] = alpha * l_s[rows, :] + jnp.sum(p, axis=-1, keepdims=True)
                acc[rows, :] = alpha * acc[rows, :] + jnp.dot(p.astype(BF16), vv[:nc], preferred_element_type=F32)
                m_s[rows, :] = m_new

        def unmasked(j, carry):
            step(j, False)
            return carry

        if causal:
            lax.fori_loop(0, qi, unmasked, 0)
            step(qi, True)
        else:
            lax.fori_loop(0, nk, unmasked, 0)
        o_ref[...] = (acc[...] / l_s[...]).astype(o_ref.dtype)
        lse_ref[...] = m_s[...] + jnp.log(l_s[...])

    qspec = lambda c0: pl.BlockSpec((tq, 128), lambda b, h, i: (b * nq + i, c0 + h))
    kspec = lambda c0: pl.BlockSpec((Sk, 128), lambda b, h, i: (b, c0 + kv_stride * h))
    in_specs, args = [qspec(q_c0)], [q]
    if rope:
        in_specs.append(qspec(0)); args.append(qr)
    in_specs.append(kspec(k_c0)); args.append(k)
    if rope:
        in_specs.append(pl.BlockSpec((Sk, 128), lambda b, h, i: (b, 0))); args.append(kr)
    in_specs.append(kspec(v_c0)); args.append(v)
    aliases = {}
    if into is not None:
        aliases = {len(args): 0}
        in_specs.append(pl.BlockSpec(memory_space=pl.ANY)); args.append(into)
        o_shape = jax.ShapeDtypeStruct(into.shape, into.dtype)
    else:
        o_shape = jax.ShapeDtypeStruct((B * Sq, o_width), F32)
    return _call(body, name, [o_shape, jax.ShapeDtypeStruct((B * H, Sq, 1), F32)], grid=(B, H, nq), in_specs=in_specs,
                 out_specs=[qspec(o_c0), pl.BlockSpec((None, tq, 1), lambda b, h, i: (b * H + h, i, 0))],
                 scratch=[pltpu.VMEM((tq, 1), F32), pltpu.VMEM((tq, 1), F32), pltpu.VMEM((tq, 128), F32)],
                 dims=("parallel", "parallel", "arbitrary"), aliases=aliases)(*args)


def _attn_bwd(q, q_c0, qr, k, k_c0, kr, v, v_c0, o, do, o_c0, lse, B, Sq, Sk, H, causal, scale, name, dq_into=None,
              kv_stride=1):
    tq, tk, sub = _attn_shapes(Sq, Sk, causal, "bwd")
    nq, nk, nsub = Sq // tq, Sk // tk, tq // sub
    rope = qr is not None
    dk_w = 256 if rope else 128

    def body(*refs):
        refs = list(refs)
        qn_ref = refs.pop(0)
        qr_ref = refs.pop(0) if rope else None
        kn_ref = refs.pop(0)
        kr_ref = refs.pop(0) if rope else None
        v_ref, o_ref, do_ref, lse_ref = refs[:4]
        refs = refs[4 + (0 if dq_into is None else 1):]
        dqn_ref = refs.pop(0)
        dqr_ref = refs.pop(0) if rope else None
        dkn_ref = refs.pop(0)
        dkr_ref = refs.pop(0) if rope else None
        dv_ref = None if rope else refs.pop(0)
        q_s, do_s, dl_s, dq_acc, dk_acc, dv_acc = refs
        kj = pl.program_id(2)

        @pl.when(kj == 0)
        def _():
            qn = qn_ref[...].astype(BF16)
            q_s[...] = jnp.concatenate([qn, qr_ref[...]], axis=1) if rope else qn
            dof = do_ref[...]
            do_s[...] = dof.astype(BF16)
            dl_s[...] = jnp.sum(dof * o_ref[...], axis=-1, keepdims=True)
            dq_acc[...] = jnp.zeros_like(dq_acc)

        kk = jnp.concatenate([kn_ref[...], kr_ref[...]], axis=1) if rope else kn_ref[...]
        vv = v_ref[...]
        bias = _causal_bias(sub) if causal else None
        dk_acc[...] = jnp.zeros_like(dk_acc)
        dv_acc[...] = jnp.zeros_like(dv_acc)

        def step(i, masked):
            for r in range(nsub):
                rows = pl.ds(pl.multiple_of(i * tq + r * sub, sub), sub)
                qq, dob = q_s[rows, :], do_s[rows, :]
                nc = (r + 1) * sub if masked else tk
                kc, vc = kk[:nc], vv[:nc]
                s = lax.dot_general(qq, kc, _DOT_DIMS["nt"], preferred_element_type=F32) * scale
                if masked:
                    s = _mask_diagonal(s, bias)
                p = jnp.exp(s - lse_ref[rows, :])
                dp = lax.dot_general(dob, vc, _DOT_DIMS["nt"], preferred_element_type=F32)
                ds = (p * (dp - dl_s[rows, :]) * scale).astype(BF16)
                dv_acc[0:nc, :] += lax.dot_general(p.astype(BF16), dob, _DOT_DIMS["tn"], preferred_element_type=F32)
                dk_acc[0:nc, :] += lax.dot_general(ds, qq, _DOT_DIMS["tn"], preferred_element_type=F32)
                dq_acc[rows, :] += jnp.dot(ds, kc, preferred_element_type=F32)

        def unmasked(i, carry):
            step(i, False)
            return carry

        if causal:
            step(kj, True)
            lax.fori_loop(kj + 1, nq, unmasked, 0)
        else:
            lax.fori_loop(0, nq, unmasked, 0)
        if rope:
            dkn_ref[...] = jnp.concatenate([dk_acc[:, 0:128], dv_acc[...]], axis=1).astype(dkn_ref.dtype)
            dkr_ref[...] = dk_acc[:, 128:256]
        else:
            dkn_ref[...] = dk_acc[...]
            dv_ref[...] = dv_acc[...]

        @pl.when(kj == nk - 1)
        def _():
            dqn_ref[...] = dq_acc[:, 0:128].astype(dqn_ref.dtype)
            if rope:
                dqr_ref[...] = dq_acc[:, 128:256]

    qspec = lambda c0: pl.BlockSpec((Sq, 128), lambda b, h, j: (b, c0 + h))
    kspec = lambda c0: pl.BlockSpec((tk, 128), lambda b, h, j: (b * nk + j, c0 + kv_stride * h))
    in_specs, args = [qspec(q_c0)], [q]
    if rope:
        in_specs.append(qspec(0)); args.append(qr)
    in_specs.append(kspec(k_c0)); args.append(k)
    if rope:
        in_specs.append(pl.BlockSpec((tk, 128), lambda b, h, j: (b * nk + j, 0))); args.append(kr)
    in_specs += [kspec(v_c0), qspec(o_c0), qspec(o_c0), pl.BlockSpec((None, Sq, 1), lambda b, h, j: (b * H + h, 0, 0))]
    args += [v, o, do, lse]
    h_rows_q = jax.ShapeDtypeStruct((B * Sq, H * 128), F32)
    h_rows_k = jax.ShapeDtypeStruct((B * Sk, H * 128), F32)
    out_shape, out_specs, aliases = [h_rows_q], [qspec(0)], None
    if rope:
        out_shape = [jax.ShapeDtypeStruct((B * Sq, 2 * H * 128), BF16)]
    if dq_into is not None:
        aliases = {len(args): 0}
        in_specs.append(pl.BlockSpec(memory_space=pl.ANY)); args.append(dq_into[0])
        out_shape, out_specs = [jax.ShapeDtypeStruct(dq_into[0].shape, dq_into[0].dtype)], [qspec(dq_into[1])]
    if rope:
        out_shape.append(h_rows_q); out_specs.append(qspec(0))
    hspec = lambda w: pl.BlockSpec((tk, w), lambda b, h, j: (b * nk + j, h))
    if rope:
        out_shape += [jax.ShapeDtypeStruct((B * Sk, H * 256), BF16), h_rows_k]
        out_specs += [hspec(256), hspec(128)]
    else:
        out_shape += [h_rows_k, h_rows_k]
        out_specs += [hspec(128), hspec(128)]
    return _call(body, name, out_shape, grid=(B, H, nk), in_specs=in_specs, out_specs=out_specs,
                 scratch=[pltpu.VMEM((Sq, dk_w), BF16), pltpu.VMEM((Sq, 128), BF16), pltpu.VMEM((Sq, 1), F32),
                          pltpu.VMEM((Sq, dk_w), F32), pltpu.VMEM((tk, dk_w), F32), pltpu.VMEM((tk, 128), F32)],
                 dims=("parallel", "parallel", "arbitrary"), aliases=aliases)(*args)


def _mem_attention_fwd(proj, q_col, ycat, mem2, mem_g, w_mem, B, S, tag):
    M = mem2.shape[0] // B
    (memn,) = _rowwise(_f_rms, [mem2], [mem_g], [(mem2.shape[1], BF16)], tag + "_memnorm")
    kvm = _mm(memn, w_mem, "nn", BF16, tag + "_memkv")
    o_c0 = ycat.shape[1] // 128 - MEM_HEADS
    ycat, lse = _attn_fwd(proj, q_col // 128, None, kvm, 0, None, kvm, MEM_HEADS, B, S, M, MEM_HEADS, False,
                          MEM_HEAD_DIM ** -0.5, tag + "_memattn", into=ycat, o_c0=o_c0)
    return ycat, (memn, kvm, lse)


def _mem_attention_bwd(proj, q_col, ycat, d_ycat, d_proj, saved, mem2, mem_g, w_mem, B, S, tag):
    memn, kvm, lse = saved
    M = mem2.shape[0] // B
    o_c0 = ycat.shape[1] // 128 - MEM_HEADS
    d_q, d_k, d_v = _attn_bwd(proj, q_col // 128, None, kvm, 0, None, kvm, MEM_HEADS, ycat, d_ycat, o_c0, lse, B, S, M,
                              MEM_HEADS, False, MEM_HEAD_DIM ** -0.5, tag + "_memattn_bwd", dq_into=(d_proj, q_col // 128))
    d_kvm = jnp.concatenate([d_k, d_v], axis=1).astype(BF16)
    d_w_mem = _mm(memn, d_kvm, "tn", F32, tag + "_memkv_dw")
    d_memn = _mm(d_kvm, w_mem, "nt", F32, tag + "_memkv_dx")
    _, d_mem_g = _rowwise_bwd(_f_rms, [mem2], [mem_g], [d_memn], 1, tag + "_memnorm_bwd")
    return d_q, d_w_mem, d_mem_g


def _rope_tables(positions):
    inv_freq = 1.0 / (ROPE_THETA ** (jnp.arange(0, MLA_ROPE, 2, dtype=F32) / MLA_ROPE))
    ang = positions.astype(F32).reshape(-1, 1) * inv_freq
    cos, sin, zero = jnp.cos(ang), jnp.sin(ang), jnp.zeros_like(ang)
    return jnp.concatenate([cos, zero, cos, zero], axis=1), jnp.concatenate([-sin, zero, sin, zero], axis=1)


def _forward_backward(x, mem, positions, target, W):
    B, S, D = x.shape
    T = B * S
    conv_w = W["conv_dw"].shape[1]
    mix_w = 2 * D
    h0 = x.reshape(T, D)
    mem2 = mem.reshape(-1, D)
    tgt = target.reshape(T, D)
    row = lambda v: v.reshape(1, -1)
    n_nope = MLA_HEADS * MLA_NOPE

    g0 = row(W["norm_g"][0])
    (u0,) = _rowwise(_f_rms, [h0], [g0], [(D, BF16)], "l0_norm", carry=W.carry("l0_norm"))
    proj0 = _mm(u0, W["conv_w_in"], "nn", F32, "l0_in", carry=W.carry("l0_in"))
    qm0_col, z0_col = 2 * conv_w, 2 * conv_w + MEM_WIDTH
    dw, dwb = W["conv_dw"], row(W["conv_dw_b"][0])
    cv = _dwconv_fwd(proj0, conv_w, dw, dwb, B, S, "l0_dwconv", carry=W.carry("l0_dwconv")).reshape(T, conv_w)
    ln_g, ln_b = row(W["conv_ln_g"][0]), row(W["conv_ln_b"][0])
    (ycat0,) = _rowwise(_f_ln_silu, [cv], [ln_g, ln_b], [(conv_w, F32, mix_w)], "l0_ln", carry=W.carry("l0_ln"))
    mg0 = row(W["mem_norm_g"][0])
    ycat0, mem_saved0 = _mem_attention_fwd(proj0, qm0_col, ycat0, mem2, mg0, W["w_mem_kv"][0], B, S, "l0")
    y0, y0_t = _gate_fwd(ycat0, proj0, z0_col, "l0_gate")
    h1 = _mm(y0, W["w_out"][0], "nn", F32, "l0_out", res=h0, carry=W.carry("l0_out"))

    g1 = row(W["norm_g"][1])
    (u1,) = _rowwise(_f_rms, [h1], [g1], [(D, BF16)], "l1_norm")
    proj1 = _mm(u1, W["mla_w_in"], "nn", F32, "l1_in")
    z1_col = Q_RANK
    qm1_col = z1_col + mix_w
    ckv_col = qm1_col + MEM_WIDTH
    kr_col = ckv_col + KV_RANK
    cq, ckv = (proj1, Q_RANK, 0), (proj1, KV_RANK, ckv_col // KV_RANK)
    qg, kvg = row(W["mla_q_norm_g"]), row(W["mla_kv_norm_g"])
    (cqn,) = _rowwise(_f_rms, [cq], [qg], [(Q_RANK, BF16)], "l1_qnorm")
    (ckvn,) = _rowwise(_f_rms, [ckv], [kvg], [(KV_RANK, BF16)], "l1_kvnorm")
    qf = _mm(cqn, W["mla_w_uq"], "nn", F32, "l1_uq")
    kvf = _mm(ckvn, W["mla_w_ukv"], "nn", BF16, "l1_ukv")
    cos_p, sin_p = _rope_tables(positions)
    qr, kr = _rowwise(_f_rope, [(qf, n_nope, 1), (proj1, 128, kr_col // 128), cos_p, sin_p], [],
                      [(n_nope, BF16), (128, BF16)], "l1_rope")
    scale1 = MLA_QK ** -0.5
    ycat1, lse1 = _attn_fwd(qf, 0, qr, kvf, 0, kr, kvf, 1, B, S, S, MLA_HEADS, True, scale1, "l1_attn",
                            o_width=mix_w, kv_stride=2)
    mg1 = row(W["mem_norm_g"][1])
    ycat1, mem_saved1 = _mem_attention_fwd(proj1, qm1_col, ycat1, mem2, mg1, W["w_mem_kv"][1], B, S, "l1")
    y1, y1_t = _gate_fwd(ycat1, proj1, z1_col, "l1_gate")
    h2 = _mm(y1, W["w_out"][1], "nn", F32, "l1_out", res=h1)

    gf = row(W["final_norm_g"])
    dh2, d_gf, loss128 = _final_loss(h2, tgt, gf, "final_loss")
    G = {"final_norm_g": d_gf.reshape(-1)}
    L1 = {}

    d_wout1 = _mm(y1_t, dh2, "nn", F32, "l1_out_dw")
    d_ycat1, d_proj1 = _out_dx_gate_bwd(dh2, W["w_out"][1], ycat1, proj1, z1_col, "l1_out_dx")
    d_proj1, d_wmem1, d_mg1 = _mem_attention_bwd(proj1, qm1_col, ycat1, d_ycat1, d_proj1, mem_saved1, mem2, mg1,
                                                 W["w_mem_kv"][1], B, S, "l1")
    d_qf, d_qr, d_kvf, d_kr_heads = _attn_bwd(qf, 0, qr, kvf, 0, kr, kvf, 1, ycat1, d_ycat1, 0, lse1, B, S, S,
                                              MLA_HEADS, True, scale1, "l1_attn_bwd", kv_stride=2)
    d_qf, d_proj1 = _rowwise(_f_rope_t, [d_qr, d_kr_heads, cos_p, sin_p], [], [(n_nope, F32), (128, F32)], "l1_rope_bwd",
                             into=[(0, d_qf, 1), (1, d_proj1, kr_col // 128)])
    d_cqn = _mm(d_qf, W["mla_w_uq"], "nt", F32, "l1_uq_dx")
    L1[("mla_w_uq", None)] = _mm(cqn, d_qf, "tn", F32, "l1_uq_dw")
    d_ckvn = _mm(d_kvf, W["mla_w_ukv"], "nt", F32, "l1_ukv_dx")
    L1[("mla_w_ukv", None)] = _mm(ckvn, d_kvf, "tn", F32, "l1_ukv_dw")
    d_proj1, d_qg = _rowwise_bwd(_f_rms, [cq], [qg], [d_cqn], 1, "l1_qnorm_bwd", into=(d_proj1, cq[2]))
    d_proj1, d_kvg = _rowwise_bwd(_f_rms, [ckv], [kvg], [d_ckvn], 1, "l1_kvnorm_bwd", into=(d_proj1, ckv[2]))
    L1[("w_mem_kv", 1)] = d_wmem1
    L1[("mla_w_in", None)] = _mm(u1, d_proj1, "tn", F32, "l1_in_dw")
    L1[("w_out", 1)] = d_wout1
    W.ready("l1", L1)
    dh1, d_g1 = _in_dx_norm_bwd(d_proj1, W["mla_w_in"], h1, g1, dh2, "l1_in_dx", carry=W.carry("l1_in_dx"))

    d_wout0 = _mm(y0_t, dh1, "nn", F32, "l0_out_dw")
    d_ycat0, d_proj0 = _out_dx_gate_bwd(dh1, W["w_out"][0], ycat0, proj0, z0_col, "l0_out_dx", carry=W.carry("l0_out_dx"))
    d_proj0, d_wmem0, d_mg0 = _mem_attention_bwd(proj0, qm0_col, ycat0, d_ycat0, d_proj0, mem_saved0, mem2, mg0,
                                                 W["w_mem_kv"][0], B, S, "l0")
    W.ready("l0a", {("w_mem_kv", 0): d_wmem0, ("w_out", 0): d_wout0})
    d_cv, d_ln_g, d_ln_b = _rowwise_bwd(_f_ln_silu, [cv], [ln_g, ln_b], [(d_ycat0, conv_w, 0)], 1, "l0_ln_bwd",
                                        carry=W.carry("l0_ln_bwd"))
    d_glu, d_dw, d_dwb = _dwconv_bwd(proj0, dw, d_cv.reshape(B, S, conv_w), "l0_dwconv_bwd",
                                     carry=W.carry("l0_dwconv_bwd"))
    d_proj0 = _glu_bwd(proj0, d_glu.reshape(T, conv_w), d_proj0, "l0_glu_bwd")
    d_conv_w_in = _mm(u0, d_proj0, "tn", F32, "l0_in_dw", carry=W.carry("l0_in_dw"))
    W.ready("l0b", {("conv_w_in", None): d_conv_w_in, ("conv_dw", None): d_dw,
                    ("mla_q_norm_g", None): d_qg.reshape(-1), ("mla_kv_norm_g", None): d_kvg.reshape(-1)})
    dx, d_g0 = _in_dx_norm_bwd(d_proj0, W["conv_w_in"], h0, g0, dh1, "l0_in_dx", carry=W.carry("l0_in_dx"))
    dx = dx.reshape(B, S, D)

    G["norm_g"] = jnp.concatenate([d_g0, d_g1], axis=0)
    G["mem_norm_g"] = jnp.concatenate([d_mg0, d_mg1], axis=0)
    G["conv_dw_b"] = d_dwb
    G["conv_ln_g"], G["conv_ln_b"] = d_ln_g, d_ln_b
    return loss128[0, 0], dx, G


def _mla_in_perm(w):
    c1, c2 = Q_RANK, Q_RANK + KV_RANK
    c3 = c2 + MLA_ROPE
    c4 = c3 + MEM_WIDTH
    zero = jnp.zeros((w.shape[0], HALF_ROPE), w.dtype)
    return jnp.concatenate([w[:, :c1], w[:, c4:], w[:, c3:c4], w[:, c1:c2], w[:, c2:c2 + HALF_ROPE], zero,
                            w[:, c2 + HALF_ROPE:c3], zero], axis=1)


def _mla_in_unperm(g):
    z_w = g.shape[1] - (Q_RANK + MEM_WIDTH + KV_RANK + 128)
    z0, q0 = Q_RANK, Q_RANK + z_w
    k0 = q0 + MEM_WIDTH
    r = k0 + KV_RANK
    return jnp.concatenate([g[:, :Q_RANK], g[:, k0:r], g[:, r:r + HALF_ROPE], g[:, r + 64:r + 64 + HALF_ROPE],
                            g[:, q0:k0], g[:, z0:q0]], axis=1)


def _uq_perm(w):
    n = w.shape[0]
    w3 = w.reshape(n, MLA_HEADS, MLA_QK)
    zero = jnp.zeros((n, MLA_HEADS, HALF_ROPE), w.dtype)
    rope = jnp.concatenate([w3[:, :, MLA_NOPE:MLA_NOPE + HALF_ROPE], zero, w3[:, :, MLA_NOPE + HALF_ROPE:], zero], axis=2)
    return jnp.concatenate([w3[:, :, :MLA_NOPE].reshape(n, -1), rope.reshape(n, -1)], axis=1)


def _uq_unperm(g):
    n = g.shape[0]
    n_nope = MLA_HEADS * MLA_NOPE
    rope = g[:, n_nope:].reshape(n, MLA_HEADS, 128)
    return jnp.concatenate([g[:, :n_nope].reshape(n, MLA_HEADS, MLA_NOPE), rope[:, :, :HALF_ROPE],
                            rope[:, :, 64:64 + HALF_ROPE]], axis=2).reshape(n, -1)


_ROW_CUT = ("w_mem_kv", "w_out")
_COL_CUT = ("conv_w_in", "mla_w_in", "mla_w_uq", "mla_w_ukv", "conv_dw")
_BIG = ("w_mem_kv", "w_out", "conv_w_in", "mla_w_in", "mla_w_uq", "mla_w_ukv")
_SMALL_SHARDED = ("conv_dw", "mla_q_norm_g", "mla_kv_norm_g")
_REPLICATED = ("norm_g", "mem_norm_g", "conv_dw_b", "conv_ln_g", "conv_ln_b", "final_norm_g")
_PERM = {"mla_w_in": (_mla_in_perm, _mla_in_unperm), "mla_w_uq": (_uq_perm, _uq_unperm)}


def _join(n, blocks):
    if n in _ROW_CUT:
        _, L, r, c = blocks.shape
        return blocks.transpose(1, 0, 2, 3).reshape(L, N_DEV * r, c)
    if n in _COL_CUT:
        _, _, r, c = blocks.shape
        return blocks.reshape(N_DEV, r, c).transpose(1, 0, 2).reshape(r, N_DEV * c)
    return blocks.reshape(-1)


def _cut(n, full, shard_shape):
    if n in _ROW_CUT:
        L, r, c = shard_shape
        return full.reshape(L, N_DEV, r, c).transpose(1, 0, 2, 3)
    if n in _COL_CUT:
        _, r, c = shard_shape
        return full.reshape(r, N_DEV, c).transpose(1, 0, 2).reshape(N_DEV, 1, r, c)
    return full.reshape(N_DEV, 1, -1)


def _flat_pad(parts, size):
    flat = jnp.concatenate([p.reshape(-1) for p in parts])
    return jnp.concatenate([flat, jnp.zeros((size - flat.shape[0],), flat.dtype)])


SMALL_LANES = 128 * 8


def _as_tiles(flat_parts):
    total = sum(p.size for p in flat_parts)
    size = -(-total // SMALL_LANES) * SMALL_LANES
    return _flat_pad(flat_parts, size).reshape(8, size // 8)


def _split_flat(flat, like):
    out, o = [], 0
    for a in like:
        out.append(flat[o:o + a.size].reshape(a.shape))
        o += a.size
    return out


_HBM = pl.BlockSpec(memory_space=pltpu.HBM)
_VMEM = pl.BlockSpec(memory_space=pltpu.VMEM)


def _position():
    return lax.axis_index("x"), lax.axis_index("y"), lax.axis_index("c")


def _dma_sems(n):
    return [pltpu.SemaphoreType.DMA((n,)), pltpu.SemaphoreType.DMA((n,))]


def _run_stage(stage, name):
    n_in, n_out = len(stage.ins), len(stage.out_shapes)

    def body(*refs):
        ins, outs, sems = refs[:n_in], refs[n_in:n_in + n_out], refs[n_in + n_out:]
        stage.start(ins, outs, sems)
        stage.wait(ins, outs, sems)

    outs = _call(body, name, stage.out_shapes, in_specs=[_HBM] * n_in, out_specs=[_HBM] * n_out, scratch=stage.sems,
                 aliases=stage.aliases)(*stage.ins)
    stage.outs = list(outs)
    return stage.outs


def _gather_chips_stage(shards):
    n = len(shards)

    def copies(x_refs, out_refs, sems):
        send_sems, recv_sems, _ = sems
        x, y, c = _position()
        peers = [(x, y, 1 - c), (1 - x, y, c), (x, 1 - y, c), (1 - x, 1 - y, c)]
        out = []
        for a in range(n):
            for k, (px, py, pc) in enumerate(peers):
                send = pltpu.make_async_remote_copy(src_ref=x_refs[a], dst_ref=out_refs[a].at[4 * x + 2 * y + c],
                                                    send_sem=send_sems.at[4 * a + k], recv_sem=recv_sems.at[4 * a + k],
                                                    device_id=(px, py, pc), device_id_type=MESH)
                recv = pltpu.make_async_remote_copy(src_ref=x_refs[a], dst_ref=out_refs[a].at[4 * px + 2 * py + pc],
                                                    send_sem=send_sems.at[4 * a + k], recv_sem=recv_sems.at[4 * a + k],
                                                    device_id=(px, py, pc), device_id_type=MESH)
                out.append((send, recv))
        return out

    def local(x_refs, out_refs, sems):
        x, y, c = _position()
        return [pltpu.make_async_copy(x_refs[a], out_refs[a].at[4 * x + 2 * y + c], sems[2].at[a]) for a in range(n)]

    def start(x_refs, out_refs, sems):
        for cp in local(x_refs, out_refs, sems):
            cp.start()
        for send, _ in copies(x_refs, out_refs, sems):
            send.start()

    def wait(x_refs, out_refs, sems):
        for send, recv in copies(x_refs, out_refs, sems):
            recv.wait_recv()
            send.wait_send()
        for cp in local(x_refs, out_refs, sems):
            cp.wait()

    return _Stage(shards, [jax.ShapeDtypeStruct((N_DEV,) + a.shape, a.dtype) for a in shards],
                  _dma_sems(4 * n) + [pltpu.SemaphoreType.DMA((n,))], start, wait)


def _gather_sibling_stage(bufs):
    n = len(bufs)

    def copies(out_refs, sems):
        send_sems, recv_sems = sems
        x, y, c = _position()
        out = []
        for a in range(n):
            for j, (px, py) in enumerate([(1 - x, y), (x, 1 - y), (1 - x, 1 - y)]):
                mine, theirs = out_refs[a].at[4 * px + 2 * py + c], out_refs[a].at[4 * px + 2 * py + (1 - c)]
                send = pltpu.make_async_remote_copy(src_ref=mine, dst_ref=mine, send_sem=send_sems.at[3 * a + j],
                                                    recv_sem=recv_sems.at[3 * a + j], device_id=(x, y, 1 - c),
                                                    device_id_type=MESH)
                recv = pltpu.make_async_remote_copy(src_ref=mine, dst_ref=theirs, send_sem=send_sems.at[3 * a + j],
                                                    recv_sem=recv_sems.at[3 * a + j], device_id=(x, y, 1 - c),
                                                    device_id_type=MESH)
                out.append((send, recv))
        return out

    def start(_, out_refs, sems):
        for send, _r in copies(out_refs, sems):
            send.start()

    def wait(_, out_refs, sems):
        for send, recv in copies(out_refs, sems):
            recv.wait_recv()
            send.wait_send()

    return _Stage(bufs, [jax.ShapeDtypeStruct(b.shape, b.dtype) for b in bufs], _dma_sems(3 * n), start, wait,
                  aliases={a: a for a in range(n)})


def _all_gather_small(v, name):
    r, n = v.shape

    def body(x_ref, out_ref, send_sems, recv_sems, local_sem):
        x, y, c = _position()
        me = 4 * x + 2 * y + c
        mine = pltpu.make_async_copy(x_ref, out_ref.at[me], local_sem)
        mine.start()
        flips = [(fx, fy, fc) for fx in (0, 1) for fy in (0, 1) for fc in (0, 1)][1:]
        copies = []
        for k, (fx, fy, fc) in enumerate(flips):
            peer = (x ^ fx, y ^ fy, c ^ fc)
            cp = pltpu.make_async_remote_copy(src_ref=x_ref, dst_ref=out_ref.at[me], send_sem=send_sems.at[k],
                                              recv_sem=recv_sems.at[k], device_id=peer, device_id_type=MESH)
            cp.start()
            copies.append(cp)
        for k, (fx, fy, fc) in enumerate(flips):
            px, py, pc = x ^ fx, y ^ fy, c ^ fc
            src = out_ref.at[4 * px + 2 * py + pc]
            pltpu.make_async_remote_copy(src_ref=x_ref, dst_ref=src, send_sem=send_sems.at[k], recv_sem=recv_sems.at[k],
                                         device_id=(px, py, pc), device_id_type=MESH).wait_recv()
        for cp in copies:
            cp.wait_send()
        mine.wait()

    return _call(body, name, jax.ShapeDtypeStruct((N_DEV, r, n), v.dtype), in_specs=[_VMEM], out_specs=_VMEM,
                 scratch=_dma_sems(7) + [pltpu.SemaphoreType.DMA(())])(v)


def _reduce_sibling_stage(gs):
    n = len(gs)

    def copies(g_refs, out_refs, sems):
        send_sems, recv_sems = sems
        x, y, c = _position()
        return [pltpu.make_async_remote_copy(src_ref=g_refs[a].at[2 * k + (1 - c)], dst_ref=out_refs[a].at[k],
                                             send_sem=send_sems.at[4 * a + k], recv_sem=recv_sems.at[4 * a + k],
                                             device_id=(x, y, 1 - c), device_id_type=MESH)
                for a in range(n) for k in range(4)]

    def start(g_refs, out_refs, sems):
        for cp in copies(g_refs, out_refs, sems):
            cp.start()

    def wait(g_refs, out_refs, sems):
        for cp in copies(g_refs, out_refs, sems):
            cp.wait()

    return _Stage(gs, [jax.ShapeDtypeStruct((4,) + g.shape[1:], g.dtype) for g in gs], _dma_sems(4 * n), start, wait)


def _rows2d(shape):
    cols = shape[-1]
    rows = 1
    for s in shape[:-1]:
        rows *= s
    return rows, cols


def _add_own(g, recv, name):
    rows, cols = _rows2d(g.shape[1:])
    tr = _pick(rows, 256, 8)
    c = lax.axis_index("c").astype(jnp.int32).reshape(1)

    def body(c_ref, g_ref, r_ref, o_ref):
        o_ref[...] = (g_ref[...].astype(F32) + r_ref[...].astype(F32)).astype(o_ref.dtype)

    grid_spec = pltpu.PrefetchScalarGridSpec(
        num_scalar_prefetch=1, grid=(4, rows // tr),
        in_specs=[pl.BlockSpec((None, None, tr, cols), lambda k, i, c_ref: (k, c_ref[0], i, 0)),
                  pl.BlockSpec((None, tr, cols), lambda k, i, c_ref: (k, i, 0))],
        out_specs=pl.BlockSpec((None, tr, cols), lambda k, i, c_ref: (k, i, 0)))
    return _call(body, name, jax.ShapeDtypeStruct((4, rows, cols), g.dtype), grid_spec=grid_spec,
                 dims=("parallel", "parallel"))(c, g.reshape(4, 2, rows, cols), recv.reshape(4, rows, cols))


def _reduce_chips_stage(pas):
    n = len(pas)

    def copies(pa_refs, out_refs, sems):
        send_sems, recv_sems, _ = sems
        x, y, c = _position()
        my_chip = 2 * x + y
        out = []
        for a in range(n):
            for j, (px, py) in enumerate([(1 - x, y), (x, 1 - y), (1 - x, 1 - y)]):
                send = pltpu.make_async_remote_copy(src_ref=pa_refs[a].at[2 * px + py], dst_ref=out_refs[a].at[my_chip],
                                                    send_sem=send_sems.at[3 * a + j], recv_sem=recv_sems.at[3 * a + j],
                                                    device_id=(px, py, c), device_id_type=MESH)
                recv = pltpu.make_async_remote_copy(src_ref=pa_refs[a].at[2 * px + py], dst_ref=out_refs[a].at[2 * px + py],
                                                    send_sem=send_sems.at[3 * a + j], recv_sem=recv_sems.at[3 * a + j],
                                                    device_id=(px, py, c), device_id_type=MESH)
                out.append((send, recv))
        return out

    def local(pa_refs, out_refs, sems):
        x, y, _ = _position()
        return [pltpu.make_async_copy(pa_refs[a].at[2 * x + y], out_refs[a].at[2 * x + y], sems[2].at[a]) for a in range(n)]

    def start(pa_refs, out_refs, sems):
        for cp in local(pa_refs, out_refs, sems):
            cp.start()
        for send, _r in copies(pa_refs, out_refs, sems):
            send.start()

    def wait(pa_refs, out_refs, sems):
        for send, recv in copies(pa_refs, out_refs, sems):
            recv.wait_recv()
            send.wait_send()
        for cp in local(pa_refs, out_refs, sems):
            cp.wait()

    return _Stage(pas, [jax.ShapeDtypeStruct(pa.shape, pa.dtype) for pa in pas],
                  _dma_sems(3 * n) + [pltpu.SemaphoreType.DMA((n,))], start, wait)


def _adamw_math(w, g, m, v):
    m = ADAM_B1 * m + (1.0 - ADAM_B1) * g
    v = ADAM_B2 * v + (1.0 - ADAM_B2) * (g * g)
    m_hat = m / (1.0 - ADAM_B1 ** ADAM_STEP)
    v_hat = v / (1.0 - ADAM_B2 ** ADAM_STEP)
    delta = -ADAM_LR * (m_hat / (jnp.sqrt(v_hat) + ADAM_EPS) + ADAM_WD * w)
    return delta, m, v


def _sum_adamw(parts, w, m, v, name):
    n, rows, cols = parts.shape
    tr = _pick(rows, 128, 8)

    def body(p_ref, w_ref, m_ref, v_ref, g_ref, d_ref, nm_ref, nv_ref):
        g = p_ref[0].astype(F32)
        for k in range(1, n):
            g = g + p_ref[k].astype(F32)
        d, nm, nv = _adamw_math(w_ref[...], g, m_ref[...], v_ref[...])
        g_ref[...], d_ref[...], nm_ref[...], nv_ref[...] = g, d, nm, nv

    blk = pl.BlockSpec((tr, cols), lambda i: (i, 0))
    return _call(body, name, [jax.ShapeDtypeStruct((rows, cols), F32)] * 4, grid=(rows // tr,),
                 in_specs=[pl.BlockSpec((n, tr, cols), lambda i: (0, i, 0)), blk, blk, blk],
                 out_specs=[blk] * 4, dims=("parallel",))(parts, w, m, v)


_WEIGHTS = ("norm_g", "mem_norm_g", "w_mem_kv", "w_out", "conv_w_in", "conv_dw", "conv_dw_b", "conv_ln_g", "conv_ln_b",
            "mla_w_in", "mla_q_norm_g", "mla_w_uq", "mla_kv_norm_g", "mla_w_ukv", "final_norm_g")


_GATHER_GROUPS = {"a": ("conv_w_in",), "b": ("w_mem_kv", "w_out"), "c": ("mla_w_in", "mla_w_uq", "mla_w_ukv")}
_CARRIERS = {"l0_norm": ("gather chips", ("a",)), "l0_in": ("gather chips", ("b",)), "l0_dwconv": ("gather chips", ("c",)),
             "l0_ln": ("gather sibling", ("b",)), "l0_out": ("gather sibling", ("c",)),
             "l1_in_dx": ("reduce sibling", ("l1",)), "l0_ln_bwd": ("reduce sibling", ("l0a",)),
             "l0_out_dx": ("reduce chips", ("l1", 0, 2)), "l0_dwconv_bwd": ("reduce chips", ("l1", 2, 5)),
             "l0_in_dw": ("reduce chips", ("l0a",)), "l0_in_dx": ("reduce sibling alone, then chips", ("l0b",))}


class _Schedule:
    def __init__(self, w):
        self.w, self.full, self.gather, self.reduce = w, {}, {}, {}
        small = _all_gather_small(_as_tiles([w[n] for n in _SMALL_SHARDED]), "gather_small_weights").reshape(N_DEV, -1)
        o = 0
        for n in _SMALL_SHARDED:
            self.full[n] = _join(n, small[:, o:o + w[n].size].reshape((N_DEV,) + w[n].shape))
            o += w[n].size
        for n in _REPLICATED:
            self.full[n] = w[n]

    def carry(self, call):
        kind, (g, *part) = _CARRIERS[call]
        if kind == "gather chips":
            self.gather[g] = [_gather_chips_stage([self.w[n].astype(BF16) for n in _GATHER_GROUPS[g]])]
            return self.gather[g][0]
        if kind == "gather sibling":
            self.gather[g].append(_gather_sibling_stage(self.gather[g][0].outs))
            return self.gather[g][1]
        r = self.reduce[g]
        if kind == "reduce sibling":
            r["sibling"] = _reduce_sibling_stage(r["cut"])
            return r["sibling"]
        if kind != "reduce chips":
            r["sibling"] = _reduce_sibling_stage(r["cut"])
            _run_stage(r["sibling"], "reduce_sibling_" + g)
        if "partial" not in r:
            r["partial"] = [_add_own(c, s, "reduce_add_%s_%d" % (g, i))
                            for i, (c, s) in enumerate(zip(r["cut"], r["sibling"].outs))]
        lo, hi = part if part else (0, len(r["keys"]))
        stage = _reduce_chips_stage(r["partial"][lo:hi])
        r.setdefault("chips", []).append((r["keys"][lo:hi], stage))
        return stage

    def __getitem__(self, name):
        if name not in self.full:
            g = [k for k, names in _GATHER_GROUPS.items() if name in names][0]
            if len(self.gather[g]) == 1:
                self.gather[g].append(_gather_sibling_stage(self.gather[g][0].outs))
                _run_stage(self.gather[g][1], "gather_sibling_" + g)
            for n, buf in zip(_GATHER_GROUPS[g], self.gather[g][1].outs):
                self.full[n] = _PERM[n][0](_join(n, buf)) if n in _PERM else _join(n, buf)
        return self.full[name]

    def ready(self, group, grads, payload=BF16):
        keys, cut, small = [], [], []
        for (n, layer), g in grads.items():
            if n in _SMALL_SHARDED:
                small.append(_cut(n, g, self.w[n].shape).reshape(N_DEV, -1))
                continue
            keys.append((n, layer))
            if layer is not None:
                cut.append(g.reshape((N_DEV,) + self.w[n].shape[1:]).astype(payload))
            else:
                cut.append(_cut(n, _PERM[n][1](g) if n in _PERM else g, self.w[n].shape).astype(payload))
        if small:
            keys.append(("small", None))
            cut.append(jax.vmap(lambda r: _as_tiles([r]))(jnp.concatenate(small, axis=1)))
        self.reduce[group] = {"keys": keys, "cut": cut}

    def finish(self):
        out = {}
        for r in self.reduce.values():
            for keys, stage in r["chips"]:
                out.update(dict(zip(keys, stage.outs)))
        return out


def kernel(x, mem, positions, norm_g, mem_norm_g, w_mem_kv, w_out, conv_w_in, conv_dw, conv_dw_b, conv_ln_g, conv_ln_b, mla_w_in, mla_q_norm_g, mla_w_uq, mla_kv_norm_g, mla_w_ukv, final_norm_g, loss_target, m_norm_g, m_mem_norm_g, m_w_mem_kv, m_w_out, m_conv_w_in, m_conv_dw, m_conv_dw_b, m_conv_ln_g, m_conv_ln_b, m_mla_w_in, m_mla_q_norm_g, m_mla_w_uq, m_mla_kv_norm_g, m_mla_w_ukv, m_final_norm_g, v_norm_g, v_mem_norm_g, v_w_mem_kv, v_w_out, v_conv_w_in, v_conv_dw, v_conv_dw_b, v_conv_ln_g, v_conv_ln_b, v_mla_w_in, v_mla_q_norm_g, v_mla_w_uq, v_mla_kv_norm_g, v_mla_w_ukv, v_final_norm_g):
    w = dict(zip(_WEIGHTS, (norm_g, mem_norm_g, w_mem_kv, w_out, conv_w_in, conv_dw, conv_dw_b, conv_ln_g, conv_ln_b,
                            mla_w_in, mla_q_norm_g, mla_w_uq, mla_kv_norm_g, mla_w_ukv, final_norm_g)))
    m = dict(zip(_WEIGHTS, (m_norm_g, m_mem_norm_g, m_w_mem_kv, m_w_out, m_conv_w_in, m_conv_dw, m_conv_dw_b, m_conv_ln_g,
                            m_conv_ln_b, m_mla_w_in, m_mla_q_norm_g, m_mla_w_uq, m_mla_kv_norm_g, m_mla_w_ukv, m_final_norm_g)))
    v = dict(zip(_WEIGHTS, (v_norm_g, v_mem_norm_g, v_w_mem_kv, v_w_out, v_conv_w_in, v_conv_dw, v_conv_dw_b, v_conv_ln_g,
                            v_conv_ln_b, v_mla_w_in, v_mla_q_norm_g, v_mla_w_uq, v_mla_kv_norm_g, v_mla_w_ukv, v_final_norm_g)))

    sched = _Schedule(w)
    loss_local, dx, G = _forward_backward(x, mem, positions, loss_target, sched)
    loss = lax.psum(loss_local, ("x", "y", "c"))

    from_chips = sched.finish()
    out = [{}, {}, {}, {}]
    for n in _BIG:
        if n in _ROW_CUT:
            res = [_sum_adamw(from_chips[(n, l)], w[n][l], m[n][l], v[n][l], "adamw_%s_%d" % (n, l)) for l in range(w[n].shape[0])]
            res = [jnp.stack(r) for r in zip(*res)]
        else:
            rows, cols = _rows2d(w[n].shape)
            res = _sum_adamw(from_chips[(n, None)], w[n].reshape(rows, cols), m[n].reshape(rows, cols),
                             v[n].reshape(rows, cols), "adamw_" + n)
        for o, r in zip(out, res):
            o[n] = r.reshape(w[n].shape)
    small_like = [w[n] for n in _SMALL_SHARDED]
    res = _sum_adamw(from_chips[("small", None)], _as_tiles(small_like), _as_tiles([m[n] for n in _SMALL_SHARDED]),
                     _as_tiles([v[n] for n in _SMALL_SHARDED]), "adamw_small")
    for o, r in zip(out, res):
        for n, a in zip(_SMALL_SHARDED, _split_flat(r.reshape(-1), small_like)):
            o[n] = a

    rep_like = [w[n] for n in _REPLICATED]
    rep_parts = _all_gather_small(_as_tiles([G[n] for n in _REPLICATED]), "gather_replicated_grads")
    res = _sum_adamw(rep_parts, _as_tiles(rep_like), _as_tiles([m[n] for n in _REPLICATED]),
                     _as_tiles([v[n] for n in _REPLICATED]), "adamw_replicated")
    for o, r in zip(out, res):
        for n, a in zip(_REPLICATED, _split_flat(r.reshape(-1), rep_like)):
            o[n] = a

    return (loss, dx, *[out[0][n] for n in _WEIGHTS], *[out[1][n] for n in _WEIGHTS],
            *[out[2][n] for n in _WEIGHTS], *[out[3][n] for n in _WEIGHTS])
```

```python
import jax
import jax.numpy as jnp
from jax import lax
from jax.experimental import pallas as pl
from jax.experimental.pallas import tpu as pltpu

F32 = jnp.float32
BF16 = jnp.bfloat16
MESH = pl.DeviceIdType.MESH
N_DEV = 8
VMEM_LIMIT_BYTES = 48 * 1024 * 1024

MEM_HEADS, MEM_HEAD_DIM = 4, 128
MEM_WIDTH = MEM_HEADS * MEM_HEAD_DIM
CONV_KERNEL = 31
CONV_PAD = 32
MLA_HEADS, MLA_NOPE, MLA_ROPE = 12, 128, 64
MLA_QK = MLA_NOPE + MLA_ROPE
HALF_ROPE = MLA_ROPE // 2
Q_RANK, KV_RANK = 512, 256
ROPE_THETA = 10000.0
RMS_EPS = 1e-6
LN_EPS = 1e-5
ADAM_LR, ADAM_B1, ADAM_B2, ADAM_EPS, ADAM_WD, ADAM_STEP = 0.001, 0.9, 0.999, 1e-08, 0.01, 10
NEG = -1e30


class _Stage:
    def __init__(self, ins, out_shapes, sems, start, wait, aliases=None):
        self.ins, self.out_shapes, self.sems = list(ins), list(out_shapes), list(sems)
        self.start, self.wait, self.aliases, self.outs = start, wait, dict(aliases or {}), None


def _call(body, name, out_shape, grid=None, in_specs=None, out_specs=None, scratch=(), dims=None, grid_spec=None, aliases=None,
          carry=None):
    params = dict(vmem_limit_bytes=VMEM_LIMIT_BYTES)
    if dims is not None:
        params["dimension_semantics"] = dims
    kw = {}
    if carry is not None:
        single = not isinstance(out_shape, (list, tuple))
        main_out = [out_shape] if single else list(out_shape)
        main_specs = [out_specs] if single else list(out_specs)
        n_in, n_out, n_scr = len(in_specs), len(main_out), len(scratch)
        x_in, x_out = len(carry.ins), len(carry.out_shapes)
        inner, steps = body, tuple(grid)

        def body(*refs):
            ins, xin = refs[:n_in], refs[n_in:n_in + x_in]
            outs = refs[n_in + x_in:n_in + x_in + n_out]
            xout = refs[n_in + x_in + n_out:n_in + x_in + n_out + x_out]
            scr = refs[n_in + x_in + n_out + x_out:n_in + x_in + n_out + x_out + n_scr]
            xsem = refs[n_in + x_in + n_out + x_out + n_scr:]
            ids = [pl.program_id(a) for a in range(len(steps))]
            first, last = ids[0] == 0, ids[0] == steps[0] - 1
            for a in range(1, len(steps)):
                first = jnp.logical_and(first, ids[a] == 0)
                last = jnp.logical_and(last, ids[a] == steps[a] - 1)
            pl.when(first)(lambda: carry.start(xin, xout, xsem))
            inner(*ins, *outs, *scr)
            pl.when(last)(lambda: carry.wait(xin, xout, xsem))

        hbm = pl.BlockSpec(memory_space=pltpu.HBM)
        aliases = dict(aliases or {})
        aliases.update({n_in + k: n_out + v for k, v in carry.aliases.items()})
        res = _call(body, name, main_out + carry.out_shapes, grid=grid, in_specs=list(in_specs) + [hbm] * x_in,
                    out_specs=main_specs + [hbm] * x_out, scratch=list(scratch) + carry.sems, dims=dims, aliases=aliases)

        def run(*args):
            outs = res(*args, *carry.ins)
            carry.outs = list(outs[n_out:])
            return outs[0] if single else outs[:n_out]

        return run
    if aliases:
        kw["input_output_aliases"] = aliases
    if grid_spec is not None:
        kw["grid_spec"] = grid_spec
    else:
        if grid is not None:
            kw["grid"] = grid
        kw["in_specs"] = in_specs
        kw["out_specs"] = out_specs
        kw["scratch_shapes"] = list(scratch)
    return pl.pallas_call(body, name=name, out_shape=out_shape, compiler_params=pltpu.CompilerParams(**params), **kw)


def _pick(n, target, mult):
    best = None
    for d in range(mult, min(n, target) + 1, mult):
        if n % d == 0:
            best = d
    return n if best is None else best


_DOT_DIMS = {"nn": (((1,), (0,)), ((), ())), "nt": (((1,), (1,)), ((), ())), "tn": (((0,), (0,)), ((), ()))}


def _mm(a, b, mode, out_dtype, name, res=None, carry=None):
    if mode == "tn":
        a, mode = a.T, "nn"
    if mode == "nn":
        (M, K), N = a.shape, b.shape[1]
    else:
        (M, K), N = a.shape, b.shape[0]
    tm = _pick(M, 1024, 8)
    tn = _pick(N, 1536, 128)
    tk = _pick(K, 1536, 128)
    nk = K // tk
    has_res = res is not None

    def body(*refs):
        if has_res:
            a_ref, b_ref, r_ref, o_ref, acc = refs
        else:
            a_ref, b_ref, o_ref, acc = refs
        k = pl.program_id(2)
        part = lax.dot_general(a_ref[...].astype(BF16), b_ref[...].astype(BF16), _DOT_DIMS[mode],
                               preferred_element_type=F32)
        if nk == 1:
            o_ref[...] = (part + r_ref[...] if has_res else part).astype(o_ref.dtype)
            return

        @pl.when(k == 0)
        def _():
            acc[...] = part

        @pl.when(k > 0)
        def _():
            acc[...] += part

        @pl.when(k == nk - 1)
        def _():
            r = acc[...]
            if has_res:
                r = r + r_ref[...]
            o_ref[...] = r.astype(o_ref.dtype)

    a_spec = pl.BlockSpec((tm, tk), lambda i, j, k: (i, k))
    b_spec = {"nn": pl.BlockSpec((tk, tn), lambda i, j, k: (k, j)),
              "nt": pl.BlockSpec((tn, tk), lambda i, j, k: (j, k))}[mode]
    o_spec = pl.BlockSpec((tm, tn), lambda i, j, k: (i, j))
    in_specs = [a_spec, b_spec] + ([o_spec] if has_res else [])
    args = (a, b) + ((res,) if has_res else ())
    return _call(body, name, jax.ShapeDtypeStruct((M, N), out_dtype), grid=(M // tm, N // tn, nk),
                 in_specs=in_specs, out_specs=o_spec, scratch=[pltpu.VMEM((tm, tn), F32)],
                 dims=("parallel", "parallel", "arbitrary"), carry=carry)(*args)


def _views(rows):
    return [r if isinstance(r, tuple) else (r, r.shape[1], 0) for r in rows]


def _row_tile(T, rows):
    return min(T, 512 if max(w for _, w, _ in rows) <= 1024 else 256)


def _rowwise(f, rows, params, outs, name, carry=None, into=None):
    rows = _views(rows)
    T = rows[0][0].shape[0]
    tb = _row_tile(T, rows)
    nr, npar = len(rows), len(params)
    outs = [o if len(o) == 3 else (o[0], o[1], o[0]) for o in outs]
    into = into or []

    def body(*refs):
        vals = f(*[r[...].astype(F32) for r in refs[:nr]], *[p[...] for p in refs[nr:nr + npar]])
        for o_ref, v in zip(refs[nr + npar + len(into):], vals):
            o_ref[...] = v.astype(o_ref.dtype)

    row_spec = lambda w, cb=0: pl.BlockSpec((tb, w), lambda i: (i, cb))
    par_spec = lambda w: pl.BlockSpec((1, w), lambda i: (0, 0))
    out_shape = [jax.ShapeDtypeStruct((T, tw), dt) for _, dt, tw in outs]
    out_specs = [row_spec(w) for w, _, _ in outs]
    in_specs = [row_spec(w, cb) for _, w, cb in rows] + [par_spec(p.shape[1]) for p in params]
    args = [r[0] for r in rows] + list(params)
    aliases = {}
    for k, arr, cb in into:
        aliases[len(args)] = k
        in_specs.append(pl.BlockSpec(memory_space=pl.ANY))
        args.append(arr)
        out_shape[k] = jax.ShapeDtypeStruct(arr.shape, arr.dtype)
        out_specs[k] = row_spec(outs[k][0], cb)
    return _call(body, name, out_shape, grid=(T // tb,), in_specs=in_specs, out_specs=out_specs, dims=("parallel",),
                 carry=carry, aliases=aliases)(*args)


def _rowwise_bwd(f, rows, params, douts, n_diff, name, carry=None, into=None):
    rows, douts = _views(rows), _views(douts)
    T = rows[0][0].shape[0]
    tb = _row_tile(T, rows)
    nr, npar, nd = len(rows), len(params), len(douts)

    def body(*refs):
        rv = [r[...].astype(F32) for r in refs[:nr]]
        pv = [p[...] for p in refs[nr:nr + npar]]
        dv = [d[...].astype(F32) for d in refs[nr + npar:nr + npar + nd]]
        o_refs = refs[nr + npar + nd + (0 if into is None else 1):]
        fixed = rv[n_diff:]

        def g(*xs):
            return tuple(f(*xs[:n_diff], *fixed, *xs[n_diff:]))

        _, vjp = jax.vjp(g, *rv[:n_diff], *pv)
        grads = vjp(tuple(dv))
        for o_ref, gr in zip(o_refs[:n_diff], grads[:n_diff]):
            o_ref[...] = gr.astype(o_ref.dtype)
        first = pl.program_id(0) == 0
        for o_ref, gr in zip(o_refs[n_diff:], grads[n_diff:]):
            @pl.when(first)
            def _(o_ref=o_ref):
                o_ref[...] = jnp.zeros_like(o_ref)

            o_ref[...] += gr

    row_spec = lambda w, cb=0: pl.BlockSpec((tb, w), lambda i: (i, cb))
    par_spec = lambda w: pl.BlockSpec((1, w), lambda i: (0, 0))
    out_shape = ([jax.ShapeDtypeStruct((T, w), F32) for _, w, _ in rows[:n_diff]]
                 + [jax.ShapeDtypeStruct((1, p.shape[1]), F32) for p in params])
    out_specs = [row_spec(w) for _, w, _ in rows[:n_diff]] + [par_spec(p.shape[1]) for p in params]
    in_specs = ([row_spec(w, cb) for _, w, cb in rows] + [par_spec(p.shape[1]) for p in params]
                + [row_spec(w, cb) for _, w, cb in douts])
    args = [r[0] for r in rows] + list(params) + [d[0] for d in douts]
    aliases = None
    if into is not None:
        aliases = {len(args): 0}
        in_specs.append(pl.BlockSpec(memory_space=pl.ANY))
        args.append(into[0])
        out_shape[0] = jax.ShapeDtypeStruct(into[0].shape, into[0].dtype)
        out_specs[0] = row_spec(rows[0][1], into[1])
    return _call(body, name, out_shape, grid=(T // tb,), in_specs=in_specs, out_specs=out_specs,
                 dims=("arbitrary",), carry=carry, aliases=aliases)(*args)


def _sig(x):
    return 1.0 / (1.0 + jnp.exp(-x))


def _rms(x, g):
    return x * lax.rsqrt(jnp.mean(x * x, axis=-1, keepdims=True) + RMS_EPS) * g


def _f_rms(x, g):
    return (_rms(x, g),)


def _f_ln_silu(x, g, b):
    mu = jnp.mean(x, axis=-1, keepdims=True)
    xc = x - mu
    var = jnp.mean(xc * xc, axis=-1, keepdims=True)
    y = xc * lax.rsqrt(var + LN_EPS) * g + b
    return (y * _sig(y),)


def _rope128(x, cos_p, sin_p):
    return x * cos_p + pltpu.roll(x, 64, 1) * sin_p


def _rope128_t(d, cos_p, sin_p):
    return d * cos_p + pltpu.roll(d * sin_p, 64, 1)


def _f_rope(xq, xk, cos_p, sin_p):
    heads = [_rope128(xq[:, h * 128:(h + 1) * 128], cos_p, sin_p) for h in range(MLA_HEADS)]
    return (jnp.concatenate(heads, axis=1), _rope128(xk, cos_p, sin_p))


def _f_rope_t(dq, dk_heads, cos_p, sin_p):
    heads = [_rope128_t(dq[:, h * 128:(h + 1) * 128], cos_p, sin_p) for h in range(MLA_HEADS)]
    dk = dk_heads[:, 0:128]
    for h in range(1, MLA_HEADS):
        dk = dk + dk_heads[:, h * 128:(h + 1) * 128]
    return (jnp.concatenate(heads, axis=1), _rope128_t(dk, cos_p, sin_p))


GATE_LANES = 512


def _gate_out(ycat, proj, z_col, w_out, res, name, tb=1024, carry=None):
    T, width = ycat.shape
    D = w_out.shape[1]
    zb = z_col // GATE_LANES
    nk = width // GATE_LANES

    def body(y_ref, z_ref, w_ref, r_ref, o_ref, yt_ref, acc):
        k = pl.program_id(1)
        z = z_ref[...]
        y = y_ref[...] * (z * _sig(z))
        yt_ref[...] = y.T.astype(yt_ref.dtype)
        part = jnp.dot(y.astype(BF16), w_ref[...], preferred_element_type=F32)

        @pl.when(k == 0)
        def _():
            acc[...] = part

        @pl.when(k > 0)
        def _():
            acc[...] += part

        @pl.when(k == nk - 1)
        def _():
            o_ref[...] = acc[...] + r_ref[...]

    row = pl.BlockSpec((tb, D), lambda i, k: (i, 0))
    return _call(body, name, [jax.ShapeDtypeStruct((T, D), F32), jax.ShapeDtypeStruct((width, T), BF16)],
                 grid=(T // tb, nk),
                 in_specs=[pl.BlockSpec((tb, GATE_LANES), lambda i, k: (i, k)),
                           pl.BlockSpec((tb, GATE_LANES), lambda i, k: (i, zb + k)),
                           pl.BlockSpec((GATE_LANES, D), lambda i, k: (k, 0)), row],
                 out_specs=[row, pl.BlockSpec((GATE_LANES, tb), lambda i, k: (k, i))],
                 scratch=[pltpu.VMEM((tb, D), F32)], dims=("parallel", "arbitrary"), carry=carry)(ycat, proj, w_out, res)


def _out_dx_gate_bwd(dh, w_out, ycat, proj, z_col, name, tb=1024, carry=None):
    T, width = ycat.shape
    D = dh.shape[1]
    zb = z_col // GATE_LANES

    def body(dh_ref, w_ref, y_ref, z_ref, dycat_ref, dz_ref):
        d = lax.dot_general(dh_ref[...].astype(BF16), w_ref[...], _DOT_DIMS["nt"], preferred_element_type=F32)
        z = z_ref[...]
        s = _sig(z)
        dycat_ref[...] = d * (z * s)
        dz_ref[...] = (d * y_ref[...] * (s * (1.0 + z * (1.0 - s)))).astype(dz_ref.dtype)

    blk = pl.BlockSpec((tb, GATE_LANES), lambda i, c: (i, c))
    zblk = pl.BlockSpec((tb, GATE_LANES), lambda i, c: (i, zb + c))
    return _call(body, name, [jax.ShapeDtypeStruct((T, width), F32), jax.ShapeDtypeStruct(proj.shape, BF16)],
                 grid=(T // tb, width // GATE_LANES),
                 in_specs=[pl.BlockSpec((tb, D), lambda i, c: (i, 0)), pl.BlockSpec((GATE_LANES, D), lambda i, c: (c, 0)),
                           blk, zblk],
                 out_specs=[blk, zblk], dims=("parallel", "parallel"), carry=carry)(dh, w_out, ycat, proj)


def _in_dx_norm_bwd(d_proj, w_in, h, g, add, name, tm=512, carry=None):
    T, K = d_proj.shape
    D = w_in.shape[0]
    tk = _pick(K, 1536, 128)
    nk = K // tk

    def body(a_ref, b_ref, h_ref, g_ref, add_ref, dx_ref, dg_ref, acc):
        i, k = pl.program_id(0), pl.program_id(1)
        part = lax.dot_general(a_ref[...], b_ref[...], _DOT_DIMS["nt"], preferred_element_type=F32)

        @pl.when(jnp.logical_and(i == 0, k == 0))
        def _():
            dg_ref[...] = jnp.zeros_like(dg_ref)

        @pl.when(k == 0)
        def _():
            acc[...] = part

        @pl.when(k > 0)
        def _():
            acc[...] += part

        @pl.when(k == nk - 1)
        def _():
            _, vjp = jax.vjp(_rms, h_ref[...], g_ref[...])
            dh, dg = vjp(acc[...])
            dx_ref[...] = dh + add_ref[...]
            dg_ref[...] += dg

    row = pl.BlockSpec((tm, D), lambda i, k: (i, 0))
    par = pl.BlockSpec((1, D), lambda i, k: (0, 0))
    return _call(body, name, [jax.ShapeDtypeStruct((T, D), F32), jax.ShapeDtypeStruct((1, D), F32)], grid=(T // tm, nk),
                 in_specs=[pl.BlockSpec((tm, tk), lambda i, k: (i, k)), pl.BlockSpec((D, tk), lambda i, k: (0, k)),
                           row, par, row],
                 out_specs=[row, par], scratch=[pltpu.VMEM((tm, D), F32)], dims=("arbitrary", "arbitrary"),
                 carry=carry)(d_proj, w_in, h, g, add)


def _glu_bwd(proj, d_glu, d_proj, name, tb=256):
    T, w = d_glu.shape

    def body(a_ref, g_ref, d_ref, _, o_ref):
        s, d = _sig(g_ref[...]), d_ref[...]
        o_ref[:, 0:w] = (d * s).astype(o_ref.dtype)
        o_ref[:, w:2 * w] = (d * a_ref[...] * (s * (1.0 - s))).astype(o_ref.dtype)

    return _call(body, name, jax.ShapeDtypeStruct(d_proj.shape, d_proj.dtype), grid=(T // tb,),
                 in_specs=[pl.BlockSpec((tb, w), lambda i: (i, 0)), pl.BlockSpec((tb, w), lambda i: (i, 1)),
                           pl.BlockSpec((tb, w), lambda i: (i, 0)), pl.BlockSpec(memory_space=pl.ANY)],
                 out_specs=pl.BlockSpec((tb, 2 * w), lambda i: (i, 0)), dims=("parallel",),
                 aliases={3: 0})(proj, proj, d_glu, d_proj)


def _final_loss(h, tgt, g, name, tb=512):
    T, D = h.shape

    def body(h_ref, t_ref, g_ref, dh_ref, dg_ref, loss_ref):
        tv = t_ref[...]

        def rowloss(hh, gg):
            e = _rms(hh, gg) - tv
            return 0.5 * jnp.mean(e * e, axis=-1, keepdims=True)

        lr, vjp = jax.vjp(rowloss, h_ref[...], g_ref[...])
        dh, dg = vjp(jnp.ones_like(lr))
        dh_ref[...] = dh

        @pl.when(pl.program_id(0) == 0)
        def _():
            dg_ref[...] = jnp.zeros_like(dg_ref)
            loss_ref[...] = jnp.zeros_like(loss_ref)

        dg_ref[...] += dg
        loss_ref[...] += jnp.broadcast_to(jnp.sum(lr, axis=0, keepdims=True), loss_ref.shape)

    row = pl.BlockSpec((tb, D), lambda i: (i, 0))
    par = pl.BlockSpec((1, D), lambda i: (0, 0))
    return _call(body, name,
                 [jax.ShapeDtypeStruct((T, D), F32), jax.ShapeDtypeStruct((1, D), F32), jax.ShapeDtypeStruct((1, 128), F32)],
                 grid=(T // tb,), in_specs=[row, row, par],
                 out_specs=[row, par, pl.BlockSpec((1, 128), lambda i: (0, 0))], dims=("arbitrary",))(h, tgt, g)


CONV_ROWS = 128
CONV_LANES = 256


def _sublane_phases(pad, n):
    for r in range(1, 8):
        for c0 in range(0, n - 8, 256):
            rows = min(256, n - 8 - c0)
            pad[r, c0:c0 + rows, :] = pad[0, c0 + r:c0 + r + rows, :]


def _dwconv_fwd(proj, C, w, b, B, S, name, carry=None):
    cb = CONV_LANES
    off = CONV_PAD - (CONV_KERNEL - 1)

    def body(a_ref, g_ref, w_ref, b_ref, o_ref, pad):
        pad[0, 0:CONV_PAD, :] = jnp.zeros((CONV_PAD, cb), F32)
        for c0 in range(0, S, 256):
            pad[0, CONV_PAD + c0:CONV_PAD + c0 + 256, :] = a_ref[c0:c0 + 256, :] * _sig(g_ref[c0:c0 + 256, :])
        _sublane_phases(pad, S + CONV_PAD)
        for t0 in range(0, S, CONV_ROWS):
            acc = jnp.broadcast_to(b_ref[...], (CONV_ROWS, cb))
            for k in range(CONV_KERNEL):
                r, base = (off + k) % 8, t0 + (off + k) // 8 * 8
                acc = acc + w_ref[k:k + 1, :] * pad[r, base:base + CONV_ROWS, :]
            o_ref[t0:t0 + CONV_ROWS, :] = acc

    return _call(body, name, jax.ShapeDtypeStruct((B, S, C), F32), grid=(B, C // cb),
                 in_specs=[pl.BlockSpec((S, cb), lambda i, j: (i, j)), pl.BlockSpec((S, cb), lambda i, j: (i, C // cb + j)),
                           pl.BlockSpec((CONV_KERNEL, cb), lambda i, j: (0, j)),
                           pl.BlockSpec((1, cb), lambda i, j: (0, j))],
                 out_specs=pl.BlockSpec((None, S, cb), lambda i, j: (i, 0, j)),
                 scratch=[pltpu.VMEM((8, S + CONV_PAD, cb), F32)], dims=("parallel", "parallel"),
                 carry=carry)(proj, proj, w, b)


def _dwconv_bwd(proj, w, dy, name, carry=None):
    B, S, C = dy.shape
    cb = CONV_LANES
    groups = CONV_ROWS // 8

    def body(a_ref, g_ref, w_ref, dy_ref, dx_ref, dw_ref, db_ref, dypad, wacc):
        dypad[0, 0:S, :] = dy_ref[...]
        dypad[0, S:, :] = jnp.zeros((CONV_PAD, cb), F32)
        _sublane_phases(dypad, S + CONV_PAD)
        wacc[...] = jnp.zeros_like(wacc)
        for t0 in range(0, S, CONV_ROWS):
            xc = a_ref[t0:t0 + CONV_ROWS, :] * _sig(g_ref[t0:t0 + CONV_ROWS, :])
            acc = jnp.zeros((CONV_ROWS, cb), F32)
            for k in range(CONV_KERNEL):
                o = (CONV_KERNEL - 1) - k
                dys = dypad[o % 8, t0 + o // 8 * 8:t0 + o // 8 * 8 + CONV_ROWS, :]
                acc = acc + w_ref[k:k + 1, :] * dys
                wacc[k] += jnp.sum((dys * xc).reshape(groups, 8, cb), axis=0)
            wacc[CONV_KERNEL] += jnp.sum(dy_ref[t0:t0 + CONV_ROWS, :].reshape(groups, 8, cb), axis=0)
            dx_ref[t0:t0 + CONV_ROWS, :] = acc

        @pl.when(pl.program_id(1) == 0)
        def _():
            dw_ref[...] = jnp.zeros_like(dw_ref)
            db_ref[...] = jnp.zeros_like(db_ref)

        for k in range(CONV_KERNEL):
            dw_ref[k:k + 1, :] += jnp.sum(wacc[k], axis=0, keepdims=True)
        db_ref[...] += jnp.sum(wacc[CONV_KERNEL], axis=0, keepdims=True)

    blk = pl.BlockSpec((None, S, cb), lambda j, i: (i, 0, j))
    return _call(body, name,
                 [jax.ShapeDtypeStruct((B, S, C), F32), jax.ShapeDtypeStruct((CONV_KERNEL, C), F32),
                  jax.ShapeDtypeStruct((1, C), F32)],
                 grid=(C // cb, B),
                 in_specs=[pl.BlockSpec((S, cb), lambda j, i: (i, j)), pl.BlockSpec((S, cb), lambda j, i: (i, C // cb + j)),
                           pl.BlockSpec((CONV_KERNEL, cb), lambda j, i: (0, j)), blk],
                 out_specs=[blk, pl.BlockSpec((CONV_KERNEL, cb), lambda j, i: (0, j)),
                            pl.BlockSpec((1, cb), lambda j, i: (0, j))],
                 scratch=[pltpu.VMEM((8, S + CONV_PAD, cb), F32), pltpu.VMEM((CONV_KERNEL + 1, 8, cb), F32)],
                 dims=("parallel", "arbitrary"), carry=carry)(proj, proj, w, dy)


ATTN_TILE = {"fwd": 1024, "bwd": 1024, "cross fwd": 512}
ATTN_SUB = {"fwd": 256, "bwd": 512}


def _attn_shapes(Sq, Sk, causal, pass_):
    tq = min(Sq, ATTN_TILE[pass_ if causal or pass_ == "bwd" else "cross fwd"])
    tk = tq if causal else min(Sk, ATTN_TILE[pass_])
    return tq, tk, min(ATTN_SUB[pass_], tq)


def _causal_bias(n):
    r = lax.broadcasted_iota(jnp.int32, (n, n), 0)
    c = lax.broadcasted_iota(jnp.int32, (n, n), 1)
    return jnp.where(c <= r, 0.0, NEG).astype(F32)


def _mask_diagonal(s, bias):
    n, nc = s.shape
    if nc == n:
        return s + bias
    return jnp.concatenate([s[:, :nc - n], s[:, nc - n:] + bias], axis=1)


def _attn_fwd(q, q_c0, qr, k, k_c0, kr, v, v_c0, B, Sq, Sk, H, causal, scale, name, into=None, o_c0=0, o_width=None,
              kv_stride=1):
    tq, tk, sub = _attn_shapes(Sq, Sk, causal, "fwd")
    nq, nk, nsub = Sq // tq, Sk // tk, tq // sub
    rope = qr is not None

    def body(*refs):
        refs = list(refs)
        qn_ref = refs.pop(0)
        qr_ref = refs.pop(0) if rope else None
        kn_ref = refs.pop(0)
        kr_ref = refs.pop(0) if rope else None
        v_ref = refs.pop(0)
        if into is not None:
            refs.pop(0)
        o_ref, lse_ref, m_s, l_s, acc = refs
        qi = pl.program_id(2)
        m_s[...] = jnp.full_like(m_s, NEG)
        l_s[...] = jnp.zeros_like(l_s)
        acc[...] = jnp.zeros_like(acc)
        bias = _causal_bias(sub) if causal else None
        qs = []
        for r in range(nsub):
            qn = qn_ref[r * sub:(r + 1) * sub, :].astype(BF16)
            qs.append(jnp.concatenate([qn, qr_ref[r * sub:(r + 1) * sub, :]], axis=1) if rope else qn)

        def step(j, masked):
            ks = pl.ds(pl.multiple_of(j * tk, tk), tk)
            kk = jnp.concatenate([kn_ref[ks, :], kr_ref[ks, :]], axis=1) if rope else kn_ref[ks, :]
            vv = v_ref[ks, :]
            for r in range(nsub):
                rows = slice(r * sub, (r + 1) * sub)
                nc = (r + 1) * sub if masked else tk
                s = lax.dot_general(qs[r], kk[:nc], _DOT_DIMS["nt"], preferred_element_type=F32) * scale
                if masked:
                    s = _mask_diagonal(s, bias)
                m_old = m_s[rows, :]
                m_new = jnp.maximum(m_old, jnp.max(s, axis=-1, keepdims=True))
                p = jnp.exp(s - m_new)
                alpha = jnp.exp(m_old - m_new)
                l_s[rows, :] = alpha * l_s[rows, :] + jnp.sum(p, axis=-1, keepdims=True)
                acc[rows, :] = alpha * acc[rows, :] + jnp.dot(p.astype(BF16), vv[:nc], preferred_element_type=F32)
                m_s[rows, :] = m_new

        def unmasked(j, carry):
            step(j, False)
            return carry

        if causal:
            lax.fori_loop(0, qi, unmasked, 0)
            step(qi, True)
        else:
            lax.fori_loop(0, nk, unmasked, 0)
        o_ref[...] = (acc[...] / l_s[...]).astype(o_ref.dtype)
        lse_ref[...] = m_s[...] + jnp.log(l_s[...])

    qspec = lambda c0: pl.BlockSpec((tq, 128), lambda b, h, i: (b * nq + i, c0 + h))
    kspec = lambda c0: pl.BlockSpec((Sk, 128), lambda b, h, i: (b, c0 + kv_stride * h))
    in_specs, args = [qspec(q_c0)], [q]
    if rope:
        in_specs.append(qspec(0)); args.append(qr)
    in_specs.append(kspec(k_c0)); args.append(k)
    if rope:
        in_specs.append(pl.BlockSpec((Sk, 128), lambda b, h, i: (b, 0))); args.append(kr)
    in_specs.append(kspec(v_c0)); args.append(v)
    aliases = {}
    if into is not None:
        aliases = {len(args): 0}
        in_specs.append(pl.BlockSpec(memory_space=pl.ANY)); args.append(into)
        o_shape = jax.ShapeDtypeStruct(into.shape, into.dtype)
    else:
        o_shape = jax.ShapeDtypeStruct((B * Sq, o_width), F32)
    return _call(body, name, [o_shape, jax.ShapeDtypeStruct((B * H, Sq, 1), F32)], grid=(B, H, nq), in_specs=in_specs,
                 out_specs=[qspec(o_c0), pl.BlockSpec((None, tq, 1), lambda b, h, i: (b * H + h, i, 0))],
                 scratch=[pltpu.VMEM((tq, 1), F32), pltpu.VMEM((tq, 1), F32), pltpu.VMEM((tq, 128), F32)],
                 dims=("parallel", "parallel", "arbitrary"), aliases=aliases)(*args)


def _attn_bwd(q, q_c0, qr, k, k_c0, kr, v, v_c0, o, do, o_c0, lse, B, Sq, Sk, H, causal, scale, name, dq_into=None,
              kv_stride=1):
    tq, tk, sub = _attn_shapes(Sq, Sk, causal, "bwd")
    nq, nk, nsub = Sq // tq, Sk // tk, tq // sub
    rope = qr is not None
    dk_w = 256 if rope else 128

    def body(*refs):
        refs = list(refs)
        qn_ref = refs.pop(0)
        qr_ref = refs.pop(0) if rope else None
        kn_ref = refs.pop(0)
        kr_ref = refs.pop(0) if rope else None
        v_ref, o_ref, do_ref, lse_ref = refs[:4]
        refs = refs[4 + (0 if dq_into is None else 1):]
        dqn_ref = refs.pop(0)
        dqr_ref = refs.pop(0) if rope else None
        dkn_ref = refs.pop(0)
        dkr_ref = refs.pop(0) if rope else None
        dv_ref = None if rope else refs.pop(0)
        q_s, do_s, dl_s, dq_acc, dk_acc, dv_acc = refs
        kj = pl.program_id(2)

        @pl.when(kj == 0)
        def _():
            qn = qn_ref[...].astype(BF16)
            q_s[...] = jnp.concatenate([qn, qr_ref[...]], axis=1) if rope else qn
            dof = do_ref[...]
            do_s[...] = dof.astype(BF16)
            dl_s[...] = jnp.sum(dof * o_ref[...], axis=-1, keepdims=True)
            dq_acc[...] = jnp.zeros_like(dq_acc)

        kk = jnp.concatenate([kn_ref[...], kr_ref[...]], axis=1) if rope else kn_ref[...]
        vv = v_ref[...]
        bias = _causal_bias(sub) if causal else None
        dk_acc[...] = jnp.zeros_like(dk_acc)
        dv_acc[...] = jnp.zeros_like(dv_acc)

        def step(i, masked):
            for r in range(nsub):
                rows = pl.ds(pl.multiple_of(i * tq + r * sub, sub), sub)
                qq, dob = q_s[rows, :], do_s[rows, :]
                nc = (r + 1) * sub if masked else tk
                kc, vc = kk[:nc], vv[:nc]
                s = lax.dot_general(qq, kc, _DOT_DIMS["nt"], preferred_element_type=F32) * scale
                if masked:
                    s = _mask_diagonal(s, bias)
                p = jnp.exp(s - lse_ref[rows, :])
                dp = lax.dot_general(dob, vc, _DOT_DIMS["nt"], preferred_element_type=F32)
                ds = (p * (dp - dl_s[rows, :]) * scale).astype(BF16)
                dv_acc[0:nc, :] += lax.dot_general(p.astype(BF16), dob, _DOT_DIMS["tn"], preferred_element_type=F32)
                dk_acc[0:nc, :] += lax.dot_general(ds, qq, _DOT_DIMS["tn"], preferred_element_type=F32)
                dq_acc[rows, :] += jnp.dot(ds, kc, preferred_element_type=F32)

        def unmasked(i, carry):
            step(i, False)
            return carry

        if causal:
            step(kj, True)
            lax.fori_loop(kj + 1, nq, unmasked, 0)
        else:
            lax.fori_loop(0, nq, unmasked, 0)
        if rope:
            dkn_ref[...] = jnp.concatenate([dk_acc[:, 0:128], dv_acc[...]], axis=1).astype(dkn_ref.dtype)
            dkr_ref[...] = dk_acc[:, 128:256]
        else:
            dkn_ref[...] = dk_acc[...]
            dv_ref[...] = dv_acc[...]

        @pl.when(kj == nk - 1)
        def _():
            dqn_ref[...] = dq_acc[:, 0:128].astype(dqn_ref.dtype)
            if rope:
                dqr_ref[...] = dq_acc[:, 128:256]

    qspec = lambda c0: pl.BlockSpec((Sq, 128), lambda b, h, j: (b, c0 + h))
    kspec = lambda c0: pl.BlockSpec((tk, 128), lambda b, h, j: (b * nk + j, c0 + kv_stride * h))
    in_specs, args = [qspec(q_c0)], [q]
    if rope:
        in_specs.append(qspec(0)); args.append(qr)
    in_specs.append(kspec(k_c0)); args.append(k)
    if rope:
        in_specs.append(pl.BlockSpec((tk, 128), lambda b, h, j: (b * nk + j, 0))); args.append(kr)
    in_specs += [kspec(v_c0), qspec(o_c0), qspec(o_c0), pl.BlockSpec((None, Sq, 1), lambda b, h, j: (b * H + h, 0, 0))]
    args += [v, o, do, lse]
    h_rows_q = jax.ShapeDtypeStruct((B * Sq, H * 128), F32)
    h_rows_k = jax.ShapeDtypeStruct((B * Sk, H * 128), F32)
    out_shape, out_specs, aliases = [h_rows_q], [qspec(0)], None
    if rope:
        out_shape = [jax.ShapeDtypeStruct((B * Sq, 2 * H * 128), BF16)]
    if dq_into is not None:
        aliases = {len(args): 0}
        in_specs.append(pl.BlockSpec(memory_space=pl.ANY)); args.append(dq_into[0])
        out_shape, out_specs = [jax.ShapeDtypeStruct(dq_into[0].shape, dq_into[0].dtype)], [qspec(dq_into[1])]
    if rope:
        out_shape.append(h_rows_q); out_specs.append(qspec(0))
    hspec = lambda w: pl.BlockSpec((tk, w), lambda b, h, j: (b * nk + j, h))
    if rope:
        out_shape += [jax.ShapeDtypeStruct((B * Sk, H * 256), BF16), h_rows_k]
        out_specs += [hspec(256), hspec(128)]
    else:
        out_shape += [h_rows_k, h_rows_k]
        out_specs += [hspec(128), hspec(128)]
    return _call(body, name, out_shape, grid=(B, H, nk), in_specs=in_specs, out_specs=out_specs,
                 scratch=[pltpu.VMEM((Sq, dk_w), BF16), pltpu.VMEM((Sq, 128), BF16), pltpu.VMEM((Sq, 1), F32),
                          pltpu.VMEM((Sq, dk_w), F32), pltpu.VMEM((tk, dk_w), F32), pltpu.VMEM((tk, 128), F32)],
                 dims=("parallel", "parallel", "arbitrary"), aliases=aliases)(*args)


def _mem_attention_fwd(proj, q_col, ycat, mem2, mem_g, w_mem, B, S, tag):
    M = mem2.shape[0] // B
    (memn,) = _rowwise(_f_rms, [mem2], [mem_g], [(mem2.shape[1], BF16)], tag + "_memnorm")
    kvm = _mm(memn, w_mem, "nn", BF16, tag + "_memkv")
    o_c0 = ycat.shape[1] // 128 - MEM_HEADS
    ycat, lse = _attn_fwd(proj, q_col // 128, None, kvm, 0, None, kvm, MEM_HEADS, B, S, M, MEM_HEADS, False,
                          MEM_HEAD_DIM ** -0.5, tag + "_memattn", into=ycat, o_c0=o_c0)
    return ycat, (memn, kvm, lse)


def _mem_attention_bwd(proj, q_col, ycat, d_ycat, d_proj, saved, mem2, mem_g, w_mem, B, S, tag):
    memn, kvm, lse = saved
    M = mem2.shape[0] // B
    o_c0 = ycat.shape[1] // 128 - MEM_HEADS
    d_q, d_k, d_v = _attn_bwd(proj, q_col // 128, None, kvm, 0, None, kvm, MEM_HEADS, ycat, d_ycat, o_c0, lse, B, S, M,
                              MEM_HEADS, False, MEM_HEAD_DIM ** -0.5, tag + "_memattn_bwd", dq_into=(d_proj, q_col // 128))
    d_kvm = jnp.concatenate([d_k, d_v], axis=1).astype(BF16)
    d_w_mem = _mm(memn, d_kvm, "tn", F32, tag + "_memkv_dw")
    d_memn = _mm(d_kvm, w_mem, "nt", F32, tag + "_memkv_dx")
    _, d_mem_g = _rowwise_bwd(_f_rms, [mem2], [mem_g], [d_memn], 1, tag + "_memnorm_bwd")
    return d_q, d_w_mem, d_mem_g


def _rope_tables(positions):
    inv_freq = 1.0 / (ROPE_THETA ** (jnp.arange(0, MLA_ROPE, 2, dtype=F32) / MLA_ROPE))
    ang = positions.astype(F32).reshape(-1, 1) * inv_freq
    cos, sin, zero = jnp.cos(ang), jnp.sin(ang), jnp.zeros_like(ang)
    return jnp.concatenate([cos, zero, cos, zero], axis=1), jnp.concatenate([-sin, zero, sin, zero], axis=1)


def _forward_backward(x, mem, positions, target, W):
    B, S, D = x.shape
    T = B * S
    conv_w = W["conv_dw"].shape[1]
    mix_w = 2 * D
    h0 = x.reshape(T, D)
    mem2 = mem.reshape(-1, D)
    tgt = target.reshape(T, D)
    row = lambda v: v.reshape(1, -1)
    n_nope = MLA_HEADS * MLA_NOPE

    g0 = row(W["norm_g"][0])
    (u0,) = _rowwise(_f_rms, [h0], [g0], [(D, BF16)], "l0_norm", carry=W.carry("l0_norm"))
    proj0 = _mm(u0, W["conv_w_in"], "nn", F32, "l0_in", carry=W.carry("l0_in"))
    qm0_col, z0_col = 2 * conv_w, 2 * conv_w + MEM_WIDTH
    dw, dwb = W["conv_dw"], row(W["conv_dw_b"][0])
    cv = _dwconv_fwd(proj0, conv_w, dw, dwb, B, S, "l0_dwconv", carry=W.carry("l0_dwconv")).reshape(T, conv_w)
    ln_g, ln_b = row(W["conv_ln_g"][0]), row(W["conv_ln_b"][0])
    (ycat0,) = _rowwise(_f_ln_silu, [cv], [ln_g, ln_b], [(conv_w, F32, mix_w)], "l0_ln", carry=W.carry("l0_ln"))
    mg0 = row(W["mem_norm_g"][0])
    ycat0, mem_saved0 = _mem_attention_fwd(proj0, qm0_col, ycat0, mem2, mg0, W["w_mem_kv"][0], B, S, "l0")
    h1, y0_t = _gate_out(ycat0, proj0, z0_col, W["w_out"][0], h0, "l0_out", carry=W.carry("l0_out"))

    g1 = row(W["norm_g"][1])
    (u1,) = _rowwise(_f_rms, [h1], [g1], [(D, BF16)], "l1_norm")
    proj1 = _mm(u1, W["mla_w_in"], "nn", F32, "l1_in")
    z1_col = Q_RANK
    qm1_col = z1_col + mix_w
    ckv_col = qm1_col + MEM_WIDTH
    kr_col = ckv_col + KV_RANK
    cq, ckv = (proj1, Q_RANK, 0), (proj1, KV_RANK, ckv_col // KV_RANK)
    qg, kvg = row(W["mla_q_norm_g"]), row(W["mla_kv_norm_g"])
    (cqn,) = _rowwise(_f_rms, [cq], [qg], [(Q_RANK, BF16)], "l1_qnorm")
    (ckvn,) = _rowwise(_f_rms, [ckv], [kvg], [(KV_RANK, BF16)], "l1_kvnorm")
    qf = _mm(cqn, W["mla_w_uq"], "nn", F32, "l1_uq")
    kvf = _mm(ckvn, W["mla_w_ukv"], "nn", BF16, "l1_ukv")
    cos_p, sin_p = _rope_tables(positions)
    qr, kr = _rowwise(_f_rope, [(qf, n_nope, 1), (proj1, 128, kr_col // 128), cos_p, sin_p], [],
                      [(n_nope, BF16), (128, BF16)], "l1_rope")
    scale1 = MLA_QK ** -0.5
    ycat1, lse1 = _attn_fwd(qf, 0, qr, kvf, 0, kr, kvf, 1, B, S, S, MLA_HEADS, True, scale1, "l1_attn",
                            o_width=mix_w, kv_stride=2)
    mg1 = row(W["mem_norm_g"][1])
    ycat1, mem_saved1 = _mem_attention_fwd(proj1, qm1_col, ycat1, mem2, mg1, W["w_mem_kv"][1], B, S, "l1")
    h2, y1_t = _gate_out(ycat1, proj1, z1_col, W["w_out"][1], h1, "l1_out")

    gf = row(W["final_norm_g"])
    dh2, d_gf, loss128 = _final_loss(h2, tgt, gf, "final_loss")
    G = {"final_norm_g": d_gf.reshape(-1)}
    L1 = {}

    d_wout1 = _mm(y1_t, dh2, "nn", F32, "l1_out_dw")
    d_ycat1, d_proj1 = _out_dx_gate_bwd(dh2, W["w_out"][1], ycat1, proj1, z1_col, "l1_out_dx")
    d_proj1, d_wmem1, d_mg1 = _mem_attention_bwd(proj1, qm1_col, ycat1, d_ycat1, d_proj1, mem_saved1, mem2, mg1,
                                                 W["w_mem_kv"][1], B, S, "l1")
    d_qf, d_qr, d_kvf, d_kr_heads = _attn_bwd(qf, 0, qr, kvf, 0, kr, kvf, 1, ycat1, d_ycat1, 0, lse1, B, S, S,
                                              MLA_HEADS, True, scale1, "l1_attn_bwd", kv_stride=2)
    d_qf, d_proj1 = _rowwise(_f_rope_t, [d_qr, d_kr_heads, cos_p, sin_p], [], [(n_nope, F32), (128, F32)], "l1_rope_bwd",
                             into=[(0, d_qf, 1), (1, d_proj1, kr_col // 128)])
    d_cqn = _mm(d_qf, W["mla_w_uq"], "nt", F32, "l1_uq_dx")
    L1[("mla_w_uq", None)] = _mm(cqn, d_qf, "tn", F32, "l1_uq_dw")
    d_ckvn = _mm(d_kvf, W["mla_w_ukv"], "nt", F32, "l1_ukv_dx")
    L1[("mla_w_ukv", None)] = _mm(ckvn, d_kvf, "tn", F32, "l1_ukv_dw")
    d_proj1, d_qg = _rowwise_bwd(_f_rms, [cq], [qg], [d_cqn], 1, "l1_qnorm_bwd", into=(d_proj1, cq[2]))
    d_proj1, d_kvg = _rowwise_bwd(_f_rms, [ckv], [kvg], [d_ckvn], 1, "l1_kvnorm_bwd", into=(d_proj1, ckv[2]))
    L1[("w_mem_kv", 1)] = d_wmem1
    L1[("mla_w_in", None)] = _mm(u1, d_proj1, "tn", F32, "l1_in_dw")
    L1[("w_out", 1)] = d_wout1
    W.ready("l1", L1)
    dh1, d_g1 = _in_dx_norm_bwd(d_proj1, W["mla_w_in"], h1, g1, dh2, "l1_in_dx", carry=W.carry("l1_in_dx"))

    d_wout0 = _mm(y0_t, dh1, "nn", F32, "l0_out_dw")
    d_ycat0, d_proj0 = _out_dx_gate_bwd(dh1, W["w_out"][0], ycat0, proj0, z0_col, "l0_out_dx", carry=W.carry("l0_out_dx"))
    d_proj0, d_wmem0, d_mg0 = _mem_attention_bwd(proj0, qm0_col, ycat0, d_ycat0, d_proj0, mem_saved0, mem2, mg0,
                                                 W["w_mem_kv"][0], B, S, "l0")
    W.ready("l0a", {("w_mem_kv", 0): d_wmem0, ("w_out", 0): d_wout0})
    d_cv, d_ln_g, d_ln_b = _rowwise_bwd(_f_ln_silu, [cv], [ln_g, ln_b], [(d_ycat0, conv_w, 0)], 1, "l0_ln_bwd",
                                        carry=W.carry("l0_ln_bwd"))
    d_glu, d_dw, d_dwb = _dwconv_bwd(proj0, dw, d_cv.reshape(B, S, conv_w), "l0_dwconv_bwd",
                                     carry=W.carry("l0_dwconv_bwd"))
    d_proj0 = _glu_bwd(proj0, d_glu.reshape(T, conv_w), d_proj0, "l0_glu_bwd")
    d_conv_w_in = _mm(u0, d_proj0, "tn", F32, "l0_in_dw", carry=W.carry("l0_in_dw"))
    W.ready("l0b", {("conv_w_in", None): d_conv_w_in, ("conv_dw", None): d_dw,
                    ("mla_q_norm_g", None): d_qg.reshape(-1), ("mla_kv_norm_g", None): d_kvg.reshape(-1)})
    dx, d_g0 = _in_dx_norm_bwd(d_proj0, W["conv_w_in"], h0, g0, dh1, "l0_in_dx", carry=W.carry("l0_in_dx"))
    dx = dx.reshape(B, S, D)

    G["norm_g"] = jnp.concatenate([d_g0, d_g1], axis=0)
    G["mem_norm_g"] = jnp.concatenate([d_mg0, d_mg1], axis=0)
    G["conv_dw_b"] = d_dwb
    G["conv_ln_g"], G["conv_ln_b"] = d_ln_g, d_ln_b
    return loss128[0, 0], dx, G


def _mla_in_perm(w):
    c1, c2 = Q_RANK, Q_RANK + KV_RANK
    c3 = c2 + MLA_ROPE
    c4 = c3 + MEM_WIDTH
    zero = jnp.zeros((w.shape[0], HALF_ROPE), w.dtype)
    return jnp.concatenate([w[:, :c1], w[:, c4:], w[:, c3:c4], w[:, c1:c2], w[:, c2:c2 + HALF_ROPE], zero,
                            w[:, c2 + HALF_ROPE:c3], zero], axis=1)


def _mla_in_unperm(g):
    z_w = g.shape[1] - (Q_RANK + MEM_WIDTH + KV_RANK + 128)
    z0, q0 = Q_RANK, Q_RANK + z_w
    k0 = q0 + MEM_WIDTH
    r = k0 + KV_RANK
    return jnp.concatenate([g[:, :Q_RANK], g[:, k0:r], g[:, r:r + HALF_ROPE], g[:, r + 64:r + 64 + HALF_ROPE],
                            g[:, q0:k0], g[:, z0:q0]], axis=1)


def _uq_perm(w):
    n = w.shape[0]
    w3 = w.reshape(n, MLA_HEADS, MLA_QK)
    zero = jnp.zeros((n, MLA_HEADS, HALF_ROPE), w.dtype)
    rope = jnp.concatenate([w3[:, :, MLA_NOPE:MLA_NOPE + HALF_ROPE], zero, w3[:, :, MLA_NOPE + HALF_ROPE:], zero], axis=2)
    return jnp.concatenate([w3[:, :, :MLA_NOPE].reshape(n, -1), rope.reshape(n, -1)], axis=1)


def _uq_unperm(g):
    n = g.shape[0]
    n_nope = MLA_HEADS * MLA_NOPE
    rope = g[:, n_nope:].reshape(n, MLA_HEADS, 128)
    return jnp.concatenate([g[:, :n_nope].reshape(n, MLA_HEADS, MLA_NOPE), rope[:, :, :HALF_ROPE],
                            rope[:, :, 64:64 + HALF_ROPE]], axis=2).reshape(n, -1)


_ROW_CUT = ("w_mem_kv", "w_out")
_COL_CUT = ("conv_w_in", "mla_w_in", "mla_w_uq", "mla_w_ukv", "conv_dw")
_BIG = ("w_mem_kv", "w_out", "conv_w_in", "mla_w_in", "mla_w_uq", "mla_w_ukv")
_SMALL_SHARDED = ("conv_dw", "mla_q_norm_g", "mla_kv_norm_g")
_REPLICATED = ("norm_g", "mem_norm_g", "conv_dw_b", "conv_ln_g", "conv_ln_b", "final_norm_g")
_PERM = {"mla_w_in": (_mla_in_perm, _mla_in_unperm), "mla_w_uq": (_uq_perm, _uq_unperm)}


def _join(n, blocks):
    if n in _ROW_CUT:
        _, L, r, c = blocks.shape
        return blocks.transpose(1, 0, 2, 3).reshape(L, N_DEV * r, c)
    if n in _COL_CUT:
        _, _, r, c = blocks.shape
        return blocks.reshape(N_DEV, r, c).transpose(1, 0, 2).reshape(r, N_DEV * c)
    return blocks.reshape(-1)


def _cut(n, full, shard_shape):
    if n in _ROW_CUT:
        L, r, c = shard_shape
        return full.reshape(L, N_DEV, r, c).transpose(1, 0, 2, 3)
    if n in _COL_CUT:
        _, r, c = shard_shape
        return full.reshape(r, N_DEV, c).transpose(1, 0, 2).reshape(N_DEV, 1, r, c)
    return full.reshape(N_DEV, 1, -1)


def _flat_pad(parts, size):
    flat = jnp.concatenate([p.reshape(-1) for p in parts])
    return jnp.concatenate([flat, jnp.zeros((size - flat.shape[0],), flat.dtype)])


SMALL_LANES = 128 * 8


def _as_tiles(flat_parts):
    total = sum(p.size for p in flat_parts)
    size = -(-total // SMALL_LANES) * SMALL_LANES
    return _flat_pad(flat_parts, size).reshape(8, size // 8)


def _split_flat(flat, like):
    out, o = [], 0
    for a in like:
        out.append(flat[o:o + a.size].reshape(a.shape))
        o += a.size
    return out


_HBM = pl.BlockSpec(memory_space=pltpu.HBM)
_VMEM = pl.BlockSpec(memory_space=pltpu.VMEM)


def _position():
    return lax.axis_index("x"), lax.axis_index("y"), lax.axis_index("c")


def _dma_sems(n):
    return [pltpu.SemaphoreType.DMA((n,)), pltpu.SemaphoreType.DMA((n,))]


def _run_stage(stage, name):
    n_in, n_out = len(stage.ins), len(stage.out_shapes)

    def body(*refs):
        ins, outs, sems = refs[:n_in], refs[n_in:n_in + n_out], refs[n_in + n_out:]
        stage.start(ins, outs, sems)
        stage.wait(ins, outs, sems)

    outs = _call(body, name, stage.out_shapes, in_specs=[_HBM] * n_in, out_specs=[_HBM] * n_out, scratch=stage.sems,
                 aliases=stage.aliases)(*stage.ins)
    stage.outs = list(outs)
    return stage.outs


def _gather_chips_stage(shards):
    n = len(shards)

    def copies(x_refs, out_refs, sems):
        send_sems, recv_sems, _ = sems
        x, y, c = _position()
        peers = [(x, y, 1 - c), (1 - x, y, c), (x, 1 - y, c), (1 - x, 1 - y, c)]
        out = []
        for a in range(n):
            for k, (px, py, pc) in enumerate(peers):
                send = pltpu.make_async_remote_copy(src_ref=x_refs[a], dst_ref=out_refs[a].at[4 * x + 2 * y + c],
                                                    send_sem=send_sems.at[4 * a + k], recv_sem=recv_sems.at[4 * a + k],
                                                    device_id=(px, py, pc), device_id_type=MESH)
                recv = pltpu.make_async_remote_copy(src_ref=x_refs[a], dst_ref=out_refs[a].at[4 * px + 2 * py + pc],
                                                    send_sem=send_sems.at[4 * a + k], recv_sem=recv_sems.at[4 * a + k],
                                                    device_id=(px, py, pc), device_id_type=MESH)
                out.append((send, recv))
        return out

    def local(x_refs, out_refs, sems):
        x, y, c = _position()
        return [pltpu.make_async_copy(x_refs[a], out_refs[a].at[4 * x + 2 * y + c], sems[2].at[a]) for a in range(n)]

    def start(x_refs, out_refs, sems):
        for cp in local(x_refs, out_refs, sems):
            cp.start()
        for send, _ in copies(x_refs, out_refs, sems):
            send.start()

    def wait(x_refs, out_refs, sems):
        for send, recv in copies(x_refs, out_refs, sems):
            recv.wait_recv()
            send.wait_send()
        for cp in local(x_refs, out_refs, sems):
            cp.wait()

    return _Stage(shards, [jax.ShapeDtypeStruct((N_DEV,) + a.shape, a.dtype) for a in shards],
                  _dma_sems(4 * n) + [pltpu.SemaphoreType.DMA((n,))], start, wait)


def _gather_sibling_stage(bufs):
    n = len(bufs)

    def copies(out_refs, sems):
        send_sems, recv_sems = sems
        x, y, c = _position()
        out = []
        for a in range(n):
            for j, (px, py) in enumerate([(1 - x, y), (x, 1 - y), (1 - x, 1 - y)]):
                mine, theirs = out_refs[a].at[4 * px + 2 * py + c], out_refs[a].at[4 * px + 2 * py + (1 - c)]
                send = pltpu.make_async_remote_copy(src_ref=mine, dst_ref=mine, send_sem=send_sems.at[3 * a + j],
                                                    recv_sem=recv_sems.at[3 * a + j], device_id=(x, y, 1 - c),
                                                    device_id_type=MESH)
                recv = pltpu.make_async_remote_copy(src_ref=mine, dst_ref=theirs, send_sem=send_sems.at[3 * a + j],
                                                    recv_sem=recv_sems.at[3 * a + j], device_id=(x, y, 1 - c),
                                                    device_id_type=MESH)
                out.append((send, recv))
        return out

    def start(_, out_refs, sems):
        for send, _r in copies(out_refs, sems):
            send.start()

    def wait(_, out_refs, sems):
        for send, recv in copies(out_refs, sems):
            recv.wait_recv()
            send.wait_send()

    return _Stage(bufs, [jax.ShapeDtypeStruct(b.shape, b.dtype) for b in bufs], _dma_sems(3 * n), start, wait,
                  aliases={a: a for a in range(n)})


def _all_gather_small(v, name):
    r, n = v.shape

    def body(x_ref, out_ref, send_sems, recv_sems, local_sem):
        x, y, c = _position()
        me = 4 * x + 2 * y + c
        mine = pltpu.make_async_copy(x_ref, out_ref.at[me], local_sem)
        mine.start()
        flips = [(fx, fy, fc) for fx in (0, 1) for fy in (0, 1) for fc in (0, 1)][1:]
        copies = []
        for k, (fx, fy, fc) in enumerate(flips):
            peer = (x ^ fx, y ^ fy, c ^ fc)
            cp = pltpu.make_async_remote_copy(src_ref=x_ref, dst_ref=out_ref.at[me], send_sem=send_sems.at[k],
                                              recv_sem=recv_sems.at[k], device_id=peer, device_id_type=MESH)
            cp.start()
            copies.append(cp)
        for k, (fx, fy, fc) in enumerate(flips):
            px, py, pc = x ^ fx, y ^ fy, c ^ fc
            src = out_ref.at[4 * px + 2 * py + pc]
            pltpu.make_async_remote_copy(src_ref=x_ref, dst_ref=src, send_sem=send_sems.at[k], recv_sem=recv_sems.at[k],
                                         device_id=(px, py, pc), device_id_type=MESH).wait_recv()
        for cp in copies:
            cp.wait_send()
        mine.wait()

    return _call(body, name, jax.ShapeDtypeStruct((N_DEV, r, n), v.dtype), in_specs=[_VMEM], out_specs=_VMEM,
                 scratch=_dma_sems(7) + [pltpu.SemaphoreType.DMA(())])(v)


def _reduce_sibling_stage(gs):
    n = len(gs)

    def copies(g_refs, out_refs, sems):
        send_sems, recv_sems = sems
        x, y, c = _position()
        return [pltpu.make_async_remote_copy(src_ref=g_refs[a].at[2 * k + (1 - c)], dst_ref=out_refs[a].at[k],
                                             send_sem=send_sems.at[4 * a + k], recv_sem=recv_sems.at[4 * a + k],
                                             device_id=(x, y, 1 - c), device_id_type=MESH)
                for a in range(n) for k in range(4)]

    def start(g_refs, out_refs, sems):
        for cp in copies(g_refs, out_refs, sems):
            cp.start()

    def wait(g_refs, out_refs, sems):
        for cp in copies(g_refs, out_refs, sems):
            cp.wait()

    return _Stage(gs, [jax.ShapeDtypeStruct((4,) + g.shape[1:], g.dtype) for g in gs], _dma_sems(4 * n), start, wait)


def _rows2d(shape):
    cols = shape[-1]
    rows = 1
    for s in shape[:-1]:
        rows *= s
    return rows, cols


def _add_own(g, recv, name):
    rows, cols = _rows2d(g.shape[1:])
    tr = _pick(rows, 256, 8)
    c = lax.axis_index("c").astype(jnp.int32).reshape(1)

    def body(c_ref, g_ref, r_ref, o_ref):
        o_ref[...] = (g_ref[...].astype(F32) + r_ref[...].astype(F32)).astype(o_ref.dtype)

    grid_spec = pltpu.PrefetchScalarGridSpec(
        num_scalar_prefetch=1, grid=(4, rows // tr),
        in_specs=[pl.BlockSpec((None, None, tr, cols), lambda k, i, c_ref: (k, c_ref[0], i, 0)),
                  pl.BlockSpec((None, tr, cols), lambda k, i, c_ref: (k, i, 0))],
        out_specs=pl.BlockSpec((None, tr, cols), lambda k, i, c_ref: (k, i, 0)))
    return _call(body, name, jax.ShapeDtypeStruct((4, rows, cols), g.dtype), grid_spec=grid_spec,
                 dims=("parallel", "parallel"))(c, g.reshape(4, 2, rows, cols), recv.reshape(4, rows, cols))


def _reduce_chips_stage(pas):
    n = len(pas)

    def copies(pa_refs, out_refs, sems):
        send_sems, recv_sems, _ = sems
        x, y, c = _position()
        my_chip = 2 * x + y
        out = []
        for a in range(n):
            for j, (px, py) in enumerate([(1 - x, y), (x, 1 - y), (1 - x, 1 - y)]):
                send = pltpu.make_async_remote_copy(src_ref=pa_refs[a].at[2 * px + py], dst_ref=out_refs[a].at[my_chip],
                                                    send_sem=send_sems.at[3 * a + j], recv_sem=recv_sems.at[3 * a + j],
                                                    device_id=(px, py, c), device_id_type=MESH)
                recv = pltpu.make_async_remote_copy(src_ref=pa_refs[a].at[2 * px + py], dst_ref=out_refs[a].at[2 * px + py],
                                                    send_sem=send_sems.at[3 * a + j], recv_sem=recv_sems.at[3 * a + j],
                                                    device_id=(px, py, c), device_id_type=MESH)
                out.append((send, recv))
        return out

    def local(pa_refs, out_refs, sems):
        x, y, _ = _position()
        return [pltpu.make_async_copy(pa_refs[a].at[2 * x + y], out_refs[a].at[2 * x + y], sems[2].at[a]) for a in range(n)]

    def start(pa_refs, out_refs, sems):
        for cp in local(pa_refs, out_refs, sems):
            cp.start()
        for send, _r in copies(pa_refs, out_refs, sems):
            send.start()

    def wait(pa_refs, out_refs, sems):
        for send, recv in copies(pa_refs, out_refs, sems):
            recv.wait_recv()
            send.wait_send()
        for cp in local(pa_refs, out_refs, sems):
            cp.wait()

    return _Stage(pas, [jax.ShapeDtypeStruct(pa.shape, pa.dtype) for pa in pas],
                  _dma_sems(3 * n) + [pltpu.SemaphoreType.DMA((n,))], start, wait)


def _adamw_math(w, g, m, v):
    m = ADAM_B1 * m + (1.0 - ADAM_B1) * g
    v = ADAM_B2 * v + (1.0 - ADAM_B2) * (g * g)
    m_hat = m / (1.0 - ADAM_B1 ** ADAM_STEP)
    v_hat = v / (1.0 - ADAM_B2 ** ADAM_STEP)
    delta = -ADAM_LR * (m_hat / (jnp.sqrt(v_hat) + ADAM_EPS) + ADAM_WD * w)
    return delta, m, v


def _sum_adamw(parts, w, m, v, name):
    n, rows, cols = parts.shape
    tr = _pick(rows, 128, 8)

    def body(p_ref, w_ref, m_ref, v_ref, g_ref, d_ref, nm_ref, nv_ref):
        g = p_ref[0].astype(F32)
        for k in range(1, n):
            g = g + p_ref[k].astype(F32)
        d, nm, nv = _adamw_math(w_ref[...], g, m_ref[...], v_ref[...])
        g_ref[...], d_ref[...], nm_ref[...], nv_ref[...] = g, d, nm, nv

    blk = pl.BlockSpec((tr, cols), lambda i: (i, 0))
    return _call(body, name, [jax.ShapeDtypeStruct((rows, cols), F32)] * 4, grid=(rows // tr,),
                 in_specs=[pl.BlockSpec((n, tr, cols), lambda i: (0, i, 0)), blk, blk, blk],
                 out_specs=[blk] * 4, dims=("parallel",))(parts, w, m, v)


_WEIGHTS = ("norm_g", "mem_norm_g", "w_mem_kv", "w_out", "conv_w_in", "conv_dw", "conv_dw_b", "conv_ln_g", "conv_ln_b",
            "mla_w_in", "mla_q_norm_g", "mla_w_uq", "mla_kv_norm_g", "mla_w_ukv", "final_norm_g")


_GATHER_GROUPS = {"a": ("conv_w_in",), "b": ("w_mem_kv", "w_out"), "c": ("mla_w_in", "mla_w_uq", "mla_w_ukv")}
_CARRIERS = {"l0_norm": ("gather chips", ("a",)), "l0_in": ("gather chips", ("b",)), "l0_dwconv": ("gather chips", ("c",)),
             "l0_ln": ("gather sibling", ("b",)), "l0_out": ("gather sibling", ("c",)),
             "l1_in_dx": ("reduce sibling", ("l1",)), "l0_ln_bwd": ("reduce sibling", ("l0a",)),
             "l0_out_dx": ("reduce chips", ("l1", 0, 2)), "l0_dwconv_bwd": ("reduce chips", ("l1", 2, 5)),
             "l0_in_dw": ("reduce chips", ("l0a",)), "l0_in_dx": ("reduce sibling alone, then chips", ("l0b",))}


class _Schedule:
    def __init__(self, w):
        self.w, self.full, self.gather, self.reduce = w, {}, {}, {}
        small = _all_gather_small(_as_tiles([w[n] for n in _SMALL_SHARDED]), "gather_small_weights").reshape(N_DEV, -1)
        o = 0
        for n in _SMALL_SHARDED:
            self.full[n] = _join(n, small[:, o:o + w[n].size].reshape((N_DEV,) + w[n].shape))
            o += w[n].size
        for n in _REPLICATED:
            self.full[n] = w[n]

    def carry(self, call):
        kind, (g, *part) = _CARRIERS[call]
        if kind == "gather chips":
            self.gather[g] = [_gather_chips_stage([self.w[n].astype(BF16) for n in _GATHER_GROUPS[g]])]
            return self.gather[g][0]
        if kind == "gather sibling":
            self.gather[g].append(_gather_sibling_stage(self.gather[g][0].outs))
            return self.gather[g][1]
        r = self.reduce[g]
        if kind == "reduce sibling":
            r["sibling"] = _reduce_sibling_stage(r["cut"])
            return r["sibling"]
        if kind != "reduce chips":
            r["sibling"] = _reduce_sibling_stage(r["cut"])
            _run_stage(r["sibling"], "reduce_sibling_" + g)
        if "partial" not in r:
            r["partial"] = [_add_own(c, s, "reduce_add_%s_%d" % (g, i))
                            for i, (c, s) in enumerate(zip(r["cut"], r["sibling"].outs))]
        lo, hi = part if part else (0, len(r["keys"]))
        stage = _reduce_chips_stage(r["partial"][lo:hi])
        r.setdefault("chips", []).append((r["keys"][lo:hi], stage))
        return stage

    def __getitem__(self, name):
        if name not in self.full:
            g = [k for k, names in _GATHER_GROUPS.items() if name in names][0]
            if len(self.gather[g]) == 1:
                self.gather[g].append(_gather_sibling_stage(self.gather[g][0].outs))
                _run_stage(self.gather[g][1], "gather_sibling_" + g)
            for n, buf in zip(_GATHER_GROUPS[g], self.gather[g][1].outs):
                self.full[n] = _PERM[n][0](_join(n, buf)) if n in _PERM else _join(n, buf)
        return self.full[name]

    def ready(self, group, grads, payload=BF16):
        keys, cut, small = [], [], []
        for (n, layer), g in grads.items():
            if n in _SMALL_SHARDED:
                small.append(_cut(n, g, self.w[n].shape).reshape(N_DEV, -1))
                continue
            keys.append((n, layer))
            if layer is not None:
                cut.append(g.reshape((N_DEV,) + self.w[n].shape[1:]).astype(payload))
            else:
                cut.append(_cut(n, _PERM[n][1](g) if n in _PERM else g, self.w[n].shape).astype(payload))
        if small:
            keys.append(("small", None))
            cut.append(jax.vmap(lambda r: _as_tiles([r]))(jnp.concatenate(small, axis=1)))
        self.reduce[group] = {"keys": keys, "cut": cut}

    def finish(self):
        out = {}
        for r in self.reduce.values():
            for keys, stage in r["chips"]:
                out.update(dict(zip(keys, stage.outs)))
        return out


def kernel(x, mem, positions, norm_g, mem_norm_g, w_mem_kv, w_out, conv_w_in, conv_dw, conv_dw_b, conv_ln_g, conv_ln_b, mla_w_in, mla_q_norm_g, mla_w_uq, mla_kv_norm_g, mla_w_ukv, final_norm_g, loss_target, m_norm_g, m_mem_norm_g, m_w_mem_kv, m_w_out, m_conv_w_in, m_conv_dw, m_conv_dw_b, m_conv_ln_g, m_conv_ln_b, m_mla_w_in, m_mla_q_norm_g, m_mla_w_uq, m_mla_kv_norm_g, m_mla_w_ukv, m_final_norm_g, v_norm_g, v_mem_norm_g, v_w_mem_kv, v_w_out, v_conv_w_in, v_conv_dw, v_conv_dw_b, v_conv_ln_g, v_conv_ln_b, v_mla_w_in, v_mla_q_norm_g, v_mla_w_uq, v_mla_kv_norm_g, v_mla_w_ukv, v_final_norm_g):
    w = dict(zip(_WEIGHTS, (norm_g, mem_norm_g, w_mem_kv, w_out, conv_w_in, conv_dw, conv_dw_b, conv_ln_g, conv_ln_b,
                            mla_w_in, mla_q_norm_g, mla_w_uq, mla_kv_norm_g, mla_w_ukv, final_norm_g)))
    m = dict(zip(_WEIGHTS, (m_norm_g, m_mem_norm_g, m_w_mem_kv, m_w_out, m_conv_w_in, m_conv_dw, m_conv_dw_b, m_conv_ln_g,
                            m_conv_ln_b, m_mla_w_in, m_mla_q_norm_g, m_mla_w_uq, m_mla_kv_norm_g, m_mla_w_ukv, m_final_norm_g)))
    v = dict(zip(_WEIGHTS, (v_norm_g, v_mem_norm_g, v_w_mem_kv, v_w_out, v_conv_w_in, v_conv_dw, v_conv_dw_b, v_conv_ln_g,
                            v_conv_ln_b, v_mla_w_in, v_mla_q_norm_g, v_mla_w_uq, v_mla_kv_norm_g, v_mla_w_ukv, v_final_norm_g)))

    sched = _Schedule(w)
    loss_local, dx, G = _forward_backward(x, mem, positions, loss_target, sched)
    loss = lax.psum(loss_local, ("x", "y", "c"))

    from_chips = sched.finish()
    out = [{}, {}, {}, {}]
    for n in _BIG:
        if n in _ROW_CUT:
            res = [_sum_adamw(from_chips[(n, l)], w[n][l], m[n][l], v[n][l], "adamw_%s_%d" % (n, l)) for l in range(w[n].shape[0])]
            res = [jnp.stack(r) for r in zip(*res)]
        else:
            rows, cols = _rows2d(w[n].shape)
            res = _sum_adamw(from_chips[(n, None)], w[n].reshape(rows, cols), m[n].reshape(rows, cols),
                             v[n].reshape(rows, cols), "adamw_" + n)
        for o, r in zip(out, res):
            o[n] = r.reshape(w[n].shape)
    small_like = [w[n] for n in _SMALL_SHARDED]
    res = _sum_adamw(from_chips[("small", None)], _as_tiles(small_like), _as_tiles([m[n] for n in _SMALL_SHARDED]),
                     _as_tiles([v[n] for n in _SMALL_SHARDED]), "adamw_small")
    for o, r in zip(out, res):
        for n, a in zip(_SMALL_SHARDED, _split_flat(r.reshape(-1), small_like)):
            o[n] = a

    rep_like = [w[n] for n in _REPLICATED]
    rep_parts = _all_gather_small(_as_tiles([G[n] for n in _REPLICATED]), "gather_replicated_grads")
    res = _sum_adamw(rep_parts, _as_tiles(rep_like), _as_tiles([m[n] for n in _REPLICATED]),
                     _as_tiles([v[n] for n in _REPLICATED]), "adamw_replicated")
    for o, r in zip(out, res):
        for n, a in zip(_REPLICATED, _split_flat(r.reshape(-1), rep_like)):
            o[n] = a

    return (loss, dx, *[out[0][n] for n in _WEIGHTS], *[out[1][n] for n in _WEIGHTS],
            *[out[2][n] for n in _WEIGHTS], *[out[3][n] for n in _WEIGHTS])
```

```python
import jax
import jax.numpy as jnp
from jax import lax
from jax.experimental import pallas as pl
from jax.experimental.pallas import tpu as pltpu

F32 = jnp.float32
BF16 = jnp.bfloat16
MESH = pl.DeviceIdType.MESH
N_DEV = 8
VMEM_LIMIT_BYTES = 48 * 1024 * 1024

MEM_HEADS, MEM_HEAD_DIM = 4, 128
MEM_WIDTH = MEM_HEADS * MEM_HEAD_DIM
CONV_KERNEL = 31
CONV_PAD = 32
MLA_HEADS, MLA_NOPE, MLA_ROPE = 12, 128, 64
MLA_QK = MLA_NOPE + MLA_ROPE
HALF_ROPE = MLA_ROPE // 2
Q_RANK, KV_RANK = 512, 256
ROPE_THETA = 10000.0
RMS_EPS = 1e-6
LN_EPS = 1e-5
ADAM_LR, ADAM_B1, ADAM_B2, ADAM_EPS, ADAM_WD, ADAM_STEP = 0.001, 0.9, 0.999, 1e-08, 0.01, 10
NEG = -1e30


class _Stage:
    def __init__(self, ins, out_shapes, sems, start, wait, aliases=None):
        self.ins, self.out_shapes, self.sems = list(ins), list(out_shapes), list(sems)
        self.start, self.wait, self.aliases, self.outs = start, wait, dict(aliases or {}), None


def _call(body, name, out_shape, grid=None, in_specs=None, out_specs=None, scratch=(), dims=None, grid_spec=None, aliases=None,
          carry=None):
    params = dict(vmem_limit_bytes=VMEM_LIMIT_BYTES)
    if dims is not None:
        params["dimension_semantics"] = dims
    kw = {}
    if carry is not None:
        single = not isinstance(out_shape, (list, tuple))
        main_out = [out_shape] if single else list(out_shape)
        main_specs = [out_specs] if single else list(out_specs)
        n_in, n_out, n_scr = len(in_specs), len(main_out), len(scratch)
        x_in, x_out = len(carry.ins), len(carry.out_shapes)
        inner, steps = body, tuple(grid)

        def body(*refs):
            ins, xin = refs[:n_in], refs[n_in:n_in + x_in]
            outs = refs[n_in + x_in:n_in + x_in + n_out]
            xout = refs[n_in + x_in + n_out:n_in + x_in + n_out + x_out]
            scr = refs[n_in + x_in + n_out + x_out:n_in + x_in + n_out + x_out + n_scr]
            xsem = refs[n_in + x_in + n_out + x_out + n_scr:]
            ids = [pl.program_id(a) for a in range(len(steps))]
            first, last = ids[0] == 0, ids[0] == steps[0] - 1
            for a in range(1, len(steps)):
                first = jnp.logical_and(first, ids[a] == 0)
                last = jnp.logical_and(last, ids[a] == steps[a] - 1)
            pl.when(first)(lambda: carry.start(xin, xout, xsem))
            inner(*ins, *outs, *scr)
            pl.when(last)(lambda: carry.wait(xin, xout, xsem))

        hbm = pl.BlockSpec(memory_space=pltpu.HBM)
        aliases = dict(aliases or {})
        aliases.update({n_in + k: n_out + v for k, v in carry.aliases.items()})
        res = _call(body, name, main_out + carry.out_shapes, grid=grid, in_specs=list(in_specs) + [hbm] * x_in,
                    out_specs=main_specs + [hbm] * x_out, scratch=list(scratch) + carry.sems, dims=dims, aliases=aliases)

        def run(*args):
            outs = res(*args, *carry.ins)
            carry.outs = list(outs[n_out:])
            return outs[0] if single else outs[:n_out]

        return run
    if aliases:
        kw["input_output_aliases"] = aliases
    if grid_spec is not None:
        kw["grid_spec"] = grid_spec
    else:
        if grid is not None:
            kw["grid"] = grid
        kw["in_specs"] = in_specs
        kw["out_specs"] = out_specs
        kw["scratch_shapes"] = list(scratch)
    return pl.pallas_call(body, name=name, out_shape=out_shape, compiler_params=pltpu.CompilerParams(**params), **kw)


def _pick(n, target, mult):
    best = None
    for d in range(mult, min(n, target) + 1, mult):
        if n % d == 0:
            best = d
    return n if best is None else best


_DOT_DIMS = {"nn": (((1,), (0,)), ((), ())), "nt": (((1,), (1,)), ((), ())), "tn": (((0,), (0,)), ((), ()))}


def _mm(a, b, mode, out_dtype, name, res=None, carry=None):
    if mode == "tn":
        a, mode = a.T, "nn"
    if mode == "nn":
        (M, K), N = a.shape, b.shape[1]
    else:
        (M, K), N = a.shape, b.shape[0]
    tm = _pick(M, 1024, 8)
    tn = _pick(N, 1536, 128)
    tk = _pick(K, 1536, 128)
    nk = K // tk
    has_res = res is not None

    def body(*refs):
        if has_res:
            a_ref, b_ref, r_ref, o_ref, acc = refs
        else:
            a_ref, b_ref, o_ref, acc = refs
        k = pl.program_id(2)
        part = lax.dot_general(a_ref[...].astype(BF16), b_ref[...].astype(BF16), _DOT_DIMS[mode],
                               preferred_element_type=F32)
        if nk == 1:
            o_ref[...] = (part + r_ref[...] if has_res else part).astype(o_ref.dtype)
            return

        @pl.when(k == 0)
        def _():
            acc[...] = part

        @pl.when(k > 0)
        def _():
            acc[...] += part

        @pl.when(k == nk - 1)
        def _():
            r = acc[...]
            if has_res:
                r = r + r_ref[...]
            o_ref[...] = r.astype(o_ref.dtype)

    a_spec = pl.BlockSpec((tm, tk), lambda i, j, k: (i, k))
    b_spec = {"nn": pl.BlockSpec((tk, tn), lambda i, j, k: (k, j)),
              "nt": pl.BlockSpec((tn, tk), lambda i, j, k: (j, k))}[mode]
    o_spec = pl.BlockSpec((tm, tn), lambda i, j, k: (i, j))
    in_specs = [a_spec, b_spec] + ([o_spec] if has_res else [])
    args = (a, b) + ((res,) if has_res else ())
    return _call(body, name, jax.ShapeDtypeStruct((M, N), out_dtype), grid=(M // tm, N // tn, nk),
                 in_specs=in_specs, out_specs=o_spec, scratch=[pltpu.VMEM((tm, tn), F32)],
                 dims=("parallel", "parallel", "arbitrary"), carry=carry)(*args)


def _views(rows):
    return [r if isinstance(r, tuple) else (r, r.shape[1], 0) for r in rows]


def _row_tile(T, rows):
    return min(T, 512 if max(w for _, w, _ in rows) <= 1024 else 256)


def _rowwise(f, rows, params, outs, name, carry=None, into=None):
    rows = _views(rows)
    T = rows[0][0].shape[0]
    tb = _row_tile(T, rows)
    nr, npar = len(rows), len(params)
    outs = [o if len(o) == 3 else (o[0], o[1], o[0]) for o in outs]
    into = into or []

    def body(*refs):
        vals = f(*[r[...].astype(F32) for r in refs[:nr]], *[p[...] for p in refs[nr:nr + npar]])
        for o_ref, v in zip(refs[nr + npar + len(into):], vals):
            o_ref[...] = v.astype(o_ref.dtype)

    row_spec = lambda w, cb=0: pl.BlockSpec((tb, w), lambda i: (i, cb))
    par_spec = lambda w: pl.BlockSpec((1, w), lambda i: (0, 0))
    out_shape = [jax.ShapeDtypeStruct((T, tw), dt) for _, dt, tw in outs]
    out_specs = [row_spec(w) for w, _, _ in outs]
    in_specs = [row_spec(w, cb) for _, w, cb in rows] + [par_spec(p.shape[1]) for p in params]
    args = [r[0] for r in rows] + list(params)
    aliases = {}
    for k, arr, cb in into:
        aliases[len(args)] = k
        in_specs.append(pl.BlockSpec(memory_space=pl.ANY))
        args.append(arr)
        out_shape[k] = jax.ShapeDtypeStruct(arr.shape, arr.dtype)
        out_specs[k] = row_spec(outs[k][0], cb)
    return _call(body, name, out_shape, grid=(T // tb,), in_specs=in_specs, out_specs=out_specs, dims=("parallel",),
                 carry=carry, aliases=aliases)(*args)


def _rowwise_bwd(f, rows, params, douts, n_diff, name, carry=None, into=None):
    rows, douts = _views(rows), _views(douts)
    T = rows[0][0].shape[0]
    tb = _row_tile(T, rows)
    nr, npar, nd = len(rows), len(params), len(douts)

    def body(*refs):
        rv = [r[...].astype(F32) for r in refs[:nr]]
        pv = [p[...] for p in refs[nr:nr + npar]]
        dv = [d[...].astype(F32) for d in refs[nr + npar:nr + npar + nd]]
        o_refs = refs[nr + npar + nd + (0 if into is None else 1):]
        fixed = rv[n_diff:]

        def g(*xs):
            return tuple(f(*xs[:n_diff], *fixed, *xs[n_diff:]))

        _, vjp = jax.vjp(g, *rv[:n_diff], *pv)
        grads = vjp(tuple(dv))
        for o_ref, gr in zip(o_refs[:n_diff], grads[:n_diff]):
            o_ref[...] = gr.astype(o_ref.dtype)
        first = pl.program_id(0) == 0
        for o_ref, gr in zip(o_refs[n_diff:], grads[n_diff:]):
            @pl.when(first)
            def _(o_ref=o_ref):
                o_ref[...] = jnp.zeros_like(o_ref)

            o_ref[...] += gr

    row_spec = lambda w, cb=0: pl.BlockSpec((tb, w), lambda i: (i, cb))
    par_spec = lambda w: pl.BlockSpec((1, w), lambda i: (0, 0))
    out_shape = ([jax.ShapeDtypeStruct((T, w), F32) for _, w, _ in rows[:n_diff]]
                 + [jax.ShapeDtypeStruct((1, p.shape[1]), F32) for p in params])
    out_specs = [row_spec(w) for _, w, _ in rows[:n_diff]] + [par_spec(p.shape[1]) for p in params]
    in_specs = ([row_spec(w, cb) for _, w, cb in rows] + [par_spec(p.shape[1]) for p in params]
                + [row_spec(w, cb) for _, w, cb in douts])
    args = [r[0] for r in rows] + list(params) + [d[0] for d in douts]
    aliases = None
    if into is not None:
        aliases = {len(args): 0}
        in_specs.append(pl.BlockSpec(memory_space=pl.ANY))
        args.append(into[0])
        out_shape[0] = jax.ShapeDtypeStruct(into[0].shape, into[0].dtype)
        out_specs[0] = row_spec(rows[0][1], into[1])
    return _call(body, name, out_shape, grid=(T // tb,), in_specs=in_specs, out_specs=out_specs,
                 dims=("arbitrary",), carry=carry, aliases=aliases)(*args)


def _sig(x):
    return 1.0 / (1.0 + jnp.exp(-x))


def _rms(x, g):
    return x * lax.rsqrt(jnp.mean(x * x, axis=-1, keepdims=True) + RMS_EPS) * g


def _f_rms(x, g):
    return (_rms(x, g),)


def _f_ln_silu(x, g, b):
    mu = jnp.mean(x, axis=-1, keepdims=True)
    xc = x - mu
    var = jnp.mean(xc * xc, axis=-1, keepdims=True)
    y = xc * lax.rsqrt(var + LN_EPS) * g + b
    return (y * _sig(y),)


def _rope128(x, cos_p, sin_p):
    return x * cos_p + pltpu.roll(x, 64, 1) * sin_p


def _rope128_t(d, cos_p, sin_p):
    return d * cos_p + pltpu.roll(d * sin_p, 64, 1)


def _f_rope(xq, xk, cos_p, sin_p):
    heads = [_rope128(xq[:, h * 128:(h + 1) * 128], cos_p, sin_p) for h in range(MLA_HEADS)]
    return (jnp.concatenate(heads, axis=1), _rope128(xk, cos_p, sin_p))


def _f_rope_t(dq, dk_heads, cos_p, sin_p):
    heads = [_rope128_t(dq[:, h * 128:(h + 1) * 128], cos_p, sin_p) for h in range(MLA_HEADS)]
    dk = dk_heads[:, 0:128]
    for h in range(1, MLA_HEADS):
        dk = dk + dk_heads[:, h * 128:(h + 1) * 128]
    return (jnp.concatenate(heads, axis=1), _rope128_t(dk, cos_p, sin_p))


GATE_LANES = 512


def _gate_out(ycat, proj, z_col, w_out, res, name, tb=1024, carry=None):
    T, width = ycat.shape
    D = w_out.shape[1]
    zb = z_col // GATE_LANES
    nk = width // GATE_LANES

    def body(y_ref, z_ref, w_ref, r_ref, o_ref, yt_ref, acc):
        k = pl.program_id(1)
        z = z_ref[...]
        y = y_ref[...] * (z * _sig(z))
        yt_ref[...] = y.T.astype(yt_ref.dtype)
        part = jnp.dot(y.astype(BF16), w_ref[...], preferred_element_type=F32)

        @pl.when(k == 0)
        def _():
            acc[...] = part

        @pl.when(k > 0)
        def _():
            acc[...] += part

        @pl.when(k == nk - 1)
        def _():
            o_ref[...] = acc[...] + r_ref[...]

    row = pl.BlockSpec((tb, D), lambda i, k: (i, 0))
    return _call(body, name, [jax.ShapeDtypeStruct((T, D), F32), jax.ShapeDtypeStruct((width, T), BF16)],
                 grid=(T // tb, nk),
                 in_specs=[pl.BlockSpec((tb, GATE_LANES), lambda i, k: (i, k)),
                           pl.BlockSpec((tb, GATE_LANES), lambda i, k: (i, zb + k)),
                           pl.BlockSpec((GATE_LANES, D), lambda i, k: (k, 0)), row],
                 out_specs=[row, pl.BlockSpec((GATE_LANES, tb), lambda i, k: (k, i))],
                 scratch=[pltpu.VMEM((tb, D), F32)], dims=("parallel", "arbitrary"), carry=carry)(ycat, proj, w_out, res)


def _out_dx_gate_bwd(dh, w_out, ycat, proj, z_col, name, tb=1024, carry=None):
    T, width = ycat.shape
    D = dh.shape[1]
    zb = z_col // GATE_LANES

    def body(dh_ref, w_ref, y_ref, z_ref, dycat_ref, dz_ref):
        d = lax.dot_general(dh_ref[...].astype(BF16), w_ref[...], _DOT_DIMS["nt"], preferred_element_type=F32)
        z = z_ref[...]
        s = _sig(z)
        dycat_ref[...] = d * (z * s)
        dz_ref[...] = (d * y_ref[...] * (s * (1.0 + z * (1.0 - s)))).astype(dz_ref.dtype)

    blk = pl.BlockSpec((tb, GATE_LANES), lambda i, c: (i, c))
    zblk = pl.BlockSpec((tb, GATE_LANES), lambda i, c: (i, zb + c))
    return _call(body, name, [jax.ShapeDtypeStruct((T, width), F32), jax.ShapeDtypeStruct(proj.shape, BF16)],
                 grid=(T // tb, width // GATE_LANES),
                 in_specs=[pl.BlockSpec((tb, D), lambda i, c: (i, 0)), pl.BlockSpec((GATE_LANES, D), lambda i, c: (c, 0)),
                           blk, zblk],
                 out_specs=[blk, zblk], dims=("parallel", "parallel"), carry=carry)(dh, w_out, ycat, proj)


def _in_dx_norm_bwd(d_proj, w_in, h, g, add, name, tm=512, carry=None):
    T, K = d_proj.shape
    D = w_in.shape[0]
    tk = _pick(K, 1536, 128)
    nk = K // tk

    def body(a_ref, b_ref, h_ref, g_ref, add_ref, dx_ref, dg_ref, acc):
        i, k = pl.program_id(0), pl.program_id(1)
        part = lax.dot_general(a_ref[...], b_ref[...], _DOT_DIMS["nt"], preferred_element_type=F32)

        @pl.when(jnp.logical_and(i == 0, k == 0))
        def _():
            dg_ref[...] = jnp.zeros_like(dg_ref)

        @pl.when(k == 0)
        def _():
            acc[...] = part

        @pl.when(k > 0)
        def _():
            acc[...] += part

        @pl.when(k == nk - 1)
        def _():
            _, vjp = jax.vjp(_rms, h_ref[...], g_ref[...])
            dh, dg = vjp(acc[...])
            dx_ref[...] = dh + add_ref[...]
            dg_ref[...] += dg

    row = pl.BlockSpec((tm, D), lambda i, k: (i, 0))
    par = pl.BlockSpec((1, D), lambda i, k: (0, 0))
    return _call(body, name, [jax.ShapeDtypeStruct((T, D), F32), jax.ShapeDtypeStruct((1, D), F32)], grid=(T // tm, nk),
                 in_specs=[pl.BlockSpec((tm, tk), lambda i, k: (i, k)), pl.BlockSpec((D, tk), lambda i, k: (0, k)),
                           row, par, row],
                 out_specs=[row, par], scratch=[pltpu.VMEM((tm, D), F32)], dims=("arbitrary", "arbitrary"),
                 carry=carry)(d_proj, w_in, h, g, add)


def _glu_bwd(proj, d_glu, d_proj, name, tb=256):
    T, w = d_glu.shape

    def body(a_ref, g_ref, d_ref, _, o_ref):
        s, d = _sig(g_ref[...]), d_ref[...]
        o_ref[:, 0:w] = (d * s).astype(o_ref.dtype)
        o_ref[:, w:2 * w] = (d * a_ref[...] * (s * (1.0 - s))).astype(o_ref.dtype)

    return _call(body, name, jax.ShapeDtypeStruct(d_proj.shape, d_proj.dtype), grid=(T // tb,),
                 in_specs=[pl.BlockSpec((tb, w), lambda i: (i, 0)), pl.BlockSpec((tb, w), lambda i: (i, 1)),
                           pl.BlockSpec((tb, w), lambda i: (i, 0)), pl.BlockSpec(memory_space=pl.ANY)],
                 out_specs=pl.BlockSpec((tb, 2 * w), lambda i: (i, 0)), dims=("parallel",),
                 aliases={3: 0})(proj, proj, d_glu, d_proj)


def _final_loss(h, tgt, g, name, tb=512):
    T, D = h.shape

    def body(h_ref, t_ref, g_ref, dh_ref, dg_ref, loss_ref):
        tv = t_ref[...]

        def rowloss(hh, gg):
            e = _rms(hh, gg) - tv
            return 0.5 * jnp.mean(e * e, axis=-1, keepdims=True)

        lr, vjp = jax.vjp(rowloss, h_ref[...], g_ref[...])
        dh, dg = vjp(jnp.ones_like(lr))
        dh_ref[...] = dh

        @pl.when(pl.program_id(0) == 0)
        def _():
            dg_ref[...] = jnp.zeros_like(dg_ref)
            loss_ref[...] = jnp.zeros_like(loss_ref)

        dg_ref[...] += dg
        loss_ref[...] += jnp.broadcast_to(jnp.sum(lr, axis=0, keepdims=True), loss_ref.shape)

    row = pl.BlockSpec((tb, D), lambda i: (i, 0))
    par = pl.BlockSpec((1, D), lambda i: (0, 0))
    return _call(body, name,
                 [jax.ShapeDtypeStruct((T, D), F32), jax.ShapeDtypeStruct((1, D), F32), jax.ShapeDtypeStruct((1, 128), F32)],
                 grid=(T // tb,), in_specs=[row, row, par],
                 out_specs=[row, par, pl.BlockSpec((1, 128), lambda i: (0, 0))], dims=("arbitrary",))(h, tgt, g)


CONV_ROWS = 128
CONV_LANES = 256


def _sublane_phases(pad, n):
    for r in range(1, 8):
        for c0 in range(0, n - 8, 256):
            rows = min(256, n - 8 - c0)
            pad[r, c0:c0 + rows, :] = pad[0, c0 + r:c0 + r + rows, :]


def _dwconv_fwd(proj, C, w, b, B, S, name, carry=None):
    cb = CONV_LANES
    off = CONV_PAD - (CONV_KERNEL - 1)

    def body(a_ref, g_ref, w_ref, b_ref, o_ref, pad):
        pad[0, 0:CONV_PAD, :] = jnp.zeros((CONV_PAD, cb), F32)
        for c0 in range(0, S, 256):
            pad[0, CONV_PAD + c0:CONV_PAD + c0 + 256, :] = a_ref[c0:c0 + 256, :] * _sig(g_ref[c0:c0 + 256, :])
        _sublane_phases(pad, S + CONV_PAD)
        for t0 in range(0, S, CONV_ROWS):
            acc = jnp.broadcast_to(b_ref[...], (CONV_ROWS, cb))
            for k in range(CONV_KERNEL):
                r, base = (off + k) % 8, t0 + (off + k) // 8 * 8
                acc = acc + w_ref[k:k + 1, :] * pad[r, base:base + CONV_ROWS, :]
            o_ref[t0:t0 + CONV_ROWS, :] = acc

    return _call(body, name, jax.ShapeDtypeStruct((B, S, C), F32), grid=(B, C // cb),
                 in_specs=[pl.BlockSpec((S, cb), lambda i, j: (i, j)), pl.BlockSpec((S, cb), lambda i, j: (i, C // cb + j)),
                           pl.BlockSpec((CONV_KERNEL, cb), lambda i, j: (0, j)),
                           pl.BlockSpec((1, cb), lambda i, j: (0, j))],
                 out_specs=pl.BlockSpec((None, S, cb), lambda i, j: (i, 0, j)),
                 scratch=[pltpu.VMEM((8, S + CONV_PAD, cb), F32)], dims=("parallel", "parallel"),
                 carry=carry)(proj, proj, w, b)


def _dwconv_bwd(proj, w, dy, name, carry=None):
    B, S, C = dy.shape
    cb = CONV_LANES
    groups = CONV_ROWS // 8

    def body(a_ref, g_ref, w_ref, dy_ref, dx_ref, dw_ref, db_ref, dypad, wacc):
        dypad[0, 0:S, :] = dy_ref[...]
        dypad[0, S:, :] = jnp.zeros((CONV_PAD, cb), F32)
        _sublane_phases(dypad, S + CONV_PAD)
        wacc[...] = jnp.zeros_like(wacc)
        for t0 in range(0, S, CONV_ROWS):
            xc = a_ref[t0:t0 + CONV_ROWS, :] * _sig(g_ref[t0:t0 + CONV_ROWS, :])
            acc = jnp.zeros((CONV_ROWS, cb), F32)
            for k in range(CONV_KERNEL):
                o = (CONV_KERNEL - 1) - k
                dys = dypad[o % 8, t0 + o // 8 * 8:t0 + o // 8 * 8 + CONV_ROWS, :]
                acc = acc + w_ref[k:k + 1, :] * dys
                wacc[k] += jnp.sum((dys * xc).reshape(groups, 8, cb), axis=0)
            wacc[CONV_KERNEL] += jnp.sum(dy_ref[t0:t0 + CONV_ROWS, :].reshape(groups, 8, cb), axis=0)
            dx_ref[t0:t0 + CONV_ROWS, :] = acc

        @pl.when(pl.program_id(1) == 0)
        def _():
            dw_ref[...] = jnp.zeros_like(dw_ref)
            db_ref[...] = jnp.zeros_like(db_ref)

        for k in range(CONV_KERNEL):
            dw_ref[k:k + 1, :] += jnp.sum(wacc[k], axis=0, keepdims=True)
        db_ref[...] += jnp.sum(wacc[CONV_KERNEL], axis=0, keepdims=True)

    blk = pl.BlockSpec((None, S, cb), lambda j, i: (i, 0, j))
    return _call(body, name,
                 [jax.ShapeDtypeStruct((B, S, C), F32), jax.ShapeDtypeStruct((CONV_KERNEL, C), F32),
                  jax.ShapeDtypeStruct((1, C), F32)],
                 grid=(C // cb, B),
                 in_specs=[pl.BlockSpec((S, cb), lambda j, i: (i, j)), pl.BlockSpec((S, cb), lambda j, i: (i, C // cb + j)),
                           pl.BlockSpec((CONV_KERNEL, cb), lambda j, i: (0, j)), blk],
                 out_specs=[blk, pl.BlockSpec((CONV_KERNEL, cb), lambda j, i: (0, j)),
                            pl.BlockSpec((1, cb), lambda j, i: (0, j))],
                 scratch=[pltpu.VMEM((8, S + CONV_PAD, cb), F32), pltpu.VMEM((CONV_KERNEL + 1, 8, cb), F32)],
                 dims=("parallel", "arbitrary"), carry=carry)(proj, proj, w, dy)


ATTN_TILE = {"fwd": 1024, "bwd": 1024, "cross fwd": 512}
ATTN_SUB = {"fwd": 256, "bwd": 256}


def _attn_shapes(Sq, Sk, causal, pass_):
    tq = min(Sq, ATTN_TILE[pass_ if causal or pass_ == "bwd" else "cross fwd"])
    tk = tq if causal else min(Sk, ATTN_TILE[pass_])
    return tq, tk, min(ATTN_SUB[pass_], tq)


def _causal_bias(n):
    r = lax.broadcasted_iota(jnp.int32, (n, n), 0)
    c = lax.broadcasted_iota(jnp.int32, (n, n), 1)
    return jnp.where(c <= r, 0.0, NEG).astype(F32)


def _mask_diagonal(s, bias):
    n, nc = s.shape
    if nc == n:
        return s + bias
    return jnp.concatenate([s[:, :nc - n], s[:, nc - n:] + bias], axis=1)


def _attn_fwd(q, q_c0, qr, k, k_c0, kr, v, v_c0, B, Sq, Sk, H, causal, scale, name, into=None, o_c0=0, o_width=None,
              kv_stride=1):
    tq, tk, sub = _attn_shapes(Sq, Sk, causal, "fwd")
    nq, nk, nsub = Sq // tq, Sk // tk, tq // sub
    rope = qr is not None

    def body(*refs):
        refs = list(refs)
        qn_ref = refs.pop(0)
        qr_ref = refs.pop(0) if rope else None
        kn_ref = refs.pop(0)
        kr_ref = refs.pop(0) if rope else None
        v_ref = refs.pop(0)
        if into is not None:
            refs.pop(0)
        o_ref, lse_ref, m_s, l_s, acc = refs
        qi = pl.program_id(2)
        m_s[...] = jnp.full_like(m_s, NEG)
        l_s[...] = jnp.zeros_like(l_s)
        acc[...] = jnp.zeros_like(acc)
        bias = _causal_bias(sub) if causal else None
        qs = []
        for r in range(nsub):
            qn = qn_ref[r * sub:(r + 1) * sub, :].astype(BF16)
            qs.append(jnp.concatenate([qn, qr_ref[r * sub:(r + 1) * sub, :]], axis=1) if rope else qn)

        def step(j, masked):
            ks = pl.ds(pl.multiple_of(j * tk, tk), tk)
            kk = jnp.concatenate([kn_ref[ks, :], kr_ref[ks, :]], axis=1) if rope else kn_ref[ks, :]
            vv = v_ref[ks, :]
            for r in range(nsub):
                rows = slice(r * sub, (r + 1) * sub)
                nc = (r + 1) * sub if masked else tk
                s = lax.dot_general(qs[r], kk[:nc], _DOT_DIMS["nt"], preferred_element_type=F32) * scale
                if masked:
                    s = _mask_diagonal(s, bias)
                m_old = m_s[rows, :]
                m_new = jnp.maximum(m_old, jnp.max(s, axis=-1, keepdims=True))
                p = jnp.exp(s - m_new)
                alpha = jnp.exp(m_old - m_new)
                l_s[rows, :] = alpha * l_s[rows, :] + jnp.sum(p, axis=-1, keepdims=True)
                acc[rows, :] = alpha * acc[rows, :] + jnp.dot(p.astype(BF16), vv[:nc], preferred_element_type=F32)
                m_s[rows, :] = m_new

        def unmasked(j, carry):
            step(j, False)
            return carry

        if causal:
            lax.fori_loop(0, qi, unmasked, 0)
            step(qi, True)
        else:
            lax.fori_loop(0, nk, unmasked, 0)
        o_ref[...] = (acc[...] / l_s[...]).astype(o_ref.dtype)
        lse_ref[...] = m_s[...] + jnp.log(l_s[...])

    qspec = lambda c0: pl.BlockSpec((tq, 128), lambda b, h, i: (b * nq + i, c0 + h))
    kspec = lambda c0: pl.BlockSpec((Sk, 128), lambda b, h, i: (b, c0 + kv_stride * h))
    in_specs, args = [qspec(q_c0)], [q]
    if rope:
        in_specs.append(qspec(0)); args.append(qr)
    in_specs.append(kspec(k_c0)); args.append(k)
    if rope:
        in_specs.append(pl.BlockSpec((Sk, 128), lambda b, h, i: (b, 0))); args.append(kr)
    in_specs.append(kspec(v_c0)); args.append(v)
    aliases = {}
    if into is not None:
        aliases = {len(args): 0}
        in_specs.append(pl.BlockSpec(memory_space=pl.ANY)); args.append(into)
        o_shape = jax.ShapeDtypeStruct(into.shape, into.dtype)
    else:
        o_shape = jax.ShapeDtypeStruct((B * Sq, o_width), F32)
    return _call(body, name, [o_shape, jax.ShapeDtypeStruct((B * H, Sq, 1), F32)], grid=(B, H, nq), in_specs=in_specs,
                 out_specs=[qspec(o_c0), pl.BlockSpec((None, tq, 1), lambda b, h, i: (b * H + h, i, 0))],
                 scratch=[pltpu.VMEM((tq, 1), F32), pltpu.VMEM((tq, 1), F32), pltpu.VMEM((tq, 128), F32)],
                 dims=("parallel", "parallel", "arbitrary"), aliases=aliases)(*args)


def _attn_bwd(q, q_c0, qr, k, k_c0, kr, v, v_c0, o, do, o_c0, lse, B, Sq, Sk, H, causal, scale, name, dq_into=None,
              kv_stride=1):
    tq, tk, sub = _attn_shapes(Sq, Sk, causal, "bwd")
    nq, nk, nsub = Sq // tq, Sk // tk, tq // sub
    rope = qr is not None
    dk_w = 256 if rope else 128

    def body(*refs):
        refs = list(refs)
        qn_ref = refs.pop(0)
        qr_ref = refs.pop(0) if rope else None
        kn_ref = refs.pop(0)
        kr_ref = refs.pop(0) if rope else None
        v_ref, o_ref, do_ref, lse_ref = refs[:4]
        refs = refs[4 + (0 if dq_into is None else 1):]
        dqn_ref = refs.pop(0)
        dqr_ref = refs.pop(0) if rope else None
        dkn_ref = refs.pop(0)
        dkr_ref = refs.pop(0) if rope else None
        dv_ref = None if rope else refs.pop(0)
        q_s, do_s, dl_s, dq_acc, dk_acc, dv_acc = refs
        kj = pl.program_id(2)

        @pl.when(kj == 0)
        def _():
            qn = qn_ref[...].astype(BF16)
            q_s[...] = jnp.concatenate([qn, qr_ref[...]], axis=1) if rope else qn
            dof = do_ref[...]
            do_s[...] = dof.astype(BF16)
            dl_s[...] = jnp.sum(dof * o_ref[...], axis=-1, keepdims=True)
            dq_acc[...] = jnp.zeros_like(dq_acc)

        kk = jnp.concatenate([kn_ref[...], kr_ref[...]], axis=1) if rope else kn_ref[...]
        vv = v_ref[...]
        bias = _causal_bias(sub) if causal else None
        dk_acc[...] = jnp.zeros_like(dk_acc)
        dv_acc[...] = jnp.zeros_like(dv_acc)

        def step(i, masked):
            for r in range(nsub):
                rows = pl.ds(pl.multiple_of(i * tq + r * sub, sub), sub)
                qq, dob = q_s[rows, :], do_s[rows, :]
                nc = (r + 1) * sub if masked else tk
                kc, vc = kk[:nc], vv[:nc]
                s = lax.dot_general(qq, kc, _DOT_DIMS["nt"], preferred_element_type=F32) * scale
                if masked:
                    s = _mask_diagonal(s, bias)
                p = jnp.exp(s - lse_ref[rows, :])
                dp = lax.dot_general(dob, vc, _DOT_DIMS["nt"], preferred_element_type=F32)
                ds = (p * (dp - dl_s[rows, :]) * scale).astype(BF16)
                dv_acc[0:nc, :] += lax.dot_general(p.astype(BF16), dob, _DOT_DIMS["tn"], preferred_element_type=F32)
                dk_acc[0:nc, :] += lax.dot_general(ds, qq, _DOT_DIMS["tn"], preferred_element_type=F32)
                dq_acc[rows, :] += jnp.dot(ds, kc, preferred_element_type=F32)

        def unmasked(i, carry):
            step(i, False)
            return carry

        if causal:
            step(kj, True)
            lax.fori_loop(kj + 1, nq, unmasked, 0)
        else:
            lax.fori_loop(0, nq, unmasked, 0)
        if rope:
            dkn_ref[...] = jnp.concatenate([dk_acc[:, 0:128], dv_acc[...]], axis=1).astype(dkn_ref.dtype)
            dkr_ref[...] = dk_acc[:, 128:256]
        else:
            dkn_ref[...] = dk_acc[...]
            dv_ref[...] = dv_acc[...]

        @pl.when(kj == nk - 1)
        def _():
            dqn_ref[...] = dq_acc[:, 0:128].astype(dqn_ref.dtype)
            if rope:
                dqr_ref[...] = dq_acc[:, 128:256]

    qspec = lambda c0: pl.BlockSpec((Sq, 128), lambda b, h, j: (b, c0 + h))
    kspec = lambda c0: pl.BlockSpec((tk, 128), lambda b, h, j: (b * nk + j, c0 + kv_stride * h))
    in_specs, args = [qspec(q_c0)], [q]
    if rope:
        in_specs.append(qspec(0)); args.append(qr)
    in_specs.append(kspec(k_c0)); args.append(k)
    if rope:
        in_specs.append(pl.BlockSpec((tk, 128), lambda b, h, j: (b * nk + j, 0))); args.append(kr)
    in_specs += [kspec(v_c0), qspec(o_c0), qspec(o_c0), pl.BlockSpec((None, Sq, 1), lambda b, h, j: (b * H + h, 0, 0))]
    args += [v, o, do, lse]
    h_rows_q = jax.ShapeDtypeStruct((B * Sq, H * 128), F32)
    h_rows_k = jax.ShapeDtypeStruct((B * Sk, H * 128), F32)
    out_shape, out_specs, aliases = [h_rows_q], [qspec(0)], None
    if rope:
        out_shape = [jax.ShapeDtypeStruct((B * Sq, 2 * H * 128), BF16)]
    if dq_into is not None:
        aliases = {len(args): 0}
        in_specs.append(pl.BlockSpec(memory_space=pl.ANY)); args.append(dq_into[0])
        out_shape, out_specs = [jax.ShapeDtypeStruct(dq_into[0].shape, dq_into[0].dtype)], [qspec(dq_into[1])]
    if rope:
        out_shape.append(h_rows_q); out_specs.append(qspec(0))
    hspec = lambda w: pl.BlockSpec((tk, w), lambda b, h, j: (b * nk + j, h))
    if rope:
        out_shape += [jax.ShapeDtypeStruct((B * Sk, H * 256), BF16), h_rows_k]
        out_specs += [hspec(256), hspec(128)]
    else:
        out_shape += [h_rows_k, h_rows_k]
        out_specs += [hspec(128), hspec(128)]
    return _call(body, name, out_shape, grid=(B, H, nk), in_specs=in_specs, out_specs=out_specs,
                 scratch=[pltpu.VMEM((Sq, dk_w), BF16), pltpu.VMEM((Sq, 128), BF16), pltpu.VMEM((Sq, 1), F32),
                          pltpu.VMEM((Sq, dk_w), F32), pltpu.VMEM((tk, dk_w), F32), pltpu.VMEM((tk, 128), F32)],
                 dims=("parallel", "parallel", "arbitrary"), aliases=aliases)(*args)


def _mem_attention_fwd(proj, q_col, ycat, mem2, mem_g, w_mem, B, S, tag):
    M = mem2.shape[0] // B
    (memn,) = _rowwise(_f_rms, [mem2], [mem_g], [(mem2.shape[1], BF16)], tag + "_memnorm")
    kvm = _mm(memn, w_mem, "nn", BF16, tag + "_memkv")
    o_c0 = ycat.shape[1] // 128 - MEM_HEADS
    ycat, lse = _attn_fwd(proj, q_col // 128, None, kvm, 0, None, kvm, MEM_HEADS, B, S, M, MEM_HEADS, False,
                          MEM_HEAD_DIM ** -0.5, tag + "_memattn", into=ycat, o_c0=o_c0)
    return ycat, (memn, kvm, lse)


def _mem_attention_bwd(proj, q_col, ycat, d_ycat, d_proj, saved, mem2, mem_g, w_mem, B, S, tag):
    memn, kvm, lse = saved
    M = mem2.shape[0] // B
    o_c0 = ycat.shape[1] // 128 - MEM_HEADS
    d_q, d_k, d_v = _attn_bwd(proj, q_col // 128, None, kvm, 0, None, kvm, MEM_HEADS, ycat, d_ycat, o_c0, lse, B, S, M,
                              MEM_HEADS, False, MEM_HEAD_DIM ** -0.5, tag + "_memattn_bwd", dq_into=(d_proj, q_col // 128))
    d_kvm = jnp.concatenate([d_k, d_v], axis=1).astype(BF16)
    d_w_mem = _mm(memn, d_kvm, "tn", F32, tag + "_memkv_dw")
    d_memn = _mm(d_kvm, w_mem, "nt", F32, tag + "_memkv_dx")
    _, d_mem_g = _rowwise_bwd(_f_rms, [mem2], [mem_g], [d_memn], 1, tag + "_memnorm_bwd")
    return d_q, d_w_mem, d_mem_g


def _rope_tables(positions):
    inv_freq = 1.0 / (ROPE_THETA ** (jnp.arange(0, MLA_ROPE, 2, dtype=F32) / MLA_ROPE))
    ang = positions.astype(F32).reshape(-1, 1) * inv_freq
    cos, sin, zero = jnp.cos(ang), jnp.sin(ang), jnp.zeros_like(ang)
    return jnp.concatenate([cos, zero, cos, zero], axis=1), jnp.concatenate([-sin, zero, sin, zero], axis=1)


def _forward_backward(x, mem, positions, target, W):
    B, S, D = x.shape
    T = B * S
    conv_w = W["conv_dw"].shape[1]
    mix_w = 2 * D
    h0 = x.reshape(T, D)
    mem2 = mem.reshape(-1, D)
    tgt = target.reshape(T, D)
    row = lambda v: v.reshape(1, -1)
    n_nope = MLA_HEADS * MLA_NOPE

    g0 = row(W["norm_g"][0])
    (u0,) = _rowwise(_f_rms, [h0], [g0], [(D, BF16)], "l0_norm", carry=W.carry("l0_norm"))
    proj0 = _mm(u0, W["conv_w_in"], "nn", F32, "l0_in", carry=W.carry("l0_in"))
    qm0_col, z0_col = 2 * conv_w, 2 * conv_w + MEM_WIDTH
    dw, dwb = W["conv_dw"], row(W["conv_dw_b"][0])
    cv = _dwconv_fwd(proj0, conv_w, dw, dwb, B, S, "l0_dwconv", carry=W.carry("l0_dwconv")).reshape(T, conv_w)
    ln_g, ln_b = row(W["conv_ln_g"][0]), row(W["conv_ln_b"][0])
    (ycat0,) = _rowwise(_f_ln_silu, [cv], [ln_g, ln_b], [(conv_w, F32, mix_w)], "l0_ln", carry=W.carry("l0_ln"))
    mg0 = row(W["mem_norm_g"][0])
    ycat0, mem_saved0 = _mem_attention_fwd(proj0, qm0_col, ycat0, mem2, mg0, W["w_mem_kv"][0], B, S, "l0")
    h1, y0_t = _gate_out(ycat0, proj0, z0_col, W["w_out"][0], h0, "l0_out", carry=W.carry("l0_out"))

    g1 = row(W["norm_g"][1])
    (u1,) = _rowwise(_f_rms, [h1], [g1], [(D, BF16)], "l1_norm")
    proj1 = _mm(u1, W["mla_w_in"], "nn", F32, "l1_in")
    z1_col = Q_RANK
    qm1_col = z1_col + mix_w
    ckv_col = qm1_col + MEM_WIDTH
    kr_col = ckv_col + KV_RANK
    cq, ckv = (proj1, Q_RANK, 0), (proj1, KV_RANK, ckv_col // KV_RANK)
    qg, kvg = row(W["mla_q_norm_g"]), row(W["mla_kv_norm_g"])
    (cqn,) = _rowwise(_f_rms, [cq], [qg], [(Q_RANK, BF16)], "l1_qnorm")
    (ckvn,) = _rowwise(_f_rms, [ckv], [kvg], [(KV_RANK, BF16)], "l1_kvnorm")
    qf = _mm(cqn, W["mla_w_uq"], "nn", F32, "l1_uq")
    kvf = _mm(ckvn, W["mla_w_ukv"], "nn", BF16, "l1_ukv")
    cos_p, sin_p = _rope_tables(positions)
    qr, kr = _rowwise(_f_rope, [(qf, n_nope, 1), (proj1, 128, kr_col // 128), cos_p, sin_p], [],
                      [(n_nope, BF16), (128, BF16)], "l1_rope")
    scale1 = MLA_QK ** -0.5
    ycat1, lse1 = _attn_fwd(qf, 0, qr, kvf, 0, kr, kvf, 1, B, S, S, MLA_HEADS, True, scale1, "l1_attn",
                            o_width=mix_w, kv_stride=2)
    mg1 = row(W["mem_norm_g"][1])
    ycat1, mem_saved1 = _mem_attention_fwd(proj1, qm1_col, ycat1, mem2, mg1, W["w_mem_kv"][1], B, S, "l1")
    h2, y1_t = _gate_out(ycat1, proj1, z1_col, W["w_out"][1], h1, "l1_out")

    gf = row(W["final_norm_g"])
    dh2, d_gf, loss128 = _final_loss(h2, tgt, gf, "final_loss")
    G = {"final_norm_g": d_gf.reshape(-1)}
    L1 = {}

    d_wout1 = _mm(y1_t, dh2, "nn", F32, "l1_out_dw")
    d_ycat1, d_proj1 = _out_dx_gate_bwd(dh2, W["w_out"][1], ycat1, proj1, z1_col, "l1_out_dx")
    d_proj1, d_wmem1, d_mg1 = _mem_attention_bwd(proj1, qm1_col, ycat1, d_ycat1, d_proj1, mem_saved1, mem2, mg1,
                                                 W["w_mem_kv"][1], B, S, "l1")
    d_qf, d_qr, d_kvf, d_kr_heads = _attn_bwd(qf, 0, qr, kvf, 0, kr, kvf, 1, ycat1, d_ycat1, 0, lse1, B, S, S,
                                              MLA_HEADS, True, scale1, "l1_attn_bwd", kv_stride=2)
    d_qf, d_proj1 = _rowwise(_f_rope_t, [d_qr, d_kr_heads, cos_p, sin_p], [], [(n_nope, F32), (128, F32)], "l1_rope_bwd",
                             into=[(0, d_qf, 1), (1, d_proj1, kr_col // 128)])
    d_cqn = _mm(d_qf, W["mla_w_uq"], "nt", F32, "l1_uq_dx")
    L1[("mla_w_uq", None)] = _mm(cqn, d_qf, "tn", F32, "l1_uq_dw")
    d_ckvn = _mm(d_kvf, W["mla_w_ukv"], "nt", F32, "l1_ukv_dx")
    L1[("mla_w_ukv", None)] = _mm(ckvn, d_kvf, "tn", F32, "l1_ukv_dw")
    d_proj1, d_qg = _rowwise_bwd(_f_rms, [cq], [qg], [d_cqn], 1, "l1_qnorm_bwd", into=(d_proj1, cq[2]))
    d_proj1, d_kvg = _rowwise_bwd(_f_rms, [ckv], [kvg], [d_ckvn], 1, "l1_kvnorm_bwd", into=(d_proj1, ckv[2]))
    L1[("w_mem_kv", 1)] = d_wmem1
    L1[("mla_w_in", None)] = _mm(u1, d_proj1, "tn", F32, "l1_in_dw")
    L1[("w_out", 1)] = d_wout1
    W.ready("l1", L1)
    dh1, d_g1 = _in_dx_norm_bwd(d_proj1, W["mla_w_in"], h1, g1, dh2, "l1_in_dx", carry=W.carry("l1_in_dx"))

    d_wout0 = _mm(y0_t, dh1, "nn", F32, "l0_out_dw")
    d_ycat0, d_proj0 = _out_dx_gate_bwd(dh1, W["w_out"][0], ycat0, proj0, z0_col, "l0_out_dx", carry=W.carry("l0_out_dx"))
    d_proj0, d_wmem0, d_mg0 = _mem_attention_bwd(proj0, qm0_col, ycat0, d_ycat0, d_proj0, mem_saved0, mem2, mg0,
                                                 W["w_mem_kv"][0], B, S, "l0")
    W.ready("l0a", {("w_mem_kv", 0): d_wmem0, ("w_out", 0): d_wout0})
    d_cv, d_ln_g, d_ln_b = _rowwise_bwd(_f_ln_silu, [cv], [ln_g, ln_b], [(d_ycat0, conv_w, 0)], 1, "l0_ln_bwd",
                                        carry=W.carry("l0_ln_bwd"))
    d_glu, d_dw, d_dwb = _dwconv_bwd(proj0, dw, d_cv.reshape(B, S, conv_w), "l0_dwconv_bwd",
                                     carry=W.carry("l0_dwconv_bwd"))
    d_proj0 = _glu_bwd(proj0, d_glu.reshape(T, conv_w), d_proj0, "l0_glu_bwd")
    d_conv_w_in = _mm(u0, d_proj0, "tn", F32, "l0_in_dw", carry=W.carry("l0_in_dw"))
    W.ready("l0b", {("conv_w_in", None): d_conv_w_in, ("conv_dw", None): d_dw,
                    ("mla_q_norm_g", None): d_qg.reshape(-1), ("mla_kv_norm_g", None): d_kvg.reshape(-1)})
    dx, d_g0 = _in_dx_norm_bwd(d_proj0, W["conv_w_in"], h0, g0, dh1, "l0_in_dx", carry=W.carry("l0_in_dx"))
    dx = dx.reshape(B, S, D)

    G["norm_g"] = jnp.concatenate([d_g0, d_g1], axis=0)
    G["mem_norm_g"] = jnp.concatenate([d_mg0, d_mg1], axis=0)
    G["conv_dw_b"] = d_dwb
    G["conv_ln_g"], G["conv_ln_b"] = d_ln_g, d_ln_b
    return loss128[0, 0], dx, G


def _mla_in_perm(w):
    c1, c2 = Q_RANK, Q_RANK + KV_RANK
    c3 = c2 + MLA_ROPE
    c4 = c3 + MEM_WIDTH
    zero = jnp.zeros((w.shape[0], HALF_ROPE), w.dtype)
    return jnp.concatenate([w[:, :c1], w[:, c4:], w[:, c3:c4], w[:, c1:c2], w[:, c2:c2 + HALF_ROPE], zero,
                            w[:, c2 + HALF_ROPE:c3], zero], axis=1)


def _mla_in_unperm(g):
    z_w = g.shape[1] - (Q_RANK + MEM_WIDTH + KV_RANK + 128)
    z0, q0 = Q_RANK, Q_RANK + z_w
    k0 = q0 + MEM_WIDTH
    r = k0 + KV_RANK
    return jnp.concatenate([g[:, :Q_RANK], g[:, k0:r], g[:, r:r + HALF_ROPE], g[:, r + 64:r + 64 + HALF_ROPE],
                            g[:, q0:k0], g[:, z0:q0]], axis=1)


def _uq_perm(w):
    n = w.shape[0]
    w3 = w.reshape(n, MLA_HEADS, MLA_QK)
    zero = jnp.zeros((n, MLA_HEADS, HALF_ROPE), w.dtype)
    rope = jnp.concatenate([w3[:, :, MLA_NOPE:MLA_NOPE + HALF_ROPE], zero, w3[:, :, MLA_NOPE + HALF_ROPE:], zero], axis=2)
    return jnp.concatenate([w3[:, :, :MLA_NOPE].reshape(n, -1), rope.reshape(n, -1)], axis=1)


def _uq_unperm(g):
    n = g.shape[0]
    n_nope = MLA_HEADS * MLA_NOPE
    rope = g[:, n_nope:].reshape(n, MLA_HEADS, 128)
    return jnp.concatenate([g[:, :n_nope].reshape(n, MLA_HEADS, MLA_NOPE), rope[:, :, :HALF_ROPE],
                            rope[:, :, 64:64 + HALF_ROPE]], axis=2).reshape(n, -1)


_ROW_CUT = ("w_mem_kv", "w_out")
_COL_CUT = ("conv_w_in", "mla_w_in", "mla_w_uq", "mla_w_ukv", "conv_dw")
_BIG = ("w_mem_kv", "w_out", "conv_w_in", "mla_w_in", "mla_w_uq", "mla_w_ukv")
_SMALL_SHARDED = ("conv_dw", "mla_q_norm_g", "mla_kv_norm_g")
_REPLICATED = ("norm_g", "mem_norm_g", "conv_dw_b", "conv_ln_g", "conv_ln_b", "final_norm_g")
_PERM = {"mla_w_in": (_mla_in_perm, _mla_in_unperm), "mla_w_uq": (_uq_perm, _uq_unperm)}


def _join(n, blocks):
    if n in _ROW_CUT:
        _, L, r, c = blocks.shape
        return blocks.transpose(1, 0, 2, 3).reshape(L, N_DEV * r, c)
    if n in _COL_CUT:
        _, _, r, c = blocks.shape
        return blocks.reshape(N_DEV, r, c).transpose(1, 0, 2).reshape(r, N_DEV * c)
    return blocks.reshape(-1)


def _cut(n, full, shard_shape):
    if n in _ROW_CUT:
        L, r, c = shard_shape
        return full.reshape(L, N_DEV, r, c).transpose(1, 0, 2, 3)
    if n in _COL_CUT:
        _, r, c = shard_shape
        return full.reshape(r, N_DEV, c).transpose(1, 0, 2).reshape(N_DEV, 1, r, c)
    return full.reshape(N_DEV, 1, -1)


def _flat_pad(parts, size):
    flat = jnp.concatenate([p.reshape(-1) for p in parts])
    return jnp.concatenate([flat, jnp.zeros((size - flat.shape[0],), flat.dtype)])


SMALL_LANES = 128 * 8


def _as_tiles(flat_parts):
    total = sum(p.size for p in flat_parts)
    size = -(-total // SMALL_LANES) * SMALL_LANES
    return _flat_pad(flat_parts, size).reshape(8, size // 8)


def _split_flat(flat, like):
    out, o = [], 0
    for a in like:
        out.append(flat[o:o + a.size].reshape(a.shape))
        o += a.size
    return out


_HBM = pl.BlockSpec(memory_space=pltpu.HBM)
_VMEM = pl.BlockSpec(memory_space=pltpu.VMEM)


def _position():
    return lax.axis_index("x"), lax.axis_index("y"), lax.axis_index("c")


def _dma_sems(n):
    return [pltpu.SemaphoreType.DMA((n,)), pltpu.SemaphoreType.DMA((n,))]


def _run_stage(stage, name):
    n_in, n_out = len(stage.ins), len(stage.out_shapes)

    def body(*refs):
        ins, outs, sems = refs[:n_in], refs[n_in:n_in + n_out], refs[n_in + n_out:]
        stage.start(ins, outs, sems)
        stage.wait(ins, outs, sems)

    outs = _call(body, name, stage.out_shapes, in_specs=[_HBM] * n_in, out_specs=[_HBM] * n_out, scratch=stage.sems,
                 aliases=stage.aliases)(*stage.ins)
    stage.outs = list(outs)
    return stage.outs


def _gather_chips_stage(shards):
    n = len(shards)

    def copies(x_refs, out_refs, sems):
        send_sems, recv_sems, _ = sems
        x, y, c = _position()
        peers = [(x, y, 1 - c), (1 - x, y, c), (x, 1 - y, c), (1 - x, 1 - y, c)]
        out = []
        for a in range(n):
            for k, (px, py, pc) in enumerate(peers):
                send = pltpu.make_async_remote_copy(src_ref=x_refs[a], dst_ref=out_refs[a].at[4 * x + 2 * y + c],
                                                    send_sem=send_sems.at[4 * a + k], recv_sem=recv_sems.at[4 * a + k],
                                                    device_id=(px, py, pc), device_id_type=MESH)
                recv = pltpu.make_async_remote_copy(src_ref=x_refs[a], dst_ref=out_refs[a].at[4 * px + 2 * py + pc],
                                                    send_sem=send_sems.at[4 * a + k], recv_sem=recv_sems.at[4 * a + k],
                                                    device_id=(px, py, pc), device_id_type=MESH)
                out.append((send, recv))
        return out

    def local(x_refs, out_refs, sems):
        x, y, c = _position()
        return [pltpu.make_async_copy(x_refs[a], out_refs[a].at[4 * x + 2 * y + c], sems[2].at[a]) for a in range(n)]

    def start(x_refs, out_refs, sems):
        for cp in local(x_refs, out_refs, sems):
            cp.start()
        for send, _ in copies(x_refs, out_refs, sems):
            send.start()

    def wait(x_refs, out_refs, sems):
        for send, recv in copies(x_refs, out_refs, sems):
            recv.wait_recv()
            send.wait_send()
        for cp in local(x_refs, out_refs, sems):
            cp.wait()

    return _Stage(shards, [jax.ShapeDtypeStruct((N_DEV,) + a.shape, a.dtype) for a in shards],
                  _dma_sems(4 * n) + [pltpu.SemaphoreType.DMA((n,))], start, wait)


def _gather_sibling_stage(bufs):
    n = len(bufs)

    def copies(out_refs, sems):
        send_sems, recv_sems = sems
        x, y, c = _position()
        out = []
        for a in range(n):
            for j, (px, py) in enumerate([(1 - x, y), (x, 1 - y), (1 - x, 1 - y)]):
                mine, theirs = out_refs[a].at[4 * px + 2 * py + c], out_refs[a].at[4 * px + 2 * py + (1 - c)]
                send = pltpu.make_async_remote_copy(src_ref=mine, dst_ref=mine, send_sem=send_sems.at[3 * a + j],
                                                    recv_sem=recv_sems.at[3 * a + j], device_id=(x, y, 1 - c),
                                                    device_id_type=MESH)
                recv = pltpu.make_async_remote_copy(src_ref=mine, dst_ref=theirs, send_sem=send_sems.at[3 * a + j],
                                                    recv_sem=recv_sems.at[3 * a + j], device_id=(x, y, 1 - c),
                                                    device_id_type=MESH)
                out.append((send, recv))
        return out

    def start(_, out_refs, sems):
        for send, _r in copies(out_refs, sems):
            send.start()

    def wait(_, out_refs, sems):
        for send, recv in copies(out_refs, sems):
            recv.wait_recv()
            send.wait_send()

    return _Stage(bufs, [jax.ShapeDtypeStruct(b.shape, b.dtype) for b in bufs], _dma_sems(3 * n), start, wait,
                  aliases={a: a for a in range(n)})


def _all_gather_small(v, name):
    r, n = v.shape

    def body(x_ref, out_ref, send_sems, recv_sems, local_sem):
        x, y, c = _position()
        me = 4 * x + 2 * y + c
        mine = pltpu.make_async_copy(x_ref, out_ref.at[me], local_sem)
        mine.start()
        flips = [(fx, fy, fc) for fx in (0, 1) for fy in (0, 1) for fc in (0, 1)][1:]
        copies = []
        for k, (fx, fy, fc) in enumerate(flips):
            peer = (x ^ fx, y ^ fy, c ^ fc)
            cp = pltpu.make_async_remote_copy(src_ref=x_ref, dst_ref=out_ref.at[me], send_sem=send_sems.at[k],
                                              recv_sem=recv_sems.at[k], device_id=peer, device_id_type=MESH)
            cp.start()
            copies.append(cp)
        for k, (fx, fy, fc) in enumerate(flips):
            px, py, pc = x ^ fx, y ^ fy, c ^ fc
            src = out_ref.at[4 * px + 2 * py + pc]
            pltpu.make_async_remote_copy(src_ref=x_ref, dst_ref=src, send_sem=send_sems.at[k], recv_sem=recv_sems.at[k],
                                         device_id=(px, py, pc), device_id_type=MESH).wait_recv()
        for cp in copies:
            cp.wait_send()
        mine.wait()

    return _call(body, name, jax.ShapeDtypeStruct((N_DEV, r, n), v.dtype), in_specs=[_VMEM], out_specs=_VMEM,
                 scratch=_dma_sems(7) + [pltpu.SemaphoreType.DMA(())])(v)


def _reduce_sibling_stage(gs):
    n = len(gs)

    def copies(g_refs, out_refs, sems):
        send_sems, recv_sems = sems
        x, y, c = _position()
        return [pltpu.make_async_remote_copy(src_ref=g_refs[a].at[2 * k + (1 - c)], dst_ref=out_refs[a].at[k],
                                             send_sem=send_sems.at[4 * a + k], recv_sem=recv_sems.at[4 * a + k],
                                             device_id=(x, y, 1 - c), device_id_type=MESH)
                for a in range(n) for k in range(4)]

    def start(g_refs, out_refs, sems):
        for cp in copies(g_refs, out_refs, sems):
            cp.start()

    def wait(g_refs, out_refs, sems):
        for cp in copies(g_refs, out_refs, sems):
            cp.wait()

    return _Stage(gs, [jax.ShapeDtypeStruct((4,) + g.shape[1:], g.dtype) for g in gs], _dma_sems(4 * n), start, wait)


def _rows2d(shape):
    cols = shape[-1]
    rows = 1
    for s in shape[:-1]:
        rows *= s
    return rows, cols


def _add_own(g, recv, name):
    rows, cols = _rows2d(g.shape[1:])
    tr = _pick(rows, 256, 8)
    c = lax.axis_index("c").astype(jnp.int32).reshape(1)

    def body(c_ref, g_ref, r_ref, o_ref):
        o_ref[...] = (g_ref[...].astype(F32) + r_ref[...].astype(F32)).astype(o_ref.dtype)

    grid_spec = pltpu.PrefetchScalarGridSpec(
        num_scalar_prefetch=1, grid=(4, rows // tr),
        in_specs=[pl.BlockSpec((None, None, tr, cols), lambda k, i, c_ref: (k, c_ref[0], i, 0)),
                  pl.BlockSpec((None, tr, cols), lambda k, i, c_ref: (k, i, 0))],
        out_specs=pl.BlockSpec((None, tr, cols), lambda k, i, c_ref: (k, i, 0)))
    return _call(body, name, jax.ShapeDtypeStruct((4, rows, cols), g.dtype), grid_spec=grid_spec,
                 dims=("parallel", "parallel"))(c, g.reshape(4, 2, rows, cols), recv.reshape(4, rows, cols))


def _reduce_chips_stage(pas):
    n = len(pas)

    def copies(pa_refs, out_refs, sems):
        send_sems, recv_sems, _ = sems
        x, y, c = _position()
        my_chip = 2 * x + y
        out = []
        for a in range(n):
            for j, (px, py) in enumerate([(1 - x, y), (x, 1 - y), (1 - x, 1 - y)]):
                send = pltpu.make_async_remote_copy(src_ref=pa_refs[a].at[2 * px + py], dst_ref=out_refs[a].at[my_chip],
                                                    send_sem=send_sems.at[3 * a + j], recv_sem=recv_sems.at[3 * a + j],
                                                    device_id=(px, py, c), device_id_type=MESH)
                recv = pltpu.make_async_remote_copy(src_ref=pa_refs[a].at[2 * px + py], dst_ref=out_refs[a].at[2 * px + py],
                                                    send_sem=send_sems.at[3 * a + j], recv_sem=recv_sems.at[3 * a + j],
                                                    device_id=(px, py, c), device_id_type=MESH)
                out.append((send, recv))
        return out

    def local(pa_refs, out_refs, sems):
        x, y, _ = _position()
        return [pltpu.make_async_copy(pa_refs[a].at[2 * x + y], out_refs[a].at[2 * x + y], sems[2].at[a]) for a in range(n)]

    def start(pa_refs, out_refs, sems):
        for cp in local(pa_refs, out_refs, sems):
            cp.start()
        for send, _r in copies(pa_refs, out_refs, sems):
            send.start()

    def wait(pa_refs, out_refs, sems):
        for send, recv in copies(pa_refs, out_refs, sems):
            recv.wait_recv()
            send.wait_send()
        for cp in local(pa_refs, out_refs, sems):
            cp.wait()

    return _Stage(pas, [jax.ShapeDtypeStruct(pa.shape, pa.dtype) for pa in pas],
                  _dma_sems(3 * n) + [pltpu.SemaphoreType.DMA((n,))], start, wait)


def _adamw_math(w, g, m, v):
    m = ADAM_B1 * m + (1.0 - ADAM_B1) * g
    v = ADAM_B2 * v + (1.0 - ADAM_B2) * (g * g)
    m_hat = m / (1.0 - ADAM_B1 ** ADAM_STEP)
    v_hat = v / (1.0 - ADAM_B2 ** ADAM_STEP)
    delta = -ADAM_LR * (m_hat / (jnp.sqrt(v_hat) + ADAM_EPS) + ADAM_WD * w)
    return delta, m, v


def _sum_adamw(parts, w, m, v, name):
    n, rows, cols = parts.shape
    tr = _pick(rows, 128, 8)

    def body(p_ref, w_ref, m_ref, v_ref, g_ref, d_ref, nm_ref, nv_ref):
        g = p_ref[0].astype(F32)
        for k in range(1, n):
            g = g + p_ref[k].astype(F32)
        d, nm, nv = _adamw_math(w_ref[...], g, m_ref[...], v_ref[...])
        g_ref[...], d_ref[...], nm_ref[...], nv_ref[...] = g, d, nm, nv

    blk = pl.BlockSpec((tr, cols), lambda i: (i, 0))
    return _call(body, name, [jax.ShapeDtypeStruct((rows, cols), F32)] * 4, grid=(rows // tr,),
                 in_specs=[pl.BlockSpec((n, tr, cols), lambda i: (0, i, 0)), blk, blk, blk],
                 out_specs=[blk] * 4, dims=("parallel",))(parts, w, m, v)


_WEIGHTS = ("norm_g", "mem_norm_g", "w_mem_kv", "w_out", "conv_w_in", "conv_dw", "conv_dw_b", "conv_ln_g", "conv_ln_b",
            "mla_w_in", "mla_q_norm_g", "mla_w_uq", "mla_kv_norm_g", "mla_w_ukv", "final_norm_g")


_GATHER_GROUPS = {"a": ("conv_w_in",), "b": ("w_mem_kv", "w_out"), "c": ("mla_w_in", "mla_w_uq", "mla_w_ukv")}
_CARRIERS = {"l0_norm": ("gather chips", ("a",)), "l0_in": ("gather chips", ("b",)), "l0_dwconv": ("gather chips", ("c",)),
             "l0_ln": ("gather sibling", ("b",)), "l0_out": ("gather sibling", ("c",)),
             "l1_in_dx": ("reduce sibling", ("l1",)), "l0_ln_bwd": ("reduce sibling", ("l0a",)),
             "l0_out_dx": ("reduce chips", ("l1", 0, 2)), "l0_dwconv_bwd": ("reduce chips", ("l1", 2, 5)),
             "l0_in_dw": ("reduce chips", ("l0a",)), "l0_in_dx": ("reduce sibling alone, then chips", ("l0b",))}


class _Schedule:
    def __init__(self, w):
        self.w, self.full, self.gather, self.reduce = w, {}, {}, {}
        small = _all_gather_small(_as_tiles([w[n] for n in _SMALL_SHARDED]), "gather_small_weights").reshape(N_DEV, -1)
        o = 0
        for n in _SMALL_SHARDED:
            self.full[n] = _join(n, small[:, o:o + w[n].size].reshape((N_DEV,) + w[n].shape))
            o += w[n].size
        for n in _REPLICATED:
            self.full[n] = w[n]

    def carry(self, call):
        kind, (g, *part) = _CARRIERS[call]
        if kind == "gather chips":
            self.gather[g] = [_gather_chips_stage([self.w[n].astype(BF16) for n in _GATHER_GROUPS[g]])]
            return self.gather[g][0]
        if kind == "gather sibling":
            self.gather[g].append(_gather_sibling_stage(self.gather[g][0].outs))
            return self.gather[g][1]
        r = self.reduce[g]
        if kind == "reduce sibling":
            r["sibling"] = _reduce_sibling_stage(r["cut"])
            return r["sibling"]
        if kind != "reduce chips":
            r["sibling"] = _reduce_sibling_stage(r["cut"])
            _run_stage(r["sibling"], "reduce_sibling_" + g)
        if "partial" not in r:
            r["partial"] = [_add_own(c, s, "reduce_add_%s_%d" % (g, i))
                            for i, (c, s) in enumerate(zip(r["cut"], r["sibling"].outs))]
        lo, hi = part if part else (0, len(r["keys"]))
        stage = _reduce_chips_stage(r["partial"][lo:hi])
        r.setdefault("chips", []).append((r["keys"][lo:hi], stage))
        return stage

    def __getitem__(self, name):
        if name not in self.full:
            g = [k for k, names in _GATHER_GROUPS.items() if name in names][0]
            if len(self.gather[g]) == 1:
                self.gather[g].append(_gather_sibling_stage(self.gather[g][0].outs))
                _run_stage(self.gather[g][1], "gather_sibling_" + g)
            for n, buf in zip(_GATHER_GROUPS[g], self.gather[g][1].outs):
                self.full[n] = _PERM[n][0](_join(n, buf)) if n in _PERM else _join(n, buf)
        return self.full[name]

    def ready(self, group, grads, payload=BF16):
        keys, cut, small = [], [], []
        for (n, layer), g in grads.items():
            if n in _SMALL_SHARDED:
                small.append(_cut(n, g, self.w[n].shape).reshape(N_DEV, -1))
                continue
            keys.append((n, layer))
            if layer is not None:
                cut.append(g.reshape((N_DEV,) + self.w[n].shape[1:]).astype(payload))
            else:
                cut.append(_cut(n, _PERM[n][1](g) if n in _PERM else g, self.w[n].shape).astype(payload))
        if small:
            keys.append(("small", None))
            cut.append(jax.vmap(lambda r: _as_tiles([r]))(jnp.concatenate(small, axis=1)))
        self.reduce[group] = {"keys": keys, "cut": cut}

    def finish(self):
        out = {}
        for r in self.reduce.values():
            for keys, stage in r["chips"]:
                out.update(dict(zip(keys, stage.outs)))
        return out


def kernel(x, mem, positions, norm_g, mem_norm_g, w_mem_kv, w_out, conv_w_in, conv_dw, conv_dw_b, conv_ln_g, conv_ln_b, mla_w_in, mla_q_norm_g, mla_w_uq, mla_kv_norm_g, mla_w_ukv, final_norm_g, loss_target, m_norm_g, m_mem_norm_g, m_w_mem_kv, m_w_out, m_conv_w_in, m_conv_dw, m_conv_dw_b, m_conv_ln_g, m_conv_ln_b, m_mla_w_in, m_mla_q_norm_g, m_mla_w_uq, m_mla_kv_norm_g, m_mla_w_ukv, m_final_norm_g, v_norm_g, v_mem_norm_g, v_w_mem_kv, v_w_out, v_conv_w_in, v_conv_dw, v_conv_dw_b, v_conv_ln_g, v_conv_ln_b, v_mla_w_in, v_mla_q_norm_g, v_mla_w_uq, v_mla_kv_norm_g, v_mla_w_ukv, v_final_norm_g):
    w = dict(zip(_WEIGHTS, (norm_g, mem_norm_g, w_mem_kv, w_out, conv_w_in, conv_dw, conv_dw_b, conv_ln_g, conv_ln_b,
                            mla_w_in, mla_q_norm_g, mla_w_uq, mla_kv_norm_g, mla_w_ukv, final_norm_g)))
    m = dict(zip(_WEIGHTS, (m_norm_g, m_mem_norm_g, m_w_mem_kv, m_w_out, m_conv_w_in, m_conv_dw, m_conv_dw_b, m_conv_ln_g,
                            m_conv_ln_b, m_mla_w_in, m_mla_q_norm_g, m_mla_w_uq, m_mla_kv_norm_g, m_mla_w_ukv, m_final_norm_g)))
    v = dict(zip(_WEIGHTS, (v_norm_g, v_mem_norm_g, v_w_mem_kv, v_w_out, v_conv_w_in, v_conv_dw, v_conv_dw_b, v_conv_ln_g,
                            v_conv_ln_b, v_mla_w_in, v_mla_q_norm_g, v_mla_w_uq, v_mla_kv_norm_g, v_mla_w_ukv, v_final_norm_g)))

    sched = _Schedule(w)
    loss_local, dx, G = _forward_backward(x, mem, positions, loss_target, sched)
    loss = lax.psum(loss_local, ("x", "y", "c"))

    from_chips = sched.finish()
    out = [{}, {}, {}, {}]
    for n in _BIG:
        if n in _ROW_CUT:
            res = [_sum_adamw(from_chips[(n, l)], w[n][l], m[n][l], v[n][l], "adamw_%s_%d" % (n, l)) for l in range(w[n].shape[0])]
            res = [jnp.stack(r) for r in zip(*res)]
        else:
            rows, cols = _rows2d(w[n].shape)
            res = _sum_adamw(from_chips[(n, None)], w[n].reshape(rows, cols), m[n].reshape(rows, cols),
                             v[n].reshape(rows, cols), "adamw_" + n)
        for o, r in zip(out, res):
            o[n] = r.reshape(w[n].shape)
    small_like = [w[n] for n in _SMALL_SHARDED]
    res = _sum_adamw(from_chips[("small", None)], _as_tiles(small_like), _as_tiles([m[n] for n in _SMALL_SHARDED]),
                     _as_tiles([v[n] for n in _SMALL_SHARDED]), "adamw_small")
    for o, r in zip(out, res):
        for n, a in zip(_SMALL_SHARDED, _split_flat(r.reshape(-1), small_like)):
            o[n] = a

    rep_like = [w[n] for n in _REPLICATED]
    rep_parts = _all_gather_small(_as_tiles([G[n] for n in _REPLICATED]), "gather_replicated_grads")
    res = _sum_adamw(rep_parts, _as_tiles(rep_like), _as_tiles([m[n] for n in _REPLICATED]),
                     _as_tiles([v[n] for n in _REPLICATED]), "adamw_replicated")
    for o, r in zip(out, res):
        for n, a in zip(_REPLICATED, _split_flat(r.reshape(-1), rep_like)):
            o[n] = a

    return (loss, dx, *[out[0][n] for n in _WEIGHTS], *[out[1][n] for n in _WEIGHTS],
            *[out[2][n] for n in _WEIGHTS], *[out[3][n] for n in _WEIGHTS])
```

```python
import jax
import jax.numpy as jnp
from jax import lax
from jax.experimental import pallas as pl
from jax.experimental.pallas import tpu as pltpu

F32 = jnp.float32
BF16 = jnp.bfloat16
MESH = pl.DeviceIdType.MESH
N_DEV = 8
VMEM_LIMIT_BYTES = 48 * 1024 * 1024

MEM_HEADS, MEM_HEAD_DIM = 4, 128
MEM_WIDTH = MEM_HEADS * MEM_HEAD_DIM
CONV_KERNEL = 31
CONV_PAD = 32
MLA_HEADS, MLA_NOPE, MLA_ROPE = 12, 128, 64
MLA_QK = MLA_NOPE + MLA_ROPE
HALF_ROPE = MLA_ROPE // 2
Q_RANK, KV_RANK = 512, 256
ROPE_THETA = 10000.0
RMS_EPS = 1e-6
LN_EPS = 1e-5
ADAM_LR, ADAM_B1, ADAM_B2, ADAM_EPS, ADAM_WD, ADAM_STEP = 0.001, 0.9, 0.999, 1e-08, 0.01, 10
NEG = -1e30


class _Stage:
    def __init__(self, ins, out_shapes, sems, start, wait, aliases=None):
        self.ins, self.out_shapes, self.sems = list(ins), list(out_shapes), list(sems)
        self.start, self.wait, self.aliases, self.outs = start, wait, dict(aliases or {}), None


def _call(body, name, out_shape, grid=None, in_specs=None, out_specs=None, scratch=(), dims=None, grid_spec=None, aliases=None,
          carry=None):
    params = dict(vmem_limit_bytes=VMEM_LIMIT_BYTES)
    if dims is not None:
        params["dimension_semantics"] = dims
    kw = {}
    if carry is not None:
        single = not isinstance(out_shape, (list, tuple))
        main_out = [out_shape] if single else list(out_shape)
        main_specs = [out_specs] if single else list(out_specs)
        n_in, n_out, n_scr = len(in_specs), len(main_out), len(scratch)
        x_in, x_out = len(carry.ins), len(carry.out_shapes)
        inner, steps = body, tuple(grid)

        def body(*refs):
            ins, xin = refs[:n_in], refs[n_in:n_in + x_in]
            outs = refs[n_in + x_in:n_in + x_in + n_out]
            xout = refs[n_in + x_in + n_out:n_in + x_in + n_out + x_out]
            scr = refs[n_in + x_in + n_out + x_out:n_in + x_in + n_out + x_out + n_scr]
            xsem = refs[n_in + x_in + n_out + x_out + n_scr:]
            ids = [pl.program_id(a) for a in range(len(steps))]
            first, last = ids[0] == 0, ids[0] == steps[0] - 1
            for a in range(1, len(steps)):
                first = jnp.logical_and(first, ids[a] == 0)
                last = jnp.logical_and(last, ids[a] == steps[a] - 1)
            pl.when(first)(lambda: carry.start(xin, xout, xsem))
            inner(*ins, *outs, *scr)
            pl.when(last)(lambda: carry.wait(xin, xout, xsem))

        hbm = pl.BlockSpec(memory_space=pltpu.HBM)
        aliases = dict(aliases or {})
        aliases.update({n_in + k: n_out + v for k, v in carry.aliases.items()})
        res = _call(body, name, main_out + carry.out_shapes, grid=grid, in_specs=list(in_specs) + [hbm] * x_in,
                    out_specs=main_specs + [hbm] * x_out, scratch=list(scratch) + carry.sems, dims=dims, aliases=aliases)

        def run(*args):
            outs = res(*args, *carry.ins)
            carry.outs = list(outs[n_out:])
            return outs[0] if single else outs[:n_out]

        return run
    if aliases:
        kw["input_output_aliases"] = aliases
    if grid_spec is not None:
        kw["grid_spec"] = grid_spec
    else:
        if grid is not None:
            kw["grid"] = grid
        kw["in_specs"] = in_specs
        kw["out_specs"] = out_specs
        kw["scratch_shapes"] = list(scratch)
    return pl.pallas_call(body, name=name, out_shape=out_shape, compiler_params=pltpu.CompilerParams(**params), **kw)


def _pick(n, target, mult):
    best = None
    for d in range(mult, min(n, target) + 1, mult):
        if n % d == 0:
            best = d
    return n if best is None else best


_DOT_DIMS = {"nn": (((1,), (0,)), ((), ())), "nt": (((1,), (1,)), ((), ())), "tn": (((0,), (0,)), ((), ()))}


def _mm(a, b, mode, out_dtype, name, res=None, carry=None):
    if mode == "tn":
        a, mode = a.T, "nn"
    if mode == "nn":
        (M, K), N = a.shape, b.shape[1]
    else:
        (M, K), N = a.shape, b.shape[0]
    tm = _pick(M, 1024, 8)
    tn = _pick(N, 1536, 128)
    tk = _pick(K, 1536, 128)
    nk = K // tk
    has_res = res is not None

    def body(*refs):
        if has_res:
            a_ref, b_ref, r_ref, o_ref, acc = refs
        else:
            a_ref, b_ref, o_ref, acc = refs
        k = pl.program_id(2)
        part = lax.dot_general(a_ref[...].astype(BF16), b_ref[...].astype(BF16), _DOT_DIMS[mode],
                               preferred_element_type=F32)
        if nk == 1:
            o_ref[...] = (part + r_ref[...] if has_res else part).astype(o_ref.dtype)
            return

        @pl.when(k == 0)
        def _():
            acc[...] = part

        @pl.when(k > 0)
        def _():
            acc[...] += part

        @pl.when(k == nk - 1)
        def _():
            r = acc[...]
            if has_res:
                r = r + r_ref[...]
            o_ref[...] = r.astype(o_ref.dtype)

    a_spec = pl.BlockSpec((tm, tk), lambda i, j, k: (i, k))
    b_spec = {"nn": pl.BlockSpec((tk, tn), lambda i, j, k: (k, j)),
              "nt": pl.BlockSpec((tn, tk), lambda i, j, k: (j, k))}[mode]
    o_spec = pl.BlockSpec((tm, tn), lambda i, j, k: (i, j))
    in_specs = [a_spec, b_spec] + ([o_spec] if has_res else [])
    args = (a, b) + ((res,) if has_res else ())
    return _call(body, name, jax.ShapeDtypeStruct((M, N), out_dtype), grid=(M // tm, N // tn, nk),
                 in_specs=in_specs, out_specs=o_spec, scratch=[pltpu.VMEM((tm, tn), F32)],
                 dims=("parallel", "parallel", "arbitrary"), carry=carry)(*args)


def _views(rows):
    return [r if isinstance(r, tuple) else (r, r.shape[1], 0) for r in rows]


def _row_tile(T, rows):
    return min(T, 512 if max(w for _, w, _ in rows) <= 1024 else 256)


def _rowwise(f, rows, params, outs, name, carry=None, into=None):
    rows = _views(rows)
    T = rows[0][0].shape[0]
    tb = _row_tile(T, rows)
    nr, npar = len(rows), len(params)
    outs = [o if len(o) == 3 else (o[0], o[1], o[0]) for o in outs]
    into = into or []

    def body(*refs):
        vals = f(*[r[...].astype(F32) for r in refs[:nr]], *[p[...] for p in refs[nr:nr + npar]])
        for o_ref, v in zip(refs[nr + npar + len(into):], vals):
            o_ref[...] = v.astype(o_ref.dtype)

    row_spec = lambda w, cb=0: pl.BlockSpec((tb, w), lambda i: (i, cb))
    par_spec = lambda w: pl.BlockSpec((1, w), lambda i: (0, 0))
    out_shape = [jax.ShapeDtypeStruct((T, tw), dt) for _, dt, tw in outs]
    out_specs = [row_spec(w) for w, _, _ in outs]
    in_specs = [row_spec(w, cb) for _, w, cb in rows] + [par_spec(p.shape[1]) for p in params]
    args = [r[0] for r in rows] + list(params)
    aliases = {}
    for k, arr, cb in into:
        aliases[len(args)] = k
        in_specs.append(pl.BlockSpec(memory_space=pl.ANY))
        args.append(arr)
        out_shape[k] = jax.ShapeDtypeStruct(arr.shape, arr.dtype)
        out_specs[k] = row_spec(outs[k][0], cb)
    return _call(body, name, out_shape, grid=(T // tb,), in_specs=in_specs, out_specs=out_specs, dims=("parallel",),
                 carry=carry, aliases=aliases)(*args)


def _rowwise_bwd(f, rows, params, douts, n_diff, name, carry=None, into=None):
    rows, douts = _views(rows), _views(douts)
    T = rows[0][0].shape[0]
    tb = _row_tile(T, rows)
    nr, npar, nd = len(rows), len(params), len(douts)

    def body(*refs):
        rv = [r[...].astype(F32) for r in refs[:nr]]
        pv = [p[...] for p in refs[nr:nr + npar]]
        dv = [d[...].astype(F32) for d in refs[nr + npar:nr + npar + nd]]
        o_refs = refs[nr + npar + nd + (0 if into is None else 1):]
        fixed = rv[n_diff:]

        def g(*xs):
            return tuple(f(*xs[:n_diff], *fixed, *xs[n_diff:]))

        _, vjp = jax.vjp(g, *rv[:n_diff], *pv)
        grads = vjp(tuple(dv))
        for o_ref, gr in zip(o_refs[:n_diff], grads[:n_diff]):
            o_ref[...] = gr.astype(o_ref.dtype)
        first = pl.program_id(0) == 0
        for o_ref, gr in zip(o_refs[n_diff:], grads[n_diff:]):
            @pl.when(first)
            def _(o_ref=o_ref):
                o_ref[...] = jnp.zeros_like(o_ref)

            o_ref[...] += gr

    row_spec = lambda w, cb=0: pl.BlockSpec((tb, w), lambda i: (i, cb))
    par_spec = lambda w: pl.BlockSpec((1, w), lambda i: (0, 0))
    out_shape = ([jax.ShapeDtypeStruct((T, w), F32) for _, w, _ in rows[:n_diff]]
                 + [jax.ShapeDtypeStruct((1, p.shape[1]), F32) for p in params])
    out_specs = [row_spec(w) for _, w, _ in rows[:n_diff]] + [par_spec(p.shape[1]) for p in params]
    in_specs = ([row_spec(w, cb) for _, w, cb in rows] + [par_spec(p.shape[1]) for p in params]
                + [row_spec(w, cb) for _, w, cb in douts])
    args = [r[0] for r in rows] + list(params) + [d[0] for d in douts]
    aliases = None
    if into is not None:
        aliases = {len(args): 0}
        in_specs.append(pl.BlockSpec(memory_space=pl.ANY))
        args.append(into[0])
        out_shape[0] = jax.ShapeDtypeStruct(into[0].shape, into[0].dtype)
        out_specs[0] = row_spec(rows[0][1], into[1])
    return _call(body, name, out_shape, grid=(T // tb,), in_specs=in_specs, out_specs=out_specs,
                 dims=("arbitrary",), carry=carry, aliases=aliases)(*args)


def _sig(x):
    return 1.0 / (1.0 + jnp.exp(-x))


def _rms(x, g):
    return x * lax.rsqrt(jnp.mean(x * x, axis=-1, keepdims=True) + RMS_EPS) * g


def _f_rms(x, g):
    return (_rms(x, g),)


def _f_ln_silu(x, g, b):
    mu = jnp.mean(x, axis=-1, keepdims=True)
    xc = x - mu
    var = jnp.mean(xc * xc, axis=-1, keepdims=True)
    y = xc * lax.rsqrt(var + LN_EPS) * g + b
    return (y * _sig(y),)


def _rope128(x, cos_p, sin_p):
    return x * cos_p + pltpu.roll(x, 64, 1) * sin_p


def _rope128_t(d, cos_p, sin_p):
    return d * cos_p + pltpu.roll(d * sin_p, 64, 1)


def _f_rope(xq, xk, cos_p, sin_p):
    heads = [_rope128(xq[:, h * 128:(h + 1) * 128], cos_p, sin_p) for h in range(MLA_HEADS)]
    return (jnp.concatenate(heads, axis=1), _rope128(xk, cos_p, sin_p))


def _f_rope_t(dq, dk_heads, cos_p, sin_p):
    heads = [_rope128_t(dq[:, h * 128:(h + 1) * 128], cos_p, sin_p) for h in range(MLA_HEADS)]
    dk = dk_heads[:, 0:128]
    for h in range(1, MLA_HEADS):
        dk = dk + dk_heads[:, h * 128:(h + 1) * 128]
    return (jnp.concatenate(heads, axis=1), _rope128_t(dk, cos_p, sin_p))


GATE_LANES = 512


def _gate_out(ycat, proj, z_col, w_out, res, name, tb=1024, carry=None):
    T, width = ycat.shape
    D = w_out.shape[1]
    zb = z_col // GATE_LANES
    nk = width // GATE_LANES

    def body(y_ref, z_ref, w_ref, r_ref, o_ref, yt_ref, acc):
        k = pl.program_id(1)
        z = z_ref[...]
        y = y_ref[...] * (z * _sig(z))
        yt_ref[...] = y.T.astype(yt_ref.dtype)
        part = jnp.dot(y.astype(BF16), w_ref[...], preferred_element_type=F32)

        @pl.when(k == 0)
        def _():
            acc[...] = part

        @pl.when(k > 0)
        def _():
            acc[...] += part

        @pl.when(k == nk - 1)
        def _():
            o_ref[...] = acc[...] + r_ref[...]

    row = pl.BlockSpec((tb, D), lambda i, k: (i, 0))
    return _call(body, name, [jax.ShapeDtypeStruct((T, D), F32), jax.ShapeDtypeStruct((width, T), BF16)],
                 grid=(T // tb, nk),
                 in_specs=[pl.BlockSpec((tb, GATE_LANES), lambda i, k: (i, k)),
                           pl.BlockSpec((tb, GATE_LANES), lambda i, k: (i, zb + k)),
                           pl.BlockSpec((GATE_LANES, D), lambda i, k: (k, 0)), row],
                 out_specs=[row, pl.BlockSpec((GATE_LANES, tb), lambda i, k: (k, i))],
                 scratch=[pltpu.VMEM((tb, D), F32)], dims=("parallel", "arbitrary"), carry=carry)(ycat, proj, w_out, res)


def _out_dx_gate_bwd(dh, w_out, ycat, proj, z_col, name, tb=1024, carry=None):
    T, width = ycat.shape
    D = dh.shape[1]
    zb = z_col // GATE_LANES

    def body(dh_ref, w_ref, y_ref, z_ref, dycat_ref, dz_ref):
        d = lax.dot_general(dh_ref[...].astype(BF16), w_ref[...], _DOT_DIMS["nt"], preferred_element_type=F32)
        z = z_ref[...]
        s = _sig(z)
        dycat_ref[...] = d * (z * s)
        dz_ref[...] = (d * y_ref[...] * (s * (1.0 + z * (1.0 - s)))).astype(dz_ref.dtype)

    blk = pl.BlockSpec((tb, GATE_LANES), lambda i, c: (i, c))
    zblk = pl.BlockSpec((tb, GATE_LANES), lambda i, c: (i, zb + c))
    return _call(body, name, [jax.ShapeDtypeStruct((T, width), F32), jax.ShapeDtypeStruct(proj.shape, BF16)],
                 grid=(T // tb, width // GATE_LANES),
                 in_specs=[pl.BlockSpec((tb, D), lambda i, c: (i, 0)), pl.BlockSpec((GATE_LANES, D), lambda i, c: (c, 0)),
                           blk, zblk],
                 out_specs=[blk, zblk], dims=("parallel", "parallel"), carry=carry)(dh, w_out, ycat, proj)


def _in_dx_norm_bwd(d_proj, w_in, h, g, add, name, tm=512, carry=None):
    T, K = d_proj.shape
    D = w_in.shape[0]
    tk = _pick(K, 1536, 128)
    nk = K // tk

    def body(a_ref, b_ref, h_ref, g_ref, add_ref, dx_ref, dg_ref, acc):
        i, k = pl.program_id(0), pl.program_id(1)
        part = lax.dot_general(a_ref[...], b_ref[...], _DOT_DIMS["nt"], preferred_element_type=F32)

        @pl.when(jnp.logical_and(i == 0, k == 0))
        def _():
            dg_ref[...] = jnp.zeros_like(dg_ref)

        @pl.when(k == 0)
        def _():
            acc[...] = part

        @pl.when(k > 0)
        def _():
            acc[...] += part

        @pl.when(k == nk - 1)
        def _():
            _, vjp = jax.vjp(_rms, h_ref[...], g_ref[...])
            dh, dg = vjp(acc[...])
            dx_ref[...] = dh + add_ref[...]
            dg_ref[...] += dg

    row = pl.BlockSpec((tm, D), lambda i, k: (i, 0))
    par = pl.BlockSpec((1, D), lambda i, k: (0, 0))
    return _call(body, name, [jax.ShapeDtypeStruct((T, D), F32), jax.ShapeDtypeStruct((1, D), F32)], grid=(T // tm, nk),
                 in_specs=[pl.BlockSpec((tm, tk), lambda i, k: (i, k)), pl.BlockSpec((D, tk), lambda i, k: (0, k)),
                           row, par, row],
                 out_specs=[row, par], scratch=[pltpu.VMEM((tm, D), F32)], dims=("arbitrary", "arbitrary"),
                 carry=carry)(d_proj, w_in, h, g, add)


def _glu_bwd(proj, d_glu, d_proj, name, tb=256):
    T, w = d_glu.shape

    def body(a_ref, g_ref, d_ref, _, o_ref):
        s, d = _sig(g_ref[...]), d_ref[...]
        o_ref[:, 0:w] = (d * s).astype(o_ref.dtype)
        o_ref[:, w:2 * w] = (d * a_ref[...] * (s * (1.0 - s))).astype(o_ref.dtype)

    return _call(body, name, jax.ShapeDtypeStruct(d_proj.shape, d_proj.dtype), grid=(T // tb,),
                 in_specs=[pl.BlockSpec((tb, w), lambda i: (i, 0)), pl.BlockSpec((tb, w), lambda i: (i, 1)),
                           pl.BlockSpec((tb, w), lambda i: (i, 0)), pl.BlockSpec(memory_space=pl.ANY)],
                 out_specs=pl.BlockSpec((tb, 2 * w), lambda i: (i, 0)), dims=("parallel",),
                 aliases={3: 0})(proj, proj, d_glu, d_proj)


def _final_loss(h, tgt, g, name, tb=512):
    T, D = h.shape

    def body(h_ref, t_ref, g_ref, dh_ref, dg_ref, loss_ref):
        tv = t_ref[...]

        def rowloss(hh, gg):
            e = _rms(hh, gg) - tv
            return 0.5 * jnp.mean(e * e, axis=-1, keepdims=True)

        lr, vjp = jax.vjp(rowloss, h_ref[...], g_ref[...])
        dh, dg = vjp(jnp.ones_like(lr))
        dh_ref[...] = dh

        @pl.when(pl.program_id(0) == 0)
        def _():
            dg_ref[...] = jnp.zeros_like(dg_ref)
            loss_ref[...] = jnp.zeros_like(loss_ref)

        dg_ref[...] += dg
        loss_ref[...] += jnp.broadcast_to(jnp.sum(lr, axis=0, keepdims=True), loss_ref.shape)

    row = pl.BlockSpec((tb, D), lambda i: (i, 0))
    par = pl.BlockSpec((1, D), lambda i: (0, 0))
    return _call(body, name,
                 [jax.ShapeDtypeStruct((T, D), F32), jax.ShapeDtypeStruct((1, D), F32), jax.ShapeDtypeStruct((1, 128), F32)],
                 grid=(T // tb,), in_specs=[row, row, par],
                 out_specs=[row, par, pl.BlockSpec((1, 128), lambda i: (0, 0))], dims=("arbitrary",))(h, tgt, g)


CONV_ROWS = 128
CONV_LANES = 256


def _sublane_phases(pad, n):
    for r in range(1, 8):
        for c0 in range(0, n - 8, 256):
            rows = min(256, n - 8 - c0)
            pad[r, c0:c0 + rows, :] = pad[0, c0 + r:c0 + r + rows, :]


def _dwconv_fwd(proj, C, w, b, B, S, name, carry=None):
    cb = CONV_LANES
    off = CONV_PAD - (CONV_KERNEL - 1)

    def body(a_ref, g_ref, w_ref, b_ref, o_ref, pad):
        pad[0, 0:CONV_PAD, :] = jnp.zeros((CONV_PAD, cb), F32)
        for c0 in range(0, S, 256):
            pad[0, CONV_PAD + c0:CONV_PAD + c0 + 256, :] = a_ref[c0:c0 + 256, :] * _sig(g_ref[c0:c0 + 256, :])
        _sublane_phases(pad, S + CONV_PAD)
        for t0 in range(0, S, CONV_ROWS):
            acc = jnp.broadcast_to(b_ref[...], (CONV_ROWS, cb))
            for k in range(CONV_KERNEL):
                r, base = (off + k) % 8, t0 + (off + k) // 8 * 8
                acc = acc + w_ref[k:k + 1, :] * pad[r, base:base + CONV_ROWS, :]
            o_ref[t0:t0 + CONV_ROWS, :] = acc

    return _call(body, name, jax.ShapeDtypeStruct((B, S, C), F32), grid=(B, C // cb),
                 in_specs=[pl.BlockSpec((S, cb), lambda i, j: (i, j)), pl.BlockSpec((S, cb), lambda i, j: (i, C // cb + j)),
                           pl.BlockSpec((CONV_KERNEL, cb), lambda i, j: (0, j)),
                           pl.BlockSpec((1, cb), lambda i, j: (0, j))],
                 out_specs=pl.BlockSpec((None, S, cb), lambda i, j: (i, 0, j)),
                 scratch=[pltpu.VMEM((8, S + CONV_PAD, cb), F32)], dims=("parallel", "parallel"),
                 carry=carry)(proj, proj, w, b)


def _dwconv_bwd(proj, w, dy, name, carry=None):
    B, S, C = dy.shape
    cb = CONV_LANES
    groups = CONV_ROWS // 8

    def body(a_ref, g_ref, w_ref, dy_ref, dx_ref, dw_ref, db_ref, dypad, wacc):
        dypad[0, 0:S, :] = dy_ref[...]
        dypad[0, S:, :] = jnp.zeros((CONV_PAD, cb), F32)
        _sublane_phases(dypad, S + CONV_PAD)
        wacc[...] = jnp.zeros_like(wacc)
        for t0 in range(0, S, CONV_ROWS):
            xc = a_ref[t0:t0 + CONV_ROWS, :] * _sig(g_ref[t0:t0 + CONV_ROWS, :])
            acc = jnp.zeros((CONV_ROWS, cb), F32)
            for k in range(CONV_KERNEL):
                o = (CONV_KERNEL - 1) - k
                dys = dypad[o % 8, t0 + o // 8 * 8:t0 + o // 8 * 8 + CONV_ROWS, :]
                acc = acc + w_ref[k:k + 1, :] * dys
                wacc[k] += jnp.sum((dys * xc).reshape(groups, 8, cb), axis=0)
            wacc[CONV_KERNEL] += jnp.sum(dy_ref[t0:t0 + CONV_ROWS, :].reshape(groups, 8, cb), axis=0)
            dx_ref[t0:t0 + CONV_ROWS, :] = acc

        @pl.when(pl.program_id(1) == 0)
        def _():
            dw_ref[...] = jnp.zeros_like(dw_ref)
            db_ref[...] = jnp.zeros_like(db_ref)

        for k in range(CONV_KERNEL):
            dw_ref[k:k + 1, :] += jnp.sum(wacc[k], axis=0, keepdims=True)
        db_ref[...] += jnp.sum(wacc[CONV_KERNEL], axis=0, keepdims=True)

    blk = pl.BlockSpec((None, S, cb), lambda j, i: (i, 0, j))
    return _call(body, name,
                 [jax.ShapeDtypeStruct((B, S, C), F32), jax.ShapeDtypeStruct((CONV_KERNEL, C), F32),
                  jax.ShapeDtypeStruct((1, C), F32)],
                 grid=(C // cb, B),
                 in_specs=[pl.BlockSpec((S, cb), lambda j, i: (i, j)), pl.BlockSpec((S, cb), lambda j, i: (i, C // cb + j)),
                           pl.BlockSpec((CONV_KERNEL, cb), lambda j, i: (0, j)), blk],
                 out_specs=[blk, pl.BlockSpec((CONV_KERNEL, cb), lambda j, i: (0, j)),
                            pl.BlockSpec((1, cb), lambda j, i: (0, j))],
                 scratch=[pltpu.VMEM((8, S + CONV_PAD, cb), F32), pltpu.VMEM((CONV_KERNEL + 1, 8, cb), F32)],
                 dims=("parallel", "arbitrary"), carry=carry)(proj, proj, w, dy)


ATTN_TILE = {"fwd": 1024, "bwd": 1024, "cross fwd": 512}
ATTN_SUB = {"fwd": 256, "bwd": 512}


def _attn_shapes(Sq, Sk, causal, pass_):
    tq = min(Sq, ATTN_TILE[pass_ if causal or pass_ == "bwd" else "cross fwd"])
    tk = tq if causal else min(Sk, ATTN_TILE[pass_])
    return tq, tk, min(ATTN_SUB[pass_], tq)


def _causal_bias(n):
    r = lax.broadcasted_iota(jnp.int32, (n, n), 0)
    c = lax.broadcasted_iota(jnp.int32, (n, n), 1)
    return jnp.where(c <= r, 0.0, NEG).astype(F32)


def _mask_diagonal(s, bias):
    n, nc = s.shape
    if nc == n:
        return s + bias
    return jnp.concatenate([s[:, :nc - n], s[:, nc - n:] + bias], axis=1)


def _attn_fwd(q, q_c0, qr, k, k_c0, kr, v, v_c0, B, Sq, Sk, H, causal, scale, name, into=None, o_c0=0, o_width=None,
              kv_stride=1):
    tq, tk, sub = _attn_shapes(Sq, Sk, causal, "fwd")
    nq, nk, nsub = Sq // tq, Sk // tk, tq // sub
    rope = qr is not None

    def body(*refs):
        refs = list(refs)
        qn_ref = refs.pop(0)
        qr_ref = refs.pop(0) if rope else None
        kn_ref = refs.pop(0)
        kr_ref = refs.pop(0) if rope else None
        v_ref = refs.pop(0)
        if into is not None:
            refs.pop(0)
        o_ref, lse_ref, m_s, l_s, acc = refs
        qi = pl.program_id(2)
        m_s[...] = jnp.full_like(m_s, NEG)
        l_s[...] = jnp.zeros_like(l_s)
        acc[...] = jnp.zeros_like(acc)
        bias = _causal_bias(sub) if causal else None
        qs = []
        for r in range(nsub):
            qn = qn_ref[r * sub:(r + 1) * sub, :].astype(BF16)
            qs.append(jnp.concatenate([qn, qr_ref[r * sub:(r + 1) * sub, :]], axis=1) if rope else qn)

        def step(j, masked):
            ks = pl.ds(pl.multiple_of(j * tk, tk), tk)
            kk = jnp.concatenate([kn_ref[ks, :], kr_ref[ks, :]], axis=1) if rope else kn_ref[ks, :]
            vv = v_ref[ks, :]
            for r in range(nsub):
                rows = slice(r * sub, (r + 1) * sub)
                nc = (r + 1) * sub if masked else tk
                s = lax.dot_general(qs[r], kk[:nc], _DOT_DIMS["nt"], preferred_element_type=F32) * scale
                if masked:
                    s = _mask_diagonal(s, bias)
                m_old = m_s[rows, :]
                m_new = jnp.maximum(m_old, jnp.max(s, axis=-1, keepdims=True))
                p = jnp.exp(s - m_new)
                alpha = jnp.exp(m_old - m_new)
                l_s[rows, :] = alpha * l_s[rows, :] + jnp.sum(p, axis=-1, keepdims=True)
                acc[rows, :] = alpha * acc[rows, :] + jnp.dot(p.astype(BF16), vv[:nc], preferred_element_type=F32)
                m_s[rows, :] = m_new

        def unmasked(j, carry):
            step(j, False)
            return carry

        if causal:
            lax.fori_loop(0, qi, unmasked, 0)
            step(qi, True)
        else:
            lax.fori_loop(0, nk, unmasked, 0)
        o_ref[...] = (acc[...] / l_s[...]).astype(o_ref.dtype)
        lse_ref[...] = m_s[...] + jnp.log(l_s[...])

    qspec = lambda c0: pl.BlockSpec((tq, 128), lambda b, h, i: (b * nq + i, c0 + h))
    kspec = lambda c0: pl.BlockSpec((Sk, 128), lambda b, h, i: (b, c0 + kv_stride * h))
    in_specs, args = [qspec(q_c0)], [q]
    if rope:
        in_specs.append(qspec(0)); args.append(qr)
    in_specs.append(kspec(k_c0)); args.append(k)
    if rope:
        in_specs.append(pl.BlockSpec((Sk, 128), lambda b, h, i: (b, 0))); args.append(kr)
    in_specs.append(kspec(v_c0)); args.append(v)
    aliases = {}
    if into is not None:
        aliases = {len(args): 0}
        in_specs.append(pl.BlockSpec(memory_space=pl.ANY)); args.append(into)
        o_shape = jax.ShapeDtypeStruct(into.shape, into.dtype)
    else:
        o_shape = jax.ShapeDtypeStruct((B * Sq, o_width), F32)
    return _call(body, name, [o_shape, jax.ShapeDtypeStruct((B * H, Sq, 1), F32)], grid=(B, H, nq), in_specs=in_specs,
                 out_specs=[qspec(o_c0), pl.BlockSpec((None, tq, 1), lambda b, h, i: (b * H + h, i, 0))],
                 scratch=[pltpu.VMEM((tq, 1), F32), pltpu.VMEM((tq, 1), F32), pltpu.VMEM((tq, 128), F32)],
                 dims=("parallel", "parallel", "arbitrary"), aliases=aliases)(*args)


def _attn_bwd(q, q_c0, qr, k, k_c0, kr, v, v_c0, o, do, o_c0, lse, B, Sq, Sk, H, causal, scale, name, dq_into=None,
              kv_stride=1):
    tq, tk, sub = _attn_shapes(Sq, Sk, causal, "bwd")
    nq, nk, nsub = Sq // tq, Sk // tk, tq // sub
    rope = qr is not None
    dk_w = 256 if rope else 128

    def body(*refs):
        refs = list(refs)
        qn_ref = refs.pop(0)
        qr_ref = refs.pop(0) if rope else None
        kn_ref = refs.pop(0)
        kr_ref = refs.pop(0) if rope else None
        v_ref, o_ref, do_ref, lse_ref = refs[:4]
        refs = refs[4 + (0 if dq_into is None else 1):]
        dqn_ref = refs.pop(0)
        dqr_ref = refs.pop(0) if rope else None
        dkn_ref = refs.pop(0)
        dkr_ref = refs.pop(0) if rope else None
        dv_ref = None if rope else refs.pop(0)
        q_s, do_s, dl_s, dq_acc, dk_acc, dv_acc = refs
        kj = pl.program_id(2)

        @pl.when(kj == 0)
        def _():
            qn = qn_ref[...].astype(BF16)
            q_s[...] = jnp.concatenate([qn, qr_ref[...]], axis=1) if rope else qn
            dof = do_ref[...]
            do_s[...] = dof.astype(BF16)
            dl_s[...] = jnp.sum(dof * o_ref[...], axis=-1, keepdims=True)
            dq_acc[...] = jnp.zeros_like(dq_acc)

        kk = jnp.concatenate([kn_ref[...], kr_ref[...]], axis=1) if rope else kn_ref[...]
        vv = v_ref[...]
        bias = _causal_bias(sub) if causal else None
        dk_acc[...] = jnp.zeros_like(dk_acc)
        dv_acc[...] = jnp.zeros_like(dv_acc)

        def step(i, masked):
            for r in range(nsub):
                rows = pl.ds(pl.multiple_of(i * tq + r * sub, sub), sub)
                qq, dob = q_s[rows, :], do_s[rows, :]
                nc = (r + 1) * sub if masked else tk
                kc, vc = kk[:nc], vv[:nc]
                s = lax.dot_general(qq, kc, _DOT_DIMS["nt"], preferred_element_type=F32) * scale
                if masked:
                    s = _mask_diagonal(s, bias)
                p = jnp.exp(s - lse_ref[rows, :])
                dp = lax.dot_general(dob, vc, _DOT_DIMS["nt"], preferred_element_type=F32)
                ds = (p * (dp - dl_s[rows, :]) * scale).astype(BF16)
                dv_acc[0:nc, :] += lax.dot_general(p.astype(BF16), dob, _DOT_DIMS["tn"], preferred_element_type=F32)
                dk_acc[0:nc, :] += lax.dot_general(ds, qq, _DOT_DIMS["tn"], preferred_element_type=F32)
                dq_acc[rows, :] += jnp.dot(ds, kc, preferred_element_type=F32)

        def unmasked(i, carry):
            step(i, False)
            return carry

        if causal:
            step(kj, True)
            lax.fori_loop(kj + 1, nq, unmasked, 0)
        else:
            lax.fori_loop(0, nq, unmasked, 0)
        if rope:
            dkn_ref[...] = jnp.concatenate([dk_acc[:, 0:128], dv_acc[...]], axis=1).astype(dkn_ref.dtype)
            dkr_ref[...] = dk_acc[:, 128:256]
        else:
            dkn_ref[...] = dk_acc[...]
            dv_ref[...] = dv_acc[...]

        @pl.when(kj == nk - 1)
        def _():
            dqn_ref[...] = dq_acc[:, 0:128].astype(dqn_ref.dtype)
            if rope:
                dqr_ref[...] = dq_acc[:, 128:256]

    qspec = lambda c0: pl.BlockSpec((Sq, 128), lambda b, h, j: (b, c0 + h))
    kspec = lambda c0: pl.BlockSpec((tk, 128), lambda b, h, j: (b * nk + j, c0 + kv_stride * h))
    in_specs, args = [qspec(q_c0)], [q]
    if rope:
        in_specs.append(qspec(0)); args.append(qr)
    in_specs.append(kspec(k_c0)); args.append(k)
    if rope:
        in_specs.append(pl.BlockSpec((tk, 128), lambda b, h, j: (b * nk + j, 0))); args.append(kr)
    in_specs += [kspec(v_c0), qspec(o_c0), qspec(o_c0), pl.BlockSpec((None, Sq, 1), lambda b, h, j: (b * H + h, 0, 0))]
    args += [v, o, do, lse]
    h_rows_q = jax.ShapeDtypeStruct((B * Sq, H * 128), F32)
    h_rows_k = jax.ShapeDtypeStruct((B * Sk, H * 128), F32)
    out_shape, out_specs, aliases = [h_rows_q], [qspec(0)], None
    if rope:
        out_shape = [jax.ShapeDtypeStruct((B * Sq, 2 * H * 128), BF16)]
    if dq_into is not None:
        aliases = {len(args): 0}
        in_specs.append(pl.BlockSpec(memory_space=pl.ANY)); args.append(dq_into[0])
        out_shape, out_specs = [jax.ShapeDtypeStruct(dq_into[0].shape, dq_into[0].dtype)], [qspec(dq_into[1])]
    if rope:
        out_shape.append(h_rows_q); out_specs.append(qspec(0))
    hspec = lambda w: pl.BlockSpec((tk, w), lambda b, h, j: (b * nk + j, h))
    if rope:
        out_shape += [jax.ShapeDtypeStruct((B * Sk, H * 256), BF16), h_rows_k]
        out_specs += [hspec(256), hspec(128)]
    else:
        out_shape += [h_rows_k, h_rows_k]
        out_specs += [hspec(128), hspec(128)]
    return _call(body, name, out_shape, grid=(B, H, nk), in_specs=in_specs, out_specs=out_specs,
                 scratch=[pltpu.VMEM((Sq, dk_w), BF16), pltpu.VMEM((Sq, 128), BF16), pltpu.VMEM((Sq, 1), F32),
                          pltpu.VMEM((Sq, dk_w), F32), pltpu.VMEM((tk, dk_w), F32), pltpu.VMEM((tk, 128), F32)],
                 dims=("parallel", "parallel", "arbitrary"), aliases=aliases)(*args)


def _mem_attention_fwd(proj, q_col, ycat, mem2, mem_g, w_mem, B, S, tag):
    M = mem2.shape[0] // B
    (memn,) = _rowwise(_f_rms, [mem2], [mem_g], [(mem2.shape[1], BF16)], tag + "_memnorm")
    kvm = _mm(memn, w_mem, "nn", BF16, tag + "_memkv")
    o_c0 = ycat.shape[1] // 128 - MEM_HEADS
    ycat, lse = _attn_fwd(proj, q_col // 128, None, kvm, 0, None, kvm, MEM_HEADS, B, S, M, MEM_HEADS, False,
                          MEM_HEAD_DIM ** -0.5, tag + "_memattn", into=ycat, o_c0=o_c0)
    return ycat, (memn, kvm, lse)


def _mem_attention_bwd(proj, q_col, ycat, d_ycat, d_proj, saved, mem2, mem_g, w_mem, B, S, tag):
    memn, kvm, lse = saved
    M = mem2.shape[0] // B
    o_c0 = ycat.shape[1] // 128 - MEM_HEADS
    d_q, d_k, d_v = _attn_bwd(proj, q_col // 128, None, kvm, 0, None, kvm, MEM_HEADS, ycat, d_ycat, o_c0, lse, B, S, M,
                              MEM_HEADS, False, MEM_HEAD_DIM ** -0.5, tag + "_memattn_bwd", dq_into=(d_proj, q_col // 128))
    d_kvm = jnp.concatenate([d_k, d_v], axis=1).astype(BF16)
    d_w_mem = _mm(memn, d_kvm, "tn", F32, tag + "_memkv_dw")
    d_memn = _mm(d_kvm, w_mem, "nt", F32, tag + "_memkv_dx")
    _, d_mem_g = _rowwise_bwd(_f_rms, [mem2], [mem_g], [d_memn], 1, tag + "_memnorm_bwd")
    return d_q, d_w_mem, d_mem_g


def _rope_tables(positions):
    inv_freq = 1.0 / (ROPE_THETA ** (jnp.arange(0, MLA_ROPE, 2, dtype=F32) / MLA_ROPE))
    ang = positions.astype(F32).reshape(-1, 1) * inv_freq
    cos, sin, zero = jnp.cos(ang), jnp.sin(ang), jnp.zeros_like(ang)
    return jnp.concatenate([cos, zero, cos, zero], axis=1), jnp.concatenate([-sin, zero, sin, zero], axis=1)


def _forward_backward(x, mem, positions, target, W):
    B, S, D = x.shape
    T = B * S
    mix_w = 2 * D
    h0 = x.reshape(T, D)
    mem2 = mem.reshape(-1, D)
    tgt = target.reshape(T, D)
    row = lambda v: v.reshape(1, -1)
    n_nope = MLA_HEADS * MLA_NOPE

    g0 = row(W["norm_g"][0])
    (u0,) = _rowwise(_f_rms, [h0], [g0], [(D, BF16)], "l0_norm", carry=W.carry("l0_norm"))
    proj0 = _mm(u0, W["conv_w_in"], "nn", F32, "l0_in", carry=W.carry("l0_in"))
    dw, dwb = W["conv_dw"], row(W["conv_dw_b"][0])
    conv_w = dw.shape[1]
    qm0_col, z0_col = 2 * conv_w, 2 * conv_w + MEM_WIDTH
    cv = _dwconv_fwd(proj0, conv_w, dw, dwb, B, S, "l0_dwconv", carry=W.carry("l0_dwconv")).reshape(T, conv_w)
    ln_g, ln_b = row(W["conv_ln_g"][0]), row(W["conv_ln_b"][0])
    (ycat0,) = _rowwise(_f_ln_silu, [cv], [ln_g, ln_b], [(conv_w, F32, mix_w)], "l0_ln", carry=W.carry("l0_ln"))
    mg0 = row(W["mem_norm_g"][0])
    ycat0, mem_saved0 = _mem_attention_fwd(proj0, qm0_col, ycat0, mem2, mg0, W["w_mem_kv"][0], B, S, "l0")
    h1, y0_t = _gate_out(ycat0, proj0, z0_col, W["w_out"][0], h0, "l0_out", carry=W.carry("l0_out"))

    g1 = row(W["norm_g"][1])
    (u1,) = _rowwise(_f_rms, [h1], [g1], [(D, BF16)], "l1_norm")
    proj1 = _mm(u1, W["mla_w_in"], "nn", F32, "l1_in")
    z1_col = Q_RANK
    qm1_col = z1_col + mix_w
    ckv_col = qm1_col + MEM_WIDTH
    kr_col = ckv_col + KV_RANK
    cq, ckv = (proj1, Q_RANK, 0), (proj1, KV_RANK, ckv_col // KV_RANK)
    qg, kvg = row(W["mla_q_norm_g"]), row(W["mla_kv_norm_g"])
    (cqn,) = _rowwise(_f_rms, [cq], [qg], [(Q_RANK, BF16)], "l1_qnorm")
    (ckvn,) = _rowwise(_f_rms, [ckv], [kvg], [(KV_RANK, BF16)], "l1_kvnorm")
    qf = _mm(cqn, W["mla_w_uq"], "nn", F32, "l1_uq")
    kvf = _mm(ckvn, W["mla_w_ukv"], "nn", BF16, "l1_ukv")
    cos_p, sin_p = _rope_tables(positions)
    qr, kr = _rowwise(_f_rope, [(qf, n_nope, 1), (proj1, 128, kr_col // 128), cos_p, sin_p], [],
                      [(n_nope, BF16), (128, BF16)], "l1_rope")
    scale1 = MLA_QK ** -0.5
    ycat1, lse1 = _attn_fwd(qf, 0, qr, kvf, 0, kr, kvf, 1, B, S, S, MLA_HEADS, True, scale1, "l1_attn",
                            o_width=mix_w, kv_stride=2)
    mg1 = row(W["mem_norm_g"][1])
    ycat1, mem_saved1 = _mem_attention_fwd(proj1, qm1_col, ycat1, mem2, mg1, W["w_mem_kv"][1], B, S, "l1")
    h2, y1_t = _gate_out(ycat1, proj1, z1_col, W["w_out"][1], h1, "l1_out")

    gf = row(W["final_norm_g"])
    dh2, d_gf, loss128 = _final_loss(h2, tgt, gf, "final_loss")
    G = {"final_norm_g": d_gf.reshape(-1)}
    L1 = {}

    d_wout1 = _mm(y1_t, dh2, "nn", F32, "l1_out_dw")
    d_ycat1, d_proj1 = _out_dx_gate_bwd(dh2, W["w_out"][1], ycat1, proj1, z1_col, "l1_out_dx")
    d_proj1, d_wmem1, d_mg1 = _mem_attention_bwd(proj1, qm1_col, ycat1, d_ycat1, d_proj1, mem_saved1, mem2, mg1,
                                                 W["w_mem_kv"][1], B, S, "l1")
    d_qf, d_qr, d_kvf, d_kr_heads = _attn_bwd(qf, 0, qr, kvf, 0, kr, kvf, 1, ycat1, d_ycat1, 0, lse1, B, S, S,
                                              MLA_HEADS, True, scale1, "l1_attn_bwd", kv_stride=2)
    d_qf, d_proj1 = _rowwise(_f_rope_t, [d_qr, d_kr_heads, cos_p, sin_p], [], [(n_nope, F32), (128, F32)], "l1_rope_bwd",
                             into=[(0, d_qf, 1), (1, d_proj1, kr_col // 128)])
    d_cqn = _mm(d_qf, W["mla_w_uq"], "nt", F32, "l1_uq_dx")
    L1[("mla_w_uq", None)] = _mm(cqn, d_qf, "tn", F32, "l1_uq_dw")
    d_ckvn = _mm(d_kvf, W["mla_w_ukv"], "nt", F32, "l1_ukv_dx")
    L1[("mla_w_ukv", None)] = _mm(ckvn, d_kvf, "tn", F32, "l1_ukv_dw")
    d_proj1, d_qg = _rowwise_bwd(_f_rms, [cq], [qg], [d_cqn], 1, "l1_qnorm_bwd", into=(d_proj1, cq[2]))
    d_proj1, d_kvg = _rowwise_bwd(_f_rms, [ckv], [kvg], [d_ckvn], 1, "l1_kvnorm_bwd", into=(d_proj1, ckv[2]))
    L1[("w_mem_kv", 1)] = d_wmem1
    L1[("mla_w_in", None)] = _mm(u1, d_proj1, "tn", F32, "l1_in_dw")
    L1[("w_out", 1)] = d_wout1
    W.ready("l1", L1)
    dh1, d_g1 = _in_dx_norm_bwd(d_proj1, W["mla_w_in"], h1, g1, dh2, "l1_in_dx", carry=W.carry("l1_in_dx"))

    d_wout0 = _mm(y0_t, dh1, "nn", F32, "l0_out_dw")
    d_ycat0, d_proj0 = _out_dx_gate_bwd(dh1, W["w_out"][0], ycat0, proj0, z0_col, "l0_out_dx", carry=W.carry("l0_out_dx"))
    d_proj0, d_wmem0, d_mg0 = _mem_attention_bwd(proj0, qm0_col, ycat0, d_ycat0, d_proj0, mem_saved0, mem2, mg0,
                                                 W["w_mem_kv"][0], B, S, "l0")
    W.ready("l0a", {("w_mem_kv", 0): d_wmem0, ("w_out", 0): d_wout0})
    d_cv, d_ln_g, d_ln_b = _rowwise_bwd(_f_ln_silu, [cv], [ln_g, ln_b], [(d_ycat0, conv_w, 0)], 1, "l0_ln_bwd",
                                        carry=W.carry("l0_ln_bwd"))
    d_glu, d_dw, d_dwb = _dwconv_bwd(proj0, dw, d_cv.reshape(B, S, conv_w), "l0_dwconv_bwd",
                                     carry=W.carry("l0_dwconv_bwd"))
    d_proj0 = _glu_bwd(proj0, d_glu.reshape(T, conv_w), d_proj0, "l0_glu_bwd")
    d_conv_w_in = _mm(u0, d_proj0, "tn", F32, "l0_in_dw", carry=W.carry("l0_in_dw"))
    W.ready("l0b", {("conv_w_in", None): d_conv_w_in, ("conv_dw", None): d_dw,
                    ("mla_q_norm_g", None): d_qg.reshape(-1), ("mla_kv_norm_g", None): d_kvg.reshape(-1)})
    dx, d_g0 = _in_dx_norm_bwd(d_proj0, W["conv_w_in"], h0, g0, dh1, "l0_in_dx", carry=W.carry("l0_in_dx"))
    dx = dx.reshape(B, S, D)

    G["norm_g"] = jnp.concatenate([d_g0, d_g1], axis=0)
    G["mem_norm_g"] = jnp.concatenate([d_mg0, d_mg1], axis=0)
    G["conv_dw_b"] = d_dwb
    G["conv_ln_g"], G["conv_ln_b"] = d_ln_g, d_ln_b
    return loss128[0, 0], dx, G


def _mla_in_perm(w):
    c1, c2 = Q_RANK, Q_RANK + KV_RANK
    c3 = c2 + MLA_ROPE
    c4 = c3 + MEM_WIDTH
    zero = jnp.zeros((w.shape[0], HALF_ROPE), w.dtype)
    return jnp.concatenate([w[:, :c1], w[:, c4:], w[:, c3:c4], w[:, c1:c2], w[:, c2:c2 + HALF_ROPE], zero,
                            w[:, c2 + HALF_ROPE:c3], zero], axis=1)


def _mla_in_unperm(g):
    z_w = g.shape[1] - (Q_RANK + MEM_WIDTH + KV_RANK + 128)
    z0, q0 = Q_RANK, Q_RANK + z_w
    k0 = q0 + MEM_WIDTH
    r = k0 + KV_RANK
    return jnp.concatenate([g[:, :Q_RANK], g[:, k0:r], g[:, r:r + HALF_ROPE], g[:, r + 64:r + 64 + HALF_ROPE],
                            g[:, q0:k0], g[:, z0:q0]], axis=1)


def _uq_perm(w):
    n = w.shape[0]
    w3 = w.reshape(n, MLA_HEADS, MLA_QK)
    zero = jnp.zeros((n, MLA_HEADS, HALF_ROPE), w.dtype)
    rope = jnp.concatenate([w3[:, :, MLA_NOPE:MLA_NOPE + HALF_ROPE], zero, w3[:, :, MLA_NOPE + HALF_ROPE:], zero], axis=2)
    return jnp.concatenate([w3[:, :, :MLA_NOPE].reshape(n, -1), rope.reshape(n, -1)], axis=1)


def _uq_unperm(g):
    n = g.shape[0]
    n_nope = MLA_HEADS * MLA_NOPE
    rope = g[:, n_nope:].reshape(n, MLA_HEADS, 128)
    return jnp.concatenate([g[:, :n_nope].reshape(n, MLA_HEADS, MLA_NOPE), rope[:, :, :HALF_ROPE],
                            rope[:, :, 64:64 + HALF_ROPE]], axis=2).reshape(n, -1)


_ROW_CUT = ("w_mem_kv", "w_out")
_COL_CUT = ("conv_w_in", "mla_w_in", "mla_w_uq", "mla_w_ukv", "conv_dw")
_BIG = ("w_mem_kv", "w_out", "conv_w_in", "mla_w_in", "mla_w_uq", "mla_w_ukv")
_SMALL_SHARDED = ("conv_dw", "mla_q_norm_g", "mla_kv_norm_g")
_REPLICATED = ("norm_g", "mem_norm_g", "conv_dw_b", "conv_ln_g", "conv_ln_b", "final_norm_g")
_PERM = {"mla_w_in": (_mla_in_perm, _mla_in_unperm), "mla_w_uq": (_uq_perm, _uq_unperm)}


def _join(n, blocks):
    if n in _ROW_CUT:
        _, L, r, c = blocks.shape
        return blocks.transpose(1, 0, 2, 3).reshape(L, N_DEV * r, c)
    if n in _COL_CUT:
        _, _, r, c = blocks.shape
        return blocks.reshape(N_DEV, r, c).transpose(1, 0, 2).reshape(r, N_DEV * c)
    return blocks.reshape(-1)


def _cut(n, full, shard_shape):
    if n in _ROW_CUT:
        L, r, c = shard_shape
        return full.reshape(L, N_DEV, r, c).transpose(1, 0, 2, 3)
    if n in _COL_CUT:
        _, r, c = shard_shape
        return full.reshape(r, N_DEV, c).transpose(1, 0, 2).reshape(N_DEV, 1, r, c)
    return full.reshape(N_DEV, 1, -1)


def _flat_pad(parts, size):
    flat = jnp.concatenate([p.reshape(-1) for p in parts])
    return jnp.concatenate([flat, jnp.zeros((size - flat.shape[0],), flat.dtype)])


SMALL_LANES = 128 * 8


def _as_tiles(flat_parts):
    total = sum(p.size for p in flat_parts)
    size = -(-total // SMALL_LANES) * SMALL_LANES
    return _flat_pad(flat_parts, size).reshape(8, size // 8)


def _split_flat(flat, like):
    out, o = [], 0
    for a in like:
        out.append(flat[o:o + a.size].reshape(a.shape))
        o += a.size
    return out


_HBM = pl.BlockSpec(memory_space=pltpu.HBM)
_VMEM = pl.BlockSpec(memory_space=pltpu.VMEM)


def _position():
    return lax.axis_index("x"), lax.axis_index("y"), lax.axis_index("c")


def _dma_sems(n):
    return [pltpu.SemaphoreType.DMA((n,)), pltpu.SemaphoreType.DMA((n,))]


def _run_stage(stage, name):
    n_in, n_out = len(stage.ins), len(stage.out_shapes)

    def body(*refs):
        ins, outs, sems = refs[:n_in], refs[n_in:n_in + n_out], refs[n_in + n_out:]
        stage.start(ins, outs, sems)
        stage.wait(ins, outs, sems)

    outs = _call(body, name, stage.out_shapes, in_specs=[_HBM] * n_in, out_specs=[_HBM] * n_out, scratch=stage.sems,
                 aliases=stage.aliases)(*stage.ins)
    stage.outs = list(outs)
    return stage.outs


def _gather_chips_stage(shards):
    n = len(shards)

    def copies(x_refs, out_refs, sems):
        send_sems, recv_sems, _ = sems
        x, y, c = _position()
        peers = [(x, y, 1 - c), (1 - x, y, c), (x, 1 - y, c), (1 - x, 1 - y, c)]
        out = []
        for a in range(n):
            for k, (px, py, pc) in enumerate(peers):
                send = pltpu.make_async_remote_copy(src_ref=x_refs[a], dst_ref=out_refs[a].at[4 * x + 2 * y + c],
                                                    send_sem=send_sems.at[4 * a + k], recv_sem=recv_sems.at[4 * a + k],
                                                    device_id=(px, py, pc), device_id_type=MESH)
                recv = pltpu.make_async_remote_copy(src_ref=x_refs[a], dst_ref=out_refs[a].at[4 * px + 2 * py + pc],
                                                    send_sem=send_sems.at[4 * a + k], recv_sem=recv_sems.at[4 * a + k],
                                                    device_id=(px, py, pc), device_id_type=MESH)
                out.append((send, recv))
        return out

    def local(x_refs, out_refs, sems):
        x, y, c = _position()
        return [pltpu.make_async_copy(x_refs[a], out_refs[a].at[4 * x + 2 * y + c], sems[2].at[a]) for a in range(n)]

    def start(x_refs, out_refs, sems):
        for cp in local(x_refs, out_refs, sems):
            cp.start()
        for send, _ in copies(x_refs, out_refs, sems):
            send.start()

    def wait(x_refs, out_refs, sems):
        for send, recv in copies(x_refs, out_refs, sems):
            recv.wait_recv()
            send.wait_send()
        for cp in local(x_refs, out_refs, sems):
            cp.wait()

    return _Stage(shards, [jax.ShapeDtypeStruct((N_DEV,) + a.shape, a.dtype) for a in shards],
                  _dma_sems(4 * n) + [pltpu.SemaphoreType.DMA((n,))], start, wait)


def _gather_sibling_stage(bufs):
    n = len(bufs)

    def copies(out_refs, sems):
        send_sems, recv_sems = sems
        x, y, c = _position()
        out = []
        for a in range(n):
            for j, (px, py) in enumerate([(1 - x, y), (x, 1 - y), (1 - x, 1 - y)]):
                mine, theirs = out_refs[a].at[4 * px + 2 * py + c], out_refs[a].at[4 * px + 2 * py + (1 - c)]
                send = pltpu.make_async_remote_copy(src_ref=mine, dst_ref=mine, send_sem=send_sems.at[3 * a + j],
                                                    recv_sem=recv_sems.at[3 * a + j], device_id=(x, y, 1 - c),
                                                    device_id_type=MESH)
                recv = pltpu.make_async_remote_copy(src_ref=mine, dst_ref=theirs, send_sem=send_sems.at[3 * a + j],
                                                    recv_sem=recv_sems.at[3 * a + j], device_id=(x, y, 1 - c),
                                                    device_id_type=MESH)
                out.append((send, recv))
        return out

    def start(_, out_refs, sems):
        for send, _r in copies(out_refs, sems):
            send.start()

    def wait(_, out_refs, sems):
        for send, recv in copies(out_refs, sems):
            recv.wait_recv()
            send.wait_send()

    return _Stage(bufs, [jax.ShapeDtypeStruct(b.shape, b.dtype) for b in bufs], _dma_sems(3 * n), start, wait,
                  aliases={a: a for a in range(n)})


def _all_gather_small(v, name):
    r, n = v.shape

    def body(x_ref, out_ref, send_sems, recv_sems, local_sem):
        x, y, c = _position()
        me = 4 * x + 2 * y + c
        mine = pltpu.make_async_copy(x_ref, out_ref.at[me], local_sem)
        mine.start()
        flips = [(fx, fy, fc) for fx in (0, 1) for fy in (0, 1) for fc in (0, 1)][1:]
        copies = []
        for k, (fx, fy, fc) in enumerate(flips):
            peer = (x ^ fx, y ^ fy, c ^ fc)
            cp = pltpu.make_async_remote_copy(src_ref=x_ref, dst_ref=out_ref.at[me], send_sem=send_sems.at[k],
                                              recv_sem=recv_sems.at[k], device_id=peer, device_id_type=MESH)
            cp.start()
            copies.append(cp)
        for k, (fx, fy, fc) in enumerate(flips):
            px, py, pc = x ^ fx, y ^ fy, c ^ fc
            src = out_ref.at[4 * px + 2 * py + pc]
            pltpu.make_async_remote_copy(src_ref=x_ref, dst_ref=src, send_sem=send_sems.at[k], recv_sem=recv_sems.at[k],
                                         device_id=(px, py, pc), device_id_type=MESH).wait_recv()
        for cp in copies:
            cp.wait_send()
        mine.wait()

    return _call(body, name, jax.ShapeDtypeStruct((N_DEV, r, n), v.dtype), in_specs=[_VMEM], out_specs=_VMEM,
                 scratch=_dma_sems(7) + [pltpu.SemaphoreType.DMA(())])(v)


def _reduce_sibling_stage(gs):
    n = len(gs)

    def copies(g_refs, out_refs, sems):
        send_sems, recv_sems = sems
        x, y, c = _position()
        return [pltpu.make_async_remote_copy(src_ref=g_refs[a].at[2 * k + (1 - c)], dst_ref=out_refs[a].at[k],
                                             send_sem=send_sems.at[4 * a + k], recv_sem=recv_sems.at[4 * a + k],
                                             device_id=(x, y, 1 - c), device_id_type=MESH)
                for a in range(n) for k in range(4)]

    def start(g_refs, out_refs, sems):
        for cp in copies(g_refs, out_refs, sems):
            cp.start()

    def wait(g_refs, out_refs, sems):
        for cp in copies(g_refs, out_refs, sems):
            cp.wait()

    return _Stage(gs, [jax.ShapeDtypeStruct((4,) + g.shape[1:], g.dtype) for g in gs], _dma_sems(4 * n), start, wait)


def _rows2d(shape):
    cols = shape[-1]
    rows = 1
    for s in shape[:-1]:
        rows *= s
    return rows, cols


def _add_own(g, recv, name):
    rows, cols = _rows2d(g.shape[1:])
    tr = _pick(rows, 256, 8)
    c = lax.axis_index("c").astype(jnp.int32).reshape(1)

    def body(c_ref, g_ref, r_ref, o_ref):
        o_ref[...] = (g_ref[...].astype(F32) + r_ref[...].astype(F32)).astype(o_ref.dtype)

    grid_spec = pltpu.PrefetchScalarGridSpec(
        num_scalar_prefetch=1, grid=(4, rows // tr),
        in_specs=[pl.BlockSpec((None, None, tr, cols), lambda k, i, c_ref: (k, c_ref[0], i, 0)),
                  pl.BlockSpec((None, tr, cols), lambda k, i, c_ref: (k, i, 0))],
        out_specs=pl.BlockSpec((None, tr, cols), lambda k, i, c_ref: (k, i, 0)))
    return _call(body, name, jax.ShapeDtypeStruct((4, rows, cols), g.dtype), grid_spec=grid_spec,
                 dims=("parallel", "parallel"))(c, g.reshape(4, 2, rows, cols), recv.reshape(4, rows, cols))


def _reduce_chips_stage(pas):
    n = len(pas)

    def copies(pa_refs, out_refs, sems):
        send_sems, recv_sems, _ = sems
        x, y, c = _position()
        my_chip = 2 * x + y
        out = []
        for a in range(n):
            for j, (px, py) in enumerate([(1 - x, y), (x, 1 - y), (1 - x, 1 - y)]):
                send = pltpu.make_async_remote_copy(src_ref=pa_refs[a].at[2 * px + py], dst_ref=out_refs[a].at[my_chip],
                                                    send_sem=send_sems.at[3 * a + j], recv_sem=recv_sems.at[3 * a + j],
                                                    device_id=(px, py, c), device_id_type=MESH)
                recv = pltpu.make_async_remote_copy(src_ref=pa_refs[a].at[2 * px + py], dst_ref=out_refs[a].at[2 * px + py],
                                                    send_sem=send_sems.at[3 * a + j], recv_sem=recv_sems.at[3 * a + j],
                                                    device_id=(px, py, c), device_id_type=MESH)
                out.append((send, recv))
        return out

    def local(pa_refs, out_refs, sems):
        x, y, _ = _position()
        return [pltpu.make_async_copy(pa_refs[a].at[2 * x + y], out_refs[a].at[2 * x + y], sems[2].at[a]) for a in range(n)]

    def start(pa_refs, out_refs, sems):
        for cp in local(pa_refs, out_refs, sems):
            cp.start()
        for send, _r in copies(pa_refs, out_refs, sems):
            send.start()

    def wait(pa_refs, out_refs, sems):
        for send, recv in copies(pa_refs, out_refs, sems):
            recv.wait_recv()
            send.wait_send()
        for cp in local(pa_refs, out_refs, sems):
            cp.wait()

    return _Stage(pas, [jax.ShapeDtypeStruct(pa.shape, pa.dtype) for pa in pas],
                  _dma_sems(3 * n) + [pltpu.SemaphoreType.DMA((n,))], start, wait)


def _adamw_math(w, g, m, v):
    m = ADAM_B1 * m + (1.0 - ADAM_B1) * g
    v = ADAM_B2 * v + (1.0 - ADAM_B2) * (g * g)
    m_hat = m / (1.0 - ADAM_B1 ** ADAM_STEP)
    v_hat = v / (1.0 - ADAM_B2 ** ADAM_STEP)
    delta = -ADAM_LR * (m_hat / (jnp.sqrt(v_hat) + ADAM_EPS) + ADAM_WD * w)
    return delta, m, v


def _sum_adamw(parts, w, m, v, name):
    n, rows, cols = parts.shape
    tr = _pick(rows, 128, 8)

    def body(p_ref, w_ref, m_ref, v_ref, g_ref, d_ref, nm_ref, nv_ref):
        g = p_ref[0].astype(F32)
        for k in range(1, n):
            g = g + p_ref[k].astype(F32)
        d, nm, nv = _adamw_math(w_ref[...], g, m_ref[...], v_ref[...])
        g_ref[...], d_ref[...], nm_ref[...], nv_ref[...] = g, d, nm, nv

    blk = pl.BlockSpec((tr, cols), lambda i: (i, 0))
    return _call(body, name, [jax.ShapeDtypeStruct((rows, cols), F32)] * 4, grid=(rows // tr,),
                 in_specs=[pl.BlockSpec((n, tr, cols), lambda i: (0, i, 0)), blk, blk, blk],
                 out_specs=[blk] * 4, dims=("parallel",))(parts, w, m, v)


_WEIGHTS = ("norm_g", "mem_norm_g", "w_mem_kv", "w_out", "conv_w_in", "conv_dw", "conv_dw_b", "conv_ln_g", "conv_ln_b",
            "mla_w_in", "mla_q_norm_g", "mla_w_uq", "mla_kv_norm_g", "mla_w_ukv", "final_norm_g")


_GATHER_GROUPS = {"a": ("conv_w_in",), "b": ("w_mem_kv", "w_out"), "c": ("mla_w_in", "mla_w_uq", "mla_w_ukv")}
_CARRIERS = {"l0_norm": ("gather chips", ("a",)), "l0_in": ("gather chips", ("b",)), "l0_dwconv": ("gather chips", ("c",)),
             "l0_ln": ("gather sibling", ("b",)), "l0_out": ("gather sibling", ("c",)),
             "l1_in_dx": ("reduce sibling", ("l1",)), "l0_ln_bwd": ("reduce sibling", ("l0a",)),
             "l0_out_dx": ("reduce chips", ("l1", 0, 2)), "l0_dwconv_bwd": ("reduce chips", ("l1", 2, 5)),
             "l0_in_dw": ("reduce chips", ("l0a",)), "l0_in_dx": ("reduce sibling alone, then chips", ("l0b",))}


class _Schedule:
    def __init__(self, w):
        self.w, self.full, self.gather, self.reduce = w, {}, {}, {}
        self.small = _as_tiles([w[n] for n in _SMALL_SHARDED])
        for n in _REPLICATED:
            self.full[n] = w[n]

    def carry(self, call):
        kind, (g, *part) = _CARRIERS[call]
        if kind == "gather chips":
            shards = [self.w[n].astype(BF16) for n in _GATHER_GROUPS[g]] + ([self.small] if g == "a" else [])
            self.gather[g] = [_gather_chips_stage(shards)]
            return self.gather[g][0]
        if kind == "gather sibling":
            self.gather[g].append(_gather_sibling_stage(self.gather[g][0].outs))
            return self.gather[g][1]
        r = self.reduce[g]
        if kind == "reduce sibling":
            r["sibling"] = _reduce_sibling_stage(r["cut"])
            return r["sibling"]
        if kind != "reduce chips":
            r["sibling"] = _reduce_sibling_stage(r["cut"])
            _run_stage(r["sibling"], "reduce_sibling_" + g)
        if "partial" not in r:
            r["partial"] = [_add_own(c, s, "reduce_add_%s_%d" % (g, i))
                            for i, (c, s) in enumerate(zip(r["cut"], r["sibling"].outs))]
        lo, hi = part if part else (0, len(r["keys"]))
        stage = _reduce_chips_stage(r["partial"][lo:hi])
        r.setdefault("chips", []).append((r["keys"][lo:hi], stage))
        return stage

    def __getitem__(self, name):
        if name not in self.full:
            g = "a" if name in _SMALL_SHARDED else [k for k, names in _GATHER_GROUPS.items() if name in names][0]
            if len(self.gather[g]) == 1:
                self.gather[g].append(_gather_sibling_stage(self.gather[g][0].outs))
                _run_stage(self.gather[g][1], "gather_sibling_" + g)
            for n, buf in zip(_GATHER_GROUPS[g], self.gather[g][1].outs):
                self.full[n] = _PERM[n][0](_join(n, buf)) if n in _PERM else _join(n, buf)
            if g == "a":
                small, o = self.gather[g][1].outs[-1].reshape(N_DEV, -1), 0
                for n in _SMALL_SHARDED:
                    self.full[n] = _join(n, small[:, o:o + self.w[n].size].reshape((N_DEV,) + self.w[n].shape))
                    o += self.w[n].size
        return self.full[name]

    def ready(self, group, grads, payload=BF16):
        keys, cut, small = [], [], []
        for (n, layer), g in grads.items():
            if n in _SMALL_SHARDED:
                small.append(_cut(n, g, self.w[n].shape).reshape(N_DEV, -1))
                continue
            keys.append((n, layer))
            if layer is not None:
                cut.append(g.reshape((N_DEV,) + self.w[n].shape[1:]).astype(payload))
            else:
                cut.append(_cut(n, _PERM[n][1](g) if n in _PERM else g, self.w[n].shape).astype(payload))
        if small:
            keys.append(("small", None))
            cut.append(jax.vmap(lambda r: _as_tiles([r]))(jnp.concatenate(small, axis=1)))
        self.reduce[group] = {"keys": keys, "cut": cut}

    def finish(self):
        out = {}
        for r in self.reduce.values():
            for keys, stage in r["chips"]:
                out.update(dict(zip(keys, stage.outs)))
        return out


def kernel(x, mem, positions, norm_g, mem_norm_g, w_mem_kv, w_out, conv_w_in, conv_dw, conv_dw_b, conv_ln_g, conv_ln_b, mla_w_in, mla_q_norm_g, mla_w_uq, mla_kv_norm_g, mla_w_ukv, final_norm_g, loss_target, m_norm_g, m_mem_norm_g, m_w_mem_kv, m_w_out, m_conv_w_in, m_conv_dw, m_conv_dw_b, m_conv_ln_g, m_conv_ln_b, m_mla_w_in, m_mla_q_norm_g, m_mla_w_uq, m_mla_kv_norm_g, m_mla_w_ukv, m_final_norm_g, v_norm_g, v_mem_norm_g, v_w_mem_kv, v_w_out, v_conv_w_in, v_conv_dw, v_conv_dw_b, v_conv_ln_g, v_conv_ln_b, v_mla_w_in, v_mla_q_norm_g, v_mla_w_uq, v_mla_kv_norm_g, v_mla_w_ukv, v_final_norm_g):
    w = dict(zip(_WEIGHTS, (norm_g, mem_norm_g, w_mem_kv, w_out, conv_w_in, conv_dw, conv_dw_b, conv_ln_g, conv_ln_b,
                            mla_w_in, mla_q_norm_g, mla_w_uq, mla_kv_norm_g, mla_w_ukv, final_norm_g)))
    m = dict(zip(_WEIGHTS, (m_norm_g, m_mem_norm_g, m_w_mem_kv, m_w_out, m_conv_w_in, m_conv_dw, m_conv_dw_b, m_conv_ln_g,
                            m_conv_ln_b, m_mla_w_in, m_mla_q_norm_g, m_mla_w_uq, m_mla_kv_norm_g, m_mla_w_ukv, m_final_norm_g)))
    v = dict(zip(_WEIGHTS, (v_norm_g, v_mem_norm_g, v_w_mem_kv, v_w_out, v_conv_w_in, v_conv_dw, v_conv_dw_b, v_conv_ln_g,
                            v_conv_ln_b, v_mla_w_in, v_mla_q_norm_g, v_mla_w_uq, v_mla_kv_norm_g, v_mla_w_ukv, v_final_norm_g)))

    sched = _Schedule(w)
    loss_local, dx, G = _forward_backward(x, mem, positions, loss_target, sched)
    loss = lax.psum(loss_local, ("x", "y", "c"))

    from_chips = sched.finish()
    out = [{}, {}, {}, {}]
    for n in _BIG:
        if n in _ROW_CUT:
            res = [_sum_adamw(from_chips[(n, l)], w[n][l], m[n][l], v[n][l], "adamw_%s_%d" % (n, l)) for l in range(w[n].shape[0])]
            res = [jnp.stack(r) for r in zip(*res)]
        else:
            rows, cols = _rows2d(w[n].shape)
            res = _sum_adamw(from_chips[(n, None)], w[n].reshape(rows, cols), m[n].reshape(rows, cols),
                             v[n].reshape(rows, cols), "adamw_" + n)
        for o, r in zip(out, res):
            o[n] = r.reshape(w[n].shape)
    small_like = [w[n] for n in _SMALL_SHARDED]
    res = _sum_adamw(from_chips[("small", None)], _as_tiles(small_like), _as_tiles([m[n] for n in _SMALL_SHARDED]),
                     _as_tiles([v[n] for n in _SMALL_SHARDED]), "adamw_small")
    for o, r in zip(out, res):
        for n, a in zip(_SMALL_SHARDED, _split_flat(r.reshape(-1), small_like)):
            o[n] = a

    rep_like = [w[n] for n in _REPLICATED]
    rep_parts = _all_gather_small(_as_tiles([G[n] for n in _REPLICATED]), "gather_replicated_grads")
    res = _sum_adamw(rep_parts, _as_tiles(rep_like), _as_tiles([m[n] for n in _REPLICATED]),
                     _as_tiles([v[n] for n in _REPLICATED]), "adamw_replicated")
    for o, r in zip(out, res):
        for n, a in zip(_REPLICATED, _split_flat(r.reshape(-1), rep_like)):
            o[n] = a

    return (loss, dx, *[out[0][n] for n in _WEIGHTS], *[out[1][n] for n in _WEIGHTS],
            *[out[2][n] for n in _WEIGHTS], *[out[3][n] for n in _WEIGHTS])
```

```python
import jax
import jax.numpy as jnp
from jax import lax
from jax.experimental import pallas as pl
from jax.experimental.pallas import tpu as pltpu

F32 = jnp.float32
BF16 = jnp.bfloat16
MESH = pl.DeviceIdType.MESH
N_DEV = 8
VMEM_LIMIT_BYTES = 48 * 1024 * 1024

MEM_HEADS, MEM_HEAD_DIM = 4, 128
MEM_WIDTH = MEM_HEADS * MEM_HEAD_DIM
CONV_KERNEL = 31
CONV_PAD = 32
MLA_HEADS, MLA_NOPE, MLA_ROPE = 12, 128, 64
MLA_QK = MLA_NOPE + MLA_ROPE
HALF_ROPE = MLA_ROPE // 2
Q_RANK, KV_RANK = 512, 256
ROPE_THETA = 10000.0
RMS_EPS = 1e-6
LN_EPS = 1e-5
ADAM_LR, ADAM_B1, ADAM_B2, ADAM_EPS, ADAM_WD, ADAM_STEP = 0.001, 0.9, 0.999, 1e-08, 0.01, 10
NEG = -1e30


class _Stage:
    def __init__(self, ins, out_shapes, sems, start, wait, aliases=None):
        self.ins, self.out_shapes, self.sems = list(ins), list(out_shapes), list(sems)
        self.start, self.wait, self.aliases, self.outs = start, wait, dict(aliases or {}), None


def _call(body, name, out_shape, grid=None, in_specs=None, out_specs=None, scratch=(), dims=None, grid_spec=None, aliases=None,
          carry=None):
    params = dict(vmem_limit_bytes=VMEM_LIMIT_BYTES)
    if dims is not None:
        params["dimension_semantics"] = dims
    kw = {}
    if carry is not None:
        single = not isinstance(out_shape, (list, tuple))
        main_out = [out_shape] if single else list(out_shape)
        main_specs = [out_specs] if single else list(out_specs)
        n_in, n_out, n_scr = len(in_specs), len(main_out), len(scratch)
        x_in, x_out = len(carry.ins), len(carry.out_shapes)
        inner, steps = body, tuple(grid)

        def body(*refs):
            ins, xin = refs[:n_in], refs[n_in:n_in + x_in]
            outs = refs[n_in + x_in:n_in + x_in + n_out]
            xout = refs[n_in + x_in + n_out:n_in + x_in + n_out + x_out]
            scr = refs[n_in + x_in + n_out + x_out:n_in + x_in + n_out + x_out + n_scr]
            xsem = refs[n_in + x_in + n_out + x_out + n_scr:]
            ids = [pl.program_id(a) for a in range(len(steps))]
            first, last = ids[0] == 0, ids[0] == steps[0] - 1
            for a in range(1, len(steps)):
                first = jnp.logical_and(first, ids[a] == 0)
                last = jnp.logical_and(last, ids[a] == steps[a] - 1)
            pl.when(first)(lambda: carry.start(xin, xout, xsem))
            inner(*ins, *outs, *scr)
            pl.when(last)(lambda: carry.wait(xin, xout, xsem))

        hbm = pl.BlockSpec(memory_space=pltpu.HBM)
        aliases = dict(aliases or {})
        aliases.update({n_in + k: n_out + v for k, v in carry.aliases.items()})
        res = _call(body, name, main_out + carry.out_shapes, grid=grid, in_specs=list(in_specs) + [hbm] * x_in,
                    out_specs=main_specs + [hbm] * x_out, scratch=list(scratch) + carry.sems, dims=dims, aliases=aliases)

        def run(*args):
            outs = res(*args, *carry.ins)
            carry.outs = list(outs[n_out:])
            return outs[0] if single else outs[:n_out]

        return run
    if aliases:
        kw["input_output_aliases"] = aliases
    if grid_spec is not None:
        kw["grid_spec"] = grid_spec
    else:
        if grid is not None:
            kw["grid"] = grid
        kw["in_specs"] = in_specs
        kw["out_specs"] = out_specs
        kw["scratch_shapes"] = list(scratch)
    return pl.pallas_call(body, name=name, out_shape=out_shape, compiler_params=pltpu.CompilerParams(**params), **kw)


def _pick(n, target, mult):
    best = None
    for d in range(mult, min(n, target) + 1, mult):
        if n % d == 0:
            best = d
    return n if best is None else best


_DOT_DIMS = {"nn": (((1,), (0,)), ((), ())), "nt": (((1,), (1,)), ((), ())), "tn": (((0,), (0,)), ((), ()))}


def _mm(a, b, mode, out_dtype, name, res=None, carry=None):
    if mode == "tn":
        a, mode = a.T, "nn"
    if mode == "nn":
        (M, K), N = a.shape, b.shape[1]
    else:
        (M, K), N = a.shape, b.shape[0]
    tm = _pick(M, 1024, 8)
    tn = _pick(N, 1536, 128)
    tk = _pick(K, 1536, 128)
    nk = K // tk
    has_res = res is not None

    def body(*refs):
        if has_res:
            a_ref, b_ref, r_ref, o_ref, acc = refs
        else:
            a_ref, b_ref, o_ref, acc = refs
        k = pl.program_id(2)
        part = lax.dot_general(a_ref[...].astype(BF16), b_ref[...].astype(BF16), _DOT_DIMS[mode],
                               preferred_element_type=F32)
        if nk == 1:
            o_ref[...] = (part + r_ref[...] if has_res else part).astype(o_ref.dtype)
            return

        @pl.when(k == 0)
        def _():
            acc[...] = part

        @pl.when(k > 0)
        def _():
            acc[...] += part

        @pl.when(k == nk - 1)
        def _():
            r = acc[...]
            if has_res:
                r = r + r_ref[...]
            o_ref[...] = r.astype(o_ref.dtype)

    a_spec = pl.BlockSpec((tm, tk), lambda i, j, k: (i, k))
    b_spec = {"nn": pl.BlockSpec((tk, tn), lambda i, j, k: (k, j)),
              "nt": pl.BlockSpec((tn, tk), lambda i, j, k: (j, k))}[mode]
    o_spec = pl.BlockSpec((tm, tn), lambda i, j, k: (i, j))
    in_specs = [a_spec, b_spec] + ([o_spec] if has_res else [])
    args = (a, b) + ((res,) if has_res else ())
    return _call(body, name, jax.ShapeDtypeStruct((M, N), out_dtype), grid=(M // tm, N // tn, nk),
                 in_specs=in_specs, out_specs=o_spec, scratch=[pltpu.VMEM((tm, tn), F32)],
                 dims=("parallel", "parallel", "arbitrary"), carry=carry)(*args)


def _views(rows):
    return [r if isinstance(r, tuple) else (r, r.shape[1], 0) for r in rows]


def _row_tile(T, rows):
    return min(T, 512 if max(w for _, w, _ in rows) <= 1024 else 256)


def _rowwise(f, rows, params, outs, name, carry=None, into=None):
    rows = _views(rows)
    T = rows[0][0].shape[0]
    tb = _row_tile(T, rows)
    nr, npar = len(rows), len(params)
    outs = [o if len(o) == 3 else (o[0], o[1], o[0]) for o in outs]
    into = into or []

    def body(*refs):
        vals = f(*[r[...].astype(F32) for r in refs[:nr]], *[p[...] for p in refs[nr:nr + npar]])
        for o_ref, v in zip(refs[nr + npar + len(into):], vals):
            o_ref[...] = v.astype(o_ref.dtype)

    row_spec = lambda w, cb=0: pl.BlockSpec((tb, w), lambda i: (i, cb))
    par_spec = lambda w: pl.BlockSpec((1, w), lambda i: (0, 0))
    out_shape = [jax.ShapeDtypeStruct((T, tw), dt) for _, dt, tw in outs]
    out_specs = [row_spec(w) for w, _, _ in outs]
    in_specs = [row_spec(w, cb) for _, w, cb in rows] + [par_spec(p.shape[1]) for p in params]
    args = [r[0] for r in rows] + list(params)
    aliases = {}
    for k, arr, cb in into:
        aliases[len(args)] = k
        in_specs.append(pl.BlockSpec(memory_space=pl.ANY))
        args.append(arr)
        out_shape[k] = jax.ShapeDtypeStruct(arr.shape, arr.dtype)
        out_specs[k] = row_spec(outs[k][0], cb)
    return _call(body, name, out_shape, grid=(T // tb,), in_specs=in_specs, out_specs=out_specs, dims=("parallel",),
                 carry=carry, aliases=aliases)(*args)


def _rowwise_bwd(f, rows, params, douts, n_diff, name, carry=None, into=None):
    rows, douts = _views(rows), _views(douts)
    T = rows[0][0].shape[0]
    tb = _row_tile(T, rows)
    nr, npar, nd = len(rows), len(params), len(douts)

    def body(*refs):
        rv = [r[...].astype(F32) for r in refs[:nr]]
        pv = [p[...] for p in refs[nr:nr + npar]]
        dv = [d[...].astype(F32) for d in refs[nr + npar:nr + npar + nd]]
        o_refs = refs[nr + npar + nd + (0 if into is None else 1):]
        fixed = rv[n_diff:]

        def g(*xs):
            return tuple(f(*xs[:n_diff], *fixed, *xs[n_diff:]))

        _, vjp = jax.vjp(g, *rv[:n_diff], *pv)
        grads = vjp(tuple(dv))
        for o_ref, gr in zip(o_refs[:n_diff], grads[:n_diff]):
            o_ref[...] = gr.astype(o_ref.dtype)
        first = pl.program_id(0) == 0
        for o_ref, gr in zip(o_refs[n_diff:], grads[n_diff:]):
            @pl.when(first)
            def _(o_ref=o_ref):
                o_ref[...] = jnp.zeros_like(o_ref)

            o_ref[...] += gr

    row_spec = lambda w, cb=0: pl.BlockSpec((tb, w), lambda i: (i, cb))
    par_spec = lambda w: pl.BlockSpec((1, w), lambda i: (0, 0))
    out_shape = ([jax.ShapeDtypeStruct((T, w), F32) for _, w, _ in rows[:n_diff]]
                 + [jax.ShapeDtypeStruct((1, p.shape[1]), F32) for p in params])
    out_specs = [row_spec(w) for _, w, _ in rows[:n_diff]] + [par_spec(p.shape[1]) for p in params]
    in_specs = ([row_spec(w, cb) for _, w, cb in rows] + [par_spec(p.shape[1]) for p in params]
                + [row_spec(w, cb) for _, w, cb in douts])
    args = [r[0] for r in rows] + list(params) + [d[0] for d in douts]
    aliases = None
    if into is not None:
        aliases = {len(args): 0}
        in_specs.append(pl.BlockSpec(memory_space=pl.ANY))
        args.append(into[0])
        out_shape[0] = jax.ShapeDtypeStruct(into[0].shape, into[0].dtype)
        out_specs[0] = row_spec(rows[0][1], into[1])
    return _call(body, name, out_shape, grid=(T // tb,), in_specs=in_specs, out_specs=out_specs,
                 dims=("arbitrary",), carry=carry, aliases=aliases)(*args)


def _sig(x):
    return 1.0 / (1.0 + jnp.exp(-x))


def _rms(x, g):
    return x * lax.rsqrt(jnp.mean(x * x, axis=-1, keepdims=True) + RMS_EPS) * g


def _f_rms(x, g):
    return (_rms(x, g),)


def _f_ln_silu(x, g, b):
    mu = jnp.mean(x, axis=-1, keepdims=True)
    xc = x - mu
    var = jnp.mean(xc * xc, axis=-1, keepdims=True)
    y = xc * lax.rsqrt(var + LN_EPS) * g + b
    return (y * _sig(y),)


def _rope128(x, cos_p, sin_p):
    return x * cos_p + pltpu.roll(x, 64, 1) * sin_p


def _rope128_t(d, cos_p, sin_p):
    return d * cos_p + pltpu.roll(d * sin_p, 64, 1)


def _f_rope(xq, xk, cos_p, sin_p):
    heads = [_rope128(xq[:, h * 128:(h + 1) * 128], cos_p, sin_p) for h in range(MLA_HEADS)]
    return (jnp.concatenate(heads, axis=1), _rope128(xk, cos_p, sin_p))


def _f_rope_t(dq, dk_heads, cos_p, sin_p):
    heads = [_rope128_t(dq[:, h * 128:(h + 1) * 128], cos_p, sin_p) for h in range(MLA_HEADS)]
    dk = dk_heads[:, 0:128]
    for h in range(1, MLA_HEADS):
        dk = dk + dk_heads[:, h * 128:(h + 1) * 128]
    return (jnp.concatenate(heads, axis=1), _rope128_t(dk, cos_p, sin_p))


GATE_LANES = 512


def _gate_out(ycat, proj, z_col, w_out, res, name, tb=1024, carry=None):
    T, width = ycat.shape
    D = w_out.shape[1]
    zb = z_col // GATE_LANES
    nk = width // GATE_LANES

    def body(y_ref, z_ref, w_ref, r_ref, o_ref, yt_ref, acc):
        k = pl.program_id(1)
        z = z_ref[...]
        y = y_ref[...] * (z * _sig(z))
        yt_ref[...] = y.T.astype(yt_ref.dtype)
        part = jnp.dot(y.astype(BF16), w_ref[...], preferred_element_type=F32)

        @pl.when(k == 0)
        def _():
            acc[...] = part

        @pl.when(k > 0)
        def _():
            acc[...] += part

        @pl.when(k == nk - 1)
        def _():
            o_ref[...] = acc[...] + r_ref[...]

    row = pl.BlockSpec((tb, D), lambda i, k: (i, 0))
    return _call(body, name, [jax.ShapeDtypeStruct((T, D), F32), jax.ShapeDtypeStruct((width, T), BF16)],
                 grid=(T // tb, nk),
                 in_specs=[pl.BlockSpec((tb, GATE_LANES), lambda i, k: (i, k)),
                           pl.BlockSpec((tb, GATE_LANES), lambda i, k: (i, zb + k)),
                           pl.BlockSpec((GATE_LANES, D), lambda i, k: (k, 0)), row],
                 out_specs=[row, pl.BlockSpec((GATE_LANES, tb), lambda i, k: (k, i))],
                 scratch=[pltpu.VMEM((tb, D), F32)], dims=("parallel", "arbitrary"), carry=carry)(ycat, proj, w_out, res)


def _out_dx_gate_bwd(dh, w_out, ycat, proj, z_col, name, tb=1024, carry=None):
    T, width = ycat.shape
    D = dh.shape[1]
    zb = z_col // GATE_LANES

    def body(dh_ref, w_ref, y_ref, z_ref, dycat_ref, dz_ref):
        d = lax.dot_general(dh_ref[...].astype(BF16), w_ref[...], _DOT_DIMS["nt"], preferred_element_type=F32)
        z = z_ref[...]
        s = _sig(z)
        dycat_ref[...] = d * (z * s)
        dz_ref[...] = (d * y_ref[...] * (s * (1.0 + z * (1.0 - s)))).astype(dz_ref.dtype)

    blk = pl.BlockSpec((tb, GATE_LANES), lambda i, c: (i, c))
    zblk = pl.BlockSpec((tb, GATE_LANES), lambda i, c: (i, zb + c))
    return _call(body, name, [jax.ShapeDtypeStruct((T, width), F32), jax.ShapeDtypeStruct(proj.shape, BF16)],
                 grid=(T // tb, width // GATE_LANES),
                 in_specs=[pl.BlockSpec((tb, D), lambda i, c: (i, 0)), pl.BlockSpec((GATE_LANES, D), lambda i, c: (c, 0)),
                           blk, zblk],
                 out_specs=[blk, zblk], dims=("parallel", "parallel"), carry=carry)(dh, w_out, ycat, proj)


def _in_dx_norm_bwd(d_proj, w_in, h, g, add, name, tm=512, carry=None):
    T, K = d_proj.shape
    D = w_in.shape[0]
    tk = _pick(K, 1536, 128)
    nk = K // tk

    def body(a_ref, b_ref, h_ref, g_ref, add_ref, dx_ref, dg_ref, acc):
        i, k = pl.program_id(0), pl.program_id(1)
        part = lax.dot_general(a_ref[...], b_ref[...], _DOT_DIMS["nt"], preferred_element_type=F32)

        @pl.when(jnp.logical_and(i == 0, k == 0))
        def _():
            dg_ref[...] = jnp.zeros_like(dg_ref)

        @pl.when(k == 0)
        def _():
            acc[...] = part

        @pl.when(k > 0)
        def _():
            acc[...] += part

        @pl.when(k == nk - 1)
        def _():
            _, vjp = jax.vjp(_rms, h_ref[...], g_ref[...])
            dh, dg = vjp(acc[...])
            dx_ref[...] = dh + add_ref[...]
            dg_ref[...] += dg

    row = pl.BlockSpec((tm, D), lambda i, k: (i, 0))
    par = pl.BlockSpec((1, D), lambda i, k: (0, 0))
    return _call(body, name, [jax.ShapeDtypeStruct((T, D), F32), jax.ShapeDtypeStruct((1, D), F32)], grid=(T // tm, nk),
                 in_specs=[pl.BlockSpec((tm, tk), lambda i, k: (i, k)), pl.BlockSpec((D, tk), lambda i, k: (0, k)),
                           row, par, row],
                 out_specs=[row, par], scratch=[pltpu.VMEM((tm, D), F32)], dims=("arbitrary", "arbitrary"),
                 carry=carry)(d_proj, w_in, h, g, add)


def _glu_bwd(proj, d_glu, d_proj, name, tb=256):
    T, w = d_glu.shape

    def body(a_ref, g_ref, d_ref, _, o_ref):
        s, d = _sig(g_ref[...]), d_ref[...]
        o_ref[:, 0:w] = (d * s).astype(o_ref.dtype)
        o_ref[:, w:2 * w] = (d * a_ref[...] * (s * (1.0 - s))).astype(o_ref.dtype)

    return _call(body, name, jax.ShapeDtypeStruct(d_proj.shape, d_proj.dtype), grid=(T // tb,),
                 in_specs=[pl.BlockSpec((tb, w), lambda i: (i, 0)), pl.BlockSpec((tb, w), lambda i: (i, 1)),
                           pl.BlockSpec((tb, w), lambda i: (i, 0)), pl.BlockSpec(memory_space=pl.ANY)],
                 out_specs=pl.BlockSpec((tb, 2 * w), lambda i: (i, 0)), dims=("parallel",),
                 aliases={3: 0})(proj, proj, d_glu, d_proj)


def _final_loss(h, tgt, g, name, tb=512):
    T, D = h.shape

    def body(h_ref, t_ref, g_ref, dh_ref, dg_ref, loss_ref):
        tv = t_ref[...]

        def rowloss(hh, gg):
            e = _rms(hh, gg) - tv
            return 0.5 * jnp.mean(e * e, axis=-1, keepdims=True)

        lr, vjp = jax.vjp(rowloss, h_ref[...], g_ref[...])
        dh, dg = vjp(jnp.ones_like(lr))
        dh_ref[...] = dh

        @pl.when(pl.program_id(0) == 0)
        def _():
            dg_ref[...] = jnp.zeros_like(dg_ref)
            loss_ref[...] = jnp.zeros_like(loss_ref)

        dg_ref[...] += dg
        loss_ref[...] += jnp.broadcast_to(jnp.sum(lr, axis=0, keepdims=True), loss_ref.shape)

    row = pl.BlockSpec((tb, D), lambda i: (i, 0))
    par = pl.BlockSpec((1, D), lambda i: (0, 0))
    return _call(body, name,
                 [jax.ShapeDtypeStruct((T, D), F32), jax.ShapeDtypeStruct((1, D), F32), jax.ShapeDtypeStruct((1, 128), F32)],
                 grid=(T // tb,), in_specs=[row, row, par],
                 out_specs=[row, par, pl.BlockSpec((1, 128), lambda i: (0, 0))], dims=("arbitrary",))(h, tgt, g)


CONV_ROWS = 128
CONV_LANES = 256


def _sublane_phases(pad, n):
    for r in range(1, 8):
        for c0 in range(0, n - 8, 256):
            rows = min(256, n - 8 - c0)
            pad[r, c0:c0 + rows, :] = pad[0, c0 + r:c0 + r + rows, :]


def _dwconv_fwd(proj, C, w, b, B, S, name, carry=None):
    cb = CONV_LANES
    off = CONV_PAD - (CONV_KERNEL - 1)

    def body(a_ref, g_ref, w_ref, b_ref, o_ref, pad):
        pad[0, 0:CONV_PAD, :] = jnp.zeros((CONV_PAD, cb), F32)
        for c0 in range(0, S, 256):
            pad[0, CONV_PAD + c0:CONV_PAD + c0 + 256, :] = a_ref[c0:c0 + 256, :] * _sig(g_ref[c0:c0 + 256, :])
        _sublane_phases(pad, S + CONV_PAD)
        for t0 in range(0, S, CONV_ROWS):
            acc = jnp.broadcast_to(b_ref[...], (CONV_ROWS, cb))
            for k in range(CONV_KERNEL):
                r, base = (off + k) % 8, t0 + (off + k) // 8 * 8
                acc = acc + w_ref[k:k + 1, :] * pad[r, base:base + CONV_ROWS, :]
            o_ref[t0:t0 + CONV_ROWS, :] = acc

    return _call(body, name, jax.ShapeDtypeStruct((B, S, C), F32), grid=(B, C // cb),
                 in_specs=[pl.BlockSpec((S, cb), lambda i, j: (i, j)), pl.BlockSpec((S, cb), lambda i, j: (i, C // cb + j)),
                           pl.BlockSpec((CONV_KERNEL, cb), lambda i, j: (0, j)),
                           pl.BlockSpec((1, cb), lambda i, j: (0, j))],
                 out_specs=pl.BlockSpec((None, S, cb), lambda i, j: (i, 0, j)),
                 scratch=[pltpu.VMEM((8, S + CONV_PAD, cb), F32)], dims=("parallel", "parallel"),
                 carry=carry)(proj, proj, w, b)


def _dwconv_bwd(proj, w, dy, name, carry=None):
    B, S, C = dy.shape
    cb = CONV_LANES
    groups = CONV_ROWS // 8

    def body(a_ref, g_ref, w_ref, dy_ref, dx_ref, dw_ref, db_ref, dypad, wacc):
        dypad[0, 0:S, :] = dy_ref[...]
        dypad[0, S:, :] = jnp.zeros((CONV_PAD, cb), F32)
        _sublane_phases(dypad, S + CONV_PAD)
        wacc[...] = jnp.zeros_like(wacc)
        for t0 in range(0, S, CONV_ROWS):
            xc = a_ref[t0:t0 + CONV_ROWS, :] * _sig(g_ref[t0:t0 + CONV_ROWS, :])
            acc = jnp.zeros((CONV_ROWS, cb), F32)
            for k in range(CONV_KERNEL):
                o = (CONV_KERNEL - 1) - k
                dys = dypad[o % 8, t0 + o // 8 * 8:t0 + o // 8 * 8 + CONV_ROWS, :]
                acc = acc + w_ref[k:k + 1, :] * dys
                wacc[k] += jnp.sum((dys * xc).reshape(groups, 8, cb), axis=0)
            wacc[CONV_KERNEL] += jnp.sum(dy_ref[t0:t0 + CONV_ROWS, :].reshape(groups, 8, cb), axis=0)
            dx_ref[t0:t0 + CONV_ROWS, :] = acc

        @pl.when(pl.program_id(1) == 0)
        def _():
            dw_ref[...] = jnp.zeros_like(dw_ref)
            db_ref[...] = jnp.zeros_like(db_ref)

        for k in range(CONV_KERNEL):
            dw_ref[k:k + 1, :] += jnp.sum(wacc[k], axis=0, keepdims=True)
        db_ref[...] += jnp.sum(wacc[CONV_KERNEL], axis=0, keepdims=True)

    blk = pl.BlockSpec((None, S, cb), lambda j, i: (i, 0, j))
    return _call(body, name,
                 [jax.ShapeDtypeStruct((B, S, C), F32), jax.ShapeDtypeStruct((CONV_KERNEL, C), F32),
                  jax.ShapeDtypeStruct((1, C), F32)],
                 grid=(C // cb, B),
                 in_specs=[pl.BlockSpec((S, cb), lambda j, i: (i, j)), pl.BlockSpec((S, cb), lambda j, i: (i, C // cb + j)),
                           pl.BlockSpec((CONV_KERNEL, cb), lambda j, i: (0, j)), blk],
                 out_specs=[blk, pl.BlockSpec((CONV_KERNEL, cb), lambda j, i: (0, j)),
                            pl.BlockSpec((1, cb), lambda j, i: (0, j))],
                 scratch=[pltpu.VMEM((8, S + CONV_PAD, cb), F32), pltpu.VMEM((CONV_KERNEL + 1, 8, cb), F32)],
                 dims=("parallel", "arbitrary"), carry=carry)(proj, proj, w, dy)


ATTN_TILE = {"fwd": 1024, "bwd": 1024, "cross fwd": 512}
ATTN_SUB = {"fwd": 256, "bwd": 512}
ATTN_KEYS = 512


def _attn_shapes(Sq, Sk, causal, pass_):
    tq = min(Sq, ATTN_TILE[pass_ if causal or pass_ == "bwd" else "cross fwd"])
    tk = tq if causal else min(Sk, ATTN_TILE[pass_])
    return tq, tk, min(ATTN_SUB[pass_], tq)


def _causal_bias(n):
    r = lax.broadcasted_iota(jnp.int32, (n, n), 0)
    c = lax.broadcasted_iota(jnp.int32, (n, n), 1)
    return jnp.where(c <= r, 0.0, NEG).astype(F32)


def _mask_diagonal(s, bias):
    n, nc = s.shape
    if nc == n:
        return s + bias
    return jnp.concatenate([s[:, :nc - n], s[:, nc - n:] + bias], axis=1)


def _attn_fwd(q, q_c0, qr, k, k_c0, kr, v, v_c0, B, Sq, Sk, H, causal, scale, name, into=None, o_c0=0, o_width=None,
              kv_stride=1):
    tq, tk, sub = _attn_shapes(Sq, Sk, causal, "fwd")
    nq, nk, nsub = Sq // tq, Sk // tk, tq // sub
    rope = qr is not None

    def body(*refs):
        refs = list(refs)
        qn_ref = refs.pop(0)
        qr_ref = refs.pop(0) if rope else None
        kn_ref = refs.pop(0)
        kr_ref = refs.pop(0) if rope else None
        v_ref = refs.pop(0)
        if into is not None:
            refs.pop(0)
        o_ref, lse_ref, m_s, l_s, acc = refs
        qi = pl.program_id(2)
        m_s[...] = jnp.full_like(m_s, NEG)
        l_s[...] = jnp.zeros_like(l_s)
        acc[...] = jnp.zeros_like(acc)
        bias = _causal_bias(sub) if causal else None
        qs = []
        for r in range(nsub):
            qn = qn_ref[r * sub:(r + 1) * sub, :].astype(BF16)
            qs.append(jnp.concatenate([qn, qr_ref[r * sub:(r + 1) * sub, :]], axis=1) if rope else qn)

        def step(j, masked):
            ks = pl.ds(pl.multiple_of(j * tk, tk), tk)
            kk = jnp.concatenate([kn_ref[ks, :], kr_ref[ks, :]], axis=1) if rope else kn_ref[ks, :]
            vv = v_ref[ks, :]
            for r in range(nsub):
                rows = slice(r * sub, (r + 1) * sub)
                nc = (r + 1) * sub if masked else tk
                for c0 in range(0, nc, ATTN_KEYS):
                    c1 = min(nc, c0 + ATTN_KEYS)
                    s = lax.dot_general(qs[r], kk[c0:c1], _DOT_DIMS["nt"], preferred_element_type=F32) * scale
                    if masked and c1 == nc:
                        s = _mask_diagonal(s, bias)
                    m_old = m_s[rows, :]
                    m_new = jnp.maximum(m_old, jnp.max(s, axis=-1, keepdims=True))
                    p = jnp.exp(s - m_new)
                    alpha = jnp.exp(m_old - m_new)
                    l_s[rows, :] = alpha * l_s[rows, :] + jnp.sum(p, axis=-1, keepdims=True)
                    acc[rows, :] = alpha * acc[rows, :] + jnp.dot(p.astype(BF16), vv[c0:c1], preferred_element_type=F32)
                    m_s[rows, :] = m_new

        def unmasked(j, carry):
            step(j, False)
            return carry

        if causal:
            lax.fori_loop(0, qi, unmasked, 0)
            step(qi, True)
        else:
            lax.fori_loop(0, nk, unmasked, 0)
        o_ref[...] = (acc[...] / l_s[...]).astype(o_ref.dtype)
        lse_ref[...] = m_s[...] + jnp.log(l_s[...])

    qspec = lambda c0: pl.BlockSpec((tq, 128), lambda b, h, i: (b * nq + i, c0 + h))
    kspec = lambda c0: pl.BlockSpec((Sk, 128), lambda b, h, i: (b, c0 + kv_stride * h))
    in_specs, args = [qspec(q_c0)], [q]
    if rope:
        in_specs.append(qspec(0)); args.append(qr)
    in_specs.append(kspec(k_c0)); args.append(k)
    if rope:
        in_specs.append(pl.BlockSpec((Sk, 128), lambda b, h, i: (b, 0))); args.append(kr)
    in_specs.append(kspec(v_c0)); args.append(v)
    aliases = {}
    if into is not None:
        aliases = {len(args): 0}
        in_specs.append(pl.BlockSpec(memory_space=pl.ANY)); args.append(into)
        o_shape = jax.ShapeDtypeStruct(into.shape, into.dtype)
    else:
        o_shape = jax.ShapeDtypeStruct((B * Sq, o_width), F32)
    return _call(body, name, [o_shape, jax.ShapeDtypeStruct((B * H, Sq, 1), F32)], grid=(B, H, nq), in_specs=in_specs,
                 out_specs=[qspec(o_c0), pl.BlockSpec((None, tq, 1), lambda b, h, i: (b * H + h, i, 0))],
                 scratch=[pltpu.VMEM((tq, 1), F32), pltpu.VMEM((tq, 1), F32), pltpu.VMEM((tq, 128), F32)],
                 dims=("parallel", "parallel", "arbitrary"), aliases=aliases)(*args)


def _attn_bwd(q, q_c0, qr, k, k_c0, kr, v, v_c0, o, do, o_c0, lse, B, Sq, Sk, H, causal, scale, name, dq_into=None,
              kv_stride=1):
    tq, tk, sub = _attn_shapes(Sq, Sk, causal, "bwd")
    nq, nk, nsub = Sq // tq, Sk // tk, tq // sub
    rope = qr is not None
    dk_w = 256 if rope else 128

    def body(*refs):
        refs = list(refs)
        qn_ref = refs.pop(0)
        qr_ref = refs.pop(0) if rope else None
        kn_ref = refs.pop(0)
        kr_ref = refs.pop(0) if rope else None
        v_ref, o_ref, do_ref, lse_ref = refs[:4]
        refs = refs[4 + (0 if dq_into is None else 1):]
        dqn_ref = refs.pop(0)
        dqr_ref = refs.pop(0) if rope else None
        dkn_ref = refs.pop(0)
        dkr_ref = refs.pop(0) if rope else None
        dv_ref = None if rope else refs.pop(0)
        q_s, do_s, dl_s, dq_acc, dk_acc, dv_acc = refs
        kj = pl.program_id(2)

        @pl.when(kj == 0)
        def _():
            qn = qn_ref[...].astype(BF16)
            q_s[...] = jnp.concatenate([qn, qr_ref[...]], axis=1) if rope else qn
            dof = do_ref[...]
            do_s[...] = dof.astype(BF16)
            dl_s[...] = jnp.sum(dof * o_ref[...], axis=-1, keepdims=True)
            dq_acc[...] = jnp.zeros_like(dq_acc)

        kk = jnp.concatenate([kn_ref[...], kr_ref[...]], axis=1) if rope else kn_ref[...]
        vv = v_ref[...]
        bias = _causal_bias(sub) if causal else None
        dk_acc[...] = jnp.zeros_like(dk_acc)
        dv_acc[...] = jnp.zeros_like(dv_acc)

        def step(i, masked):
            for r in range(nsub):
                rows = pl.ds(pl.multiple_of(i * tq + r * sub, sub), sub)
                qq, dob = q_s[rows, :], do_s[rows, :]
                nc = (r + 1) * sub if masked else tk
                kc, vc = kk[:nc], vv[:nc]
                s = lax.dot_general(qq, kc, _DOT_DIMS["nt"], preferred_element_type=F32) * scale
                if masked:
                    s = _mask_diagonal(s, bias)
                p = jnp.exp(s - lse_ref[rows, :])
                dp = lax.dot_general(dob, vc, _DOT_DIMS["nt"], preferred_element_type=F32)
                ds = (p * (dp - dl_s[rows, :]) * scale).astype(BF16)
                dv_acc[0:nc, :] += lax.dot_general(p.astype(BF16), dob, _DOT_DIMS["tn"], preferred_element_type=F32)
                dk_acc[0:nc, :] += lax.dot_general(ds, qq, _DOT_DIMS["tn"], preferred_element_type=F32)
                dq_acc[rows, :] += jnp.dot(ds, kc, preferred_element_type=F32)

        def unmasked(i, carry):
            step(i, False)
            return carry

        if causal:
            step(kj, True)
            lax.fori_loop(kj + 1, nq, unmasked, 0)
        else:
            lax.fori_loop(0, nq, unmasked, 0)
        if rope:
            dkn_ref[...] = jnp.concatenate([dk_acc[:, 0:128], dv_acc[...]], axis=1).astype(dkn_ref.dtype)
            dkr_ref[...] = dk_acc[:, 128:256]
        else:
            dkn_ref[...] = dk_acc[...]
            dv_ref[...] = dv_acc[...]

        @pl.when(kj == nk - 1)
        def _():
            dqn_ref[...] = dq_acc[:, 0:128].astype(dqn_ref.dtype)
            if rope:
                dqr_ref[...] = dq_acc[:, 128:256]

    qspec = lambda c0: pl.BlockSpec((Sq, 128), lambda b, h, j: (b, c0 + h))
    kspec = lambda c0: pl.BlockSpec((tk, 128), lambda b, h, j: (b * nk + j, c0 + kv_stride * h))
    in_specs, args = [qspec(q_c0)], [q]
    if rope:
        in_specs.append(qspec(0)); args.append(qr)
    in_specs.append(kspec(k_c0)); args.append(k)
    if rope:
        in_specs.append(pl.BlockSpec((tk, 128), lambda b, h, j: (b * nk + j, 0))); args.append(kr)
    in_specs += [kspec(v_c0), qspec(o_c0), qspec(o_c0), pl.BlockSpec((None, Sq, 1), lambda b, h, j: (b * H + h, 0, 0))]
    args += [v, o, do, lse]
    h_rows_q = jax.ShapeDtypeStruct((B * Sq, H * 128), F32)
    h_rows_k = jax.ShapeDtypeStruct((B * Sk, H * 128), F32)
    out_shape, out_specs, aliases = [h_rows_q], [qspec(0)], None
    if rope:
        out_shape = [jax.ShapeDtypeStruct((B * Sq, 2 * H * 128), BF16)]
    if dq_into is not None:
        aliases = {len(args): 0}
        in_specs.append(pl.BlockSpec(memory_space=pl.ANY)); args.append(dq_into[0])
        out_shape, out_specs = [jax.ShapeDtypeStruct(dq_into[0].shape, dq_into[0].dtype)], [qspec(dq_into[1])]
    if rope:
        out_shape.append(h_rows_q); out_specs.append(qspec(0))
    hspec = lambda w: pl.BlockSpec((tk, w), lambda b, h, j: (b * nk + j, h))
    if rope:
        out_shape += [jax.ShapeDtypeStruct((B * Sk, H * 256), BF16), h_rows_k]
        out_specs += [hspec(256), hspec(128)]
    else:
        out_shape += [h_rows_k, h_rows_k]
        out_specs += [hspec(128), hspec(128)]
    return _call(body, name, out_shape, grid=(B, H, nk), in_specs=in_specs, out_specs=out_specs,
                 scratch=[pltpu.VMEM((Sq, dk_w), BF16), pltpu.VMEM((Sq, 128), BF16), pltpu.VMEM((Sq, 1), F32),
                          pltpu.VMEM((Sq, dk_w), F32), pltpu.VMEM((tk, dk_w), F32), pltpu.VMEM((tk, 128), F32)],
                 dims=("parallel", "parallel", "arbitrary"), aliases=aliases)(*args)


def _mem_attention_fwd(proj, q_col, ycat, mem2, mem_g, w_mem, B, S, tag):
    M = mem2.shape[0] // B
    (memn,) = _rowwise(_f_rms, [mem2], [mem_g], [(mem2.shape[1], BF16)], tag + "_memnorm")
    kvm = _mm(memn, w_mem, "nn", BF16, tag + "_memkv")
    o_c0 = ycat.shape[1] // 128 - MEM_HEADS
    ycat, lse = _attn_fwd(proj, q_col // 128, None, kvm, 0, None, kvm, MEM_HEADS, B, S, M, MEM_HEADS, False,
                          MEM_HEAD_DIM ** -0.5, tag + "_memattn", into=ycat, o_c0=o_c0)
    return ycat, (memn, kvm, lse)


def _mem_attention_bwd(proj, q_col, ycat, d_ycat, d_proj, saved, mem2, mem_g, w_mem, B, S, tag):
    memn, kvm, lse = saved
    M = mem2.shape[0] // B
    o_c0 = ycat.shape[1] // 128 - MEM_HEADS
    d_q, d_k, d_v = _attn_bwd(proj, q_col // 128, None, kvm, 0, None, kvm, MEM_HEADS, ycat, d_ycat, o_c0, lse, B, S, M,
                              MEM_HEADS, False, MEM_HEAD_DIM ** -0.5, tag + "_memattn_bwd", dq_into=(d_proj, q_col // 128))
    d_kvm = jnp.concatenate([d_k, d_v], axis=1).astype(BF16)
    d_w_mem = _mm(memn, d_kvm, "tn", F32, tag + "_memkv_dw")
    d_memn = _mm(d_kvm, w_mem, "nt", F32, tag + "_memkv_dx")
    _, d_mem_g = _rowwise_bwd(_f_rms, [mem2], [mem_g], [d_memn], 1, tag + "_memnorm_bwd")
    return d_q, d_w_mem, d_mem_g


def _rope_tables(positions):
    inv_freq = 1.0 / (ROPE_THETA ** (jnp.arange(0, MLA_ROPE, 2, dtype=F32) / MLA_ROPE))
    ang = positions.astype(F32).reshape(-1, 1) * inv_freq
    cos, sin, zero = jnp.cos(ang), jnp.sin(ang), jnp.zeros_like(ang)
    return jnp.concatenate([cos, zero, cos, zero], axis=1), jnp.concatenate([-sin, zero, sin, zero], axis=1)


def _forward_backward(x, mem, positions, target, W):
    B, S, D = x.shape
    T = B * S
    mix_w = 2 * D
    h0 = x.reshape(T, D)
    mem2 = mem.reshape(-1, D)
    tgt = target.reshape(T, D)
    row = lambda v: v.reshape(1, -1)
    n_nope = MLA_HEADS * MLA_NOPE

    g0 = row(W["norm_g"][0])
    (u0,) = _rowwise(_f_rms, [h0], [g0], [(D, BF16)], "l0_norm", carry=W.carry("l0_norm"))
    proj0 = _mm(u0, W["conv_w_in"], "nn", F32, "l0_in", carry=W.carry("l0_in"))
    dw, dwb = W["conv_dw"], row(W["conv_dw_b"][0])
    conv_w = dw.shape[1]
    qm0_col, z0_col = 2 * conv_w, 2 * conv_w + MEM_WIDTH
    cv = _dwconv_fwd(proj0, conv_w, dw, dwb, B, S, "l0_dwconv", carry=W.carry("l0_dwconv")).reshape(T, conv_w)
    ln_g, ln_b = row(W["conv_ln_g"][0]), row(W["conv_ln_b"][0])
    (ycat0,) = _rowwise(_f_ln_silu, [cv], [ln_g, ln_b], [(conv_w, F32, mix_w)], "l0_ln", carry=W.carry("l0_ln"))
    mg0 = row(W["mem_norm_g"][0])
    ycat0, mem_saved0 = _mem_attention_fwd(proj0, qm0_col, ycat0, mem2, mg0, W["w_mem_kv"][0], B, S, "l0")
    h1, y0_t = _gate_out(ycat0, proj0, z0_col, W["w_out"][0], h0, "l0_out", carry=W.carry("l0_out"))

    g1 = row(W["norm_g"][1])
    (u1,) = _rowwise(_f_rms, [h1], [g1], [(D, BF16)], "l1_norm")
    proj1 = _mm(u1, W["mla_w_in"], "nn", F32, "l1_in")
    z1_col = Q_RANK
    qm1_col = z1_col + mix_w
    ckv_col = qm1_col + MEM_WIDTH
    kr_col = ckv_col + KV_RANK
    cq, ckv = (proj1, Q_RANK, 0), (proj1, KV_RANK, ckv_col // KV_RANK)
    qg, kvg = row(W["mla_q_norm_g"]), row(W["mla_kv_norm_g"])
    (cqn,) = _rowwise(_f_rms, [cq], [qg], [(Q_RANK, BF16)], "l1_qnorm")
    (ckvn,) = _rowwise(_f_rms, [ckv], [kvg], [(KV_RANK, BF16)], "l1_kvnorm")
    qf = _mm(cqn, W["mla_w_uq"], "nn", F32, "l1_uq")
    kvf = _mm(ckvn, W["mla_w_ukv"], "nn", BF16, "l1_ukv")
    cos_p, sin_p = _rope_tables(positions)
    qr, kr = _rowwise(_f_rope, [(qf, n_nope, 1), (proj1, 128, kr_col // 128), cos_p, sin_p], [],
                      [(n_nope, BF16), (128, BF16)], "l1_rope")
    scale1 = MLA_QK ** -0.5
    ycat1, lse1 = _attn_fwd(qf, 0, qr, kvf, 0, kr, kvf, 1, B, S, S, MLA_HEADS, True, scale1, "l1_attn",
                            o_width=mix_w, kv_stride=2)
    mg1 = row(W["mem_norm_g"][1])
    ycat1, mem_saved1 = _mem_attention_fwd(proj1, qm1_col, ycat1, mem2, mg1, W["w_mem_kv"][1], B, S, "l1")
    h2, y1_t = _gate_out(ycat1, proj1, z1_col, W["w_out"][1], h1, "l1_out")

    gf = row(W["final_norm_g"])
    dh2, d_gf, loss128 = _final_loss(h2, tgt, gf, "final_loss")
    G = {"final_norm_g": d_gf.reshape(-1)}
    L1 = {}

    d_wout1 = _mm(y1_t, dh2, "nn", F32, "l1_out_dw")
    d_ycat1, d_proj1 = _out_dx_gate_bwd(dh2, W["w_out"][1], ycat1, proj1, z1_col, "l1_out_dx")
    d_proj1, d_wmem1, d_mg1 = _mem_attention_bwd(proj1, qm1_col, ycat1, d_ycat1, d_proj1, mem_saved1, mem2, mg1,
                                                 W["w_mem_kv"][1], B, S, "l1")
    d_qf, d_qr, d_kvf, d_kr_heads = _attn_bwd(qf, 0, qr, kvf, 0, kr, kvf, 1, ycat1, d_ycat1, 0, lse1, B, S, S,
                                              MLA_HEADS, True, scale1, "l1_attn_bwd", kv_stride=2)
    d_qf, d_proj1 = _rowwise(_f_rope_t, [d_qr, d_kr_heads, cos_p, sin_p], [], [(n_nope, F32), (128, F32)], "l1_rope_bwd",
                             into=[(0, d_qf, 1), (1, d_proj1, kr_col // 128)])
    d_cqn = _mm(d_qf, W["mla_w_uq"], "nt", F32, "l1_uq_dx")
    L1[("mla_w_uq", None)] = _mm(cqn, d_qf, "tn", F32, "l1_uq_dw")
    d_ckvn = _mm(d_kvf, W["mla_w_ukv"], "nt", F32, "l1_ukv_dx")
    L1[("mla_w_ukv", None)] = _mm(ckvn, d_kvf, "tn", F32, "l1_ukv_dw")
    d_proj1, d_qg = _rowwise_bwd(_f_rms, [cq], [qg], [d_cqn], 1, "l1_qnorm_bwd", into=(d_proj1, cq[2]))
    d_proj1, d_kvg = _rowwise_bwd(_f_rms, [ckv], [kvg], [d_ckvn], 1, "l1_kvnorm_bwd", into=(d_proj1, ckv[2]))
    L1[("w_mem_kv", 1)] = d_wmem1
    L1[("mla_w_in", None)] = _mm(u1, d_proj1, "tn", F32, "l1_in_dw")
    L1[("w_out", 1)] = d_wout1
    W.ready("l1", L1)
    dh1, d_g1 = _in_dx_norm_bwd(d_proj1, W["mla_w_in"], h1, g1, dh2, "l1_in_dx", carry=W.carry("l1_in_dx"))

    d_wout0 = _mm(y0_t, dh1, "nn", F32, "l0_out_dw")
    d_ycat0, d_proj0 = _out_dx_gate_bwd(dh1, W["w_out"][0], ycat0, proj0, z0_col, "l0_out_dx", carry=W.carry("l0_out_dx"))
    d_proj0, d_wmem0, d_mg0 = _mem_attention_bwd(proj0, qm0_col, ycat0, d_ycat0, d_proj0, mem_saved0, mem2, mg0,
                                                 W["w_mem_kv"][0], B, S, "l0")
    W.ready("l0a", {("w_mem_kv", 0): d_wmem0, ("w_out", 0): d_wout0})
    d_cv, d_ln_g, d_ln_b = _rowwise_bwd(_f_ln_silu, [cv], [ln_g, ln_b], [(d_ycat0, conv_w, 0)], 1, "l0_ln_bwd",
                                        carry=W.carry("l0_ln_bwd"))
    d_glu, d_dw, d_dwb = _dwconv_bwd(proj0, dw, d_cv.reshape(B, S, conv_w), "l0_dwconv_bwd",
                                     carry=W.carry("l0_dwconv_bwd"))
    d_proj0 = _glu_bwd(proj0, d_glu.reshape(T, conv_w), d_proj0, "l0_glu_bwd")
    d_conv_w_in = _mm(u0, d_proj0, "tn", F32, "l0_in_dw", carry=W.carry("l0_in_dw"))
    W.ready("l0b", {("conv_w_in", None): d_conv_w_in, ("conv_dw", None): d_dw,
                    ("mla_q_norm_g", None): d_qg.reshape(-1), ("mla_kv_norm_g", None): d_kvg.reshape(-1)})
    dx, d_g0 = _in_dx_norm_bwd(d_proj0, W["conv_w_in"], h0, g0, dh1, "l0_in_dx", carry=W.carry("l0_in_dx"))
    dx = dx.reshape(B, S, D)

    G["norm_g"] = jnp.concatenate([d_g0, d_g1], axis=0)
    G["mem_norm_g"] = jnp.concatenate([d_mg0, d_mg1], axis=0)
    G["conv_dw_b"] = d_dwb
    G["conv_ln_g"], G["conv_ln_b"] = d_ln_g, d_ln_b
    return loss128[0, 0], dx, G


def _mla_in_perm(w):
    c1, c2 = Q_RANK, Q_RANK + KV_RANK
    c3 = c2 + MLA_ROPE
    c4 = c3 + MEM_WIDTH
    zero = jnp.zeros((w.shape[0], HALF_ROPE), w.dtype)
    return jnp.concatenate([w[:, :c1], w[:, c4:], w[:, c3:c4], w[:, c1:c2], w[:, c2:c2 + HALF_ROPE], zero,
                            w[:, c2 + HALF_ROPE:c3], zero], axis=1)


def _mla_in_unperm(g):
    z_w = g.shape[1] - (Q_RANK + MEM_WIDTH + KV_RANK + 128)
    z0, q0 = Q_RANK, Q_RANK + z_w
    k0 = q0 + MEM_WIDTH
    r = k0 + KV_RANK
    return jnp.concatenate([g[:, :Q_RANK], g[:, k0:r], g[:, r:r + HALF_ROPE], g[:, r + 64:r + 64 + HALF_ROPE],
                            g[:, q0:k0], g[:, z0:q0]], axis=1)


def _uq_perm(w):
    n = w.shape[0]
    w3 = w.reshape(n, MLA_HEADS, MLA_QK)
    zero = jnp.zeros((n, MLA_HEADS, HALF_ROPE), w.dtype)
    rope = jnp.concatenate([w3[:, :, MLA_NOPE:MLA_NOPE + HALF_ROPE], zero, w3[:, :, MLA_NOPE + HALF_ROPE:], zero], axis=2)
    return jnp.concatenate([w3[:, :, :MLA_NOPE].reshape(n, -1), rope.reshape(n, -1)], axis=1)


def _uq_unperm(g):
    n = g.shape[0]
    n_nope = MLA_HEADS * MLA_NOPE
    rope = g[:, n_nope:].reshape(n, MLA_HEADS, 128)
    return jnp.concatenate([g[:, :n_nope].reshape(n, MLA_HEADS, MLA_NOPE), rope[:, :, :HALF_ROPE],
                            rope[:, :, 64:64 + HALF_ROPE]], axis=2).reshape(n, -1)


_ROW_CUT = ("w_mem_kv", "w_out")
_COL_CUT = ("conv_w_in", "mla_w_in", "mla_w_uq", "mla_w_ukv", "conv_dw")
_BIG = ("w_mem_kv", "w_out", "conv_w_in", "mla_w_in", "mla_w_uq", "mla_w_ukv")
_SMALL_SHARDED = ("conv_dw", "mla_q_norm_g", "mla_kv_norm_g")
_REPLICATED = ("norm_g", "mem_norm_g", "conv_dw_b", "conv_ln_g", "conv_ln_b", "final_norm_g")
_PERM = {"mla_w_in": (_mla_in_perm, _mla_in_unperm), "mla_w_uq": (_uq_perm, _uq_unperm)}


def _join(n, blocks):
    if n in _ROW_CUT:
        _, L, r, c = blocks.shape
        return blocks.transpose(1, 0, 2, 3).reshape(L, N_DEV * r, c)
    if n in _COL_CUT:
        _, _, r, c = blocks.shape
        return blocks.reshape(N_DEV, r, c).transpose(1, 0, 2).reshape(r, N_DEV * c)
    return blocks.reshape(-1)


def _cut(n, full, shard_shape):
    if n in _ROW_CUT:
        L, r, c = shard_shape
        return full.reshape(L, N_DEV, r, c).transpose(1, 0, 2, 3)
    if n in _COL_CUT:
        _, r, c = shard_shape
        return full.reshape(r, N_DEV, c).transpose(1, 0, 2).reshape(N_DEV, 1, r, c)
    return full.reshape(N_DEV, 1, -1)


def _flat_pad(parts, size):
    flat = jnp.concatenate([p.reshape(-1) for p in parts])
    return jnp.concatenate([flat, jnp.zeros((size - flat.shape[0],), flat.dtype)])


SMALL_LANES = 128 * 8


def _as_tiles(flat_parts):
    total = sum(p.size for p in flat_parts)
    size = -(-total // SMALL_LANES) * SMALL_LANES
    return _flat_pad(flat_parts, size).reshape(8, size // 8)


def _split_flat(flat, like):
    out, o = [], 0
    for a in like:
        out.append(flat[o:o + a.size].reshape(a.shape))
        o += a.size
    return out


_HBM = pl.BlockSpec(memory_space=pltpu.HBM)
_VMEM = pl.BlockSpec(memory_space=pltpu.VMEM)


def _position():
    return lax.axis_index("x"), lax.axis_index("y"), lax.axis_index("c")


def _dma_sems(n):
    return [pltpu.SemaphoreType.DMA((n,)), pltpu.SemaphoreType.DMA((n,))]


def _run_stage(stage, name):
    n_in, n_out = len(stage.ins), len(stage.out_shapes)

    def body(*refs):
        ins, outs, sems = refs[:n_in], refs[n_in:n_in + n_out], refs[n_in + n_out:]
        stage.start(ins, outs, sems)
        stage.wait(ins, outs, sems)

    outs = _call(body, name, stage.out_shapes, in_specs=[_HBM] * n_in, out_specs=[_HBM] * n_out, scratch=stage.sems,
                 aliases=stage.aliases)(*stage.ins)
    stage.outs = list(outs)
    return stage.outs


def _gather_chips_stage(shards):
    n = len(shards)

    def copies(x_refs, out_refs, sems):
        send_sems, recv_sems, _ = sems
        x, y, c = _position()
        peers = [(x, y, 1 - c), (1 - x, y, c), (x, 1 - y, c), (1 - x, 1 - y, c)]
        out = []
        for a in range(n):
            for k, (px, py, pc) in enumerate(peers):
                send = pltpu.make_async_remote_copy(src_ref=x_refs[a], dst_ref=out_refs[a].at[4 * x + 2 * y + c],
                                                    send_sem=send_sems.at[4 * a + k], recv_sem=recv_sems.at[4 * a + k],
                                                    device_id=(px, py, pc), device_id_type=MESH)
                recv = pltpu.make_async_remote_copy(src_ref=x_refs[a], dst_ref=out_refs[a].at[4 * px + 2 * py + pc],
                                                    send_sem=send_sems.at[4 * a + k], recv_sem=recv_sems.at[4 * a + k],
                                                    device_id=(px, py, pc), device_id_type=MESH)
                out.append((send, recv))
        return out

    def local(x_refs, out_refs, sems):
        x, y, c = _position()
        return [pltpu.make_async_copy(x_refs[a], out_refs[a].at[4 * x + 2 * y + c], sems[2].at[a]) for a in range(n)]

    def start(x_refs, out_refs, sems):
        for cp in local(x_refs, out_refs, sems):
            cp.start()
        for send, _ in copies(x_refs, out_refs, sems):
            send.start()

    def wait(x_refs, out_refs, sems):
        for send, recv in copies(x_refs, out_refs, sems):
            recv.wait_recv()
            send.wait_send()
        for cp in local(x_refs, out_refs, sems):
            cp.wait()

    return _Stage(shards, [jax.ShapeDtypeStruct((N_DEV,) + a.shape, a.dtype) for a in shards],
                  _dma_sems(4 * n) + [pltpu.SemaphoreType.DMA((n,))], start, wait)


def _gather_sibling_stage(bufs):
    n = len(bufs)

    def copies(out_refs, sems):
        send_sems, recv_sems = sems
        x, y, c = _position()
        out = []
        for a in range(n):
            for j, (px, py) in enumerate([(1 - x, y), (x, 1 - y), (1 - x, 1 - y)]):
                mine, theirs = out_refs[a].at[4 * px + 2 * py + c], out_refs[a].at[4 * px + 2 * py + (1 - c)]
                send = pltpu.make_async_remote_copy(src_ref=mine, dst_ref=mine, send_sem=send_sems.at[3 * a + j],
                                                    recv_sem=recv_sems.at[3 * a + j], device_id=(x, y, 1 - c),
                                                    device_id_type=MESH)
                recv = pltpu.make_async_remote_copy(src_ref=mine, dst_ref=theirs, send_sem=send_sems.at[3 * a + j],
                                                    recv_sem=recv_sems.at[3 * a + j], device_id=(x, y, 1 - c),
                                                    device_id_type=MESH)
                out.append((send, recv))
        return out

    def start(_, out_refs, sems):
        for send, _r in copies(out_refs, sems):
            send.start()

    def wait(_, out_refs, sems):
        for send, recv in copies(out_refs, sems):
            recv.wait_recv()
            send.wait_send()

    return _Stage(bufs, [jax.ShapeDtypeStruct(b.shape, b.dtype) for b in bufs], _dma_sems(3 * n), start, wait,
                  aliases={a: a for a in range(n)})


def _all_gather_small(v, name):
    r, n = v.shape

    def body(x_ref, out_ref, send_sems, recv_sems, local_sem):
        x, y, c = _position()
        me = 4 * x + 2 * y + c
        mine = pltpu.make_async_copy(x_ref, out_ref.at[me], local_sem)
        mine.start()
        flips = [(fx, fy, fc) for fx in (0, 1) for fy in (0, 1) for fc in (0, 1)][1:]
        copies = []
        for k, (fx, fy, fc) in enumerate(flips):
            peer = (x ^ fx, y ^ fy, c ^ fc)
            cp = pltpu.make_async_remote_copy(src_ref=x_ref, dst_ref=out_ref.at[me], send_sem=send_sems.at[k],
                                              recv_sem=recv_sems.at[k], device_id=peer, device_id_type=MESH)
            cp.start()
            copies.append(cp)
        for k, (fx, fy, fc) in enumerate(flips):
            px, py, pc = x ^ fx, y ^ fy, c ^ fc
            src = out_ref.at[4 * px + 2 * py + pc]
            pltpu.make_async_remote_copy(src_ref=x_ref, dst_ref=src, send_sem=send_sems.at[k], recv_sem=recv_sems.at[k],
                                         device_id=(px, py, pc), device_id_type=MESH).wait_recv()
        for cp in copies:
            cp.wait_send()
        mine.wait()

    return _call(body, name, jax.ShapeDtypeStruct((N_DEV, r, n), v.dtype), in_specs=[_VMEM], out_specs=_VMEM,
                 scratch=_dma_sems(7) + [pltpu.SemaphoreType.DMA(())])(v)


def _reduce_sibling_stage(gs):
    n = len(gs)

    def copies(g_refs, out_refs, sems):
        send_sems, recv_sems = sems
        x, y, c = _position()
        return [pltpu.make_async_remote_copy(src_ref=g_refs[a].at[2 * k + (1 - c)], dst_ref=out_refs[a].at[k],
                                             send_sem=send_sems.at[4 * a + k], recv_sem=recv_sems.at[4 * a + k],
                                             device_id=(x, y, 1 - c), device_id_type=MESH)
                for a in range(n) for k in range(4)]

    def start(g_refs, out_refs, sems):
        for cp in copies(g_refs, out_refs, sems):
            cp.start()

    def wait(g_refs, out_refs, sems):
        for cp in copies(g_refs, out_refs, sems):
            cp.wait()

    return _Stage(gs, [jax.ShapeDtypeStruct((4,) + g.shape[1:], g.dtype) for g in gs], _dma_sems(4 * n), start, wait)


def _rows2d(shape):
    cols = shape[-1]
    rows = 1
    for s in shape[:-1]:
        rows *= s
    return rows, cols


def _add_own(g, recv, name):
    rows, cols = _rows2d(g.shape[1:])
    tr = _pick(rows, 256, 8)
    c = lax.axis_index("c").astype(jnp.int32).reshape(1)

    def body(c_ref, g_ref, r_ref, o_ref):
        o_ref[...] = (g_ref[...].astype(F32) + r_ref[...].astype(F32)).astype(o_ref.dtype)

    grid_spec = pltpu.PrefetchScalarGridSpec(
        num_scalar_prefetch=1, grid=(4, rows // tr),
        in_specs=[pl.BlockSpec((None, None, tr, cols), lambda k, i, c_ref: (k, c_ref[0], i, 0)),
                  pl.BlockSpec((None, tr, cols), lambda k, i, c_ref: (k, i, 0))],
        out_specs=pl.BlockSpec((None, tr, cols), lambda k, i, c_ref: (k, i, 0)))
    return _call(body, name, jax.ShapeDtypeStruct((4, rows, cols), g.dtype), grid_spec=grid_spec,
                 dims=("parallel", "parallel"))(c, g.reshape(4, 2, rows, cols), recv.reshape(4, rows, cols))


def _reduce_chips_stage(pas):
    n = len(pas)

    def copies(pa_refs, out_refs, sems):
        send_sems, recv_sems, _ = sems
        x, y, c = _position()
        my_chip = 2 * x + y
        out = []
        for a in range(n):
            for j, (px, py) in enumerate([(1 - x, y), (x, 1 - y), (1 - x, 1 - y)]):
                send = pltpu.make_async_remote_copy(src_ref=pa_refs[a].at[2 * px + py], dst_ref=out_refs[a].at[my_chip],
                                                    send_sem=send_sems.at[3 * a + j], recv_sem=recv_sems.at[3 * a + j],
                                                    device_id=(px, py, c), device_id_type=MESH)
                recv = pltpu.make_async_remote_copy(src_ref=pa_refs[a].at[2 * px + py], dst_ref=out_refs[a].at[2 * px + py],
                                                    send_sem=send_sems.at[3 * a + j], recv_sem=recv_sems.at[3 * a + j],
                                                    device_id=(px, py, c), device_id_type=MESH)
                out.append((send, recv))
        return out

    def local(pa_refs, out_refs, sems):
        x, y, _ = _position()
        return [pltpu.make_async_copy(pa_refs[a].at[2 * x + y], out_refs[a].at[2 * x + y], sems[2].at[a]) for a in range(n)]

    def start(pa_refs, out_refs, sems):
        for cp in local(pa_refs, out_refs, sems):
            cp.start()
        for send, _r in copies(pa_refs, out_refs, sems):
            send.start()

    def wait(pa_refs, out_refs, sems):
        for send, recv in copies(pa_refs, out_refs, sems):
            recv.wait_recv()
            send.wait_send()
        for cp in local(pa_refs, out_refs, sems):
            cp.wait()

    return _Stage(pas, [jax.ShapeDtypeStruct(pa.shape, pa.dtype) for pa in pas],
                  _dma_sems(3 * n) + [pltpu.SemaphoreType.DMA((n,))], start, wait)


def _adamw_math(w, g, m, v):
    m = ADAM_B1 * m + (1.0 - ADAM_B1) * g
    v = ADAM_B2 * v + (1.0 - ADAM_B2) * (g * g)
    m_hat = m / (1.0 - ADAM_B1 ** ADAM_STEP)
    v_hat = v / (1.0 - ADAM_B2 ** ADAM_STEP)
    delta = -ADAM_LR * (m_hat / (jnp.sqrt(v_hat) + ADAM_EPS) + ADAM_WD * w)
    return delta, m, v


def _sum_adamw(parts, w, m, v, name):
    n, rows, cols = parts.shape
    tr = _pick(rows, 128, 8)

    def body(p_ref, w_ref, m_ref, v_ref, g_ref, d_ref, nm_ref, nv_ref):
        g = p_ref[0].astype(F32)
        for k in range(1, n):
            g = g + p_ref[k].astype(F32)
        d, nm, nv = _adamw_math(w_ref[...], g, m_ref[...], v_ref[...])
        g_ref[...], d_ref[...], nm_ref[...], nv_ref[...] = g, d, nm, nv

    blk = pl.BlockSpec((tr, cols), lambda i: (i, 0))
    return _call(body, name, [jax.ShapeDtypeStruct((rows, cols), F32)] * 4, grid=(rows // tr,),
                 in_specs=[pl.BlockSpec((n, tr, cols), lambda i: (0, i, 0)), blk, blk, blk],
                 out_specs=[blk] * 4, dims=("parallel",))(parts, w, m, v)


_WEIGHTS = ("norm_g", "mem_norm_g", "w_mem_kv", "w_out", "conv_w_in", "conv_dw", "conv_dw_b", "conv_ln_g", "conv_ln_b",
            "mla_w_in", "mla_q_norm_g", "mla_w_uq", "mla_kv_norm_g", "mla_w_ukv", "final_norm_g")


_GATHER_GROUPS = {"a": ("conv_w_in",), "b": ("w_mem_kv", "w_out"), "c": ("mla_w_in", "mla_w_uq", "mla_w_ukv")}
_CARRIERS = {"l0_norm": ("gather chips", ("a",)), "l0_in": ("gather chips", ("b",)), "l0_dwconv": ("gather chips", ("c",)),
             "l0_ln": ("gather sibling", ("b",)), "l0_out": ("gather sibling", ("c",)),
             "l1_in_dx": ("reduce sibling", ("l1",)), "l0_ln_bwd": ("reduce sibling", ("l0a",)),
             "l0_out_dx": ("reduce chips", ("l1", 0, 2)), "l0_dwconv_bwd": ("reduce chips", ("l1", 2, 5)),
             "l0_in_dw": ("reduce chips", ("l0a",)), "l0_in_dx": ("reduce sibling alone, then chips", ("l0b",))}


class _Schedule:
    def __init__(self, w):
        self.w, self.full, self.gather, self.reduce = w, {}, {}, {}
        self.small = _as_tiles([w[n] for n in _SMALL_SHARDED])
        for n in _REPLICATED:
            self.full[n] = w[n]

    def carry(self, call):
        kind, (g, *part) = _CARRIERS[call]
        if kind == "gather chips":
            shards = [self.w[n].astype(BF16) for n in _GATHER_GROUPS[g]] + ([self.small] if g == "a" else [])
            self.gather[g] = [_gather_chips_stage(shards)]
            return self.gather[g][0]
        if kind == "gather sibling":
            self.gather[g].append(_gather_sibling_stage(self.gather[g][0].outs))
            return self.gather[g][1]
        r = self.reduce[g]
        if kind == "reduce sibling":
            r["sibling"] = _reduce_sibling_stage(r["cut"])
            return r["sibling"]
        if kind != "reduce chips":
            r["sibling"] = _reduce_sibling_stage(r["cut"])
            _run_stage(r["sibling"], "reduce_sibling_" + g)
        if "partial" not in r:
            r["partial"] = [_add_own(c, s, "reduce_add_%s_%d" % (g, i))
                            for i, (c, s) in enumerate(zip(r["cut"], r["sibling"].outs))]
        lo, hi = part if part else (0, len(r["keys"]))
        stage = _reduce_chips_stage(r["partial"][lo:hi])
        r.setdefault("chips", []).append((r["keys"][lo:hi], stage))
        return stage

    def __getitem__(self, name):
        if name not in self.full:
            g = "a" if name in _SMALL_SHARDED else [k for k, names in _GATHER_GROUPS.items() if name in names][0]
            if len(self.gather[g]) == 1:
                self.gather[g].append(_gather_sibling_stage(self.gather[g][0].outs))
                _run_stage(self.gather[g][1], "gather_sibling_" + g)
            for n, buf in zip(_GATHER_GROUPS[g], self.gather[g][1].outs):
                self.full[n] = _PERM[n][0](_join(n, buf)) if n in _PERM else _join(n, buf)
            if g == "a":
                small, o = self.gather[g][1].outs[-1].reshape(N_DEV, -1), 0
                for n in _SMALL_SHARDED:
                    self.full[n] = _join(n, small[:, o:o + self.w[n].size].reshape((N_DEV,) + self.w[n].shape))
                    o += self.w[n].size
        return self.full[name]

    def ready(self, group, grads, payload=BF16):
        keys, cut, small = [], [], []
        for (n, layer), g in grads.items():
            if n in _SMALL_SHARDED:
                small.append(_cut(n, g, self.w[n].shape).reshape(N_DEV, -1))
                continue
            keys.append((n, layer))
            if layer is not None:
                cut.append(g.reshape((N_DEV,) + self.w[n].shape[1:]).astype(payload))
            else:
                cut.append(_cut(n, _PERM[n][1](g) if n in _PERM else g, self.w[n].shape).astype(payload))
        if small:
            keys.append(("small", None))
            cut.append(jax.vmap(lambda r: _as_tiles([r]))(jnp.concatenate(small, axis=1)))
        self.reduce[group] = {"keys": keys, "cut": cut}

    def finish(self):
        out = {}
        for r in self.reduce.values():
            for keys, stage in r["chips"]:
                out.update(dict(zip(keys, stage.outs)))
        return out


def kernel(x, mem, positions, norm_g, mem_norm_g, w_mem_kv, w_out, conv_w_in, conv_dw, conv_dw_b, conv_ln_g, conv_ln_b, mla_w_in, mla_q_norm_g, mla_w_uq, mla_kv_norm_g, mla_w_ukv, final_norm_g, loss_target, m_norm_g, m_mem_norm_g, m_w_mem_kv, m_w_out, m_conv_w_in, m_conv_dw, m_conv_dw_b, m_conv_ln_g, m_conv_ln_b, m_mla_w_in, m_mla_q_norm_g, m_mla_w_uq, m_mla_kv_norm_g, m_mla_w_ukv, m_final_norm_g, v_norm_g, v_mem_norm_g, v_w_mem_kv, v_w_out, v_conv_w_in, v_conv_dw, v_conv_dw_b, v_conv_ln_g, v_conv_ln_b, v_mla_w_in, v_mla_q_norm_g, v_mla_w_uq, v_mla_kv_norm_g, v_mla_w_ukv, v_final_norm_g):
    w = dict(zip(_WEIGHTS, (norm_g, mem_norm_g, w_mem_kv, w_out, conv_w_in, conv_dw, conv_dw_b, conv_ln_g, conv_ln_b,
                            mla_w_in, mla_q_norm_g, mla_w_uq, mla_kv_norm_g, mla_w_ukv, final_norm_g)))
    m = dict(zip(_WEIGHTS, (m_norm_g, m_mem_norm_g, m_w_mem_kv, m_w_out, m_conv_w_in, m_conv_dw, m_conv_dw_b, m_conv_ln_g,
                            m_conv_ln_b, m_mla_w_in, m_mla_q_norm_g, m_mla_w_uq, m_mla_kv_norm_g, m_mla_w_ukv, m_final_norm_g)))
    v = dict(zip(_WEIGHTS, (v_norm_g, v_mem_norm_g, v_w_mem_kv, v_w_out, v_conv_w_in, v_conv_dw, v_conv_dw_b, v_conv_ln_g,
                            v_conv_ln_b, v_mla_w_in, v_mla_q_norm_g, v_mla_w_uq, v_mla_kv_norm_g, v_mla_w_ukv, v_final_norm_g)))

    sched = _Schedule(w)
    loss_local, dx, G = _forward_backward(x, mem, positions, loss_target, sched)
    loss = lax.psum(loss_local, ("x", "y", "c"))

    from_chips = sched.finish()
    out = [{}, {}, {}, {}]
    for n in _BIG:
        if n in _ROW_CUT:
            res = [_sum_adamw(from_chips[(n, l)], w[n][l], m[n][l], v[n][l], "adamw_%s_%d" % (n, l)) for l in range(w[n].shape[0])]
            res = [jnp.stack(r) for r in zip(*res)]
        else:
            rows, cols = _rows2d(w[n].shape)
            res = _sum_adamw(from_chips[(n, None)], w[n].reshape(rows, cols), m[n].reshape(rows, cols),
                             v[n].reshape(rows, cols), "adamw_" + n)
        for o, r in zip(out, res):
            o[n] = r.reshape(w[n].shape)
    small_like = [w[n] for n in _SMALL_SHARDED]
    res = _sum_adamw(from_chips[("small", None)], _as_tiles(small_like), _as_tiles([m[n] for n in _SMALL_SHARDED]),
                     _as_tiles([v[n] for n in _SMALL_SHARDED]), "adamw_small")
    for o, r in zip(out, res):
        for n, a in zip(_SMALL_SHARDED, _split_flat(r.reshape(-1), small_like)):
            o[n] = a

    rep_like = [w[n] for n in _REPLICATED]
    rep_parts = _all_gather_small(_as_tiles([G[n] for n in _REPLICATED]), "gather_replicated_grads")
    res = _sum_adamw(rep_parts, _as_tiles(rep_like), _as_tiles([m[n] for n in _REPLICATED]),
                     _as_tiles([v[n] for n in _REPLICATED]), "adamw_replicated")
    for o, r in zip(out, res):
        for n, a in zip(_REPLICATED, _split_flat(r.reshape(-1), rep_like)):
            o[n] = a

    return (loss, dx, *[out[0][n] for n in _WEIGHTS], *[out[1][n] for n in _WEIGHTS],
            *[out[2][n] for n in _WEIGHTS], *[out[3][n] for n in _WEIGHTS])
```
